```python
import math
import jax, jax.numpy as jnp
from jax import lax
import numpy as np

D_MODEL = 1024
BATCH = 8
SEQ = 4096
DEPTH = 2

N_MIXERS = 2
N_SGU_LAYERS = (DEPTH + 1) // 2
N_SWA_LAYERS = DEPTH // 2

CHUNK = 128
SGU_WIDTH = 2 * D_MODEL
SGU_GROUPS = 16
SGU_GROUP_DIM = SGU_WIDTH // SGU_GROUPS

HEAD_DIM = 64
N_HEADS = D_MODEL // HEAD_DIM
N_KV_HEADS = 4
KV_GROUP = N_HEADS // N_KV_HEADS
WINDOW = 128
BLOCK = WINDOW

REL_BUCKETS = 32
REL_MAX_DIST = 128

D_FF = 2816
CONV_WIDTH = 3

EPS = 1e-6

kernel_name = "hybrid_sgu_swa_convffn"


def rms_norm(x, gain):
    xf = x.astype(jnp.float32)
    y = xf * lax.rsqrt(jnp.mean(xf * xf, axis=-1, keepdims=True) + EPS)
    return (y * gain.astype(jnp.float32)).astype(x.dtype)


def _rel_buckets():
    q = np.arange(BLOCK)[:, None] + BLOCK
    k = np.arange(2 * BLOCK)[None, :]
    dist = q - k
    n = np.maximum(dist, 0)
    max_exact = REL_BUCKETS // 2
    large = max_exact + (np.log(np.maximum(n, 1).astype(np.float32) / max_exact)
                         / math.log(REL_MAX_DIST / max_exact)
                         * (REL_BUCKETS - max_exact)).astype(np.int32)
    large = np.minimum(large, REL_BUCKETS - 1)
    bucket = np.where(n < max_exact, n, large).astype(np.int32)
    return dist, bucket


def sgu_mixer(h, w_in, v_gain, w_s, b_s, w_out):
    B, S, _ = h.shape
    nc = S // CHUNK
    z = jax.nn.gelu(h @ w_in, approximate=False)
    u, v = jnp.split(z, 2, axis=-1)
    v = rms_norm(v, v_gain).reshape(B, nc, CHUNK, SGU_GROUPS, SGU_GROUP_DIM)
    w_causal = jnp.tril(w_s)
    s = jnp.einsum('gts,bcsgd->bctgd', w_causal, v) + b_s.T[:, :, None]
    return (u * s.reshape(B, S, SGU_WIDTH)) @ w_out


def swa_mixer(h, w_qkv, q_gain, k_gain, sinks, w_o, rel_bias):
    B, S, _ = h.shape
    nb = S // BLOCK
    qkv = h @ w_qkv
    q, k, v = jnp.split(qkv, [N_HEADS * HEAD_DIM, (N_HEADS + N_KV_HEADS) * HEAD_DIM], axis=-1)
    q = rms_norm(q.reshape(B, S, N_HEADS, HEAD_DIM), q_gain)
    k = rms_norm(k.reshape(B, S, N_KV_HEADS, HEAD_DIM), k_gain)
    v = v.reshape(B, S, N_KV_HEADS, HEAD_DIM)
    q = q.reshape(B, nb, BLOCK, N_KV_HEADS, KV_GROUP, HEAD_DIM)

    def band(t):
        t = t.reshape(B, nb, BLOCK, N_KV_HEADS, HEAD_DIM)
        prev = jnp.pad(t, ((0, 0), (1, 0), (0, 0), (0, 0), (0, 0)))[:, :-1]
        return jnp.concatenate([prev, t], axis=2)

    kb, vb = band(k), band(v)
    scores = jnp.einsum('bnqhgd,bnkhd->bnhgqk', q, kb).astype(jnp.float32) * (HEAD_DIM ** -0.5)

    dist, bucket = _rel_buckets()
    bias = rel_bias.astype(jnp.float32)[bucket]
    bias = jnp.transpose(bias, (2, 0, 1)).reshape(N_KV_HEADS, KV_GROUP, BLOCK, 2 * BLOCK)
    scores = scores + bias

    blk = np.arange(nb)[:, None, None]
    kk = np.arange(2 * BLOCK)[None, None, :]
    valid = ((dist >= 0) & (dist < WINDOW))[None] & ((blk > 0) | (kk >= BLOCK))
    scores = jnp.where(valid[None, :, None, None], scores, -jnp.inf)

    sink = sinks.astype(jnp.float32).reshape(N_KV_HEADS, KV_GROUP)[None, None, :, :, None, None]
    m = jnp.maximum(jnp.max(scores, axis=-1, keepdims=True), sink)
    p = jnp.exp(scores - m)
    denom = jnp.sum(p, axis=-1, keepdims=True) + jnp.exp(sink - m)
    p = (p / denom).astype(vb.dtype)
    o = jnp.einsum('bnhgqk,bnkhd->bnqhgd', p, vb).reshape(B, S, N_HEADS * HEAD_DIM)
    return o @ w_o


def conv_glu_ffn(h, w_up, conv_w, conv_b, w_down):
    a = h @ w_up
    C = a.shape[-1]
    c = lax.conv_general_dilated(a, conv_w[:, None, :].astype(a.dtype), window_strides=(1,),
                                 padding=[(CONV_WIDTH - 1, 0)],
                                 dimension_numbers=('NWC', 'WIO', 'NWC'),
                                 feature_group_count=C) + conv_b
    g, val = jnp.split(c, 2, axis=-1)
    return (jax.nn.silu(g) * val) @ w_down


def _fwd_setup_inputs(seed: int = 0) -> dict:
    key = jax.random.key(seed)
    ks = jax.random.split(key, 20)
    f32 = jnp.float32
    nrm = lambda k, shape, scale: jax.random.normal(k, shape, f32) * scale
    return {
        "x": nrm(ks[0], (BATCH, SEQ, D_MODEL), 1.0),
        "mix_norm": 1.0 + nrm(ks[1], (DEPTH, D_MODEL), 0.05),
        "ffn_norm": 1.0 + nrm(ks[2], (DEPTH, D_MODEL), 0.05),
        "sgu_w_in": nrm(ks[3], (N_SGU_LAYERS, D_MODEL, 2 * SGU_WIDTH), D_MODEL ** -0.5),
        "sgu_v_gain": 1.0 + nrm(ks[4], (N_SGU_LAYERS, SGU_WIDTH), 0.05),
        "sgu_w_s": nrm(ks[5], (N_SGU_LAYERS, SGU_GROUPS, CHUNK, CHUNK), CHUNK ** -0.5),
        "sgu_b_s": 1.0 + nrm(ks[6], (N_SGU_LAYERS, SGU_GROUPS, CHUNK), 0.05),
        "sgu_w_out": nrm(ks[7], (N_SGU_LAYERS, SGU_WIDTH, D_MODEL), SGU_WIDTH ** -0.5),
        "attn_w_qkv": nrm(ks[8], (N_SWA_LAYERS, D_MODEL, (N_HEADS + 2 * N_KV_HEADS) * HEAD_DIM), D_MODEL ** -0.5),
        "attn_q_gain": 1.0 + nrm(ks[9], (N_SWA_LAYERS, HEAD_DIM), 0.05),
        "attn_k_gain": 1.0 + nrm(ks[10], (N_SWA_LAYERS, HEAD_DIM), 0.05),
        "attn_sinks": nrm(ks[11], (N_SWA_LAYERS, N_HEADS), 0.5),
        "attn_w_o": nrm(ks[12], (N_SWA_LAYERS, N_HEADS * HEAD_DIM, D_MODEL), (N_HEADS * HEAD_DIM) ** -0.5),
        "rel_bias": nrm(ks[13], (REL_BUCKETS, N_HEADS), 0.5),
        "ffn_w_up": nrm(ks[14], (DEPTH, D_MODEL, 2 * D_FF), D_MODEL ** -0.5),
        "ffn_conv_w": nrm(ks[15], (DEPTH, CONV_WIDTH, 2 * D_FF), CONV_WIDTH ** -0.5),
        "ffn_conv_b": nrm(ks[16], (DEPTH, 2 * D_FF), 0.02),
        "ffn_w_down": nrm(ks[17], (DEPTH, D_FF, D_MODEL), D_FF ** -0.5),
    }


def _fwd_reference(x, mix_norm, ffn_norm, sgu_w_in, sgu_v_gain, sgu_w_s, sgu_b_s, sgu_w_out,
              attn_w_qkv, attn_q_gain, attn_k_gain, attn_sinks, attn_w_o, rel_bias,
              ffn_w_up, ffn_conv_w, ffn_conv_b, ffn_w_down):
    h = x
    for i in range(DEPTH):
        j = i // N_MIXERS
        hn = rms_norm(h, mix_norm[i])
        if i % N_MIXERS == 0:
            h = h + sgu_mixer(hn, sgu_w_in[j], sgu_v_gain[j], sgu_w_s[j], sgu_b_s[j], sgu_w_out[j])
        else:
            h = h + swa_mixer(hn, attn_w_qkv[j], attn_q_gain[j], attn_k_gain[j], attn_sinks[j],
                              attn_w_o[j], rel_bias)
        h = h + conv_glu_ffn(rms_norm(h, ffn_norm[i]), ffn_w_up[i], ffn_conv_w[i], ffn_conv_b[i], ffn_w_down[i])
    return h


import jax as _jax
import jax.numpy as _jnp

TWIN_FORMAT = 'train_step'
FWD_PARAMS = ['x', 'mix_norm', 'ffn_norm', 'sgu_w_in', 'sgu_v_gain', 'sgu_w_s', 'sgu_b_s', 'sgu_w_out', 'attn_w_qkv', 'attn_q_gain', 'attn_k_gain', 'attn_sinks', 'attn_w_o', 'rel_bias', 'ffn_w_up', 'ffn_conv_w', 'ffn_conv_b', 'ffn_w_down']
TWIN_WEIGHTS = ['mix_norm', 'ffn_norm', 'sgu_w_in', 'sgu_v_gain', 'sgu_w_s', 'sgu_b_s', 'sgu_w_out', 'attn_w_qkv', 'attn_q_gain', 'attn_k_gain', 'attn_sinks', 'attn_w_o', 'rel_bias', 'ffn_w_up', 'ffn_conv_w', 'ffn_conv_b', 'ffn_w_down']
TWIN_DIFF_INPUT = 'x'
TWIN_INPUTS = ['x', 'mix_norm', 'ffn_norm', 'sgu_w_in', 'sgu_v_gain', 'sgu_w_s', 'sgu_b_s', 'sgu_w_out', 'attn_w_qkv', 'attn_q_gain', 'attn_k_gain', 'attn_sinks', 'attn_w_o', 'rel_bias', 'ffn_w_up', 'ffn_conv_w', 'ffn_conv_b', 'ffn_w_down', 'loss_target', 'm_mix_norm', 'm_ffn_norm', 'm_sgu_w_in', 'm_sgu_v_gain', 'm_sgu_w_s', 'm_sgu_b_s', 'm_sgu_w_out', 'm_attn_w_qkv', 'm_attn_q_gain', 'm_attn_k_gain', 'm_attn_sinks', 'm_attn_w_o', 'm_rel_bias', 'm_ffn_w_up', 'm_ffn_conv_w', 'm_ffn_conv_b', 'm_ffn_w_down', 'v_mix_norm', 'v_ffn_norm', 'v_sgu_w_in', 'v_sgu_v_gain', 'v_sgu_w_s', 'v_sgu_b_s', 'v_sgu_w_out', 'v_attn_w_qkv', 'v_attn_q_gain', 'v_attn_k_gain', 'v_attn_sinks', 'v_attn_w_o', 'v_rel_bias', 'v_ffn_w_up', 'v_ffn_conv_w', 'v_ffn_conv_b', 'v_ffn_w_down']
TWIN_OUTPUTS = ['loss', 'grad_x', 'grad_mix_norm', 'grad_ffn_norm', 'grad_sgu_w_in', 'grad_sgu_v_gain', 'grad_sgu_w_s', 'grad_sgu_b_s', 'grad_sgu_w_out', 'grad_attn_w_qkv', 'grad_attn_q_gain', 'grad_attn_k_gain', 'grad_attn_sinks', 'grad_attn_w_o', 'grad_rel_bias', 'grad_ffn_w_up', 'grad_ffn_conv_w', 'grad_ffn_conv_b', 'grad_ffn_w_down', 'delta_mix_norm', 'delta_ffn_norm', 'delta_sgu_w_in', 'delta_sgu_v_gain', 'delta_sgu_w_s', 'delta_sgu_b_s', 'delta_sgu_w_out', 'delta_attn_w_qkv', 'delta_attn_q_gain', 'delta_attn_k_gain', 'delta_attn_sinks', 'delta_attn_w_o', 'delta_rel_bias', 'delta_ffn_w_up', 'delta_ffn_conv_w', 'delta_ffn_conv_b', 'delta_ffn_w_down', 'new_m_mix_norm', 'new_m_ffn_norm', 'new_m_sgu_w_in', 'new_m_sgu_v_gain', 'new_m_sgu_w_s', 'new_m_sgu_b_s', 'new_m_sgu_w_out', 'new_m_attn_w_qkv', 'new_m_attn_q_gain', 'new_m_attn_k_gain', 'new_m_attn_sinks', 'new_m_attn_w_o', 'new_m_rel_bias', 'new_m_ffn_w_up', 'new_m_ffn_conv_w', 'new_m_ffn_conv_b', 'new_m_ffn_w_down', 'new_v_mix_norm', 'new_v_ffn_norm', 'new_v_sgu_w_in', 'new_v_sgu_v_gain', 'new_v_sgu_w_s', 'new_v_sgu_b_s', 'new_v_sgu_w_out', 'new_v_attn_w_qkv', 'new_v_attn_q_gain', 'new_v_attn_k_gain', 'new_v_attn_sinks', 'new_v_attn_w_o', 'new_v_rel_bias', 'new_v_ffn_w_up', 'new_v_ffn_conv_w', 'new_v_ffn_conv_b', 'new_v_ffn_w_down']
TWIN_LEAF_KINDS = {'loss': 'loss', 'grad_x': 'grad_x', 'grad_mix_norm': 'grad_w', 'grad_ffn_norm': 'grad_w', 'grad_sgu_w_in': 'grad_w', 'grad_sgu_v_gain': 'grad_w', 'grad_sgu_w_s': 'grad_w', 'grad_sgu_b_s': 'grad_w', 'grad_sgu_w_out': 'grad_w', 'grad_attn_w_qkv': 'grad_w', 'grad_attn_q_gain': 'grad_w', 'grad_attn_k_gain': 'grad_w', 'grad_attn_sinks': 'grad_w', 'grad_attn_w_o': 'grad_w', 'grad_rel_bias': 'grad_w', 'grad_ffn_w_up': 'grad_w', 'grad_ffn_conv_w': 'grad_w', 'grad_ffn_conv_b': 'grad_w', 'grad_ffn_w_down': 'grad_w', 'delta_mix_norm': 'delta_w', 'delta_ffn_norm': 'delta_w', 'delta_sgu_w_in': 'delta_w', 'delta_sgu_v_gain': 'delta_w', 'delta_sgu_w_s': 'delta_w', 'delta_sgu_b_s': 'delta_w', 'delta_sgu_w_out': 'delta_w', 'delta_attn_w_qkv': 'delta_w', 'delta_attn_q_gain': 'delta_w', 'delta_attn_k_gain': 'delta_w', 'delta_attn_sinks': 'delta_w', 'delta_attn_w_o': 'delta_w', 'delta_rel_bias': 'delta_w', 'delta_ffn_w_up': 'delta_w', 'delta_ffn_conv_w': 'delta_w', 'delta_ffn_conv_b': 'delta_w', 'delta_ffn_w_down': 'delta_w', 'new_m_mix_norm': 'new_m', 'new_m_ffn_norm': 'new_m', 'new_m_sgu_w_in': 'new_m', 'new_m_sgu_v_gain': 'new_m', 'new_m_sgu_w_s': 'new_m', 'new_m_sgu_b_s': 'new_m', 'new_m_sgu_w_out': 'new_m', 'new_m_attn_w_qkv': 'new_m', 'new_m_attn_q_gain': 'new_m', 'new_m_attn_k_gain': 'new_m', 'new_m_attn_sinks': 'new_m', 'new_m_attn_w_o': 'new_m', 'new_m_rel_bias': 'new_m', 'new_m_ffn_w_up': 'new_m', 'new_m_ffn_conv_w': 'new_m', 'new_m_ffn_conv_b': 'new_m', 'new_m_ffn_w_down': 'new_m', 'new_v_mix_norm': 'new_v', 'new_v_ffn_norm': 'new_v', 'new_v_sgu_w_in': 'new_v', 'new_v_sgu_v_gain': 'new_v', 'new_v_sgu_w_s': 'new_v', 'new_v_sgu_b_s': 'new_v', 'new_v_sgu_w_out': 'new_v', 'new_v_attn_w_qkv': 'new_v', 'new_v_attn_q_gain': 'new_v', 'new_v_attn_k_gain': 'new_v', 'new_v_attn_sinks': 'new_v', 'new_v_attn_w_o': 'new_v', 'new_v_rel_bias': 'new_v', 'new_v_ffn_w_up': 'new_v', 'new_v_ffn_conv_w': 'new_v', 'new_v_ffn_conv_b': 'new_v', 'new_v_ffn_w_down': 'new_v'}


def _forward(args):
    return _fwd_reference(*[args[k] for k in FWD_PARAMS])


def _output_shape():
    out = _jax.eval_shape(lambda: _forward(_fwd_setup_inputs(0)))
    return out.shape, out.dtype

N_MICROBATCH = 1
ADAM_LR = 0.001
ADAM_B1 = 0.9
ADAM_B2 = 0.999
ADAM_EPS = 1e-08
ADAM_WD = 0.01
ADAM_STEP = 10
PER_EXAMPLE_BATCH_AXIS = {'x': 0, 'loss_target': 0}
SHARED_INPUTS = []
_WEIGHT_DTYPES = {'mix_norm': _jnp.float32, 'ffn_norm': _jnp.float32, 'sgu_w_in': _jnp.float32, 'sgu_v_gain': _jnp.float32, 'sgu_w_s': _jnp.float32, 'sgu_b_s': _jnp.float32, 'sgu_w_out': _jnp.float32, 'attn_w_qkv': _jnp.float32, 'attn_q_gain': _jnp.float32, 'attn_k_gain': _jnp.float32, 'attn_sinks': _jnp.float32, 'attn_w_o': _jnp.float32, 'rel_bias': _jnp.float32, 'ffn_w_up': _jnp.float32, 'ffn_conv_w': _jnp.float32, 'ffn_conv_b': _jnp.float32, 'ffn_w_down': _jnp.float32}
MOMENT_SCALE = {'mix_norm': 1.748721e+01, 'ffn_norm': 2.658096e+01, 'sgu_w_in': 3.930121e-01, 'sgu_v_gain': 3.127073e+00, 'sgu_w_s': 2.186672e+00, 'sgu_b_s': 7.452295e+00, 'sgu_w_out': 6.633380e+00, 'attn_w_qkv': 2.817945e+00, 'attn_q_gain': 6.224910e+00, 'attn_k_gain': 6.193285e+00, 'attn_sinks': 9.691114e-01, 'attn_w_o': 2.601957e+00, 'rel_bias': 6.050231e-01, 'ffn_w_up': 1.179959e+00, 'ffn_conv_w': 4.090881e+00, 'ffn_conv_b': 4.141866e+00, 'ffn_w_down': 9.397110e-01}


def _to_microbatches(a, axis):
    t = _jnp.moveaxis(a, axis, 0)
    t = t.reshape((N_MICROBATCH, t.shape[0] // N_MICROBATCH) + t.shape[1:])
    return _jnp.moveaxis(t, 1, axis + 1)


def setup_inputs(seed: int = 0) -> dict:
    inp = _fwd_setup_inputs(seed)
    key = _jax.random.fold_in(_jax.random.key(seed), 7919)
    shape, _ = _output_shape()
    out = dict(inp)
    out["loss_target"] = _jax.random.normal(_jax.random.fold_in(key, 0), shape, _jnp.float32)
    for i, name in enumerate(TWIN_WEIGHTS):
        w = inp[name].astype(_jnp.float32)
        if MOMENT_SCALE is None:
            s = _jnp.sqrt(_jnp.mean(_jnp.square(w)) + 1e-30)
        else:
            s = MOMENT_SCALE[name]
        km, kv = _jax.random.split(_jax.random.fold_in(key, i + 1))
        out[name] = w
        out["m_" + name] = s * _jax.random.normal(km, w.shape, _jnp.float32)
        out["v_" + name] = (s * s) * _jax.random.uniform(kv, w.shape, _jnp.float32, 0.5, 1.5)
    if N_MICROBATCH > 1:
        for name, axis in PER_EXAMPLE_BATCH_AXIS.items():
            out[name] = _to_microbatches(out[name], axis)
    return {'x': out['x'], 'mix_norm': out['mix_norm'], 'ffn_norm': out['ffn_norm'], 'sgu_w_in': out['sgu_w_in'], 'sgu_v_gain': out['sgu_v_gain'], 'sgu_w_s': out['sgu_w_s'], 'sgu_b_s': out['sgu_b_s'], 'sgu_w_out': out['sgu_w_out'], 'attn_w_qkv': out['attn_w_qkv'], 'attn_q_gain': out['attn_q_gain'], 'attn_k_gain': out['attn_k_gain'], 'attn_sinks': out['attn_sinks'], 'attn_w_o': out['attn_w_o'], 'rel_bias': out['rel_bias'], 'ffn_w_up': out['ffn_w_up'], 'ffn_conv_w': out['ffn_conv_w'], 'ffn_conv_b': out['ffn_conv_b'], 'ffn_w_down': out['ffn_w_down'], 'loss_target': out['loss_target'], 'm_mix_norm': out['m_mix_norm'], 'm_ffn_norm': out['m_ffn_norm'], 'm_sgu_w_in': out['m_sgu_w_in'], 'm_sgu_v_gain': out['m_sgu_v_gain'], 'm_sgu_w_s': out['m_sgu_w_s'], 'm_sgu_b_s': out['m_sgu_b_s'], 'm_sgu_w_out': out['m_sgu_w_out'], 'm_attn_w_qkv': out['m_attn_w_qkv'], 'm_attn_q_gain': out['m_attn_q_gain'], 'm_attn_k_gain': out['m_attn_k_gain'], 'm_attn_sinks': out['m_attn_sinks'], 'm_attn_w_o': out['m_attn_w_o'], 'm_rel_bias': out['m_rel_bias'], 'm_ffn_w_up': out['m_ffn_w_up'], 'm_ffn_conv_w': out['m_ffn_conv_w'], 'm_ffn_conv_b': out['m_ffn_conv_b'], 'm_ffn_w_down': out['m_ffn_w_down'], 'v_mix_norm': out['v_mix_norm'], 'v_ffn_norm': out['v_ffn_norm'], 'v_sgu_w_in': out['v_sgu_w_in'], 'v_sgu_v_gain': out['v_sgu_v_gain'], 'v_sgu_w_s': out['v_sgu_w_s'], 'v_sgu_b_s': out['v_sgu_b_s'], 'v_sgu_w_out': out['v_sgu_w_out'], 'v_attn_w_qkv': out['v_attn_w_qkv'], 'v_attn_q_gain': out['v_attn_q_gain'], 'v_attn_k_gain': out['v_attn_k_gain'], 'v_attn_sinks': out['v_attn_sinks'], 'v_attn_w_o': out['v_attn_w_o'], 'v_rel_bias': out['v_rel_bias'], 'v_ffn_w_up': out['v_ffn_w_up'], 'v_ffn_conv_w': out['v_ffn_conv_w'], 'v_ffn_conv_b': out['v_ffn_conv_b'], 'v_ffn_w_down': out['v_ffn_w_down']}


def _loss(weights, diff, rest, loss_target):
    with _jax.named_scope("forward"):
        args = {**rest, TWIN_DIFF_INPUT: diff, **{k: w.astype(_WEIGHT_DTYPES[k]) for k, w in weights.items()}}
        y = _forward(args)
    with _jax.named_scope("loss_head"):
        err = _jnp.square(y.astype(_jnp.float32) - loss_target)
        return 0.5 * _jnp.sum(_jnp.mean(err, axis=-1)) if err.ndim else 0.5 * err


def _adamw(w, g, m, v):
    m = ADAM_B1 * m + (1.0 - ADAM_B1) * g
    v = ADAM_B2 * v + (1.0 - ADAM_B2) * _jnp.square(g)
    m_hat = m / (1.0 - ADAM_B1 ** ADAM_STEP)
    v_hat = v / (1.0 - ADAM_B2 ** ADAM_STEP)
    delta = -ADAM_LR * (m_hat / (_jnp.sqrt(v_hat) + ADAM_EPS) + ADAM_WD * w)
    return delta, m, v


def reference(x, mix_norm, ffn_norm, sgu_w_in, sgu_v_gain, sgu_w_s, sgu_b_s, sgu_w_out, attn_w_qkv, attn_q_gain, attn_k_gain, attn_sinks, attn_w_o, rel_bias, ffn_w_up, ffn_conv_w, ffn_conv_b, ffn_w_down, loss_target, m_mix_norm, m_ffn_norm, m_sgu_w_in, m_sgu_v_gain, m_sgu_w_s, m_sgu_b_s, m_sgu_w_out, m_attn_w_qkv, m_attn_q_gain, m_attn_k_gain, m_attn_sinks, m_attn_w_o, m_rel_bias, m_ffn_w_up, m_ffn_conv_w, m_ffn_conv_b, m_ffn_w_down, v_mix_norm, v_ffn_norm, v_sgu_w_in, v_sgu_v_gain, v_sgu_w_s, v_sgu_b_s, v_sgu_w_out, v_attn_w_qkv, v_attn_q_gain, v_attn_k_gain, v_attn_sinks, v_attn_w_o, v_rel_bias, v_ffn_w_up, v_ffn_conv_w, v_ffn_conv_b, v_ffn_w_down):
    given = dict(x=x, mix_norm=mix_norm, ffn_norm=ffn_norm, sgu_w_in=sgu_w_in, sgu_v_gain=sgu_v_gain, sgu_w_s=sgu_w_s, sgu_b_s=sgu_b_s, sgu_w_out=sgu_w_out, attn_w_qkv=attn_w_qkv, attn_q_gain=attn_q_gain, attn_k_gain=attn_k_gain, attn_sinks=attn_sinks, attn_w_o=attn_w_o, rel_bias=rel_bias, ffn_w_up=ffn_w_up, ffn_conv_w=ffn_conv_w, ffn_conv_b=ffn_conv_b, ffn_w_down=ffn_w_down, loss_target=loss_target, m_mix_norm=m_mix_norm, m_ffn_norm=m_ffn_norm, m_sgu_w_in=m_sgu_w_in, m_sgu_v_gain=m_sgu_v_gain, m_sgu_w_s=m_sgu_w_s, m_sgu_b_s=m_sgu_b_s, m_sgu_w_out=m_sgu_w_out, m_attn_w_qkv=m_attn_w_qkv, m_attn_q_gain=m_attn_q_gain, m_attn_k_gain=m_attn_k_gain, m_attn_sinks=m_attn_sinks, m_attn_w_o=m_attn_w_o, m_rel_bias=m_rel_bias, m_ffn_w_up=m_ffn_w_up, m_ffn_conv_w=m_ffn_conv_w, m_ffn_conv_b=m_ffn_conv_b, m_ffn_w_down=m_ffn_w_down, v_mix_norm=v_mix_norm, v_ffn_norm=v_ffn_norm, v_sgu_w_in=v_sgu_w_in, v_sgu_v_gain=v_sgu_v_gain, v_sgu_w_s=v_sgu_w_s, v_sgu_b_s=v_sgu_b_s, v_sgu_w_out=v_sgu_w_out, v_attn_w_qkv=v_attn_w_qkv, v_attn_q_gain=v_attn_q_gain, v_attn_k_gain=v_attn_k_gain, v_attn_sinks=v_attn_sinks, v_attn_w_o=v_attn_w_o, v_rel_bias=v_rel_bias, v_ffn_w_up=v_ffn_w_up, v_ffn_conv_w=v_ffn_conv_w, v_ffn_conv_b=v_ffn_conv_b, v_ffn_w_down=v_ffn_w_down)
    weights = {n: given[n] for n in TWIN_WEIGHTS}
    shared = {n: given[n] for n in SHARED_INPUTS}
    per_example = {n: given[n] for n in ['x']}
    grad_fn = _jax.value_and_grad(_loss, argnums=(0, 1))

    def one_microbatch(ex, loss_target):
        ex = dict(ex)
        diff = ex.pop(TWIN_DIFF_INPUT)
        return grad_fn(weights, diff, {**shared, **ex}, loss_target)

    if N_MICROBATCH == 1:
        loss, (grad_w, grad_x) = one_microbatch(per_example, given["loss_target"])
    else:
        def body(carry, xs):
            loss_sum, grad_sum = carry
            l_k, (gw_k, gx_k) = one_microbatch(xs[0], xs[1])
            with _jax.named_scope("update"):
                return (loss_sum + l_k, _jax.tree.map(_jnp.add, grad_sum, gw_k)), gx_k

        init = (_jnp.zeros((), _jnp.float32), _jax.tree.map(_jnp.zeros_like, weights))
        (loss, grad_w), grad_x = _jax.lax.scan(body, init, (per_example, given["loss_target"]))
    with _jax.named_scope("update"):
        delta_w, new_m, new_v = {}, {}, {}
        for n in TWIN_WEIGHTS:
            delta_w[n], new_m[n], new_v[n] = _adamw(weights[n], grad_w[n], given["m_" + n], given["v_" + n])
    return (loss, grad_x, *[grad_w[n] for n in TWIN_WEIGHTS], *[delta_w[n] for n in TWIN_WEIGHTS],
            *[new_m[n] for n in TWIN_WEIGHTS], *[new_v[n] for n in TWIN_WEIGHTS])
```

```python
import functools
import math

import numpy as np
import jax
import jax.numpy as jnp
from jax import lax
from jax.experimental import pallas as pl
from jax.experimental.pallas import tpu as pltpu

F32 = jnp.float32
BF16 = jnp.bfloat16

EPS = 1e-6
CHUNK = 128
SGU_GROUPS = 16
HEAD_DIM = 64
N_HEADS = 16
N_KV_HEADS = 4
REL_BUCKETS = 32
REL_MAX_DIST = 128
LANES = 128
HALO = 16

ADAM_LR = 0.001
ADAM_B1 = 0.9
ADAM_B2 = 0.999
ADAM_EPS = 1e-08
ADAM_WD = 0.01
ADAM_STEP = 10

VMEM_LIMIT_V7X = 56 * 1024 * 1024

_SQRT_HALF = math.sqrt(0.5)
_INV_SQRT_2PI = 1.0 / math.sqrt(2.0 * math.pi)


def _cp(sem):
    return pltpu.CompilerParams(dimension_semantics=sem, vmem_limit_bytes=VMEM_LIMIT_V7X)


def _resident(shape):
    nd = len(shape)
    return pl.BlockSpec(shape, lambda *_: (0,) * nd, pipeline_mode=pl.Buffered(1))


def _dot(a, b):
    return jnp.dot(a, b, preferred_element_type=F32)


def _dot_nt(a, b):
    return lax.dot_general(a, b, (((1,), (1,)), ((), ())), preferred_element_type=F32)


def _dot_tn(a, b):
    return lax.dot_general(a, b, (((0,), (0,)), ((), ())), preferred_element_type=F32)


def _gelu(x):
    return 0.5 * x * (1.0 + lax.erf(x * _SQRT_HALF))


def _gelu_and_grad(x):
    cdf = 0.5 * (1.0 + lax.erf(x * _SQRT_HALF))
    return x * cdf, cdf + x * jnp.exp(-0.5 * x * x) * _INV_SQRT_2PI


def _rms_bwd(dy, x, gain):
    r = lax.rsqrt(jnp.mean(x * x, axis=-1, keepdims=True) + EPS)
    xhat = x * r
    gdy = dy * gain
    dx = r * (gdy - xhat * jnp.mean(gdy * xhat, axis=-1, keepdims=True))
    return dx, jnp.sum(dy * xhat, axis=0, keepdims=True)


def _norm_matmul(x, gain, w4, name, tm=512):
    S, D = x.shape
    nsh, _, ns = w4.shape
    tm = min(tm, S)

    def body(x_ref, g_ref, w_ref, hn_ref, o_ref):
        xf = x_ref[...]
        r = lax.rsqrt(jnp.mean(xf * xf, axis=-1, keepdims=True) + EPS)
        hn = (xf * r * g_ref[...]).astype(BF16)
        hn_ref[...] = hn
        for j in range(nsh):
            o_ref[:, j * ns:(j + 1) * ns] = _dot(hn, w_ref[j]).astype(BF16)

    return pl.pallas_call(
        body, name=name, grid=(S // tm,),
        in_specs=[pl.BlockSpec((tm, D), lambda i: (i, 0)), _resident((1, D)), _resident(w4.shape)],
        out_specs=[pl.BlockSpec((tm, D), lambda i: (i, 0)), pl.BlockSpec((tm, nsh * ns), lambda i: (i, 0))],
        out_shape=[jax.ShapeDtypeStruct((S, D), BF16), jax.ShapeDtypeStruct((S, nsh * ns), BF16)],
        compiler_params=_cp(("parallel",)),
    )(x, gain, w4)


def _matmul_res(a, w, res, name, tm=512):
    S, K = a.shape
    N = w.shape[1]
    tm = min(tm, S)

    def body(a_ref, w_ref, r_ref, o_ref):
        o_ref[...] = r_ref[...] + _dot(a_ref[...], w_ref[...])

    return pl.pallas_call(
        body, name=name, grid=(S // tm,),
        in_specs=[pl.BlockSpec((tm, K), lambda i: (i, 0)), _resident(w.shape), pl.BlockSpec((tm, N), lambda i: (i, 0))],
        out_specs=pl.BlockSpec((tm, N), lambda i: (i, 0)),
        out_shape=jax.ShapeDtypeStruct((S, N), F32),
        compiler_params=_cp(("parallel",)),
    )(a, w, res)


def _matmul_nt(dh, w, name, tm=512):
    S, N = dh.shape
    K = w.shape[0]
    tm = min(tm, S)

    def body(d_ref, w_ref, o_ref):
        o_ref[...] = _dot_nt(d_ref[...].astype(BF16), w_ref[...]).astype(BF16)

    return pl.pallas_call(
        body, name=name, grid=(S // tm,),
        in_specs=[pl.BlockSpec((tm, N), lambda i: (i, 0)), _resident(w.shape)],
        out_specs=pl.BlockSpec((tm, K), lambda i: (i, 0)),
        out_shape=jax.ShapeDtypeStruct((S, K), BF16),
        compiler_params=_cp(("parallel",)),
    )(dh, w)


def _matmul_tn(a, b, name, *, ka, nb, out_dtype, ts=512):
    S, KA = a.shape
    NB = b.shape[1]
    ts = min(ts, S)
    J = max(KA // ka, NB // nb)
    a_map = (lambda j, s: (s, j)) if KA // ka > 1 else (lambda j, s: (s, 0))
    b_map = (lambda j, s: (s, j)) if NB // nb > 1 else (lambda j, s: (s, 0))
    last = S // ts - 1

    def body(a_ref, b_ref, o_ref, acc):
        s = pl.program_id(1)

        @pl.when(s == 0)
        def _():
            acc[...] = jnp.zeros_like(acc)

        acc[...] += _dot_tn(a_ref[...].astype(BF16), b_ref[...].astype(BF16))

        @pl.when(s == last)
        def _():
            o_ref[0] = acc[...].astype(out_dtype)

    return pl.pallas_call(
        body, name=name, grid=(J, S // ts),
        in_specs=[pl.BlockSpec((ts, ka), a_map), pl.BlockSpec((ts, nb), b_map)],
        out_specs=pl.BlockSpec((1, ka, nb), lambda j, s: (j, 0, 0)),
        out_shape=jax.ShapeDtypeStruct((J, ka, nb), out_dtype),
        scratch_shapes=[pltpu.VMEM((ka, nb), F32)],
        compiler_params=_cp(("parallel", "arbitrary")),
    )(a, b)


def _sgu_prep(w_s):
    G = w_s.shape[0]

    def body(w_ref, t_ref, tt_ref):
        tri = lax.broadcasted_iota(jnp.int32, (CHUNK, CHUNK), 0) >= lax.broadcasted_iota(jnp.int32, (CHUNK, CHUNK), 1)
        for g in range(G):
            t = jnp.where(tri, w_ref[g], 0.0)
            t_ref[g] = t.astype(BF16)
            tt_ref[g] = t.T.astype(BF16)

    return pl.pallas_call(
        body, name="sgu_prep",
        out_shape=[jax.ShapeDtypeStruct(w_s.shape, BF16), jax.ShapeDtypeStruct(w_s.shape, BF16)],
        compiler_params=_cp(None),
    )(w_s)


def _sgu_fwd(z, x, vg, wtril, bT, wout, tm=256):
    S = z.shape[0]
    W = z.shape[1] // 2
    D = x.shape[1]
    tm = min(tm, S)

    def body(z_ref, x_ref, vg_ref, wt_ref, bT_ref, wo_ref, yp_ref, h_ref):
        def chunk(c, carry):
            r0 = pl.multiple_of(c * CHUNK, CHUNK)
            zc = z_ref[pl.ds(r0, CHUNK), :].astype(F32)
            u = _gelu(zc[:, :W])
            v = _gelu(zc[:, W:])
            rv = lax.rsqrt(jnp.mean(v * v, axis=-1, keepdims=True) + EPS)
            vn = (v * rv * vg_ref[...]).astype(BF16)
            for g in range(SGU_GROUPS):
                sl = slice(g * LANES, (g + 1) * LANES)
                s = _dot(wt_ref[g], vn[:, sl]) + bT_ref[:, g:g + 1]
                yp_ref[pl.ds(r0, CHUNK), sl] = (u[:, sl] * s).astype(BF16)
            return carry

        lax.fori_loop(0, tm // CHUNK, chunk, 0)
        h_ref[...] = x_ref[...] + _dot(yp_ref[...], wo_ref[...])

    return pl.pallas_call(
        body, name="sgu_fwd", grid=(S // tm,),
        in_specs=[pl.BlockSpec((tm, 2 * W), lambda i: (i, 0)), pl.BlockSpec((tm, D), lambda i: (i, 0)),
                  _resident((1, W)), _resident(wtril.shape), _resident(bT.shape), _resident(wout.shape)],
        out_specs=[pl.BlockSpec((tm, W), lambda i: (i, 0)), pl.BlockSpec((tm, D), lambda i: (i, 0))],
        out_shape=[jax.ShapeDtypeStruct((S, W), BF16), jax.ShapeDtypeStruct((S, D), F32)],
        compiler_params=_cp(("parallel",)),
    )(z, x, vg, wtril, bT, wout)


def _sgu_bwd(dh, z, vg, wtril, wtrilT, bT, wout, tm=256):
    S = z.shape[0]
    W = z.shape[1] // 2
    D = dh.shape[1]
    G = SGU_GROUPS
    tm = min(tm, S)
    last = S // tm - 1

    def body(dh_ref, z_ref, vg_ref, wt_ref, wtT_ref, bT_ref, wo_ref,
             dz_ref, dws_ref, dbT_ref, dvg_ref, dyp_s, du_s, dvn_s, dsacc):
        i = pl.program_id(0)

        @pl.when(i == 0)
        def _():
            dws_ref[...] = jnp.zeros_like(dws_ref)
            dvg_ref[...] = jnp.zeros_like(dvg_ref)
            dsacc[...] = jnp.zeros_like(dsacc)

        dyp_s[...] = _dot_nt(dh_ref[...].astype(BF16), wo_ref[...])
        tri = lax.broadcasted_iota(jnp.int32, (CHUNK, CHUNK), 0) >= lax.broadcasted_iota(jnp.int32, (CHUNK, CHUNK), 1)

        def chunk(c, carry):
            r0 = pl.multiple_of(c * CHUNK, CHUNK)
            zc = z_ref[pl.ds(r0, CHUNK), :].astype(F32)
            u, gu = _gelu_and_grad(zc[:, :W])
            v, gv = _gelu_and_grad(zc[:, W:])
            rv = lax.rsqrt(jnp.mean(v * v, axis=-1, keepdims=True) + EPS)
            vhat = v * rv
            vgain = vg_ref[...]
            vn = (vhat * vgain).astype(BF16)
            dyp = dyp_s[pl.ds(r0, CHUNK), :]
            for g in range(G):
                sl = slice(g * LANES, (g + 1) * LANES)
                vng = vn[:, sl]
                s = _dot(wt_ref[g], vng) + bT_ref[:, g:g + 1]
                ds = dyp[:, sl] * u[:, sl]
                du_s[:, sl] = dyp[:, sl] * s
                dsb = ds.astype(BF16)
                dvn_s[:, sl] = _dot(wtT_ref[g], dsb)
                dws_ref[g] += jnp.where(tri, _dot_nt(dsb, vng), 0.0)
                dsacc[g] += ds
            dvn = dvn_s[...]
            dvg_ref[...] += jnp.sum(dvn * vhat, axis=0, keepdims=True)
            gdy = dvn * vgain
            dv = rv * (gdy - vhat * jnp.mean(gdy * vhat, axis=-1, keepdims=True))
            dz_ref[pl.ds(r0, CHUNK), :W] = (du_s[...] * gu).astype(BF16)
            dz_ref[pl.ds(r0, CHUNK), W:] = (dv * gv).astype(BF16)
            return carry

        lax.fori_loop(0, tm // CHUNK, chunk, 0)

        @pl.when(i == last)
        def _():
            for g in range(G):
                dbT_ref[:, g:g + 1] = jnp.sum(dsacc[g], axis=1, keepdims=True)

    return pl.pallas_call(
        body, name="sgu_bwd", grid=(S // tm,),
        in_specs=[pl.BlockSpec((tm, D), lambda i: (i, 0)), pl.BlockSpec((tm, 2 * W), lambda i: (i, 0)),
                  _resident((1, W)), _resident(wtril.shape), _resident(wtrilT.shape), _resident(bT.shape),
                  _resident(wout.shape)],
        out_specs=[pl.BlockSpec((tm, 2 * W), lambda i: (i, 0)),
                   pl.BlockSpec((G, CHUNK, CHUNK), lambda i: (0, 0, 0)),
                   pl.BlockSpec((CHUNK, G), lambda i: (0, 0)),
                   pl.BlockSpec((1, W), lambda i: (0, 0))],
        out_shape=[jax.ShapeDtypeStruct((S, 2 * W), BF16), jax.ShapeDtypeStruct((G, CHUNK, CHUNK), F32),
                   jax.ShapeDtypeStruct((CHUNK, G), F32), jax.ShapeDtypeStruct((1, W), F32)],
        scratch_shapes=[pltpu.VMEM((tm, W), F32), pltpu.VMEM((CHUNK, W), F32), pltpu.VMEM((CHUNK, W), F32),
                        pltpu.VMEM((G, CHUNK, CHUNK), F32)],
        compiler_params=_cp(("arbitrary",)),
    )(dh, z, vg, wtril, wtrilT, bT, wout)


def _conv_taps(a32, r0, R):
    X = a32[pl.ds(r0 + HALO - 8, R + 8), :]
    return X[8:], pltpu.roll(X, 1, 0)[8:], pltpu.roll(X, 2, 0)[8:]


def _ffn_fwd(a, h_in, cw, cb, wdown, name, tm=256, R=64):
    S, C = a.shape
    F = C // 2
    D = h_in.shape[1]
    tm = min(tm, S)

    def body(a_ref, halo_ref, h_ref, cw_ref, cb_ref, wd_ref, f_ref, ho_ref, a32):
        i = pl.program_id(0)
        a32[0:HALO, :] = jnp.where(i > 0, halo_ref[...].astype(F32), 0.0)
        a32[HALO:, :] = a_ref[...].astype(F32)

        def sub(k, carry):
            r0 = pl.multiple_of(k * R, R)
            a0, a1, a2 = _conv_taps(a32, r0, R)
            cpre = cw_ref[0:1, :] * a2 + cw_ref[1:2, :] * a1 + cw_ref[2:3, :] * a0 + cb_ref[...]
            g = cpre[:, :F]
            f_ref[pl.ds(r0, R), :] = (g * jax.nn.sigmoid(g) * cpre[:, F:]).astype(BF16)
            return carry

        lax.fori_loop(0, tm // R, sub, 0)
        ho_ref[...] = h_ref[...] + _dot(f_ref[...], wd_ref[...])

    hb = tm // HALO
    return pl.pallas_call(
        body, name=name, grid=(S // tm,),
        in_specs=[pl.BlockSpec((tm, C), lambda i: (i, 0)),
                  pl.BlockSpec((HALO, C), lambda i: (jnp.maximum(i * hb - 1, 0), 0)),
                  pl.BlockSpec((tm, D), lambda i: (i, 0)),
                  _resident((3, C)), _resident((1, C)), _resident(wdown.shape)],
        out_specs=[pl.BlockSpec((tm, F), lambda i: (i, 0)), pl.BlockSpec((tm, D), lambda i: (i, 0))],
        out_shape=[jax.ShapeDtypeStruct((S, F), BF16), jax.ShapeDtypeStruct((S, D), F32)],
        scratch_shapes=[pltpu.VMEM((HALO + tm, C), F32)],
        compiler_params=_cp(("parallel",)),
    )(a, a, h_in, cw, cb, wdown)


def _ffn_bwd_dc(dh, a, cw, cb, wdown, name, tm=256, R=64):
    S, C = a.shape
    F = C // 2
    D = dh.shape[1]
    tm = min(tm, S)

    def body(dh_ref, a_ref, halo_ref, cw_ref, cb_ref, wd_ref, dc_ref, dcw_ref, dcb_ref, a32, df_s):
        i = pl.program_id(0)

        @pl.when(i == 0)
        def _():
            dcw_ref[...] = jnp.zeros_like(dcw_ref)
            dcb_ref[...] = jnp.zeros_like(dcb_ref)

        a32[0:HALO, :] = jnp.where(i > 0, halo_ref[...].astype(F32), 0.0)
        a32[HALO:, :] = a_ref[...].astype(F32)
        df_s[...] = _dot_nt(dh_ref[...].astype(BF16), wd_ref[...])

        def sub(k, carry):
            r0 = pl.multiple_of(k * R, R)
            a0, a1, a2 = _conv_taps(a32, r0, R)
            cpre = cw_ref[0:1, :] * a2 + cw_ref[1:2, :] * a1 + cw_ref[2:3, :] * a0 + cb_ref[...]
            g = cpre[:, :F]
            val = cpre[:, F:]
            sg = jax.nn.sigmoid(g)
            df = df_s[pl.ds(r0, R), :]
            dg = df * val * (sg * (1.0 + g * (1.0 - sg)))
            dval = df * (g * sg)
            dc = jnp.concatenate([dg, dval], axis=1)
            dc_ref[pl.ds(r0, R), :] = dc.astype(BF16)
            dcw_ref[0:1, :] += jnp.sum(dc * a2, axis=0, keepdims=True)
            dcw_ref[1:2, :] += jnp.sum(dc * a1, axis=0, keepdims=True)
            dcw_ref[2:3, :] += jnp.sum(dc * a0, axis=0, keepdims=True)
            dcb_ref[...] += jnp.sum(dc, axis=0, keepdims=True)
            return carry

        lax.fori_loop(0, tm // R, sub, 0)

    hb = tm // HALO
    return pl.pallas_call(
        body, name=name, grid=(S // tm,),
        in_specs=[pl.BlockSpec((tm, D), lambda i: (i, 0)),
                  pl.BlockSpec((tm, C), lambda i: (i, 0)),
                  pl.BlockSpec((HALO, C), lambda i: (jnp.maximum(i * hb - 1, 0), 0)),
                  _resident((3, C)), _resident((1, C)), _resident(wdown.shape)],
        out_specs=[pl.BlockSpec((tm, C), lambda i: (i, 0)),
                   pl.BlockSpec((3, C), lambda i: (0, 0)), pl.BlockSpec((1, C), lambda i: (0, 0))],
        out_shape=[jax.ShapeDtypeStruct((S, C), BF16), jax.ShapeDtypeStruct((3, C), F32),
                   jax.ShapeDtypeStruct((1, C), F32)],
        scratch_shapes=[pltpu.VMEM((HALO + tm, C), F32), pltpu.VMEM((tm, F), F32)],
        compiler_params=_cp(("arbitrary",)),
    )(dh, a, a, cw, cb, wdown)


def _bwd_norm(dA, w4, h_in, gain, dh_out, name, conv_w=None, tm=256, R=64):
    S, N = dA.shape
    nsh, D, ns = w4.shape
    tm = min(tm, S)
    nt = S // tm
    conv = conv_w is not None

    def finish(src_ref, w_ref, h_ref, g_ref, dho_ref, dhi_ref, dg_ref):
        dhn = _dot_nt(src_ref[:, 0:ns], w_ref[0])
        for j in range(1, nsh):
            dhn += _dot_nt(src_ref[:, j * ns:(j + 1) * ns], w_ref[j])
        dx, dgain = _rms_bwd(dhn, h_ref[...], g_ref[...])
        dg_ref[...] += dgain
        dhi_ref[...] = dho_ref[...] + dx

    def body_plain(dA_ref, w_ref, h_ref, g_ref, dho_ref, dhi_ref, dg_ref):
        @pl.when(pl.program_id(0) == 0)
        def _():
            dg_ref[...] = jnp.zeros_like(dg_ref)

        finish(dA_ref, w_ref, h_ref, g_ref, dho_ref, dhi_ref, dg_ref)

    def body_conv(dc_ref, halo_ref, cw_ref, w_ref, h_ref, g_ref, dho_ref, dhi_ref, dg_ref, da_ref, dc32):
        i = pl.program_id(0)

        @pl.when(i == 0)
        def _():
            dg_ref[...] = jnp.zeros_like(dg_ref)

        dc32[0:tm, :] = dc_ref[...].astype(F32)
        dc32[tm:, :] = jnp.where(i < nt - 1, halo_ref[...].astype(F32), 0.0)

        def sub(k, carry):
            r0 = pl.multiple_of(k * R, R)
            X = dc32[pl.ds(r0, R + 8), :]
            d0 = X[:R]
            d1 = pltpu.roll(X, R + 7, 0)[:R]
            d2 = pltpu.roll(X, R + 6, 0)[:R]
            da = cw_ref[2:3, :] * d0 + cw_ref[1:2, :] * d1 + cw_ref[0:1, :] * d2
            da_ref[pl.ds(r0, R), :] = da.astype(BF16)
            return carry

        lax.fori_loop(0, tm // R, sub, 0)
        finish(da_ref, w_ref, h_ref, g_ref, dho_ref, dhi_ref, dg_ref)

    row = lambda width: pl.BlockSpec((tm, width), lambda i: (i, 0))
    common_in = [_resident(w4.shape), row(D), _resident((1, D)), row(D)]
    common_out = [row(D), pl.BlockSpec((1, D), lambda i: (0, 0))]
    common_shape = [jax.ShapeDtypeStruct((S, D), F32), jax.ShapeDtypeStruct((1, D), F32)]
    if not conv:
        return pl.pallas_call(
            body_plain, name=name, grid=(nt,),
            in_specs=[row(N)] + common_in, out_specs=common_out, out_shape=common_shape,
            compiler_params=_cp(("arbitrary",)),
        )(dA, w4, h_in, gain, dh_out)
    hb = tm // HALO
    nhb = S // HALO
    return pl.pallas_call(
        body_conv, name=name, grid=(nt,),
        in_specs=[row(N), pl.BlockSpec((HALO, N), lambda i: (jnp.minimum((i + 1) * hb, nhb - 1), 0)),
                  _resident((3, N))] + common_in,
        out_specs=common_out + [row(N)],
        out_shape=common_shape + [jax.ShapeDtypeStruct((S, N), BF16)],
        scratch_shapes=[pltpu.VMEM((tm + HALO, N), F32)],
        compiler_params=_cp(("arbitrary",)),
    )(dA, dA, conv_w, w4, h_in, gain, dh_out)


def _rel_buckets_flat():
    q = np.arange(CHUNK)[:, None] + CHUNK
    k = np.arange(2 * CHUNK)[None, :]
    n = np.maximum(q - k, 0)
    max_exact = REL_BUCKETS // 2
    large = max_exact + (np.log(np.maximum(n, 1).astype(np.float32) / max_exact)
                         / math.log(REL_MAX_DIST / max_exact) * (REL_BUCKETS - max_exact)).astype(np.int32)
    large = np.minimum(large, REL_BUCKETS - 1)
    return np.where(n < max_exact, n, large).astype(np.int32).reshape(1, CHUNK * 2 * CHUNK)


def _split_bf16(x):
    hi = x.astype(BF16)
    return hi, (x - hi.astype(F32)).astype(BF16)


def _rel_bias_expand(rel_bias, bucket):
    B, H = rel_bias.shape
    n = bucket.shape[1]

    def body(rb_ref, bk_ref, o_ref):
        oh = (bk_ref[...] == lax.broadcasted_iota(jnp.int32, (B, n), 0)).astype(BF16)
        hi, lo = _split_bf16(rb_ref[...])
        o_ref[...] = _dot_tn(hi, oh) + _dot_tn(lo, oh)

    return pl.pallas_call(body, name="rel_bias_expand", out_shape=jax.ShapeDtypeStruct((H, n), F32),
                          compiler_params=_cp(None))(rel_bias, bucket)


def _rel_bias_reduce(dbias, bucket):
    H, n = dbias.shape
    B = REL_BUCKETS

    def body(db_ref, bk_ref, o_ref):
        oh = (bk_ref[...] == lax.broadcasted_iota(jnp.int32, (B, n), 0)).astype(BF16)
        hi, lo = _split_bf16(db_ref[...])
        o_ref[...] = _dot_nt(oh, hi) + _dot_nt(oh, lo)

    return pl.pallas_call(body, name="rel_bias_reduce", out_shape=jax.ShapeDtypeStruct((B, H), F32),
                          compiler_params=_cp(None))(dbias, bucket)


def _lo_mask(rows):
    return lax.broadcasted_iota(jnp.int32, (rows, LANES), 1) < HEAD_DIM


def _half_sums(y, lo):
    s_lo = jnp.sum(jnp.where(lo, y, 0.0), axis=-1, keepdims=True)
    s_hi = jnp.sum(jnp.where(lo, 0.0, y), axis=-1, keepdims=True)
    return jnp.where(lo, s_lo, s_hi)


def _half_rms(x, gain, lo):
    r = lax.rsqrt(_half_sums(x * x, lo) * (1.0 / HEAD_DIM) + EPS)
    xhat = x * r
    return xhat * gain, xhat, r


def _half_rms_bwd(dy, xhat, r, gain, lo):
    gdy = dy * gain
    dx = r * (gdy - xhat * (_half_sums(gdy * xhat, lo) * (1.0 / HEAD_DIM)))
    return dx, jnp.sum(dy * xhat, axis=0, keepdims=True)


def _dup_half(pair, e, lo):
    sw = pltpu.roll(pair, HEAD_DIM, 1)
    return jnp.where(lo, pair, sw) if e == 0 else jnp.where(lo, sw, pair)


def _band_valid(n):
    qi = lax.broadcasted_iota(jnp.int32, (CHUNK, 2 * CHUNK), 0)
    ki = lax.broadcasted_iota(jnp.int32, (CHUNK, 2 * CHUNK), 1)
    dist = qi + CHUNK - ki
    return (dist >= 0) & (dist < CHUNK) & ((n > 0) | (ki >= CHUNK))


def _softmax_sink(s, valid, sink):
    s = jnp.where(valid, s, -jnp.inf)
    m = jnp.maximum(jnp.max(s, axis=-1, keepdims=True), sink)
    p = jnp.exp(s - m)
    esink = jnp.exp(sink - m)
    inv = 1.0 / (jnp.sum(p, axis=-1, keepdims=True) + esink)
    return p * inv, esink * inv


QW = N_HEADS * HEAD_DIM
KVW = N_KV_HEADS * HEAD_DIM


def _attn_fwd(qkv, qg2, kg2, sinks, bias):
    S = qkv.shape[0]
    nb = S // CHUNK

    def body(cur_ref, prev_ref, qg_ref, kg_ref, sink_ref, bias_ref, o_ref):
        n = pl.program_id(0)
        lo = _lo_mask(CHUNK)
        lo2 = _lo_mask(2 * CHUNK)
        valid = _band_valid(n)
        for j in range(N_KV_HEADS // 2):
            kc = slice(QW + j * LANES, QW + (j + 1) * LANES)
            vc = slice(QW + KVW + j * LANES, QW + KVW + (j + 1) * LANES)
            kpair = jnp.concatenate([prev_ref[:, j * LANES:(j + 1) * LANES], cur_ref[:, kc]], axis=0).astype(F32)
            vpair = jnp.concatenate([prev_ref[:, KVW + j * LANES:KVW + (j + 1) * LANES], cur_ref[:, vc]], axis=0).astype(F32)
            knpair, _, _ = _half_rms(kpair, kg_ref[...], lo2)
            for e in range(2):
                hk = 2 * j + e
                kdup = _dup_half(knpair, e, lo2).astype(BF16)
                vdup = _dup_half(vpair, e, lo2).astype(BF16)
                for i2 in range(2):
                    pi = 2 * hk + i2
                    qpair = cur_ref[:, pi * LANES:(pi + 1) * LANES].astype(F32)
                    qn, _, _ = _half_rms(qpair, qg_ref[...], lo)
                    outs = []
                    for par in range(2):
                        h = 2 * pi + par
                        hm = lo if par == 0 else jnp.logical_not(lo)
                        qm = jnp.where(hm, qn, 0.0).astype(BF16)
                        s = _dot_nt(qm, kdup) * (HEAD_DIM ** -0.5) + bias_ref[h]
                        p, _ = _softmax_sink(s, valid, sink_ref[h])
                        outs.append(_dot(p.astype(BF16), vdup))
                    o_ref[:, pi * LANES:(pi + 1) * LANES] = jnp.where(lo, outs[0], outs[1]).astype(BF16)

    return pl.pallas_call(
        body, name="attn_fwd", grid=(nb,),
        in_specs=[pl.BlockSpec((CHUNK, QW + 2 * KVW), lambda n: (n, 0)),
                  pl.BlockSpec((CHUNK, 2 * KVW), lambda n: (jnp.maximum(n - 1, 0), QW // (2 * KVW))),
                  _resident((1, LANES)), _resident((1, LANES)),
                  pl.BlockSpec(memory_space=pltpu.SMEM),
                  _resident(bias.shape)],
        out_specs=pl.BlockSpec((CHUNK, QW), lambda n: (n, 0)),
        out_shape=jax.ShapeDtypeStruct((S, QW), BF16),
        compiler_params=_cp(("parallel",)),
    )(qkv, qkv, qg2, kg2, sinks, bias)


def _attn_bwd(qkv, do, qg2, kg2, sinks, bias):
    S = qkv.shape[0]
    nb = S // CHUNK
    H = N_HEADS

    def body(cur_ref, prev_ref, do_ref, qg_ref, kg_ref, sink_ref, bias_ref,
             dqkv_ref, dbias_ref, dqg_ref, dkg_ref, dsink_ref, carry, band, dsacc, gacc):
        i = pl.program_id(0)
        n = nb - 1 - i
        lo = _lo_mask(CHUNK)
        lo2 = _lo_mask(2 * CHUNK)
        lane = lax.broadcasted_iota(jnp.int32, (CHUNK, LANES), 1)
        valid = _band_valid(n)

        @pl.when(i == 0)
        def _():
            dbias_ref[...] = jnp.zeros_like(dbias_ref)
            carry[...] = jnp.zeros_like(carry)
            dsacc[...] = jnp.zeros_like(dsacc)
            gacc[...] = jnp.zeros_like(gacc)

        qgain = qg_ref[...]
        kgain = kg_ref[...]
        for j in range(N_KV_HEADS // 2):
            kc = slice(QW + j * LANES, QW + (j + 1) * LANES)
            vc = slice(QW + KVW + j * LANES, QW + KVW + (j + 1) * LANES)
            kpair = jnp.concatenate([prev_ref[:, j * LANES:(j + 1) * LANES], cur_ref[:, kc]], axis=0).astype(F32)
            vpair = jnp.concatenate([prev_ref[:, KVW + j * LANES:KVW + (j + 1) * LANES], cur_ref[:, vc]], axis=0).astype(F32)
            knpair, khat, kr = _half_rms(kpair, kgain, lo2)
            dk_folds = []
            dv_folds = []
            for e in range(2):
                hk = 2 * j + e
                kdup = _dup_half(knpair, e, lo2).astype(BF16)
                vdup = _dup_half(vpair, e, lo2).astype(BF16)
                dkd = jnp.zeros((2 * CHUNK, LANES), F32)
                dvd = jnp.zeros((2 * CHUNK, LANES), F32)
                for i2 in range(2):
                    pi = 2 * hk + i2
                    qpair = cur_ref[:, pi * LANES:(pi + 1) * LANES].astype(F32)
                    qn, qhat, qr = _half_rms(qpair, qgain, lo)
                    dopair = do_ref[:, pi * LANES:(pi + 1) * LANES].astype(F32)
                    dqn = jnp.zeros((CHUNK, LANES), F32)
                    for par in range(2):
                        h = 2 * pi + par
                        hm = lo if par == 0 else jnp.logical_not(lo)
                        qm = jnp.where(hm, qn, 0.0).astype(BF16)
                        s = _dot_nt(qm, kdup) * (HEAD_DIM ** -0.5) + bias_ref[h]
                        p, psink = _softmax_sink(s, valid, sink_ref[h])
                        dom = jnp.where(hm, dopair, 0.0).astype(BF16)
                        dp = _dot_nt(dom, vdup)
                        delta = jnp.sum(p * dp, axis=-1, keepdims=True)
                        ds = p * (dp - delta)
                        dbias_ref[h] += ds
                        dsacc[...] += jnp.where(lane == h, -(psink * delta), 0.0)
                        dsr = (ds * (HEAD_DIM ** -0.5)).astype(BF16)
                        dqn += jnp.where(hm, _dot(dsr, kdup), 0.0)
                        dkd += _dot_tn(dsr, qm)
                        dvd += _dot_tn(p.astype(BF16), dom)
                    dq, dqg = _half_rms_bwd(dqn, qhat, qr, qgain, lo)
                    gacc[0:1, :] += dqg
                    dqkv_ref[:, pi * LANES:(pi + 1) * LANES] = dq.astype(BF16)
                dk_folds.append(dkd + pltpu.roll(dkd, HEAD_DIM, 1))
                dv_folds.append(dvd + pltpu.roll(dvd, HEAD_DIM, 1))
            dkn = jnp.where(lo2, dk_folds[0], dk_folds[1])
            dk, dkg = _half_rms_bwd(dkn, khat, kr, kgain, lo2)
            gacc[1:2, :] += dkg
            band[:, j * LANES:(j + 1) * LANES] = dk
            band[:, KVW + j * LANES:KVW + (j + 1) * LANES] = jnp.where(lo2, dv_folds[0], dv_folds[1])
        dqkv_ref[:, QW:] = (band[CHUNK:, :] + carry[...]).astype(BF16)
        carry[...] = band[0:CHUNK, :]

        @pl.when(i == nb - 1)
        def _():
            g = gacc[...]
            g = g + pltpu.roll(g, HEAD_DIM, 1)
            dqg_ref[...] = g[0:1, :]
            dkg_ref[...] = g[1:2, :]
            dsink_ref[...] = jnp.sum(dsacc[...], axis=0, keepdims=True)

    vec = pl.BlockSpec((1, LANES), lambda i: (0, 0))
    return pl.pallas_call(
        body, name="attn_bwd", grid=(nb,),
        in_specs=[pl.BlockSpec((CHUNK, QW + 2 * KVW), lambda i: (nb - 1 - i, 0)),
                  pl.BlockSpec((CHUNK, 2 * KVW), lambda i: (jnp.maximum(nb - 2 - i, 0), QW // (2 * KVW))),
                  pl.BlockSpec((CHUNK, QW), lambda i: (nb - 1 - i, 0)),
                  _resident((1, LANES)), _resident((1, LANES)),
                  pl.BlockSpec(memory_space=pltpu.SMEM),
                  _resident(bias.shape)],
        out_specs=[pl.BlockSpec((CHUNK, QW + 2 * KVW), lambda i: (nb - 1 - i, 0)),
                   pl.BlockSpec((H, CHUNK, 2 * CHUNK), lambda i: (0, 0, 0)), vec, vec, vec],
        out_shape=[jax.ShapeDtypeStruct((S, QW + 2 * KVW), BF16),
                   jax.ShapeDtypeStruct((H, CHUNK, 2 * CHUNK), F32),
                   jax.ShapeDtypeStruct((1, LANES), F32), jax.ShapeDtypeStruct((1, LANES), F32),
                   jax.ShapeDtypeStruct((1, LANES), F32)],
        scratch_shapes=[pltpu.VMEM((CHUNK, 2 * KVW), F32), pltpu.VMEM((2 * CHUNK, 2 * KVW), F32),
                        pltpu.VMEM((CHUNK, LANES), F32), pltpu.VMEM((8, LANES), F32)],
        compiler_params=_cp(("arbitrary",)),
    )(qkv, qkv, do, qg2, kg2, sinks, bias)


def _loss_head(y, target, tm=512):
    S, D = y.shape
    tm = min(tm, S)

    def body(y_ref, t_ref, dy_ref, l_ref):
        @pl.when(pl.program_id(0) == 0)
        def _():
            l_ref[...] = jnp.zeros_like(l_ref)

        e = y_ref[...] - t_ref[...]
        dy_ref[...] = e * (1.0 / D)
        rows = jnp.sum(e * e, axis=-1, keepdims=True) * (1.0 / D)
        l_ref[...] += 0.5 * jnp.sum(rows, axis=0, keepdims=True)

    dy, l = pl.pallas_call(
        body, name="loss_head", grid=(S // tm,),
        in_specs=[pl.BlockSpec((tm, D), lambda i: (i, 0)), pl.BlockSpec((tm, D), lambda i: (i, 0))],
        out_specs=[pl.BlockSpec((tm, D), lambda i: (i, 0)), pl.BlockSpec((1, 1), lambda i: (0, 0))],
        out_shape=[jax.ShapeDtypeStruct((S, D), F32), jax.ShapeDtypeStruct((1, 1), F32)],
        compiler_params=_cp(("arbitrary",)),
    )(y, target)
    return dy, l[0, 0]


def _local_grads(x, target, p):
    D = x.shape[1]
    g = {}
    bucket = jnp.asarray(_rel_buckets_flat())
    wtril, wtrilT = _sgu_prep(p["sgu_w_s"])
    bT = p["sgu_b_s"].T
    bias = _rel_bias_expand(p["rel_bias"], bucket).reshape(N_HEADS, CHUNK, 2 * CHUNK)
    qg2 = jnp.tile(p["attn_q_gain"], (1, 2))
    kg2 = jnp.tile(p["attn_k_gain"], (1, 2))
    sinks = p["attn_sinks"].reshape(N_HEADS)
    mix0, mix1 = p["mix_norm"][0:1], p["mix_norm"][1:2]
    fn0, fn1 = p["ffn_norm"][0:1], p["ffn_norm"][1:2]

    hn0, z = _norm_matmul(x, mix0, p["sgu_w_in"], "sgu_in")
    yp, h1 = _sgu_fwd(z, x, p["sgu_v_gain"], wtril, bT, p["sgu_w_out"])
    hf0, a0 = _norm_matmul(h1, fn0, p["ffn_w_up"][0], "ffn0_up")
    f0, h2 = _ffn_fwd(a0, h1, p["ffn_conv_w"][0], p["ffn_conv_b"][0:1], p["ffn_w_down"][0], "ffn0_fwd")
    hn1, qkv = _norm_matmul(h2, mix1, p["attn_w_qkv"], "attn_qkv")
    o = _attn_fwd(qkv, qg2, kg2, sinks, bias)
    h3 = _matmul_res(o, p["attn_w_o"], h2, "attn_out")
    hf1, a1 = _norm_matmul(h3, fn1, p["ffn_w_up"][1], "ffn1_up")
    f1, h4 = _ffn_fwd(a1, h3, p["ffn_conv_w"][1], p["ffn_conv_b"][1:2], p["ffn_w_down"][1], "ffn1_fwd")
    dh4, loss = _loss_head(h4, target)

    nup = p["ffn_w_up"][0].shape[2]
    dc1, dcw1, dcb1 = _ffn_bwd_dc(dh4, a1, p["ffn_conv_w"][1], p["ffn_conv_b"][1:2], p["ffn_w_down"][1], "ffn1_bwd_dc")
    dh3, dfn1, da1 = _bwd_norm(dc1, p["ffn_w_up"][1], h3, fn1, dh4, "ffn1_bwd_in", conv_w=p["ffn_conv_w"][1])
    g_wdown1 = _matmul_tn(f1, dh4, "ffn1_dw_down", ka=f1.shape[1] // 2, nb=D, out_dtype=BF16)
    g_wup1 = _matmul_tn(hf1, da1, "ffn1_dw_up", ka=D, nb=nup, out_dtype=BF16)
    do = _matmul_nt(dh3, p["attn_w_o"], "attn_bwd_out")
    dqkv, dbias, dqg, dkg, dsink = _attn_bwd(qkv, do, qg2, kg2, sinks, bias)
    dh2, dmix1 = _bwd_norm(dqkv, p["attn_w_qkv"], h2, mix1, dh3, "attn_bwd_in")
    g["attn_w_o"] = _matmul_tn(o, dh3, "attn_dw_o", ka=o.shape[1] // 4, nb=D, out_dtype=BF16)
    g["attn_w_qkv"] = _matmul_tn(hn1, dqkv, "attn_dw_qkv", ka=D, nb=dqkv.shape[1] // 4, out_dtype=BF16)
    g["rel_bias"] = _rel_bias_reduce(dbias.reshape(N_HEADS, CHUNK * 2 * CHUNK), bucket)
    g["attn_q_gain"] = dqg[:, :HEAD_DIM]
    g["attn_k_gain"] = dkg[:, :HEAD_DIM]
    g["attn_sinks"] = dsink[:, :N_HEADS]
    dc0, dcw0, dcb0 = _ffn_bwd_dc(dh2, a0, p["ffn_conv_w"][0], p["ffn_conv_b"][0:1], p["ffn_w_down"][0], "ffn0_bwd_dc")
    dh1, dfn0, da0 = _bwd_norm(dc0, p["ffn_w_up"][0], h1, fn0, dh2, "ffn0_bwd_in", conv_w=p["ffn_conv_w"][0])
    g_wdown0 = _matmul_tn(f0, dh2, "ffn0_dw_down", ka=f0.shape[1] // 2, nb=D, out_dtype=BF16)
    g_wup0 = _matmul_tn(hf0, da0, "ffn0_dw_up", ka=D, nb=nup, out_dtype=BF16)
    dz, dws, dbT, dvg = _sgu_bwd(dh1, z, p["sgu_v_gain"], wtril, wtrilT, bT, p["sgu_w_out"])
    dx, dmix0 = _bwd_norm(dz, p["sgu_w_in"], x, mix0, dh1, "sgu_bwd_in")
    g["sgu_w_out"] = _matmul_tn(yp, dh1, "sgu_dw_out", ka=yp.shape[1] // 4, nb=D, out_dtype=BF16)
    g["sgu_w_in"] = _matmul_tn(hn0, dz, "sgu_dw_in", ka=D, nb=dz.shape[1] // 4, out_dtype=BF16)
    g["sgu_w_s"] = dws
    g["sgu_b_s"] = dbT.T
    g["sgu_v_gain"] = dvg
    g["mix_norm"] = jnp.concatenate([dmix0, dmix1], axis=0)
    g["ffn_norm"] = jnp.concatenate([dfn0, dfn1], axis=0)
    g["ffn_w_up"] = (g_wup0, g_wup1)
    g["ffn_w_down"] = (g_wdown0, g_wdown1)
    g["ffn_conv_w"] = (dcw0, dcw1)
    g["ffn_conv_b"] = jnp.concatenate([dcb0, dcb1], axis=0)
    return loss, dx, g


def _row_tile(rows, cols, n_arrays):
    budget = VMEM_LIMIT_V7X // 4 // (2 * n_arrays * 4 * cols)
    best = 8
    for n in range(1, rows // 8 + 1):
        if rows % n == 0 and (rows // n) % 8 == 0 and rows // n <= budget:
            best = rows // n
            break
    return best


def _adamw(g, w, m, v, name):
    R, C = g.shape
    tr = _row_tile(R, C, 7)

    def body(g_ref, w_ref, m_ref, v_ref, d_ref, mo_ref, vo_ref):
        gg = g_ref[...]
        mn = ADAM_B1 * m_ref[...] + (1.0 - ADAM_B1) * gg
        vn = ADAM_B2 * v_ref[...] + (1.0 - ADAM_B2) * jnp.square(gg)
        m_hat = mn / (1.0 - ADAM_B1 ** ADAM_STEP)
        v_hat = vn / (1.0 - ADAM_B2 ** ADAM_STEP)
        d_ref[...] = -ADAM_LR * (m_hat / (jnp.sqrt(v_hat) + ADAM_EPS) + ADAM_WD * w_ref[...])
        mo_ref[...] = mn
        vo_ref[...] = vn

    spec = pl.BlockSpec((tr, C), lambda i: (i, 0))
    return pl.pallas_call(
        body, name=name, grid=(R // tr,), in_specs=[spec] * 4, out_specs=[spec] * 3,
        out_shape=[jax.ShapeDtypeStruct((R, C), F32)] * 3, compiler_params=_cp(("parallel",)),
    )(g, w, m, v)


def _pair_add(g4, rsib, c, name):
    J, R, C = g4.shape
    Rh = R // 2
    tr = _row_tile(Rh, C, 3)
    g5 = g4.reshape(J, 2, Rh, C)

    def body(c_ref, g_ref, r_ref, o_ref):
        o_ref[...] = (g_ref[...].astype(F32)[0] + r_ref[...].astype(F32)).astype(BF16)

    return pl.pallas_call(
        body, name=name,
        grid_spec=pltpu.PrefetchScalarGridSpec(
            num_scalar_prefetch=1, grid=(J, Rh // tr),
            in_specs=[pl.BlockSpec((1, 1, tr, C), lambda j, i, c_ref: (j, c_ref[0], i, 0)),
                      pl.BlockSpec((1, tr, C), lambda j, i, c_ref: (j, i, 0))],
            out_specs=pl.BlockSpec((1, tr, C), lambda j, i, c_ref: (j, i, 0))),
        out_shape=jax.ShapeDtypeStruct((J, Rh, C), BF16),
        compiler_params=_cp(("parallel", "parallel")),
    )(c, g5, rsib)


def _sum_chips(q, name):
    J, Rh, C = q.shape
    tr = _row_tile(Rh, C, 3)

    def body(q_ref, o_ref):
        qq = q_ref[...].astype(F32)
        o_ref[...] = ((qq[0] + qq[1]) + qq[2]) + qq[3]

    return pl.pallas_call(
        body, name=name, grid=(Rh // tr,),
        in_specs=[pl.BlockSpec((J, tr, C), lambda i: (0, i, 0))],
        out_specs=pl.BlockSpec((tr, C), lambda i: (i, 0)),
        out_shape=jax.ShapeDtypeStruct((Rh, C), F32),
        compiler_params=_cp(("parallel",)),
    )(q)


MESH = pl.DeviceIdType.MESH
ANY = pl.BlockSpec(memory_space=pl.ANY)


def _place():
    x, y, c = lax.axis_index("x"), lax.axis_index("y"), lax.axis_index("c")
    others = [(1 - x, y), (x, 1 - y), (1 - x, 1 - y)]
    return x, y, c, 2 * x + y, others, [2 * ox + oy for ox, oy in others]


def _gather_weights(shards, split):
    n = len(shards)

    def body(*refs):
        ins, outs = refs[:n], refs[n:2 * n]
        send, recv, fsend, frecv, lsem = refs[2 * n:]
        x, y, c, me, others, okey = _place()
        sib = (x, y, 1 - c)

        def rows(t, ref, half):
            if not split[t]:
                return ref
            rh = shards[t].shape[0] // 2
            return ref.at[pl.ds(half * rh, rh), :]

        local = [pltpu.make_async_copy(ins[t], outs[t].at[me], lsem.at[t]) for t in range(n)]
        for cp in local:
            cp.start()
        first = []
        for t in range(n):
            for j, (ox, oy) in enumerate(others):
                cp = pltpu.make_async_remote_copy(
                    src_ref=rows(t, ins[t], c), dst_ref=rows(t, outs[t].at[me], c),
                    send_sem=send.at[t, j], recv_sem=recv.at[t, j], device_id=(ox, oy, c), device_id_type=MESH)
                cp.start()
                first.append(cp)
        passed = []
        for t in range(n):
            for j in range(3):
                landed = rows(t, outs[t].at[okey[j]], c)
                pltpu.make_async_remote_copy(
                    src_ref=landed, dst_ref=landed, send_sem=send.at[t, j], recv_sem=recv.at[t, j],
                    device_id=sib, device_id_type=MESH).wait_recv()
                if split[t]:
                    cp = pltpu.make_async_remote_copy(
                        src_ref=landed, dst_ref=landed, send_sem=fsend.at[t, j], recv_sem=frecv.at[t, j],
                        device_id=sib, device_id_type=MESH)
                    cp.start()
                    passed.append(cp)
        for t in range(n):
            if split[t]:
                for j in range(3):
                    theirs = rows(t, outs[t].at[okey[j]], 1 - c)
                    pltpu.make_async_remote_copy(
                        src_ref=theirs, dst_ref=theirs, send_sem=fsend.at[t, j], recv_sem=frecv.at[t, j],
                        device_id=sib, device_id_type=MESH).wait_recv()
        for cp in first + passed:
            cp.wait_send()
        for cp in local:
            cp.wait()

    sems = [pltpu.SemaphoreType.DMA((n, 3))] * 4 + [pltpu.SemaphoreType.DMA((n,))]
    return pl.pallas_call(
        body, name="gather_weights",
        in_specs=[ANY] * n, out_specs=[ANY] * n,
        out_shape=[jax.ShapeDtypeStruct((4,) + s.shape, s.dtype) for s in shards],
        scratch_shapes=sems,
    )(*shards)


def _pair_exchange(gs):
    n = len(gs)

    def body(*refs):
        ins, outs = refs[:n], refs[n:2 * n]
        send, recv = refs[2 * n:]
        x, y, c, _, _, _ = _place()
        cps = []
        for t in range(n):
            rh = gs[t].shape[1] // 2
            cp = pltpu.make_async_remote_copy(
                src_ref=ins[t].at[:, pl.ds((1 - c) * rh, rh), :], dst_ref=outs[t],
                send_sem=send.at[t], recv_sem=recv.at[t], device_id=(x, y, 1 - c), device_id_type=MESH)
            cp.start()
            cps.append(cp)
        for cp in cps:
            cp.wait()

    return pl.pallas_call(
        body, name="grad_pair_exchange",
        in_specs=[ANY] * n, out_specs=[ANY] * n,
        out_shape=[jax.ShapeDtypeStruct((4, g.shape[1] // 2, g.shape[2]), g.dtype) for g in gs],
        scratch_shapes=[pltpu.SemaphoreType.DMA((n,))] * 2,
    )(*gs)


def _chip_scatter(ps):
    n = len(ps)

    def body(*refs):
        ins, outs = refs[:n], refs[n:2 * n]
        send, recv, lsem = refs[2 * n:]
        x, y, c, me, others, okey = _place()
        local = [pltpu.make_async_copy(ins[t].at[me], outs[t].at[me], lsem.at[t]) for t in range(n)]
        for cp in local:
            cp.start()
        cps = []
        for t in range(n):
            for j, (ox, oy) in enumerate(others):
                cp = pltpu.make_async_remote_copy(
                    src_ref=ins[t].at[okey[j]], dst_ref=outs[t].at[me],
                    send_sem=send.at[t, j], recv_sem=recv.at[t, j], device_id=(ox, oy, c), device_id_type=MESH)
                cp.start()
                cps.append(cp)
        for t in range(n):
            for j in range(3):
                slot = outs[t].at[okey[j]]
                pltpu.make_async_remote_copy(
                    src_ref=slot, dst_ref=slot, send_sem=send.at[t, j], recv_sem=recv.at[t, j],
                    device_id=(x, y, c), device_id_type=MESH).wait_recv()
        for cp in cps:
            cp.wait_send()
        for cp in local:
            cp.wait()

    return pl.pallas_call(
        body, name="grad_chip_scatter",
        in_specs=[ANY] * n, out_specs=[ANY] * n,
        out_shape=[jax.ShapeDtypeStruct(p.shape, p.dtype) for p in ps],
        scratch_shapes=[pltpu.SemaphoreType.DMA((n, 3))] * 2 + [pltpu.SemaphoreType.DMA((n,))],
    )(*ps)


def _half_exchange(hs, out_shapes, where):
    n, no = len(hs), len(out_shapes)

    def body(*refs):
        ins, outs = refs[:n], refs[n:n + no]
        send, recv, lsem = refs[n + no:]
        x, y, c, _, _, _ = _place()
        local, cps = [], []
        for t in range(n):
            oi, layer = where[t]
            rh = hs[t].shape[0]
            mine = outs[oi].at[layer, pl.ds(c * rh, rh), :]
            cp = pltpu.make_async_copy(ins[t], mine, lsem.at[t])
            cp.start()
            local.append(cp)
            cp = pltpu.make_async_remote_copy(
                src_ref=ins[t], dst_ref=mine, send_sem=send.at[t], recv_sem=recv.at[t],
                device_id=(x, y, 1 - c), device_id_type=MESH)
            cp.start()
            cps.append(cp)
        for t in range(n):
            oi, layer = where[t]
            rh = hs[t].shape[0]
            theirs = outs[oi].at[layer, pl.ds((1 - c) * rh, rh), :]
            pltpu.make_async_remote_copy(
                src_ref=theirs, dst_ref=theirs, send_sem=send.at[t], recv_sem=recv.at[t],
                device_id=(x, y, 1 - c), device_id_type=MESH).wait_recv()
        for cp in cps:
            cp.wait_send()
        for cp in local:
            cp.wait()

    return pl.pallas_call(
        body, name="grad_half_exchange",
        in_specs=[ANY] * n, out_specs=[ANY] * no,
        out_shape=[jax.ShapeDtypeStruct(s, F32) for s in out_shapes],
        scratch_shapes=[pltpu.SemaphoreType.DMA((n,))] * 3,
    )(*hs)


SMALL_COLS = 1024
SMALL_PIECE_ROWS = 40


def _allreduce_small(buf):
    pr = SMALL_PIECE_ROWS
    flips = [(d >> 2 & 1, d >> 1 & 1, d & 1) for d in range(1, 8)]

    def body(x_ref, o_ref, rbuf, send1, recv1, send2, recv2):
        x, y, c = lax.axis_index("x"), lax.axis_index("y"), lax.axis_index("c")
        me = 4 * x + 2 * y + c
        peers = [(x ^ fx, y ^ fy, c ^ fc) for fx, fy, fc in flips]
        pid = [4 * px + 2 * py + pc for px, py, pc in peers]

        def piece(ref, p):
            return ref.at[pl.ds(pl.multiple_of(p * pr, 8), pr), :]

        cps = []
        for d in range(7):
            cp = pltpu.make_async_remote_copy(
                src_ref=piece(x_ref, pid[d]), dst_ref=rbuf.at[d + 1],
                send_sem=send1.at[d], recv_sem=recv1.at[d], device_id=peers[d], device_id_type=MESH)
            cp.start()
            cps.append(cp)
        acc = piece(x_ref, me)[...]
        for d in range(7):
            cps[d].wait_recv()
            acc = acc + rbuf[d + 1]
        piece(o_ref, me)[...] = acc
        out = []
        for d in range(7):
            cp = pltpu.make_async_remote_copy(
                src_ref=piece(o_ref, me), dst_ref=piece(o_ref, me),
                send_sem=send2.at[d], recv_sem=recv2.at[d], device_id=peers[d], device_id_type=MESH)
            cp.start()
            out.append(cp)
        for d in range(7):
            pltpu.make_async_remote_copy(
                src_ref=piece(o_ref, pid[d]), dst_ref=piece(o_ref, pid[d]),
                send_sem=send2.at[d], recv_sem=recv2.at[d], device_id=peers[d], device_id_type=MESH).wait_recv()
        for cp in cps + out:
            cp.wait_send()

    vm = pl.BlockSpec(memory_space=pltpu.VMEM)
    return pl.pallas_call(
        body, name="small_allreduce", in_specs=[vm], out_specs=vm,
        out_shape=jax.ShapeDtypeStruct(buf.shape, F32),
        scratch_shapes=[pltpu.VMEM((8, pr, SMALL_COLS), F32)] + [pltpu.SemaphoreType.DMA((7,))] * 4,
    )(buf)


def _pack(arrays, rows):
    flat = jnp.concatenate([a.reshape(-1) for a in arrays])
    return jnp.pad(flat, (0, rows * SMALL_COLS - flat.shape[0])).reshape(rows, SMALL_COLS)


def _unpack(buf, shapes):
    flat = buf.reshape(-1)
    out, off = [], 0
    for s in shapes:
        k = math.prod(s)
        out.append(flat[off:off + k].reshape(s))
        off += k
    return out


BIG = ["sgu_w_in", "sgu_w_out", "attn_w_qkv", "attn_w_o", "ffn_w_up", "ffn_w_down"]
SMALL = ["mix_norm", "ffn_norm", "sgu_v_gain", "sgu_w_s", "sgu_b_s", "attn_q_gain", "attn_k_gain", "attn_sinks",
         "rel_bias", "ffn_conv_b"]
ORDER = ["mix_norm", "ffn_norm", "sgu_w_in", "sgu_v_gain", "sgu_w_s", "sgu_b_s", "sgu_w_out", "attn_w_qkv",
         "attn_q_gain", "attn_k_gain", "attn_sinks", "attn_w_o", "rel_bias", "ffn_w_up", "ffn_conv_w", "ffn_conv_b",
         "ffn_w_down"]


def kernel(x, mix_norm, ffn_norm, sgu_w_in, sgu_v_gain, sgu_w_s, sgu_b_s, sgu_w_out, attn_w_qkv, attn_q_gain, attn_k_gain, attn_sinks, attn_w_o, rel_bias, ffn_w_up, ffn_conv_w, ffn_conv_b, ffn_w_down, loss_target, m_mix_norm, m_ffn_norm, m_sgu_w_in, m_sgu_v_gain, m_sgu_w_s, m_sgu_b_s, m_sgu_w_out, m_attn_w_qkv, m_attn_q_gain, m_attn_k_gain, m_attn_sinks, m_attn_w_o, m_rel_bias, m_ffn_w_up, m_ffn_conv_w, m_ffn_conv_b, m_ffn_w_down, v_mix_norm, v_ffn_norm, v_sgu_w_in, v_sgu_v_gain, v_sgu_w_s, v_sgu_b_s, v_sgu_w_out, v_attn_w_qkv, v_attn_q_gain, v_attn_k_gain, v_attn_sinks, v_attn_w_o, v_rel_bias, v_ffn_w_up, v_ffn_conv_w, v_ffn_conv_b, v_ffn_w_down):
    w = dict(mix_norm=mix_norm, ffn_norm=ffn_norm, sgu_w_in=sgu_w_in, sgu_v_gain=sgu_v_gain, sgu_w_s=sgu_w_s,
             sgu_b_s=sgu_b_s, sgu_w_out=sgu_w_out, attn_w_qkv=attn_w_qkv, attn_q_gain=attn_q_gain,
             attn_k_gain=attn_k_gain, attn_sinks=attn_sinks, attn_w_o=attn_w_o, rel_bias=rel_bias, ffn_w_up=ffn_w_up,
             ffn_conv_w=ffn_conv_w, ffn_conv_b=ffn_conv_b, ffn_w_down=ffn_w_down)
    mom = dict(mix_norm=m_mix_norm, ffn_norm=m_ffn_norm, sgu_w_in=m_sgu_w_in, sgu_v_gain=m_sgu_v_gain,
               sgu_w_s=m_sgu_w_s, sgu_b_s=m_sgu_b_s, sgu_w_out=m_sgu_w_out, attn_w_qkv=m_attn_w_qkv,
               attn_q_gain=m_attn_q_gain, attn_k_gain=m_attn_k_gain, attn_sinks=m_attn_sinks, attn_w_o=m_attn_w_o,
               rel_bias=m_rel_bias, ffn_w_up=m_ffn_w_up, ffn_conv_w=m_ffn_conv_w, ffn_conv_b=m_ffn_conv_b,
               ffn_w_down=m_ffn_w_down)
    var = dict(mix_norm=v_mix_norm, ffn_norm=v_ffn_norm, sgu_w_in=v_sgu_w_in, sgu_v_gain=v_sgu_v_gain,
               sgu_w_s=v_sgu_w_s, sgu_b_s=v_sgu_b_s, sgu_w_out=v_sgu_w_out, attn_w_qkv=v_attn_w_qkv,
               attn_q_gain=v_attn_q_gain, attn_k_gain=v_attn_k_gain, attn_sinks=v_attn_sinks, attn_w_o=v_attn_w_o,
               rel_bias=v_rel_bias, ffn_w_up=v_ffn_w_up, ffn_conv_w=v_ffn_conv_w, ffn_conv_b=v_ffn_conv_b,
               ffn_w_down=v_ffn_w_down)
    chip = 2 * lax.axis_index("x") + lax.axis_index("y")
    core = lax.axis_index("c")

    shards = [sgu_w_in[0].astype(BF16), sgu_w_out[0].astype(BF16), attn_w_qkv[0].astype(BF16),
              attn_w_o[0].astype(BF16), ffn_w_up[0].astype(BF16), ffn_w_up[1].astype(BF16),
              ffn_w_down[0].astype(BF16), ffn_w_down[1].astype(BF16), ffn_conv_w[0], ffn_conv_w[1]]
    full = _gather_weights(shards, [True] * 8 + [False] * 2)
    D = x.shape[2]
    unshard_cols = lambda a: jnp.transpose(a, (1, 0, 2)).reshape(a.shape[1], -1)
    p = dict(
        mix_norm=mix_norm, ffn_norm=ffn_norm, sgu_v_gain=sgu_v_gain, sgu_w_s=sgu_w_s[0], sgu_b_s=sgu_b_s[0],
        attn_q_gain=attn_q_gain, attn_k_gain=attn_k_gain, attn_sinks=attn_sinks, rel_bias=rel_bias,
        ffn_conv_b=ffn_conv_b,
        sgu_w_in=full[0], sgu_w_out=full[1].reshape(-1, D), attn_w_qkv=full[2], attn_w_o=full[3].reshape(-1, D),
        ffn_w_up=(full[4], full[5]), ffn_w_down=(full[6].reshape(-1, D), full[7].reshape(-1, D)),
        ffn_conv_w=(unshard_cols(full[8]), unshard_cols(full[9])),
    )

    loss_local, grad_x, g = _local_grads(x[0], loss_target[0], p)
    loss = lax.psum(loss_local, ("x", "y", "c"))

    rows4 = lambda a, r: a.reshape(4, r, a.shape[-1])
    gs = [g["sgu_w_in"], g["sgu_w_out"], g["attn_w_qkv"], g["attn_w_o"], g["ffn_w_up"][0], g["ffn_w_up"][1],
          rows4(g["ffn_w_down"][0], ffn_w_down.shape[1]), rows4(g["ffn_w_down"][1], ffn_w_down.shape[1])]
    names = ["sgu_w_in", "sgu_w_out", "attn_w_qkv", "attn_w_o", "ffn_w_up0", "ffn_w_up1", "ffn_w_down0", "ffn_w_down1"]
    rsib = _pair_exchange(gs)
    cvec = core.reshape(1).astype(jnp.int32)
    ps = [_pair_add(gs[t], rsib[t], cvec, "pair_add_" + names[t]) for t in range(len(gs))]
    qs = _chip_scatter(ps)
    hs = [_sum_chips(qs[t], "sum_chips_" + names[t]) for t in range(len(gs))]
    big_grads = _half_exchange(hs, [w[k].shape for k in BIG], [(0, 0), (1, 0), (2, 0), (3, 0), (4, 0), (4, 1), (5, 0), (5, 1)])
    grads = dict(zip(BIG, big_grads))

    small_list = [g[k].reshape(w[k].shape) for k in SMALL] + [jnp.stack(g["ffn_conv_w"])]
    small_shapes = [a.shape for a in small_list]
    red = _unpack(_allreduce_small(_pack(small_list, 8 * SMALL_PIECE_ROWS)), small_shapes)
    for k, a in zip(SMALL, red[:-1]):
        grads[k] = a
    grads["ffn_conv_w"] = lax.dynamic_slice_in_dim(red[-1], chip * ffn_conv_w.shape[2], ffn_conv_w.shape[2], axis=2)

    delta, new_m, new_v = {}, {}, {}
    for k in BIG:
        two = lambda a: a.reshape(-1, a.shape[-1])
        d2, m2, v2 = _adamw(two(grads[k]), two(w[k]), two(mom[k]), two(var[k]), "adamw_" + k)
        delta[k], new_m[k], new_v[k] = (a.reshape(w[k].shape) for a in (d2, m2, v2))
    sm = SMALL + ["ffn_conv_w"]
    sm_shapes = [w[k].shape for k in sm]
    n_el = sum(math.prod(s) for s in sm_shapes)
    rows = -(-n_el // (8 * SMALL_COLS)) * 8
    d2, m2, v2 = _adamw(_pack([grads[k] for k in sm], rows), _pack([w[k] for k in sm], rows),
                        _pack([mom[k] for k in sm], rows), _pack([var[k] for k in sm], rows), "adamw_small")
    for dst, buf in ((delta, d2), (new_m, m2), (new_v, v2)):
        for k, a in zip(sm, _unpack(buf, sm_shapes)):
            dst[k] = a

    return (loss, grad_x[None], *[grads[k] for k in ORDER], *[delta[k] for k in ORDER],
            *[new_m[k] for k in ORDER], *[new_v[k] for k in ORDER])
```

```python
import functools
import math

import numpy as np
import jax
import jax.numpy as jnp
from jax import lax
from jax.experimental import pallas as pl
from jax.experimental.pallas import tpu as pltpu

F32 = jnp.float32
BF16 = jnp.bfloat16

EPS = 1e-6
CHUNK = 128
SGU_GROUPS = 16
HEAD_DIM = 64
N_HEADS = 16
N_KV_HEADS = 4
REL_BUCKETS = 32
REL_MAX_DIST = 128
LANES = 128
HALO = 16

ADAM_LR = 0.001
ADAM_B1 = 0.9
ADAM_B2 = 0.999
ADAM_EPS = 1e-08
ADAM_WD = 0.01
ADAM_STEP = 10

VMEM_LIMIT_V7X = 56 * 1024 * 1024

_SQRT_HALF = math.sqrt(0.5)
_INV_SQRT_2PI = 1.0 / math.sqrt(2.0 * math.pi)


def _cp(sem):
    return pltpu.CompilerParams(dimension_semantics=sem, vmem_limit_bytes=VMEM_LIMIT_V7X)


def _resident(shape):
    nd = len(shape)
    return pl.BlockSpec(shape, lambda *_: (0,) * nd, pipeline_mode=pl.Buffered(1))


def _dot(a, b):
    return jnp.dot(a, b, preferred_element_type=F32)


def _dot_nt(a, b):
    return lax.dot_general(a, b, (((1,), (1,)), ((), ())), preferred_element_type=F32)


def _dot_tn(a, b):
    return lax.dot_general(a, b, (((0,), (0,)), ((), ())), preferred_element_type=F32)


def _gelu(x):
    return 0.5 * x * (1.0 + lax.erf(x * _SQRT_HALF))


def _gelu_and_grad(x):
    cdf = 0.5 * (1.0 + lax.erf(x * _SQRT_HALF))
    return x * cdf, cdf + x * jnp.exp(-0.5 * x * x) * _INV_SQRT_2PI


def _rms_bwd(dy, x, gain):
    r = lax.rsqrt(jnp.mean(x * x, axis=-1, keepdims=True) + EPS)
    xhat = x * r
    gdy = dy * gain
    dx = r * (gdy - xhat * jnp.mean(gdy * xhat, axis=-1, keepdims=True))
    return dx, jnp.sum(dy * xhat, axis=0, keepdims=True)


def _norm_matmul(x, gain, w4, name, tm=512):
    S, D = x.shape
    nsh, _, ns = w4.shape
    tm = min(tm, S)

    def body(x_ref, g_ref, w_ref, hn_ref, o_ref):
        xf = x_ref[...]
        r = lax.rsqrt(jnp.mean(xf * xf, axis=-1, keepdims=True) + EPS)
        hn = (xf * r * g_ref[...]).astype(BF16)
        hn_ref[...] = hn
        for j in range(nsh):
            o_ref[:, j * ns:(j + 1) * ns] = _dot(hn, w_ref[j]).astype(BF16)

    return pl.pallas_call(
        body, name=name, grid=(S // tm,),
        in_specs=[pl.BlockSpec((tm, D), lambda i: (i, 0)), _resident((1, D)), _resident(w4.shape)],
        out_specs=[pl.BlockSpec((tm, D), lambda i: (i, 0)), pl.BlockSpec((tm, nsh * ns), lambda i: (i, 0))],
        out_shape=[jax.ShapeDtypeStruct((S, D), BF16), jax.ShapeDtypeStruct((S, nsh * ns), BF16)],
        compiler_params=_cp(("parallel",)),
    )(x, gain, w4)


def _matmul_res(a, w, res, name, tm=512):
    S, K = a.shape
    N = w.shape[1]
    tm = min(tm, S)

    def body(a_ref, w_ref, r_ref, o_ref):
        o_ref[...] = r_ref[...] + _dot(a_ref[...], w_ref[...])

    return pl.pallas_call(
        body, name=name, grid=(S // tm,),
        in_specs=[pl.BlockSpec((tm, K), lambda i: (i, 0)), _resident(w.shape), pl.BlockSpec((tm, N), lambda i: (i, 0))],
        out_specs=pl.BlockSpec((tm, N), lambda i: (i, 0)),
        out_shape=jax.ShapeDtypeStruct((S, N), F32),
        compiler_params=_cp(("parallel",)),
    )(a, w, res)


def _matmul_nt(dh, w, name, tm=512):
    S, N = dh.shape
    K = w.shape[0]
    tm = min(tm, S)

    def body(d_ref, w_ref, o_ref):
        o_ref[...] = _dot_nt(d_ref[...].astype(BF16), w_ref[...]).astype(BF16)

    return pl.pallas_call(
        body, name=name, grid=(S // tm,),
        in_specs=[pl.BlockSpec((tm, N), lambda i: (i, 0)), _resident(w.shape)],
        out_specs=pl.BlockSpec((tm, K), lambda i: (i, 0)),
        out_shape=jax.ShapeDtypeStruct((S, K), BF16),
        compiler_params=_cp(("parallel",)),
    )(dh, w)


def _matmul_tn(a, b, name, *, ka, nb, out_dtype, ts=512):
    S, KA = a.shape
    NB = b.shape[1]
    ts = min(ts, S)
    J = max(KA // ka, NB // nb)
    a_map = (lambda j, s: (s, j)) if KA // ka > 1 else (lambda j, s: (s, 0))
    b_map = (lambda j, s: (s, j)) if NB // nb > 1 else (lambda j, s: (s, 0))
    last = S // ts - 1

    def body(a_ref, b_ref, o_ref, acc):
        s = pl.program_id(1)

        @pl.when(s == 0)
        def _():
            acc[...] = jnp.zeros_like(acc)

        acc[...] += _dot_tn(a_ref[...].astype(BF16), b_ref[...].astype(BF16))

        @pl.when(s == last)
        def _():
            o_ref[0] = acc[...].astype(out_dtype)

    return pl.pallas_call(
        body, name=name, grid=(J, S // ts),
        in_specs=[pl.BlockSpec((ts, ka), a_map), pl.BlockSpec((ts, nb), b_map)],
        out_specs=pl.BlockSpec((1, ka, nb), lambda j, s: (j, 0, 0)),
        out_shape=jax.ShapeDtypeStruct((J, ka, nb), out_dtype),
        scratch_shapes=[pltpu.VMEM((ka, nb), F32)],
        compiler_params=_cp(("parallel", "arbitrary")),
    )(a, b)


def _sgu_prep(w_s):
    G = w_s.shape[0]

    def body(w_ref, t_ref, tt_ref):
        tri = lax.broadcasted_iota(jnp.int32, (CHUNK, CHUNK), 0) >= lax.broadcasted_iota(jnp.int32, (CHUNK, CHUNK), 1)
        for g in range(G):
            t = jnp.where(tri, w_ref[g], 0.0)
            t_ref[g] = t.astype(BF16)
            tt_ref[g] = t.T.astype(BF16)

    return pl.pallas_call(
        body, name="sgu_prep",
        out_shape=[jax.ShapeDtypeStruct(w_s.shape, BF16), jax.ShapeDtypeStruct(w_s.shape, BF16)],
        compiler_params=_cp(None),
    )(w_s)


def _sgu_fwd(z, x, vg, wtril, bT, wout, tm=256):
    S = z.shape[0]
    W = z.shape[1] // 2
    D = x.shape[1]
    tm = min(tm, S)

    def body(z_ref, x_ref, vg_ref, wt_ref, bT_ref, wo_ref, yp_ref, h_ref):
        def chunk(c, carry):
            r0 = pl.multiple_of(c * CHUNK, CHUNK)
            zc = z_ref[pl.ds(r0, CHUNK), :].astype(F32)
            u = _gelu(zc[:, :W])
            v = _gelu(zc[:, W:])
            rv = lax.rsqrt(jnp.mean(v * v, axis=-1, keepdims=True) + EPS)
            vn = (v * rv * vg_ref[...]).astype(BF16)
            for g in range(SGU_GROUPS):
                sl = slice(g * LANES, (g + 1) * LANES)
                s = _dot(wt_ref[g], vn[:, sl]) + bT_ref[:, g:g + 1]
                yp_ref[pl.ds(r0, CHUNK), sl] = (u[:, sl] * s).astype(BF16)
            return carry

        lax.fori_loop(0, tm // CHUNK, chunk, 0)
        h_ref[...] = x_ref[...] + _dot(yp_ref[...], wo_ref[...])

    return pl.pallas_call(
        body, name="sgu_fwd", grid=(S // tm,),
        in_specs=[pl.BlockSpec((tm, 2 * W), lambda i: (i, 0)), pl.BlockSpec((tm, D), lambda i: (i, 0)),
                  _resident((1, W)), _resident(wtril.shape), _resident(bT.shape), _resident(wout.shape)],
        out_specs=[pl.BlockSpec((tm, W), lambda i: (i, 0)), pl.BlockSpec((tm, D), lambda i: (i, 0))],
        out_shape=[jax.ShapeDtypeStruct((S, W), BF16), jax.ShapeDtypeStruct((S, D), F32)],
        compiler_params=_cp(("parallel",)),
    )(z, x, vg, wtril, bT, wout)


def _sgu_bwd(dh, z, vg, wtril, wtrilT, bT, wout, tm=256):
    S = z.shape[0]
    W = z.shape[1] // 2
    D = dh.shape[1]
    G = SGU_GROUPS
    tm = min(tm, S)
    last = S // tm - 1

    def body(dh_ref, z_ref, vg_ref, wt_ref, wtT_ref, bT_ref, wo_ref,
             dz_ref, dws_ref, dbT_ref, dvg_ref, dyp_s, du_s, dvn_s, dsacc):
        i = pl.program_id(0)

        @pl.when(i == 0)
        def _():
            dws_ref[...] = jnp.zeros_like(dws_ref)
            dvg_ref[...] = jnp.zeros_like(dvg_ref)
            dsacc[...] = jnp.zeros_like(dsacc)

        dyp_s[...] = _dot_nt(dh_ref[...].astype(BF16), wo_ref[...])
        tri = lax.broadcasted_iota(jnp.int32, (CHUNK, CHUNK), 0) >= lax.broadcasted_iota(jnp.int32, (CHUNK, CHUNK), 1)

        def chunk(c, carry):
            r0 = pl.multiple_of(c * CHUNK, CHUNK)
            zc = z_ref[pl.ds(r0, CHUNK), :].astype(F32)
            u, gu = _gelu_and_grad(zc[:, :W])
            v, gv = _gelu_and_grad(zc[:, W:])
            rv = lax.rsqrt(jnp.mean(v * v, axis=-1, keepdims=True) + EPS)
            vhat = v * rv
            vgain = vg_ref[...]
            vn = (vhat * vgain).astype(BF16)
            dyp = dyp_s[pl.ds(r0, CHUNK), :]
            for g in range(G):
                sl = slice(g * LANES, (g + 1) * LANES)
                vng = vn[:, sl]
                s = _dot(wt_ref[g], vng) + bT_ref[:, g:g + 1]
                ds = dyp[:, sl] * u[:, sl]
                du_s[:, sl] = dyp[:, sl] * s
                dsb = ds.astype(BF16)
                dvn_s[:, sl] = _dot(wtT_ref[g], dsb)
                dws_ref[g] += jnp.where(tri, _dot_nt(dsb, vng), 0.0)
                dsacc[g] += ds
            dvn = dvn_s[...]
            dvg_ref[...] += jnp.sum(dvn * vhat, axis=0, keepdims=True)
            gdy = dvn * vgain
            dv = rv * (gdy - vhat * jnp.mean(gdy * vhat, axis=-1, keepdims=True))
            dz_ref[pl.ds(r0, CHUNK), :W] = (du_s[...] * gu).astype(BF16)
            dz_ref[pl.ds(r0, CHUNK), W:] = (dv * gv).astype(BF16)
            return carry

        lax.fori_loop(0, tm // CHUNK, chunk, 0)

        @pl.when(i == last)
        def _():
            for g in range(G):
                dbT_ref[:, g:g + 1] = jnp.sum(dsacc[g], axis=1, keepdims=True)

    return pl.pallas_call(
        body, name="sgu_bwd", grid=(S // tm,),
        in_specs=[pl.BlockSpec((tm, D), lambda i: (i, 0)), pl.BlockSpec((tm, 2 * W), lambda i: (i, 0)),
                  _resident((1, W)), _resident(wtril.shape), _resident(wtrilT.shape), _resident(bT.shape),
                  _resident(wout.shape)],
        out_specs=[pl.BlockSpec((tm, 2 * W), lambda i: (i, 0)),
                   pl.BlockSpec((G, CHUNK, CHUNK), lambda i: (0, 0, 0)),
                   pl.BlockSpec((CHUNK, G), lambda i: (0, 0)),
                   pl.BlockSpec((1, W), lambda i: (0, 0))],
        out_shape=[jax.ShapeDtypeStruct((S, 2 * W), BF16), jax.ShapeDtypeStruct((G, CHUNK, CHUNK), F32),
                   jax.ShapeDtypeStruct((CHUNK, G), F32), jax.ShapeDtypeStruct((1, W), F32)],
        scratch_shapes=[pltpu.VMEM((tm, W), F32), pltpu.VMEM((CHUNK, W), F32), pltpu.VMEM((CHUNK, W), F32),
                        pltpu.VMEM((G, CHUNK, CHUNK), F32)],
        compiler_params=_cp(("arbitrary",)),
    )(dh, z, vg, wtril, wtrilT, bT, wout)


def _conv_taps(a32, r0, R):
    X = a32[pl.ds(r0 + HALO - 8, R + 8), :]
    return X[8:], pltpu.roll(X, 1, 0)[8:], pltpu.roll(X, 2, 0)[8:]


def _ffn_fwd(a, h_in, cw, cb, wdown, name, tm=256, R=64):
    S, C = a.shape
    F = C // 2
    D = h_in.shape[1]
    tm = min(tm, S)

    def body(a_ref, halo_ref, h_ref, cw_ref, cb_ref, wd_ref, f_ref, ho_ref, a32):
        i = pl.program_id(0)
        a32[0:HALO, :] = jnp.where(i > 0, halo_ref[...].astype(F32), 0.0)
        a32[HALO:, :] = a_ref[...].astype(F32)

        def sub(k, carry):
            r0 = pl.multiple_of(k * R, R)
            a0, a1, a2 = _conv_taps(a32, r0, R)
            cpre = cw_ref[0:1, :] * a2 + cw_ref[1:2, :] * a1 + cw_ref[2:3, :] * a0 + cb_ref[...]
            g = cpre[:, :F]
            f_ref[pl.ds(r0, R), :] = (g * jax.nn.sigmoid(g) * cpre[:, F:]).astype(BF16)
            return carry

        lax.fori_loop(0, tm // R, sub, 0)
        ho_ref[...] = h_ref[...] + _dot(f_ref[...], wd_ref[...])

    hb = tm // HALO
    return pl.pallas_call(
        body, name=name, grid=(S // tm,),
        in_specs=[pl.BlockSpec((tm, C), lambda i: (i, 0)),
                  pl.BlockSpec((HALO, C), lambda i: (jnp.maximum(i * hb - 1, 0), 0)),
                  pl.BlockSpec((tm, D), lambda i: (i, 0)),
                  _resident((3, C)), _resident((1, C)), _resident(wdown.shape)],
        out_specs=[pl.BlockSpec((tm, F), lambda i: (i, 0)), pl.BlockSpec((tm, D), lambda i: (i, 0))],
        out_shape=[jax.ShapeDtypeStruct((S, F), BF16), jax.ShapeDtypeStruct((S, D), F32)],
        scratch_shapes=[pltpu.VMEM((HALO + tm, C), F32)],
        compiler_params=_cp(("parallel",)),
    )(a, a, h_in, cw, cb, wdown)


def _ffn_bwd_dc(dh, a, cw, cb, wdown, name, tm=256, R=64):
    S, C = a.shape
    F = C // 2
    D = dh.shape[1]
    tm = min(tm, S)

    def body(dh_ref, a_ref, halo_ref, cw_ref, cb_ref, wd_ref, dc_ref, dcw_ref, dcb_ref, a32, df_s):
        i = pl.program_id(0)

        @pl.when(i == 0)
        def _():
            dcw_ref[...] = jnp.zeros_like(dcw_ref)
            dcb_ref[...] = jnp.zeros_like(dcb_ref)

        a32[0:HALO, :] = jnp.where(i > 0, halo_ref[...].astype(F32), 0.0)
        a32[HALO:, :] = a_ref[...].astype(F32)
        df_s[...] = _dot_nt(dh_ref[...].astype(BF16), wd_ref[...])

        def sub(k, carry):
            r0 = pl.multiple_of(k * R, R)
            a0, a1, a2 = _conv_taps(a32, r0, R)
            cpre = cw_ref[0:1, :] * a2 + cw_ref[1:2, :] * a1 + cw_ref[2:3, :] * a0 + cb_ref[...]
            g = cpre[:, :F]
            val = cpre[:, F:]
            sg = jax.nn.sigmoid(g)
            df = df_s[pl.ds(r0, R), :]
            dg = df * val * (sg * (1.0 + g * (1.0 - sg)))
            dval = df * (g * sg)
            dc = jnp.concatenate([dg, dval], axis=1)
            dc_ref[pl.ds(r0, R), :] = dc.astype(BF16)
            dcw_ref[0:1, :] += jnp.sum(dc * a2, axis=0, keepdims=True)
            dcw_ref[1:2, :] += jnp.sum(dc * a1, axis=0, keepdims=True)
            dcw_ref[2:3, :] += jnp.sum(dc * a0, axis=0, keepdims=True)
            dcb_ref[...] += jnp.sum(dc, axis=0, keepdims=True)
            return carry

        lax.fori_loop(0, tm // R, sub, 0)

    hb = tm // HALO
    return pl.pallas_call(
        body, name=name, grid=(S // tm,),
        in_specs=[pl.BlockSpec((tm, D), lambda i: (i, 0)),
                  pl.BlockSpec((tm, C), lambda i: (i, 0)),
                  pl.BlockSpec((HALO, C), lambda i: (jnp.maximum(i * hb - 1, 0), 0)),
                  _resident((3, C)), _resident((1, C)), _resident(wdown.shape)],
        out_specs=[pl.BlockSpec((tm, C), lambda i: (i, 0)),
                   pl.BlockSpec((3, C), lambda i: (0, 0)), pl.BlockSpec((1, C), lambda i: (0, 0))],
        out_shape=[jax.ShapeDtypeStruct((S, C), BF16), jax.ShapeDtypeStruct((3, C), F32),
                   jax.ShapeDtypeStruct((1, C), F32)],
        scratch_shapes=[pltpu.VMEM((HALO + tm, C), F32), pltpu.VMEM((tm, F), F32)],
        compiler_params=_cp(("arbitrary",)),
    )(dh, a, a, cw, cb, wdown)


def _bwd_norm(dA, w4, h_in, gain, dh_out, name, conv_w=None, tm=256, R=64):
    S, N = dA.shape
    nsh, D, ns = w4.shape
    tm = min(tm, S)
    nt = S // tm
    conv = conv_w is not None

    def finish(src_ref, w_ref, h_ref, g_ref, dho_ref, dhi_ref, dg_ref):
        dhn = _dot_nt(src_ref[:, 0:ns], w_ref[0])
        for j in range(1, nsh):
            dhn += _dot_nt(src_ref[:, j * ns:(j + 1) * ns], w_ref[j])
        dx, dgain = _rms_bwd(dhn, h_ref[...], g_ref[...])
        dg_ref[...] += dgain
        dhi_ref[...] = dho_ref[...] + dx

    def body_plain(dA_ref, w_ref, h_ref, g_ref, dho_ref, dhi_ref, dg_ref):
        @pl.when(pl.program_id(0) == 0)
        def _():
            dg_ref[...] = jnp.zeros_like(dg_ref)

        finish(dA_ref, w_ref, h_ref, g_ref, dho_ref, dhi_ref, dg_ref)

    def body_conv(dc_ref, halo_ref, cw_ref, w_ref, h_ref, g_ref, dho_ref, dhi_ref, dg_ref, da_ref, dc32):
        i = pl.program_id(0)

        @pl.when(i == 0)
        def _():
            dg_ref[...] = jnp.zeros_like(dg_ref)

        dc32[0:tm, :] = dc_ref[...].astype(F32)
        dc32[tm:, :] = jnp.where(i < nt - 1, halo_ref[...].astype(F32), 0.0)

        def sub(k, carry):
            r0 = pl.multiple_of(k * R, R)
            X = dc32[pl.ds(r0, R + 8), :]
            d0 = X[:R]
            d1 = pltpu.roll(X, R + 7, 0)[:R]
            d2 = pltpu.roll(X, R + 6, 0)[:R]
            da = cw_ref[2:3, :] * d0 + cw_ref[1:2, :] * d1 + cw_ref[0:1, :] * d2
            da_ref[pl.ds(r0, R), :] = da.astype(BF16)
            return carry

        lax.fori_loop(0, tm // R, sub, 0)
        finish(da_ref, w_ref, h_ref, g_ref, dho_ref, dhi_ref, dg_ref)

    row = lambda width: pl.BlockSpec((tm, width), lambda i: (i, 0))
    common_in = [_resident(w4.shape), row(D), _resident((1, D)), row(D)]
    common_out = [row(D), pl.BlockSpec((1, D), lambda i: (0, 0))]
    common_shape = [jax.ShapeDtypeStruct((S, D), F32), jax.ShapeDtypeStruct((1, D), F32)]
    if not conv:
        return pl.pallas_call(
            body_plain, name=name, grid=(nt,),
            in_specs=[row(N)] + common_in, out_specs=common_out, out_shape=common_shape,
            compiler_params=_cp(("arbitrary",)),
        )(dA, w4, h_in, gain, dh_out)
    hb = tm // HALO
    nhb = S // HALO
    return pl.pallas_call(
        body_conv, name=name, grid=(nt,),
        in_specs=[row(N), pl.BlockSpec((HALO, N), lambda i: (jnp.minimum((i + 1) * hb, nhb - 1), 0)),
                  _resident((3, N))] + common_in,
        out_specs=common_out + [row(N)],
        out_shape=common_shape + [jax.ShapeDtypeStruct((S, N), BF16)],
        scratch_shapes=[pltpu.VMEM((tm + HALO, N), F32)],
        compiler_params=_cp(("arbitrary",)),
    )(dA, dA, conv_w, w4, h_in, gain, dh_out)


def _rel_buckets_flat():
    q = np.arange(CHUNK)[:, None] + CHUNK
    k = np.arange(2 * CHUNK)[None, :]
    n = np.maximum(q - k, 0)
    max_exact = REL_BUCKETS // 2
    large = max_exact + (np.log(np.maximum(n, 1).astype(np.float32) / max_exact)
                         / math.log(REL_MAX_DIST / max_exact) * (REL_BUCKETS - max_exact)).astype(np.int32)
    large = np.minimum(large, REL_BUCKETS - 1)
    return np.where(n < max_exact, n, large).astype(np.int32).reshape(1, CHUNK * 2 * CHUNK)


def _split_bf16(x):
    hi = x.astype(BF16)
    return hi, (x - hi.astype(F32)).astype(BF16)


def _rel_bias_expand(rel_bias, bucket):
    B, H = rel_bias.shape
    n = bucket.shape[1]

    def body(rb_ref, bk_ref, o_ref):
        oh = (bk_ref[...] == lax.broadcasted_iota(jnp.int32, (B, n), 0)).astype(BF16)
        hi, lo = _split_bf16(rb_ref[...])
        o_ref[...] = _dot_tn(hi, oh) + _dot_tn(lo, oh)

    return pl.pallas_call(body, name="rel_bias_expand", out_shape=jax.ShapeDtypeStruct((H, n), F32),
                          compiler_params=_cp(None))(rel_bias, bucket)


def _rel_bias_reduce(dbias, bucket):
    H, n = dbias.shape
    B = REL_BUCKETS

    def body(db_ref, bk_ref, o_ref):
        oh = (bk_ref[...] == lax.broadcasted_iota(jnp.int32, (B, n), 0)).astype(BF16)
        hi, lo = _split_bf16(db_ref[...])
        o_ref[...] = _dot_nt(oh, hi) + _dot_nt(oh, lo)

    return pl.pallas_call(body, name="rel_bias_reduce", out_shape=jax.ShapeDtypeStruct((B, H), F32),
                          compiler_params=_cp(None))(dbias, bucket)


def _lo_mask(rows):
    return lax.broadcasted_iota(jnp.int32, (rows, LANES), 1) < HEAD_DIM


def _half_sums(y, lo):
    s_lo = jnp.sum(jnp.where(lo, y, 0.0), axis=-1, keepdims=True)
    s_hi = jnp.sum(jnp.where(lo, 0.0, y), axis=-1, keepdims=True)
    return jnp.where(lo, s_lo, s_hi)


def _half_rms(x, gain, lo):
    r = lax.rsqrt(_half_sums(x * x, lo) * (1.0 / HEAD_DIM) + EPS)
    xhat = x * r
    return xhat * gain, xhat, r


def _half_rms_bwd(dy, xhat, r, gain, lo):
    gdy = dy * gain
    dx = r * (gdy - xhat * (_half_sums(gdy * xhat, lo) * (1.0 / HEAD_DIM)))
    return dx, jnp.sum(dy * xhat, axis=0, keepdims=True)


def _dup_half(pair, e, lo):
    sw = pltpu.roll(pair, HEAD_DIM, 1)
    return jnp.where(lo, pair, sw) if e == 0 else jnp.where(lo, sw, pair)


def _band_valid(n):
    qi = lax.broadcasted_iota(jnp.int32, (CHUNK, 2 * CHUNK), 0)
    ki = lax.broadcasted_iota(jnp.int32, (CHUNK, 2 * CHUNK), 1)
    dist = qi + CHUNK - ki
    return (dist >= 0) & (dist < CHUNK) & ((n > 0) | (ki >= CHUNK))


def _softmax_sink(s, valid, sink):
    s = jnp.where(valid, s, -jnp.inf)
    m = jnp.maximum(jnp.max(s, axis=-1, keepdims=True), sink)
    p = jnp.exp(s - m)
    esink = jnp.exp(sink - m)
    inv = 1.0 / (jnp.sum(p, axis=-1, keepdims=True) + esink)
    return p * inv, esink * inv


QW = N_HEADS * HEAD_DIM
KVW = N_KV_HEADS * HEAD_DIM


def _attn_fwd(qkv, qg2, kg2, sinks, bias):
    S = qkv.shape[0]
    nb = S // CHUNK

    def body(cur_ref, prev_ref, qg_ref, kg_ref, sink_ref, bias_ref, o_ref):
        n = pl.program_id(0)
        lo = _lo_mask(CHUNK)
        lo2 = _lo_mask(2 * CHUNK)
        valid = _band_valid(n)
        for j in range(N_KV_HEADS // 2):
            kc = slice(QW + j * LANES, QW + (j + 1) * LANES)
            vc = slice(QW + KVW + j * LANES, QW + KVW + (j + 1) * LANES)
            kpair = jnp.concatenate([prev_ref[:, j * LANES:(j + 1) * LANES], cur_ref[:, kc]], axis=0).astype(F32)
            vpair = jnp.concatenate([prev_ref[:, KVW + j * LANES:KVW + (j + 1) * LANES], cur_ref[:, vc]], axis=0).astype(F32)
            knpair, _, _ = _half_rms(kpair, kg_ref[...], lo2)
            for e in range(2):
                hk = 2 * j + e
                kdup = _dup_half(knpair, e, lo2).astype(BF16)
                vdup = _dup_half(vpair, e, lo2).astype(BF16)
                for i2 in range(2):
                    pi = 2 * hk + i2
                    qpair = cur_ref[:, pi * LANES:(pi + 1) * LANES].astype(F32)
                    qn, _, _ = _half_rms(qpair, qg_ref[...], lo)
                    outs = []
                    for par in range(2):
                        h = 2 * pi + par
                        hm = lo if par == 0 else jnp.logical_not(lo)
                        qm = jnp.where(hm, qn, 0.0).astype(BF16)
                        s = _dot_nt(qm, kdup) * (HEAD_DIM ** -0.5) + bias_ref[h]
                        p, _ = _softmax_sink(s, valid, sink_ref[h])
                        outs.append(_dot(p.astype(BF16), vdup))
                    o_ref[:, pi * LANES:(pi + 1) * LANES] = jnp.where(lo, outs[0], outs[1]).astype(BF16)

    return pl.pallas_call(
        body, name="attn_fwd", grid=(nb,),
        in_specs=[pl.BlockSpec((CHUNK, QW + 2 * KVW), lambda n: (n, 0)),
                  pl.BlockSpec((CHUNK, 2 * KVW), lambda n: (jnp.maximum(n - 1, 0), QW // (2 * KVW))),
                  _resident((1, LANES)), _resident((1, LANES)),
                  pl.BlockSpec(memory_space=pltpu.SMEM),
                  _resident(bias.shape)],
        out_specs=pl.BlockSpec((CHUNK, QW), lambda n: (n, 0)),
        out_shape=jax.ShapeDtypeStruct((S, QW), BF16),
        compiler_params=_cp(("parallel",)),
    )(qkv, qkv, qg2, kg2, sinks, bias)


def _attn_bwd(qkv, do, qg2, kg2, sinks, bias):
    S = qkv.shape[0]
    nb = S // CHUNK
    H = N_HEADS

    def body(cur_ref, prev_ref, do_ref, qg_ref, kg_ref, sink_ref, bias_ref,
             dqkv_ref, dbias_ref, dqg_ref, dkg_ref, dsink_ref, carry, band, dsacc, gacc):
        i = pl.program_id(0)
        n = nb - 1 - i
        lo = _lo_mask(CHUNK)
        lo2 = _lo_mask(2 * CHUNK)
        lane = lax.broadcasted_iota(jnp.int32, (CHUNK, LANES), 1)
        valid = _band_valid(n)

        @pl.when(i == 0)
        def _():
            dbias_ref[...] = jnp.zeros_like(dbias_ref)
            carry[...] = jnp.zeros_like(carry)
            dsacc[...] = jnp.zeros_like(dsacc)
            gacc[...] = jnp.zeros_like(gacc)

        qgain = qg_ref[...]
        kgain = kg_ref[...]
        for j in range(N_KV_HEADS // 2):
            kc = slice(QW + j * LANES, QW + (j + 1) * LANES)
            vc = slice(QW + KVW + j * LANES, QW + KVW + (j + 1) * LANES)
            kpair = jnp.concatenate([prev_ref[:, j * LANES:(j + 1) * LANES], cur_ref[:, kc]], axis=0).astype(F32)
            vpair = jnp.concatenate([prev_ref[:, KVW + j * LANES:KVW + (j + 1) * LANES], cur_ref[:, vc]], axis=0).astype(F32)
            knpair, khat, kr = _half_rms(kpair, kgain, lo2)
            dk_folds = []
            dv_folds = []
            for e in range(2):
                hk = 2 * j + e
                kdup = _dup_half(knpair, e, lo2).astype(BF16)
                vdup = _dup_half(vpair, e, lo2).astype(BF16)
                dkd = jnp.zeros((2 * CHUNK, LANES), F32)
                dvd = jnp.zeros((2 * CHUNK, LANES), F32)
                for i2 in range(2):
                    pi = 2 * hk + i2
                    qpair = cur_ref[:, pi * LANES:(pi + 1) * LANES].astype(F32)
                    qn, qhat, qr = _half_rms(qpair, qgain, lo)
                    dopair = do_ref[:, pi * LANES:(pi + 1) * LANES].astype(F32)
                    dqn = jnp.zeros((CHUNK, LANES), F32)
                    for par in range(2):
                        h = 2 * pi + par
                        hm = lo if par == 0 else jnp.logical_not(lo)
                        qm = jnp.where(hm, qn, 0.0).astype(BF16)
                        s = _dot_nt(qm, kdup) * (HEAD_DIM ** -0.5) + bias_ref[h]
                        p, psink = _softmax_sink(s, valid, sink_ref[h])
                        dom = jnp.where(hm, dopair, 0.0).astype(BF16)
                        dp = _dot_nt(dom, vdup)
                        delta = jnp.sum(p * dp, axis=-1, keepdims=True)
                        ds = p * (dp - delta)
                        dbias_ref[h] += ds
                        dsacc[...] += jnp.where(lane == h, -(psink * delta), 0.0)
                        dsr = (ds * (HEAD_DIM ** -0.5)).astype(BF16)
                        dqn += jnp.where(hm, _dot(dsr, kdup), 0.0)
                        dkd += _dot_tn(dsr, qm)
                        dvd += _dot_tn(p.astype(BF16), dom)
                    dq, dqg = _half_rms_bwd(dqn, qhat, qr, qgain, lo)
                    gacc[0:1, :] += dqg
                    dqkv_ref[:, pi * LANES:(pi + 1) * LANES] = dq.astype(BF16)
                dk_folds.append(dkd + pltpu.roll(dkd, HEAD_DIM, 1))
                dv_folds.append(dvd + pltpu.roll(dvd, HEAD_DIM, 1))
            dkn = jnp.where(lo2, dk_folds[0], dk_folds[1])
            dk, dkg = _half_rms_bwd(dkn, khat, kr, kgain, lo2)
            gacc[1:2, :] += dkg
            band[:, j * LANES:(j + 1) * LANES] = dk
            band[:, KVW + j * LANES:KVW + (j + 1) * LANES] = jnp.where(lo2, dv_folds[0], dv_folds[1])
        dqkv_ref[:, QW:] = (band[CHUNK:, :] + carry[...]).astype(BF16)
        carry[...] = band[0:CHUNK, :]

        @pl.when(i == nb - 1)
        def _():
            g = gacc[...]
            g = g + pltpu.roll(g, HEAD_DIM, 1)
            dqg_ref[...] = g[0:1, :]
            dkg_ref[...] = g[1:2, :]
            dsink_ref[...] = jnp.sum(dsacc[...], axis=0, keepdims=True)

    vec = pl.BlockSpec((1, LANES), lambda i: (0, 0))
    return pl.pallas_call(
        body, name="attn_bwd", grid=(nb,),
        in_specs=[pl.BlockSpec((CHUNK, QW + 2 * KVW), lambda i: (nb - 1 - i, 0)),
                  pl.BlockSpec((CHUNK, 2 * KVW), lambda i: (jnp.maximum(nb - 2 - i, 0), QW // (2 * KVW))),
                  pl.BlockSpec((CHUNK, QW), lambda i: (nb - 1 - i, 0)),
                  _resident((1, LANES)), _resident((1, LANES)),
                  pl.BlockSpec(memory_space=pltpu.SMEM),
                  _resident(bias.shape)],
        out_specs=[pl.BlockSpec((CHUNK, QW + 2 * KVW), lambda i: (nb - 1 - i, 0)),
                   pl.BlockSpec((H, CHUNK, 2 * CHUNK), lambda i: (0, 0, 0)), vec, vec, vec],
        out_shape=[jax.ShapeDtypeStruct((S, QW + 2 * KVW), BF16),
                   jax.ShapeDtypeStruct((H, CHUNK, 2 * CHUNK), F32),
                   jax.ShapeDtypeStruct((1, LANES), F32), jax.ShapeDtypeStruct((1, LANES), F32),
                   jax.ShapeDtypeStruct((1, LANES), F32)],
        scratch_shapes=[pltpu.VMEM((CHUNK, 2 * KVW), F32), pltpu.VMEM((2 * CHUNK, 2 * KVW), F32),
                        pltpu.VMEM((CHUNK, LANES), F32), pltpu.VMEM((8, LANES), F32)],
        compiler_params=_cp(("arbitrary",)),
    )(qkv, qkv, do, qg2, kg2, sinks, bias)


def _loss_head(y, target, tm=512):
    S, D = y.shape
    tm = min(tm, S)

    def body(y_ref, t_ref, dy_ref, l_ref):
        @pl.when(pl.program_id(0) == 0)
        def _():
            l_ref[...] = jnp.zeros_like(l_ref)

        e = y_ref[...] - t_ref[...]
        dy_ref[...] = e * (1.0 / D)
        rows = jnp.sum(e * e, axis=-1, keepdims=True) * (1.0 / D)
        l_ref[...] += 0.5 * jnp.sum(rows, axis=0, keepdims=True)

    dy, l = pl.pallas_call(
        body, name="loss_head", grid=(S // tm,),
        in_specs=[pl.BlockSpec((tm, D), lambda i: (i, 0)), pl.BlockSpec((tm, D), lambda i: (i, 0))],
        out_specs=[pl.BlockSpec((tm, D), lambda i: (i, 0)), pl.BlockSpec((1, 1), lambda i: (0, 0))],
        out_shape=[jax.ShapeDtypeStruct((S, D), F32), jax.ShapeDtypeStruct((1, 1), F32)],
        compiler_params=_cp(("arbitrary",)),
    )(y, target)
    return dy, l[0, 0]


def _local_grads(x, target, p):
    D = x.shape[1]
    g = {}
    bucket = jnp.asarray(_rel_buckets_flat())
    wtril, wtrilT = _sgu_prep(p["sgu_w_s"])
    bT = p["sgu_b_s"].T
    bias = _rel_bias_expand(p["rel_bias"], bucket).reshape(N_HEADS, CHUNK, 2 * CHUNK)
    qg2 = jnp.tile(p["attn_q_gain"], (1, 2))
    kg2 = jnp.tile(p["attn_k_gain"], (1, 2))
    sinks = p["attn_sinks"].reshape(N_HEADS)
    mix0, mix1 = p["mix_norm"][0:1], p["mix_norm"][1:2]
    fn0, fn1 = p["ffn_norm"][0:1], p["ffn_norm"][1:2]

    hn0, z = _norm_matmul(x, mix0, p["sgu_w_in"], "sgu_in")
    yp, h1 = _sgu_fwd(z, x, p["sgu_v_gain"], wtril, bT, p["sgu_w_out"])
    hf0, a0 = _norm_matmul(h1, fn0, p["ffn_w_up"][0], "ffn0_up")
    f0, h2 = _ffn_fwd(a0, h1, p["ffn_conv_w"][0], p["ffn_conv_b"][0:1], p["ffn_w_down"][0], "ffn0_fwd")
    hn1, qkv = _norm_matmul(h2, mix1, p["attn_w_qkv"], "attn_qkv")
    o = _attn_fwd(qkv, qg2, kg2, sinks, bias)
    h3 = _matmul_res(o, p["attn_w_o"], h2, "attn_out")
    hf1, a1 = _norm_matmul(h3, fn1, p["ffn_w_up"][1], "ffn1_up")
    f1, h4 = _ffn_fwd(a1, h3, p["ffn_conv_w"][1], p["ffn_conv_b"][1:2], p["ffn_w_down"][1], "ffn1_fwd")
    dh4, loss = _loss_head(h4, target)

    nup = p["ffn_w_up"][0].shape[2]
    dc1, dcw1, dcb1 = _ffn_bwd_dc(dh4, a1, p["ffn_conv_w"][1], p["ffn_conv_b"][1:2], p["ffn_w_down"][1], "ffn1_bwd_dc")
    dh3, dfn1, da1 = _bwd_norm(dc1, p["ffn_w_up"][1], h3, fn1, dh4, "ffn1_bwd_in", conv_w=p["ffn_conv_w"][1])
    g_wdown1 = _matmul_tn(f1, dh4, "ffn1_dw_down", ka=f1.shape[1] // 2, nb=D, out_dtype=BF16)
    g_wup1 = _matmul_tn(hf1, da1, "ffn1_dw_up", ka=D, nb=nup, out_dtype=BF16)
    do = _matmul_nt(dh3, p["attn_w_o"], "attn_bwd_out")
    dqkv, dbias, dqg, dkg, dsink = _attn_bwd(qkv, do, qg2, kg2, sinks, bias)
    dh2, dmix1 = _bwd_norm(dqkv, p["attn_w_qkv"], h2, mix1, dh3, "attn_bwd_in")
    g["attn_w_o"] = _matmul_tn(o, dh3, "attn_dw_o", ka=o.shape[1] // 4, nb=D, out_dtype=BF16)
    g["attn_w_qkv"] = _matmul_tn(hn1, dqkv, "attn_dw_qkv", ka=D, nb=dqkv.shape[1] // 4, out_dtype=BF16)
    g["rel_bias"] = _rel_bias_reduce(dbias.reshape(N_HEADS, CHUNK * 2 * CHUNK), bucket)
    g["attn_q_gain"] = dqg[:, :HEAD_DIM]
    g["attn_k_gain"] = dkg[:, :HEAD_DIM]
    g["attn_sinks"] = dsink[:, :N_HEADS]
    dc0, dcw0, dcb0 = _ffn_bwd_dc(dh2, a0, p["ffn_conv_w"][0], p["ffn_conv_b"][0:1], p["ffn_w_down"][0], "ffn0_bwd_dc")
    dh1, dfn0, da0 = _bwd_norm(dc0, p["ffn_w_up"][0], h1, fn0, dh2, "ffn0_bwd_in", conv_w=p["ffn_conv_w"][0])
    g_wdown0 = _matmul_tn(f0, dh2, "ffn0_dw_down", ka=f0.shape[1] // 2, nb=D, out_dtype=BF16)
    g_wup0 = _matmul_tn(hf0, da0, "ffn0_dw_up", ka=D, nb=nup, out_dtype=BF16)
    dz, dws, dbT, dvg = _sgu_bwd(dh1, z, p["sgu_v_gain"], wtril, wtrilT, bT, p["sgu_w_out"])
    dx, dmix0 = _bwd_norm(dz, p["sgu_w_in"], x, mix0, dh1, "sgu_bwd_in")
    g["sgu_w_out"] = _matmul_tn(yp, dh1, "sgu_dw_out", ka=yp.shape[1] // 4, nb=D, out_dtype=BF16)
    g["sgu_w_in"] = _matmul_tn(hn0, dz, "sgu_dw_in", ka=D, nb=dz.shape[1] // 4, out_dtype=BF16)
    g["sgu_w_s"] = dws
    g["sgu_b_s"] = dbT.T
    g["sgu_v_gain"] = dvg
    g["mix_norm"] = jnp.concatenate([dmix0, dmix1], axis=0)
    g["ffn_norm"] = jnp.concatenate([dfn0, dfn1], axis=0)
    g["ffn_w_up"] = (g_wup0, g_wup1)
    g["ffn_w_down"] = (g_wdown0, g_wdown1)
    g["ffn_conv_w"] = (dcw0, dcw1)
    g["ffn_conv_b"] = jnp.concatenate([dcb0, dcb1], axis=0)
    return loss, dx, g


def _row_tile(rows, cols, n_arrays):
    budget = VMEM_LIMIT_V7X // 4 // (2 * n_arrays * 4 * cols)
    best = 8
    for n in range(1, rows // 8 + 1):
        if rows % n == 0 and (rows // n) % 8 == 0 and rows // n <= budget:
            best = rows // n
            break
    return best


def _adamw(g, w, m, v, name):
    R, C = g.shape
    tr = _row_tile(R, C, 7)

    def body(g_ref, w_ref, m_ref, v_ref, d_ref, mo_ref, vo_ref):
        gg = g_ref[...]
        mn = ADAM_B1 * m_ref[...] + (1.0 - ADAM_B1) * gg
        vn = ADAM_B2 * v_ref[...] + (1.0 - ADAM_B2) * jnp.square(gg)
        m_hat = mn / (1.0 - ADAM_B1 ** ADAM_STEP)
        v_hat = vn / (1.0 - ADAM_B2 ** ADAM_STEP)
        d_ref[...] = -ADAM_LR * (m_hat / (jnp.sqrt(v_hat) + ADAM_EPS) + ADAM_WD * w_ref[...])
        mo_ref[...] = mn
        vo_ref[...] = vn

    spec = pl.BlockSpec((tr, C), lambda i: (i, 0))
    return pl.pallas_call(
        body, name=name, grid=(R // tr,), in_specs=[spec] * 4, out_specs=[spec] * 3,
        out_shape=[jax.ShapeDtypeStruct((R, C), F32)] * 3, compiler_params=_cp(("parallel",)),
    )(g, w, m, v)


def _place_shard(shard, where, dtype, name):
    R, C = shard.shape
    tr = _row_tile(R, C, 2) if R % 8 == 0 else R

    def body(s_ref, x_ref, o_ref):
        o_ref[0] = x_ref[...].astype(dtype)

    return pl.pallas_call(
        body, name=name,
        grid_spec=pltpu.PrefetchScalarGridSpec(
            num_scalar_prefetch=1, grid=(R // tr,),
            in_specs=[pl.BlockSpec((tr, C), lambda i, s_ref: (i, 0))],
            out_specs=pl.BlockSpec((1, tr, C), lambda i, s_ref: (s_ref[1], i, 0))),
        out_shape=jax.ShapeDtypeStruct((4, R, C), dtype),
        compiler_params=_cp(("parallel",)),
    )(where, shard)


def _pair_add(g4, rsib, where, name):
    J, R, C = g4.shape
    Rh = R // 2
    tr = _row_tile(Rh, C, 4)
    g5 = g4.reshape(J, 2, Rh, C)

    def body(s_ref, g_ref, r_ref, p_ref, q_ref):
        val = (g_ref[...].astype(F32)[0] + r_ref[...].astype(F32)).astype(BF16)
        p_ref[...] = val

        @pl.when(pl.program_id(1) == s_ref[1])
        def _():
            q_ref[...] = val

    return pl.pallas_call(
        body, name=name,
        grid_spec=pltpu.PrefetchScalarGridSpec(
            num_scalar_prefetch=1, grid=(Rh // tr, J),
            in_specs=[pl.BlockSpec((1, 1, tr, C), lambda i, j, s_ref: (j, s_ref[0], i, 0)),
                      pl.BlockSpec((1, tr, C), lambda i, j, s_ref: (j, i, 0))],
            out_specs=[pl.BlockSpec((1, tr, C), lambda i, j, s_ref: (j, i, 0)),
                       pl.BlockSpec((1, tr, C), lambda i, j, s_ref: (s_ref[1], i, 0))]),
        out_shape=[jax.ShapeDtypeStruct((J, Rh, C), BF16)] * 2,
        compiler_params=_cp(("parallel", "arbitrary")),
    )(where, g5, rsib)


def _sum_chips(q, where, dest, layer, out_shape, name):
    J, Rh, C = q.shape
    tr = _row_tile(Rh, C, 3)
    nb = Rh // tr

    def body(s_ref, q_ref, *rest):
        qq = q_ref[...].astype(F32)
        rest[-1][0] = ((qq[0] + qq[1]) + qq[2]) + qq[3]

    have = dest is not None
    return pl.pallas_call(
        body, name=name,
        grid_spec=pltpu.PrefetchScalarGridSpec(
            num_scalar_prefetch=1, grid=(nb,),
            in_specs=[pl.BlockSpec((J, tr, C), lambda i, s_ref: (0, i, 0))] + ([ANY] if have else []),
            out_specs=pl.BlockSpec((1, tr, C), lambda i, s_ref: (layer, s_ref[0] * nb + i, 0))),
        out_shape=jax.ShapeDtypeStruct(out_shape, F32),
        input_output_aliases={2: 0} if have else {},
        compiler_params=_cp(("parallel",)),
    )(*((where, q, dest) if have else (where, q)))


MESH = pl.DeviceIdType.MESH
ANY = pl.BlockSpec(memory_space=pl.ANY)


def _place():
    x, y, c = lax.axis_index("x"), lax.axis_index("y"), lax.axis_index("c")
    others = [(1 - x, y), (x, 1 - y), (1 - x, 1 - y)]
    return x, y, c, 2 * x + y, others, [2 * ox + oy for ox, oy in others]


def _gather_weights(placed, split):
    n = len(placed)

    def body(*refs):
        outs = refs[n:2 * n]
        send, recv, fsend, frecv = refs[2 * n:]
        x, y, c, me, others, okey = _place()
        sib = (x, y, 1 - c)

        def rows(t, ref, half):
            if not split[t]:
                return ref
            rh = placed[t].shape[1] // 2
            return ref.at[pl.ds(half * rh, rh), :]

        first = []
        for t in range(n):
            mine = rows(t, outs[t].at[me], c)
            for j, (ox, oy) in enumerate(others):
                cp = pltpu.make_async_remote_copy(
                    src_ref=mine, dst_ref=mine,
                    send_sem=send.at[t, j], recv_sem=recv.at[t, j], device_id=(ox, oy, c), device_id_type=MESH)
                cp.start()
                first.append(cp)
        passed = []
        for t in range(n):
            for j in range(3):
                landed = rows(t, outs[t].at[okey[j]], c)
                pltpu.make_async_remote_copy(
                    src_ref=landed, dst_ref=landed, send_sem=send.at[t, j], recv_sem=recv.at[t, j],
                    device_id=sib, device_id_type=MESH).wait_recv()
                if split[t]:
                    cp = pltpu.make_async_remote_copy(
                        src_ref=landed, dst_ref=landed, send_sem=fsend.at[t, j], recv_sem=frecv.at[t, j],
                        device_id=sib, device_id_type=MESH)
                    cp.start()
                    passed.append(cp)
        for t in range(n):
            if split[t]:
                for j in range(3):
                    theirs = rows(t, outs[t].at[okey[j]], 1 - c)
                    pltpu.make_async_remote_copy(
                        src_ref=theirs, dst_ref=theirs, send_sem=fsend.at[t, j], recv_sem=frecv.at[t, j],
                        device_id=sib, device_id_type=MESH).wait_recv()
        for cp in first + passed:
            cp.wait_send()

    return pl.pallas_call(
        body, name="gather_weights",
        in_specs=[ANY] * n, out_specs=[ANY] * n,
        out_shape=[jax.ShapeDtypeStruct(a.shape, a.dtype) for a in placed],
        input_output_aliases={t: t for t in range(n)},
        scratch_shapes=[pltpu.SemaphoreType.DMA((n, 3))] * 4,
    )(*placed)


def _pair_exchange(gs):
    n = len(gs)

    def body(*refs):
        ins, outs = refs[:n], refs[n:2 * n]
        send, recv = refs[2 * n:]
        x, y, c, _, _, _ = _place()
        cps = []
        for t in range(n):
            rh = gs[t].shape[1] // 2
            cp = pltpu.make_async_remote_copy(
                src_ref=ins[t].at[:, pl.ds((1 - c) * rh, rh), :], dst_ref=outs[t],
                send_sem=send.at[t], recv_sem=recv.at[t], device_id=(x, y, 1 - c), device_id_type=MESH)
            cp.start()
            cps.append(cp)
        for cp in cps:
            cp.wait()

    return pl.pallas_call(
        body, name="grad_pair_exchange",
        in_specs=[ANY] * n, out_specs=[ANY] * n,
        out_shape=[jax.ShapeDtypeStruct((4, g.shape[1] // 2, g.shape[2]), g.dtype) for g in gs],
        scratch_shapes=[pltpu.SemaphoreType.DMA((n,))] * 2,
    )(*gs)


def _chip_scatter(ps, qs):
    n = len(ps)

    def body(*refs):
        ins, outs = refs[:n], refs[2 * n:3 * n]
        send, recv = refs[3 * n:]
        x, y, c, me, others, okey = _place()
        cps = []
        for t in range(n):
            for j, (ox, oy) in enumerate(others):
                cp = pltpu.make_async_remote_copy(
                    src_ref=ins[t].at[okey[j]], dst_ref=outs[t].at[me],
                    send_sem=send.at[t, j], recv_sem=recv.at[t, j], device_id=(ox, oy, c), device_id_type=MESH)
                cp.start()
                cps.append(cp)
        for t in range(n):
            for j in range(3):
                slot = outs[t].at[okey[j]]
                pltpu.make_async_remote_copy(
                    src_ref=slot, dst_ref=slot, send_sem=send.at[t, j], recv_sem=recv.at[t, j],
                    device_id=(x, y, c), device_id_type=MESH).wait_recv()
        for cp in cps:
            cp.wait_send()

    return pl.pallas_call(
        body, name="grad_chip_scatter",
        in_specs=[ANY] * (2 * n), out_specs=[ANY] * n,
        out_shape=[jax.ShapeDtypeStruct(q.shape, q.dtype) for q in qs],
        input_output_aliases={n + t: t for t in range(n)},
        scratch_shapes=[pltpu.SemaphoreType.DMA((n, 3))] * 2,
    )(*ps, *qs)


def _half_exchange(arrs):
    n = len(arrs)
    items = [(t, layer) for t in range(n) for layer in range(arrs[t].shape[0])]

    def body(*refs):
        outs = refs[n:2 * n]
        send, recv = refs[2 * n:]
        x, y, c, _, _, _ = _place()
        cps = []
        for k, (t, layer) in enumerate(items):
            rh = arrs[t].shape[1] // 2
            mine = outs[t].at[layer, pl.ds(c * rh, rh), :]
            cp = pltpu.make_async_remote_copy(
                src_ref=mine, dst_ref=mine, send_sem=send.at[k], recv_sem=recv.at[k],
                device_id=(x, y, 1 - c), device_id_type=MESH)
            cp.start()
            cps.append(cp)
        for k, (t, layer) in enumerate(items):
            rh = arrs[t].shape[1] // 2
            theirs = outs[t].at[layer, pl.ds((1 - c) * rh, rh), :]
            pltpu.make_async_remote_copy(
                src_ref=theirs, dst_ref=theirs, send_sem=send.at[k], recv_sem=recv.at[k],
                device_id=(x, y, 1 - c), device_id_type=MESH).wait_recv()
        for cp in cps:
            cp.wait_send()

    return pl.pallas_call(
        body, name="grad_half_exchange",
        in_specs=[ANY] * n, out_specs=[ANY] * n,
        out_shape=[jax.ShapeDtypeStruct(a.shape, F32) for a in arrs],
        input_output_aliases={t: t for t in range(n)},
        scratch_shapes=[pltpu.SemaphoreType.DMA((len(items),))] * 2,
    )(*arrs)


SMALL_COLS = 1024
SMALL_PIECE_ROWS = 40


def _allreduce_small(buf):
    pr = SMALL_PIECE_ROWS
    flips = [(d >> 2 & 1, d >> 1 & 1, d & 1) for d in range(1, 8)]

    def body(x_ref, o_ref, rbuf, send1, recv1, send2, recv2):
        x, y, c = lax.axis_index("x"), lax.axis_index("y"), lax.axis_index("c")
        me = 4 * x + 2 * y + c
        peers = [(x ^ fx, y ^ fy, c ^ fc) for fx, fy, fc in flips]
        pid = [4 * px + 2 * py + pc for px, py, pc in peers]

        def piece(ref, p):
            return ref.at[pl.ds(pl.multiple_of(p * pr, 8), pr), :]

        cps = []
        for d in range(7):
            cp = pltpu.make_async_remote_copy(
                src_ref=piece(x_ref, pid[d]), dst_ref=rbuf.at[d + 1],
                send_sem=send1.at[d], recv_sem=recv1.at[d], device_id=peers[d], device_id_type=MESH)
            cp.start()
            cps.append(cp)
        acc = piece(x_ref, me)[...]
        for d in range(7):
            cps[d].wait_recv()
            acc = acc + rbuf[d + 1]
        piece(o_ref, me)[...] = acc
        out = []
        for d in range(7):
            cp = pltpu.make_async_remote_copy(
                src_ref=piece(o_ref, me), dst_ref=piece(o_ref, me),
                send_sem=send2.at[d], recv_sem=recv2.at[d], device_id=peers[d], device_id_type=MESH)
            cp.start()
            out.append(cp)
        for d in range(7):
            pltpu.make_async_remote_copy(
                src_ref=piece(o_ref, pid[d]), dst_ref=piece(o_ref, pid[d]),
                send_sem=send2.at[d], recv_sem=recv2.at[d], device_id=peers[d], device_id_type=MESH).wait_recv()
        for cp in cps + out:
            cp.wait_send()

    vm = pl.BlockSpec(memory_space=pltpu.VMEM)
    return pl.pallas_call(
        body, name="small_allreduce", in_specs=[vm], out_specs=vm,
        out_shape=jax.ShapeDtypeStruct(buf.shape, F32),
        scratch_shapes=[pltpu.VMEM((8, pr, SMALL_COLS), F32)] + [pltpu.SemaphoreType.DMA((7,))] * 4,
    )(buf)


def _pack(arrays, rows):
    flat = jnp.concatenate([a.reshape(-1) for a in arrays])
    return jnp.pad(flat, (0, rows * SMALL_COLS - flat.shape[0])).reshape(rows, SMALL_COLS)


def _unpack(buf, shapes):
    flat = buf.reshape(-1)
    out, off = [], 0
    for s in shapes:
        k = math.prod(s)
        out.append(flat[off:off + k].reshape(s))
        off += k
    return out


BIG = ["sgu_w_in", "sgu_w_out", "attn_w_qkv", "attn_w_o", "ffn_w_up", "ffn_w_down"]
SMALL = ["mix_norm", "ffn_norm", "sgu_v_gain", "sgu_w_s", "sgu_b_s", "attn_q_gain", "attn_k_gain", "attn_sinks",
         "rel_bias", "ffn_conv_b"]
ORDER = ["mix_norm", "ffn_norm", "sgu_w_in", "sgu_v_gain", "sgu_w_s", "sgu_b_s", "sgu_w_out", "attn_w_qkv",
         "attn_q_gain", "attn_k_gain", "attn_sinks", "attn_w_o", "rel_bias", "ffn_w_up", "ffn_conv_w", "ffn_conv_b",
         "ffn_w_down"]


def kernel(x, mix_norm, ffn_norm, sgu_w_in, sgu_v_gain, sgu_w_s, sgu_b_s, sgu_w_out, attn_w_qkv, attn_q_gain, attn_k_gain, attn_sinks, attn_w_o, rel_bias, ffn_w_up, ffn_conv_w, ffn_conv_b, ffn_w_down, loss_target, m_mix_norm, m_ffn_norm, m_sgu_w_in, m_sgu_v_gain, m_sgu_w_s, m_sgu_b_s, m_sgu_w_out, m_attn_w_qkv, m_attn_q_gain, m_attn_k_gain, m_attn_sinks, m_attn_w_o, m_rel_bias, m_ffn_w_up, m_ffn_conv_w, m_ffn_conv_b, m_ffn_w_down, v_mix_norm, v_ffn_norm, v_sgu_w_in, v_sgu_v_gain, v_sgu_w_s, v_sgu_b_s, v_sgu_w_out, v_attn_w_qkv, v_attn_q_gain, v_attn_k_gain, v_attn_sinks, v_attn_w_o, v_rel_bias, v_ffn_w_up, v_ffn_conv_w, v_ffn_conv_b, v_ffn_w_down):
    w = dict(mix_norm=mix_norm, ffn_norm=ffn_norm, sgu_w_in=sgu_w_in, sgu_v_gain=sgu_v_gain, sgu_w_s=sgu_w_s,
             sgu_b_s=sgu_b_s, sgu_w_out=sgu_w_out, attn_w_qkv=attn_w_qkv, attn_q_gain=attn_q_gain,
             attn_k_gain=attn_k_gain, attn_sinks=attn_sinks, attn_w_o=attn_w_o, rel_bias=rel_bias, ffn_w_up=ffn_w_up,
             ffn_conv_w=ffn_conv_w, ffn_conv_b=ffn_conv_b, ffn_w_down=ffn_w_down)
    mom = dict(mix_norm=m_mix_norm, ffn_norm=m_ffn_norm, sgu_w_in=m_sgu_w_in, sgu_v_gain=m_sgu_v_gain,
               sgu_w_s=m_sgu_w_s, sgu_b_s=m_sgu_b_s, sgu_w_out=m_sgu_w_out, attn_w_qkv=m_attn_w_qkv,
               attn_q_gain=m_attn_q_gain, attn_k_gain=m_attn_k_gain, attn_sinks=m_attn_sinks, attn_w_o=m_attn_w_o,
               rel_bias=m_rel_bias, ffn_w_up=m_ffn_w_up, ffn_conv_w=m_ffn_conv_w, ffn_conv_b=m_ffn_conv_b,
               ffn_w_down=m_ffn_w_down)
    var = dict(mix_norm=v_mix_norm, ffn_norm=v_ffn_norm, sgu_w_in=v_sgu_w_in, sgu_v_gain=v_sgu_v_gain,
               sgu_w_s=v_sgu_w_s, sgu_b_s=v_sgu_b_s, sgu_w_out=v_sgu_w_out, attn_w_qkv=v_attn_w_qkv,
               attn_q_gain=v_attn_q_gain, attn_k_gain=v_attn_k_gain, attn_sinks=v_attn_sinks, attn_w_o=v_attn_w_o,
               rel_bias=v_rel_bias, ffn_w_up=v_ffn_w_up, ffn_conv_w=v_ffn_conv_w, ffn_conv_b=v_ffn_conv_b,
               ffn_w_down=v_ffn_w_down)
    chip = 2 * lax.axis_index("x") + lax.axis_index("y")
    core = lax.axis_index("c")

    where = jnp.stack([core, chip]).astype(jnp.int32)
    names = ["sgu_w_in", "sgu_w_out", "attn_w_qkv", "attn_w_o", "ffn_w_up0", "ffn_w_up1", "ffn_w_down0", "ffn_w_down1"]
    shards = [sgu_w_in[0], sgu_w_out[0], attn_w_qkv[0], attn_w_o[0], ffn_w_up[0], ffn_w_up[1],
              ffn_w_down[0], ffn_w_down[1]]
    placed = [_place_shard(s, where, BF16, "place_" + nm) for s, nm in zip(shards, names)]
    placed += [_place_shard(ffn_conv_w[l], where, F32, "place_conv_w%d" % l) for l in range(2)]
    full = _gather_weights(placed, [True] * 8 + [False] * 2)
    D = x.shape[2]
    unshard_cols = lambda a: jnp.transpose(a, (1, 0, 2)).reshape(a.shape[1], -1)
    p = dict(
        mix_norm=mix_norm, ffn_norm=ffn_norm, sgu_v_gain=sgu_v_gain, sgu_w_s=sgu_w_s[0], sgu_b_s=sgu_b_s[0],
        attn_q_gain=attn_q_gain, attn_k_gain=attn_k_gain, attn_sinks=attn_sinks, rel_bias=rel_bias,
        ffn_conv_b=ffn_conv_b,
        sgu_w_in=full[0], sgu_w_out=full[1].reshape(-1, D), attn_w_qkv=full[2], attn_w_o=full[3].reshape(-1, D),
        ffn_w_up=(full[4], full[5]), ffn_w_down=(full[6].reshape(-1, D), full[7].reshape(-1, D)),
        ffn_conv_w=(unshard_cols(full[8]), unshard_cols(full[9])),
    )

    loss_local, grad_x, g = _local_grads(x[0], loss_target[0], p)
    loss = lax.psum(loss_local, ("x", "y", "c"))

    rows4 = lambda a, r: a.reshape(4, r, a.shape[-1])
    gs = [g["sgu_w_in"], g["sgu_w_out"], g["attn_w_qkv"], g["attn_w_o"], g["ffn_w_up"][0], g["ffn_w_up"][1],
          rows4(g["ffn_w_down"][0], ffn_w_down.shape[1]), rows4(g["ffn_w_down"][1], ffn_w_down.shape[1])]
    rsib = _pair_exchange(gs)
    pq = [_pair_add(gs[t], rsib[t], where, "pair_add_" + names[t]) for t in range(len(gs))]
    qs = _chip_scatter([a for a, _ in pq], [b for _, b in pq])
    dest_of = [("sgu_w_in", 0), ("sgu_w_out", 0), ("attn_w_qkv", 0), ("attn_w_o", 0), ("ffn_w_up", 0), ("ffn_w_up", 1),
               ("ffn_w_down", 0), ("ffn_w_down", 1)]
    halves = {}
    for t, (k, layer) in enumerate(dest_of):
        halves[k] = _sum_chips(qs[t], where, halves.get(k), layer, w[k].shape, "sum_chips_" + names[t])
    grads = dict(zip(BIG, _half_exchange([halves[k] for k in BIG])))

    small_list = [g[k].reshape(w[k].shape) for k in SMALL] + [jnp.stack(g["ffn_conv_w"])]
    small_shapes = [a.shape for a in small_list]
    red = _unpack(_allreduce_small(_pack(small_list, 8 * SMALL_PIECE_ROWS)), small_shapes)
    for k, a in zip(SMALL, red[:-1]):
        grads[k] = a
    grads["ffn_conv_w"] = lax.dynamic_slice_in_dim(red[-1], chip * ffn_conv_w.shape[2], ffn_conv_w.shape[2], axis=2)

    delta, new_m, new_v = {}, {}, {}
    for k in BIG:
        two = lambda a: a.reshape(-1, a.shape[-1])
        d2, m2, v2 = _adamw(two(grads[k]), two(w[k]), two(mom[k]), two(var[k]), "adamw_" + k)
        delta[k], new_m[k], new_v[k] = (a.reshape(w[k].shape) for a in (d2, m2, v2))
    sm = SMALL + ["ffn_conv_w"]
    sm_shapes = [w[k].shape for k in sm]
    n_el = sum(math.prod(s) for s in sm_shapes)
    rows = -(-n_el // (8 * SMALL_COLS)) * 8
    d2, m2, v2 = _adamw(_pack([grads[k] for k in sm], rows), _pack([w[k] for k in sm], rows),
                        _pack([mom[k] for k in sm], rows), _pack([var[k] for k in sm], rows), "adamw_small")
    for dst, buf in ((delta, d2), (new_m, m2), (new_v, v2)):
        for k, a in zip(sm, _unpack(buf, sm_shapes)):
            dst[k] = a

    return (loss, grad_x[None], *[grads[k] for k in ORDER], *[delta[k] for k in ORDER],
            *[new_m[k] for k in ORDER], *[new_v[k] for k in ORDER])
```

```python
import functools
import math

import numpy as np
import jax
import jax.numpy as jnp
from jax import lax
from jax.experimental import pallas as pl
from jax.experimental.pallas import tpu as pltpu

F32 = jnp.float32
BF16 = jnp.bfloat16

EPS = 1e-6
CHUNK = 128
SGU_GROUPS = 16
HEAD_DIM = 64
N_HEADS = 16
N_KV_HEADS = 4
REL_BUCKETS = 32
REL_MAX_DIST = 128
LANES = 128
HALO = 16

ADAM_LR = 0.001
ADAM_B1 = 0.9
ADAM_B2 = 0.999
ADAM_EPS = 1e-08
ADAM_WD = 0.01
ADAM_STEP = 10

VMEM_LIMIT_V7X = 56 * 1024 * 1024

_SQRT_HALF = math.sqrt(0.5)
_INV_SQRT_2PI = 1.0 / math.sqrt(2.0 * math.pi)


def _cp(sem):
    return pltpu.CompilerParams(dimension_semantics=sem, vmem_limit_bytes=VMEM_LIMIT_V7X)


def _resident(shape):
    nd = len(shape)
    return pl.BlockSpec(shape, lambda *_: (0,) * nd, pipeline_mode=pl.Buffered(1))


class _Comm:
    def __init__(self, srcs, zones, news, sems, start, finish):
        self.srcs, self.zones, self.news, self.sems = list(srcs), list(zones), list(news), list(sems)
        self.start, self.finish = start, finish


def _join(comms):
    comms = [c for c in comms if c is not None]
    if not comms:
        return None

    def part(seq, attr):
        out, k = [], 0
        for c in comms:
            n = len(getattr(c, attr))
            out.append(seq[k:k + n])
            k += n
        return out

    def run(which):
        def f(srcs, zones, news, sems):
            for c, a, b, d, e in zip(comms, part(srcs, "srcs"), part(zones, "zones"), part(news, "news"), part(sems, "sems")):
                getattr(c, which)(a, b, d, e)
        return f

    cat = lambda attr: [v for c in comms for v in getattr(c, attr)]
    return _Comm(cat("srcs"), cat("zones"), cat("news"), cat("sems"), run("start"), run("finish"))


def _split(comms, res):
    zones, news = res
    out, kz, kn = [], 0, 0
    for c in comms:
        out.append((zones[kz:kz + len(c.zones)], news[kn:kn + len(c.news)]))
        kz += len(c.zones)
        kn += len(c.news)
    return out


def _call(body, *, name, grid, in_specs, out_specs, out_shape, args, scratch_shapes=(), sem=None, comm=None):
    if comm is None:
        res = pl.pallas_call(body, name=name, grid=grid, in_specs=in_specs, out_specs=out_specs, out_shape=out_shape,
                             scratch_shapes=list(scratch_shapes), compiler_params=_cp(sem))(*args)
        return res, None
    single = not isinstance(out_shape, (list, tuple))
    out_specs_l = [out_specs] if single else list(out_specs)
    out_shape_l = [out_shape] if single else list(out_shape)
    n_in, n_out, n_scr = len(in_specs), len(out_shape_l), len(scratch_shapes)
    ns, nz, nn = len(comm.srcs), len(comm.zones), len(comm.news)

    def wrapped(*refs):
        k = n_in
        ins, srcs = refs[:k], refs[k:k + ns]
        k += ns + nz
        outs, zones, news = refs[k:k + n_out], refs[k + n_out:k + n_out + nz], refs[k + n_out + nz:k + n_out + nz + nn]
        k += n_out + nz + nn
        scr, sems = refs[k:k + n_scr], refs[k + n_scr:]
        if not grid:
            comm.start(srcs, zones, news, sems)
            body(*ins, *outs, *scr)
            comm.finish(srcs, zones, news, sems)
            return
        first = functools.reduce(jnp.logical_and, [pl.program_id(a) == 0 for a in range(len(grid))])
        last = functools.reduce(jnp.logical_and, [pl.program_id(a) == grid[a] - 1 for a in range(len(grid))])

        @pl.when(first)
        def _():
            comm.start(srcs, zones, news, sems)

        body(*ins, *outs, *scr)

        @pl.when(last)
        def _():
            comm.finish(srcs, zones, news, sems)

    res = pl.pallas_call(
        wrapped, name=name, grid=grid,
        in_specs=list(in_specs) + [ANY] * (ns + nz), out_specs=out_specs_l + [ANY] * (nz + nn),
        out_shape=out_shape_l + [jax.ShapeDtypeStruct(z.shape, z.dtype) for z in comm.zones] + comm.news,
        input_output_aliases={n_in + ns + i: n_out + i for i in range(nz)},
        scratch_shapes=list(scratch_shapes) + comm.sems,
        compiler_params=_cp(("arbitrary",) * len(grid)),
    )(*args, *comm.srcs, *comm.zones)
    main = res[0] if single else list(res[:n_out])
    return main, (list(res[n_out:n_out + nz]), list(res[n_out + nz:]))


def _run_comm(comm, name):
    ns, nz, nn = len(comm.srcs), len(comm.zones), len(comm.news)

    def body(*refs):
        srcs, zones, news, sems = refs[:ns], refs[ns + nz:ns + 2 * nz], refs[ns + 2 * nz:ns + 2 * nz + nn], refs[ns + 2 * nz + nn:]
        comm.start(srcs, zones, news, sems)
        comm.finish(srcs, zones, news, sems)

    res = pl.pallas_call(
        body, name=name, in_specs=[ANY] * (ns + nz), out_specs=[ANY] * (nz + nn),
        out_shape=[jax.ShapeDtypeStruct(z.shape, z.dtype) for z in comm.zones] + comm.news,
        input_output_aliases={ns + i: i for i in range(nz)}, scratch_shapes=comm.sems,
    )(*comm.srcs, *comm.zones)
    return list(res[:nz]), list(res[nz:])


def _dot(a, b):
    return jnp.dot(a, b, preferred_element_type=F32)


def _dot_nt(a, b):
    return lax.dot_general(a, b, (((1,), (1,)), ((), ())), preferred_element_type=F32)


def _dot_tn(a, b):
    return lax.dot_general(a, b, (((0,), (0,)), ((), ())), preferred_element_type=F32)


def _gelu(x):
    return 0.5 * x * (1.0 + lax.erf(x * _SQRT_HALF))


def _gelu_and_grad(x):
    cdf = 0.5 * (1.0 + lax.erf(x * _SQRT_HALF))
    return x * cdf, cdf + x * jnp.exp(-0.5 * x * x) * _INV_SQRT_2PI


def _rms_bwd(dy, x, gain):
    r = lax.rsqrt(jnp.mean(x * x, axis=-1, keepdims=True) + EPS)
    xhat = x * r
    gdy = dy * gain
    dx = r * (gdy - xhat * jnp.mean(gdy * xhat, axis=-1, keepdims=True))
    return dx, jnp.sum(dy * xhat, axis=0, keepdims=True)


def _norm_matmul(x, gain, w4, name, comm=None, tm=512):
    S, D = x.shape
    nsh, _, ns = w4.shape
    tm = min(tm, S)

    def body(x_ref, g_ref, w_ref, hn_ref, o_ref):
        xf = x_ref[...]
        r = lax.rsqrt(jnp.mean(xf * xf, axis=-1, keepdims=True) + EPS)
        hn = (xf * r * g_ref[...]).astype(BF16)
        hn_ref[...] = hn
        for j in range(nsh):
            o_ref[:, j * ns:(j + 1) * ns] = _dot(hn, w_ref[j]).astype(BF16)

    return _call(
        body, name=name, grid=(S // tm,),
        in_specs=[pl.BlockSpec((tm, D), lambda i: (i, 0)), _resident((1, D)), _resident(w4.shape)],
        out_specs=[pl.BlockSpec((tm, D), lambda i: (i, 0)), pl.BlockSpec((tm, nsh * ns), lambda i: (i, 0))],
        out_shape=[jax.ShapeDtypeStruct((S, D), BF16), jax.ShapeDtypeStruct((S, nsh * ns), BF16)],
        sem=("parallel",), args=(x, gain, w4), comm=comm)


def _matmul_res(a, w, res, name, comm=None, tm=512):
    S, K = a.shape
    N = w.shape[1]
    tm = min(tm, S)

    def body(a_ref, w_ref, r_ref, o_ref):
        o_ref[...] = r_ref[...] + _dot(a_ref[...], w_ref[...])

    return _call(
        body, name=name, grid=(S // tm,),
        in_specs=[pl.BlockSpec((tm, K), lambda i: (i, 0)), _resident(w.shape), pl.BlockSpec((tm, N), lambda i: (i, 0))],
        out_specs=pl.BlockSpec((tm, N), lambda i: (i, 0)),
        out_shape=jax.ShapeDtypeStruct((S, N), F32),
        sem=("parallel",), args=(a, w, res), comm=comm)


def _matmul_nt(dh, w, name, comm=None, tm=512):
    S, N = dh.shape
    K = w.shape[0]
    tm = min(tm, S)

    def body(d_ref, w_ref, o_ref):
        o_ref[...] = _dot_nt(d_ref[...].astype(BF16), w_ref[...]).astype(BF16)

    return _call(
        body, name=name, grid=(S // tm,),
        in_specs=[pl.BlockSpec((tm, N), lambda i: (i, 0)), _resident(w.shape)],
        out_specs=pl.BlockSpec((tm, K), lambda i: (i, 0)),
        out_shape=jax.ShapeDtypeStruct((S, K), BF16),
        sem=("parallel",), args=(dh, w), comm=comm)


def _matmul_tn(a, b, name, *, ka, nb, out_dtype, comm=None, ts=512):
    S, KA = a.shape
    NB = b.shape[1]
    ts = min(ts, S)
    J = max(KA // ka, NB // nb)
    a_map = (lambda j, s: (s, j)) if KA // ka > 1 else (lambda j, s: (s, 0))
    b_map = (lambda j, s: (s, j)) if NB // nb > 1 else (lambda j, s: (s, 0))
    last = S // ts - 1

    def body(a_ref, b_ref, o_ref, acc):
        s = pl.program_id(1)

        @pl.when(s == 0)
        def _():
            acc[...] = jnp.zeros_like(acc)

        acc[...] += _dot_tn(a_ref[...].astype(BF16), b_ref[...].astype(BF16))

        @pl.when(s == last)
        def _():
            o_ref[0] = acc[...].astype(out_dtype)

    return _call(
        body, name=name, grid=(J, S // ts),
        in_specs=[pl.BlockSpec((ts, ka), a_map), pl.BlockSpec((ts, nb), b_map)],
        out_specs=pl.BlockSpec((1, ka, nb), lambda j, s: (j, 0, 0)),
        out_shape=jax.ShapeDtypeStruct((J, ka, nb), out_dtype),
        scratch_shapes=[pltpu.VMEM((ka, nb), F32)],
        sem=("parallel", "arbitrary"), args=(a, b), comm=comm)


def _sgu_prep(w_s):
    G = w_s.shape[0]

    def body(w_ref, t_ref, tt_ref):
        tri = lax.broadcasted_iota(jnp.int32, (CHUNK, CHUNK), 0) >= lax.broadcasted_iota(jnp.int32, (CHUNK, CHUNK), 1)
        for g in range(G):
            t = jnp.where(tri, w_ref[g], 0.0)
            t_ref[g] = t.astype(BF16)
            tt_ref[g] = t.T.astype(BF16)

    return pl.pallas_call(
        body, name="sgu_prep",
        out_shape=[jax.ShapeDtypeStruct(w_s.shape, BF16), jax.ShapeDtypeStruct(w_s.shape, BF16)],
        compiler_params=_cp(None),
    )(w_s)


def _sgu_fwd(z, x, vg, wtril, bT, wout, comm=None, tm=256):
    S = z.shape[0]
    W = z.shape[1] // 2
    D = x.shape[1]
    tm = min(tm, S)

    def body(z_ref, x_ref, vg_ref, wt_ref, bT_ref, wo_ref, yp_ref, h_ref):
        def chunk(c, carry):
            r0 = pl.multiple_of(c * CHUNK, CHUNK)
            zc = z_ref[pl.ds(r0, CHUNK), :].astype(F32)
            u = _gelu(zc[:, :W])
            v = _gelu(zc[:, W:])
            rv = lax.rsqrt(jnp.mean(v * v, axis=-1, keepdims=True) + EPS)
            vn = (v * rv * vg_ref[...]).astype(BF16)
            for g in range(SGU_GROUPS):
                sl = slice(g * LANES, (g + 1) * LANES)
                s = _dot(wt_ref[g], vn[:, sl]) + bT_ref[:, g:g + 1]
                yp_ref[pl.ds(r0, CHUNK), sl] = (u[:, sl] * s).astype(BF16)
            return carry

        lax.fori_loop(0, tm // CHUNK, chunk, 0)
        h_ref[...] = x_ref[...] + _dot(yp_ref[...], wo_ref[...])

    return _call(
        body, name="sgu_fwd", grid=(S // tm,),
        in_specs=[pl.BlockSpec((tm, 2 * W), lambda i: (i, 0)), pl.BlockSpec((tm, D), lambda i: (i, 0)),
                  _resident((1, W)), _resident(wtril.shape), _resident(bT.shape), _resident(wout.shape)],
        out_specs=[pl.BlockSpec((tm, W), lambda i: (i, 0)), pl.BlockSpec((tm, D), lambda i: (i, 0))],
        out_shape=[jax.ShapeDtypeStruct((S, W), BF16), jax.ShapeDtypeStruct((S, D), F32)],
        sem=("parallel",), args=(z, x, vg, wtril, bT, wout), comm=comm)


def _sgu_bwd(dh, z, vg, wtril, wtrilT, bT, wout, comm=None, tm=256):
    S = z.shape[0]
    W = z.shape[1] // 2
    D = dh.shape[1]
    G = SGU_GROUPS
    tm = min(tm, S)
    last = S // tm - 1

    def body(dh_ref, z_ref, vg_ref, wt_ref, wtT_ref, bT_ref, wo_ref,
             dz_ref, dws_ref, dbT_ref, dvg_ref, dyp_s, du_s, dvn_s, dsacc):
        i = pl.program_id(0)

        @pl.when(i == 0)
        def _():
            dws_ref[...] = jnp.zeros_like(dws_ref)
            dvg_ref[...] = jnp.zeros_like(dvg_ref)
            dsacc[...] = jnp.zeros_like(dsacc)

        dyp_s[...] = _dot_nt(dh_ref[...].astype(BF16), wo_ref[...])
        tri = lax.broadcasted_iota(jnp.int32, (CHUNK, CHUNK), 0) >= lax.broadcasted_iota(jnp.int32, (CHUNK, CHUNK), 1)

        def chunk(c, carry):
            r0 = pl.multiple_of(c * CHUNK, CHUNK)
            zc = z_ref[pl.ds(r0, CHUNK), :].astype(F32)
            u, gu = _gelu_and_grad(zc[:, :W])
            v, gv = _gelu_and_grad(zc[:, W:])
            rv = lax.rsqrt(jnp.mean(v * v, axis=-1, keepdims=True) + EPS)
            vhat = v * rv
            vgain = vg_ref[...]
            vn = (vhat * vgain).astype(BF16)
            dyp = dyp_s[pl.ds(r0, CHUNK), :]
            for g in range(G):
                sl = slice(g * LANES, (g + 1) * LANES)
                vng = vn[:, sl]
                s = _dot(wt_ref[g], vng) + bT_ref[:, g:g + 1]
                ds = dyp[:, sl] * u[:, sl]
                du_s[:, sl] = dyp[:, sl] * s
                dsb = ds.astype(BF16)
                dvn_s[:, sl] = _dot(wtT_ref[g], dsb)
                dws_ref[g] += jnp.where(tri, _dot_nt(dsb, vng), 0.0)
                dsacc[g] += ds
            dvn = dvn_s[...]
            dvg_ref[...] += jnp.sum(dvn * vhat, axis=0, keepdims=True)
            gdy = dvn * vgain
            dv = rv * (gdy - vhat * jnp.mean(gdy * vhat, axis=-1, keepdims=True))
            dz_ref[pl.ds(r0, CHUNK), :W] = (du_s[...] * gu).astype(BF16)
            dz_ref[pl.ds(r0, CHUNK), W:] = (dv * gv).astype(BF16)
            return carry

        lax.fori_loop(0, tm // CHUNK, chunk, 0)

        @pl.when(i == last)
        def _():
            for g in range(G):
                dbT_ref[:, g:g + 1] = jnp.sum(dsacc[g], axis=1, keepdims=True)

    return _call(
        body, name="sgu_bwd", grid=(S // tm,),
        in_specs=[pl.BlockSpec((tm, D), lambda i: (i, 0)), pl.BlockSpec((tm, 2 * W), lambda i: (i, 0)),
                  _resident((1, W)), _resident(wtril.shape), _resident(wtrilT.shape), _resident(bT.shape),
                  _resident(wout.shape)],
        out_specs=[pl.BlockSpec((tm, 2 * W), lambda i: (i, 0)),
                   pl.BlockSpec((G, CHUNK, CHUNK), lambda i: (0, 0, 0)),
                   pl.BlockSpec((CHUNK, G), lambda i: (0, 0)),
                   pl.BlockSpec((1, W), lambda i: (0, 0))],
        out_shape=[jax.ShapeDtypeStruct((S, 2 * W), BF16), jax.ShapeDtypeStruct((G, CHUNK, CHUNK), F32),
                   jax.ShapeDtypeStruct((CHUNK, G), F32), jax.ShapeDtypeStruct((1, W), F32)],
        scratch_shapes=[pltpu.VMEM((tm, W), F32), pltpu.VMEM((CHUNK, W), F32), pltpu.VMEM((CHUNK, W), F32),
                        pltpu.VMEM((G, CHUNK, CHUNK), F32)],
        sem=("arbitrary",), args=(dh, z, vg, wtril, wtrilT, bT, wout), comm=comm)


def _conv_taps(a32, r0, R):
    X = a32[pl.ds(r0 + HALO - 8, R + 8), :]
    return X[8:], pltpu.roll(X, 1, 0)[8:], pltpu.roll(X, 2, 0)[8:]


def _ffn_fwd(a, h_in, cw, cb, wdown, name, comm=None, tm=256, R=64):
    S, C = a.shape
    F = C // 2
    D = h_in.shape[1]
    tm = min(tm, S)

    def body(a_ref, halo_ref, h_ref, cw_ref, cb_ref, wd_ref, f_ref, ho_ref, a32):
        i = pl.program_id(0)
        a32[0:HALO, :] = jnp.where(i > 0, halo_ref[...].astype(F32), 0.0)
        a32[HALO:, :] = a_ref[...].astype(F32)

        def sub(k, carry):
            r0 = pl.multiple_of(k * R, R)
            a0, a1, a2 = _conv_taps(a32, r0, R)
            cpre = cw_ref[0:1, :] * a2 + cw_ref[1:2, :] * a1 + cw_ref[2:3, :] * a0 + cb_ref[...]
            g = cpre[:, :F]
            f_ref[pl.ds(r0, R), :] = (g * jax.nn.sigmoid(g) * cpre[:, F:]).astype(BF16)
            return carry

        lax.fori_loop(0, tm // R, sub, 0)
        ho_ref[...] = h_ref[...] + _dot(f_ref[...], wd_ref[...])

    hb = tm // HALO
    return _call(
        body, name=name, grid=(S // tm,),
        in_specs=[pl.BlockSpec((tm, C), lambda i: (i, 0)),
                  pl.BlockSpec((HALO, C), lambda i: (jnp.maximum(i * hb - 1, 0), 0)),
                  pl.BlockSpec((tm, D), lambda i: (i, 0)),
                  _resident((3, C)), _resident((1, C)), _resident(wdown.shape)],
        out_specs=[pl.BlockSpec((tm, F), lambda i: (i, 0)), pl.BlockSpec((tm, D), lambda i: (i, 0))],
        out_shape=[jax.ShapeDtypeStruct((S, F), BF16), jax.ShapeDtypeStruct((S, D), F32)],
        scratch_shapes=[pltpu.VMEM((HALO + tm, C), F32)],
        sem=("parallel",), args=(a, a, h_in, cw, cb, wdown), comm=comm)


def _ffn_bwd_dc(dh, a, cw, cb, wdown, name, comm=None, tm=256, R=64):
    S, C = a.shape
    F = C // 2
    D = dh.shape[1]
    tm = min(tm, S)

    def body(dh_ref, a_ref, halo_ref, cw_ref, cb_ref, wd_ref, dc_ref, dcw_ref, dcb_ref, a32, df_s):
        i = pl.program_id(0)

        @pl.when(i == 0)
        def _():
            dcw_ref[...] = jnp.zeros_like(dcw_ref)
            dcb_ref[...] = jnp.zeros_like(dcb_ref)

        a32[0:HALO, :] = jnp.where(i > 0, halo_ref[...].astype(F32), 0.0)
        a32[HALO:, :] = a_ref[...].astype(F32)
        df_s[...] = _dot_nt(dh_ref[...].astype(BF16), wd_ref[...])

        def sub(k, carry):
            r0 = pl.multiple_of(k * R, R)
            a0, a1, a2 = _conv_taps(a32, r0, R)
            cpre = cw_ref[0:1, :] * a2 + cw_ref[1:2, :] * a1 + cw_ref[2:3, :] * a0 + cb_ref[...]
            g = cpre[:, :F]
            val = cpre[:, F:]
            sg = jax.nn.sigmoid(g)
            df = df_s[pl.ds(r0, R), :]
            dg = df * val * (sg * (1.0 + g * (1.0 - sg)))
            dval = df * (g * sg)
            dc = jnp.concatenate([dg, dval], axis=1)
            dc_ref[pl.ds(r0, R), :] = dc.astype(BF16)
            dcw_ref[0:1, :] += jnp.sum(dc * a2, axis=0, keepdims=True)
            dcw_ref[1:2, :] += jnp.sum(dc * a1, axis=0, keepdims=True)
            dcw_ref[2:3, :] += jnp.sum(dc * a0, axis=0, keepdims=True)
            dcb_ref[...] += jnp.sum(dc, axis=0, keepdims=True)
            return carry

        lax.fori_loop(0, tm // R, sub, 0)

    hb = tm // HALO
    return _call(
        body, name=name, grid=(S // tm,),
        in_specs=[pl.BlockSpec((tm, D), lambda i: (i, 0)),
                  pl.BlockSpec((tm, C), lambda i: (i, 0)),
                  pl.BlockSpec((HALO, C), lambda i: (jnp.maximum(i * hb - 1, 0), 0)),
                  _resident((3, C)), _resident((1, C)), _resident(wdown.shape)],
        out_specs=[pl.BlockSpec((tm, C), lambda i: (i, 0)),
                   pl.BlockSpec((3, C), lambda i: (0, 0)), pl.BlockSpec((1, C), lambda i: (0, 0))],
        out_shape=[jax.ShapeDtypeStruct((S, C), BF16), jax.ShapeDtypeStruct((3, C), F32),
                   jax.ShapeDtypeStruct((1, C), F32)],
        scratch_shapes=[pltpu.VMEM((HALO + tm, C), F32), pltpu.VMEM((tm, F), F32)],
        sem=("arbitrary",), args=(dh, a, a, cw, cb, wdown), comm=comm)


def _bwd_norm(dA, w4, h_in, gain, dh_out, name, conv_w=None, comm=None, tm=256, R=64):
    S, N = dA.shape
    nsh, D, ns = w4.shape
    tm = min(tm, S)
    nt = S // tm
    conv = conv_w is not None

    def finish(src_ref, w_ref, h_ref, g_ref, dho_ref, dhi_ref, dg_ref):
        dhn = _dot_nt(src_ref[:, 0:ns], w_ref[0])
        for j in range(1, nsh):
            dhn += _dot_nt(src_ref[:, j * ns:(j + 1) * ns], w_ref[j])
        dx, dgain = _rms_bwd(dhn, h_ref[...], g_ref[...])
        dg_ref[...] += dgain
        dhi_ref[...] = dho_ref[...] + dx

    def body_plain(dA_ref, w_ref, h_ref, g_ref, dho_ref, dhi_ref, dg_ref):
        @pl.when(pl.program_id(0) == 0)
        def _():
            dg_ref[...] = jnp.zeros_like(dg_ref)

        finish(dA_ref, w_ref, h_ref, g_ref, dho_ref, dhi_ref, dg_ref)

    def body_conv(dc_ref, halo_ref, cw_ref, w_ref, h_ref, g_ref, dho_ref, dhi_ref, dg_ref, da_ref, dc32):
        i = pl.program_id(0)

        @pl.when(i == 0)
        def _():
            dg_ref[...] = jnp.zeros_like(dg_ref)

        dc32[0:tm, :] = dc_ref[...].astype(F32)
        dc32[tm:, :] = jnp.where(i < nt - 1, halo_ref[...].astype(F32), 0.0)

        def sub(k, carry):
            r0 = pl.multiple_of(k * R, R)
            X = dc32[pl.ds(r0, R + 8), :]
            d0 = X[:R]
            d1 = pltpu.roll(X, R + 7, 0)[:R]
            d2 = pltpu.roll(X, R + 6, 0)[:R]
            da = cw_ref[2:3, :] * d0 + cw_ref[1:2, :] * d1 + cw_ref[0:1, :] * d2
            da_ref[pl.ds(r0, R), :] = da.astype(BF16)
            return carry

        lax.fori_loop(0, tm // R, sub, 0)
        finish(da_ref, w_ref, h_ref, g_ref, dho_ref, dhi_ref, dg_ref)

    row = lambda width: pl.BlockSpec((tm, width), lambda i: (i, 0))
    common_in = [_resident(w4.shape), row(D), _resident((1, D)), row(D)]
    common_out = [row(D), pl.BlockSpec((1, D), lambda i: (0, 0))]
    common_shape = [jax.ShapeDtypeStruct((S, D), F32), jax.ShapeDtypeStruct((1, D), F32)]
    if not conv:
        return _call(
            body_plain, name=name, grid=(nt,),
            in_specs=[row(N)] + common_in, out_specs=common_out, out_shape=common_shape,
            sem=("arbitrary",), args=(dA, w4, h_in, gain, dh_out), comm=comm)
    hb = tm // HALO
    nhb = S // HALO
    return _call(
        body_conv, name=name, grid=(nt,),
        in_specs=[row(N), pl.BlockSpec((HALO, N), lambda i: (jnp.minimum((i + 1) * hb, nhb - 1), 0)),
                  _resident((3, N))] + common_in,
        out_specs=common_out + [row(N)],
        out_shape=common_shape + [jax.ShapeDtypeStruct((S, N), BF16)],
        scratch_shapes=[pltpu.VMEM((tm + HALO, N), F32)],
        sem=("arbitrary",), args=(dA, dA, conv_w, w4, h_in, gain, dh_out), comm=comm)


def _rel_buckets_flat():
    q = np.arange(CHUNK)[:, None] + CHUNK
    k = np.arange(2 * CHUNK)[None, :]
    n = np.maximum(q - k, 0)
    max_exact = REL_BUCKETS // 2
    large = max_exact + (np.log(np.maximum(n, 1).astype(np.float32) / max_exact)
                         / math.log(REL_MAX_DIST / max_exact) * (REL_BUCKETS - max_exact)).astype(np.int32)
    large = np.minimum(large, REL_BUCKETS - 1)
    return np.where(n < max_exact, n, large).astype(np.int32).reshape(1, CHUNK * 2 * CHUNK)


def _split_bf16(x):
    hi = x.astype(BF16)
    return hi, (x - hi.astype(F32)).astype(BF16)


def _rel_bias_expand(rel_bias, bucket):
    B, H = rel_bias.shape
    n = bucket.shape[1]

    def body(rb_ref, bk_ref, o_ref):
        oh = (bk_ref[...] == lax.broadcasted_iota(jnp.int32, (B, n), 0)).astype(BF16)
        hi, lo = _split_bf16(rb_ref[...])
        o_ref[...] = _dot_tn(hi, oh) + _dot_tn(lo, oh)

    return pl.pallas_call(body, name="rel_bias_expand", out_shape=jax.ShapeDtypeStruct((H, n), F32),
                          compiler_params=_cp(None))(rel_bias, bucket)


def _rel_bias_reduce(dbias, bucket):
    H, n = dbias.shape
    B = REL_BUCKETS

    def body(db_ref, bk_ref, o_ref):
        oh = (bk_ref[...] == lax.broadcasted_iota(jnp.int32, (B, n), 0)).astype(BF16)
        hi, lo = _split_bf16(db_ref[...])
        o_ref[...] = _dot_nt(oh, hi) + _dot_nt(oh, lo)

    return pl.pallas_call(body, name="rel_bias_reduce", out_shape=jax.ShapeDtypeStruct((B, H), F32),
                          compiler_params=_cp(None))(dbias, bucket)


def _lo_mask(rows):
    return lax.broadcasted_iota(jnp.int32, (rows, LANES), 1) < HEAD_DIM


def _half_sums(y, lo):
    s_lo = jnp.sum(jnp.where(lo, y, 0.0), axis=-1, keepdims=True)
    s_hi = jnp.sum(jnp.where(lo, 0.0, y), axis=-1, keepdims=True)
    return jnp.where(lo, s_lo, s_hi)


def _half_rms(x, gain, lo):
    r = lax.rsqrt(_half_sums(x * x, lo) * (1.0 / HEAD_DIM) + EPS)
    xhat = x * r
    return xhat * gain, xhat, r


def _half_rms_bwd(dy, xhat, r, gain, lo):
    gdy = dy * gain
    dx = r * (gdy - xhat * (_half_sums(gdy * xhat, lo) * (1.0 / HEAD_DIM)))
    return dx, jnp.sum(dy * xhat, axis=0, keepdims=True)


def _dup_half(pair, e, lo):
    sw = pltpu.roll(pair, HEAD_DIM, 1)
    return jnp.where(lo, pair, sw) if e == 0 else jnp.where(lo, sw, pair)


def _band_valid(n):
    qi = lax.broadcasted_iota(jnp.int32, (CHUNK, 2 * CHUNK), 0)
    ki = lax.broadcasted_iota(jnp.int32, (CHUNK, 2 * CHUNK), 1)
    dist = qi + CHUNK - ki
    return (dist >= 0) & (dist < CHUNK) & ((n > 0) | (ki >= CHUNK))


def _softmax_sink(s, valid, sink):
    s = jnp.where(valid, s, -jnp.inf)
    m = jnp.maximum(jnp.max(s, axis=-1, keepdims=True), sink)
    p = jnp.exp(s - m)
    esink = jnp.exp(sink - m)
    inv = 1.0 / (jnp.sum(p, axis=-1, keepdims=True) + esink)
    return p * inv, esink * inv


QW = N_HEADS * HEAD_DIM
KVW = N_KV_HEADS * HEAD_DIM


def _attn_fwd(qkv, qg2, kg2, sinks, bias, comm=None):
    S = qkv.shape[0]
    nb = S // CHUNK

    def body(cur_ref, prev_ref, qg_ref, kg_ref, sink_ref, bias_ref, o_ref):
        n = pl.program_id(0)
        lo = _lo_mask(CHUNK)
        lo2 = _lo_mask(2 * CHUNK)
        valid = _band_valid(n)
        for j in range(N_KV_HEADS // 2):
            kc = slice(QW + j * LANES, QW + (j + 1) * LANES)
            vc = slice(QW + KVW + j * LANES, QW + KVW + (j + 1) * LANES)
            kpair = jnp.concatenate([prev_ref[:, j * LANES:(j + 1) * LANES], cur_ref[:, kc]], axis=0).astype(F32)
            vpair = jnp.concatenate([prev_ref[:, KVW + j * LANES:KVW + (j + 1) * LANES], cur_ref[:, vc]], axis=0).astype(F32)
            knpair, _, _ = _half_rms(kpair, kg_ref[...], lo2)
            for e in range(2):
                hk = 2 * j + e
                kdup = _dup_half(knpair, e, lo2).astype(BF16)
                vdup = _dup_half(vpair, e, lo2).astype(BF16)
                for i2 in range(2):
                    pi = 2 * hk + i2
                    qpair = cur_ref[:, pi * LANES:(pi + 1) * LANES].astype(F32)
                    qn, _, _ = _half_rms(qpair, qg_ref[...], lo)
                    outs = []
                    for par in range(2):
                        h = 2 * pi + par
                        hm = lo if par == 0 else jnp.logical_not(lo)
                        qm = jnp.where(hm, qn, 0.0).astype(BF16)
                        s = _dot_nt(qm, kdup) * (HEAD_DIM ** -0.5) + bias_ref[h]
                        p, _ = _softmax_sink(s, valid, sink_ref[h])
                        outs.append(_dot(p.astype(BF16), vdup))
                    o_ref[:, pi * LANES:(pi + 1) * LANES] = jnp.where(lo, outs[0], outs[1]).astype(BF16)

    return _call(
        body, name="attn_fwd", grid=(nb,),
        in_specs=[pl.BlockSpec((CHUNK, QW + 2 * KVW), lambda n: (n, 0)),
                  pl.BlockSpec((CHUNK, 2 * KVW), lambda n: (jnp.maximum(n - 1, 0), QW // (2 * KVW))),
                  _resident((1, LANES)), _resident((1, LANES)),
                  pl.BlockSpec(memory_space=pltpu.SMEM),
                  _resident(bias.shape)],
        out_specs=pl.BlockSpec((CHUNK, QW), lambda n: (n, 0)),
        out_shape=jax.ShapeDtypeStruct((S, QW), BF16),
        sem=("parallel",), args=(qkv, qkv, qg2, kg2, sinks, bias), comm=comm)


def _attn_bwd(qkv, do, qg2, kg2, sinks, bias, comm=None):
    S = qkv.shape[0]
    nb = S // CHUNK
    H = N_HEADS

    def body(cur_ref, prev_ref, do_ref, qg_ref, kg_ref, sink_ref, bias_ref,
             dqkv_ref, dbias_ref, dqg_ref, dkg_ref, dsink_ref, carry, band, dsacc, gacc):
        i = pl.program_id(0)
        n = nb - 1 - i
        lo = _lo_mask(CHUNK)
        lo2 = _lo_mask(2 * CHUNK)
        lane = lax.broadcasted_iota(jnp.int32, (CHUNK, LANES), 1)
        valid = _band_valid(n)

        @pl.when(i == 0)
        def _():
            dbias_ref[...] = jnp.zeros_like(dbias_ref)
            carry[...] = jnp.zeros_like(carry)
            dsacc[...] = jnp.zeros_like(dsacc)
            gacc[...] = jnp.zeros_like(gacc)

        qgain = qg_ref[...]
        kgain = kg_ref[...]
        for j in range(N_KV_HEADS // 2):
            kc = slice(QW + j * LANES, QW + (j + 1) * LANES)
            vc = slice(QW + KVW + j * LANES, QW + KVW + (j + 1) * LANES)
            kpair = jnp.concatenate([prev_ref[:, j * LANES:(j + 1) * LANES], cur_ref[:, kc]], axis=0).astype(F32)
            vpair = jnp.concatenate([prev_ref[:, KVW + j * LANES:KVW + (j + 1) * LANES], cur_ref[:, vc]], axis=0).astype(F32)
            knpair, khat, kr = _half_rms(kpair, kgain, lo2)
            dk_folds = []
            dv_folds = []
            for e in range(2):
                hk = 2 * j + e
                kdup = _dup_half(knpair, e, lo2).astype(BF16)
                vdup = _dup_half(vpair, e, lo2).astype(BF16)
                dkd = jnp.zeros((2 * CHUNK, LANES), F32)
                dvd = jnp.zeros((2 * CHUNK, LANES), F32)
                for i2 in range(2):
                    pi = 2 * hk + i2
                    qpair = cur_ref[:, pi * LANES:(pi + 1) * LANES].astype(F32)
                    qn, qhat, qr = _half_rms(qpair, qgain, lo)
                    dopair = do_ref[:, pi * LANES:(pi + 1) * LANES].astype(F32)
                    dqn = jnp.zeros((CHUNK, LANES), F32)
                    for par in range(2):
                        h = 2 * pi + par
                        hm = lo if par == 0 else jnp.logical_not(lo)
                        qm = jnp.where(hm, qn, 0.0).astype(BF16)
                        s = _dot_nt(qm, kdup) * (HEAD_DIM ** -0.5) + bias_ref[h]
                        p, psink = _softmax_sink(s, valid, sink_ref[h])
                        dom = jnp.where(hm, dopair, 0.0).astype(BF16)
                        dp = _dot_nt(dom, vdup)
                        delta = jnp.sum(p * dp, axis=-1, keepdims=True)
                        ds = p * (dp - delta)
                        dbias_ref[h] += ds
                        dsacc[...] += jnp.where(lane == h, -(psink * delta), 0.0)
                        dsr = (ds * (HEAD_DIM ** -0.5)).astype(BF16)
                        dqn += jnp.where(hm, _dot(dsr, kdup), 0.0)
                        dkd += _dot_tn(dsr, qm)
                        dvd += _dot_tn(p.astype(BF16), dom)
                    dq, dqg = _half_rms_bwd(dqn, qhat, qr, qgain, lo)
                    gacc[0:1, :] += dqg
                    dqkv_ref[:, pi * LANES:(pi + 1) * LANES] = dq.astype(BF16)
                dk_folds.append(dkd + pltpu.roll(dkd, HEAD_DIM, 1))
                dv_folds.append(dvd + pltpu.roll(dvd, HEAD_DIM, 1))
            dkn = jnp.where(lo2, dk_folds[0], dk_folds[1])
            dk, dkg = _half_rms_bwd(dkn, khat, kr, kgain, lo2)
            gacc[1:2, :] += dkg
            band[:, j * LANES:(j + 1) * LANES] = dk
            band[:, KVW + j * LANES:KVW + (j + 1) * LANES] = jnp.where(lo2, dv_folds[0], dv_folds[1])
        dqkv_ref[:, QW:] = (band[CHUNK:, :] + carry[...]).astype(BF16)
        carry[...] = band[0:CHUNK, :]

        @pl.when(i == nb - 1)
        def _():
            g = gacc[...]
            g = g + pltpu.roll(g, HEAD_DIM, 1)
            dqg_ref[...] = g[0:1, :]
            dkg_ref[...] = g[1:2, :]
            dsink_ref[...] = jnp.sum(dsacc[...], axis=0, keepdims=True)

    vec = pl.BlockSpec((1, LANES), lambda i: (0, 0))
    return _call(
        body, name="attn_bwd", grid=(nb,),
        in_specs=[pl.BlockSpec((CHUNK, QW + 2 * KVW), lambda i: (nb - 1 - i, 0)),
                  pl.BlockSpec((CHUNK, 2 * KVW), lambda i: (jnp.maximum(nb - 2 - i, 0), QW // (2 * KVW))),
                  pl.BlockSpec((CHUNK, QW), lambda i: (nb - 1 - i, 0)),
                  _resident((1, LANES)), _resident((1, LANES)),
                  pl.BlockSpec(memory_space=pltpu.SMEM),
                  _resident(bias.shape)],
        out_specs=[pl.BlockSpec((CHUNK, QW + 2 * KVW), lambda i: (nb - 1 - i, 0)),
                   pl.BlockSpec((H, CHUNK, 2 * CHUNK), lambda i: (0, 0, 0)), vec, vec, vec],
        out_shape=[jax.ShapeDtypeStruct((S, QW + 2 * KVW), BF16),
                   jax.ShapeDtypeStruct((H, CHUNK, 2 * CHUNK), F32),
                   jax.ShapeDtypeStruct((1, LANES), F32), jax.ShapeDtypeStruct((1, LANES), F32),
                   jax.ShapeDtypeStruct((1, LANES), F32)],
        scratch_shapes=[pltpu.VMEM((CHUNK, 2 * KVW), F32), pltpu.VMEM((2 * CHUNK, 2 * KVW), F32),
                        pltpu.VMEM((CHUNK, LANES), F32), pltpu.VMEM((8, LANES), F32)],
        sem=("arbitrary",), args=(qkv, qkv, do, qg2, kg2, sinks, bias), comm=comm)


def _loss_head(y, target, tm=512):
    S, D = y.shape
    tm = min(tm, S)

    def body(y_ref, t_ref, dy_ref, l_ref):
        @pl.when(pl.program_id(0) == 0)
        def _():
            l_ref[...] = jnp.zeros_like(l_ref)

        e = y_ref[...] - t_ref[...]
        dy_ref[...] = e * (1.0 / D)
        rows = jnp.sum(e * e, axis=-1, keepdims=True) * (1.0 / D)
        l_ref[...] += 0.5 * jnp.sum(rows, axis=0, keepdims=True)

    dy, l = pl.pallas_call(
        body, name="loss_head", grid=(S // tm,),
        in_specs=[pl.BlockSpec((tm, D), lambda i: (i, 0)), pl.BlockSpec((tm, D), lambda i: (i, 0))],
        out_specs=[pl.BlockSpec((tm, D), lambda i: (i, 0)), pl.BlockSpec((1, 1), lambda i: (0, 0))],
        out_shape=[jax.ShapeDtypeStruct((S, D), F32), jax.ShapeDtypeStruct((1, 1), F32)],
        compiler_params=_cp(("arbitrary",)),
    )(y, target)
    return dy, l[0, 0]


def _row_tile(rows, cols, n_arrays):
    budget = VMEM_LIMIT_V7X // 4 // (2 * n_arrays * 4 * cols)
    best = 8
    for n in range(1, rows // 8 + 1):
        if rows % n == 0 and (rows // n) % 8 == 0 and rows // n <= budget:
            best = rows // n
            break
    return best


def _adamw(g, w, m, v, name, comm=None):
    R, C = g.shape
    tr = _row_tile(R, C, 7)

    def body(g_ref, w_ref, m_ref, v_ref, d_ref, mo_ref, vo_ref):
        gg = g_ref[...]
        mn = ADAM_B1 * m_ref[...] + (1.0 - ADAM_B1) * gg
        vn = ADAM_B2 * v_ref[...] + (1.0 - ADAM_B2) * jnp.square(gg)
        m_hat = mn / (1.0 - ADAM_B1 ** ADAM_STEP)
        v_hat = vn / (1.0 - ADAM_B2 ** ADAM_STEP)
        d_ref[...] = -ADAM_LR * (m_hat / (jnp.sqrt(v_hat) + ADAM_EPS) + ADAM_WD * w_ref[...])
        mo_ref[...] = mn
        vo_ref[...] = vn

    spec = pl.BlockSpec((tr, C), lambda i: (i, 0))
    return _call(
        body, name=name, grid=(R // tr,), in_specs=[spec] * 4, out_specs=[spec] * 3,
        out_shape=[jax.ShapeDtypeStruct((R, C), F32)] * 3, sem=("parallel",), args=(g, w, m, v), comm=comm)


def _place_shard(shard, where, dtype, name):
    R, C = shard.shape
    tr = _row_tile(R, C, 2) if R % 8 == 0 else R

    def body(s_ref, x_ref, o_ref):
        o_ref[0] = x_ref[...].astype(dtype)

    return pl.pallas_call(
        body, name=name,
        grid_spec=pltpu.PrefetchScalarGridSpec(
            num_scalar_prefetch=1, grid=(R // tr,),
            in_specs=[pl.BlockSpec((tr, C), lambda i, s_ref: (i, 0))],
            out_specs=pl.BlockSpec((1, tr, C), lambda i, s_ref: (s_ref[1], i, 0))),
        out_shape=jax.ShapeDtypeStruct((4, R, C), dtype),
        compiler_params=_cp(("parallel",)),
    )(where, shard)


def _pair_add(g4, rsib, where, name):
    J, R, C = g4.shape
    Rh = R // 2
    tr = _row_tile(Rh, C, 4)
    g5 = g4.reshape(J, 2, Rh, C)

    def body(s_ref, g_ref, r_ref, p_ref, q_ref):
        val = (g_ref[...].astype(F32)[0] + r_ref[...].astype(F32)).astype(BF16)
        p_ref[...] = val

        @pl.when(pl.program_id(1) == s_ref[1])
        def _():
            q_ref[...] = val

    return pl.pallas_call(
        body, name=name,
        grid_spec=pltpu.PrefetchScalarGridSpec(
            num_scalar_prefetch=1, grid=(Rh // tr, J),
            in_specs=[pl.BlockSpec((1, 1, tr, C), lambda i, j, s_ref: (j, s_ref[0], i, 0)),
                      pl.BlockSpec((1, tr, C), lambda i, j, s_ref: (j, i, 0))],
            out_specs=[pl.BlockSpec((1, tr, C), lambda i, j, s_ref: (j, i, 0)),
                       pl.BlockSpec((1, tr, C), lambda i, j, s_ref: (s_ref[1], i, 0))]),
        out_shape=[jax.ShapeDtypeStruct((J, Rh, C), BF16)] * 2,
        compiler_params=_cp(("parallel", "arbitrary")),
    )(where, g5, rsib)


def _sum_chips(q, where, dest, layer, out_shape, name):
    J, Rh, C = q.shape
    tr = _row_tile(Rh, C, 3)
    nb = Rh // tr

    def body(s_ref, q_ref, *rest):
        qq = q_ref[...].astype(F32)
        rest[-1][0] = ((qq[0] + qq[1]) + qq[2]) + qq[3]

    have = dest is not None
    return pl.pallas_call(
        body, name=name,
        grid_spec=pltpu.PrefetchScalarGridSpec(
            num_scalar_prefetch=1, grid=(nb,),
            in_specs=[pl.BlockSpec((J, tr, C), lambda i, s_ref: (0, i, 0))] + ([ANY] if have else []),
            out_specs=pl.BlockSpec((1, tr, C), lambda i, s_ref: (layer, s_ref[0] * nb + i, 0))),
        out_shape=jax.ShapeDtypeStruct(out_shape, F32),
        input_output_aliases={2: 0} if have else {},
        compiler_params=_cp(("parallel",)),
    )(*((where, q, dest) if have else (where, q)))


MESH = pl.DeviceIdType.MESH
ANY = pl.BlockSpec(memory_space=pl.ANY)


def _place():
    x, y, c = lax.axis_index("x"), lax.axis_index("y"), lax.axis_index("c")
    others = [(1 - x, y), (x, 1 - y), (1 - x, 1 - y)]
    return x, y, c, 2 * x + y, others, [2 * ox + oy for ox, oy in others]


def _gather_comm(placed, split):
    n = len(placed)

    def rows(t, ref, half):
        if not split[t]:
            return ref
        rh = placed[t].shape[1] // 2
        return ref.at[pl.ds(half * rh, rh), :]

    def sends(outs, sems):
        send, recv = sems[0], sems[1]
        x, y, c, me, others, okey = _place()
        cps = []
        for t in range(n):
            mine = rows(t, outs[t].at[me], c)
            for j, (ox, oy) in enumerate(others):
                cps.append(pltpu.make_async_remote_copy(
                    src_ref=mine, dst_ref=mine,
                    send_sem=send.at[t, j], recv_sem=recv.at[t, j], device_id=(ox, oy, c), device_id_type=MESH))
        return cps

    def start(srcs, outs, news, sems):
        for cp in sends(outs, sems):
            cp.start()

    def finish(srcs, outs, news, sems):
        send, recv, fsend, frecv = sems
        x, y, c, me, others, okey = _place()
        sib = (x, y, 1 - c)
        first = sends(outs, sems)
        passed = []
        for t in range(n):
            for j in range(3):
                landed = rows(t, outs[t].at[okey[j]], c)
                pltpu.make_async_remote_copy(
                    src_ref=landed, dst_ref=landed, send_sem=send.at[t, j], recv_sem=recv.at[t, j],
                    device_id=sib, device_id_type=MESH).wait_recv()
                if split[t]:
                    cp = pltpu.make_async_remote_copy(
                        src_ref=landed, dst_ref=landed, send_sem=fsend.at[t, j], recv_sem=frecv.at[t, j],
                        device_id=sib, device_id_type=MESH)
                    cp.start()
                    passed.append(cp)
        for t in range(n):
            if split[t]:
                for j in range(3):
                    theirs = rows(t, outs[t].at[okey[j]], 1 - c)
                    pltpu.make_async_remote_copy(
                        src_ref=theirs, dst_ref=theirs, send_sem=fsend.at[t, j], recv_sem=frecv.at[t, j],
                        device_id=sib, device_id_type=MESH).wait_recv()
        for cp in first + passed:
            cp.wait_send()

    return _Comm([], placed, [], [pltpu.SemaphoreType.DMA((n, 3))] * 4, start, finish)


def _pair_exchange_comm(gs):
    n = len(gs)

    def copies(ins, outs, sems):
        send, recv = sems
        x, y, c, _, _, _ = _place()
        cps = []
        for t in range(n):
            rh = gs[t].shape[1] // 2
            cps.append(pltpu.make_async_remote_copy(
                src_ref=ins[t].at[:, pl.ds((1 - c) * rh, rh), :], dst_ref=outs[t],
                send_sem=send.at[t], recv_sem=recv.at[t], device_id=(x, y, 1 - c), device_id_type=MESH))
        return cps

    def start(ins, zones, outs, sems):
        for cp in copies(ins, outs, sems):
            cp.start()

    def finish(ins, zones, outs, sems):
        for cp in copies(ins, outs, sems):
            cp.wait()

    news = [jax.ShapeDtypeStruct((4, g.shape[1] // 2, g.shape[2]), g.dtype) for g in gs]
    return _Comm(gs, [], news, [pltpu.SemaphoreType.DMA((n,))] * 2, start, finish)


def _chip_scatter_comm(ps, qs):
    n = len(ps)

    def sends(ins, outs, sems):
        send, recv = sems
        x, y, c, me, others, okey = _place()
        return [pltpu.make_async_remote_copy(
            src_ref=ins[t].at[okey[j]], dst_ref=outs[t].at[me],
            send_sem=send.at[t, j], recv_sem=recv.at[t, j], device_id=(ox, oy, c), device_id_type=MESH)
            for t in range(n) for j, (ox, oy) in enumerate(others)]

    def start(ins, outs, news, sems):
        for cp in sends(ins, outs, sems):
            cp.start()

    def finish(ins, outs, news, sems):
        send, recv = sems
        x, y, c, me, others, okey = _place()
        for t in range(n):
            for j in range(3):
                slot = outs[t].at[okey[j]]
                pltpu.make_async_remote_copy(
                    src_ref=slot, dst_ref=slot, send_sem=send.at[t, j], recv_sem=recv.at[t, j],
                    device_id=(x, y, c), device_id_type=MESH).wait_recv()
        for cp in sends(ins, outs, sems):
            cp.wait_send()

    return _Comm(ps, qs, [], [pltpu.SemaphoreType.DMA((n, 3))] * 2, start, finish)


def _half_exchange_comm(arrs, layers=None):
    n = len(arrs)
    items = [(t, layer) for t in range(n) for layer in (range(arrs[t].shape[0]) if layers is None else layers[t])]

    def sends(outs, sems):
        send, recv = sems
        x, y, c, _, _, _ = _place()
        cps = []
        for k, (t, layer) in enumerate(items):
            rh = arrs[t].shape[1] // 2
            mine = outs[t].at[layer, pl.ds(c * rh, rh), :]
            cps.append(pltpu.make_async_remote_copy(
                src_ref=mine, dst_ref=mine, send_sem=send.at[k], recv_sem=recv.at[k],
                device_id=(x, y, 1 - c), device_id_type=MESH))
        return cps

    def start(srcs, outs, news, sems):
        for cp in sends(outs, sems):
            cp.start()

    def finish(srcs, outs, news, sems):
        send, recv = sems
        x, y, c, _, _, _ = _place()
        for k, (t, layer) in enumerate(items):
            rh = arrs[t].shape[1] // 2
            theirs = outs[t].at[layer, pl.ds((1 - c) * rh, rh), :]
            pltpu.make_async_remote_copy(
                src_ref=theirs, dst_ref=theirs, send_sem=send.at[k], recv_sem=recv.at[k],
                device_id=(x, y, 1 - c), device_id_type=MESH).wait_recv()
        for cp in sends(outs, sems):
            cp.wait_send()

    return _Comm([], arrs, [], [pltpu.SemaphoreType.DMA((len(items),))] * 2, start, finish)


SMALL_COLS = 1024
SMALL_PIECE_ROWS = 40


def _allreduce_small(buf, comm=None):
    pr = SMALL_PIECE_ROWS
    flips = [(d >> 2 & 1, d >> 1 & 1, d & 1) for d in range(1, 8)]

    def body(x_ref, o_ref, rbuf, send1, recv1, send2, recv2):
        x, y, c = lax.axis_index("x"), lax.axis_index("y"), lax.axis_index("c")
        me = 4 * x + 2 * y + c
        peers = [(x ^ fx, y ^ fy, c ^ fc) for fx, fy, fc in flips]
        pid = [4 * px + 2 * py + pc for px, py, pc in peers]

        def piece(ref, p):
            return ref.at[pl.ds(pl.multiple_of(p * pr, 8), pr), :]

        cps = []
        for d in range(7):
            cp = pltpu.make_async_remote_copy(
                src_ref=piece(x_ref, pid[d]), dst_ref=rbuf.at[d + 1],
                send_sem=send1.at[d], recv_sem=recv1.at[d], device_id=peers[d], device_id_type=MESH)
            cp.start()
            cps.append(cp)
        acc = piece(x_ref, me)[...]
        for d in range(7):
            cps[d].wait_recv()
            acc = acc + rbuf[d + 1]
        piece(o_ref, me)[...] = acc
        out = []
        for d in range(7):
            cp = pltpu.make_async_remote_copy(
                src_ref=piece(o_ref, me), dst_ref=piece(o_ref, me),
                send_sem=send2.at[d], recv_sem=recv2.at[d], device_id=peers[d], device_id_type=MESH)
            cp.start()
            out.append(cp)
        for d in range(7):
            pltpu.make_async_remote_copy(
                src_ref=piece(o_ref, pid[d]), dst_ref=piece(o_ref, pid[d]),
                send_sem=send2.at[d], recv_sem=recv2.at[d], device_id=peers[d], device_id_type=MESH).wait_recv()
        for cp in cps + out:
            cp.wait_send()

    vm = pl.BlockSpec(memory_space=pltpu.VMEM)
    return _call(
        body, name="small_allreduce", grid=(), in_specs=[vm], out_specs=vm,
        out_shape=jax.ShapeDtypeStruct(buf.shape, F32),
        scratch_shapes=[pltpu.VMEM((8, pr, SMALL_COLS), F32)] + [pltpu.SemaphoreType.DMA((7,))] * 4,
        args=(buf,), comm=comm)


def _pack(arrays, rows):
    flat = jnp.concatenate([a.reshape(-1) for a in arrays])
    return jnp.pad(flat, (0, rows * SMALL_COLS - flat.shape[0])).reshape(rows, SMALL_COLS)


def _unpack(buf, shapes):
    flat = buf.reshape(-1)
    out, off = [], 0
    for s in shapes:
        k = math.prod(s)
        out.append(flat[off:off + k].reshape(s))
        off += k
    return out


BIG = ["sgu_w_in", "sgu_w_out", "attn_w_qkv", "attn_w_o", "ffn_w_up", "ffn_w_down"]
SMALL = ["mix_norm", "ffn_norm", "sgu_v_gain", "sgu_w_s", "sgu_b_s", "attn_q_gain", "attn_k_gain", "attn_sinks",
         "rel_bias", "ffn_conv_b"]
ORDER = ["mix_norm", "ffn_norm", "sgu_w_in", "sgu_v_gain", "sgu_w_s", "sgu_b_s", "sgu_w_out", "attn_w_qkv",
         "attn_q_gain", "attn_k_gain", "attn_sinks", "attn_w_o", "rel_bias", "ffn_w_up", "ffn_conv_w", "ffn_conv_b",
         "ffn_w_down"]


def kernel(x, mix_norm, ffn_norm, sgu_w_in, sgu_v_gain, sgu_w_s, sgu_b_s, sgu_w_out, attn_w_qkv, attn_q_gain, attn_k_gain, attn_sinks, attn_w_o, rel_bias, ffn_w_up, ffn_conv_w, ffn_conv_b, ffn_w_down, loss_target, m_mix_norm, m_ffn_norm, m_sgu_w_in, m_sgu_v_gain, m_sgu_w_s, m_sgu_b_s, m_sgu_w_out, m_attn_w_qkv, m_attn_q_gain, m_attn_k_gain, m_attn_sinks, m_attn_w_o, m_rel_bias, m_ffn_w_up, m_ffn_conv_w, m_ffn_conv_b, m_ffn_w_down, v_mix_norm, v_ffn_norm, v_sgu_w_in, v_sgu_v_gain, v_sgu_w_s, v_sgu_b_s, v_sgu_w_out, v_attn_w_qkv, v_attn_q_gain, v_attn_k_gain, v_attn_sinks, v_attn_w_o, v_rel_bias, v_ffn_w_up, v_ffn_conv_w, v_ffn_conv_b, v_ffn_w_down):
    w = dict(mix_norm=mix_norm, ffn_norm=ffn_norm, sgu_w_in=sgu_w_in, sgu_v_gain=sgu_v_gain, sgu_w_s=sgu_w_s,
             sgu_b_s=sgu_b_s, sgu_w_out=sgu_w_out, attn_w_qkv=attn_w_qkv, attn_q_gain=attn_q_gain,
             attn_k_gain=attn_k_gain, attn_sinks=attn_sinks, attn_w_o=attn_w_o, rel_bias=rel_bias, ffn_w_up=ffn_w_up,
             ffn_conv_w=ffn_conv_w, ffn_conv_b=ffn_conv_b, ffn_w_down=ffn_w_down)
    mom = dict(mix_norm=m_mix_norm, ffn_norm=m_ffn_norm, sgu_w_in=m_sgu_w_in, sgu_v_gain=m_sgu_v_gain,
               sgu_w_s=m_sgu_w_s, sgu_b_s=m_sgu_b_s, sgu_w_out=m_sgu_w_out, attn_w_qkv=m_attn_w_qkv,
               attn_q_gain=m_attn_q_gain, attn_k_gain=m_attn_k_gain, attn_sinks=m_attn_sinks, attn_w_o=m_attn_w_o,
               rel_bias=m_rel_bias, ffn_w_up=m_ffn_w_up, ffn_conv_w=m_ffn_conv_w, ffn_conv_b=m_ffn_conv_b,
               ffn_w_down=m_ffn_w_down)
    var = dict(mix_norm=v_mix_norm, ffn_norm=v_ffn_norm, sgu_w_in=v_sgu_w_in, sgu_v_gain=v_sgu_v_gain,
               sgu_w_s=v_sgu_w_s, sgu_b_s=v_sgu_b_s, sgu_w_out=v_sgu_w_out, attn_w_qkv=v_attn_w_qkv,
               attn_q_gain=v_attn_q_gain, attn_k_gain=v_attn_k_gain, attn_sinks=v_attn_sinks, attn_w_o=v_attn_w_o,
               rel_bias=v_rel_bias, ffn_w_up=v_ffn_w_up, ffn_conv_w=v_ffn_conv_w, ffn_conv_b=v_ffn_conv_b,
               ffn_w_down=v_ffn_w_down)
    chip = 2 * lax.axis_index("x") + lax.axis_index("y")
    core = lax.axis_index("c")

    where = jnp.stack([core, chip]).astype(jnp.int32)
    names = ["sgu_w_in", "sgu_w_out", "attn_w_qkv", "attn_w_o", "ffn_w_up0", "ffn_w_up1", "ffn_w_down0", "ffn_w_down1"]
    shards = [sgu_w_in[0], sgu_w_out[0], attn_w_qkv[0], attn_w_o[0], ffn_w_up[0], ffn_w_up[1],
              ffn_w_down[0], ffn_w_down[1]]
    T = {nm: _place_shard(s, where, BF16, "place_" + nm) for s, nm in zip(shards, names)}
    for l in range(2):
        T["conv_w%d" % l] = _place_shard(ffn_conv_w[l], where, F32, "place_conv_w%d" % l)

    def gather(keys):
        return _gather_comm([T[k] for k in keys], [not k.startswith("conv") for k in keys])

    def gathered(keys, res):
        for k, a in zip(keys, res[0]):
            T[k] = a

    D = x.shape[2]
    first = ["sgu_w_in", "conv_w0", "conv_w1"]
    gathered(first, _run_comm(gather(first), "gather_first"))
    unshard_cols = lambda a: jnp.transpose(a, (1, 0, 2)).reshape(a.shape[1], -1)
    cw = [unshard_cols(T["conv_w0"]), unshard_cols(T["conv_w1"])]
    cb = ffn_conv_b
    flat = lambda k: T[k].reshape(-1, D)
    x2, target = x[0], loss_target[0]
    bucket = jnp.asarray(_rel_buckets_flat())
    wtril, wtrilT = _sgu_prep(sgu_w_s[0])
    bT = sgu_b_s[0].T
    bias = _rel_bias_expand(rel_bias, bucket).reshape(N_HEADS, CHUNK, 2 * CHUNK)
    qg2 = jnp.tile(attn_q_gain, (1, 2))
    kg2 = jnp.tile(attn_k_gain, (1, 2))
    sinks = attn_sinks.reshape(N_HEADS)
    mix0, mix1 = mix_norm[0:1], mix_norm[1:2]
    fn0, fn1 = ffn_norm[0:1], ffn_norm[1:2]

    ks = ["sgu_w_out", "attn_w_qkv"]
    (hn0, z), r = _norm_matmul(x2, mix0, T["sgu_w_in"], "sgu_in", comm=gather(ks))
    gathered(ks, r)
    ks = ["ffn_w_up0"]
    (yp, h1), r = _sgu_fwd(z, x2, sgu_v_gain, wtril, bT, flat("sgu_w_out"), comm=gather(ks))
    gathered(ks, r)
    ks = ["ffn_w_down0", "attn_w_o"]
    (hf0, a0), r = _norm_matmul(h1, fn0, T["ffn_w_up0"], "ffn0_up", comm=gather(ks))
    gathered(ks, r)
    ks = ["ffn_w_up1"]
    (f0, h2), r = _ffn_fwd(a0, h1, cw[0], cb[0:1], flat("ffn_w_down0"), "ffn0_fwd", comm=gather(ks))
    gathered(ks, r)
    (hn1, qkv), _ = _norm_matmul(h2, mix1, T["attn_w_qkv"], "attn_qkv")
    ks = ["ffn_w_down1"]
    o, r = _attn_fwd(qkv, qg2, kg2, sinks, bias, comm=gather(ks))
    gathered(ks, r)
    h3, _ = _matmul_res(o, flat("attn_w_o"), h2, "attn_out")
    (hf1, a1), _ = _norm_matmul(h3, fn1, T["ffn_w_up1"], "ffn1_up")
    (f1, h4), _ = _ffn_fwd(a1, h3, cw[1], cb[1:2], flat("ffn_w_down1"), "ffn1_fwd")
    dh4, loss_local = _loss_head(h4, target)
    loss = lax.psum(loss_local, ("x", "y", "c"))

    G, RS, PQ, QD, halves, grads = {}, {}, {}, {}, {}, {}
    dest_of = {"sgu_w_in": ("sgu_w_in", 0), "sgu_w_out": ("sgu_w_out", 0), "attn_w_qkv": ("attn_w_qkv", 0),
               "attn_w_o": ("attn_w_o", 0), "ffn_w_up0": ("ffn_w_up", 0), "ffn_w_up1": ("ffn_w_up", 1),
               "ffn_w_down0": ("ffn_w_down", 0), "ffn_w_down1": ("ffn_w_down", 1)}

    def px(keys):
        return _pair_exchange_comm([G[k] for k in keys])

    def px_done(keys, res):
        for k, a in zip(keys, res[1]):
            RS[k] = a
            PQ[k] = _pair_add(G[k], a, where, "pair_add_" + k)

    def sc(keys):
        return _chip_scatter_comm([PQ[k][0] for k in keys], [PQ[k][1] for k in keys])

    def sc_done(keys, res):
        for k, a in zip(keys, res[0]):
            wk, layer = dest_of[k]
            halves[wk] = _sum_chips(a, where, halves.get(wk), layer, w[wk].shape, "sum_chips_" + k)

    nup = ffn_w_up.shape[2]
    ndown = ffn_w_down.shape[1]
    rows4 = lambda a: a.reshape(4, ndown, D)
    (dc1, dcw1, dcb1), _ = _ffn_bwd_dc(dh4, a1, cw[1], cb[1:2], flat("ffn_w_down1"), "ffn1_bwd_dc")
    gw, _ = _matmul_tn(f1, dh4, "ffn1_dw_down", ka=2 * ndown, nb=D, out_dtype=BF16)
    G["ffn_w_down1"] = rows4(gw)
    (dh3, dfn1, da1), r = _bwd_norm(dc1, T["ffn_w_up1"], h3, fn1, dh4, "ffn1_bwd_in", conv_w=cw[1],
                                    comm=px(["ffn_w_down1"]))
    px_done(["ffn_w_down1"], r)
    G["ffn_w_up1"], r = _matmul_tn(hf1, da1, "ffn1_dw_up", ka=D, nb=nup, out_dtype=BF16, comm=sc(["ffn_w_down1"]))
    sc_done(["ffn_w_down1"], r)
    G["attn_w_o"], r = _matmul_tn(o, dh3, "attn_dw_o", ka=QW // 4, nb=D, out_dtype=BF16, comm=px(["ffn_w_up1"]))
    px_done(["ffn_w_up1"], r)
    do, r = _matmul_nt(dh3, flat("attn_w_o"), "attn_bwd_out", comm=px(["attn_w_o"]))
    px_done(["attn_w_o"], r)
    ks = ["ffn_w_up1", "attn_w_o"]
    (dqkv, dbias, dqg, dkg, dsink), r = _attn_bwd(qkv, do, qg2, kg2, sinks, bias, comm=sc(ks))
    sc_done(ks, r)
    G["attn_w_qkv"], _ = _matmul_tn(hn1, dqkv, "attn_dw_qkv", ka=D, nb=dqkv.shape[1] // 4, out_dtype=BF16)
    (dh2, dmix1), r = _bwd_norm(dqkv, T["attn_w_qkv"], h2, mix1, dh3, "attn_bwd_in", comm=px(["attn_w_qkv"]))
    px_done(["attn_w_qkv"], r)
    (dc0, dcw0, dcb0), r = _ffn_bwd_dc(dh2, a0, cw[0], cb[0:1], flat("ffn_w_down0"), "ffn0_bwd_dc",
                                       comm=sc(["attn_w_qkv"]))
    sc_done(["attn_w_qkv"], r)
    gw, _ = _matmul_tn(f0, dh2, "ffn0_dw_down", ka=2 * ndown, nb=D, out_dtype=BF16)
    G["ffn_w_down0"] = rows4(gw)
    (dh1, dfn0, da0), r = _bwd_norm(dc0, T["ffn_w_up0"], h1, fn0, dh2, "ffn0_bwd_in", conv_w=cw[0],
                                    comm=px(["ffn_w_down0"]))
    px_done(["ffn_w_down0"], r)
    G["ffn_w_up0"], r = _matmul_tn(hf0, da0, "ffn0_dw_up", ka=D, nb=nup, out_dtype=BF16, comm=sc(["ffn_w_down0"]))
    sc_done(["ffn_w_down0"], r)
    G["sgu_w_out"], r = _matmul_tn(yp, dh1, "sgu_dw_out", ka=yp.shape[1] // 4, nb=D, out_dtype=BF16,
                                   comm=px(["ffn_w_up0"]))
    px_done(["ffn_w_up0"], r)
    both = [sc(["ffn_w_up0"]), px(["sgu_w_out"])]
    (dz, dws, dbT, dvg), r = _sgu_bwd(dh1, z, sgu_v_gain, wtril, wtrilT, bT, flat("sgu_w_out"), comm=_join(both))
    r = _split(both, r)
    sc_done(["ffn_w_up0"], r[0])
    px_done(["sgu_w_out"], r[1])
    G["sgu_w_in"], r = _matmul_tn(hn0, dz, "sgu_dw_in", ka=D, nb=dz.shape[1] // 4, out_dtype=BF16,
                                  comm=sc(["sgu_w_out"]))
    sc_done(["sgu_w_out"], r)
    (grad_x, dmix0), r = _bwd_norm(dz, T["sgu_w_in"], x2, mix0, dh1, "sgu_bwd_in", comm=px(["sgu_w_in"]))
    px_done(["sgu_w_in"], r)

    g = dict(mix_norm=jnp.concatenate([dmix0, dmix1], axis=0), ffn_norm=jnp.concatenate([dfn0, dfn1], axis=0),
             sgu_v_gain=dvg, sgu_w_s=dws, sgu_b_s=dbT.T, attn_q_gain=dqg[:, :HEAD_DIM], attn_k_gain=dkg[:, :HEAD_DIM],
             attn_sinks=dsink[:, :N_HEADS],
             rel_bias=_rel_bias_reduce(dbias.reshape(N_HEADS, CHUNK * 2 * CHUNK), bucket),
             ffn_conv_b=jnp.concatenate([dcb0, dcb1], axis=0))
    small_list = [g[k].reshape(w[k].shape) for k in SMALL] + [jnp.stack([dcw0, dcw1])]
    small_shapes = [a.shape for a in small_list]
    done = ["sgu_w_out", "attn_w_qkv", "attn_w_o", "ffn_w_up", "ffn_w_down"]
    both = [sc(["sgu_w_in"]), _half_exchange_comm([halves[k] for k in done])]
    red, r = _allreduce_small(_pack(small_list, 8 * SMALL_PIECE_ROWS), comm=_join(both))
    r = _split(both, r)
    sc_done(["sgu_w_in"], r[0])
    for k, a in zip(done, r[1][0]):
        grads[k] = a
    red = _unpack(red, small_shapes)
    for k, a in zip(SMALL, red[:-1]):
        grads[k] = a
    grads["ffn_conv_w"] = lax.dynamic_slice_in_dim(red[-1], chip * ffn_conv_w.shape[2], ffn_conv_w.shape[2], axis=2)

    delta, new_m, new_v = {}, {}, {}
    two = lambda a: a.reshape(-1, a.shape[-1])
    last = _half_exchange_comm([halves["sgu_w_in"]])
    for k in ["ffn_w_up", "ffn_w_down", "sgu_w_out", "attn_w_qkv", "attn_w_o", "sgu_w_in"]:
        (d2, m2, v2), r = _adamw(two(grads[k]), two(w[k]), two(mom[k]), two(var[k]), "adamw_" + k,
                                 comm=last if k == "ffn_w_up" else None)
        if k == "ffn_w_up":
            grads["sgu_w_in"] = r[0][0]
        delta[k], new_m[k], new_v[k] = (a.reshape(w[k].shape) for a in (d2, m2, v2))
    sm = SMALL + ["ffn_conv_w"]
    sm_shapes = [w[k].shape for k in sm]
    n_el = sum(math.prod(s) for s in sm_shapes)
    rows = -(-n_el // (8 * SMALL_COLS)) * 8
    (d2, m2, v2), _ = _adamw(_pack([grads[k] for k in sm], rows), _pack([w[k] for k in sm], rows),
                             _pack([mom[k] for k in sm], rows), _pack([var[k] for k in sm], rows), "adamw_small")
    for dst, buf in ((delta, d2), (new_m, m2), (new_v, v2)):
        for k, a in zip(sm, _unpack(buf, sm_shapes)):
            dst[k] = a

    return (loss, grad_x[None], *[grads[k] for k in ORDER], *[delta[k] for k in ORDER],
            *[new_m[k] for k in ORDER], *[new_v[k] for k in ORDER])
```

```python
import functools
import math

import numpy as np
import jax
import jax.numpy as jnp
from jax import lax
from jax.experimental import pallas as pl
from jax.experimental.pallas import tpu as pltpu

F32 = jnp.float32
BF16 = jnp.bfloat16

EPS = 1e-6
CHUNK = 128
SGU_GROUPS = 16
HEAD_DIM = 64
N_HEADS = 16
N_KV_HEADS = 4
KV_GROUP = N_HEADS // N_KV_HEADS
REL_BUCKETS = 32
REL_MAX_DIST = 128
LANES = 128
HALO = 16

ADAM_LR = 0.001
ADAM_B1 = 0.9
ADAM_B2 = 0.999
ADAM_EPS = 1e-08
ADAM_WD = 0.01
ADAM_STEP = 10

VMEM_LIMIT_V7X = 56 * 1024 * 1024

_SQRT_HALF = math.sqrt(0.5)
_INV_SQRT_2PI = 1.0 / math.sqrt(2.0 * math.pi)


def _cp(sem):
    return pltpu.CompilerParams(dimension_semantics=sem, vmem_limit_bytes=VMEM_LIMIT_V7X)


def _resident(shape):
    nd = len(shape)
    return pl.BlockSpec(shape, lambda *_: (0,) * nd, pipeline_mode=pl.Buffered(1))


class _Comm:
    def __init__(self, srcs, zones, news, sems, start, finish):
        self.srcs, self.zones, self.news, self.sems = list(srcs), list(zones), list(news), list(sems)
        self.start, self.finish = start, finish


def _join(comms):
    comms = [c for c in comms if c is not None]
    if not comms:
        return None

    def part(seq, attr):
        out, k = [], 0
        for c in comms:
            n = len(getattr(c, attr))
            out.append(seq[k:k + n])
            k += n
        return out

    def run(which):
        def f(srcs, zones, news, sems):
            for c, a, b, d, e in zip(comms, part(srcs, "srcs"), part(zones, "zones"), part(news, "news"), part(sems, "sems")):
                getattr(c, which)(a, b, d, e)
        return f

    cat = lambda attr: [v for c in comms for v in getattr(c, attr)]
    return _Comm(cat("srcs"), cat("zones"), cat("news"), cat("sems"), run("start"), run("finish"))


def _split(comms, res):
    zones, news = res
    out, kz, kn = [], 0, 0
    for c in comms:
        out.append((zones[kz:kz + len(c.zones)], news[kn:kn + len(c.news)]))
        kz += len(c.zones)
        kn += len(c.news)
    return out


def _call(body, *, name, grid, in_specs, out_specs, out_shape, args, scratch_shapes=(), sem=None, comm=None):
    if comm is None:
        res = pl.pallas_call(body, name=name, grid=grid, in_specs=in_specs, out_specs=out_specs, out_shape=out_shape,
                             scratch_shapes=list(scratch_shapes), compiler_params=_cp(sem))(*args)
        return res, None
    single = not isinstance(out_shape, (list, tuple))
    out_specs_l = [out_specs] if single else list(out_specs)
    out_shape_l = [out_shape] if single else list(out_shape)
    n_in, n_out, n_scr = len(in_specs), len(out_shape_l), len(scratch_shapes)
    ns, nz, nn = len(comm.srcs), len(comm.zones), len(comm.news)

    def wrapped(*refs):
        k = n_in
        ins, srcs = refs[:k], refs[k:k + ns]
        k += ns + nz
        outs, zones, news = refs[k:k + n_out], refs[k + n_out:k + n_out + nz], refs[k + n_out + nz:k + n_out + nz + nn]
        k += n_out + nz + nn
        scr, sems = refs[k:k + n_scr], refs[k + n_scr:]
        if not grid:
            comm.start(srcs, zones, news, sems)
            body(*ins, *outs, *scr)
            comm.finish(srcs, zones, news, sems)
            return
        first = functools.reduce(jnp.logical_and, [pl.program_id(a) == 0 for a in range(len(grid))])
        last = functools.reduce(jnp.logical_and, [pl.program_id(a) == grid[a] - 1 for a in range(len(grid))])

        @pl.when(first)
        def _():
            comm.start(srcs, zones, news, sems)

        body(*ins, *outs, *scr)

        @pl.when(last)
        def _():
            comm.finish(srcs, zones, news, sems)

    res = pl.pallas_call(
        wrapped, name=name, grid=grid,
        in_specs=list(in_specs) + [ANY] * (ns + nz), out_specs=out_specs_l + [ANY] * (nz + nn),
        out_shape=out_shape_l + [jax.ShapeDtypeStruct(z.shape, z.dtype) for z in comm.zones] + comm.news,
        input_output_aliases={n_in + ns + i: n_out + i for i in range(nz)},
        scratch_shapes=list(scratch_shapes) + comm.sems,
        compiler_params=_cp(("arbitrary",) * len(grid)),
    )(*args, *comm.srcs, *comm.zones)
    main = res[0] if single else list(res[:n_out])
    return main, (list(res[n_out:n_out + nz]), list(res[n_out + nz:]))


def _run_comm(comm, name):
    ns, nz, nn = len(comm.srcs), len(comm.zones), len(comm.news)

    def body(*refs):
        srcs, zones, news, sems = refs[:ns], refs[ns + nz:ns + 2 * nz], refs[ns + 2 * nz:ns + 2 * nz + nn], refs[ns + 2 * nz + nn:]
        comm.start(srcs, zones, news, sems)
        comm.finish(srcs, zones, news, sems)

    res = pl.pallas_call(
        body, name=name, in_specs=[ANY] * (ns + nz), out_specs=[ANY] * (nz + nn),
        out_shape=[jax.ShapeDtypeStruct(z.shape, z.dtype) for z in comm.zones] + comm.news,
        input_output_aliases={ns + i: i for i in range(nz)}, scratch_shapes=comm.sems,
    )(*comm.srcs, *comm.zones)
    return list(res[:nz]), list(res[nz:])


def _dot(a, b):
    return jnp.dot(a, b, preferred_element_type=F32)


def _dot_nt(a, b):
    return lax.dot_general(a, b, (((1,), (1,)), ((), ())), preferred_element_type=F32)


def _dot_tn(a, b):
    return lax.dot_general(a, b, (((0,), (0,)), ((), ())), preferred_element_type=F32)


def _gelu(x):
    return 0.5 * x * (1.0 + lax.erf(x * _SQRT_HALF))


def _gelu_and_grad(x):
    cdf = 0.5 * (1.0 + lax.erf(x * _SQRT_HALF))
    return x * cdf, cdf + x * jnp.exp(-0.5 * x * x) * _INV_SQRT_2PI


def _rms_bwd(dy, x, gain):
    r = lax.rsqrt(jnp.mean(x * x, axis=-1, keepdims=True) + EPS)
    xhat = x * r
    gdy = dy * gain
    dx = r * (gdy - xhat * jnp.mean(gdy * xhat, axis=-1, keepdims=True))
    return dx, jnp.sum(dy * xhat, axis=0, keepdims=True)


def _norm_matmul(x, gain, w4, name, comm=None, tm=512):
    S, D = x.shape
    nsh, _, ns = w4.shape
    tm = min(tm, S)

    def body(x_ref, g_ref, w_ref, hn_ref, o_ref):
        xf = x_ref[...]
        r = lax.rsqrt(jnp.mean(xf * xf, axis=-1, keepdims=True) + EPS)
        hn = (xf * r * g_ref[...]).astype(BF16)
        hn_ref[...] = hn
        for j in range(nsh):
            o_ref[:, j * ns:(j + 1) * ns] = _dot(hn, w_ref[j]).astype(BF16)

    return _call(
        body, name=name, grid=(S // tm,),
        in_specs=[pl.BlockSpec((tm, D), lambda i: (i, 0)), _resident((1, D)), _resident(w4.shape)],
        out_specs=[pl.BlockSpec((tm, D), lambda i: (i, 0)), pl.BlockSpec((tm, nsh * ns), lambda i: (i, 0))],
        out_shape=[jax.ShapeDtypeStruct((S, D), BF16), jax.ShapeDtypeStruct((S, nsh * ns), BF16)],
        sem=("parallel",), args=(x, gain, w4), comm=comm)


def _matmul_res(a, w, res, name, comm=None, tm=512):
    S, K = a.shape
    N = w.shape[1]
    tm = min(tm, S)

    def body(a_ref, w_ref, r_ref, o_ref):
        o_ref[...] = r_ref[...] + _dot(a_ref[...], w_ref[...])

    return _call(
        body, name=name, grid=(S // tm,),
        in_specs=[pl.BlockSpec((tm, K), lambda i: (i, 0)), _resident(w.shape), pl.BlockSpec((tm, N), lambda i: (i, 0))],
        out_specs=pl.BlockSpec((tm, N), lambda i: (i, 0)),
        out_shape=jax.ShapeDtypeStruct((S, N), F32),
        sem=("parallel",), args=(a, w, res), comm=comm)


def _matmul_nt(dh, w, name, comm=None, tm=512):
    S, N = dh.shape
    K = w.shape[0]
    tm = min(tm, S)

    def body(d_ref, w_ref, o_ref):
        o_ref[...] = _dot_nt(d_ref[...].astype(BF16), w_ref[...]).astype(BF16)

    return _call(
        body, name=name, grid=(S // tm,),
        in_specs=[pl.BlockSpec((tm, N), lambda i: (i, 0)), _resident(w.shape)],
        out_specs=pl.BlockSpec((tm, K), lambda i: (i, 0)),
        out_shape=jax.ShapeDtypeStruct((S, K), BF16),
        sem=("parallel",), args=(dh, w), comm=comm)


def _matmul_tn(a, b, name, *, ka, nb, out_dtype, comm=None, ts=512):
    S, KA = a.shape
    NB = b.shape[1]
    ts = min(ts, S)
    J = max(KA // ka, NB // nb)
    a_map = (lambda j, s: (s, j)) if KA // ka > 1 else (lambda j, s: (s, 0))
    b_map = (lambda j, s: (s, j)) if NB // nb > 1 else (lambda j, s: (s, 0))
    last = S // ts - 1

    def body(a_ref, b_ref, o_ref, acc):
        s = pl.program_id(1)

        @pl.when(s == 0)
        def _():
            acc[...] = jnp.zeros_like(acc)

        acc[...] += _dot_tn(a_ref[...].astype(BF16), b_ref[...].astype(BF16))

        @pl.when(s == last)
        def _():
            o_ref[0] = acc[...].astype(out_dtype)

    return _call(
        body, name=name, grid=(J, S // ts),
        in_specs=[pl.BlockSpec((ts, ka), a_map), pl.BlockSpec((ts, nb), b_map)],
        out_specs=pl.BlockSpec((1, ka, nb), lambda j, s: (j, 0, 0)),
        out_shape=jax.ShapeDtypeStruct((J, ka, nb), out_dtype),
        scratch_shapes=[pltpu.VMEM((ka, nb), F32)],
        sem=("parallel", "arbitrary"), args=(a, b), comm=comm)


def _sgu_prep(w_s):
    G = w_s.shape[0]

    def body(w_ref, t_ref, tt_ref):
        tri = lax.broadcasted_iota(jnp.int32, (CHUNK, CHUNK), 0) >= lax.broadcasted_iota(jnp.int32, (CHUNK, CHUNK), 1)
        for g in range(G):
            t = jnp.where(tri, w_ref[g], 0.0)
            t_ref[g] = t.astype(BF16)
            tt_ref[g] = t.T.astype(BF16)

    return pl.pallas_call(
        body, name="sgu_prep",
        out_shape=[jax.ShapeDtypeStruct(w_s.shape, BF16), jax.ShapeDtypeStruct(w_s.shape, BF16)],
        compiler_params=_cp(None),
    )(w_s)


def _sgu_fwd(z, x, vg, wtril, bT, wout, comm=None, tm=256):
    S = z.shape[0]
    W = z.shape[1] // 2
    D = x.shape[1]
    tm = min(tm, S)

    def body(z_ref, x_ref, vg_ref, wt_ref, bT_ref, wo_ref, yp_ref, h_ref):
        def chunk(c, carry):
            r0 = pl.multiple_of(c * CHUNK, CHUNK)
            zc = z_ref[pl.ds(r0, CHUNK), :].astype(F32)
            u = _gelu(zc[:, :W])
            v = _gelu(zc[:, W:])
            rv = lax.rsqrt(jnp.mean(v * v, axis=-1, keepdims=True) + EPS)
            vn = (v * rv * vg_ref[...]).astype(BF16)
            for g in range(SGU_GROUPS):
                sl = slice(g * LANES, (g + 1) * LANES)
                s = _dot(wt_ref[g], vn[:, sl]) + bT_ref[:, g:g + 1]
                yp_ref[pl.ds(r0, CHUNK), sl] = (u[:, sl] * s).astype(BF16)
            return carry

        lax.fori_loop(0, tm // CHUNK, chunk, 0)
        h_ref[...] = x_ref[...] + _dot(yp_ref[...], wo_ref[...])

    return _call(
        body, name="sgu_fwd", grid=(S // tm,),
        in_specs=[pl.BlockSpec((tm, 2 * W), lambda i: (i, 0)), pl.BlockSpec((tm, D), lambda i: (i, 0)),
                  _resident((1, W)), _resident(wtril.shape), _resident(bT.shape), _resident(wout.shape)],
        out_specs=[pl.BlockSpec((tm, W), lambda i: (i, 0)), pl.BlockSpec((tm, D), lambda i: (i, 0))],
        out_shape=[jax.ShapeDtypeStruct((S, W), BF16), jax.ShapeDtypeStruct((S, D), F32)],
        sem=("parallel",), args=(z, x, vg, wtril, bT, wout), comm=comm)


def _sgu_bwd(dh, z, vg, wtril, wtrilT, bT, wout, comm=None, tm=256):
    S = z.shape[0]
    W = z.shape[1] // 2
    D = dh.shape[1]
    G = SGU_GROUPS
    tm = min(tm, S)
    last = S // tm - 1

    def body(dh_ref, z_ref, vg_ref, wt_ref, wtT_ref, bT_ref, wo_ref,
             dz_ref, dws_ref, dbT_ref, dvg_ref, dyp_s, du_s, dvn_s, dsacc):
        i = pl.program_id(0)

        @pl.when(i == 0)
        def _():
            dws_ref[...] = jnp.zeros_like(dws_ref)
            dvg_ref[...] = jnp.zeros_like(dvg_ref)
            dsacc[...] = jnp.zeros_like(dsacc)

        dyp_s[...] = _dot_nt(dh_ref[...].astype(BF16), wo_ref[...])
        tri = lax.broadcasted_iota(jnp.int32, (CHUNK, CHUNK), 0) >= lax.broadcasted_iota(jnp.int32, (CHUNK, CHUNK), 1)

        def chunk(c, carry):
            r0 = pl.multiple_of(c * CHUNK, CHUNK)
            zc = z_ref[pl.ds(r0, CHUNK), :].astype(F32)
            u, gu = _gelu_and_grad(zc[:, :W])
            v, gv = _gelu_and_grad(zc[:, W:])
            rv = lax.rsqrt(jnp.mean(v * v, axis=-1, keepdims=True) + EPS)
            vhat = v * rv
            vgain = vg_ref[...]
            vn = (vhat * vgain).astype(BF16)
            dyp = dyp_s[pl.ds(r0, CHUNK), :]
            for g in range(G):
                sl = slice(g * LANES, (g + 1) * LANES)
                vng = vn[:, sl]
                s = _dot(wt_ref[g], vng) + bT_ref[:, g:g + 1]
                ds = dyp[:, sl] * u[:, sl]
                du_s[:, sl] = dyp[:, sl] * s
                dsb = ds.astype(BF16)
                dvn_s[:, sl] = _dot(wtT_ref[g], dsb)
                dws_ref[g] += jnp.where(tri, _dot_nt(dsb, vng), 0.0)
                dsacc[g] += ds
            dvn = dvn_s[...]
            dvg_ref[...] += jnp.sum(dvn * vhat, axis=0, keepdims=True)
            gdy = dvn * vgain
            dv = rv * (gdy - vhat * jnp.mean(gdy * vhat, axis=-1, keepdims=True))
            dz_ref[pl.ds(r0, CHUNK), :W] = (du_s[...] * gu).astype(BF16)
            dz_ref[pl.ds(r0, CHUNK), W:] = (dv * gv).astype(BF16)
            return carry

        lax.fori_loop(0, tm // CHUNK, chunk, 0)

        @pl.when(i == last)
        def _():
            for g in range(G):
                dbT_ref[:, g:g + 1] = jnp.sum(dsacc[g], axis=1, keepdims=True)

    return _call(
        body, name="sgu_bwd", grid=(S // tm,),
        in_specs=[pl.BlockSpec((tm, D), lambda i: (i, 0)), pl.BlockSpec((tm, 2 * W), lambda i: (i, 0)),
                  _resident((1, W)), _resident(wtril.shape), _resident(wtrilT.shape), _resident(bT.shape),
                  _resident(wout.shape)],
        out_specs=[pl.BlockSpec((tm, 2 * W), lambda i: (i, 0)),
                   pl.BlockSpec((G, CHUNK, CHUNK), lambda i: (0, 0, 0)),
                   pl.BlockSpec((CHUNK, G), lambda i: (0, 0)),
                   pl.BlockSpec((1, W), lambda i: (0, 0))],
        out_shape=[jax.ShapeDtypeStruct((S, 2 * W), BF16), jax.ShapeDtypeStruct((G, CHUNK, CHUNK), F32),
                   jax.ShapeDtypeStruct((CHUNK, G), F32), jax.ShapeDtypeStruct((1, W), F32)],
        scratch_shapes=[pltpu.VMEM((tm, W), F32), pltpu.VMEM((CHUNK, W), F32), pltpu.VMEM((CHUNK, W), F32),
                        pltpu.VMEM((G, CHUNK, CHUNK), F32)],
        sem=("arbitrary",), args=(dh, z, vg, wtril, wtrilT, bT, wout), comm=comm)


def _conv_taps(a32, r0, R, reuse=False):
    base = r0 + HALO
    if not reuse:
        return a32[base:base + R, :], a32[base - 1:base - 1 + R, :], a32[base - 2:base - 2 + R, :]
    X = a32[base - 8:base + R, :]
    return X[8:], pltpu.roll(X, 1, 0)[8:], pltpu.roll(X, 2, 0)[8:]


def _ffn_fwd(a, h_in, cw, cb, wdown, name, comm=None, tm=256, R=64):
    S, C = a.shape
    F = C // 2
    D = h_in.shape[1]
    tm = min(tm, S)

    def body(a_ref, halo_ref, h_ref, cw_ref, cb_ref, wd_ref, f_ref, ho_ref, a32):
        i = pl.program_id(0)
        a32[0:HALO, :] = jnp.where(i > 0, halo_ref[...].astype(F32), 0.0)
        a32[HALO:, :] = a_ref[...].astype(F32)

        for r0 in range(0, tm, R):
            a0, a1, a2 = _conv_taps(a32, r0, R)
            cpre = cw_ref[0:1, :] * a2 + cw_ref[1:2, :] * a1 + cw_ref[2:3, :] * a0 + cb_ref[...]
            g = cpre[:, :F]
            f_ref[r0:r0 + R, :] = (g * jax.nn.sigmoid(g) * cpre[:, F:]).astype(BF16)
        ho_ref[...] = h_ref[...] + _dot(f_ref[...], wd_ref[...])

    hb = tm // HALO
    return _call(
        body, name=name, grid=(S // tm,),
        in_specs=[pl.BlockSpec((tm, C), lambda i: (i, 0)),
                  pl.BlockSpec((HALO, C), lambda i: (jnp.maximum(i * hb - 1, 0), 0)),
                  pl.BlockSpec((tm, D), lambda i: (i, 0)),
                  _resident((3, C)), _resident((1, C)), _resident(wdown.shape)],
        out_specs=[pl.BlockSpec((tm, F), lambda i: (i, 0)), pl.BlockSpec((tm, D), lambda i: (i, 0))],
        out_shape=[jax.ShapeDtypeStruct((S, F), BF16), jax.ShapeDtypeStruct((S, D), F32)],
        scratch_shapes=[pltpu.VMEM((HALO + tm, C), F32)],
        sem=("parallel",), args=(a, a, h_in, cw, cb, wdown), comm=comm)


def _ffn_bwd_dc(dh, a, cw, cb, wdown, name, comm=None, tm=256, R=64):
    S, C = a.shape
    F = C // 2
    D = dh.shape[1]
    tm = min(tm, S)

    def body(dh_ref, a_ref, halo_ref, cw_ref, cb_ref, wd_ref, dc_ref, dcw_ref, dcb_ref, a32, df_s, acc):
        i = pl.program_id(0)

        @pl.when(i == 0)
        def _():
            acc[...] = jnp.zeros_like(acc)

        a32[0:HALO, :] = jnp.where(i > 0, halo_ref[...].astype(F32), 0.0)
        a32[HALO:, :] = a_ref[...].astype(F32)
        df_s[...] = _dot_nt(dh_ref[...].astype(BF16), wd_ref[...])
        rows8 = lambda v: functools.reduce(jnp.add, [v[8 * k:8 * k + 8] for k in range(R // 8)])

        for r0 in range(0, tm, R):
            a0, a1, a2 = _conv_taps(a32, r0, R, reuse=True)
            cpre = cw_ref[0:1, :] * a2 + cw_ref[1:2, :] * a1 + cw_ref[2:3, :] * a0 + cb_ref[...]
            g = cpre[:, :F]
            val = cpre[:, F:]
            sg = jax.nn.sigmoid(g)
            df = df_s[r0:r0 + R, :]
            dg = df * val * (sg * (1.0 + g * (1.0 - sg)))
            dval = df * (g * sg)
            dc = jnp.concatenate([dg, dval], axis=1)
            dc_ref[r0:r0 + R, :] = dc.astype(BF16)
            acc[0] += rows8(dc * a2)
            acc[1] += rows8(dc * a1)
            acc[2] += rows8(dc * a0)
            acc[3] += rows8(dc)

        @pl.when(i == S // tm - 1)
        def _():
            for k in range(3):
                dcw_ref[k:k + 1, :] = jnp.sum(acc[k], axis=0, keepdims=True)
            dcb_ref[...] = jnp.sum(acc[3], axis=0, keepdims=True)

    hb = tm // HALO
    return _call(
        body, name=name, grid=(S // tm,),
        in_specs=[pl.BlockSpec((tm, D), lambda i: (i, 0)),
                  pl.BlockSpec((tm, C), lambda i: (i, 0)),
                  pl.BlockSpec((HALO, C), lambda i: (jnp.maximum(i * hb - 1, 0), 0)),
                  _resident((3, C)), _resident((1, C)), _resident(wdown.shape)],
        out_specs=[pl.BlockSpec((tm, C), lambda i: (i, 0)),
                   pl.BlockSpec((3, C), lambda i: (0, 0)), pl.BlockSpec((1, C), lambda i: (0, 0))],
        out_shape=[jax.ShapeDtypeStruct((S, C), BF16), jax.ShapeDtypeStruct((3, C), F32),
                   jax.ShapeDtypeStruct((1, C), F32)],
        scratch_shapes=[pltpu.VMEM((HALO + tm, C), F32), pltpu.VMEM((tm, F), F32), pltpu.VMEM((4, 8, C), F32)],
        sem=("arbitrary",), args=(dh, a, a, cw, cb, wdown), comm=comm)


def _bwd_norm(dA, w4, h_in, gain, dh_out, name, conv_w=None, comm=None, tm=256, R=64):
    S, N = dA.shape
    nsh, D, ns = w4.shape
    tm = min(tm, S)
    nt = S // tm
    conv = conv_w is not None

    def finish(src_ref, w_ref, h_ref, g_ref, dho_ref, dhi_ref, dg_ref):
        dhn = _dot_nt(src_ref[:, 0:ns], w_ref[0])
        for j in range(1, nsh):
            dhn += _dot_nt(src_ref[:, j * ns:(j + 1) * ns], w_ref[j])
        dx, dgain = _rms_bwd(dhn, h_ref[...], g_ref[...])
        dg_ref[...] += dgain
        dhi_ref[...] = dho_ref[...] + dx

    def body_plain(dA_ref, w_ref, h_ref, g_ref, dho_ref, dhi_ref, dg_ref):
        @pl.when(pl.program_id(0) == 0)
        def _():
            dg_ref[...] = jnp.zeros_like(dg_ref)

        finish(dA_ref, w_ref, h_ref, g_ref, dho_ref, dhi_ref, dg_ref)

    def body_conv(dc_ref, halo_ref, cw_ref, w_ref, h_ref, g_ref, dho_ref, dhi_ref, dg_ref, da_ref, dc32):
        i = pl.program_id(0)

        @pl.when(i == 0)
        def _():
            dg_ref[...] = jnp.zeros_like(dg_ref)

        dc32[0:tm, :] = dc_ref[...].astype(F32)
        dc32[tm:, :] = jnp.where(i < nt - 1, halo_ref[...].astype(F32), 0.0)

        for r0 in range(0, tm, R):
            d0 = dc32[r0:r0 + R, :]
            d1 = dc32[r0 + 1:r0 + 1 + R, :]
            d2 = dc32[r0 + 2:r0 + 2 + R, :]
            da = cw_ref[2:3, :] * d0 + cw_ref[1:2, :] * d1 + cw_ref[0:1, :] * d2
            da_ref[r0:r0 + R, :] = da.astype(BF16)
        finish(da_ref, w_ref, h_ref, g_ref, dho_ref, dhi_ref, dg_ref)

    row = lambda width: pl.BlockSpec((tm, width), lambda i: (i, 0))
    common_in = [_resident(w4.shape), row(D), _resident((1, D)), row(D)]
    common_out = [row(D), pl.BlockSpec((1, D), lambda i: (0, 0))]
    common_shape = [jax.ShapeDtypeStruct((S, D), F32), jax.ShapeDtypeStruct((1, D), F32)]
    if not conv:
        return _call(
            body_plain, name=name, grid=(nt,),
            in_specs=[row(N)] + common_in, out_specs=common_out, out_shape=common_shape,
            sem=("arbitrary",), args=(dA, w4, h_in, gain, dh_out), comm=comm)
    hb = tm // HALO
    nhb = S // HALO
    return _call(
        body_conv, name=name, grid=(nt,),
        in_specs=[row(N), pl.BlockSpec((HALO, N), lambda i: (jnp.minimum((i + 1) * hb, nhb - 1), 0)),
                  _resident((3, N))] + common_in,
        out_specs=common_out + [row(N)],
        out_shape=common_shape + [jax.ShapeDtypeStruct((S, N), BF16)],
        scratch_shapes=[pltpu.VMEM((tm + HALO, N), F32)],
        sem=("arbitrary",), args=(dA, dA, conv_w, w4, h_in, gain, dh_out), comm=comm)


def _rel_buckets_flat():
    q = np.arange(CHUNK)[:, None] + CHUNK
    k = np.arange(2 * CHUNK)[None, :]
    n = np.maximum(q - k, 0)
    max_exact = REL_BUCKETS // 2
    large = max_exact + (np.log(np.maximum(n, 1).astype(np.float32) / max_exact)
                         / math.log(REL_MAX_DIST / max_exact) * (REL_BUCKETS - max_exact)).astype(np.int32)
    large = np.minimum(large, REL_BUCKETS - 1)
    return np.where(n < max_exact, n, large).astype(np.int32).reshape(1, CHUNK * 2 * CHUNK)


def _split_bf16(x):
    hi = x.astype(BF16)
    return hi, (x - hi.astype(F32)).astype(BF16)


def _rel_bias_expand(rel_bias, bucket):
    B, H = rel_bias.shape
    n = bucket.shape[1]

    def body(rb_ref, bk_ref, o_ref):
        oh = (bk_ref[...] == lax.broadcasted_iota(jnp.int32, (B, n), 0)).astype(BF16)
        hi, lo = _split_bf16(rb_ref[...])
        o_ref[...] = _dot_tn(hi, oh) + _dot_tn(lo, oh)

    return pl.pallas_call(body, name="rel_bias_expand", out_shape=jax.ShapeDtypeStruct((H, n), F32),
                          compiler_params=_cp(None))(rel_bias, bucket)


def _rel_bias_reduce(dbias, bucket):
    H, n = dbias.shape
    B = REL_BUCKETS

    def body(db_ref, bk_ref, o_ref):
        oh = (bk_ref[...] == lax.broadcasted_iota(jnp.int32, (B, n), 0)).astype(BF16)
        hi, lo = _split_bf16(db_ref[...])
        o_ref[...] = _dot_nt(oh, hi) + _dot_nt(oh, lo)

    return pl.pallas_call(body, name="rel_bias_reduce", out_shape=jax.ShapeDtypeStruct((B, H), F32),
                          compiler_params=_cp(None))(dbias, bucket)


def _lo_mask(rows):
    return lax.broadcasted_iota(jnp.int32, (rows, LANES), 1) < HEAD_DIM


def _half_sums(y, lo):
    s_lo = jnp.sum(jnp.where(lo, y, 0.0), axis=-1, keepdims=True)
    s_hi = jnp.sum(jnp.where(lo, 0.0, y), axis=-1, keepdims=True)
    return jnp.where(lo, s_lo, s_hi)


def _half_rms(x, gain, lo):
    r = lax.rsqrt(_half_sums(x * x, lo) * (1.0 / HEAD_DIM) + EPS)
    xhat = x * r
    return xhat * gain, xhat, r


def _half_rms_bwd(dy, xhat, r, gain, lo):
    gdy = dy * gain
    dx = r * (gdy - xhat * (_half_sums(gdy * xhat, lo) * (1.0 / HEAD_DIM)))
    return dx, jnp.sum(dy * xhat, axis=0, keepdims=True)


def _dup_half(pair, e, lo):
    sw = pltpu.roll(pair, HEAD_DIM, 1)
    return jnp.where(lo, pair, sw) if e == 0 else jnp.where(lo, sw, pair)


def _band_valid(n):
    qi = lax.broadcasted_iota(jnp.int32, (KV_GROUP * CHUNK, 2 * CHUNK), 0) & (CHUNK - 1)
    ki = lax.broadcasted_iota(jnp.int32, (KV_GROUP * CHUNK, 2 * CHUNK), 1)
    dist = qi + CHUNK - ki
    return (dist >= 0) & (dist < CHUNK) & ((n > 0) | (ki >= CHUNK))


def _stack_heads(a, b, lo):
    return jnp.concatenate([jnp.where(lo, a, 0.0), jnp.where(lo, 0.0, a), jnp.where(lo, b, 0.0), jnp.where(lo, 0.0, b)],
                           axis=0)


def _unstack_heads(x4, lo):
    return (jnp.where(lo, x4[0:CHUNK], x4[CHUNK:2 * CHUNK]),
            jnp.where(lo, x4[2 * CHUNK:3 * CHUNK], x4[3 * CHUNK:]))


def _sink_col(sink_ref, hk):
    row = lax.broadcasted_iota(jnp.int32, (KV_GROUP * CHUNK, 1), 0)
    col = jnp.full((KV_GROUP * CHUNK, 1), sink_ref[KV_GROUP * hk + KV_GROUP - 1], F32)
    for r in range(KV_GROUP - 2, -1, -1):
        col = jnp.where(row < (r + 1) * CHUNK, sink_ref[KV_GROUP * hk + r], col)
    return col


def _softmax_sink(s, valid, sink):
    s = jnp.where(valid, s, -jnp.inf)
    m = jnp.maximum(jnp.max(s, axis=-1, keepdims=True), sink)
    p = jnp.exp(s - m)
    esink = jnp.exp(sink - m)
    inv = 1.0 / (jnp.sum(p, axis=-1, keepdims=True) + esink)
    return p * inv, esink * inv


QW = N_HEADS * HEAD_DIM
KVW = N_KV_HEADS * HEAD_DIM


def _attn_fwd(qkv, qg2, kg2, sinks, bias, comm=None):
    S = qkv.shape[0]
    nb = S // CHUNK

    def body(cur_ref, prev_ref, qg_ref, kg_ref, sink_ref, bias_ref, o_ref):
        n = pl.program_id(0)
        lo = _lo_mask(CHUNK)
        lo2 = _lo_mask(2 * CHUNK)
        valid = _band_valid(n)
        for j in range(N_KV_HEADS // 2):
            kc = slice(QW + j * LANES, QW + (j + 1) * LANES)
            vc = slice(QW + KVW + j * LANES, QW + KVW + (j + 1) * LANES)
            kpair = jnp.concatenate([prev_ref[:, j * LANES:(j + 1) * LANES], cur_ref[:, kc]], axis=0).astype(F32)
            vpair = jnp.concatenate([prev_ref[:, KVW + j * LANES:KVW + (j + 1) * LANES], cur_ref[:, vc]], axis=0).astype(F32)
            knpair, _, _ = _half_rms(kpair, kg_ref[...], lo2)
            for e in range(2):
                hk = 2 * j + e
                kdup = _dup_half(knpair, e, lo2).astype(BF16)
                vdup = _dup_half(vpair, e, lo2).astype(BF16)
                ca = slice(2 * hk * LANES, (2 * hk + 1) * LANES)
                cb = slice((2 * hk + 1) * LANES, (2 * hk + 2) * LANES)
                qna, _, _ = _half_rms(cur_ref[:, ca].astype(F32), qg_ref[...], lo)
                qnb, _, _ = _half_rms(cur_ref[:, cb].astype(F32), qg_ref[...], lo)
                qm4 = _stack_heads(qna, qnb, lo).astype(BF16)
                s = _dot_nt(qm4, kdup) * (HEAD_DIM ** -0.5) + bias_ref[hk]
                p, _ = _softmax_sink(s, valid, _sink_col(sink_ref, hk))
                oa, ob = _unstack_heads(_dot(p.astype(BF16), vdup), lo)
                o_ref[:, ca] = oa.astype(BF16)
                o_ref[:, cb] = ob.astype(BF16)

    return _call(
        body, name="attn_fwd", grid=(nb,),
        in_specs=[pl.BlockSpec((CHUNK, QW + 2 * KVW), lambda n: (n, 0)),
                  pl.BlockSpec((CHUNK, 2 * KVW), lambda n: (jnp.maximum(n - 1, 0), QW // (2 * KVW))),
                  _resident((1, LANES)), _resident((1, LANES)),
                  pl.BlockSpec(memory_space=pltpu.SMEM),
                  _resident(bias.shape)],
        out_specs=pl.BlockSpec((CHUNK, QW), lambda n: (n, 0)),
        out_shape=jax.ShapeDtypeStruct((S, QW), BF16),
        sem=("parallel",), args=(qkv, qkv, qg2, kg2, sinks, bias), comm=comm)


def _attn_bwd(qkv, do, qg2, kg2, sinks, bias, comm=None):
    S = qkv.shape[0]
    nb = S // CHUNK

    def body(cur_ref, prev_ref, do_ref, qg_ref, kg_ref, sink_ref, bias_ref,
             dqkv_ref, dbias_ref, dqg_ref, dkg_ref, dsink_ref, carry, band, dsacc, gacc):
        i = pl.program_id(0)
        n = nb - 1 - i
        lo = _lo_mask(CHUNK)
        lo2 = _lo_mask(2 * CHUNK)
        lane = lax.broadcasted_iota(jnp.int32, (KV_GROUP * CHUNK, LANES), 1)
        valid = _band_valid(n)

        @pl.when(i == 0)
        def _():
            dbias_ref[...] = jnp.zeros_like(dbias_ref)
            carry[...] = jnp.zeros_like(carry)
            dsacc[...] = jnp.zeros_like(dsacc)
            gacc[...] = jnp.zeros_like(gacc)

        qgain = qg_ref[...]
        kgain = kg_ref[...]
        for j in range(N_KV_HEADS // 2):
            kc = slice(QW + j * LANES, QW + (j + 1) * LANES)
            vc = slice(QW + KVW + j * LANES, QW + KVW + (j + 1) * LANES)
            kpair = jnp.concatenate([prev_ref[:, j * LANES:(j + 1) * LANES], cur_ref[:, kc]], axis=0).astype(F32)
            vpair = jnp.concatenate([prev_ref[:, KVW + j * LANES:KVW + (j + 1) * LANES], cur_ref[:, vc]], axis=0).astype(F32)
            knpair, khat, kr = _half_rms(kpair, kgain, lo2)
            dk_folds = []
            dv_folds = []
            for e in range(2):
                hk = 2 * j + e
                kdup = _dup_half(knpair, e, lo2).astype(BF16)
                vdup = _dup_half(vpair, e, lo2).astype(BF16)
                ca = slice(2 * hk * LANES, (2 * hk + 1) * LANES)
                cb = slice((2 * hk + 1) * LANES, (2 * hk + 2) * LANES)
                qna, qhata, qra = _half_rms(cur_ref[:, ca].astype(F32), qgain, lo)
                qnb, qhatb, qrb = _half_rms(cur_ref[:, cb].astype(F32), qgain, lo)
                qm4 = _stack_heads(qna, qnb, lo).astype(BF16)
                dom4 = _stack_heads(do_ref[:, ca].astype(F32), do_ref[:, cb].astype(F32), lo).astype(BF16)
                s = _dot_nt(qm4, kdup) * (HEAD_DIM ** -0.5) + bias_ref[hk]
                p, psink = _softmax_sink(s, valid, _sink_col(sink_ref, hk))
                dp = _dot_nt(dom4, vdup)
                delta = jnp.sum(p * dp, axis=-1, keepdims=True)
                ds = p * (dp - delta)
                dbias_ref[hk] += ds
                dsacc[...] += jnp.where(lane == hk, -(psink * delta), 0.0)
                dsr = (ds * (HEAD_DIM ** -0.5)).astype(BF16)
                dqna, dqnb = _unstack_heads(_dot(dsr, kdup), lo)
                dkd = _dot_tn(dsr, qm4)
                dvd = _dot_tn(p.astype(BF16), dom4)
                dqa, dqga = _half_rms_bwd(dqna, qhata, qra, qgain, lo)
                dqb, dqgb = _half_rms_bwd(dqnb, qhatb, qrb, qgain, lo)
                gacc[0:1, :] += dqga + dqgb
                dqkv_ref[:, ca] = dqa.astype(BF16)
                dqkv_ref[:, cb] = dqb.astype(BF16)
                dk_folds.append(dkd + pltpu.roll(dkd, HEAD_DIM, 1))
                dv_folds.append(dvd + pltpu.roll(dvd, HEAD_DIM, 1))
            dkn = jnp.where(lo2, dk_folds[0], dk_folds[1])
            dk, dkg = _half_rms_bwd(dkn, khat, kr, kgain, lo2)
            gacc[1:2, :] += dkg
            band[:, j * LANES:(j + 1) * LANES] = dk
            band[:, KVW + j * LANES:KVW + (j + 1) * LANES] = jnp.where(lo2, dv_folds[0], dv_folds[1])
        dqkv_ref[:, QW:] = (band[CHUNK:, :] + carry[...]).astype(BF16)
        carry[...] = band[0:CHUNK, :]

        @pl.when(i == nb - 1)
        def _():
            g = gacc[...]
            g = g + pltpu.roll(g, HEAD_DIM, 1)
            dqg_ref[...] = g[0:1, :]
            dkg_ref[...] = g[1:2, :]
            for r in range(KV_GROUP):
                dsink_ref[r:r + 1, :] = jnp.sum(dsacc[r * CHUNK:(r + 1) * CHUNK, :], axis=0, keepdims=True)

    vec = pl.BlockSpec((1, LANES), lambda i: (0, 0))
    return _call(
        body, name="attn_bwd", grid=(nb,),
        in_specs=[pl.BlockSpec((CHUNK, QW + 2 * KVW), lambda i: (nb - 1 - i, 0)),
                  pl.BlockSpec((CHUNK, 2 * KVW), lambda i: (jnp.maximum(nb - 2 - i, 0), QW // (2 * KVW))),
                  pl.BlockSpec((CHUNK, QW), lambda i: (nb - 1 - i, 0)),
                  _resident((1, LANES)), _resident((1, LANES)),
                  pl.BlockSpec(memory_space=pltpu.SMEM),
                  _resident(bias.shape)],
        out_specs=[pl.BlockSpec((CHUNK, QW + 2 * KVW), lambda i: (nb - 1 - i, 0)),
                   pl.BlockSpec(bias.shape, lambda i: (0, 0, 0)), vec, vec,
                   pl.BlockSpec((KV_GROUP, LANES), lambda i: (0, 0))],
        out_shape=[jax.ShapeDtypeStruct((S, QW + 2 * KVW), BF16),
                   jax.ShapeDtypeStruct(bias.shape, F32),
                   jax.ShapeDtypeStruct((1, LANES), F32), jax.ShapeDtypeStruct((1, LANES), F32),
                   jax.ShapeDtypeStruct((KV_GROUP, LANES), F32)],
        scratch_shapes=[pltpu.VMEM((CHUNK, 2 * KVW), F32), pltpu.VMEM((2 * CHUNK, 2 * KVW), F32),
                        pltpu.VMEM((KV_GROUP * CHUNK, LANES), F32), pltpu.VMEM((8, LANES), F32)],
        sem=("arbitrary",), args=(qkv, qkv, do, qg2, kg2, sinks, bias), comm=comm)


def _loss_head(y, target, tm=512):
    S, D = y.shape
    tm = min(tm, S)

    def body(y_ref, t_ref, dy_ref, l_ref):
        @pl.when(pl.program_id(0) == 0)
        def _():
            l_ref[...] = jnp.zeros_like(l_ref)

        e = y_ref[...] - t_ref[...]
        dy_ref[...] = e * (1.0 / D)
        rows = jnp.sum(e * e, axis=-1, keepdims=True) * (1.0 / D)
        l_ref[...] += 0.5 * jnp.sum(rows, axis=0, keepdims=True)

    dy, l = pl.pallas_call(
        body, name="loss_head", grid=(S // tm,),
        in_specs=[pl.BlockSpec((tm, D), lambda i: (i, 0)), pl.BlockSpec((tm, D), lambda i: (i, 0))],
        out_specs=[pl.BlockSpec((tm, D), lambda i: (i, 0)), pl.BlockSpec((1, 1), lambda i: (0, 0))],
        out_shape=[jax.ShapeDtypeStruct((S, D), F32), jax.ShapeDtypeStruct((1, 1), F32)],
        compiler_params=_cp(("arbitrary",)),
    )(y, target)
    return dy, l[0, 0]


def _row_tile(rows, cols, n_arrays):
    budget = VMEM_LIMIT_V7X // 4 // (2 * n_arrays * 4 * cols)
    best = 8
    for n in range(1, rows // 8 + 1):
        if rows % n == 0 and (rows // n) % 8 == 0 and rows // n <= budget:
            best = rows // n
            break
    return best


def _adamw(g, w, m, v, name, comm=None):
    R, C = g.shape
    tr = _row_tile(R, C, 7)

    def body(g_ref, w_ref, m_ref, v_ref, d_ref, mo_ref, vo_ref):
        gg = g_ref[...]
        mn = ADAM_B1 * m_ref[...] + (1.0 - ADAM_B1) * gg
        vn = ADAM_B2 * v_ref[...] + (1.0 - ADAM_B2) * jnp.square(gg)
        m_hat = mn / (1.0 - ADAM_B1 ** ADAM_STEP)
        v_hat = vn / (1.0 - ADAM_B2 ** ADAM_STEP)
        d_ref[...] = -ADAM_LR * (m_hat / (jnp.sqrt(v_hat) + ADAM_EPS) + ADAM_WD * w_ref[...])
        mo_ref[...] = mn
        vo_ref[...] = vn

    spec = pl.BlockSpec((tr, C), lambda i: (i, 0))
    return _call(
        body, name=name, grid=(R // tr,), in_specs=[spec] * 4, out_specs=[spec] * 3,
        out_shape=[jax.ShapeDtypeStruct((R, C), F32)] * 3, sem=("parallel",), args=(g, w, m, v), comm=comm)


def _place_shard(shard, where, dtype, name):
    R, C = shard.shape
    tr = _row_tile(R, C, 2) if R % 8 == 0 else R

    def body(s_ref, x_ref, o_ref):
        o_ref[0] = x_ref[...].astype(dtype)

    return pl.pallas_call(
        body, name=name,
        grid_spec=pltpu.PrefetchScalarGridSpec(
            num_scalar_prefetch=1, grid=(R // tr,),
            in_specs=[pl.BlockSpec((tr, C), lambda i, s_ref: (i, 0))],
            out_specs=pl.BlockSpec((1, tr, C), lambda i, s_ref: (s_ref[1], i, 0))),
        out_shape=jax.ShapeDtypeStruct((4, R, C), dtype),
        compiler_params=_cp(("parallel",)),
    )(where, shard)


def _pair_add(g4, rsib, where, name):
    J, R, C = g4.shape
    Rh = R // 2
    tr = _row_tile(Rh, C, 4)
    g5 = g4.reshape(J, 2, Rh, C)

    def body(s_ref, g_ref, r_ref, p_ref, q_ref):
        val = (g_ref[...].astype(F32)[0] + r_ref[...].astype(F32)).astype(BF16)
        p_ref[...] = val

        @pl.when(pl.program_id(1) == s_ref[1])
        def _():
            q_ref[...] = val

    return pl.pallas_call(
        body, name=name,
        grid_spec=pltpu.PrefetchScalarGridSpec(
            num_scalar_prefetch=1, grid=(Rh // tr, J),
            in_specs=[pl.BlockSpec((1, 1, tr, C), lambda i, j, s_ref: (j, s_ref[0], i, 0)),
                      pl.BlockSpec((1, tr, C), lambda i, j, s_ref: (j, i, 0))],
            out_specs=[pl.BlockSpec((1, tr, C), lambda i, j, s_ref: (j, i, 0)),
                       pl.BlockSpec((1, tr, C), lambda i, j, s_ref: (s_ref[1], i, 0))]),
        out_shape=[jax.ShapeDtypeStruct((J, Rh, C), BF16)] * 2,
        compiler_params=_cp(("parallel", "arbitrary")),
    )(where, g5, rsib)


def _sum_chips(q, where, dest, layer, out_shape, name):
    J, Rh, C = q.shape
    tr = _row_tile(Rh, C, 3)
    nb = Rh // tr

    def body(s_ref, q_ref, *rest):
        qq = q_ref[...].astype(F32)
        rest[-1][0] = ((qq[0] + qq[1]) + qq[2]) + qq[3]

    have = dest is not None
    return pl.pallas_call(
        body, name=name,
        grid_spec=pltpu.PrefetchScalarGridSpec(
            num_scalar_prefetch=1, grid=(nb,),
            in_specs=[pl.BlockSpec((J, tr, C), lambda i, s_ref: (0, i, 0))] + ([ANY] if have else []),
            out_specs=pl.BlockSpec((1, tr, C), lambda i, s_ref: (layer, s_ref[0] * nb + i, 0))),
        out_shape=jax.ShapeDtypeStruct(out_shape, F32),
        input_output_aliases={2: 0} if have else {},
        compiler_params=_cp(("parallel",)),
    )(*((where, q, dest) if have else (where, q)))


MESH = pl.DeviceIdType.MESH
ANY = pl.BlockSpec(memory_space=pl.ANY)


def _place():
    x, y, c = lax.axis_index("x"), lax.axis_index("y"), lax.axis_index("c")
    others = [(1 - x, y), (x, 1 - y), (1 - x, 1 - y)]
    return x, y, c, 2 * x + y, others, [2 * ox + oy for ox, oy in others]


def _gather_comm(placed, split):
    n = len(placed)

    def rows(t, ref, half):
        if not split[t]:
            return ref
        rh = placed[t].shape[1] // 2
        return ref.at[pl.ds(half * rh, rh), :]

    def sends(outs, sems):
        send, recv = sems[0], sems[1]
        x, y, c, me, others, okey = _place()
        cps = []
        for t in range(n):
            mine = rows(t, outs[t].at[me], c)
            for j, (ox, oy) in enumerate(others):
                cps.append(pltpu.make_async_remote_copy(
                    src_ref=mine, dst_ref=mine,
                    send_sem=send.at[t, j], recv_sem=recv.at[t, j], device_id=(ox, oy, c), device_id_type=MESH))
        return cps

    def start(srcs, outs, news, sems):
        for cp in sends(outs, sems):
            cp.start()

    def finish(srcs, outs, news, sems):
        send, recv, fsend, frecv = sems
        x, y, c, me, others, okey = _place()
        sib = (x, y, 1 - c)
        first = sends(outs, sems)
        passed = []
        for t in range(n):
            for j in range(3):
                landed = rows(t, outs[t].at[okey[j]], c)
                pltpu.make_async_remote_copy(
                    src_ref=landed, dst_ref=landed, send_sem=send.at[t, j], recv_sem=recv.at[t, j],
                    device_id=sib, device_id_type=MESH).wait_recv()
                if split[t]:
                    cp = pltpu.make_async_remote_copy(
                        src_ref=landed, dst_ref=landed, send_sem=fsend.at[t, j], recv_sem=frecv.at[t, j],
                        device_id=sib, device_id_type=MESH)
                    cp.start()
                    passed.append(cp)
        for t in range(n):
            if split[t]:
                for j in range(3):
                    theirs = rows(t, outs[t].at[okey[j]], 1 - c)
                    pltpu.make_async_remote_copy(
                        src_ref=theirs, dst_ref=theirs, send_sem=fsend.at[t, j], recv_sem=frecv.at[t, j],
                        device_id=sib, device_id_type=MESH).wait_recv()
        for cp in first + passed:
            cp.wait_send()

    return _Comm([], placed, [], [pltpu.SemaphoreType.DMA((n, 3))] * 4, start, finish)


def _pair_exchange_comm(gs):
    n = len(gs)

    def copies(ins, outs, sems):
        send, recv = sems
        x, y, c, _, _, _ = _place()
        cps = []
        for t in range(n):
            rh = gs[t].shape[1] // 2
            cps.append(pltpu.make_async_remote_copy(
                src_ref=ins[t].at[:, pl.ds((1 - c) * rh, rh), :], dst_ref=outs[t],
                send_sem=send.at[t], recv_sem=recv.at[t], device_id=(x, y, 1 - c), device_id_type=MESH))
        return cps

    def start(ins, zones, outs, sems):
        for cp in copies(ins, outs, sems):
            cp.start()

    def finish(ins, zones, outs, sems):
        for cp in copies(ins, outs, sems):
            cp.wait()

    news = [jax.ShapeDtypeStruct((4, g.shape[1] // 2, g.shape[2]), g.dtype) for g in gs]
    return _Comm(gs, [], news, [pltpu.SemaphoreType.DMA((n,))] * 2, start, finish)


def _chip_scatter_comm(ps, qs):
    n = len(ps)

    def sends(ins, outs, sems):
        send, recv = sems
        x, y, c, me, others, okey = _place()
        return [pltpu.make_async_remote_copy(
            src_ref=ins[t].at[okey[j]], dst_ref=outs[t].at[me],
            send_sem=send.at[t, j], recv_sem=recv.at[t, j], device_id=(ox, oy, c), device_id_type=MESH)
            for t in range(n) for j, (ox, oy) in enumerate(others)]

    def start(ins, outs, news, sems):
        for cp in sends(ins, outs, sems):
            cp.start()

    def finish(ins, outs, news, sems):
        send, recv = sems
        x, y, c, me, others, okey = _place()
        for t in range(n):
            for j in range(3):
                slot = outs[t].at[okey[j]]
                pltpu.make_async_remote_copy(
                    src_ref=slot, dst_ref=slot, send_sem=send.at[t, j], recv_sem=recv.at[t, j],
                    device_id=(x, y, c), device_id_type=MESH).wait_recv()
        for cp in sends(ins, outs, sems):
            cp.wait_send()

    return _Comm(ps, qs, [], [pltpu.SemaphoreType.DMA((n, 3))] * 2, start, finish)


def _half_exchange_comm(arrs, layers=None):
    n = len(arrs)
    items = [(t, layer) for t in range(n) for layer in (range(arrs[t].shape[0]) if layers is None else layers[t])]

    def sends(outs, sems):
        send, recv = sems
        x, y, c, _, _, _ = _place()
        cps = []
        for k, (t, layer) in enumerate(items):
            rh = arrs[t].shape[1] // 2
            mine = outs[t].at[layer, pl.ds(c * rh, rh), :]
            cps.append(pltpu.make_async_remote_copy(
                src_ref=mine, dst_ref=mine, send_sem=send.at[k], recv_sem=recv.at[k],
                device_id=(x, y, 1 - c), device_id_type=MESH))
        return cps

    def start(srcs, outs, news, sems):
        for cp in sends(outs, sems):
            cp.start()

    def finish(srcs, outs, news, sems):
        send, recv = sems
        x, y, c, _, _, _ = _place()
        for k, (t, layer) in enumerate(items):
            rh = arrs[t].shape[1] // 2
            theirs = outs[t].at[layer, pl.ds((1 - c) * rh, rh), :]
            pltpu.make_async_remote_copy(
                src_ref=theirs, dst_ref=theirs, send_sem=send.at[k], recv_sem=recv.at[k],
                device_id=(x, y, 1 - c), device_id_type=MESH).wait_recv()
        for cp in sends(outs, sems):
            cp.wait_send()

    return _Comm([], arrs, [], [pltpu.SemaphoreType.DMA((len(items),))] * 2, start, finish)


SMALL_COLS = 1024
SMALL_PIECE_ROWS = 40


def _allreduce_small(buf, comm=None):
    pr = SMALL_PIECE_ROWS
    flips = [(d >> 2 & 1, d >> 1 & 1, d & 1) for d in range(1, 8)]

    def body(x_ref, o_ref, rbuf, send1, recv1, send2, recv2):
        x, y, c = lax.axis_index("x"), lax.axis_index("y"), lax.axis_index("c")
        me = 4 * x + 2 * y + c
        peers = [(x ^ fx, y ^ fy, c ^ fc) for fx, fy, fc in flips]
        pid = [4 * px + 2 * py + pc for px, py, pc in peers]

        def piece(ref, p):
            return ref.at[pl.ds(pl.multiple_of(p * pr, 8), pr), :]

        cps = []
        for d in range(7):
            cp = pltpu.make_async_remote_copy(
                src_ref=piece(x_ref, pid[d]), dst_ref=rbuf.at[d + 1],
                send_sem=send1.at[d], recv_sem=recv1.at[d], device_id=peers[d], device_id_type=MESH)
            cp.start()
            cps.append(cp)
        acc = piece(x_ref, me)[...]
        for d in range(7):
            cps[d].wait_recv()
            acc = acc + rbuf[d + 1]
        piece(o_ref, me)[...] = acc
        out = []
        for d in range(7):
            cp = pltpu.make_async_remote_copy(
                src_ref=piece(o_ref, me), dst_ref=piece(o_ref, me),
                send_sem=send2.at[d], recv_sem=recv2.at[d], device_id=peers[d], device_id_type=MESH)
            cp.start()
            out.append(cp)
        for d in range(7):
            pltpu.make_async_remote_copy(
                src_ref=piece(o_ref, pid[d]), dst_ref=piece(o_ref, pid[d]),
                send_sem=send2.at[d], recv_sem=recv2.at[d], device_id=peers[d], device_id_type=MESH).wait_recv()
        for cp in cps + out:
            cp.wait_send()

    vm = pl.BlockSpec(memory_space=pltpu.VMEM)
    return _call(
        body, name="small_allreduce", grid=(), in_specs=[vm], out_specs=vm,
        out_shape=jax.ShapeDtypeStruct(buf.shape, F32),
        scratch_shapes=[pltpu.VMEM((8, pr, SMALL_COLS), F32)] + [pltpu.SemaphoreType.DMA((7,))] * 4,
        args=(buf,), comm=comm)


def _pack(arrays, rows):
    flat = jnp.concatenate([a.reshape(-1) for a in arrays])
    return jnp.pad(flat, (0, rows * SMALL_COLS - flat.shape[0])).reshape(rows, SMALL_COLS)


def _unpack(buf, shapes):
    flat = buf.reshape(-1)
    out, off = [], 0
    for s in shapes:
        k = math.prod(s)
        out.append(flat[off:off + k].reshape(s))
        off += k
    return out


BIG = ["sgu_w_in", "sgu_w_out", "attn_w_qkv", "attn_w_o", "ffn_w_up", "ffn_w_down"]
SMALL = ["mix_norm", "ffn_norm", "sgu_v_gain", "sgu_w_s", "sgu_b_s", "attn_q_gain", "attn_k_gain", "attn_sinks",
         "rel_bias", "ffn_conv_b"]
ORDER = ["mix_norm", "ffn_norm", "sgu_w_in", "sgu_v_gain", "sgu_w_s", "sgu_b_s", "sgu_w_out", "attn_w_qkv",
         "attn_q_gain", "attn_k_gain", "attn_sinks", "attn_w_o", "rel_bias", "ffn_w_up", "ffn_conv_w", "ffn_conv_b",
         "ffn_w_down"]


def kernel(x, mix_norm, ffn_norm, sgu_w_in, sgu_v_gain, sgu_w_s, sgu_b_s, sgu_w_out, attn_w_qkv, attn_q_gain, attn_k_gain, attn_sinks, attn_w_o, rel_bias, ffn_w_up, ffn_conv_w, ffn_conv_b, ffn_w_down, loss_target, m_mix_norm, m_ffn_norm, m_sgu_w_in, m_sgu_v_gain, m_sgu_w_s, m_sgu_b_s, m_sgu_w_out, m_attn_w_qkv, m_attn_q_gain, m_attn_k_gain, m_attn_sinks, m_attn_w_o, m_rel_bias, m_ffn_w_up, m_ffn_conv_w, m_ffn_conv_b, m_ffn_w_down, v_mix_norm, v_ffn_norm, v_sgu_w_in, v_sgu_v_gain, v_sgu_w_s, v_sgu_b_s, v_sgu_w_out, v_attn_w_qkv, v_attn_q_gain, v_attn_k_gain, v_attn_sinks, v_attn_w_o, v_rel_bias, v_ffn_w_up, v_ffn_conv_w, v_ffn_conv_b, v_ffn_w_down):
    w = dict(mix_norm=mix_norm, ffn_norm=ffn_norm, sgu_w_in=sgu_w_in, sgu_v_gain=sgu_v_gain, sgu_w_s=sgu_w_s,
             sgu_b_s=sgu_b_s, sgu_w_out=sgu_w_out, attn_w_qkv=attn_w_qkv, attn_q_gain=attn_q_gain,
             attn_k_gain=attn_k_gain, attn_sinks=attn_sinks, attn_w_o=attn_w_o, rel_bias=rel_bias, ffn_w_up=ffn_w_up,
             ffn_conv_w=ffn_conv_w, ffn_conv_b=ffn_conv_b, ffn_w_down=ffn_w_down)
    mom = dict(mix_norm=m_mix_norm, ffn_norm=m_ffn_norm, sgu_w_in=m_sgu_w_in, sgu_v_gain=m_sgu_v_gain,
               sgu_w_s=m_sgu_w_s, sgu_b_s=m_sgu_b_s, sgu_w_out=m_sgu_w_out, attn_w_qkv=m_attn_w_qkv,
               attn_q_gain=m_attn_q_gain, attn_k_gain=m_attn_k_gain, attn_sinks=m_attn_sinks, attn_w_o=m_attn_w_o,
               rel_bias=m_rel_bias, ffn_w_up=m_ffn_w_up, ffn_conv_w=m_ffn_conv_w, ffn_conv_b=m_ffn_conv_b,
               ffn_w_down=m_ffn_w_down)
    var = dict(mix_norm=v_mix_norm, ffn_norm=v_ffn_norm, sgu_w_in=v_sgu_w_in, sgu_v_gain=v_sgu_v_gain,
               sgu_w_s=v_sgu_w_s, sgu_b_s=v_sgu_b_s, sgu_w_out=v_sgu_w_out, attn_w_qkv=v_attn_w_qkv,
               attn_q_gain=v_attn_q_gain, attn_k_gain=v_attn_k_gain, attn_sinks=v_attn_sinks, attn_w_o=v_attn_w_o,
               rel_bias=v_rel_bias, ffn_w_up=v_ffn_w_up, ffn_conv_w=v_ffn_conv_w, ffn_conv_b=v_ffn_conv_b,
               ffn_w_down=v_ffn_w_down)
    chip = 2 * lax.axis_index("x") + lax.axis_index("y")
    core = lax.axis_index("c")

    where = jnp.stack([core, chip]).astype(jnp.int32)
    names = ["sgu_w_in", "sgu_w_out", "attn_w_qkv", "attn_w_o", "ffn_w_up0", "ffn_w_up1", "ffn_w_down0", "ffn_w_down1"]
    shards = [sgu_w_in[0], sgu_w_out[0], attn_w_qkv[0], attn_w_o[0], ffn_w_up[0], ffn_w_up[1],
              ffn_w_down[0], ffn_w_down[1]]
    T = {nm: _place_shard(s, where, BF16, "place_" + nm) for s, nm in zip(shards, names)}
    for l in range(2):
        T["conv_w%d" % l] = _place_shard(ffn_conv_w[l], where, F32, "place_conv_w%d" % l)

    def gather(keys):
        return _gather_comm([T[k] for k in keys], [not k.startswith("conv") for k in keys])

    def gathered(keys, res):
        for k, a in zip(keys, res[0]):
            T[k] = a

    D = x.shape[2]
    first = ["sgu_w_in", "conv_w0", "conv_w1"]
    gathered(first, _run_comm(gather(first), "gather_first"))
    unshard_cols = lambda a: jnp.transpose(a, (1, 0, 2)).reshape(a.shape[1], -1)
    cw = [unshard_cols(T["conv_w0"]), unshard_cols(T["conv_w1"])]
    cb = ffn_conv_b
    flat = lambda k: T[k].reshape(-1, D)
    x2, target = x[0], loss_target[0]
    bucket = jnp.asarray(_rel_buckets_flat())
    wtril, wtrilT = _sgu_prep(sgu_w_s[0])
    bT = sgu_b_s[0].T
    bias = _rel_bias_expand(rel_bias, bucket).reshape(N_KV_HEADS, KV_GROUP * CHUNK, 2 * CHUNK)
    qg2 = jnp.tile(attn_q_gain, (1, 2))
    kg2 = jnp.tile(attn_k_gain, (1, 2))
    sinks = attn_sinks.reshape(N_HEADS)
    mix0, mix1 = mix_norm[0:1], mix_norm[1:2]
    fn0, fn1 = ffn_norm[0:1], ffn_norm[1:2]

    ks = ["sgu_w_out", "attn_w_qkv"]
    (hn0, z), r = _norm_matmul(x2, mix0, T["sgu_w_in"], "sgu_in", comm=gather(ks))
    gathered(ks, r)
    ks = ["ffn_w_up0"]
    (yp, h1), r = _sgu_fwd(z, x2, sgu_v_gain, wtril, bT, flat("sgu_w_out"), comm=gather(ks))
    gathered(ks, r)
    ks = ["ffn_w_down0", "attn_w_o"]
    (hf0, a0), r = _norm_matmul(h1, fn0, T["ffn_w_up0"], "ffn0_up", comm=gather(ks))
    gathered(ks, r)
    ks = ["ffn_w_up1"]
    (f0, h2), r = _ffn_fwd(a0, h1, cw[0], cb[0:1], flat("ffn_w_down0"), "ffn0_fwd", comm=gather(ks))
    gathered(ks, r)
    (hn1, qkv), _ = _norm_matmul(h2, mix1, T["attn_w_qkv"], "attn_qkv")
    ks = ["ffn_w_down1"]
    o, r = _attn_fwd(qkv, qg2, kg2, sinks, bias, comm=gather(ks))
    gathered(ks, r)
    h3, _ = _matmul_res(o, flat("attn_w_o"), h2, "attn_out")
    (hf1, a1), _ = _norm_matmul(h3, fn1, T["ffn_w_up1"], "ffn1_up")
    (f1, h4), _ = _ffn_fwd(a1, h3, cw[1], cb[1:2], flat("ffn_w_down1"), "ffn1_fwd")
    dh4, loss_local = _loss_head(h4, target)
    loss = lax.psum(loss_local, ("x", "y", "c"))

    G, RS, PQ, QD, halves, grads = {}, {}, {}, {}, {}, {}
    dest_of = {"sgu_w_in": ("sgu_w_in", 0), "sgu_w_out": ("sgu_w_out", 0), "attn_w_qkv": ("attn_w_qkv", 0),
               "attn_w_o": ("attn_w_o", 0), "ffn_w_up0": ("ffn_w_up", 0), "ffn_w_up1": ("ffn_w_up", 1),
               "ffn_w_down0": ("ffn_w_down", 0), "ffn_w_down1": ("ffn_w_down", 1)}

    def px(keys):
        return _pair_exchange_comm([G[k] for k in keys])

    def px_done(keys, res):
        for k, a in zip(keys, res[1]):
            RS[k] = a
            PQ[k] = _pair_add(G[k], a, where, "pair_add_" + k)

    def sc(keys):
        return _chip_scatter_comm([PQ[k][0] for k in keys], [PQ[k][1] for k in keys])

    def sc_done(keys, res):
        for k, a in zip(keys, res[0]):
            wk, layer = dest_of[k]
            halves[wk] = _sum_chips(a, where, halves.get(wk), layer, w[wk].shape, "sum_chips_" + k)

    nup = ffn_w_up.shape[2]
    ndown = ffn_w_down.shape[1]
    rows4 = lambda a: a.reshape(4, ndown, D)
    (dc1, dcw1, dcb1), _ = _ffn_bwd_dc(dh4, a1, cw[1], cb[1:2], flat("ffn_w_down1"), "ffn1_bwd_dc")
    gw, _ = _matmul_tn(f1, dh4, "ffn1_dw_down", ka=2 * ndown, nb=D, out_dtype=BF16)
    G["ffn_w_down1"] = rows4(gw)
    (dh3, dfn1, da1), r = _bwd_norm(dc1, T["ffn_w_up1"], h3, fn1, dh4, "ffn1_bwd_in", conv_w=cw[1],
                                    comm=px(["ffn_w_down1"]))
    px_done(["ffn_w_down1"], r)
    G["ffn_w_up1"], r = _matmul_tn(hf1, da1, "ffn1_dw_up", ka=D, nb=nup, out_dtype=BF16, comm=sc(["ffn_w_down1"]))
    sc_done(["ffn_w_down1"], r)
    G["attn_w_o"], r = _matmul_tn(o, dh3, "attn_dw_o", ka=QW // 4, nb=D, out_dtype=BF16, comm=px(["ffn_w_up1"]))
    px_done(["ffn_w_up1"], r)
    do, r = _matmul_nt(dh3, flat("attn_w_o"), "attn_bwd_out", comm=px(["attn_w_o"]))
    px_done(["attn_w_o"], r)
    ks = ["ffn_w_up1", "attn_w_o"]
    (dqkv, dbias, dqg, dkg, dsink), r = _attn_bwd(qkv, do, qg2, kg2, sinks, bias, comm=sc(ks))
    sc_done(ks, r)
    G["attn_w_qkv"], _ = _matmul_tn(hn1, dqkv, "attn_dw_qkv", ka=D, nb=dqkv.shape[1] // 4, out_dtype=BF16)
    (dh2, dmix1), r = _bwd_norm(dqkv, T["attn_w_qkv"], h2, mix1, dh3, "attn_bwd_in", comm=px(["attn_w_qkv"]))
    px_done(["attn_w_qkv"], r)
    (dc0, dcw0, dcb0), r = _ffn_bwd_dc(dh2, a0, cw[0], cb[0:1], flat("ffn_w_down0"), "ffn0_bwd_dc",
                                       comm=sc(["attn_w_qkv"]))
    sc_done(["attn_w_qkv"], r)
    gw, _ = _matmul_tn(f0, dh2, "ffn0_dw_down", ka=2 * ndown, nb=D, out_dtype=BF16)
    G["ffn_w_down0"] = rows4(gw)
    (dh1, dfn0, da0), r = _bwd_norm(dc0, T["ffn_w_up0"], h1, fn0, dh2, "ffn0_bwd_in", conv_w=cw[0],
                                    comm=px(["ffn_w_down0"]))
    px_done(["ffn_w_down0"], r)
    G["ffn_w_up0"], r = _matmul_tn(hf0, da0, "ffn0_dw_up", ka=D, nb=nup, out_dtype=BF16, comm=sc(["ffn_w_down0"]))
    sc_done(["ffn_w_down0"], r)
    G["sgu_w_out"], r = _matmul_tn(yp, dh1, "sgu_dw_out", ka=yp.shape[1] // 4, nb=D, out_dtype=BF16,
                                   comm=px(["ffn_w_up0"]))
    px_done(["ffn_w_up0"], r)
    both = [sc(["ffn_w_up0"]), px(["sgu_w_out"])]
    (dz, dws, dbT, dvg), r = _sgu_bwd(dh1, z, sgu_v_gain, wtril, wtrilT, bT, flat("sgu_w_out"), comm=_join(both))
    r = _split(both, r)
    sc_done(["ffn_w_up0"], r[0])
    px_done(["sgu_w_out"], r[1])
    G["sgu_w_in"], r = _matmul_tn(hn0, dz, "sgu_dw_in", ka=D, nb=dz.shape[1] // 4, out_dtype=BF16,
                                  comm=sc(["sgu_w_out"]))
    sc_done(["sgu_w_out"], r)
    (grad_x, dmix0), r = _bwd_norm(dz, T["sgu_w_in"], x2, mix0, dh1, "sgu_bwd_in", comm=px(["sgu_w_in"]))
    px_done(["sgu_w_in"], r)

    g = dict(mix_norm=jnp.concatenate([dmix0, dmix1], axis=0), ffn_norm=jnp.concatenate([dfn0, dfn1], axis=0),
             sgu_v_gain=dvg, sgu_w_s=dws, sgu_b_s=dbT.T, attn_q_gain=dqg[:, :HEAD_DIM], attn_k_gain=dkg[:, :HEAD_DIM],
             attn_sinks=dsink[:, :N_KV_HEADS].T.reshape(1, N_HEADS),
             rel_bias=_rel_bias_reduce(dbias.reshape(N_HEADS, CHUNK * 2 * CHUNK), bucket),
             ffn_conv_b=jnp.concatenate([dcb0, dcb1], axis=0))
    small_list = [g[k].reshape(w[k].shape) for k in SMALL] + [jnp.stack([dcw0, dcw1])]
    small_shapes = [a.shape for a in small_list]
    done = ["sgu_w_out", "attn_w_qkv", "attn_w_o", "ffn_w_up", "ffn_w_down"]
    both = [sc(["sgu_w_in"]), _half_exchange_comm([halves[k] for k in done])]
    red, r = _allreduce_small(_pack(small_list, 8 * SMALL_PIECE_ROWS), comm=_join(both))
    r = _split(both, r)
    sc_done(["sgu_w_in"], r[0])
    for k, a in zip(done, r[1][0]):
        grads[k] = a
    red = _unpack(red, small_shapes)
    for k, a in zip(SMALL, red[:-1]):
        grads[k] = a
    grads["ffn_conv_w"] = lax.dynamic_slice_in_dim(red[-1], chip * ffn_conv_w.shape[2], ffn_conv_w.shape[2], axis=2)

    delta, new_m, new_v = {}, {}, {}
    two = lambda a: a.reshape(-1, a.shape[-1])
    last = _half_exchange_comm([halves["sgu_w_in"]])
    for k in ["ffn_w_up", "ffn_w_down", "sgu_w_out", "attn_w_qkv", "attn_w_o", "sgu_w_in"]:
        (d2, m2, v2), r = _adamw(two(grads[k]), two(w[k]), two(mom[k]), two(var[k]), "adamw_" + k,
                                 comm=last if k == "ffn_w_up" else None)
        if k == "ffn_w_up":
            grads["sgu_w_in"] = r[0][0]
        delta[k], new_m[k], new_v[k] = (a.reshape(w[k].shape) for a in (d2, m2, v2))
    sm = SMALL + ["ffn_conv_w"]
    sm_shapes = [w[k].shape for k in sm]
    n_el = sum(math.prod(s) for s in sm_shapes)
    rows = -(-n_el // (8 * SMALL_COLS)) * 8
    (d2, m2, v2), _ = _adamw(_pack([grads[k] for k in sm], rows), _pack([w[k] for k in sm], rows),
                             _pack([mom[k] for k in sm], rows), _pack([var[k] for k in sm], rows), "adamw_small")
    for dst, buf in ((delta, d2), (new_m, m2), (new_v, v2)):
        for k, a in zip(sm, _unpack(buf, sm_shapes)):
            dst[k] = a

    return (loss, grad_x[None], *[grads[k] for k in ORDER], *[delta[k] for k in ORDER],
            *[new_m[k] for k in ORDER], *[new_v[k] for k in ORDER])
```

```python
import functools
import math

import numpy as np
import jax
import jax.numpy as jnp
from jax import lax
from jax.experimental import pallas as pl
from jax.experimental.pallas import tpu as pltpu

F32 = jnp.float32
BF16 = jnp.bfloat16

EPS = 1e-6
CHUNK = 128
SGU_GROUPS = 16
HEAD_DIM = 64
N_HEADS = 16
N_KV_HEADS = 4
KV_GROUP = N_HEADS // N_KV_HEADS
REL_BUCKETS = 32
REL_MAX_DIST = 128
LANES = 128
HALO = 16

ADAM_LR = 0.001
ADAM_B1 = 0.9
ADAM_B2 = 0.999
ADAM_EPS = 1e-08
ADAM_WD = 0.01
ADAM_STEP = 10

VMEM_LIMIT_V7X = 56 * 1024 * 1024

_SQRT_HALF = math.sqrt(0.5)
_INV_SQRT_2PI = 1.0 / math.sqrt(2.0 * math.pi)


def _cp(sem):
    return pltpu.CompilerParams(dimension_semantics=sem, vmem_limit_bytes=VMEM_LIMIT_V7X)


def _resident(shape):
    nd = len(shape)
    return pl.BlockSpec(shape, lambda *_: (0,) * nd, pipeline_mode=pl.Buffered(1))


class _Comm:
    def __init__(self, srcs, zones, news, sems, start, finish):
        self.srcs, self.zones, self.news, self.sems = list(srcs), list(zones), list(news), list(sems)
        self.start, self.finish = start, finish


def _join(comms):
    comms = [c for c in comms if c is not None]
    if not comms:
        return None

    def part(seq, attr):
        out, k = [], 0
        for c in comms:
            n = len(getattr(c, attr))
            out.append(seq[k:k + n])
            k += n
        return out

    def run(which):
        def f(srcs, zones, news, sems):
            for c, a, b, d, e in zip(comms, part(srcs, "srcs"), part(zones, "zones"), part(news, "news"), part(sems, "sems")):
                getattr(c, which)(a, b, d, e)
        return f

    cat = lambda attr: [v for c in comms for v in getattr(c, attr)]
    return _Comm(cat("srcs"), cat("zones"), cat("news"), cat("sems"), run("start"), run("finish"))


def _split(comms, res):
    zones, news = res
    out, kz, kn = [], 0, 0
    for c in comms:
        out.append((zones[kz:kz + len(c.zones)], news[kn:kn + len(c.news)]))
        kz += len(c.zones)
        kn += len(c.news)
    return out


def _call(body, *, name, grid, in_specs, out_specs, out_shape, args, scratch_shapes=(), sem=None, comm=None):
    if comm is None:
        res = pl.pallas_call(body, name=name, grid=grid, in_specs=in_specs, out_specs=out_specs, out_shape=out_shape,
                             scratch_shapes=list(scratch_shapes), compiler_params=_cp(sem))(*args)
        return res, None
    single = not isinstance(out_shape, (list, tuple))
    out_specs_l = [out_specs] if single else list(out_specs)
    out_shape_l = [out_shape] if single else list(out_shape)
    n_in, n_out, n_scr = len(in_specs), len(out_shape_l), len(scratch_shapes)
    ns, nz, nn = len(comm.srcs), len(comm.zones), len(comm.news)

    def wrapped(*refs):
        k = n_in
        ins, srcs = refs[:k], refs[k:k + ns]
        k += ns + nz
        outs, zones, news = refs[k:k + n_out], refs[k + n_out:k + n_out + nz], refs[k + n_out + nz:k + n_out + nz + nn]
        k += n_out + nz + nn
        scr, sems = refs[k:k + n_scr], refs[k + n_scr:]
        if not grid:
            comm.start(srcs, zones, news, sems)
            body(*ins, *outs, *scr)
            comm.finish(srcs, zones, news, sems)
            return
        first = functools.reduce(jnp.logical_and, [pl.program_id(a) == 0 for a in range(len(grid))])
        last = functools.reduce(jnp.logical_and, [pl.program_id(a) == grid[a] - 1 for a in range(len(grid))])

        @pl.when(first)
        def _():
            comm.start(srcs, zones, news, sems)

        body(*ins, *outs, *scr)

        @pl.when(last)
        def _():
            comm.finish(srcs, zones, news, sems)

    res = pl.pallas_call(
        wrapped, name=name, grid=grid,
        in_specs=list(in_specs) + [ANY] * (ns + nz), out_specs=out_specs_l + [ANY] * (nz + nn),
        out_shape=out_shape_l + [jax.ShapeDtypeStruct(z.shape, z.dtype) for z in comm.zones] + comm.news,
        input_output_aliases={n_in + ns + i: n_out + i for i in range(nz)},
        scratch_shapes=list(scratch_shapes) + comm.sems,
        compiler_params=_cp(("arbitrary",) * len(grid)),
    )(*args, *comm.srcs, *comm.zones)
    main = res[0] if single else list(res[:n_out])
    return main, (list(res[n_out:n_out + nz]), list(res[n_out + nz:]))


def _run_comm(comm, name):
    ns, nz, nn = len(comm.srcs), len(comm.zones), len(comm.news)

    def body(*refs):
        srcs, zones, news, sems = refs[:ns], refs[ns + nz:ns + 2 * nz], refs[ns + 2 * nz:ns + 2 * nz + nn], refs[ns + 2 * nz + nn:]
        comm.start(srcs, zones, news, sems)
        comm.finish(srcs, zones, news, sems)

    res = pl.pallas_call(
        body, name=name, in_specs=[ANY] * (ns + nz), out_specs=[ANY] * (nz + nn),
        out_shape=[jax.ShapeDtypeStruct(z.shape, z.dtype) for z in comm.zones] + comm.news,
        input_output_aliases={ns + i: i for i in range(nz)}, scratch_shapes=comm.sems,
    )(*comm.srcs, *comm.zones)
    return list(res[:nz]), list(res[nz:])


def _dot(a, b):
    return jnp.dot(a, b, preferred_element_type=F32)


def _dot_nt(a, b):
    return lax.dot_general(a, b, (((1,), (1,)), ((), ())), preferred_element_type=F32)


def _dot_tn(a, b):
    return lax.dot_general(a, b, (((0,), (0,)), ((), ())), preferred_element_type=F32)


def _normal_cdf(x):
    return 0.5 * (1.0 + lax.erf(x * _SQRT_HALF))


def _gelu_and_grad(x, cdf):
    return x * cdf, cdf + x * jnp.exp(-0.5 * x * x) * _INV_SQRT_2PI


def _sigmoid(x):
    return 0.5 * jnp.tanh(0.5 * x) + 0.5


def _rms_bwd(dy, x, gain):
    r = lax.rsqrt(jnp.mean(x * x, axis=-1, keepdims=True) + EPS)
    xhat = x * r
    gdy = dy * gain
    dx = r * (gdy - xhat * jnp.mean(gdy * xhat, axis=-1, keepdims=True))
    return dx, jnp.sum(dy * xhat, axis=0, keepdims=True)


def _norm_matmul(x, gain, w4, name, comm=None, tm=512):
    S, D = x.shape
    nsh, _, ns = w4.shape
    tm = min(tm, S)

    def body(x_ref, g_ref, w_ref, hn_ref, o_ref):
        xf = x_ref[...]
        r = lax.rsqrt(jnp.mean(xf * xf, axis=-1, keepdims=True) + EPS)
        hn = (xf * r * g_ref[...]).astype(BF16)
        hn_ref[...] = hn
        for j in range(nsh):
            o_ref[:, j * ns:(j + 1) * ns] = _dot(hn, w_ref[j]).astype(BF16)

    return _call(
        body, name=name, grid=(S // tm,),
        in_specs=[pl.BlockSpec((tm, D), lambda i: (i, 0)), _resident((1, D)), _resident(w4.shape)],
        out_specs=[pl.BlockSpec((tm, D), lambda i: (i, 0)), pl.BlockSpec((tm, nsh * ns), lambda i: (i, 0))],
        out_shape=[jax.ShapeDtypeStruct((S, D), BF16), jax.ShapeDtypeStruct((S, nsh * ns), BF16)],
        sem=("parallel",), args=(x, gain, w4), comm=comm)


def _matmul_res(a, w, res, name, comm=None, tm=512):
    S, K = a.shape
    N = w.shape[1]
    tm = min(tm, S)

    def body(a_ref, w_ref, r_ref, o_ref):
        o_ref[...] = r_ref[...] + _dot(a_ref[...], w_ref[...])

    return _call(
        body, name=name, grid=(S // tm,),
        in_specs=[pl.BlockSpec((tm, K), lambda i: (i, 0)), _resident(w.shape), pl.BlockSpec((tm, N), lambda i: (i, 0))],
        out_specs=pl.BlockSpec((tm, N), lambda i: (i, 0)),
        out_shape=jax.ShapeDtypeStruct((S, N), F32),
        sem=("parallel",), args=(a, w, res), comm=comm)


def _matmul_nt(dh, w, name, comm=None, tm=512):
    S, N = dh.shape
    K = w.shape[0]
    tm = min(tm, S)

    def body(d_ref, w_ref, o_ref):
        o_ref[...] = _dot_nt(d_ref[...].astype(BF16), w_ref[...]).astype(BF16)

    return _call(
        body, name=name, grid=(S // tm,),
        in_specs=[pl.BlockSpec((tm, N), lambda i: (i, 0)), _resident(w.shape)],
        out_specs=pl.BlockSpec((tm, K), lambda i: (i, 0)),
        out_shape=jax.ShapeDtypeStruct((S, K), BF16),
        sem=("parallel",), args=(dh, w), comm=comm)


def _matmul_tn(a, b, name, *, ka, nb, out_dtype, comm=None, ts=512):
    S, KA = a.shape
    NB = b.shape[1]
    ts = min(ts, S)
    J = max(KA // ka, NB // nb)
    a_map = (lambda j, s: (s, j)) if KA // ka > 1 else (lambda j, s: (s, 0))
    b_map = (lambda j, s: (s, j)) if NB // nb > 1 else (lambda j, s: (s, 0))
    last = S // ts - 1

    def body(a_ref, b_ref, o_ref, acc):
        s = pl.program_id(1)

        @pl.when(s == 0)
        def _():
            acc[...] = jnp.zeros_like(acc)

        acc[...] += _dot_tn(a_ref[...].astype(BF16), b_ref[...].astype(BF16))

        @pl.when(s == last)
        def _():
            o_ref[0] = acc[...].astype(out_dtype)

    return _call(
        body, name=name, grid=(J, S // ts),
        in_specs=[pl.BlockSpec((ts, ka), a_map), pl.BlockSpec((ts, nb), b_map)],
        out_specs=pl.BlockSpec((1, ka, nb), lambda j, s: (j, 0, 0)),
        out_shape=jax.ShapeDtypeStruct((J, ka, nb), out_dtype),
        scratch_shapes=[pltpu.VMEM((ka, nb), F32)],
        sem=("parallel", "arbitrary"), args=(a, b), comm=comm)


def _sgu_prep(w_s):
    G = w_s.shape[0]

    def body(w_ref, t_ref, tt_ref):
        tri = lax.broadcasted_iota(jnp.int32, (CHUNK, CHUNK), 0) >= lax.broadcasted_iota(jnp.int32, (CHUNK, CHUNK), 1)
        for g in range(G):
            t = jnp.where(tri, w_ref[g], 0.0)
            t_ref[g] = t.astype(BF16)
            tt_ref[g] = t.T.astype(BF16)

    return pl.pallas_call(
        body, name="sgu_prep",
        out_shape=[jax.ShapeDtypeStruct(w_s.shape, BF16), jax.ShapeDtypeStruct(w_s.shape, BF16)],
        compiler_params=_cp(None),
    )(w_s)


def _sgu_fwd(z, x, vg, wtril, bT, wout, comm=None, tm=256):
    S = z.shape[0]
    W = z.shape[1] // 2
    D = x.shape[1]
    tm = min(tm, S)

    def body(z_ref, x_ref, vg_ref, wt_ref, bT_ref, wo_ref, yp_ref, h_ref, cdf_ref):
        def chunk(c, carry):
            r0 = pl.multiple_of(c * CHUNK, CHUNK)
            zc = z_ref[pl.ds(r0, CHUNK), :].astype(F32)
            cdf = _normal_cdf(zc)
            cdf_ref[pl.ds(r0, CHUNK), :] = cdf.astype(BF16)
            u = zc[:, :W] * cdf[:, :W]
            v = zc[:, W:] * cdf[:, W:]
            rv = lax.rsqrt(jnp.mean(v * v, axis=-1, keepdims=True) + EPS)
            vn = (v * rv * vg_ref[...]).astype(BF16)
            for g in range(SGU_GROUPS):
                sl = slice(g * LANES, (g + 1) * LANES)
                s = _dot(wt_ref[g], vn[:, sl]) + bT_ref[:, g:g + 1]
                yp_ref[pl.ds(r0, CHUNK), sl] = (u[:, sl] * s).astype(BF16)
            return carry

        lax.fori_loop(0, tm // CHUNK, chunk, 0)
        h_ref[...] = x_ref[...] + _dot(yp_ref[...], wo_ref[...])

    return _call(
        body, name="sgu_fwd", grid=(S // tm,),
        in_specs=[pl.BlockSpec((tm, 2 * W), lambda i: (i, 0)), pl.BlockSpec((tm, D), lambda i: (i, 0)),
                  _resident((1, W)), _resident(wtril.shape), _resident(bT.shape), _resident(wout.shape)],
        out_specs=[pl.BlockSpec((tm, W), lambda i: (i, 0)), pl.BlockSpec((tm, D), lambda i: (i, 0)),
                   pl.BlockSpec((tm, 2 * W), lambda i: (i, 0))],
        out_shape=[jax.ShapeDtypeStruct((S, W), BF16), jax.ShapeDtypeStruct((S, D), F32),
                   jax.ShapeDtypeStruct((S, 2 * W), BF16)],
        sem=("parallel",), args=(z, x, vg, wtril, bT, wout), comm=comm)


def _sgu_bwd(dh, z, cdf, vg, wtril, wtrilT, bT, wout, comm=None, tm=256):
    S = z.shape[0]
    W = z.shape[1] // 2
    D = dh.shape[1]
    G = SGU_GROUPS
    tm = min(tm, S)
    last = S // tm - 1

    def body(dh_ref, z_ref, cdf_ref, vg_ref, wt_ref, wtT_ref, bT_ref, wo_ref,
             dz_ref, dws_ref, dbT_ref, dvg_ref, dyp_s, du_s, dvn_s, dsacc):
        i = pl.program_id(0)

        @pl.when(i == 0)
        def _():
            dws_ref[...] = jnp.zeros_like(dws_ref)
            dvg_ref[...] = jnp.zeros_like(dvg_ref)
            dsacc[...] = jnp.zeros_like(dsacc)

        dyp_s[...] = _dot_nt(dh_ref[...].astype(BF16), wo_ref[...])
        tri = lax.broadcasted_iota(jnp.int32, (CHUNK, CHUNK), 0) >= lax.broadcasted_iota(jnp.int32, (CHUNK, CHUNK), 1)

        def chunk(c, carry):
            r0 = pl.multiple_of(c * CHUNK, CHUNK)
            zc = z_ref[pl.ds(r0, CHUNK), :].astype(F32)
            cdf = cdf_ref[pl.ds(r0, CHUNK), :].astype(F32)
            u, gu = _gelu_and_grad(zc[:, :W], cdf[:, :W])
            v, gv = _gelu_and_grad(zc[:, W:], cdf[:, W:])
            rv = lax.rsqrt(jnp.mean(v * v, axis=-1, keepdims=True) + EPS)
            vhat = v * rv
            vgain = vg_ref[...]
            vn = (vhat * vgain).astype(BF16)
            dyp = dyp_s[pl.ds(r0, CHUNK), :]
            for g in range(G):
                sl = slice(g * LANES, (g + 1) * LANES)
                vng = vn[:, sl]
                s = _dot(wt_ref[g], vng) + bT_ref[:, g:g + 1]
                ds = dyp[:, sl] * u[:, sl]
                du_s[:, sl] = dyp[:, sl] * s
                dsb = ds.astype(BF16)
                dvn_s[:, sl] = _dot(wtT_ref[g], dsb)
                dws_ref[g] += jnp.where(tri, _dot_nt(dsb, vng), 0.0)
                dsacc[g] += ds
            dvn = dvn_s[...]
            dvg_ref[...] += jnp.sum(dvn * vhat, axis=0, keepdims=True)
            gdy = dvn * vgain
            dv = rv * (gdy - vhat * jnp.mean(gdy * vhat, axis=-1, keepdims=True))
            dz_ref[pl.ds(r0, CHUNK), :W] = (du_s[...] * gu).astype(BF16)
            dz_ref[pl.ds(r0, CHUNK), W:] = (dv * gv).astype(BF16)
            return carry

        lax.fori_loop(0, tm // CHUNK, chunk, 0)

        @pl.when(i == last)
        def _():
            for g in range(G):
                dbT_ref[:, g:g + 1] = jnp.sum(dsacc[g], axis=1, keepdims=True)

    return _call(
        body, name="sgu_bwd", grid=(S // tm,),
        in_specs=[pl.BlockSpec((tm, D), lambda i: (i, 0)), pl.BlockSpec((tm, 2 * W), lambda i: (i, 0)),
                  pl.BlockSpec((tm, 2 * W), lambda i: (i, 0)),
                  _resident((1, W)), _resident(wtril.shape), _resident(wtrilT.shape), _resident(bT.shape),
                  _resident(wout.shape)],
        out_specs=[pl.BlockSpec((tm, 2 * W), lambda i: (i, 0)),
                   pl.BlockSpec((G, CHUNK, CHUNK), lambda i: (0, 0, 0)),
                   pl.BlockSpec((CHUNK, G), lambda i: (0, 0)),
                   pl.BlockSpec((1, W), lambda i: (0, 0))],
        out_shape=[jax.ShapeDtypeStruct((S, 2 * W), BF16), jax.ShapeDtypeStruct((G, CHUNK, CHUNK), F32),
                   jax.ShapeDtypeStruct((CHUNK, G), F32), jax.ShapeDtypeStruct((1, W), F32)],
        scratch_shapes=[pltpu.VMEM((tm, W), F32), pltpu.VMEM((CHUNK, W), F32), pltpu.VMEM((CHUNK, W), F32),
                        pltpu.VMEM((G, CHUNK, CHUNK), F32)],
        sem=("arbitrary",), args=(dh, z, cdf, vg, wtril, wtrilT, bT, wout), comm=comm)


def _conv_taps(a32, r0, R, reuse=False):
    base = r0 + HALO
    if not reuse:
        return a32[base:base + R, :], a32[base - 1:base - 1 + R, :], a32[base - 2:base - 2 + R, :]
    X = a32[base - 8:base + R, :]
    return X[8:], pltpu.roll(X, 1, 0)[8:], pltpu.roll(X, 2, 0)[8:]


def _ffn_fwd(a, h_in, cw, cb, wdown, name, comm=None, tm=256, R=64):
    S, C = a.shape
    F = C // 2
    D = h_in.shape[1]
    tm = min(tm, S)

    def body(a_ref, halo_ref, h_ref, cw_ref, cb_ref, wd_ref, f_ref, ho_ref, a32):
        i = pl.program_id(0)
        a32[0:HALO, :] = jnp.where(i > 0, halo_ref[...].astype(F32), 0.0)
        a32[HALO:, :] = a_ref[...].astype(F32)

        for r0 in range(0, tm, R):
            a0, a1, a2 = _conv_taps(a32, r0, R)
            cpre = cw_ref[0:1, :] * a2 + cw_ref[1:2, :] * a1 + cw_ref[2:3, :] * a0 + cb_ref[...]
            g = cpre[:, :F]
            f_ref[r0:r0 + R, :] = (g * _sigmoid(g) * cpre[:, F:]).astype(BF16)
        ho_ref[...] = h_ref[...] + _dot(f_ref[...], wd_ref[...])

    hb = tm // HALO
    return _call(
        body, name=name, grid=(S // tm,),
        in_specs=[pl.BlockSpec((tm, C), lambda i: (i, 0)),
                  pl.BlockSpec((HALO, C), lambda i: (jnp.maximum(i * hb - 1, 0), 0)),
                  pl.BlockSpec((tm, D), lambda i: (i, 0)),
                  _resident((3, C)), _resident((1, C)), _resident(wdown.shape)],
        out_specs=[pl.BlockSpec((tm, F), lambda i: (i, 0)), pl.BlockSpec((tm, D), lambda i: (i, 0))],
        out_shape=[jax.ShapeDtypeStruct((S, F), BF16), jax.ShapeDtypeStruct((S, D), F32)],
        scratch_shapes=[pltpu.VMEM((HALO + tm, C), F32)],
        sem=("parallel",), args=(a, a, h_in, cw, cb, wdown), comm=comm)


def _ffn_bwd_dc(dh, a, cw, cb, wdown, name, comm=None, tm=256, R=64):
    S, C = a.shape
    F = C // 2
    D = dh.shape[1]
    tm = min(tm, S)

    def body(dh_ref, a_ref, halo_ref, cw_ref, cb_ref, wd_ref, dc_ref, dcw_ref, dcb_ref, a32, df_s, acc):
        i = pl.program_id(0)

        @pl.when(i == 0)
        def _():
            acc[...] = jnp.zeros_like(acc)

        a32[0:HALO, :] = jnp.where(i > 0, halo_ref[...].astype(F32), 0.0)
        a32[HALO:, :] = a_ref[...].astype(F32)
        df_s[...] = _dot_nt(dh_ref[...].astype(BF16), wd_ref[...])
        rows8 = lambda v: functools.reduce(jnp.add, [v[8 * k:8 * k + 8] for k in range(R // 8)])

        for r0 in range(0, tm, R):
            a0, a1, a2 = _conv_taps(a32, r0, R, reuse=True)
            cpre = cw_ref[0:1, :] * a2 + cw_ref[1:2, :] * a1 + cw_ref[2:3, :] * a0 + cb_ref[...]
            g = cpre[:, :F]
            val = cpre[:, F:]
            sg = _sigmoid(g)
            df = df_s[r0:r0 + R, :]
            dg = df * val * (sg * (1.0 + g * (1.0 - sg)))
            dval = df * (g * sg)
            dc = jnp.concatenate([dg, dval], axis=1)
            dc_ref[r0:r0 + R, :] = dc.astype(BF16)
            acc[0] += rows8(dc * a2)
            acc[1] += rows8(dc * a1)
            acc[2] += rows8(dc * a0)
            acc[3] += rows8(dc)

        @pl.when(i == S // tm - 1)
        def _():
            for k in range(3):
                dcw_ref[k:k + 1, :] = jnp.sum(acc[k], axis=0, keepdims=True)
            dcb_ref[...] = jnp.sum(acc[3], axis=0, keepdims=True)

    hb = tm // HALO
    return _call(
        body, name=name, grid=(S // tm,),
        in_specs=[pl.BlockSpec((tm, D), lambda i: (i, 0)),
                  pl.BlockSpec((tm, C), lambda i: (i, 0)),
                  pl.BlockSpec((HALO, C), lambda i: (jnp.maximum(i * hb - 1, 0), 0)),
                  _resident((3, C)), _resident((1, C)), _resident(wdown.shape)],
        out_specs=[pl.BlockSpec((tm, C), lambda i: (i, 0)),
                   pl.BlockSpec((3, C), lambda i: (0, 0)), pl.BlockSpec((1, C), lambda i: (0, 0))],
        out_shape=[jax.ShapeDtypeStruct((S, C), BF16), jax.ShapeDtypeStruct((3, C), F32),
                   jax.ShapeDtypeStruct((1, C), F32)],
        scratch_shapes=[pltpu.VMEM((HALO + tm, C), F32), pltpu.VMEM((tm, F), F32), pltpu.VMEM((4, 8, C), F32)],
        sem=("arbitrary",), args=(dh, a, a, cw, cb, wdown), comm=comm)


def _bwd_norm(dA, w4, h_in, gain, dh_out, name, conv_w=None, comm=None, tm=256, R=64):
    S, N = dA.shape
    nsh, D, ns = w4.shape
    tm = min(tm, S)
    nt = S // tm
    conv = conv_w is not None

    def finish(src_ref, w_ref, h_ref, g_ref, dho_ref, dhi_ref, dg_ref):
        dhn = _dot_nt(src_ref[:, 0:ns], w_ref[0])
        for j in range(1, nsh):
            dhn += _dot_nt(src_ref[:, j * ns:(j + 1) * ns], w_ref[j])
        dx, dgain = _rms_bwd(dhn, h_ref[...], g_ref[...])
        dg_ref[...] += dgain
        dhi_ref[...] = dho_ref[...] + dx

    def body_plain(dA_ref, w_ref, h_ref, g_ref, dho_ref, dhi_ref, dg_ref):
        @pl.when(pl.program_id(0) == 0)
        def _():
            dg_ref[...] = jnp.zeros_like(dg_ref)

        finish(dA_ref, w_ref, h_ref, g_ref, dho_ref, dhi_ref, dg_ref)

    def body_conv(dc_ref, halo_ref, cw_ref, w_ref, h_ref, g_ref, dho_ref, dhi_ref, dg_ref, da_ref, dc32):
        i = pl.program_id(0)

        @pl.when(i == 0)
        def _():
            dg_ref[...] = jnp.zeros_like(dg_ref)

        dc32[0:tm, :] = dc_ref[...].astype(F32)
        dc32[tm:, :] = jnp.where(i < nt - 1, halo_ref[...].astype(F32), 0.0)

        for r0 in range(0, tm, R):
            d0 = dc32[r0:r0 + R, :]
            d1 = dc32[r0 + 1:r0 + 1 + R, :]
            d2 = dc32[r0 + 2:r0 + 2 + R, :]
            da = cw_ref[2:3, :] * d0 + cw_ref[1:2, :] * d1 + cw_ref[0:1, :] * d2
            da_ref[r0:r0 + R, :] = da.astype(BF16)
        finish(da_ref, w_ref, h_ref, g_ref, dho_ref, dhi_ref, dg_ref)

    row = lambda width: pl.BlockSpec((tm, width), lambda i: (i, 0))
    common_in = [_resident(w4.shape), row(D), _resident((1, D)), row(D)]
    common_out = [row(D), pl.BlockSpec((1, D), lambda i: (0, 0))]
    common_shape = [jax.ShapeDtypeStruct((S, D), F32), jax.ShapeDtypeStruct((1, D), F32)]
    if not conv:
        return _call(
            body_plain, name=name, grid=(nt,),
            in_specs=[row(N)] + common_in, out_specs=common_out, out_shape=common_shape,
            sem=("arbitrary",), args=(dA, w4, h_in, gain, dh_out), comm=comm)
    hb = tm // HALO
    nhb = S // HALO
    return _call(
        body_conv, name=name, grid=(nt,),
        in_specs=[row(N), pl.BlockSpec((HALO, N), lambda i: (jnp.minimum((i + 1) * hb, nhb - 1), 0)),
                  _resident((3, N))] + common_in,
        out_specs=common_out + [row(N)],
        out_shape=common_shape + [jax.ShapeDtypeStruct((S, N), BF16)],
        scratch_shapes=[pltpu.VMEM((tm + HALO, N), F32)],
        sem=("arbitrary",), args=(dA, dA, conv_w, w4, h_in, gain, dh_out), comm=comm)


def _rel_buckets_flat():
    q = np.arange(CHUNK)[:, None] + CHUNK
    k = np.arange(2 * CHUNK)[None, :]
    n = np.maximum(q - k, 0)
    max_exact = REL_BUCKETS // 2
    large = max_exact + (np.log(np.maximum(n, 1).astype(np.float32) / max_exact)
                         / math.log(REL_MAX_DIST / max_exact) * (REL_BUCKETS - max_exact)).astype(np.int32)
    large = np.minimum(large, REL_BUCKETS - 1)
    return np.where(n < max_exact, n, large).astype(np.int32).reshape(1, CHUNK * 2 * CHUNK)


def _split_bf16(x):
    hi = x.astype(BF16)
    return hi, (x - hi.astype(F32)).astype(BF16)


def _rel_bias_expand(rel_bias, bucket):
    B, H = rel_bias.shape
    n = bucket.shape[1]

    def body(rb_ref, bk_ref, o_ref):
        oh = (bk_ref[...] == lax.broadcasted_iota(jnp.int32, (B, n), 0)).astype(BF16)
        hi, lo = _split_bf16(rb_ref[...])
        o_ref[...] = _dot_tn(hi, oh) + _dot_tn(lo, oh)

    return pl.pallas_call(body, name="rel_bias_expand", out_shape=jax.ShapeDtypeStruct((H, n), F32),
                          compiler_params=_cp(None))(rel_bias, bucket)


def _rel_bias_reduce(dbias, bucket):
    H, n = dbias.shape
    B = REL_BUCKETS

    def body(db_ref, bk_ref, o_ref):
        oh = (bk_ref[...] == lax.broadcasted_iota(jnp.int32, (B, n), 0)).astype(BF16)
        hi, lo = _split_bf16(db_ref[...])
        o_ref[...] = _dot_nt(oh, hi) + _dot_nt(oh, lo)

    return pl.pallas_call(body, name="rel_bias_reduce", out_shape=jax.ShapeDtypeStruct((B, H), F32),
                          compiler_params=_cp(None))(dbias, bucket)


def _lo_mask(rows):
    return lax.broadcasted_iota(jnp.int32, (rows, LANES), 1) < HEAD_DIM


def _half_sums(y, lo):
    s_lo = jnp.sum(jnp.where(lo, y, 0.0), axis=-1, keepdims=True)
    s_hi = jnp.sum(jnp.where(lo, 0.0, y), axis=-1, keepdims=True)
    return jnp.where(lo, s_lo, s_hi)


def _half_rms(x, gain, lo):
    r = lax.rsqrt(_half_sums(x * x, lo) * (1.0 / HEAD_DIM) + EPS)
    xhat = x * r
    return xhat * gain, xhat, r


def _half_rms_bwd(dy, xhat, r, gain, lo):
    gdy = dy * gain
    dx = r * (gdy - xhat * (_half_sums(gdy * xhat, lo) * (1.0 / HEAD_DIM)))
    return dx, jnp.sum(dy * xhat, axis=0, keepdims=True)


def _dup_half(pair, e, lo):
    sw = pltpu.roll(pair, HEAD_DIM, 1)
    return jnp.where(lo, pair, sw) if e == 0 else jnp.where(lo, sw, pair)


def _band_valid(n):
    qi = lax.broadcasted_iota(jnp.int32, (KV_GROUP * CHUNK, 2 * CHUNK), 0) & (CHUNK - 1)
    ki = lax.broadcasted_iota(jnp.int32, (KV_GROUP * CHUNK, 2 * CHUNK), 1)
    dist = qi + CHUNK - ki
    return (dist >= 0) & (dist < CHUNK) & ((n > 0) | (ki >= CHUNK))


def _stack_heads(a, b, lo):
    return jnp.concatenate([jnp.where(lo, a, 0.0), jnp.where(lo, 0.0, a), jnp.where(lo, b, 0.0), jnp.where(lo, 0.0, b)],
                           axis=0)


def _unstack_heads(x4, lo):
    return (jnp.where(lo, x4[0:CHUNK], x4[CHUNK:2 * CHUNK]),
            jnp.where(lo, x4[2 * CHUNK:3 * CHUNK], x4[3 * CHUNK:]))


def _sink_col(sink_ref, hk):
    row = lax.broadcasted_iota(jnp.int32, (KV_GROUP * CHUNK, 1), 0)
    col = jnp.full((KV_GROUP * CHUNK, 1), sink_ref[KV_GROUP * hk + KV_GROUP - 1], F32)
    for r in range(KV_GROUP - 2, -1, -1):
        col = jnp.where(row < (r + 1) * CHUNK, sink_ref[KV_GROUP * hk + r], col)
    return col


def _softmax_sink(s, valid, sink):
    s = jnp.where(valid, s, -jnp.inf)
    m = jnp.maximum(jnp.max(s, axis=-1, keepdims=True), sink)
    p = jnp.exp(s - m)
    esink = jnp.exp(sink - m)
    inv = 1.0 / (jnp.sum(p, axis=-1, keepdims=True) + esink)
    return p * inv, esink * inv


QW = N_HEADS * HEAD_DIM
KVW = N_KV_HEADS * HEAD_DIM


def _attn_fwd(qkv, qg2, kg2, sinks, bias, comm=None):
    S = qkv.shape[0]
    nb = S // CHUNK

    def body(cur_ref, prev_ref, qg_ref, kg_ref, sink_ref, bias_ref, o_ref):
        n = pl.program_id(0)
        lo = _lo_mask(CHUNK)
        lo2 = _lo_mask(2 * CHUNK)
        valid = _band_valid(n)
        for j in range(N_KV_HEADS // 2):
            kc = slice(QW + j * LANES, QW + (j + 1) * LANES)
            vc = slice(QW + KVW + j * LANES, QW + KVW + (j + 1) * LANES)
            kpair = jnp.concatenate([prev_ref[:, j * LANES:(j + 1) * LANES], cur_ref[:, kc]], axis=0).astype(F32)
            vpair = jnp.concatenate([prev_ref[:, KVW + j * LANES:KVW + (j + 1) * LANES], cur_ref[:, vc]], axis=0).astype(F32)
            knpair, _, _ = _half_rms(kpair, kg_ref[...], lo2)
            for e in range(2):
                hk = 2 * j + e
                kdup = _dup_half(knpair, e, lo2).astype(BF16)
                vdup = _dup_half(vpair, e, lo2).astype(BF16)
                ca = slice(2 * hk * LANES, (2 * hk + 1) * LANES)
                cb = slice((2 * hk + 1) * LANES, (2 * hk + 2) * LANES)
                qna, _, _ = _half_rms(cur_ref[:, ca].astype(F32), qg_ref[...], lo)
                qnb, _, _ = _half_rms(cur_ref[:, cb].astype(F32), qg_ref[...], lo)
                qm4 = _stack_heads(qna, qnb, lo).astype(BF16)
                s = _dot_nt(qm4, kdup) * (HEAD_DIM ** -0.5) + bias_ref[hk]
                p, _ = _softmax_sink(s, valid, _sink_col(sink_ref, hk))
                oa, ob = _unstack_heads(_dot(p.astype(BF16), vdup), lo)
                o_ref[:, ca] = oa.astype(BF16)
                o_ref[:, cb] = ob.astype(BF16)

    return _call(
        body, name="attn_fwd", grid=(nb,),
        in_specs=[pl.BlockSpec((CHUNK, QW + 2 * KVW), lambda n: (n, 0)),
                  pl.BlockSpec((CHUNK, 2 * KVW), lambda n: (jnp.maximum(n - 1, 0), QW // (2 * KVW))),
                  _resident((1, LANES)), _resident((1, LANES)),
                  pl.BlockSpec(memory_space=pltpu.SMEM),
                  _resident(bias.shape)],
        out_specs=pl.BlockSpec((CHUNK, QW), lambda n: (n, 0)),
        out_shape=jax.ShapeDtypeStruct((S, QW), BF16),
        sem=("parallel",), args=(qkv, qkv, qg2, kg2, sinks, bias), comm=comm)


def _attn_bwd(qkv, do, qg2, kg2, sinks, bias, comm=None):
    S = qkv.shape[0]
    nb = S // CHUNK

    def body(cur_ref, prev_ref, do_ref, qg_ref, kg_ref, sink_ref, bias_ref,
             dqkv_ref, dbias_ref, dqg_ref, dkg_ref, dsink_ref, carry, band, dsacc, gacc):
        i = pl.program_id(0)
        n = nb - 1 - i
        lo = _lo_mask(CHUNK)
        lo2 = _lo_mask(2 * CHUNK)
        lane = lax.broadcasted_iota(jnp.int32, (KV_GROUP * CHUNK, LANES), 1)
        valid = _band_valid(n)

        @pl.when(i == 0)
        def _():
            dbias_ref[...] = jnp.zeros_like(dbias_ref)
            carry[...] = jnp.zeros_like(carry)
            dsacc[...] = jnp.zeros_like(dsacc)
            gacc[...] = jnp.zeros_like(gacc)

        qgain = qg_ref[...]
        kgain = kg_ref[...]
        for j in range(N_KV_HEADS // 2):
            kc = slice(QW + j * LANES, QW + (j + 1) * LANES)
            vc = slice(QW + KVW + j * LANES, QW + KVW + (j + 1) * LANES)
            kpair = jnp.concatenate([prev_ref[:, j * LANES:(j + 1) * LANES], cur_ref[:, kc]], axis=0).astype(F32)
            vpair = jnp.concatenate([prev_ref[:, KVW + j * LANES:KVW + (j + 1) * LANES], cur_ref[:, vc]], axis=0).astype(F32)
            knpair, khat, kr = _half_rms(kpair, kgain, lo2)
            dk_folds = []
            dv_folds = []
            for e in range(2):
                hk = 2 * j + e
                kdup = _dup_half(knpair, e, lo2).astype(BF16)
                vdup = _dup_half(vpair, e, lo2).astype(BF16)
                ca = slice(2 * hk * LANES, (2 * hk + 1) * LANES)
                cb = slice((2 * hk + 1) * LANES, (2 * hk + 2) * LANES)
                qna, qhata, qra = _half_rms(cur_ref[:, ca].astype(F32), qgain, lo)
                qnb, qhatb, qrb = _half_rms(cur_ref[:, cb].astype(F32), qgain, lo)
                qm4 = _stack_heads(qna, qnb, lo).astype(BF16)
                dom4 = _stack_heads(do_ref[:, ca].astype(F32), do_ref[:, cb].astype(F32), lo).astype(BF16)
                s = _dot_nt(qm4, kdup) * (HEAD_DIM ** -0.5) + bias_ref[hk]
                p, psink = _softmax_sink(s, valid, _sink_col(sink_ref, hk))
                dp = _dot_nt(dom4, vdup)
                delta = jnp.sum(p * dp, axis=-1, keepdims=True)
                ds = p * (dp - delta)
                dbias_ref[hk] += ds
                dsacc[...] += jnp.where(lane == hk, -(psink * delta), 0.0)
                dsr = (ds * (HEAD_DIM ** -0.5)).astype(BF16)
                dqna, dqnb = _unstack_heads(_dot(dsr, kdup), lo)
                dkd = _dot_tn(dsr, qm4)
                dvd = _dot_tn(p.astype(BF16), dom4)
                dqa, dqga = _half_rms_bwd(dqna, qhata, qra, qgain, lo)
                dqb, dqgb = _half_rms_bwd(dqnb, qhatb, qrb, qgain, lo)
                gacc[0:1, :] += dqga + dqgb
                dqkv_ref[:, ca] = dqa.astype(BF16)
                dqkv_ref[:, cb] = dqb.astype(BF16)
                dk_folds.append(dkd + pltpu.roll(dkd, HEAD_DIM, 1))
                dv_folds.append(dvd + pltpu.roll(dvd, HEAD_DIM, 1))
            dkn = jnp.where(lo2, dk_folds[0], dk_folds[1])
            dk, dkg = _half_rms_bwd(dkn, khat, kr, kgain, lo2)
            gacc[1:2, :] += dkg
            band[:, j * LANES:(j + 1) * LANES] = dk
            band[:, KVW + j * LANES:KVW + (j + 1) * LANES] = jnp.where(lo2, dv_folds[0], dv_folds[1])
        dqkv_ref[:, QW:] = (band[CHUNK:, :] + carry[...]).astype(BF16)
        carry[...] = band[0:CHUNK, :]

        @pl.when(i == nb - 1)
        def _():
            g = gacc[...]
            g = g + pltpu.roll(g, HEAD_DIM, 1)
            dqg_ref[...] = g[0:1, :]
            dkg_ref[...] = g[1:2, :]
            for r in range(KV_GROUP):
                dsink_ref[r:r + 1, :] = jnp.sum(dsacc[r * CHUNK:(r + 1) * CHUNK, :], axis=0, keepdims=True)

    vec = pl.BlockSpec((1, LANES), lambda i: (0, 0))
    return _call(
        body, name="attn_bwd", grid=(nb,),
        in_specs=[pl.BlockSpec((CHUNK, QW + 2 * KVW), lambda i: (nb - 1 - i, 0)),
                  pl.BlockSpec((CHUNK, 2 * KVW), lambda i: (jnp.maximum(nb - 2 - i, 0), QW // (2 * KVW))),
                  pl.BlockSpec((CHUNK, QW), lambda i: (nb - 1 - i, 0)),
                  _resident((1, LANES)), _resident((1, LANES)),
                  pl.BlockSpec(memory_space=pltpu.SMEM),
                  _resident(bias.shape)],
        out_specs=[pl.BlockSpec((CHUNK, QW + 2 * KVW), lambda i: (nb - 1 - i, 0)),
                   pl.BlockSpec(bias.shape, lambda i: (0, 0, 0)), vec, vec,
                   pl.BlockSpec((KV_GROUP, LANES), lambda i: (0, 0))],
        out_shape=[jax.ShapeDtypeStruct((S, QW + 2 * KVW), BF16),
                   jax.ShapeDtypeStruct(bias.shape, F32),
                   jax.ShapeDtypeStruct((1, LANES), F32), jax.ShapeDtypeStruct((1, LANES), F32),
                   jax.ShapeDtypeStruct((KV_GROUP, LANES), F32)],
        scratch_shapes=[pltpu.VMEM((CHUNK, 2 * KVW), F32), pltpu.VMEM((2 * CHUNK, 2 * KVW), F32),
                        pltpu.VMEM((KV_GROUP * CHUNK, LANES), F32), pltpu.VMEM((8, LANES), F32)],
        sem=("arbitrary",), args=(qkv, qkv, do, qg2, kg2, sinks, bias), comm=comm)


def _loss_head(y, target, tm=512):
    S, D = y.shape
    tm = min(tm, S)

    def body(y_ref, t_ref, dy_ref, l_ref):
        @pl.when(pl.program_id(0) == 0)
        def _():
            l_ref[...] = jnp.zeros_like(l_ref)

        e = y_ref[...] - t_ref[...]
        dy_ref[...] = e * (1.0 / D)
        rows = jnp.sum(e * e, axis=-1, keepdims=True) * (1.0 / D)
        l_ref[...] += 0.5 * jnp.sum(rows, axis=0, keepdims=True)

    dy, l = pl.pallas_call(
        body, name="loss_head", grid=(S // tm,),
        in_specs=[pl.BlockSpec((tm, D), lambda i: (i, 0)), pl.BlockSpec((tm, D), lambda i: (i, 0))],
        out_specs=[pl.BlockSpec((tm, D), lambda i: (i, 0)), pl.BlockSpec((1, 1), lambda i: (0, 0))],
        out_shape=[jax.ShapeDtypeStruct((S, D), F32), jax.ShapeDtypeStruct((1, 1), F32)],
        compiler_params=_cp(("arbitrary",)),
    )(y, target)
    return dy, l[0, 0]


def _row_tile(rows, cols, n_arrays):
    budget = VMEM_LIMIT_V7X // 4 // (2 * n_arrays * 4 * cols)
    best = 8
    for n in range(1, rows // 8 + 1):
        if rows % n == 0 and (rows // n) % 8 == 0 and rows // n <= budget:
            best = rows // n
            break
    return best


def _adamw(g, w, m, v, name, comm=None):
    R, C = g.shape
    tr = _row_tile(R, C, 8)

    def body(g_ref, w_ref, m_ref, v_ref, d_ref, mo_ref, vo_ref, go_ref):
        gg = g_ref[...]
        go_ref[...] = gg
        mn = ADAM_B1 * m_ref[...] + (1.0 - ADAM_B1) * gg
        vn = ADAM_B2 * v_ref[...] + (1.0 - ADAM_B2) * jnp.square(gg)
        m_hat = mn / (1.0 - ADAM_B1 ** ADAM_STEP)
        v_hat = vn / (1.0 - ADAM_B2 ** ADAM_STEP)
        d_ref[...] = -ADAM_LR * (m_hat / (jnp.sqrt(v_hat) + ADAM_EPS) + ADAM_WD * w_ref[...])
        mo_ref[...] = mn
        vo_ref[...] = vn

    spec = pl.BlockSpec((tr, C), lambda i: (i, 0))
    return _call(
        body, name=name, grid=(R // tr,), in_specs=[spec] * 4, out_specs=[spec] * 4,
        out_shape=[jax.ShapeDtypeStruct((R, C), F32)] * 4, sem=("parallel",), args=(g, w, m, v), comm=comm)


def _place_shard(shards, layer, where, dtype, name):
    _, R, C = shards.shape
    tr = _row_tile(R, C, 2) if R % 8 == 0 else R

    def body(s_ref, x_ref, o_ref):
        o_ref[...] = x_ref[...].astype(dtype)

    return pl.pallas_call(
        body, name=name,
        grid_spec=pltpu.PrefetchScalarGridSpec(
            num_scalar_prefetch=1, grid=(R // tr,),
            in_specs=[pl.BlockSpec((1, tr, C), lambda i, s_ref: (layer, i, 0))],
            out_specs=pl.BlockSpec((1, tr, C), lambda i, s_ref: (s_ref[1], i, 0))),
        out_shape=jax.ShapeDtypeStruct((4, R, C), dtype),
        compiler_params=_cp(("parallel",)),
    )(where, shards)


def _pair_add(g4, rsib, where, name):
    J, R, C = g4.shape
    Rh = R // 2
    tr = _row_tile(Rh, C, 4)
    g5 = g4.reshape(J, 2, Rh, C)

    def body(s_ref, g_ref, r_ref, p_ref, q_ref):
        val = (g_ref[...].astype(F32)[0] + r_ref[...].astype(F32)).astype(BF16)
        p_ref[...] = val

        @pl.when(pl.program_id(1) == s_ref[1])
        def _():
            q_ref[...] = val

    return pl.pallas_call(
        body, name=name,
        grid_spec=pltpu.PrefetchScalarGridSpec(
            num_scalar_prefetch=1, grid=(Rh // tr, J),
            in_specs=[pl.BlockSpec((1, 1, tr, C), lambda i, j, s_ref: (j, s_ref[0], i, 0)),
                      pl.BlockSpec((1, tr, C), lambda i, j, s_ref: (j, i, 0))],
            out_specs=[pl.BlockSpec((1, tr, C), lambda i, j, s_ref: (j, i, 0)),
                       pl.BlockSpec((1, tr, C), lambda i, j, s_ref: (s_ref[1], i, 0))]),
        out_shape=[jax.ShapeDtypeStruct((J, Rh, C), BF16)] * 2,
        compiler_params=_cp(("parallel", "arbitrary")),
    )(where, g5, rsib)


def _sum_chips(q, where, dest, layer, out_shape, name):
    J, Rh, C = q.shape
    tr = _row_tile(Rh, C, 3)
    nb = Rh // tr

    def body(s_ref, q_ref, *rest):
        qq = q_ref[...].astype(F32)
        rest[-1][0] = ((qq[0] + qq[1]) + qq[2]) + qq[3]

    have = dest is not None
    return pl.pallas_call(
        body, name=name,
        grid_spec=pltpu.PrefetchScalarGridSpec(
            num_scalar_prefetch=1, grid=(nb,),
            in_specs=[pl.BlockSpec((J, tr, C), lambda i, s_ref: (0, i, 0))] + ([ANY] if have else []),
            out_specs=pl.BlockSpec((1, tr, C), lambda i, s_ref: (layer, s_ref[0] * nb + i, 0))),
        out_shape=jax.ShapeDtypeStruct(out_shape, F32),
        input_output_aliases={2: 0} if have else {},
        compiler_params=_cp(("parallel",)),
    )(*((where, q, dest) if have else (where, q)))


MESH = pl.DeviceIdType.MESH
ANY = pl.BlockSpec(memory_space=pl.ANY)


def _place():
    x, y, c = lax.axis_index("x"), lax.axis_index("y"), lax.axis_index("c")
    others = [(1 - x, y), (x, 1 - y), (1 - x, 1 - y)]
    return x, y, c, 2 * x + y, others, [2 * ox + oy for ox, oy in others]


def _gather_comm(placed, split):
    n = len(placed)

    def rows(t, ref, half):
        if not split[t]:
            return ref
        rh = placed[t].shape[1] // 2
        return ref.at[pl.ds(half * rh, rh), :]

    def sends(outs, sems):
        send, recv = sems[0], sems[1]
        x, y, c, me, others, okey = _place()
        cps = []
        for t in range(n):
            mine = rows(t, outs[t].at[me], c)
            for j, (ox, oy) in enumerate(others):
                cps.append(pltpu.make_async_remote_copy(
                    src_ref=mine, dst_ref=mine,
                    send_sem=send.at[t, j], recv_sem=recv.at[t, j], device_id=(ox, oy, c), device_id_type=MESH))
        return cps

    def start(srcs, outs, news, sems):
        for cp in sends(outs, sems):
            cp.start()

    def finish(srcs, outs, news, sems):
        send, recv, fsend, frecv = sems
        x, y, c, me, others, okey = _place()
        sib = (x, y, 1 - c)
        first = sends(outs, sems)
        passed = []
        for t in range(n):
            for j in range(3):
                landed = rows(t, outs[t].at[okey[j]], c)
                pltpu.make_async_remote_copy(
                    src_ref=landed, dst_ref=landed, send_sem=send.at[t, j], recv_sem=recv.at[t, j],
                    device_id=sib, device_id_type=MESH).wait_recv()
                if split[t]:
                    cp = pltpu.make_async_remote_copy(
                        src_ref=landed, dst_ref=landed, send_sem=fsend.at[t, j], recv_sem=frecv.at[t, j],
                        device_id=sib, device_id_type=MESH)
                    cp.start()
                    passed.append(cp)
        for t in range(n):
            if split[t]:
                for j in range(3):
                    theirs = rows(t, outs[t].at[okey[j]], 1 - c)
                    pltpu.make_async_remote_copy(
                        src_ref=theirs, dst_ref=theirs, send_sem=fsend.at[t, j], recv_sem=frecv.at[t, j],
                        device_id=sib, device_id_type=MESH).wait_recv()
        for cp in first + passed:
            cp.wait_send()

    return _Comm([], placed, [], [pltpu.SemaphoreType.DMA((n, 3))] * 4, start, finish)


def _pair_exchange_comm(gs):
    n = len(gs)

    def copies(ins, outs, sems):
        send, recv = sems
        x, y, c, _, _, _ = _place()
        cps = []
        for t in range(n):
            rh = gs[t].shape[1] // 2
            cps.append(pltpu.make_async_remote_copy(
                src_ref=ins[t].at[:, pl.ds((1 - c) * rh, rh), :], dst_ref=outs[t],
                send_sem=send.at[t], recv_sem=recv.at[t], device_id=(x, y, 1 - c), device_id_type=MESH))
        return cps

    def start(ins, zones, outs, sems):
        for cp in copies(ins, outs, sems):
            cp.start()

    def finish(ins, zones, outs, sems):
        for cp in copies(ins, outs, sems):
            cp.wait()

    news = [jax.ShapeDtypeStruct((4, g.shape[1] // 2, g.shape[2]), g.dtype) for g in gs]
    return _Comm(gs, [], news, [pltpu.SemaphoreType.DMA((n,))] * 2, start, finish)


def _chip_scatter_comm(ps, qs):
    n = len(ps)

    def sends(ins, outs, sems):
        send, recv = sems
        x, y, c, me, others, okey = _place()
        return [pltpu.make_async_remote_copy(
            src_ref=ins[t].at[okey[j]], dst_ref=outs[t].at[me],
            send_sem=send.at[t, j], recv_sem=recv.at[t, j], device_id=(ox, oy, c), device_id_type=MESH)
            for t in range(n) for j, (ox, oy) in enumerate(others)]

    def start(ins, outs, news, sems):
        for cp in sends(ins, outs, sems):
            cp.start()

    def finish(ins, outs, news, sems):
        send, recv = sems
        x, y, c, me, others, okey = _place()
        for t in range(n):
            for j in range(3):
                slot = outs[t].at[okey[j]]
                pltpu.make_async_remote_copy(
                    src_ref=slot, dst_ref=slot, send_sem=send.at[t, j], recv_sem=recv.at[t, j],
                    device_id=(x, y, c), device_id_type=MESH).wait_recv()
        for cp in sends(ins, outs, sems):
            cp.wait_send()

    return _Comm(ps, qs, [], [pltpu.SemaphoreType.DMA((n, 3))] * 2, start, finish)


def _half_exchange_comm(arrs, layers=None):
    n = len(arrs)
    items = [(t, layer) for t in range(n) for layer in (range(arrs[t].shape[0]) if layers is None else layers[t])]

    def sends(outs, sems):
        send, recv = sems
        x, y, c, _, _, _ = _place()
        cps = []
        for k, (t, layer) in enumerate(items):
            rh = arrs[t].shape[1] // 2
            mine = outs[t].at[layer, pl.ds(c * rh, rh), :]
            cps.append(pltpu.make_async_remote_copy(
                src_ref=mine, dst_ref=mine, send_sem=send.at[k], recv_sem=recv.at[k],
                device_id=(x, y, 1 - c), device_id_type=MESH))
        return cps

    def start(srcs, outs, news, sems):
        for cp in sends(outs, sems):
            cp.start()

    def finish(srcs, outs, news, sems):
        send, recv = sems
        x, y, c, _, _, _ = _place()
        for k, (t, layer) in enumerate(items):
            rh = arrs[t].shape[1] // 2
            theirs = outs[t].at[layer, pl.ds((1 - c) * rh, rh), :]
            pltpu.make_async_remote_copy(
                src_ref=theirs, dst_ref=theirs, send_sem=send.at[k], recv_sem=recv.at[k],
                device_id=(x, y, 1 - c), device_id_type=MESH).wait_recv()
        for cp in sends(outs, sems):
            cp.wait_send()

    return _Comm([], arrs, [], [pltpu.SemaphoreType.DMA((len(items),))] * 2, start, finish)


SMALL_COLS = 1024
SMALL_PIECE_ROWS = 40


def _allreduce_small(buf, comm=None):
    pr = SMALL_PIECE_ROWS
    flips = [(d >> 2 & 1, d >> 1 & 1, d & 1) for d in range(1, 8)]

    def body(x_ref, o_ref, rbuf, send1, recv1, send2, recv2):
        x, y, c = lax.axis_index("x"), lax.axis_index("y"), lax.axis_index("c")
        me = 4 * x + 2 * y + c
        peers = [(x ^ fx, y ^ fy, c ^ fc) for fx, fy, fc in flips]
        pid = [4 * px + 2 * py + pc for px, py, pc in peers]

        def piece(ref, p):
            return ref.at[pl.ds(pl.multiple_of(p * pr, 8), pr), :]

        cps = []
        for d in range(7):
            cp = pltpu.make_async_remote_copy(
                src_ref=piece(x_ref, pid[d]), dst_ref=rbuf.at[d + 1],
                send_sem=send1.at[d], recv_sem=recv1.at[d], device_id=peers[d], device_id_type=MESH)
            cp.start()
            cps.append(cp)
        acc = piece(x_ref, me)[...]
        for d in range(7):
            cps[d].wait_recv()
            acc = acc + rbuf[d + 1]
        piece(o_ref, me)[...] = acc
        out = []
        for d in range(7):
            cp = pltpu.make_async_remote_copy(
                src_ref=piece(o_ref, me), dst_ref=piece(o_ref, me),
                send_sem=send2.at[d], recv_sem=recv2.at[d], device_id=peers[d], device_id_type=MESH)
            cp.start()
            out.append(cp)
        for d in range(7):
            pltpu.make_async_remote_copy(
                src_ref=piece(o_ref, pid[d]), dst_ref=piece(o_ref, pid[d]),
                send_sem=send2.at[d], recv_sem=recv2.at[d], device_id=peers[d], device_id_type=MESH).wait_recv()
        for cp in cps + out:
            cp.wait_send()

    vm = pl.BlockSpec(memory_space=pltpu.VMEM)
    return _call(
        body, name="small_allreduce", grid=(), in_specs=[vm], out_specs=vm,
        out_shape=jax.ShapeDtypeStruct(buf.shape, F32),
        scratch_shapes=[pltpu.VMEM((8, pr, SMALL_COLS), F32)] + [pltpu.SemaphoreType.DMA((7,))] * 4,
        args=(buf,), comm=comm)


def _pack(arrays, rows):
    flat = jnp.concatenate([a.reshape(-1) for a in arrays])
    return jnp.pad(flat, (0, rows * SMALL_COLS - flat.shape[0])).reshape(rows, SMALL_COLS)


def _unpack(buf, shapes):
    flat = buf.reshape(-1)
    out, off = [], 0
    for s in shapes:
        k = math.prod(s)
        out.append(flat[off:off + k].reshape(s))
        off += k
    return out


BIG = ["sgu_w_in", "sgu_w_out", "attn_w_qkv", "attn_w_o", "ffn_w_up", "ffn_w_down"]
SMALL = ["mix_norm", "ffn_norm", "sgu_v_gain", "sgu_w_s", "sgu_b_s", "attn_q_gain", "attn_k_gain", "attn_sinks",
         "rel_bias", "ffn_conv_b"]
ORDER = ["mix_norm", "ffn_norm", "sgu_w_in", "sgu_v_gain", "sgu_w_s", "sgu_b_s", "sgu_w_out", "attn_w_qkv",
         "attn_q_gain", "attn_k_gain", "attn_sinks", "attn_w_o", "rel_bias", "ffn_w_up", "ffn_conv_w", "ffn_conv_b",
         "ffn_w_down"]


def kernel(x, mix_norm, ffn_norm, sgu_w_in, sgu_v_gain, sgu_w_s, sgu_b_s, sgu_w_out, attn_w_qkv, attn_q_gain, attn_k_gain, attn_sinks, attn_w_o, rel_bias, ffn_w_up, ffn_conv_w, ffn_conv_b, ffn_w_down, loss_target, m_mix_norm, m_ffn_norm, m_sgu_w_in, m_sgu_v_gain, m_sgu_w_s, m_sgu_b_s, m_sgu_w_out, m_attn_w_qkv, m_attn_q_gain, m_attn_k_gain, m_attn_sinks, m_attn_w_o, m_rel_bias, m_ffn_w_up, m_ffn_conv_w, m_ffn_conv_b, m_ffn_w_down, v_mix_norm, v_ffn_norm, v_sgu_w_in, v_sgu_v_gain, v_sgu_w_s, v_sgu_b_s, v_sgu_w_out, v_attn_w_qkv, v_attn_q_gain, v_attn_k_gain, v_attn_sinks, v_attn_w_o, v_rel_bias, v_ffn_w_up, v_ffn_conv_w, v_ffn_conv_b, v_ffn_w_down):
    w = dict(mix_norm=mix_norm, ffn_norm=ffn_norm, sgu_w_in=sgu_w_in, sgu_v_gain=sgu_v_gain, sgu_w_s=sgu_w_s,
             sgu_b_s=sgu_b_s, sgu_w_out=sgu_w_out, attn_w_qkv=attn_w_qkv, attn_q_gain=attn_q_gain,
             attn_k_gain=attn_k_gain, attn_sinks=attn_sinks, attn_w_o=attn_w_o, rel_bias=rel_bias, ffn_w_up=ffn_w_up,
             ffn_conv_w=ffn_conv_w, ffn_conv_b=ffn_conv_b, ffn_w_down=ffn_w_down)
    mom = dict(mix_norm=m_mix_norm, ffn_norm=m_ffn_norm, sgu_w_in=m_sgu_w_in, sgu_v_gain=m_sgu_v_gain,
               sgu_w_s=m_sgu_w_s, sgu_b_s=m_sgu_b_s, sgu_w_out=m_sgu_w_out, attn_w_qkv=m_attn_w_qkv,
               attn_q_gain=m_attn_q_gain, attn_k_gain=m_attn_k_gain, attn_sinks=m_attn_sinks, attn_w_o=m_attn_w_o,
               rel_bias=m_rel_bias, ffn_w_up=m_ffn_w_up, ffn_conv_w=m_ffn_conv_w, ffn_conv_b=m_ffn_conv_b,
               ffn_w_down=m_ffn_w_down)
    var = dict(mix_norm=v_mix_norm, ffn_norm=v_ffn_norm, sgu_w_in=v_sgu_w_in, sgu_v_gain=v_sgu_v_gain,
               sgu_w_s=v_sgu_w_s, sgu_b_s=v_sgu_b_s, sgu_w_out=v_sgu_w_out, attn_w_qkv=v_attn_w_qkv,
               attn_q_gain=v_attn_q_gain, attn_k_gain=v_attn_k_gain, attn_sinks=v_attn_sinks, attn_w_o=v_attn_w_o,
               rel_bias=v_rel_bias, ffn_w_up=v_ffn_w_up, ffn_conv_w=v_ffn_conv_w, ffn_conv_b=v_ffn_conv_b,
               ffn_w_down=v_ffn_w_down)
    chip = 2 * lax.axis_index("x") + lax.axis_index("y")
    core = lax.axis_index("c")

    where = jnp.stack([core, chip]).astype(jnp.int32)
    names = ["sgu_w_in", "sgu_w_out", "attn_w_qkv", "attn_w_o", "ffn_w_up0", "ffn_w_up1", "ffn_w_down0", "ffn_w_down1"]
    shards = [(sgu_w_in, 0), (sgu_w_out, 0), (attn_w_qkv, 0), (attn_w_o, 0), (ffn_w_up, 0), (ffn_w_up, 1),
              (ffn_w_down, 0), (ffn_w_down, 1)]
    T = {nm: _place_shard(s, l, where, BF16, "place_" + nm) for (s, l), nm in zip(shards, names)}
    for l in range(2):
        T["conv_w%d" % l] = _place_shard(ffn_conv_w, l, where, F32, "place_conv_w%d" % l)

    def gather(keys):
        return _gather_comm([T[k] for k in keys], [not k.startswith("conv") for k in keys])

    def gathered(keys, res):
        for k, a in zip(keys, res[0]):
            T[k] = a

    D = x.shape[2]
    first = ["sgu_w_in", "conv_w0", "conv_w1"]
    gathered(first, _run_comm(gather(first), "gather_first"))
    unshard_cols = lambda a: jnp.transpose(a, (1, 0, 2)).reshape(a.shape[1], -1)
    cw = [unshard_cols(T["conv_w0"]), unshard_cols(T["conv_w1"])]
    cb = ffn_conv_b
    flat = lambda k: T[k].reshape(-1, D)
    x2, target = x[0], loss_target[0]
    bucket = jnp.asarray(_rel_buckets_flat())
    wtril, wtrilT = _sgu_prep(sgu_w_s[0])
    bT = sgu_b_s[0].T
    bias = _rel_bias_expand(rel_bias, bucket).reshape(N_KV_HEADS, KV_GROUP * CHUNK, 2 * CHUNK)
    qg2 = jnp.tile(attn_q_gain, (1, 2))
    kg2 = jnp.tile(attn_k_gain, (1, 2))
    sinks = attn_sinks.reshape(N_HEADS)
    mix0, mix1 = mix_norm[0:1], mix_norm[1:2]
    fn0, fn1 = ffn_norm[0:1], ffn_norm[1:2]

    ks = ["sgu_w_out", "attn_w_qkv"]
    (hn0, z), r = _norm_matmul(x2, mix0, T["sgu_w_in"], "sgu_in", comm=gather(ks))
    gathered(ks, r)
    ks = ["ffn_w_up0"]
    (yp, h1, cdf), r = _sgu_fwd(z, x2, sgu_v_gain, wtril, bT, flat("sgu_w_out"), comm=gather(ks))
    gathered(ks, r)
    ks = ["ffn_w_down0", "attn_w_o"]
    (hf0, a0), r = _norm_matmul(h1, fn0, T["ffn_w_up0"], "ffn0_up", comm=gather(ks))
    gathered(ks, r)
    ks = ["ffn_w_up1"]
    (f0, h2), r = _ffn_fwd(a0, h1, cw[0], cb[0:1], flat("ffn_w_down0"), "ffn0_fwd", comm=gather(ks))
    gathered(ks, r)
    (hn1, qkv), _ = _norm_matmul(h2, mix1, T["attn_w_qkv"], "attn_qkv")
    ks = ["ffn_w_down1"]
    o, r = _attn_fwd(qkv, qg2, kg2, sinks, bias, comm=gather(ks))
    gathered(ks, r)
    h3, _ = _matmul_res(o, flat("attn_w_o"), h2, "attn_out")
    (hf1, a1), _ = _norm_matmul(h3, fn1, T["ffn_w_up1"], "ffn1_up")
    (f1, h4), _ = _ffn_fwd(a1, h3, cw[1], cb[1:2], flat("ffn_w_down1"), "ffn1_fwd")
    dh4, loss_local = _loss_head(h4, target)
    loss = lax.psum(loss_local, ("x", "y", "c"))

    G, RS, PQ, QD, halves, grads = {}, {}, {}, {}, {}, {}
    dest_of = {"sgu_w_in": ("sgu_w_in", 0), "sgu_w_out": ("sgu_w_out", 0), "attn_w_qkv": ("attn_w_qkv", 0),
               "attn_w_o": ("attn_w_o", 0), "ffn_w_up0": ("ffn_w_up", 0), "ffn_w_up1": ("ffn_w_up", 1),
               "ffn_w_down0": ("ffn_w_down", 0), "ffn_w_down1": ("ffn_w_down", 1)}

    def px(keys):
        return _pair_exchange_comm([G[k] for k in keys])

    def px_done(keys, res):
        for k, a in zip(keys, res[1]):
            RS[k] = a
            PQ[k] = _pair_add(G[k], a, where, "pair_add_" + k)

    def sc(keys):
        return _chip_scatter_comm([PQ[k][0] for k in keys], [PQ[k][1] for k in keys])

    def sc_done(keys, res):
        for k, a in zip(keys, res[0]):
            wk, layer = dest_of[k]
            halves[wk] = _sum_chips(a, where, halves.get(wk), layer, w[wk].shape, "sum_chips_" + k)

    nup = ffn_w_up.shape[2]
    ndown = ffn_w_down.shape[1]
    rows4 = lambda a: a.reshape(4, ndown, D)
    (dc1, dcw1, dcb1), _ = _ffn_bwd_dc(dh4, a1, cw[1], cb[1:2], flat("ffn_w_down1"), "ffn1_bwd_dc")
    gw, _ = _matmul_tn(f1, dh4, "ffn1_dw_down", ka=2 * ndown, nb=D, out_dtype=BF16)
    G["ffn_w_down1"] = rows4(gw)
    (dh3, dfn1, da1), r = _bwd_norm(dc1, T["ffn_w_up1"], h3, fn1, dh4, "ffn1_bwd_in", conv_w=cw[1],
                                    comm=px(["ffn_w_down1"]))
    px_done(["ffn_w_down1"], r)
    G["ffn_w_up1"], r = _matmul_tn(hf1, da1, "ffn1_dw_up", ka=D, nb=nup, out_dtype=BF16, comm=sc(["ffn_w_down1"]))
    sc_done(["ffn_w_down1"], r)
    G["attn_w_o"], r = _matmul_tn(o, dh3, "attn_dw_o", ka=QW // 4, nb=D, out_dtype=BF16, comm=px(["ffn_w_up1"]))
    px_done(["ffn_w_up1"], r)
    do, r = _matmul_nt(dh3, flat("attn_w_o"), "attn_bwd_out", comm=px(["attn_w_o"]))
    px_done(["attn_w_o"], r)
    ks = ["ffn_w_up1", "attn_w_o"]
    (dqkv, dbias, dqg, dkg, dsink), r = _attn_bwd(qkv, do, qg2, kg2, sinks, bias, comm=sc(ks))
    sc_done(ks, r)
    G["attn_w_qkv"], _ = _matmul_tn(hn1, dqkv, "attn_dw_qkv", ka=D, nb=dqkv.shape[1] // 4, out_dtype=BF16)
    (dh2, dmix1), r = _bwd_norm(dqkv, T["attn_w_qkv"], h2, mix1, dh3, "attn_bwd_in", comm=px(["attn_w_qkv"]))
    px_done(["attn_w_qkv"], r)
    (dc0, dcw0, dcb0), r = _ffn_bwd_dc(dh2, a0, cw[0], cb[0:1], flat("ffn_w_down0"), "ffn0_bwd_dc",
                                       comm=sc(["attn_w_qkv"]))
    sc_done(["attn_w_qkv"], r)
    gw, _ = _matmul_tn(f0, dh2, "ffn0_dw_down", ka=2 * ndown, nb=D, out_dtype=BF16)
    G["ffn_w_down0"] = rows4(gw)
    (dh1, dfn0, da0), r = _bwd_norm(dc0, T["ffn_w_up0"], h1, fn0, dh2, "ffn0_bwd_in", conv_w=cw[0],
                                    comm=px(["ffn_w_down0"]))
    px_done(["ffn_w_down0"], r)
    G["ffn_w_up0"], r = _matmul_tn(hf0, da0, "ffn0_dw_up", ka=D, nb=nup, out_dtype=BF16, comm=sc(["ffn_w_down0"]))
    sc_done(["ffn_w_down0"], r)
    G["sgu_w_out"], r = _matmul_tn(yp, dh1, "sgu_dw_out", ka=yp.shape[1] // 4, nb=D, out_dtype=BF16,
                                   comm=px(["ffn_w_up0"]))
    px_done(["ffn_w_up0"], r)
    both = [sc(["ffn_w_up0"]), px(["sgu_w_out"])]
    (dz, dws, dbT, dvg), r = _sgu_bwd(dh1, z, cdf, sgu_v_gain, wtril, wtrilT, bT, flat("sgu_w_out"),
                                      comm=_join(both))
    r = _split(both, r)
    sc_done(["ffn_w_up0"], r[0])
    px_done(["sgu_w_out"], r[1])
    G["sgu_w_in"], r = _matmul_tn(hn0, dz, "sgu_dw_in", ka=D, nb=dz.shape[1] // 4, out_dtype=BF16,
                                  comm=sc(["sgu_w_out"]))
    sc_done(["sgu_w_out"], r)
    (grad_x, dmix0), r = _bwd_norm(dz, T["sgu_w_in"], x2, mix0, dh1, "sgu_bwd_in", comm=px(["sgu_w_in"]))
    px_done(["sgu_w_in"], r)

    g = dict(mix_norm=jnp.concatenate([dmix0, dmix1], axis=0), ffn_norm=jnp.concatenate([dfn0, dfn1], axis=0),
             sgu_v_gain=dvg, sgu_w_s=dws, sgu_b_s=dbT.T, attn_q_gain=dqg[:, :HEAD_DIM], attn_k_gain=dkg[:, :HEAD_DIM],
             attn_sinks=dsink[:, :N_KV_HEADS].T.reshape(1, N_HEADS),
             rel_bias=_rel_bias_reduce(dbias.reshape(N_HEADS, CHUNK * 2 * CHUNK), bucket),
             ffn_conv_b=jnp.concatenate([dcb0, dcb1], axis=0))
    small_list = [g[k].reshape(w[k].shape) for k in SMALL] + [jnp.stack([dcw0, dcw1])]
    small_shapes = [a.shape for a in small_list]
    done = ["sgu_w_out", "attn_w_qkv", "attn_w_o", "ffn_w_up", "ffn_w_down"]
    both = [sc(["sgu_w_in"]), _half_exchange_comm([halves[k] for k in done])]
    red, r = _allreduce_small(_pack(small_list, 8 * SMALL_PIECE_ROWS), comm=_join(both))
    r = _split(both, r)
    sc_done(["sgu_w_in"], r[0])
    for k, a in zip(done, r[1][0]):
        grads[k] = a
    red = _unpack(red, small_shapes)
    for k, a in zip(SMALL, red[:-1]):
        grads[k] = a
    grads["ffn_conv_w"] = lax.dynamic_slice_in_dim(red[-1], chip * ffn_conv_w.shape[2], ffn_conv_w.shape[2], axis=2)

    delta, new_m, new_v = {}, {}, {}
    two = lambda a: a.reshape(-1, a.shape[-1])
    grads["sgu_w_in"] = _run_comm(_half_exchange_comm([halves["sgu_w_in"]]), "grad_last_halves")[0][0]
    for k in BIG:
        (d2, m2, v2, g2), _ = _adamw(two(grads[k]), two(w[k]), two(mom[k]), two(var[k]), "adamw_" + k)
        delta[k], new_m[k], new_v[k], grads[k] = (a.reshape(w[k].shape) for a in (d2, m2, v2, g2))
    sm = SMALL + ["ffn_conv_w"]
    sm_shapes = [w[k].shape for k in sm]
    n_el = sum(math.prod(s) for s in sm_shapes)
    rows = -(-n_el // (8 * SMALL_COLS)) * 8
    (d2, m2, v2, _), _ = _adamw(_pack([grads[k] for k in sm], rows), _pack([w[k] for k in sm], rows),
                                _pack([mom[k] for k in sm], rows), _pack([var[k] for k in sm], rows), "adamw_small")
    for dst, buf in ((delta, d2), (new_m, m2), (new_v, v2)):
        for k, a in zip(sm, _unpack(buf, sm_shapes)):
            dst[k] = a

    return (loss, grad_x[None], *[grads[k] for k in ORDER], *[delta[k] for k in ORDER],
            *[new_m[k] for k in ORDER], *[new_v[k] for k in ORDER])
```

```python
import functools
import math

import numpy as np
import jax
import jax.numpy as jnp
from jax import lax
from jax.experimental import pallas as pl
from jax.experimental.pallas import tpu as pltpu

F32 = jnp.float32
BF16 = jnp.bfloat16

EPS = 1e-6
CHUNK = 128
SGU_GROUPS = 16
HEAD_DIM = 64
N_HEADS = 16
N_KV_HEADS = 4
KV_GROUP = N_HEADS // N_KV_HEADS
REL_BUCKETS = 32
REL_MAX_DIST = 128
LANES = 128
HALO = 16

ADAM_LR = 0.001
ADAM_B1 = 0.9
ADAM_B2 = 0.999
ADAM_EPS = 1e-08
ADAM_WD = 0.01
ADAM_STEP = 10

VMEM_LIMIT_V7X = 56 * 1024 * 1024

_SQRT_HALF = math.sqrt(0.5)
_INV_SQRT_2PI = 1.0 / math.sqrt(2.0 * math.pi)


def _cp(sem):
    return pltpu.CompilerParams(dimension_semantics=sem, vmem_limit_bytes=VMEM_LIMIT_V7X)


def _resident(shape):
    nd = len(shape)
    return pl.BlockSpec(shape, lambda *_: (0,) * nd, pipeline_mode=pl.Buffered(1))


class _Comm:
    def __init__(self, srcs, zones, news, sems, start, finish):
        self.srcs, self.zones, self.news, self.sems = list(srcs), list(zones), list(news), list(sems)
        self.start, self.finish = start, finish


def _join(comms):
    comms = [c for c in comms if c is not None]
    if not comms:
        return None

    def part(seq, attr):
        out, k = [], 0
        for c in comms:
            n = len(getattr(c, attr))
            out.append(seq[k:k + n])
            k += n
        return out

    def run(which):
        def f(srcs, zones, news, sems):
            for c, a, b, d, e in zip(comms, part(srcs, "srcs"), part(zones, "zones"), part(news, "news"), part(sems, "sems")):
                getattr(c, which)(a, b, d, e)
        return f

    cat = lambda attr: [v for c in comms for v in getattr(c, attr)]
    return _Comm(cat("srcs"), cat("zones"), cat("news"), cat("sems"), run("start"), run("finish"))


def _split(comms, res):
    zones, news = res
    out, kz, kn = [], 0, 0
    for c in comms:
        out.append((zones[kz:kz + len(c.zones)], news[kn:kn + len(c.news)]))
        kz += len(c.zones)
        kn += len(c.news)
    return out


def _call(body, *, name, grid, in_specs, out_specs, out_shape, args, scratch_shapes=(), sem=None, comm=None):
    if comm is None:
        res = pl.pallas_call(body, name=name, grid=grid, in_specs=in_specs, out_specs=out_specs, out_shape=out_shape,
                             scratch_shapes=list(scratch_shapes), compiler_params=_cp(sem))(*args)
        return res, None
    single = not isinstance(out_shape, (list, tuple))
    out_specs_l = [out_specs] if single else list(out_specs)
    out_shape_l = [out_shape] if single else list(out_shape)
    n_in, n_out, n_scr = len(in_specs), len(out_shape_l), len(scratch_shapes)
    ns, nz, nn = len(comm.srcs), len(comm.zones), len(comm.news)

    def wrapped(*refs):
        k = n_in
        ins, srcs = refs[:k], refs[k:k + ns]
        k += ns + nz
        outs, zones, news = refs[k:k + n_out], refs[k + n_out:k + n_out + nz], refs[k + n_out + nz:k + n_out + nz + nn]
        k += n_out + nz + nn
        scr, sems = refs[k:k + n_scr], refs[k + n_scr:]
        if not grid:
            comm.start(srcs, zones, news, sems)
            body(*ins, *outs, *scr)
            comm.finish(srcs, zones, news, sems)
            return
        first = functools.reduce(jnp.logical_and, [pl.program_id(a) == 0 for a in range(len(grid))])
        last = functools.reduce(jnp.logical_and, [pl.program_id(a) == grid[a] - 1 for a in range(len(grid))])

        @pl.when(first)
        def _():
            comm.start(srcs, zones, news, sems)

        body(*ins, *outs, *scr)

        @pl.when(last)
        def _():
            comm.finish(srcs, zones, news, sems)

    res = pl.pallas_call(
        wrapped, name=name, grid=grid,
        in_specs=list(in_specs) + [ANY] * (ns + nz), out_specs=out_specs_l + [ANY] * (nz + nn),
        out_shape=out_shape_l + [jax.ShapeDtypeStruct(z.shape, z.dtype) for z in comm.zones] + comm.news,
        input_output_aliases={n_in + ns + i: n_out + i for i in range(nz)},
        scratch_shapes=list(scratch_shapes) + comm.sems,
        compiler_params=_cp(("arbitrary",) * len(grid)),
    )(*args, *comm.srcs, *comm.zones)
    main = res[0] if single else list(res[:n_out])
    return main, (list(res[n_out:n_out + nz]), list(res[n_out + nz:]))


def _run_comm(comm, name):
    ns, nz, nn = len(comm.srcs), len(comm.zones), len(comm.news)

    def body(*refs):
        srcs, zones, news, sems = refs[:ns], refs[ns + nz:ns + 2 * nz], refs[ns + 2 * nz:ns + 2 * nz + nn], refs[ns + 2 * nz + nn:]
        comm.start(srcs, zones, news, sems)
        comm.finish(srcs, zones, news, sems)

    res = pl.pallas_call(
        body, name=name, in_specs=[ANY] * (ns + nz), out_specs=[ANY] * (nz + nn),
        out_shape=[jax.ShapeDtypeStruct(z.shape, z.dtype) for z in comm.zones] + comm.news,
        input_output_aliases={ns + i: i for i in range(nz)}, scratch_shapes=comm.sems,
    )(*comm.srcs, *comm.zones)
    return list(res[:nz]), list(res[nz:])


def _dot(a, b):
    return jnp.dot(a, b, preferred_element_type=F32)


def _dot_nt(a, b):
    return lax.dot_general(a, b, (((1,), (1,)), ((), ())), preferred_element_type=F32)


def _dot_tn(a, b):
    return lax.dot_general(a, b, (((0,), (0,)), ((), ())), preferred_element_type=F32)


def _normal_cdf(x):
    return 0.5 * (1.0 + lax.erf(x * _SQRT_HALF))


def _gelu_and_grad(x, cdf):
    return x * cdf, cdf + x * jnp.exp(-0.5 * x * x) * _INV_SQRT_2PI


def _sigmoid(x):
    return 0.5 * jnp.tanh(0.5 * x) + 0.5


def _rms_bwd(dy, x, gain):
    r = lax.rsqrt(jnp.mean(x * x, axis=-1, keepdims=True) + EPS)
    xhat = x * r
    gdy = dy * gain
    dx = r * (gdy - xhat * jnp.mean(gdy * xhat, axis=-1, keepdims=True))
    return dx, jnp.sum(dy * xhat, axis=0, keepdims=True)


def _norm_matmul(x, gain, w4, name, comm=None, tm=512):
    S, D = x.shape
    nsh, _, ns = w4.shape
    tm = min(tm, S)

    def body(x_ref, g_ref, w_ref, hn_ref, o_ref):
        xf = x_ref[...]
        r = lax.rsqrt(jnp.mean(xf * xf, axis=-1, keepdims=True) + EPS)
        hn = (xf * r * g_ref[...]).astype(BF16)
        hn_ref[...] = hn
        for j in range(nsh):
            o_ref[:, j * ns:(j + 1) * ns] = _dot(hn, w_ref[j]).astype(BF16)

    return _call(
        body, name=name, grid=(S // tm,),
        in_specs=[pl.BlockSpec((tm, D), lambda i: (i, 0)), _resident((1, D)), _resident(w4.shape)],
        out_specs=[pl.BlockSpec((tm, D), lambda i: (i, 0)), pl.BlockSpec((tm, nsh * ns), lambda i: (i, 0))],
        out_shape=[jax.ShapeDtypeStruct((S, D), BF16), jax.ShapeDtypeStruct((S, nsh * ns), BF16)],
        sem=("parallel",), args=(x, gain, w4), comm=comm)


def _matmul_res(a, w, res, name, comm=None, tm=512):
    S, K = a.shape
    N = w.shape[1]
    tm = min(tm, S)

    def body(a_ref, w_ref, r_ref, o_ref):
        o_ref[...] = r_ref[...] + _dot(a_ref[...], w_ref[...])

    return _call(
        body, name=name, grid=(S // tm,),
        in_specs=[pl.BlockSpec((tm, K), lambda i: (i, 0)), _resident(w.shape), pl.BlockSpec((tm, N), lambda i: (i, 0))],
        out_specs=pl.BlockSpec((tm, N), lambda i: (i, 0)),
        out_shape=jax.ShapeDtypeStruct((S, N), F32),
        sem=("parallel",), args=(a, w, res), comm=comm)


def _matmul_nt(dh, w, name, comm=None, tm=512):
    S, N = dh.shape
    K = w.shape[0]
    tm = min(tm, S)

    def body(d_ref, w_ref, o_ref):
        o_ref[...] = _dot_nt(d_ref[...].astype(BF16), w_ref[...]).astype(BF16)

    return _call(
        body, name=name, grid=(S // tm,),
        in_specs=[pl.BlockSpec((tm, N), lambda i: (i, 0)), _resident(w.shape)],
        out_specs=pl.BlockSpec((tm, K), lambda i: (i, 0)),
        out_shape=jax.ShapeDtypeStruct((S, K), BF16),
        sem=("parallel",), args=(dh, w), comm=comm)


def _matmul_tn(a, b, name, *, ka, nb, out_dtype, comm=None, ts=1024):
    S, KA = a.shape
    NB = b.shape[1]
    ts = min(ts, S)
    J = max(KA // ka, NB // nb)
    a_map = (lambda j, s: (s, j)) if KA // ka > 1 else (lambda j, s: (s, 0))
    b_map = (lambda j, s: (s, j)) if NB // nb > 1 else (lambda j, s: (s, 0))
    last = S // ts - 1

    def body(a_ref, b_ref, o_ref, acc):
        s = pl.program_id(1)

        @pl.when(s == 0)
        def _():
            acc[...] = jnp.zeros_like(acc)

        acc[...] += _dot_tn(a_ref[...].astype(BF16), b_ref[...].astype(BF16))

        @pl.when(s == last)
        def _():
            o_ref[0] = acc[...].astype(out_dtype)

    return _call(
        body, name=name, grid=(J, S // ts),
        in_specs=[pl.BlockSpec((ts, ka), a_map), pl.BlockSpec((ts, nb), b_map)],
        out_specs=pl.BlockSpec((1, ka, nb), lambda j, s: (j, 0, 0)),
        out_shape=jax.ShapeDtypeStruct((J, ka, nb), out_dtype),
        scratch_shapes=[pltpu.VMEM((ka, nb), F32)],
        sem=("parallel", "arbitrary"), args=(a, b), comm=comm)


def _sgu_prep(w_s):
    G = w_s.shape[0]

    def body(w_ref, t_ref, tt_ref):
        tri = lax.broadcasted_iota(jnp.int32, (CHUNK, CHUNK), 0) >= lax.broadcasted_iota(jnp.int32, (CHUNK, CHUNK), 1)
        for g in range(G):
            t = jnp.where(tri, w_ref[g], 0.0)
            t_ref[g] = t.astype(BF16)
            tt_ref[g] = t.T.astype(BF16)

    return pl.pallas_call(
        body, name="sgu_prep",
        out_shape=[jax.ShapeDtypeStruct(w_s.shape, BF16), jax.ShapeDtypeStruct(w_s.shape, BF16)],
        compiler_params=_cp(None),
    )(w_s)


def _sgu_fwd(z, x, vg, wtril, bT, wout, comm=None, tm=256):
    S = z.shape[0]
    W = z.shape[1] // 2
    D = x.shape[1]
    tm = min(tm, S)

    def body(z_ref, x_ref, vg_ref, wt_ref, bT_ref, wo_ref, yp_ref, h_ref, cdf_ref):
        def chunk(c, carry):
            r0 = pl.multiple_of(c * CHUNK, CHUNK)
            zc = z_ref[pl.ds(r0, CHUNK), :].astype(F32)
            cdf = _normal_cdf(zc)
            cdf_ref[pl.ds(r0, CHUNK), :] = cdf.astype(BF16)
            u = zc[:, :W] * cdf[:, :W]
            v = zc[:, W:] * cdf[:, W:]
            rv = lax.rsqrt(jnp.mean(v * v, axis=-1, keepdims=True) + EPS)
            vn = (v * rv * vg_ref[...]).astype(BF16)
            for g in range(SGU_GROUPS):
                sl = slice(g * LANES, (g + 1) * LANES)
                s = _dot(wt_ref[g], vn[:, sl]) + bT_ref[:, g:g + 1]
                yp_ref[pl.ds(r0, CHUNK), sl] = (u[:, sl] * s).astype(BF16)
            return carry

        lax.fori_loop(0, tm // CHUNK, chunk, 0)
        h_ref[...] = x_ref[...] + _dot(yp_ref[...], wo_ref[...])

    return _call(
        body, name="sgu_fwd", grid=(S // tm,),
        in_specs=[pl.BlockSpec((tm, 2 * W), lambda i: (i, 0)), pl.BlockSpec((tm, D), lambda i: (i, 0)),
                  _resident((1, W)), _resident(wtril.shape), _resident(bT.shape), _resident(wout.shape)],
        out_specs=[pl.BlockSpec((tm, W), lambda i: (i, 0)), pl.BlockSpec((tm, D), lambda i: (i, 0)),
                   pl.BlockSpec((tm, 2 * W), lambda i: (i, 0))],
        out_shape=[jax.ShapeDtypeStruct((S, W), BF16), jax.ShapeDtypeStruct((S, D), F32),
                   jax.ShapeDtypeStruct((S, 2 * W), BF16)],
        sem=("parallel",), args=(z, x, vg, wtril, bT, wout), comm=comm)


def _sgu_bwd(dh, z, cdf, vg, wtril, wtrilT, bT, wout, comm=None, tm=256):
    S = z.shape[0]
    W = z.shape[1] // 2
    D = dh.shape[1]
    G = SGU_GROUPS
    tm = min(tm, S)
    last = S // tm - 1

    def body(dh_ref, z_ref, cdf_ref, vg_ref, wt_ref, wtT_ref, bT_ref, wo_ref,
             dz_ref, dws_ref, dbT_ref, dvg_ref, dyp_s, du_s, dvn_s, dsacc):
        i = pl.program_id(0)

        @pl.when(i == 0)
        def _():
            dws_ref[...] = jnp.zeros_like(dws_ref)
            dvg_ref[...] = jnp.zeros_like(dvg_ref)
            dsacc[...] = jnp.zeros_like(dsacc)

        dyp_s[...] = _dot_nt(dh_ref[...].astype(BF16), wo_ref[...])
        tri = lax.broadcasted_iota(jnp.int32, (CHUNK, CHUNK), 0) >= lax.broadcasted_iota(jnp.int32, (CHUNK, CHUNK), 1)

        def chunk(c, carry):
            r0 = pl.multiple_of(c * CHUNK, CHUNK)
            zc = z_ref[pl.ds(r0, CHUNK), :].astype(F32)
            cdf = cdf_ref[pl.ds(r0, CHUNK), :].astype(F32)
            u, gu = _gelu_and_grad(zc[:, :W], cdf[:, :W])
            v, gv = _gelu_and_grad(zc[:, W:], cdf[:, W:])
            rv = lax.rsqrt(jnp.mean(v * v, axis=-1, keepdims=True) + EPS)
            vhat = v * rv
            vgain = vg_ref[...]
            vn = (vhat * vgain).astype(BF16)
            dyp = dyp_s[pl.ds(r0, CHUNK), :]
            for g in range(G):
                sl = slice(g * LANES, (g + 1) * LANES)
                vng = vn[:, sl]
                s = _dot(wt_ref[g], vng) + bT_ref[:, g:g + 1]
                ds = dyp[:, sl] * u[:, sl]
                du_s[:, sl] = dyp[:, sl] * s
                dsb = ds.astype(BF16)
                dvn_s[:, sl] = _dot(wtT_ref[g], dsb)
                dws_ref[g] += jnp.where(tri, _dot_nt(dsb, vng), 0.0)
                dsacc[g] += ds
            dvn = dvn_s[...]
            dvg_ref[...] += jnp.sum(dvn * vhat, axis=0, keepdims=True)
            gdy = dvn * vgain
            dv = rv * (gdy - vhat * jnp.mean(gdy * vhat, axis=-1, keepdims=True))
            dz_ref[pl.ds(r0, CHUNK), :W] = (du_s[...] * gu).astype(BF16)
            dz_ref[pl.ds(r0, CHUNK), W:] = (dv * gv).astype(BF16)
            return carry

        lax.fori_loop(0, tm // CHUNK, chunk, 0)

        @pl.when(i == last)
        def _():
            for g in range(G):
                dbT_ref[:, g:g + 1] = jnp.sum(dsacc[g], axis=1, keepdims=True)

    return _call(
        body, name="sgu_bwd", grid=(S // tm,),
        in_specs=[pl.BlockSpec((tm, D), lambda i: (i, 0)), pl.BlockSpec((tm, 2 * W), lambda i: (i, 0)),
                  pl.BlockSpec((tm, 2 * W), lambda i: (i, 0)),
                  _resident((1, W)), _resident(wtril.shape), _resident(wtrilT.shape), _resident(bT.shape),
                  _resident(wout.shape)],
        out_specs=[pl.BlockSpec((tm, 2 * W), lambda i: (i, 0)),
                   pl.BlockSpec((G, CHUNK, CHUNK), lambda i: (0, 0, 0)),
                   pl.BlockSpec((CHUNK, G), lambda i: (0, 0)),
                   pl.BlockSpec((1, W), lambda i: (0, 0))],
        out_shape=[jax.ShapeDtypeStruct((S, 2 * W), BF16), jax.ShapeDtypeStruct((G, CHUNK, CHUNK), F32),
                   jax.ShapeDtypeStruct((CHUNK, G), F32), jax.ShapeDtypeStruct((1, W), F32)],
        scratch_shapes=[pltpu.VMEM((tm, W), F32), pltpu.VMEM((CHUNK, W), F32), pltpu.VMEM((CHUNK, W), F32),
                        pltpu.VMEM((G, CHUNK, CHUNK), F32)],
        sem=("arbitrary",), args=(dh, z, cdf, vg, wtril, wtrilT, bT, wout), comm=comm)


def _conv_taps(a32, r0, R, reuse=False):
    base = r0 + HALO
    if not reuse:
        return a32[base:base + R, :], a32[base - 1:base - 1 + R, :], a32[base - 2:base - 2 + R, :]
    X = a32[base - 8:base + R, :]
    return X[8:], pltpu.roll(X, 1, 0)[8:], pltpu.roll(X, 2, 0)[8:]


def _ffn_fwd(a, h_in, cw, cb, wdown, name, comm=None, tm=256, R=64):
    S, C = a.shape
    F = C // 2
    D = h_in.shape[1]
    tm = min(tm, S)

    def body(a_ref, halo_ref, h_ref, cw_ref, cb_ref, wd_ref, f_ref, ho_ref, a32):
        i = pl.program_id(0)
        a32[0:HALO, :] = jnp.where(i > 0, halo_ref[...].astype(F32), 0.0)
        a32[HALO:, :] = a_ref[...].astype(F32)

        for r0 in range(0, tm, R):
            a0, a1, a2 = _conv_taps(a32, r0, R)
            cpre = cw_ref[0:1, :] * a2 + cw_ref[1:2, :] * a1 + cw_ref[2:3, :] * a0 + cb_ref[...]
            g = cpre[:, :F]
            f_ref[r0:r0 + R, :] = (g * _sigmoid(g) * cpre[:, F:]).astype(BF16)
        ho_ref[...] = h_ref[...] + _dot(f_ref[...], wd_ref[...])

    hb = tm // HALO
    return _call(
        body, name=name, grid=(S // tm,),
        in_specs=[pl.BlockSpec((tm, C), lambda i: (i, 0)),
                  pl.BlockSpec((HALO, C), lambda i: (jnp.maximum(i * hb - 1, 0), 0)),
                  pl.BlockSpec((tm, D), lambda i: (i, 0)),
                  _resident((3, C)), _resident((1, C)), _resident(wdown.shape)],
        out_specs=[pl.BlockSpec((tm, F), lambda i: (i, 0)), pl.BlockSpec((tm, D), lambda i: (i, 0))],
        out_shape=[jax.ShapeDtypeStruct((S, F), BF16), jax.ShapeDtypeStruct((S, D), F32)],
        scratch_shapes=[pltpu.VMEM((HALO + tm, C), F32)],
        sem=("parallel",), args=(a, a, h_in, cw, cb, wdown), comm=comm)


def _ffn_bwd_dc(dh, a, cw, cb, wdown, name, comm=None, tm=256, R=64):
    S, C = a.shape
    F = C // 2
    D = dh.shape[1]
    tm = min(tm, S)

    def body(dh_ref, a_ref, halo_ref, cw_ref, cb_ref, wd_ref, dc_ref, dcw_ref, dcb_ref, a32, df_s, acc):
        i = pl.program_id(0)

        @pl.when(i == 0)
        def _():
            acc[...] = jnp.zeros_like(acc)

        a32[0:HALO, :] = jnp.where(i > 0, halo_ref[...].astype(F32), 0.0)
        a32[HALO:, :] = a_ref[...].astype(F32)
        df_s[...] = _dot_nt(dh_ref[...].astype(BF16), wd_ref[...])
        rows8 = lambda v: functools.reduce(jnp.add, [v[8 * k:8 * k + 8] for k in range(R // 8)])

        for r0 in range(0, tm, R):
            a0, a1, a2 = _conv_taps(a32, r0, R, reuse=True)
            cpre = cw_ref[0:1, :] * a2 + cw_ref[1:2, :] * a1 + cw_ref[2:3, :] * a0 + cb_ref[...]
            g = cpre[:, :F]
            val = cpre[:, F:]
            sg = _sigmoid(g)
            df = df_s[r0:r0 + R, :]
            dg = df * val * (sg * (1.0 + g * (1.0 - sg)))
            dval = df * (g * sg)
            dc = jnp.concatenate([dg, dval], axis=1)
            dc_ref[r0:r0 + R, :] = dc.astype(BF16)
            acc[0] += rows8(dc * a2)
            acc[1] += rows8(dc * a1)
            acc[2] += rows8(dc * a0)
            acc[3] += rows8(dc)

        @pl.when(i == S // tm - 1)
        def _():
            for k in range(3):
                dcw_ref[k:k + 1, :] = jnp.sum(acc[k], axis=0, keepdims=True)
            dcb_ref[...] = jnp.sum(acc[3], axis=0, keepdims=True)

    hb = tm // HALO
    return _call(
        body, name=name, grid=(S // tm,),
        in_specs=[pl.BlockSpec((tm, D), lambda i: (i, 0)),
                  pl.BlockSpec((tm, C), lambda i: (i, 0)),
                  pl.BlockSpec((HALO, C), lambda i: (jnp.maximum(i * hb - 1, 0), 0)),
                  _resident((3, C)), _resident((1, C)), _resident(wdown.shape)],
        out_specs=[pl.BlockSpec((tm, C), lambda i: (i, 0)),
                   pl.BlockSpec((3, C), lambda i: (0, 0)), pl.BlockSpec((1, C), lambda i: (0, 0))],
        out_shape=[jax.ShapeDtypeStruct((S, C), BF16), jax.ShapeDtypeStruct((3, C), F32),
                   jax.ShapeDtypeStruct((1, C), F32)],
        scratch_shapes=[pltpu.VMEM((HALO + tm, C), F32), pltpu.VMEM((tm, F), F32), pltpu.VMEM((4, 8, C), F32)],
        sem=("arbitrary",), args=(dh, a, a, cw, cb, wdown), comm=comm)


def _bwd_norm(dA, w4, h_in, gain, dh_out, name, conv_w=None, comm=None, R=64):
    S, N = dA.shape
    nsh, D, ns = w4.shape
    conv = conv_w is not None
    tm = min(256 if conv else 512, S)
    nt = S // tm

    def finish(src_ref, w_ref, h_ref, g_ref, dho_ref, dhi_ref, dg_ref):
        dhn = _dot_nt(src_ref[:, 0:ns], w_ref[0])
        for j in range(1, nsh):
            dhn += _dot_nt(src_ref[:, j * ns:(j + 1) * ns], w_ref[j])
        dx, dgain = _rms_bwd(dhn, h_ref[...], g_ref[...])
        dg_ref[...] += dgain
        dhi_ref[...] = dho_ref[...] + dx

    def body_plain(dA_ref, w_ref, h_ref, g_ref, dho_ref, dhi_ref, dg_ref):
        @pl.when(pl.program_id(0) == 0)
        def _():
            dg_ref[...] = jnp.zeros_like(dg_ref)

        finish(dA_ref, w_ref, h_ref, g_ref, dho_ref, dhi_ref, dg_ref)

    def body_conv(dc_ref, halo_ref, cw_ref, w_ref, h_ref, g_ref, dho_ref, dhi_ref, dg_ref, da_ref, dc32):
        i = pl.program_id(0)

        @pl.when(i == 0)
        def _():
            dg_ref[...] = jnp.zeros_like(dg_ref)

        dc32[0:tm, :] = dc_ref[...].astype(F32)
        dc32[tm:, :] = jnp.where(i < nt - 1, halo_ref[...].astype(F32), 0.0)

        for r0 in range(0, tm, R):
            d0 = dc32[r0:r0 + R, :]
            d1 = dc32[r0 + 1:r0 + 1 + R, :]
            d2 = dc32[r0 + 2:r0 + 2 + R, :]
            da = cw_ref[2:3, :] * d0 + cw_ref[1:2, :] * d1 + cw_ref[0:1, :] * d2
            da_ref[r0:r0 + R, :] = da.astype(BF16)
        finish(da_ref, w_ref, h_ref, g_ref, dho_ref, dhi_ref, dg_ref)

    row = lambda width: pl.BlockSpec((tm, width), lambda i: (i, 0))
    common_in = [_resident(w4.shape), row(D), _resident((1, D)), row(D)]
    common_out = [row(D), pl.BlockSpec((1, D), lambda i: (0, 0))]
    common_shape = [jax.ShapeDtypeStruct((S, D), F32), jax.ShapeDtypeStruct((1, D), F32)]
    if not conv:
        return _call(
            body_plain, name=name, grid=(nt,),
            in_specs=[row(N)] + common_in, out_specs=common_out, out_shape=common_shape,
            sem=("arbitrary",), args=(dA, w4, h_in, gain, dh_out), comm=comm)
    hb = tm // HALO
    nhb = S // HALO
    return _call(
        body_conv, name=name, grid=(nt,),
        in_specs=[row(N), pl.BlockSpec((HALO, N), lambda i: (jnp.minimum((i + 1) * hb, nhb - 1), 0)),
                  _resident((3, N))] + common_in,
        out_specs=common_out + [row(N)],
        out_shape=common_shape + [jax.ShapeDtypeStruct((S, N), BF16)],
        scratch_shapes=[pltpu.VMEM((tm + HALO, N), F32)],
        sem=("arbitrary",), args=(dA, dA, conv_w, w4, h_in, gain, dh_out), comm=comm)


def _rel_buckets_flat():
    q = np.arange(CHUNK)[:, None] + CHUNK
    k = np.arange(2 * CHUNK)[None, :]
    n = np.maximum(q - k, 0)
    max_exact = REL_BUCKETS // 2
    large = max_exact + (np.log(np.maximum(n, 1).astype(np.float32) / max_exact)
                         / math.log(REL_MAX_DIST / max_exact) * (REL_BUCKETS - max_exact)).astype(np.int32)
    large = np.minimum(large, REL_BUCKETS - 1)
    return np.where(n < max_exact, n, large).astype(np.int32).reshape(1, CHUNK * 2 * CHUNK)


def _split_bf16(x):
    hi = x.astype(BF16)
    return hi, (x - hi.astype(F32)).astype(BF16)


def _rel_bias_expand(rel_bias, bucket):
    B, H = rel_bias.shape
    n = bucket.shape[1]

    def body(rb_ref, bk_ref, o_ref):
        oh = (bk_ref[...] == lax.broadcasted_iota(jnp.int32, (B, n), 0)).astype(BF16)
        hi, lo = _split_bf16(rb_ref[...])
        o_ref[...] = _dot_tn(hi, oh) + _dot_tn(lo, oh)

    return pl.pallas_call(body, name="rel_bias_expand", out_shape=jax.ShapeDtypeStruct((H, n), F32),
                          compiler_params=_cp(None))(rel_bias, bucket)


def _rel_bias_reduce(dbias, bucket):
    H, n = dbias.shape
    B = REL_BUCKETS

    def body(db_ref, bk_ref, o_ref):
        oh = (bk_ref[...] == lax.broadcasted_iota(jnp.int32, (B, n), 0)).astype(BF16)
        hi, lo = _split_bf16(db_ref[...])
        o_ref[...] = _dot_nt(oh, hi) + _dot_nt(oh, lo)

    return pl.pallas_call(body, name="rel_bias_reduce", out_shape=jax.ShapeDtypeStruct((B, H), F32),
                          compiler_params=_cp(None))(dbias, bucket)


def _lo_mask(rows):
    return lax.broadcasted_iota(jnp.int32, (rows, LANES), 1) < HEAD_DIM


def _half_sums(y, lo):
    s_lo = jnp.sum(jnp.where(lo, y, 0.0), axis=-1, keepdims=True)
    s_hi = jnp.sum(jnp.where(lo, 0.0, y), axis=-1, keepdims=True)
    return jnp.where(lo, s_lo, s_hi)


def _half_rms(x, gain, lo):
    r = lax.rsqrt(_half_sums(x * x, lo) * (1.0 / HEAD_DIM) + EPS)
    xhat = x * r
    return xhat * gain, xhat, r


def _half_rms_bwd(dy, xhat, r, gain, lo):
    gdy = dy * gain
    dx = r * (gdy - xhat * (_half_sums(gdy * xhat, lo) * (1.0 / HEAD_DIM)))
    return dx, jnp.sum(dy * xhat, axis=0, keepdims=True)


def _dup_half(pair, e, lo):
    sw = pltpu.roll(pair, HEAD_DIM, 1)
    return jnp.where(lo, pair, sw) if e == 0 else jnp.where(lo, sw, pair)


def _band_valid(n):
    qi = lax.broadcasted_iota(jnp.int32, (KV_GROUP * CHUNK, 2 * CHUNK), 0) & (CHUNK - 1)
    ki = lax.broadcasted_iota(jnp.int32, (KV_GROUP * CHUNK, 2 * CHUNK), 1)
    dist = qi + CHUNK - ki
    return (dist >= 0) & (dist < CHUNK) & ((n > 0) | (ki >= CHUNK))


def _stack_heads(a, b, lo):
    return jnp.concatenate([jnp.where(lo, a, 0.0), jnp.where(lo, 0.0, a), jnp.where(lo, b, 0.0), jnp.where(lo, 0.0, b)],
                           axis=0)


def _unstack_heads(x4, lo):
    return (jnp.where(lo, x4[0:CHUNK], x4[CHUNK:2 * CHUNK]),
            jnp.where(lo, x4[2 * CHUNK:3 * CHUNK], x4[3 * CHUNK:]))


def _sink_col(sink_ref, hk):
    row = lax.broadcasted_iota(jnp.int32, (KV_GROUP * CHUNK, 1), 0)
    col = jnp.full((KV_GROUP * CHUNK, 1), sink_ref[KV_GROUP * hk + KV_GROUP - 1], F32)
    for r in range(KV_GROUP - 2, -1, -1):
        col = jnp.where(row < (r + 1) * CHUNK, sink_ref[KV_GROUP * hk + r], col)
    return col


def _softmax_sink(s, valid, sink):
    s = jnp.where(valid, s, -jnp.inf)
    m = jnp.maximum(jnp.max(s, axis=-1, keepdims=True), sink)
    p = jnp.exp(s - m)
    esink = jnp.exp(sink - m)
    inv = 1.0 / (jnp.sum(p, axis=-1, keepdims=True) + esink)
    return p * inv, esink * inv


QW = N_HEADS * HEAD_DIM
KVW = N_KV_HEADS * HEAD_DIM


def _attn_fwd(qkv, qg2, kg2, sinks, bias, comm=None):
    S = qkv.shape[0]
    nb = S // CHUNK

    def body(cur_ref, prev_ref, qg_ref, kg_ref, sink_ref, bias_ref, o_ref):
        n = pl.program_id(0)
        lo = _lo_mask(CHUNK)
        lo2 = _lo_mask(2 * CHUNK)
        valid = _band_valid(n)
        for j in range(N_KV_HEADS // 2):
            kc = slice(QW + j * LANES, QW + (j + 1) * LANES)
            vc = slice(QW + KVW + j * LANES, QW + KVW + (j + 1) * LANES)
            kpair = jnp.concatenate([prev_ref[:, j * LANES:(j + 1) * LANES], cur_ref[:, kc]], axis=0).astype(F32)
            vpair = jnp.concatenate([prev_ref[:, KVW + j * LANES:KVW + (j + 1) * LANES], cur_ref[:, vc]], axis=0).astype(F32)
            knpair, _, _ = _half_rms(kpair, kg_ref[...], lo2)
            for e in range(2):
                hk = 2 * j + e
                kdup = _dup_half(knpair, e, lo2).astype(BF16)
                vdup = _dup_half(vpair, e, lo2).astype(BF16)
                ca = slice(2 * hk * LANES, (2 * hk + 1) * LANES)
                cb = slice((2 * hk + 1) * LANES, (2 * hk + 2) * LANES)
                qna, _, _ = _half_rms(cur_ref[:, ca].astype(F32), qg_ref[...], lo)
                qnb, _, _ = _half_rms(cur_ref[:, cb].astype(F32), qg_ref[...], lo)
                qm4 = _stack_heads(qna, qnb, lo).astype(BF16)
                s = _dot_nt(qm4, kdup) * (HEAD_DIM ** -0.5) + bias_ref[hk]
                p, _ = _softmax_sink(s, valid, _sink_col(sink_ref, hk))
                oa, ob = _unstack_heads(_dot(p.astype(BF16), vdup), lo)
                o_ref[:, ca] = oa.astype(BF16)
                o_ref[:, cb] = ob.astype(BF16)

    return _call(
        body, name="attn_fwd", grid=(nb,),
        in_specs=[pl.BlockSpec((CHUNK, QW + 2 * KVW), lambda n: (n, 0)),
                  pl.BlockSpec((CHUNK, 2 * KVW), lambda n: (jnp.maximum(n - 1, 0), QW // (2 * KVW))),
                  _resident((1, LANES)), _resident((1, LANES)),
                  pl.BlockSpec(memory_space=pltpu.SMEM),
                  _resident(bias.shape)],
        out_specs=pl.BlockSpec((CHUNK, QW), lambda n: (n, 0)),
        out_shape=jax.ShapeDtypeStruct((S, QW), BF16),
        sem=("parallel",), args=(qkv, qkv, qg2, kg2, sinks, bias), comm=comm)


def _attn_bwd(qkv, do, qg2, kg2, sinks, bias, comm=None):
    S = qkv.shape[0]
    nb = S // CHUNK

    def body(cur_ref, prev_ref, do_ref, qg_ref, kg_ref, sink_ref, bias_ref,
             dqkv_ref, dbias_ref, dqg_ref, dkg_ref, dsink_ref, carry, band, dsacc, gacc):
        i = pl.program_id(0)
        n = nb - 1 - i
        lo = _lo_mask(CHUNK)
        lo2 = _lo_mask(2 * CHUNK)
        lane = lax.broadcasted_iota(jnp.int32, (KV_GROUP * CHUNK, LANES), 1)
        valid = _band_valid(n)

        @pl.when(i == 0)
        def _():
            dbias_ref[...] = jnp.zeros_like(dbias_ref)
            carry[...] = jnp.zeros_like(carry)
            dsacc[...] = jnp.zeros_like(dsacc)
            gacc[...] = jnp.zeros_like(gacc)

        qgain = qg_ref[...]
        kgain = kg_ref[...]
        for j in range(N_KV_HEADS // 2):
            kc = slice(QW + j * LANES, QW + (j + 1) * LANES)
            vc = slice(QW + KVW + j * LANES, QW + KVW + (j + 1) * LANES)
            kpair = jnp.concatenate([prev_ref[:, j * LANES:(j + 1) * LANES], cur_ref[:, kc]], axis=0).astype(F32)
            vpair = jnp.concatenate([prev_ref[:, KVW + j * LANES:KVW + (j + 1) * LANES], cur_ref[:, vc]], axis=0).astype(F32)
            knpair, khat, kr = _half_rms(kpair, kgain, lo2)
            dk_folds = []
            dv_folds = []
            for e in range(2):
                hk = 2 * j + e
                kdup = _dup_half(knpair, e, lo2).astype(BF16)
                vdup = _dup_half(vpair, e, lo2).astype(BF16)
                ca = slice(2 * hk * LANES, (2 * hk + 1) * LANES)
                cb = slice((2 * hk + 1) * LANES, (2 * hk + 2) * LANES)
                qna, qhata, qra = _half_rms(cur_ref[:, ca].astype(F32), qgain, lo)
                qnb, qhatb, qrb = _half_rms(cur_ref[:, cb].astype(F32), qgain, lo)
                qm4 = _stack_heads(qna, qnb, lo).astype(BF16)
                dom4 = _stack_heads(do_ref[:, ca].astype(F32), do_ref[:, cb].astype(F32), lo).astype(BF16)
                s = _dot_nt(qm4, kdup) * (HEAD_DIM ** -0.5) + bias_ref[hk]
                p, psink = _softmax_sink(s, valid, _sink_col(sink_ref, hk))
                dp = _dot_nt(dom4, vdup)
                delta = jnp.sum(p * dp, axis=-1, keepdims=True)
                ds = p * (dp - delta)
                dbias_ref[hk] += ds
                dsacc[...] += jnp.where(lane == hk, -(psink * delta), 0.0)
                dsr = (ds * (HEAD_DIM ** -0.5)).astype(BF16)
                dqna, dqnb = _unstack_heads(_dot(dsr, kdup), lo)
                dkd = _dot_tn(dsr, qm4)
                dvd = _dot_tn(p.astype(BF16), dom4)
                dqa, dqga = _half_rms_bwd(dqna, qhata, qra, qgain, lo)
                dqb, dqgb = _half_rms_bwd(dqnb, qhatb, qrb, qgain, lo)
                gacc[0:1, :] += dqga + dqgb
                dqkv_ref[:, ca] = dqa.astype(BF16)
                dqkv_ref[:, cb] = dqb.astype(BF16)
                dk_folds.append(dkd + pltpu.roll(dkd, HEAD_DIM, 1))
                dv_folds.append(dvd + pltpu.roll(dvd, HEAD_DIM, 1))
            dkn = jnp.where(lo2, dk_folds[0], dk_folds[1])
            dk, dkg = _half_rms_bwd(dkn, khat, kr, kgain, lo2)
            gacc[1:2, :] += dkg
            band[:, j * LANES:(j + 1) * LANES] = dk
            band[:, KVW + j * LANES:KVW + (j + 1) * LANES] = jnp.where(lo2, dv_folds[0], dv_folds[1])
        dqkv_ref[:, QW:] = (band[CHUNK:, :] + carry[...]).astype(BF16)
        carry[...] = band[0:CHUNK, :]

        @pl.when(i == nb - 1)
        def _():
            g = gacc[...]
            g = g + pltpu.roll(g, HEAD_DIM, 1)
            dqg_ref[...] = g[0:1, :]
            dkg_ref[...] = g[1:2, :]
            for r in range(KV_GROUP):
                dsink_ref[r:r + 1, :] = jnp.sum(dsacc[r * CHUNK:(r + 1) * CHUNK, :], axis=0, keepdims=True)

    vec = pl.BlockSpec((1, LANES), lambda i: (0, 0))
    return _call(
        body, name="attn_bwd", grid=(nb,),
        in_specs=[pl.BlockSpec((CHUNK, QW + 2 * KVW), lambda i: (nb - 1 - i, 0)),
                  pl.BlockSpec((CHUNK, 2 * KVW), lambda i: (jnp.maximum(nb - 2 - i, 0), QW // (2 * KVW))),
                  pl.BlockSpec((CHUNK, QW), lambda i: (nb - 1 - i, 0)),
                  _resident((1, LANES)), _resident((1, LANES)),
                  pl.BlockSpec(memory_space=pltpu.SMEM),
                  _resident(bias.shape)],
        out_specs=[pl.BlockSpec((CHUNK, QW + 2 * KVW), lambda i: (nb - 1 - i, 0)),
                   pl.BlockSpec(bias.shape, lambda i: (0, 0, 0)), vec, vec,
                   pl.BlockSpec((KV_GROUP, LANES), lambda i: (0, 0))],
        out_shape=[jax.ShapeDtypeStruct((S, QW + 2 * KVW), BF16),
                   jax.ShapeDtypeStruct(bias.shape, F32),
                   jax.ShapeDtypeStruct((1, LANES), F32), jax.ShapeDtypeStruct((1, LANES), F32),
                   jax.ShapeDtypeStruct((KV_GROUP, LANES), F32)],
        scratch_shapes=[pltpu.VMEM((CHUNK, 2 * KVW), F32), pltpu.VMEM((2 * CHUNK, 2 * KVW), F32),
                        pltpu.VMEM((KV_GROUP * CHUNK, LANES), F32), pltpu.VMEM((8, LANES), F32)],
        sem=("arbitrary",), args=(qkv, qkv, do, qg2, kg2, sinks, bias), comm=comm)


def _loss_head(y, target, tm=512):
    S, D = y.shape
    tm = min(tm, S)

    def body(y_ref, t_ref, dy_ref, l_ref):
        @pl.when(pl.program_id(0) == 0)
        def _():
            l_ref[...] = jnp.zeros_like(l_ref)

        e = y_ref[...] - t_ref[...]
        dy_ref[...] = e * (1.0 / D)
        rows = jnp.sum(e * e, axis=-1, keepdims=True) * (1.0 / D)
        l_ref[...] += 0.5 * jnp.sum(rows, axis=0, keepdims=True)

    dy, l = pl.pallas_call(
        body, name="loss_head", grid=(S // tm,),
        in_specs=[pl.BlockSpec((tm, D), lambda i: (i, 0)), pl.BlockSpec((tm, D), lambda i: (i, 0))],
        out_specs=[pl.BlockSpec((tm, D), lambda i: (i, 0)), pl.BlockSpec((1, 1), lambda i: (0, 0))],
        out_shape=[jax.ShapeDtypeStruct((S, D), F32), jax.ShapeDtypeStruct((1, 1), F32)],
        compiler_params=_cp(("arbitrary",)),
    )(y, target)
    return dy, l[0, 0]


def _row_tile(rows, cols, n_arrays):
    budget = VMEM_LIMIT_V7X // 4 // (2 * n_arrays * 4 * cols)
    best = 8
    for n in range(1, rows // 8 + 1):
        if rows % n == 0 and (rows // n) % 8 == 0 and rows // n <= budget:
            best = rows // n
            break
    return best


def _adamw(g, w, m, v, name, comm=None):
    R, C = g.shape
    tr = _row_tile(R, C, 8)

    def body(g_ref, w_ref, m_ref, v_ref, d_ref, mo_ref, vo_ref, go_ref):
        gg = g_ref[...]
        go_ref[...] = gg
        mn = ADAM_B1 * m_ref[...] + (1.0 - ADAM_B1) * gg
        vn = ADAM_B2 * v_ref[...] + (1.0 - ADAM_B2) * jnp.square(gg)
        m_hat = mn / (1.0 - ADAM_B1 ** ADAM_STEP)
        v_hat = vn / (1.0 - ADAM_B2 ** ADAM_STEP)
        d_ref[...] = -ADAM_LR * (m_hat / (jnp.sqrt(v_hat) + ADAM_EPS) + ADAM_WD * w_ref[...])
        mo_ref[...] = mn
        vo_ref[...] = vn

    spec = pl.BlockSpec((tr, C), lambda i: (i, 0))
    return _call(
        body, name=name, grid=(R // tr,), in_specs=[spec] * 4, out_specs=[spec] * 4,
        out_shape=[jax.ShapeDtypeStruct((R, C), F32)] * 4, sem=("parallel",), args=(g, w, m, v), comm=comm)


def _place_shard(shards, layer, where, dtype, name):
    _, R, C = shards.shape
    tr = _row_tile(R, C, 2) if R % 8 == 0 else R

    def body(s_ref, x_ref, o_ref):
        o_ref[...] = x_ref[...].astype(dtype)

    return pl.pallas_call(
        body, name=name,
        grid_spec=pltpu.PrefetchScalarGridSpec(
            num_scalar_prefetch=1, grid=(R // tr,),
            in_specs=[pl.BlockSpec((1, tr, C), lambda i, s_ref: (layer, i, 0))],
            out_specs=pl.BlockSpec((1, tr, C), lambda i, s_ref: (s_ref[1], i, 0))),
        out_shape=jax.ShapeDtypeStruct((4, R, C), dtype),
        compiler_params=_cp(("parallel",)),
    )(where, shards)


def _pair_add(g4, rsib, where, name):
    J, R, C = g4.shape
    Rh = R // 2
    tr = _row_tile(Rh, C, 4)
    g5 = g4.reshape(J, 2, Rh, C)

    def body(s_ref, g_ref, r_ref, p_ref, q_ref):
        val = (g_ref[...].astype(F32)[0] + r_ref[...].astype(F32)).astype(BF16)
        p_ref[...] = val

        @pl.when(pl.program_id(1) == s_ref[1])
        def _():
            q_ref[...] = val

    return pl.pallas_call(
        body, name=name,
        grid_spec=pltpu.PrefetchScalarGridSpec(
            num_scalar_prefetch=1, grid=(Rh // tr, J),
            in_specs=[pl.BlockSpec((1, 1, tr, C), lambda i, j, s_ref: (j, s_ref[0], i, 0)),
                      pl.BlockSpec((1, tr, C), lambda i, j, s_ref: (j, i, 0))],
            out_specs=[pl.BlockSpec((1, tr, C), lambda i, j, s_ref: (j, i, 0)),
                       pl.BlockSpec((1, tr, C), lambda i, j, s_ref: (s_ref[1], i, 0))]),
        out_shape=[jax.ShapeDtypeStruct((J, Rh, C), BF16)] * 2,
        compiler_params=_cp(("parallel", "arbitrary")),
    )(where, g5, rsib)


def _sum_chips(q, where, dest, layer, out_shape, name):
    J, Rh, C = q.shape
    tr = _row_tile(Rh, C, 3)
    nb = Rh // tr

    def body(s_ref, q_ref, *rest):
        qq = q_ref[...].astype(F32)
        rest[-1][0] = ((qq[0] + qq[1]) + qq[2]) + qq[3]

    have = dest is not None
    return pl.pallas_call(
        body, name=name,
        grid_spec=pltpu.PrefetchScalarGridSpec(
            num_scalar_prefetch=1, grid=(nb,),
            in_specs=[pl.BlockSpec((J, tr, C), lambda i, s_ref: (0, i, 0))] + ([ANY] if have else []),
            out_specs=pl.BlockSpec((1, tr, C), lambda i, s_ref: (layer, s_ref[0] * nb + i, 0))),
        out_shape=jax.ShapeDtypeStruct(out_shape, F32),
        input_output_aliases={2: 0} if have else {},
        compiler_params=_cp(("parallel",)),
    )(*((where, q, dest) if have else (where, q)))


MESH = pl.DeviceIdType.MESH
ANY = pl.BlockSpec(memory_space=pl.ANY)


def _place():
    x, y, c = lax.axis_index("x"), lax.axis_index("y"), lax.axis_index("c")
    others = [(1 - x, y), (x, 1 - y), (1 - x, 1 - y)]
    return x, y, c, 2 * x + y, others, [2 * ox + oy for ox, oy in others]


def _gather_comm(placed, split):
    n = len(placed)

    def rows(t, ref, half):
        if not split[t]:
            return ref
        rh = placed[t].shape[1] // 2
        return ref.at[pl.ds(half * rh, rh), :]

    def sends(outs, sems):
        send, recv = sems[0], sems[1]
        x, y, c, me, others, okey = _place()
        cps = []
        for t in range(n):
            mine = rows(t, outs[t].at[me], c)
            for j, (ox, oy) in enumerate(others):
                cps.append(pltpu.make_async_remote_copy(
                    src_ref=mine, dst_ref=mine,
                    send_sem=send.at[t, j], recv_sem=recv.at[t, j], device_id=(ox, oy, c), device_id_type=MESH))
        return cps

    def start(srcs, outs, news, sems):
        for cp in sends(outs, sems):
            cp.start()

    def finish(srcs, outs, news, sems):
        send, recv, fsend, frecv = sems
        x, y, c, me, others, okey = _place()
        sib = (x, y, 1 - c)
        first = sends(outs, sems)
        passed = []
        for t in range(n):
            for j in range(3):
                landed = rows(t, outs[t].at[okey[j]], c)
                pltpu.make_async_remote_copy(
                    src_ref=landed, dst_ref=landed, send_sem=send.at[t, j], recv_sem=recv.at[t, j],
                    device_id=sib, device_id_type=MESH).wait_recv()
                if split[t]:
                    cp = pltpu.make_async_remote_copy(
                        src_ref=landed, dst_ref=landed, send_sem=fsend.at[t, j], recv_sem=frecv.at[t, j],
                        device_id=sib, device_id_type=MESH)
                    cp.start()
                    passed.append(cp)
        for t in range(n):
            if split[t]:
                for j in range(3):
                    theirs = rows(t, outs[t].at[okey[j]], 1 - c)
                    pltpu.make_async_remote_copy(
                        src_ref=theirs, dst_ref=theirs, send_sem=fsend.at[t, j], recv_sem=frecv.at[t, j],
                        device_id=sib, device_id_type=MESH).wait_recv()
        for cp in first + passed:
            cp.wait_send()

    return _Comm([], placed, [], [pltpu.SemaphoreType.DMA((n, 3))] * 4, start, finish)


def _pair_exchange_comm(gs):
    n = len(gs)

    def copies(ins, outs, sems):
        send, recv = sems
        x, y, c, _, _, _ = _place()
        cps = []
        for t in range(n):
            rh = gs[t].shape[1] // 2
            cps.append(pltpu.make_async_remote_copy(
                src_ref=ins[t].at[:, pl.ds((1 - c) * rh, rh), :], dst_ref=outs[t],
                send_sem=send.at[t], recv_sem=recv.at[t], device_id=(x, y, 1 - c), device_id_type=MESH))
        return cps

    def start(ins, zones, outs, sems):
        for cp in copies(ins, outs, sems):
            cp.start()

    def finish(ins, zones, outs, sems):
        for cp in copies(ins, outs, sems):
            cp.wait()

    news = [jax.ShapeDtypeStruct((4, g.shape[1] // 2, g.shape[2]), g.dtype) for g in gs]
    return _Comm(gs, [], news, [pltpu.SemaphoreType.DMA((n,))] * 2, start, finish)


def _chip_scatter_comm(ps, qs):
    n = len(ps)

    def sends(ins, outs, sems):
        send, recv = sems
        x, y, c, me, others, okey = _place()
        return [pltpu.make_async_remote_copy(
            src_ref=ins[t].at[okey[j]], dst_ref=outs[t].at[me],
            send_sem=send.at[t, j], recv_sem=recv.at[t, j], device_id=(ox, oy, c), device_id_type=MESH)
            for t in range(n) for j, (ox, oy) in enumerate(others)]

    def start(ins, outs, news, sems):
        for cp in sends(ins, outs, sems):
            cp.start()

    def finish(ins, outs, news, sems):
        send, recv = sems
        x, y, c, me, others, okey = _place()
        for t in range(n):
            for j in range(3):
                slot = outs[t].at[okey[j]]
                pltpu.make_async_remote_copy(
                    src_ref=slot, dst_ref=slot, send_sem=send.at[t, j], recv_sem=recv.at[t, j],
                    device_id=(x, y, c), device_id_type=MESH).wait_recv()
        for cp in sends(ins, outs, sems):
            cp.wait_send()

    return _Comm(ps, qs, [], [pltpu.SemaphoreType.DMA((n, 3))] * 2, start, finish)


def _half_exchange_comm(arrs, layers=None):
    n = len(arrs)
    items = [(t, layer) for t in range(n) for layer in (range(arrs[t].shape[0]) if layers is None else layers[t])]

    def sends(outs, sems):
        send, recv = sems
        x, y, c, _, _, _ = _place()
        cps = []
        for k, (t, layer) in enumerate(items):
            rh = arrs[t].shape[1] // 2
            mine = outs[t].at[layer, pl.ds(c * rh, rh), :]
            cps.append(pltpu.make_async_remote_copy(
                src_ref=mine, dst_ref=mine, send_sem=send.at[k], recv_sem=recv.at[k],
                device_id=(x, y, 1 - c), device_id_type=MESH))
        return cps

    def start(srcs, outs, news, sems):
        for cp in sends(outs, sems):
            cp.start()

    def finish(srcs, outs, news, sems):
        send, recv = sems
        x, y, c, _, _, _ = _place()
        for k, (t, layer) in enumerate(items):
            rh = arrs[t].shape[1] // 2
            theirs = outs[t].at[layer, pl.ds((1 - c) * rh, rh), :]
            pltpu.make_async_remote_copy(
                src_ref=theirs, dst_ref=theirs, send_sem=send.at[k], recv_sem=recv.at[k],
                device_id=(x, y, 1 - c), device_id_type=MESH).wait_recv()
        for cp in sends(outs, sems):
            cp.wait_send()

    return _Comm([], arrs, [], [pltpu.SemaphoreType.DMA((len(items),))] * 2, start, finish)


SMALL_COLS = 1024
SMALL_PIECE_ROWS = 48


def _allreduce_small(buf, comm=None):
    pr = SMALL_PIECE_ROWS
    flips = [(d >> 2 & 1, d >> 1 & 1, d & 1) for d in range(1, 8)]

    def body(x_ref, o_ref, rbuf, send1, recv1, send2, recv2):
        x, y, c = lax.axis_index("x"), lax.axis_index("y"), lax.axis_index("c")
        me = 4 * x + 2 * y + c
        peers = [(x ^ fx, y ^ fy, c ^ fc) for fx, fy, fc in flips]
        pid = [4 * px + 2 * py + pc for px, py, pc in peers]

        def piece(ref, p):
            return ref.at[pl.ds(pl.multiple_of(p * pr, 8), pr), :]

        cps = []
        for d in range(7):
            cp = pltpu.make_async_remote_copy(
                src_ref=piece(x_ref, pid[d]), dst_ref=rbuf.at[d + 1],
                send_sem=send1.at[d], recv_sem=recv1.at[d], device_id=peers[d], device_id_type=MESH)
            cp.start()
            cps.append(cp)
        acc = piece(x_ref, me)[...]
        for d in range(7):
            cps[d].wait_recv()
            acc = acc + rbuf[d + 1]
        piece(o_ref, me)[...] = acc
        out = []
        for d in range(7):
            cp = pltpu.make_async_remote_copy(
                src_ref=piece(o_ref, me), dst_ref=piece(o_ref, me),
                send_sem=send2.at[d], recv_sem=recv2.at[d], device_id=peers[d], device_id_type=MESH)
            cp.start()
            out.append(cp)
        for d in range(7):
            pltpu.make_async_remote_copy(
                src_ref=piece(o_ref, pid[d]), dst_ref=piece(o_ref, pid[d]),
                send_sem=send2.at[d], recv_sem=recv2.at[d], device_id=peers[d], device_id_type=MESH).wait_recv()
        for cp in cps + out:
            cp.wait_send()

    vm = pl.BlockSpec(memory_space=pltpu.VMEM)
    return _call(
        body, name="small_allreduce", grid=(), in_specs=[vm], out_specs=vm,
        out_shape=jax.ShapeDtypeStruct(buf.shape, F32),
        scratch_shapes=[pltpu.VMEM((8, pr, SMALL_COLS), F32)] + [pltpu.SemaphoreType.DMA((7,))] * 4,
        args=(buf,), comm=comm)


def _rows_of(shape):
    return -(-math.prod(shape) // (8 * SMALL_COLS)) * 8


def _pack(arrays, rows):
    parts = []
    for a in arrays:
        r = _rows_of(a.shape)
        parts.append(jnp.pad(a.reshape(-1), (0, r * SMALL_COLS - a.size)).reshape(r, SMALL_COLS))
    used = sum(p.shape[0] for p in parts)
    if rows > used:
        parts.append(jnp.zeros((rows - used, SMALL_COLS), F32))
    return jnp.concatenate(parts, axis=0)


def _unpack(buf, shapes):
    out, off = [], 0
    for s in shapes:
        r = _rows_of(s)
        out.append(buf[off:off + r].reshape(-1)[:math.prod(s)].reshape(s))
        off += r
    return out


BIG = ["sgu_w_in", "sgu_w_out", "attn_w_qkv", "attn_w_o", "ffn_w_up", "ffn_w_down"]
SMALL = ["mix_norm", "ffn_norm", "sgu_v_gain", "sgu_w_s", "sgu_b_s", "attn_q_gain", "attn_k_gain", "attn_sinks",
         "rel_bias", "ffn_conv_b"]
ORDER = ["mix_norm", "ffn_norm", "sgu_w_in", "sgu_v_gain", "sgu_w_s", "sgu_b_s", "sgu_w_out", "attn_w_qkv",
         "attn_q_gain", "attn_k_gain", "attn_sinks", "attn_w_o", "rel_bias", "ffn_w_up", "ffn_conv_w", "ffn_conv_b",
         "ffn_w_down"]


def kernel(x, mix_norm, ffn_norm, sgu_w_in, sgu_v_gain, sgu_w_s, sgu_b_s, sgu_w_out, attn_w_qkv, attn_q_gain, attn_k_gain, attn_sinks, attn_w_o, rel_bias, ffn_w_up, ffn_conv_w, ffn_conv_b, ffn_w_down, loss_target, m_mix_norm, m_ffn_norm, m_sgu_w_in, m_sgu_v_gain, m_sgu_w_s, m_sgu_b_s, m_sgu_w_out, m_attn_w_qkv, m_attn_q_gain, m_attn_k_gain, m_attn_sinks, m_attn_w_o, m_rel_bias, m_ffn_w_up, m_ffn_conv_w, m_ffn_conv_b, m_ffn_w_down, v_mix_norm, v_ffn_norm, v_sgu_w_in, v_sgu_v_gain, v_sgu_w_s, v_sgu_b_s, v_sgu_w_out, v_attn_w_qkv, v_attn_q_gain, v_attn_k_gain, v_attn_sinks, v_attn_w_o, v_rel_bias, v_ffn_w_up, v_ffn_conv_w, v_ffn_conv_b, v_ffn_w_down):
    w = dict(mix_norm=mix_norm, ffn_norm=ffn_norm, sgu_w_in=sgu_w_in, sgu_v_gain=sgu_v_gain, sgu_w_s=sgu_w_s,
             sgu_b_s=sgu_b_s, sgu_w_out=sgu_w_out, attn_w_qkv=attn_w_qkv, attn_q_gain=attn_q_gain,
             attn_k_gain=attn_k_gain, attn_sinks=attn_sinks, attn_w_o=attn_w_o, rel_bias=rel_bias, ffn_w_up=ffn_w_up,
             ffn_conv_w=ffn_conv_w, ffn_conv_b=ffn_conv_b, ffn_w_down=ffn_w_down)
    mom = dict(mix_norm=m_mix_norm, ffn_norm=m_ffn_norm, sgu_w_in=m_sgu_w_in, sgu_v_gain=m_sgu_v_gain,
               sgu_w_s=m_sgu_w_s, sgu_b_s=m_sgu_b_s, sgu_w_out=m_sgu_w_out, attn_w_qkv=m_attn_w_qkv,
               attn_q_gain=m_attn_q_gain, attn_k_gain=m_attn_k_gain, attn_sinks=m_attn_sinks, attn_w_o=m_attn_w_o,
               rel_bias=m_rel_bias, ffn_w_up=m_ffn_w_up, ffn_conv_w=m_ffn_conv_w, ffn_conv_b=m_ffn_conv_b,
               ffn_w_down=m_ffn_w_down)
    var = dict(mix_norm=v_mix_norm, ffn_norm=v_ffn_norm, sgu_w_in=v_sgu_w_in, sgu_v_gain=v_sgu_v_gain,
               sgu_w_s=v_sgu_w_s, sgu_b_s=v_sgu_b_s, sgu_w_out=v_sgu_w_out, attn_w_qkv=v_attn_w_qkv,
               attn_q_gain=v_attn_q_gain, attn_k_gain=v_attn_k_gain, attn_sinks=v_attn_sinks, attn_w_o=v_attn_w_o,
               rel_bias=v_rel_bias, ffn_w_up=v_ffn_w_up, ffn_conv_w=v_ffn_conv_w, ffn_conv_b=v_ffn_conv_b,
               ffn_w_down=v_ffn_w_down)
    chip = 2 * lax.axis_index("x") + lax.axis_index("y")
    core = lax.axis_index("c")

    where = jnp.stack([core, chip]).astype(jnp.int32)
    names = ["sgu_w_in", "sgu_w_out", "attn_w_qkv", "attn_w_o", "ffn_w_up0", "ffn_w_up1", "ffn_w_down0", "ffn_w_down1"]
    shards = [(sgu_w_in, 0), (sgu_w_out, 0), (attn_w_qkv, 0), (attn_w_o, 0), (ffn_w_up, 0), (ffn_w_up, 1),
              (ffn_w_down, 0), (ffn_w_down, 1)]
    T = {nm: _place_shard(s, l, where, BF16, "place_" + nm) for (s, l), nm in zip(shards, names)}
    for l in range(2):
        T["conv_w%d" % l] = _place_shard(ffn_conv_w, l, where, F32, "place_conv_w%d" % l)

    def gather(keys):
        return _gather_comm([T[k] for k in keys], [not k.startswith("conv") for k in keys])

    def gathered(keys, res):
        for k, a in zip(keys, res[0]):
            T[k] = a

    D = x.shape[2]
    first = ["sgu_w_in", "conv_w0", "conv_w1"]
    gathered(first, _run_comm(gather(first), "gather_first"))
    unshard_cols = lambda a: jnp.transpose(a, (1, 0, 2)).reshape(a.shape[1], -1)
    cw = [unshard_cols(T["conv_w0"]), unshard_cols(T["conv_w1"])]
    cb = ffn_conv_b
    flat = lambda k: T[k].reshape(-1, D)
    x2, target = x[0], loss_target[0]
    bucket = jnp.asarray(_rel_buckets_flat())
    wtril, wtrilT = _sgu_prep(sgu_w_s[0])
    bT = sgu_b_s[0].T
    bias = _rel_bias_expand(rel_bias, bucket).reshape(N_KV_HEADS, KV_GROUP * CHUNK, 2 * CHUNK)
    qg2 = jnp.tile(attn_q_gain, (1, 2))
    kg2 = jnp.tile(attn_k_gain, (1, 2))
    sinks = attn_sinks.reshape(N_HEADS)
    mix0, mix1 = mix_norm[0:1], mix_norm[1:2]
    fn0, fn1 = ffn_norm[0:1], ffn_norm[1:2]

    ks = ["sgu_w_out", "attn_w_qkv"]
    (hn0, z), r = _norm_matmul(x2, mix0, T["sgu_w_in"], "sgu_in", comm=gather(ks))
    gathered(ks, r)
    ks = ["ffn_w_up0"]
    (yp, h1, cdf), r = _sgu_fwd(z, x2, sgu_v_gain, wtril, bT, flat("sgu_w_out"), comm=gather(ks))
    gathered(ks, r)
    ks = ["ffn_w_down0", "attn_w_o"]
    (hf0, a0), r = _norm_matmul(h1, fn0, T["ffn_w_up0"], "ffn0_up", comm=gather(ks))
    gathered(ks, r)
    ks = ["ffn_w_up1"]
    (f0, h2), r = _ffn_fwd(a0, h1, cw[0], cb[0:1], flat("ffn_w_down0"), "ffn0_fwd", comm=gather(ks))
    gathered(ks, r)
    (hn1, qkv), _ = _norm_matmul(h2, mix1, T["attn_w_qkv"], "attn_qkv")
    ks = ["ffn_w_down1"]
    o, r = _attn_fwd(qkv, qg2, kg2, sinks, bias, comm=gather(ks))
    gathered(ks, r)
    h3, _ = _matmul_res(o, flat("attn_w_o"), h2, "attn_out")
    (hf1, a1), _ = _norm_matmul(h3, fn1, T["ffn_w_up1"], "ffn1_up")
    (f1, h4), _ = _ffn_fwd(a1, h3, cw[1], cb[1:2], flat("ffn_w_down1"), "ffn1_fwd")
    dh4, loss_local = _loss_head(h4, target)
    loss = lax.psum(loss_local, ("x", "y", "c"))

    G, RS, PQ, QD, halves, grads = {}, {}, {}, {}, {}, {}
    dest_of = {"sgu_w_in": ("sgu_w_in", 0), "sgu_w_out": ("sgu_w_out", 0), "attn_w_qkv": ("attn_w_qkv", 0),
               "attn_w_o": ("attn_w_o", 0), "ffn_w_up0": ("ffn_w_up", 0), "ffn_w_up1": ("ffn_w_up", 1),
               "ffn_w_down0": ("ffn_w_down", 0), "ffn_w_down1": ("ffn_w_down", 1)}

    def px(keys):
        return _pair_exchange_comm([G[k] for k in keys])

    def px_done(keys, res):
        for k, a in zip(keys, res[1]):
            RS[k] = a
            PQ[k] = _pair_add(G[k], a, where, "pair_add_" + k)

    def sc(keys):
        return _chip_scatter_comm([PQ[k][0] for k in keys], [PQ[k][1] for k in keys])

    def sc_done(keys, res):
        for k, a in zip(keys, res[0]):
            wk, layer = dest_of[k]
            halves[wk] = _sum_chips(a, where, halves.get(wk), layer, w[wk].shape, "sum_chips_" + k)

    nup = ffn_w_up.shape[2]
    ndown = ffn_w_down.shape[1]
    rows4 = lambda a: a.reshape(4, ndown, D)
    (dc1, dcw1, dcb1), _ = _ffn_bwd_dc(dh4, a1, cw[1], cb[1:2], flat("ffn_w_down1"), "ffn1_bwd_dc")
    gw, _ = _matmul_tn(f1, dh4, "ffn1_dw_down", ka=2 * ndown, nb=D, out_dtype=BF16)
    G["ffn_w_down1"] = rows4(gw)
    (dh3, dfn1, da1), r = _bwd_norm(dc1, T["ffn_w_up1"], h3, fn1, dh4, "ffn1_bwd_in", conv_w=cw[1],
                                    comm=px(["ffn_w_down1"]))
    px_done(["ffn_w_down1"], r)
    G["ffn_w_up1"], r = _matmul_tn(hf1, da1, "ffn1_dw_up", ka=D, nb=nup, out_dtype=BF16, comm=sc(["ffn_w_down1"]))
    sc_done(["ffn_w_down1"], r)
    G["attn_w_o"], r = _matmul_tn(o, dh3, "attn_dw_o", ka=QW // 4, nb=D, out_dtype=BF16, comm=px(["ffn_w_up1"]))
    px_done(["ffn_w_up1"], r)
    do, r = _matmul_nt(dh3, flat("attn_w_o"), "attn_bwd_out", comm=px(["attn_w_o"]))
    px_done(["attn_w_o"], r)
    ks = ["ffn_w_up1", "attn_w_o"]
    (dqkv, dbias, dqg, dkg, dsink), r = _attn_bwd(qkv, do, qg2, kg2, sinks, bias, comm=sc(ks))
    sc_done(ks, r)
    G["attn_w_qkv"], _ = _matmul_tn(hn1, dqkv, "attn_dw_qkv", ka=D, nb=dqkv.shape[1] // 4, out_dtype=BF16)
    (dh2, dmix1), r = _bwd_norm(dqkv, T["attn_w_qkv"], h2, mix1, dh3, "attn_bwd_in", comm=px(["attn_w_qkv"]))
    px_done(["attn_w_qkv"], r)
    (dc0, dcw0, dcb0), r = _ffn_bwd_dc(dh2, a0, cw[0], cb[0:1], flat("ffn_w_down0"), "ffn0_bwd_dc",
                                       comm=sc(["attn_w_qkv"]))
    sc_done(["attn_w_qkv"], r)
    gw, _ = _matmul_tn(f0, dh2, "ffn0_dw_down", ka=2 * ndown, nb=D, out_dtype=BF16)
    G["ffn_w_down0"] = rows4(gw)
    (dh1, dfn0, da0), r = _bwd_norm(dc0, T["ffn_w_up0"], h1, fn0, dh2, "ffn0_bwd_in", conv_w=cw[0],
                                    comm=px(["ffn_w_down0"]))
    px_done(["ffn_w_down0"], r)
    G["ffn_w_up0"], r = _matmul_tn(hf0, da0, "ffn0_dw_up", ka=D, nb=nup, out_dtype=BF16, comm=sc(["ffn_w_down0"]))
    sc_done(["ffn_w_down0"], r)
    G["sgu_w_out"], r = _matmul_tn(yp, dh1, "sgu_dw_out", ka=yp.shape[1] // 4, nb=D, out_dtype=BF16,
                                   comm=px(["ffn_w_up0"]))
    px_done(["ffn_w_up0"], r)
    both = [sc(["ffn_w_up0"]), px(["sgu_w_out"])]
    (dz, dws, dbT, dvg), r = _sgu_bwd(dh1, z, cdf, sgu_v_gain, wtril, wtrilT, bT, flat("sgu_w_out"),
                                      comm=_join(both))
    r = _split(both, r)
    sc_done(["ffn_w_up0"], r[0])
    px_done(["sgu_w_out"], r[1])
    done = ["attn_w_qkv", "attn_w_o", "ffn_w_up", "ffn_w_down"]
    both = [sc(["sgu_w_out"]), _half_exchange_comm([halves[k] for k in done])]
    G["sgu_w_in"], r = _matmul_tn(hn0, dz, "sgu_dw_in", ka=D, nb=dz.shape[1] // 4, out_dtype=BF16, comm=_join(both))
    r = _split(both, r)
    sc_done(["sgu_w_out"], r[0])
    for k, a in zip(done, r[1][0]):
        grads[k] = a
    px_done(["sgu_w_in"], _run_comm(px(["sgu_w_in"]), "grad_last_pair_exchange"))
    (grad_x, dmix0), r = _bwd_norm(dz, T["sgu_w_in"], x2, mix0, dh1, "sgu_bwd_in", comm=sc(["sgu_w_in"]))
    sc_done(["sgu_w_in"], r)

    g = dict(mix_norm=jnp.concatenate([dmix0, dmix1], axis=0), ffn_norm=jnp.concatenate([dfn0, dfn1], axis=0),
             sgu_v_gain=dvg, sgu_w_s=dws, sgu_b_s=dbT.T, attn_q_gain=dqg[:, :HEAD_DIM], attn_k_gain=dkg[:, :HEAD_DIM],
             attn_sinks=dsink[:, :N_KV_HEADS].T.reshape(1, N_HEADS),
             rel_bias=_rel_bias_reduce(dbias.reshape(N_HEADS, CHUNK * 2 * CHUNK), bucket),
             ffn_conv_b=jnp.concatenate([dcb0, dcb1], axis=0))
    small_list = [g[k].reshape(w[k].shape) for k in SMALL] + [jnp.stack([dcw0, dcw1])]
    small_shapes = [a.shape for a in small_list]
    done = ["sgu_w_in", "sgu_w_out"]
    red, r = _allreduce_small(_pack(small_list, 8 * SMALL_PIECE_ROWS),
                              comm=_half_exchange_comm([halves[k] for k in done]))
    for k, a in zip(done, r[0]):
        grads[k] = a
    red = _unpack(red, small_shapes)
    for k, a in zip(SMALL, red[:-1]):
        grads[k] = a
    grads["ffn_conv_w"] = lax.dynamic_slice_in_dim(red[-1], chip * ffn_conv_w.shape[2], ffn_conv_w.shape[2], axis=2)

    delta, new_m, new_v = {}, {}, {}
    two = lambda a: a.reshape(-1, a.shape[-1])
    for k in BIG:
        (d2, m2, v2, g2), _ = _adamw(two(grads[k]), two(w[k]), two(mom[k]), two(var[k]), "adamw_" + k)
        delta[k], new_m[k], new_v[k], grads[k] = (a.reshape(w[k].shape) for a in (d2, m2, v2, g2))
    sm = SMALL + ["ffn_conv_w"]
    sm_shapes = [w[k].shape for k in sm]
    rows = sum(_rows_of(s) for s in sm_shapes)
    (d2, m2, v2, _), _ = _adamw(_pack([grads[k] for k in sm], rows), _pack([w[k] for k in sm], rows),
                                _pack([mom[k] for k in sm], rows), _pack([var[k] for k in sm], rows), "adamw_small")
    for dst, buf in ((delta, d2), (new_m, m2), (new_v, v2)):
        for k, a in zip(sm, _unpack(buf, sm_shapes)):
            dst[k] = a

    return (loss, grad_x[None], *[grads[k] for k in ORDER], *[delta[k] for k in ORDER],
            *[new_m[k] for k in ORDER], *[new_v[k] for k in ORDER])
```

```python
import functools
import math

import numpy as np
import jax
import jax.numpy as jnp
from jax import lax
from jax.experimental import pallas as pl
from jax.experimental.pallas import tpu as pltpu

F32 = jnp.float32
BF16 = jnp.bfloat16

EPS = 1e-6
CHUNK = 128
SGU_GROUPS = 16
HEAD_DIM = 64
N_HEADS = 16
N_KV_HEADS = 4
KV_GROUP = N_HEADS // N_KV_HEADS
REL_BUCKETS = 32
REL_MAX_DIST = 128
LANES = 128
HALO = 16

ADAM_LR = 0.001
ADAM_B1 = 0.9
ADAM_B2 = 0.999
ADAM_EPS = 1e-08
ADAM_WD = 0.01
ADAM_STEP = 10

VMEM_LIMIT_V7X = 56 * 1024 * 1024

_SQRT_HALF = math.sqrt(0.5)
_INV_SQRT_2PI = 1.0 / math.sqrt(2.0 * math.pi)


def _cp(sem):
    return pltpu.CompilerParams(dimension_semantics=sem, vmem_limit_bytes=VMEM_LIMIT_V7X)


def _resident(shape):
    nd = len(shape)
    return pl.BlockSpec(shape, lambda *_: (0,) * nd, pipeline_mode=pl.Buffered(1))


class _Comm:
    def __init__(self, srcs, zones, news, sems, start, finish):
        self.srcs, self.zones, self.news, self.sems = list(srcs), list(zones), list(news), list(sems)
        self.start, self.finish = start, finish


def _join(comms):
    comms = [c for c in comms if c is not None]
    if not comms:
        return None

    def part(seq, attr):
        out, k = [], 0
        for c in comms:
            n = len(getattr(c, attr))
            out.append(seq[k:k + n])
            k += n
        return out

    def run(which):
        def f(srcs, zones, news, sems):
            for c, a, b, d, e in zip(comms, part(srcs, "srcs"), part(zones, "zones"), part(news, "news"), part(sems, "sems")):
                getattr(c, which)(a, b, d, e)
        return f

    cat = lambda attr: [v for c in comms for v in getattr(c, attr)]
    return _Comm(cat("srcs"), cat("zones"), cat("news"), cat("sems"), run("start"), run("finish"))


def _split(comms, res):
    zones, news = res
    out, kz, kn = [], 0, 0
    for c in comms:
        out.append((zones[kz:kz + len(c.zones)], news[kn:kn + len(c.news)]))
        kz += len(c.zones)
        kn += len(c.news)
    return out


def _call(body, *, name, grid, in_specs, out_specs, out_shape, args, scratch_shapes=(), sem=None, comm=None):
    if comm is None:
        res = pl.pallas_call(body, name=name, grid=grid, in_specs=in_specs, out_specs=out_specs, out_shape=out_shape,
                             scratch_shapes=list(scratch_shapes), compiler_params=_cp(sem))(*args)
        return res, None
    single = not isinstance(out_shape, (list, tuple))
    out_specs_l = [out_specs] if single else list(out_specs)
    out_shape_l = [out_shape] if single else list(out_shape)
    n_in, n_out, n_scr = len(in_specs), len(out_shape_l), len(scratch_shapes)
    ns, nz, nn = len(comm.srcs), len(comm.zones), len(comm.news)

    def wrapped(*refs):
        k = n_in
        ins, srcs = refs[:k], refs[k:k + ns]
        k += ns + nz
        outs, zones, news = refs[k:k + n_out], refs[k + n_out:k + n_out + nz], refs[k + n_out + nz:k + n_out + nz + nn]
        k += n_out + nz + nn
        scr, sems = refs[k:k + n_scr], refs[k + n_scr:]
        if not grid:
            comm.start(srcs, zones, news, sems)
            body(*ins, *outs, *scr)
            comm.finish(srcs, zones, news, sems)
            return
        first = functools.reduce(jnp.logical_and, [pl.program_id(a) == 0 for a in range(len(grid))])
        last = functools.reduce(jnp.logical_and, [pl.program_id(a) == grid[a] - 1 for a in range(len(grid))])

        @pl.when(first)
        def _():
            comm.start(srcs, zones, news, sems)

        body(*ins, *outs, *scr)

        @pl.when(last)
        def _():
            comm.finish(srcs, zones, news, sems)

    res = pl.pallas_call(
        wrapped, name=name, grid=grid,
        in_specs=list(in_specs) + [ANY] * (ns + nz), out_specs=out_specs_l + [ANY] * (nz + nn),
        out_shape=out_shape_l + [jax.ShapeDtypeStruct(z.shape, z.dtype) for z in comm.zones] + comm.news,
        input_output_aliases={n_in + ns + i: n_out + i for i in range(nz)},
        scratch_shapes=list(scratch_shapes) + comm.sems,
        compiler_params=_cp(("arbitrary",) * len(grid)),
    )(*args, *comm.srcs, *comm.zones)
    main = res[0] if single else list(res[:n_out])
    return main, (list(res[n_out:n_out + nz]), list(res[n_out + nz:]))


def _run_comm(comm, name):
    ns, nz, nn = len(comm.srcs), len(comm.zones), len(comm.news)

    def body(*refs):
        srcs, zones, news, sems = refs[:ns], refs[ns + nz:ns + 2 * nz], refs[ns + 2 * nz:ns + 2 * nz + nn], refs[ns + 2 * nz + nn:]
        comm.start(srcs, zones, news, sems)
        comm.finish(srcs, zones, news, sems)

    res = pl.pallas_call(
        body, name=name, in_specs=[ANY] * (ns + nz), out_specs=[ANY] * (nz + nn),
        out_shape=[jax.ShapeDtypeStruct(z.shape, z.dtype) for z in comm.zones] + comm.news,
        input_output_aliases={ns + i: i for i in range(nz)}, scratch_shapes=comm.sems,
    )(*comm.srcs, *comm.zones)
    return list(res[:nz]), list(res[nz:])


def _dot(a, b):
    return jnp.dot(a, b, preferred_element_type=F32)


def _dot_nt(a, b):
    return lax.dot_general(a, b, (((1,), (1,)), ((), ())), preferred_element_type=F32)


def _dot_tn(a, b):
    return lax.dot_general(a, b, (((0,), (0,)), ((), ())), preferred_element_type=F32)


def _normal_cdf(x):
    return 0.5 * (1.0 + lax.erf(x * _SQRT_HALF))


def _gelu_and_grad(x, cdf):
    return x * cdf, cdf + x * jnp.exp(-0.5 * x * x) * _INV_SQRT_2PI


def _sigmoid(x):
    return 0.5 * jnp.tanh(0.5 * x) + 0.5


def _rms_bwd(dy, x, gain):
    r = lax.rsqrt(jnp.mean(x * x, axis=-1, keepdims=True) + EPS)
    xhat = x * r
    gdy = dy * gain
    dx = r * (gdy - xhat * jnp.mean(gdy * xhat, axis=-1, keepdims=True))
    return dx, jnp.sum(dy * xhat, axis=0, keepdims=True)


def _norm_matmul(x, gain, w4, name, comm=None, tm=1024):
    S, D = x.shape
    nsh, _, ns = w4.shape
    tm = min(tm, S)

    def body(x_ref, g_ref, w_ref, hn_ref, o_ref):
        xf = x_ref[...]
        r = lax.rsqrt(jnp.mean(xf * xf, axis=-1, keepdims=True) + EPS)
        hn = (xf * r * g_ref[...]).astype(BF16)
        hn_ref[...] = hn
        for j in range(nsh):
            o_ref[:, j * ns:(j + 1) * ns] = _dot(hn, w_ref[j]).astype(BF16)

    return _call(
        body, name=name, grid=(S // tm,),
        in_specs=[pl.BlockSpec((tm, D), lambda i: (i, 0)), _resident((1, D)), _resident(w4.shape)],
        out_specs=[pl.BlockSpec((tm, D), lambda i: (i, 0)), pl.BlockSpec((tm, nsh * ns), lambda i: (i, 0))],
        out_shape=[jax.ShapeDtypeStruct((S, D), BF16), jax.ShapeDtypeStruct((S, nsh * ns), BF16)],
        sem=("parallel",), args=(x, gain, w4), comm=comm)


def _matmul_res(a, w, res, name, comm=None, tm=1024):
    S, K = a.shape
    N = w.shape[1]
    tm = min(tm, S)

    def body(a_ref, w_ref, r_ref, o_ref):
        o_ref[...] = r_ref[...] + _dot(a_ref[...], w_ref[...])

    return _call(
        body, name=name, grid=(S // tm,),
        in_specs=[pl.BlockSpec((tm, K), lambda i: (i, 0)), _resident(w.shape), pl.BlockSpec((tm, N), lambda i: (i, 0))],
        out_specs=pl.BlockSpec((tm, N), lambda i: (i, 0)),
        out_shape=jax.ShapeDtypeStruct((S, N), F32),
        sem=("parallel",), args=(a, w, res), comm=comm)


def _matmul_nt(dh, w, name, comm=None, tm=1024):
    S, N = dh.shape
    K = w.shape[0]
    tm = min(tm, S)

    def body(d_ref, w_ref, o_ref):
        o_ref[...] = _dot_nt(d_ref[...].astype(BF16), w_ref[...]).astype(BF16)

    return _call(
        body, name=name, grid=(S // tm,),
        in_specs=[pl.BlockSpec((tm, N), lambda i: (i, 0)), _resident(w.shape)],
        out_specs=pl.BlockSpec((tm, K), lambda i: (i, 0)),
        out_shape=jax.ShapeDtypeStruct((S, K), BF16),
        sem=("parallel",), args=(dh, w), comm=comm)


def _matmul_tn(a, b, name, *, ka, nb, out_dtype, comm=None, ts=1024):
    S, KA = a.shape
    NB = b.shape[1]
    ts = min(ts, S)
    J = max(KA // ka, NB // nb)
    a_map = (lambda j, s: (s, j)) if KA // ka > 1 else (lambda j, s: (s, 0))
    b_map = (lambda j, s: (s, j)) if NB // nb > 1 else (lambda j, s: (s, 0))
    last = S // ts - 1

    def body(a_ref, b_ref, o_ref, acc):
        s = pl.program_id(1)

        @pl.when(s == 0)
        def _():
            acc[...] = jnp.zeros_like(acc)

        acc[...] += _dot_tn(a_ref[...].astype(BF16), b_ref[...].astype(BF16))

        @pl.when(s == last)
        def _():
            o_ref[0] = acc[...].astype(out_dtype)

    return _call(
        body, name=name, grid=(J, S // ts),
        in_specs=[pl.BlockSpec((ts, ka), a_map), pl.BlockSpec((ts, nb), b_map)],
        out_specs=pl.BlockSpec((1, ka, nb), lambda j, s: (j, 0, 0)),
        out_shape=jax.ShapeDtypeStruct((J, ka, nb), out_dtype),
        scratch_shapes=[pltpu.VMEM((ka, nb), F32)],
        sem=("parallel", "arbitrary"), args=(a, b), comm=comm)


def _sgu_prep(w_s):
    G = w_s.shape[0]

    def body(w_ref, t_ref, tt_ref):
        tri = lax.broadcasted_iota(jnp.int32, (CHUNK, CHUNK), 0) >= lax.broadcasted_iota(jnp.int32, (CHUNK, CHUNK), 1)
        for g in range(G):
            t = jnp.where(tri, w_ref[g], 0.0)
            t_ref[g] = t.astype(BF16)
            tt_ref[g] = t.T.astype(BF16)

    return pl.pallas_call(
        body, name="sgu_prep",
        out_shape=[jax.ShapeDtypeStruct(w_s.shape, BF16), jax.ShapeDtypeStruct(w_s.shape, BF16)],
        compiler_params=_cp(None),
    )(w_s)


def _sgu_fwd(z, x, vg, wtril, bT, wout, comm=None, tm=512):
    S = z.shape[0]
    W = z.shape[1] // 2
    D = x.shape[1]
    tm = min(tm, S)

    def body(z_ref, x_ref, vg_ref, wt_ref, bT_ref, wo_ref, yp_ref, h_ref, cdf_ref):
        def chunk(c, carry):
            r0 = pl.multiple_of(c * CHUNK, CHUNK)
            zc = z_ref[pl.ds(r0, CHUNK), :].astype(F32)
            cdf = _normal_cdf(zc)
            cdf_ref[pl.ds(r0, CHUNK), :] = cdf.astype(BF16)
            u = zc[:, :W] * cdf[:, :W]
            v = zc[:, W:] * cdf[:, W:]
            rv = lax.rsqrt(jnp.mean(v * v, axis=-1, keepdims=True) + EPS)
            vn = (v * rv * vg_ref[...]).astype(BF16)
            for g in range(SGU_GROUPS):
                sl = slice(g * LANES, (g + 1) * LANES)
                s = _dot(wt_ref[g], vn[:, sl]) + bT_ref[:, g:g + 1]
                yp_ref[pl.ds(r0, CHUNK), sl] = (u[:, sl] * s).astype(BF16)
            return carry

        lax.fori_loop(0, tm // CHUNK, chunk, 0)
        h_ref[...] = x_ref[...] + _dot(yp_ref[...], wo_ref[...])

    return _call(
        body, name="sgu_fwd", grid=(S // tm,),
        in_specs=[pl.BlockSpec((tm, 2 * W), lambda i: (i, 0)), pl.BlockSpec((tm, D), lambda i: (i, 0)),
                  _resident((1, W)), _resident(wtril.shape), _resident(bT.shape), _resident(wout.shape)],
        out_specs=[pl.BlockSpec((tm, W), lambda i: (i, 0)), pl.BlockSpec((tm, D), lambda i: (i, 0)),
                   pl.BlockSpec((tm, 2 * W), lambda i: (i, 0))],
        out_shape=[jax.ShapeDtypeStruct((S, W), BF16), jax.ShapeDtypeStruct((S, D), F32),
                   jax.ShapeDtypeStruct((S, 2 * W), BF16)],
        sem=("parallel",), args=(z, x, vg, wtril, bT, wout), comm=comm)


def _sgu_bwd(dh, z, cdf, vg, wtril, wtrilT, bT, wout, comm=None, tm=512):
    S = z.shape[0]
    W = z.shape[1] // 2
    D = dh.shape[1]
    G = SGU_GROUPS
    tm = min(tm, S)
    last = S // tm - 1

    def body(dh_ref, z_ref, cdf_ref, vg_ref, wt_ref, wtT_ref, bT_ref, wo_ref,
             dz_ref, dws_ref, dbT_ref, dvg_ref, dyp_s, du_s, dvn_s, dsacc):
        i = pl.program_id(0)

        @pl.when(i == 0)
        def _():
            dws_ref[...] = jnp.zeros_like(dws_ref)
            dvg_ref[...] = jnp.zeros_like(dvg_ref)
            dsacc[...] = jnp.zeros_like(dsacc)

        dyp_s[...] = _dot_nt(dh_ref[...].astype(BF16), wo_ref[...])
        tri = lax.broadcasted_iota(jnp.int32, (CHUNK, CHUNK), 0) >= lax.broadcasted_iota(jnp.int32, (CHUNK, CHUNK), 1)

        def chunk(c, carry):
            r0 = pl.multiple_of(c * CHUNK, CHUNK)
            zc = z_ref[pl.ds(r0, CHUNK), :].astype(F32)
            cdf = cdf_ref[pl.ds(r0, CHUNK), :].astype(F32)
            u, gu = _gelu_and_grad(zc[:, :W], cdf[:, :W])
            v, gv = _gelu_and_grad(zc[:, W:], cdf[:, W:])
            rv = lax.rsqrt(jnp.mean(v * v, axis=-1, keepdims=True) + EPS)
            vhat = v * rv
            vgain = vg_ref[...]
            vn = (vhat * vgain).astype(BF16)
            dyp = dyp_s[pl.ds(r0, CHUNK), :]
            for g in range(G):
                sl = slice(g * LANES, (g + 1) * LANES)
                vng = vn[:, sl]
                s = _dot(wt_ref[g], vng) + bT_ref[:, g:g + 1]
                ds = dyp[:, sl] * u[:, sl]
                du_s[:, sl] = dyp[:, sl] * s
                dsb = ds.astype(BF16)
                dvn_s[:, sl] = _dot(wtT_ref[g], dsb)
                dws_ref[g] += jnp.where(tri, _dot_nt(dsb, vng), 0.0)
                dsacc[g] += ds
            dvn = dvn_s[...]
            dvg_ref[...] += jnp.sum(dvn * vhat, axis=0, keepdims=True)
            gdy = dvn * vgain
            dv = rv * (gdy - vhat * jnp.mean(gdy * vhat, axis=-1, keepdims=True))
            dz_ref[pl.ds(r0, CHUNK), :W] = (du_s[...] * gu).astype(BF16)
            dz_ref[pl.ds(r0, CHUNK), W:] = (dv * gv).astype(BF16)
            return carry

        lax.fori_loop(0, tm // CHUNK, chunk, 0)

        @pl.when(i == last)
        def _():
            for g in range(G):
                dbT_ref[:, g:g + 1] = jnp.sum(dsacc[g], axis=1, keepdims=True)

    return _call(
        body, name="sgu_bwd", grid=(S // tm,),
        in_specs=[pl.BlockSpec((tm, D), lambda i: (i, 0)), pl.BlockSpec((tm, 2 * W), lambda i: (i, 0)),
                  pl.BlockSpec((tm, 2 * W), lambda i: (i, 0)),
                  _resident((1, W)), _resident(wtril.shape), _resident(wtrilT.shape), _resident(bT.shape),
                  _resident(wout.shape)],
        out_specs=[pl.BlockSpec((tm, 2 * W), lambda i: (i, 0)),
                   pl.BlockSpec((G, CHUNK, CHUNK), lambda i: (0, 0, 0)),
                   pl.BlockSpec((CHUNK, G), lambda i: (0, 0)),
                   pl.BlockSpec((1, W), lambda i: (0, 0))],
        out_shape=[jax.ShapeDtypeStruct((S, 2 * W), BF16), jax.ShapeDtypeStruct((G, CHUNK, CHUNK), F32),
                   jax.ShapeDtypeStruct((CHUNK, G), F32), jax.ShapeDtypeStruct((1, W), F32)],
        scratch_shapes=[pltpu.VMEM((tm, W), F32), pltpu.VMEM((CHUNK, W), F32), pltpu.VMEM((CHUNK, W), F32),
                        pltpu.VMEM((G, CHUNK, CHUNK), F32)],
        sem=("arbitrary",), args=(dh, z, cdf, vg, wtril, wtrilT, bT, wout), comm=comm)


def _conv_taps(a32, r0, R, reuse=False):
    base = r0 + HALO
    if not reuse:
        return a32[base:base + R, :], a32[base - 1:base - 1 + R, :], a32[base - 2:base - 2 + R, :]
    X = a32[base - 8:base + R, :]
    return X[8:], pltpu.roll(X, 1, 0)[8:], pltpu.roll(X, 2, 0)[8:]


def _ffn_fwd(a, h_in, cw, cb, wdown, name, comm=None, target=None, tm=512, R=64):
    S, C = a.shape
    F = C // 2
    D = h_in.shape[1]
    tm = min(tm, S)
    head = target is not None

    def body(a_ref, halo_ref, h_ref, cw_ref, cb_ref, wd_ref, *rest):
        if head:
            t_ref, f_ref, dy_ref, l_ref, a32 = rest
        else:
            f_ref, ho_ref, a32 = rest
        i = pl.program_id(0)
        a32[0:HALO, :] = jnp.where(i > 0, halo_ref[...].astype(F32), 0.0)
        a32[HALO:, :] = a_ref[...].astype(F32)

        for r0 in range(0, tm, R):
            a0, a1, a2 = _conv_taps(a32, r0, R)
            cpre = cw_ref[0:1, :] * a2 + cw_ref[1:2, :] * a1 + cw_ref[2:3, :] * a0 + cb_ref[...]
            g = cpre[:, :F]
            f_ref[r0:r0 + R, :] = (g * _sigmoid(g) * cpre[:, F:]).astype(BF16)
        h_out = h_ref[...] + _dot(f_ref[...], wd_ref[...])
        if not head:
            ho_ref[...] = h_out
            return

        @pl.when(i == 0)
        def _():
            l_ref[...] = jnp.zeros_like(l_ref)

        e = h_out - t_ref[...]
        dy_ref[...] = e * (1.0 / D)
        rows = jnp.sum(e * e, axis=-1, keepdims=True) * (1.0 / D)
        l_ref[...] += 0.5 * jnp.sum(rows, axis=0, keepdims=True)

    hb = tm // HALO
    row = pl.BlockSpec((tm, D), lambda i: (i, 0))
    in_specs = [pl.BlockSpec((tm, C), lambda i: (i, 0)),
                pl.BlockSpec((HALO, C), lambda i: (jnp.maximum(i * hb - 1, 0), 0)),
                row, _resident((3, C)), _resident((1, C)), _resident(wdown.shape)]
    out_specs = [pl.BlockSpec((tm, F), lambda i: (i, 0)), row]
    out_shape = [jax.ShapeDtypeStruct((S, F), BF16), jax.ShapeDtypeStruct((S, D), F32)]
    args = (a, a, h_in, cw, cb, wdown)
    if head:
        in_specs.append(row)
        out_specs.append(pl.BlockSpec((1, 1), lambda i: (0, 0)))
        out_shape.append(jax.ShapeDtypeStruct((1, 1), F32))
        args += (target,)
    return _call(
        body, name=name, grid=(S // tm,), in_specs=in_specs, out_specs=out_specs, out_shape=out_shape,
        scratch_shapes=[pltpu.VMEM((HALO + tm, C), F32)],
        sem=("arbitrary",) if head else ("parallel",), args=args, comm=comm)


def _ffn_bwd_dc(dh, a, cw, cb, wdown, name, comm=None, tm=256, R=64):
    S, C = a.shape
    F = C // 2
    D = dh.shape[1]
    tm = min(tm, S)

    def body(dh_ref, a_ref, halo_ref, cw_ref, cb_ref, wd_ref, dc_ref, dcw_ref, dcb_ref, a32, df_s, acc):
        i = pl.program_id(0)

        @pl.when(i == 0)
        def _():
            acc[...] = jnp.zeros_like(acc)

        a32[0:HALO, :] = jnp.where(i > 0, halo_ref[...].astype(F32), 0.0)
        a32[HALO:, :] = a_ref[...].astype(F32)
        df_s[...] = _dot_nt(dh_ref[...].astype(BF16), wd_ref[...])
        rows8 = lambda v: functools.reduce(jnp.add, [v[8 * k:8 * k + 8] for k in range(R // 8)])

        for r0 in range(0, tm, R):
            a0, a1, a2 = _conv_taps(a32, r0, R, reuse=True)
            cpre = cw_ref[0:1, :] * a2 + cw_ref[1:2, :] * a1 + cw_ref[2:3, :] * a0 + cb_ref[...]
            g = cpre[:, :F]
            val = cpre[:, F:]
            sg = _sigmoid(g)
            df = df_s[r0:r0 + R, :]
            dg = df * val * (sg * (1.0 + g * (1.0 - sg)))
            dval = df * (g * sg)
            dc = jnp.concatenate([dg, dval], axis=1)
            dc_ref[r0:r0 + R, :] = dc.astype(BF16)
            acc[0] += rows8(dc * a2)
            acc[1] += rows8(dc * a1)
            acc[2] += rows8(dc * a0)
            acc[3] += rows8(dc)

        @pl.when(i == S // tm - 1)
        def _():
            for k in range(3):
                dcw_ref[k:k + 1, :] = jnp.sum(acc[k], axis=0, keepdims=True)
            dcb_ref[...] = jnp.sum(acc[3], axis=0, keepdims=True)

    hb = tm // HALO
    return _call(
        body, name=name, grid=(S // tm,),
        in_specs=[pl.BlockSpec((tm, D), lambda i: (i, 0)),
                  pl.BlockSpec((tm, C), lambda i: (i, 0)),
                  pl.BlockSpec((HALO, C), lambda i: (jnp.maximum(i * hb - 1, 0), 0)),
                  _resident((3, C)), _resident((1, C)), _resident(wdown.shape)],
        out_specs=[pl.BlockSpec((tm, C), lambda i: (i, 0)),
                   pl.BlockSpec((3, C), lambda i: (0, 0)), pl.BlockSpec((1, C), lambda i: (0, 0))],
        out_shape=[jax.ShapeDtypeStruct((S, C), BF16), jax.ShapeDtypeStruct((3, C), F32),
                   jax.ShapeDtypeStruct((1, C), F32)],
        scratch_shapes=[pltpu.VMEM((HALO + tm, C), F32), pltpu.VMEM((tm, F), F32), pltpu.VMEM((4, 8, C), F32)],
        sem=("arbitrary",), args=(dh, a, a, cw, cb, wdown), comm=comm)


def _bwd_norm(dA, w4, h_in, gain, dh_out, name, conv_w=None, comm=None, R=64):
    S, N = dA.shape
    nsh, D, ns = w4.shape
    conv = conv_w is not None
    tm = min(256 if conv else 512, S)
    nt = S // tm

    def finish(src_ref, w_ref, h_ref, g_ref, dho_ref, dhi_ref, dg_ref):
        dhn = _dot_nt(src_ref[:, 0:ns], w_ref[0])
        for j in range(1, nsh):
            dhn += _dot_nt(src_ref[:, j * ns:(j + 1) * ns], w_ref[j])
        dx, dgain = _rms_bwd(dhn, h_ref[...], g_ref[...])
        dg_ref[...] += dgain
        dhi_ref[...] = dho_ref[...] + dx

    def body_plain(dA_ref, w_ref, h_ref, g_ref, dho_ref, dhi_ref, dg_ref):
        @pl.when(pl.program_id(0) == 0)
        def _():
            dg_ref[...] = jnp.zeros_like(dg_ref)

        finish(dA_ref, w_ref, h_ref, g_ref, dho_ref, dhi_ref, dg_ref)

    def body_conv(dc_ref, halo_ref, cw_ref, w_ref, h_ref, g_ref, dho_ref, dhi_ref, dg_ref, da_ref, dc32):
        i = pl.program_id(0)

        @pl.when(i == 0)
        def _():
            dg_ref[...] = jnp.zeros_like(dg_ref)

        dc32[0:tm, :] = dc_ref[...].astype(F32)
        dc32[tm:, :] = jnp.where(i < nt - 1, halo_ref[...].astype(F32), 0.0)

        for r0 in range(0, tm, R):
            d0 = dc32[r0:r0 + R, :]
            d1 = dc32[r0 + 1:r0 + 1 + R, :]
            d2 = dc32[r0 + 2:r0 + 2 + R, :]
            da = cw_ref[2:3, :] * d0 + cw_ref[1:2, :] * d1 + cw_ref[0:1, :] * d2
            da_ref[r0:r0 + R, :] = da.astype(BF16)
        finish(da_ref, w_ref, h_ref, g_ref, dho_ref, dhi_ref, dg_ref)

    row = lambda width: pl.BlockSpec((tm, width), lambda i: (i, 0))
    common_in = [_resident(w4.shape), row(D), _resident((1, D)), row(D)]
    common_out = [row(D), pl.BlockSpec((1, D), lambda i: (0, 0))]
    common_shape = [jax.ShapeDtypeStruct((S, D), F32), jax.ShapeDtypeStruct((1, D), F32)]
    if not conv:
        return _call(
            body_plain, name=name, grid=(nt,),
            in_specs=[row(N)] + common_in, out_specs=common_out, out_shape=common_shape,
            sem=("arbitrary",), args=(dA, w4, h_in, gain, dh_out), comm=comm)
    hb = tm // HALO
    nhb = S // HALO
    return _call(
        body_conv, name=name, grid=(nt,),
        in_specs=[row(N), pl.BlockSpec((HALO, N), lambda i: (jnp.minimum((i + 1) * hb, nhb - 1), 0)),
                  _resident((3, N))] + common_in,
        out_specs=common_out + [row(N)],
        out_shape=common_shape + [jax.ShapeDtypeStruct((S, N), BF16)],
        scratch_shapes=[pltpu.VMEM((tm + HALO, N), F32)],
        sem=("arbitrary",), args=(dA, dA, conv_w, w4, h_in, gain, dh_out), comm=comm)


def _rel_buckets_flat():
    q = np.arange(CHUNK)[:, None] + CHUNK
    k = np.arange(2 * CHUNK)[None, :]
    n = np.maximum(q - k, 0)
    max_exact = REL_BUCKETS // 2
    large = max_exact + (np.log(np.maximum(n, 1).astype(np.float32) / max_exact)
                         / math.log(REL_MAX_DIST / max_exact) * (REL_BUCKETS - max_exact)).astype(np.int32)
    large = np.minimum(large, REL_BUCKETS - 1)
    return np.where(n < max_exact, n, large).astype(np.int32).reshape(1, CHUNK * 2 * CHUNK)


def _split_bf16(x):
    hi = x.astype(BF16)
    return hi, (x - hi.astype(F32)).astype(BF16)


def _rel_bias_expand(rel_bias, bucket):
    B, H = rel_bias.shape
    n = bucket.shape[1]

    def body(rb_ref, bk_ref, o_ref):
        oh = (bk_ref[...] == lax.broadcasted_iota(jnp.int32, (B, n), 0)).astype(BF16)
        hi, lo = _split_bf16(rb_ref[...])
        o_ref[...] = _dot_tn(hi, oh) + _dot_tn(lo, oh)

    return pl.pallas_call(body, name="rel_bias_expand", out_shape=jax.ShapeDtypeStruct((H, n), F32),
                          compiler_params=_cp(None))(rel_bias, bucket)


def _rel_bias_reduce(dbias, bucket):
    H, n = dbias.shape
    B = REL_BUCKETS

    def body(db_ref, bk_ref, o_ref):
        oh = (bk_ref[...] == lax.broadcasted_iota(jnp.int32, (B, n), 0)).astype(BF16)
        hi, lo = _split_bf16(db_ref[...])
        o_ref[...] = _dot_nt(oh, hi) + _dot_nt(oh, lo)

    return pl.pallas_call(body, name="rel_bias_reduce", out_shape=jax.ShapeDtypeStruct((B, H), F32),
                          compiler_params=_cp(None))(dbias, bucket)


def _lo_mask(rows):
    return lax.broadcasted_iota(jnp.int32, (rows, LANES), 1) < HEAD_DIM


def _half_sums(y, lo):
    s_lo = jnp.sum(jnp.where(lo, y, 0.0), axis=-1, keepdims=True)
    s_hi = jnp.sum(jnp.where(lo, 0.0, y), axis=-1, keepdims=True)
    return jnp.where(lo, s_lo, s_hi)


def _half_rms(x, gain, lo):
    r = lax.rsqrt(_half_sums(x * x, lo) * (1.0 / HEAD_DIM) + EPS)
    xhat = x * r
    return xhat * gain, xhat, r


def _half_rms_bwd(dy, xhat, r, gain, lo):
    gdy = dy * gain
    dx = r * (gdy - xhat * (_half_sums(gdy * xhat, lo) * (1.0 / HEAD_DIM)))
    return dx, jnp.sum(dy * xhat, axis=0, keepdims=True)


def _dup_half(pair, e, lo):
    sw = pltpu.roll(pair, HEAD_DIM, 1)
    return jnp.where(lo, pair, sw) if e == 0 else jnp.where(lo, sw, pair)


def _band_valid(n):
    qi = lax.broadcasted_iota(jnp.int32, (KV_GROUP * CHUNK, 2 * CHUNK), 0) & (CHUNK - 1)
    ki = lax.broadcasted_iota(jnp.int32, (KV_GROUP * CHUNK, 2 * CHUNK), 1)
    dist = qi + CHUNK - ki
    return (dist >= 0) & (dist < CHUNK) & ((n > 0) | (ki >= CHUNK))


def _stack_heads(a, b, lo):
    return jnp.concatenate([jnp.where(lo, a, 0.0), jnp.where(lo, 0.0, a), jnp.where(lo, b, 0.0), jnp.where(lo, 0.0, b)],
                           axis=0)


def _unstack_heads(x4, lo):
    return (jnp.where(lo, x4[0:CHUNK], x4[CHUNK:2 * CHUNK]),
            jnp.where(lo, x4[2 * CHUNK:3 * CHUNK], x4[3 * CHUNK:]))


def _sink_col(sink_ref, hk):
    row = lax.broadcasted_iota(jnp.int32, (KV_GROUP * CHUNK, 1), 0)
    col = jnp.full((KV_GROUP * CHUNK, 1), sink_ref[KV_GROUP * hk + KV_GROUP - 1], F32)
    for r in range(KV_GROUP - 2, -1, -1):
        col = jnp.where(row < (r + 1) * CHUNK, sink_ref[KV_GROUP * hk + r], col)
    return col


def _softmax_sink(s, valid, sink):
    s = jnp.where(valid, s, -jnp.inf)
    m = jnp.maximum(jnp.max(s, axis=-1, keepdims=True), sink)
    p = jnp.exp(s - m)
    esink = jnp.exp(sink - m)
    inv = 1.0 / (jnp.sum(p, axis=-1, keepdims=True) + esink)
    return p * inv, esink * inv


QW = N_HEADS * HEAD_DIM
KVW = N_KV_HEADS * HEAD_DIM


def _attn_fwd(qkv, qg2, kg2, sinks, bias, comm=None):
    S = qkv.shape[0]
    nb = S // CHUNK

    def body(cur_ref, prev_ref, qg_ref, kg_ref, sink_ref, bias_ref, o_ref):
        n = pl.program_id(0)
        lo = _lo_mask(CHUNK)
        lo2 = _lo_mask(2 * CHUNK)
        valid = _band_valid(n)
        for j in range(N_KV_HEADS // 2):
            kc = slice(QW + j * LANES, QW + (j + 1) * LANES)
            vc = slice(QW + KVW + j * LANES, QW + KVW + (j + 1) * LANES)
            kpair = jnp.concatenate([prev_ref[:, j * LANES:(j + 1) * LANES], cur_ref[:, kc]], axis=0).astype(F32)
            vpair = jnp.concatenate([prev_ref[:, KVW + j * LANES:KVW + (j + 1) * LANES], cur_ref[:, vc]], axis=0).astype(F32)
            knpair, _, _ = _half_rms(kpair, kg_ref[...], lo2)
            for e in range(2):
                hk = 2 * j + e
                kdup = _dup_half(knpair, e, lo2).astype(BF16)
                vdup = _dup_half(vpair, e, lo2).astype(BF16)
                ca = slice(2 * hk * LANES, (2 * hk + 1) * LANES)
                cb = slice((2 * hk + 1) * LANES, (2 * hk + 2) * LANES)
                qna, _, _ = _half_rms(cur_ref[:, ca].astype(F32), qg_ref[...], lo)
                qnb, _, _ = _half_rms(cur_ref[:, cb].astype(F32), qg_ref[...], lo)
                qm4 = _stack_heads(qna, qnb, lo).astype(BF16)
                s = _dot_nt(qm4, kdup) * (HEAD_DIM ** -0.5) + bias_ref[hk]
                p, _ = _softmax_sink(s, valid, _sink_col(sink_ref, hk))
                oa, ob = _unstack_heads(_dot(p.astype(BF16), vdup), lo)
                o_ref[:, ca] = oa.astype(BF16)
                o_ref[:, cb] = ob.astype(BF16)

    return _call(
        body, name="attn_fwd", grid=(nb,),
        in_specs=[pl.BlockSpec((CHUNK, QW + 2 * KVW), lambda n: (n, 0)),
                  pl.BlockSpec((CHUNK, 2 * KVW), lambda n: (jnp.maximum(n - 1, 0), QW // (2 * KVW))),
                  _resident((1, LANES)), _resident((1, LANES)),
                  pl.BlockSpec(memory_space=pltpu.SMEM),
                  _resident(bias.shape)],
        out_specs=pl.BlockSpec((CHUNK, QW), lambda n: (n, 0)),
        out_shape=jax.ShapeDtypeStruct((S, QW), BF16),
        sem=("parallel",), args=(qkv, qkv, qg2, kg2, sinks, bias), comm=comm)


def _attn_bwd(qkv, do, qg2, kg2, sinks, bias, comm=None):
    S = qkv.shape[0]
    nb = S // CHUNK

    def body(cur_ref, prev_ref, do_ref, qg_ref, kg_ref, sink_ref, bias_ref,
             dqkv_ref, dbias_ref, dqg_ref, dkg_ref, dsink_ref, carry, band, dsacc, gacc):
        i = pl.program_id(0)
        n = nb - 1 - i
        lo = _lo_mask(CHUNK)
        lo2 = _lo_mask(2 * CHUNK)
        lane = lax.broadcasted_iota(jnp.int32, (KV_GROUP * CHUNK, LANES), 1)
        valid = _band_valid(n)

        @pl.when(i == 0)
        def _():
            dbias_ref[...] = jnp.zeros_like(dbias_ref)
            carry[...] = jnp.zeros_like(carry)
            dsacc[...] = jnp.zeros_like(dsacc)
            gacc[...] = jnp.zeros_like(gacc)

        qgain = qg_ref[...]
        kgain = kg_ref[...]
        for j in range(N_KV_HEADS // 2):
            kc = slice(QW + j * LANES, QW + (j + 1) * LANES)
            vc = slice(QW + KVW + j * LANES, QW + KVW + (j + 1) * LANES)
            kpair = jnp.concatenate([prev_ref[:, j * LANES:(j + 1) * LANES], cur_ref[:, kc]], axis=0).astype(F32)
            vpair = jnp.concatenate([prev_ref[:, KVW + j * LANES:KVW + (j + 1) * LANES], cur_ref[:, vc]], axis=0).astype(F32)
            knpair, khat, kr = _half_rms(kpair, kgain, lo2)
            dk_folds = []
            dv_folds = []
            for e in range(2):
                hk = 2 * j + e
                kdup = _dup_half(knpair, e, lo2).astype(BF16)
                vdup = _dup_half(vpair, e, lo2).astype(BF16)
                ca = slice(2 * hk * LANES, (2 * hk + 1) * LANES)
                cb = slice((2 * hk + 1) * LANES, (2 * hk + 2) * LANES)
                qna, qhata, qra = _half_rms(cur_ref[:, ca].astype(F32), qgain, lo)
                qnb, qhatb, qrb = _half_rms(cur_ref[:, cb].astype(F32), qgain, lo)
                qm4 = _stack_heads(qna, qnb, lo).astype(BF16)
                dom4 = _stack_heads(do_ref[:, ca].astype(F32), do_ref[:, cb].astype(F32), lo).astype(BF16)
                s = _dot_nt(qm4, kdup) * (HEAD_DIM ** -0.5) + bias_ref[hk]
                p, psink = _softmax_sink(s, valid, _sink_col(sink_ref, hk))
                dp = _dot_nt(dom4, vdup)
                delta = jnp.sum(p * dp, axis=-1, keepdims=True)
                ds = p * (dp - delta)
                dbias_ref[hk] += ds
                dsacc[...] += jnp.where(lane == hk, -(psink * delta), 0.0)
                dsr = (ds * (HEAD_DIM ** -0.5)).astype(BF16)
                dqna, dqnb = _unstack_heads(_dot(dsr, kdup), lo)
                dkd = _dot_tn(dsr, qm4)
                dvd = _dot_tn(p.astype(BF16), dom4)
                dqa, dqga = _half_rms_bwd(dqna, qhata, qra, qgain, lo)
                dqb, dqgb = _half_rms_bwd(dqnb, qhatb, qrb, qgain, lo)
                gacc[0:1, :] += dqga + dqgb
                dqkv_ref[:, ca] = dqa.astype(BF16)
                dqkv_ref[:, cb] = dqb.astype(BF16)
                dk_folds.append(dkd + pltpu.roll(dkd, HEAD_DIM, 1))
                dv_folds.append(dvd + pltpu.roll(dvd, HEAD_DIM, 1))
            dkn = jnp.where(lo2, dk_folds[0], dk_folds[1])
            dk, dkg = _half_rms_bwd(dkn, khat, kr, kgain, lo2)
            gacc[1:2, :] += dkg
            band[:, j * LANES:(j + 1) * LANES] = dk
            band[:, KVW + j * LANES:KVW + (j + 1) * LANES] = jnp.where(lo2, dv_folds[0], dv_folds[1])
        dqkv_ref[:, QW:] = (band[CHUNK:, :] + carry[...]).astype(BF16)
        carry[...] = band[0:CHUNK, :]

        @pl.when(i == nb - 1)
        def _():
            g = gacc[...]
            g = g + pltpu.roll(g, HEAD_DIM, 1)
            dqg_ref[...] = g[0:1, :]
            dkg_ref[...] = g[1:2, :]
            for r in range(KV_GROUP):
                dsink_ref[r:r + 1, :] = jnp.sum(dsacc[r * CHUNK:(r + 1) * CHUNK, :], axis=0, keepdims=True)

    vec = pl.BlockSpec((1, LANES), lambda i: (0, 0))
    return _call(
        body, name="attn_bwd", grid=(nb,),
        in_specs=[pl.BlockSpec((CHUNK, QW + 2 * KVW), lambda i: (nb - 1 - i, 0)),
                  pl.BlockSpec((CHUNK, 2 * KVW), lambda i: (jnp.maximum(nb - 2 - i, 0), QW // (2 * KVW))),
                  pl.BlockSpec((CHUNK, QW), lambda i: (nb - 1 - i, 0)),
                  _resident((1, LANES)), _resident((1, LANES)),
                  pl.BlockSpec(memory_space=pltpu.SMEM),
                  _resident(bias.shape)],
        out_specs=[pl.BlockSpec((CHUNK, QW + 2 * KVW), lambda i: (nb - 1 - i, 0)),
                   pl.BlockSpec(bias.shape, lambda i: (0, 0, 0)), vec, vec,
                   pl.BlockSpec((KV_GROUP, LANES), lambda i: (0, 0))],
        out_shape=[jax.ShapeDtypeStruct((S, QW + 2 * KVW), BF16),
                   jax.ShapeDtypeStruct(bias.shape, F32),
                   jax.ShapeDtypeStruct((1, LANES), F32), jax.ShapeDtypeStruct((1, LANES), F32),
                   jax.ShapeDtypeStruct((KV_GROUP, LANES), F32)],
        scratch_shapes=[pltpu.VMEM((CHUNK, 2 * KVW), F32), pltpu.VMEM((2 * CHUNK, 2 * KVW), F32),
                        pltpu.VMEM((KV_GROUP * CHUNK, LANES), F32), pltpu.VMEM((8, LANES), F32)],
        sem=("arbitrary",), args=(qkv, qkv, do, qg2, kg2, sinks, bias), comm=comm)


def _row_tile(rows, cols, n_arrays):
    budget = VMEM_LIMIT_V7X // 4 // (2 * n_arrays * 4 * cols)
    best = 8
    for n in range(1, rows // 8 + 1):
        if rows % n == 0 and (rows // n) % 8 == 0 and rows // n <= budget:
            best = rows // n
            break
    return best


def _adamw(g, w, m, v, name, comm=None):
    R, C = g.shape
    tr = _row_tile(R, C, 8)

    def body(g_ref, w_ref, m_ref, v_ref, d_ref, mo_ref, vo_ref, go_ref):
        gg = g_ref[...]
        go_ref[...] = gg
        mn = ADAM_B1 * m_ref[...] + (1.0 - ADAM_B1) * gg
        vn = ADAM_B2 * v_ref[...] + (1.0 - ADAM_B2) * jnp.square(gg)
        m_hat = mn / (1.0 - ADAM_B1 ** ADAM_STEP)
        v_hat = vn / (1.0 - ADAM_B2 ** ADAM_STEP)
        d_ref[...] = -ADAM_LR * (m_hat / (jnp.sqrt(v_hat) + ADAM_EPS) + ADAM_WD * w_ref[...])
        mo_ref[...] = mn
        vo_ref[...] = vn

    spec = pl.BlockSpec((tr, C), lambda i: (i, 0))
    return _call(
        body, name=name, grid=(R // tr,), in_specs=[spec] * 4, out_specs=[spec] * 4,
        out_shape=[jax.ShapeDtypeStruct((R, C), F32)] * 4, sem=("parallel",), args=(g, w, m, v), comm=comm)


def _place_shard(shards, layer, where, dtype, name):
    _, R, C = shards.shape
    tr = _row_tile(R, C, 2) if R % 8 == 0 else R

    def body(s_ref, x_ref, o_ref):
        o_ref[...] = x_ref[...].astype(dtype)

    return pl.pallas_call(
        body, name=name,
        grid_spec=pltpu.PrefetchScalarGridSpec(
            num_scalar_prefetch=1, grid=(R // tr,),
            in_specs=[pl.BlockSpec((1, tr, C), lambda i, s_ref: (layer, i, 0))],
            out_specs=pl.BlockSpec((1, tr, C), lambda i, s_ref: (s_ref[1], i, 0))),
        out_shape=jax.ShapeDtypeStruct((4, R, C), dtype),
        compiler_params=_cp(("parallel",)),
    )(where, shards)


def _pair_add(g4, rsib, where, name):
    J, R, C = g4.shape
    Rh = R // 2
    tr = _row_tile(Rh, C, 4)
    g5 = g4.reshape(J, 2, Rh, C)

    def body(s_ref, g_ref, r_ref, p_ref, q_ref):
        val = (g_ref[...].astype(F32)[0] + r_ref[...].astype(F32)).astype(BF16)
        p_ref[...] = val

        @pl.when(pl.program_id(1) == s_ref[1])
        def _():
            q_ref[...] = val

    return pl.pallas_call(
        body, name=name,
        grid_spec=pltpu.PrefetchScalarGridSpec(
            num_scalar_prefetch=1, grid=(Rh // tr, J),
            in_specs=[pl.BlockSpec((1, 1, tr, C), lambda i, j, s_ref: (j, s_ref[0], i, 0)),
                      pl.BlockSpec((1, tr, C), lambda i, j, s_ref: (j, i, 0))],
            out_specs=[pl.BlockSpec((1, tr, C), lambda i, j, s_ref: (j, i, 0)),
                       pl.BlockSpec((1, tr, C), lambda i, j, s_ref: (s_ref[1], i, 0))]),
        out_shape=[jax.ShapeDtypeStruct((J, Rh, C), BF16)] * 2,
        compiler_params=_cp(("parallel", "arbitrary")),
    )(where, g5, rsib)


def _sum_chips(q, where, dest, layer, out_shape, name):
    J, Rh, C = q.shape
    tr = _row_tile(Rh, C, 3)
    nb = Rh // tr

    def body(s_ref, q_ref, *rest):
        qq = q_ref[...].astype(F32)
        rest[-1][0] = ((qq[0] + qq[1]) + qq[2]) + qq[3]

    have = dest is not None
    return pl.pallas_call(
        body, name=name,
        grid_spec=pltpu.PrefetchScalarGridSpec(
            num_scalar_prefetch=1, grid=(nb,),
            in_specs=[pl.BlockSpec((J, tr, C), lambda i, s_ref: (0, i, 0))] + ([ANY] if have else []),
            out_specs=pl.BlockSpec((1, tr, C), lambda i, s_ref: (layer, s_ref[0] * nb + i, 0))),
        out_shape=jax.ShapeDtypeStruct(out_shape, F32),
        input_output_aliases={2: 0} if have else {},
        compiler_params=_cp(("parallel",)),
    )(*((where, q, dest) if have else (where, q)))


MESH = pl.DeviceIdType.MESH
ANY = pl.BlockSpec(memory_space=pl.ANY)


def _place():
    x, y, c = lax.axis_index("x"), lax.axis_index("y"), lax.axis_index("c")
    others = [(1 - x, y), (x, 1 - y), (1 - x, 1 - y)]
    return x, y, c, 2 * x + y, others, [2 * ox + oy for ox, oy in others]


def _gather_comm(placed, split):
    n = len(placed)

    def rows(t, ref, half):
        if not split[t]:
            return ref
        rh = placed[t].shape[1] // 2
        return ref.at[pl.ds(half * rh, rh), :]

    def sends(outs, sems):
        send, recv = sems[0], sems[1]
        x, y, c, me, others, okey = _place()
        cps = []
        for t in range(n):
            mine = rows(t, outs[t].at[me], c)
            for j, (ox, oy) in enumerate(others):
                cps.append(pltpu.make_async_remote_copy(
                    src_ref=mine, dst_ref=mine,
                    send_sem=send.at[t, j], recv_sem=recv.at[t, j], device_id=(ox, oy, c), device_id_type=MESH))
        return cps

    def start(srcs, outs, news, sems):
        for cp in sends(outs, sems):
            cp.start()

    def finish(srcs, outs, news, sems):
        send, recv, fsend, frecv = sems
        x, y, c, me, others, okey = _place()
        sib = (x, y, 1 - c)
        first = sends(outs, sems)
        passed = []
        for t in range(n):
            for j in range(3):
                landed = rows(t, outs[t].at[okey[j]], c)
                pltpu.make_async_remote_copy(
                    src_ref=landed, dst_ref=landed, send_sem=send.at[t, j], recv_sem=recv.at[t, j],
                    device_id=sib, device_id_type=MESH).wait_recv()
                if split[t]:
                    cp = pltpu.make_async_remote_copy(
                        src_ref=landed, dst_ref=landed, send_sem=fsend.at[t, j], recv_sem=frecv.at[t, j],
                        device_id=sib, device_id_type=MESH)
                    cp.start()
                    passed.append(cp)
        for t in range(n):
            if split[t]:
                for j in range(3):
                    theirs = rows(t, outs[t].at[okey[j]], 1 - c)
                    pltpu.make_async_remote_copy(
                        src_ref=theirs, dst_ref=theirs, send_sem=fsend.at[t, j], recv_sem=frecv.at[t, j],
                        device_id=sib, device_id_type=MESH).wait_recv()
        for cp in first + passed:
            cp.wait_send()

    return _Comm([], placed, [], [pltpu.SemaphoreType.DMA((n, 3))] * 4, start, finish)


def _pair_exchange_comm(gs):
    n = len(gs)

    def copies(ins, outs, sems):
        send, recv = sems
        x, y, c, _, _, _ = _place()
        cps = []
        for t in range(n):
            rh = gs[t].shape[1] // 2
            cps.append(pltpu.make_async_remote_copy(
                src_ref=ins[t].at[:, pl.ds((1 - c) * rh, rh), :], dst_ref=outs[t],
                send_sem=send.at[t], recv_sem=recv.at[t], device_id=(x, y, 1 - c), device_id_type=MESH))
        return cps

    def start(ins, zones, outs, sems):
        for cp in copies(ins, outs, sems):
            cp.start()

    def finish(ins, zones, outs, sems):
        for cp in copies(ins, outs, sems):
            cp.wait()

    news = [jax.ShapeDtypeStruct((4, g.shape[1] // 2, g.shape[2]), g.dtype) for g in gs]
    return _Comm(gs, [], news, [pltpu.SemaphoreType.DMA((n,))] * 2, start, finish)


def _chip_scatter_comm(ps, qs):
    n = len(ps)

    def sends(ins, outs, sems):
        send, recv = sems
        x, y, c, me, others, okey = _place()
        return [pltpu.make_async_remote_copy(
            src_ref=ins[t].at[okey[j]], dst_ref=outs[t].at[me],
            send_sem=send.at[t, j], recv_sem=recv.at[t, j], device_id=(ox, oy, c), device_id_type=MESH)
            for t in range(n) for j, (ox, oy) in enumerate(others)]

    def start(ins, outs, news, sems):
        for cp in sends(ins, outs, sems):
            cp.start()

    def finish(ins, outs, news, sems):
        send, recv = sems
        x, y, c, me, others, okey = _place()
        for t in range(n):
            for j in range(3):
                slot = outs[t].at[okey[j]]
                pltpu.make_async_remote_copy(
                    src_ref=slot, dst_ref=slot, send_sem=send.at[t, j], recv_sem=recv.at[t, j],
                    device_id=(x, y, c), device_id_type=MESH).wait_recv()
        for cp in sends(ins, outs, sems):
            cp.wait_send()

    return _Comm(ps, qs, [], [pltpu.SemaphoreType.DMA((n, 3))] * 2, start, finish)


def _half_exchange_comm(arrs, layers=None):
    n = len(arrs)
    items = [(t, layer) for t in range(n) for layer in (range(arrs[t].shape[0]) if layers is None else layers[t])]

    def sends(outs, sems):
        send, recv = sems
        x, y, c, _, _, _ = _place()
        cps = []
        for k, (t, layer) in enumerate(items):
            rh = arrs[t].shape[1] // 2
            mine = outs[t].at[layer, pl.ds(c * rh, rh), :]
            cps.append(pltpu.make_async_remote_copy(
                src_ref=mine, dst_ref=mine, send_sem=send.at[k], recv_sem=recv.at[k],
                device_id=(x, y, 1 - c), device_id_type=MESH))
        return cps

    def start(srcs, outs, news, sems):
        for cp in sends(outs, sems):
            cp.start()

    def finish(srcs, outs, news, sems):
        send, recv = sems
        x, y, c, _, _, _ = _place()
        for k, (t, layer) in enumerate(items):
            rh = arrs[t].shape[1] // 2
            theirs = outs[t].at[layer, pl.ds((1 - c) * rh, rh), :]
            pltpu.make_async_remote_copy(
                src_ref=theirs, dst_ref=theirs, send_sem=send.at[k], recv_sem=recv.at[k],
                device_id=(x, y, 1 - c), device_id_type=MESH).wait_recv()
        for cp in sends(outs, sems):
            cp.wait_send()

    return _Comm([], arrs, [], [pltpu.SemaphoreType.DMA((len(items),))] * 2, start, finish)


SMALL_COLS = 1024
SMALL_PIECE_ROWS = 48


def _allreduce_small(buf, comm=None):
    pr = SMALL_PIECE_ROWS
    flips = [(d >> 2 & 1, d >> 1 & 1, d & 1) for d in range(1, 8)]

    def body(x_ref, o_ref, rbuf, send1, recv1, send2, recv2):
        x, y, c = lax.axis_index("x"), lax.axis_index("y"), lax.axis_index("c")
        me = 4 * x + 2 * y + c
        peers = [(x ^ fx, y ^ fy, c ^ fc) for fx, fy, fc in flips]
        pid = [4 * px + 2 * py + pc for px, py, pc in peers]

        def piece(ref, p):
            return ref.at[pl.ds(pl.multiple_of(p * pr, 8), pr), :]

        cps = []
        for d in range(7):
            cp = pltpu.make_async_remote_copy(
                src_ref=piece(x_ref, pid[d]), dst_ref=rbuf.at[d + 1],
                send_sem=send1.at[d], recv_sem=recv1.at[d], device_id=peers[d], device_id_type=MESH)
            cp.start()
            cps.append(cp)
        acc = piece(x_ref, me)[...]
        for d in range(7):
            cps[d].wait_recv()
            acc = acc + rbuf[d + 1]
        piece(o_ref, me)[...] = acc
        out = []
        for d in range(7):
            cp = pltpu.make_async_remote_copy(
                src_ref=piece(o_ref, me), dst_ref=piece(o_ref, me),
                send_sem=send2.at[d], recv_sem=recv2.at[d], device_id=peers[d], device_id_type=MESH)
            cp.start()
            out.append(cp)
        for d in range(7):
            pltpu.make_async_remote_copy(
                src_ref=piece(o_ref, pid[d]), dst_ref=piece(o_ref, pid[d]),
                send_sem=send2.at[d], recv_sem=recv2.at[d], device_id=peers[d], device_id_type=MESH).wait_recv()
        for cp in cps + out:
            cp.wait_send()

    vm = pl.BlockSpec(memory_space=pltpu.VMEM)
    return _call(
        body, name="small_allreduce", grid=(), in_specs=[vm], out_specs=vm,
        out_shape=jax.ShapeDtypeStruct(buf.shape, F32),
        scratch_shapes=[pltpu.VMEM((8, pr, SMALL_COLS), F32)] + [pltpu.SemaphoreType.DMA((7,))] * 4,
        args=(buf,), comm=comm)


def _rows_of(shape):
    return -(-math.prod(shape) // (8 * SMALL_COLS)) * 8


def _pack(arrays, rows):
    parts = []
    for a in arrays:
        r = _rows_of(a.shape)
        parts.append(jnp.pad(a.reshape(-1), (0, r * SMALL_COLS - a.size)).reshape(r, SMALL_COLS))
    used = sum(p.shape[0] for p in parts)
    if rows > used:
        parts.append(jnp.zeros((rows - used, SMALL_COLS), F32))
    return jnp.concatenate(parts, axis=0)


def _unpack(buf, shapes):
    out, off = [], 0
    for s in shapes:
        r = _rows_of(s)
        out.append(buf[off:off + r].reshape(-1)[:math.prod(s)].reshape(s))
        off += r
    return out


BIG = ["sgu_w_in", "sgu_w_out", "attn_w_qkv", "attn_w_o", "ffn_w_up", "ffn_w_down"]
SMALL = ["mix_norm", "ffn_norm", "sgu_v_gain", "sgu_w_s", "sgu_b_s", "attn_q_gain", "attn_k_gain", "attn_sinks",
         "rel_bias", "ffn_conv_b"]
ORDER = ["mix_norm", "ffn_norm", "sgu_w_in", "sgu_v_gain", "sgu_w_s", "sgu_b_s", "sgu_w_out", "attn_w_qkv",
         "attn_q_gain", "attn_k_gain", "attn_sinks", "attn_w_o", "rel_bias", "ffn_w_up", "ffn_conv_w", "ffn_conv_b",
         "ffn_w_down"]


def kernel(x, mix_norm, ffn_norm, sgu_w_in, sgu_v_gain, sgu_w_s, sgu_b_s, sgu_w_out, attn_w_qkv, attn_q_gain, attn_k_gain, attn_sinks, attn_w_o, rel_bias, ffn_w_up, ffn_conv_w, ffn_conv_b, ffn_w_down, loss_target, m_mix_norm, m_ffn_norm, m_sgu_w_in, m_sgu_v_gain, m_sgu_w_s, m_sgu_b_s, m_sgu_w_out, m_attn_w_qkv, m_attn_q_gain, m_attn_k_gain, m_attn_sinks, m_attn_w_o, m_rel_bias, m_ffn_w_up, m_ffn_conv_w, m_ffn_conv_b, m_ffn_w_down, v_mix_norm, v_ffn_norm, v_sgu_w_in, v_sgu_v_gain, v_sgu_w_s, v_sgu_b_s, v_sgu_w_out, v_attn_w_qkv, v_attn_q_gain, v_attn_k_gain, v_attn_sinks, v_attn_w_o, v_rel_bias, v_ffn_w_up, v_ffn_conv_w, v_ffn_conv_b, v_ffn_w_down):
    w = dict(mix_norm=mix_norm, ffn_norm=ffn_norm, sgu_w_in=sgu_w_in, sgu_v_gain=sgu_v_gain, sgu_w_s=sgu_w_s,
             sgu_b_s=sgu_b_s, sgu_w_out=sgu_w_out, attn_w_qkv=attn_w_qkv, attn_q_gain=attn_q_gain,
             attn_k_gain=attn_k_gain, attn_sinks=attn_sinks, attn_w_o=attn_w_o, rel_bias=rel_bias, ffn_w_up=ffn_w_up,
             ffn_conv_w=ffn_conv_w, ffn_conv_b=ffn_conv_b, ffn_w_down=ffn_w_down)
    mom = dict(mix_norm=m_mix_norm, ffn_norm=m_ffn_norm, sgu_w_in=m_sgu_w_in, sgu_v_gain=m_sgu_v_gain,
               sgu_w_s=m_sgu_w_s, sgu_b_s=m_sgu_b_s, sgu_w_out=m_sgu_w_out, attn_w_qkv=m_attn_w_qkv,
               attn_q_gain=m_attn_q_gain, attn_k_gain=m_attn_k_gain, attn_sinks=m_attn_sinks, attn_w_o=m_attn_w_o,
               rel_bias=m_rel_bias, ffn_w_up=m_ffn_w_up, ffn_conv_w=m_ffn_conv_w, ffn_conv_b=m_ffn_conv_b,
               ffn_w_down=m_ffn_w_down)
    var = dict(mix_norm=v_mix_norm, ffn_norm=v_ffn_norm, sgu_w_in=v_sgu_w_in, sgu_v_gain=v_sgu_v_gain,
               sgu_w_s=v_sgu_w_s, sgu_b_s=v_sgu_b_s, sgu_w_out=v_sgu_w_out, attn_w_qkv=v_attn_w_qkv,
               attn_q_gain=v_attn_q_gain, attn_k_gain=v_attn_k_gain, attn_sinks=v_attn_sinks, attn_w_o=v_attn_w_o,
               rel_bias=v_rel_bias, ffn_w_up=v_ffn_w_up, ffn_conv_w=v_ffn_conv_w, ffn_conv_b=v_ffn_conv_b,
               ffn_w_down=v_ffn_w_down)
    chip = 2 * lax.axis_index("x") + lax.axis_index("y")
    core = lax.axis_index("c")

    where = jnp.stack([core, chip]).astype(jnp.int32)
    names = ["sgu_w_in", "sgu_w_out", "attn_w_qkv", "attn_w_o", "ffn_w_up0", "ffn_w_up1", "ffn_w_down0", "ffn_w_down1"]
    shards = [(sgu_w_in, 0), (sgu_w_out, 0), (attn_w_qkv, 0), (attn_w_o, 0), (ffn_w_up, 0), (ffn_w_up, 1),
              (ffn_w_down, 0), (ffn_w_down, 1)]
    T = {nm: _place_shard(s, l, where, BF16, "place_" + nm) for (s, l), nm in zip(shards, names)}
    for l in range(2):
        T["conv_w%d" % l] = _place_shard(ffn_conv_w, l, where, F32, "place_conv_w%d" % l)

    def gather(keys):
        return _gather_comm([T[k] for k in keys], [not k.startswith("conv") for k in keys])

    def gathered(keys, res):
        for k, a in zip(keys, res[0]):
            T[k] = a

    D = x.shape[2]
    first = ["sgu_w_in", "conv_w0", "conv_w1"]
    gathered(first, _run_comm(gather(first), "gather_first"))
    unshard_cols = lambda a: jnp.transpose(a, (1, 0, 2)).reshape(a.shape[1], -1)
    cw = [unshard_cols(T["conv_w0"]), unshard_cols(T["conv_w1"])]
    cb = ffn_conv_b
    flat = lambda k: T[k].reshape(-1, D)
    x2, target = x[0], loss_target[0]
    bucket = jnp.asarray(_rel_buckets_flat())
    wtril, wtrilT = _sgu_prep(sgu_w_s[0])
    bT = sgu_b_s[0].T
    bias = _rel_bias_expand(rel_bias, bucket).reshape(N_KV_HEADS, KV_GROUP * CHUNK, 2 * CHUNK)
    qg2 = jnp.tile(attn_q_gain, (1, 2))
    kg2 = jnp.tile(attn_k_gain, (1, 2))
    sinks = attn_sinks.reshape(N_HEADS)
    mix0, mix1 = mix_norm[0:1], mix_norm[1:2]
    fn0, fn1 = ffn_norm[0:1], ffn_norm[1:2]

    ks = ["sgu_w_out", "attn_w_qkv"]
    (hn0, z), r = _norm_matmul(x2, mix0, T["sgu_w_in"], "sgu_in", comm=gather(ks))
    gathered(ks, r)
    ks = ["ffn_w_up0"]
    (yp, h1, cdf), r = _sgu_fwd(z, x2, sgu_v_gain, wtril, bT, flat("sgu_w_out"), comm=gather(ks))
    gathered(ks, r)
    ks = ["ffn_w_down0", "attn_w_o"]
    (hf0, a0), r = _norm_matmul(h1, fn0, T["ffn_w_up0"], "ffn0_up", comm=gather(ks))
    gathered(ks, r)
    ks = ["ffn_w_up1"]
    (f0, h2), r = _ffn_fwd(a0, h1, cw[0], cb[0:1], flat("ffn_w_down0"), "ffn0_fwd", comm=gather(ks))
    gathered(ks, r)
    (hn1, qkv), _ = _norm_matmul(h2, mix1, T["attn_w_qkv"], "attn_qkv")
    ks = ["ffn_w_down1"]
    o, r = _attn_fwd(qkv, qg2, kg2, sinks, bias, comm=gather(ks))
    gathered(ks, r)
    h3, _ = _matmul_res(o, flat("attn_w_o"), h2, "attn_out")
    (hf1, a1), _ = _norm_matmul(h3, fn1, T["ffn_w_up1"], "ffn1_up")
    (f1, dh4, loss_local), _ = _ffn_fwd(a1, h3, cw[1], cb[1:2], flat("ffn_w_down1"), "ffn1_fwd", target=target)
    loss = lax.psum(loss_local[0, 0], ("x", "y", "c"))

    G, RS, PQ, QD, halves, grads = {}, {}, {}, {}, {}, {}
    dest_of = {"sgu_w_in": ("sgu_w_in", 0), "sgu_w_out": ("sgu_w_out", 0), "attn_w_qkv": ("attn_w_qkv", 0),
               "attn_w_o": ("attn_w_o", 0), "ffn_w_up0": ("ffn_w_up", 0), "ffn_w_up1": ("ffn_w_up", 1),
               "ffn_w_down0": ("ffn_w_down", 0), "ffn_w_down1": ("ffn_w_down", 1)}

    def px(keys):
        return _pair_exchange_comm([G[k] for k in keys])

    def px_done(keys, res):
        for k, a in zip(keys, res[1]):
            RS[k] = a
            PQ[k] = _pair_add(G[k], a, where, "pair_add_" + k)

    def sc(keys):
        return _chip_scatter_comm([PQ[k][0] for k in keys], [PQ[k][1] for k in keys])

    def sc_done(keys, res):
        for k, a in zip(keys, res[0]):
            wk, layer = dest_of[k]
            halves[wk] = _sum_chips(a, where, halves.get(wk), layer, w[wk].shape, "sum_chips_" + k)

    nup = ffn_w_up.shape[2]
    ndown = ffn_w_down.shape[1]
    rows4 = lambda a: a.reshape(4, ndown, D)
    (dc1, dcw1, dcb1), _ = _ffn_bwd_dc(dh4, a1, cw[1], cb[1:2], flat("ffn_w_down1"), "ffn1_bwd_dc")
    gw, _ = _matmul_tn(f1, dh4, "ffn1_dw_down", ka=2 * ndown, nb=D, out_dtype=BF16)
    G["ffn_w_down1"] = rows4(gw)
    (dh3, dfn1, da1), r = _bwd_norm(dc1, T["ffn_w_up1"], h3, fn1, dh4, "ffn1_bwd_in", conv_w=cw[1],
                                    comm=px(["ffn_w_down1"]))
    px_done(["ffn_w_down1"], r)
    G["ffn_w_up1"], r = _matmul_tn(hf1, da1, "ffn1_dw_up", ka=D, nb=nup, out_dtype=BF16, comm=sc(["ffn_w_down1"]))
    sc_done(["ffn_w_down1"], r)
    G["attn_w_o"], r = _matmul_tn(o, dh3, "attn_dw_o", ka=QW // 4, nb=D, out_dtype=BF16, comm=px(["ffn_w_up1"]))
    px_done(["ffn_w_up1"], r)
    do, r = _matmul_nt(dh3, flat("attn_w_o"), "attn_bwd_out", comm=px(["attn_w_o"]))
    px_done(["attn_w_o"], r)
    ks = ["ffn_w_up1", "attn_w_o"]
    (dqkv, dbias, dqg, dkg, dsink), r = _attn_bwd(qkv, do, qg2, kg2, sinks, bias, comm=sc(ks))
    sc_done(ks, r)
    G["attn_w_qkv"], _ = _matmul_tn(hn1, dqkv, "attn_dw_qkv", ka=D, nb=dqkv.shape[1] // 4, out_dtype=BF16)
    (dh2, dmix1), r = _bwd_norm(dqkv, T["attn_w_qkv"], h2, mix1, dh3, "attn_bwd_in", comm=px(["attn_w_qkv"]))
    px_done(["attn_w_qkv"], r)
    (dc0, dcw0, dcb0), r = _ffn_bwd_dc(dh2, a0, cw[0], cb[0:1], flat("ffn_w_down0"), "ffn0_bwd_dc",
                                       comm=sc(["attn_w_qkv"]))
    sc_done(["attn_w_qkv"], r)
    gw, _ = _matmul_tn(f0, dh2, "ffn0_dw_down", ka=2 * ndown, nb=D, out_dtype=BF16)
    G["ffn_w_down0"] = rows4(gw)
    (dh1, dfn0, da0), r = _bwd_norm(dc0, T["ffn_w_up0"], h1, fn0, dh2, "ffn0_bwd_in", conv_w=cw[0],
                                    comm=px(["ffn_w_down0"]))
    px_done(["ffn_w_down0"], r)
    G["ffn_w_up0"], r = _matmul_tn(hf0, da0, "ffn0_dw_up", ka=D, nb=nup, out_dtype=BF16, comm=sc(["ffn_w_down0"]))
    sc_done(["ffn_w_down0"], r)
    G["sgu_w_out"], r = _matmul_tn(yp, dh1, "sgu_dw_out", ka=yp.shape[1] // 4, nb=D, out_dtype=BF16,
                                   comm=px(["ffn_w_up0"]))
    px_done(["ffn_w_up0"], r)
    both = [sc(["ffn_w_up0"]), px(["sgu_w_out"])]
    (dz, dws, dbT, dvg), r = _sgu_bwd(dh1, z, cdf, sgu_v_gain, wtril, wtrilT, bT, flat("sgu_w_out"),
                                      comm=_join(both))
    r = _split(both, r)
    sc_done(["ffn_w_up0"], r[0])
    px_done(["sgu_w_out"], r[1])
    done = ["attn_w_qkv", "attn_w_o", "ffn_w_up", "ffn_w_down"]
    both = [sc(["sgu_w_out"]), _half_exchange_comm([halves[k] for k in done])]
    G["sgu_w_in"], r = _matmul_tn(hn0, dz, "sgu_dw_in", ka=D, nb=dz.shape[1] // 4, out_dtype=BF16, comm=_join(both))
    r = _split(both, r)
    sc_done(["sgu_w_out"], r[0])
    for k, a in zip(done, r[1][0]):
        grads[k] = a
    px_done(["sgu_w_in"], _run_comm(px(["sgu_w_in"]), "grad_last_pair_exchange"))
    (grad_x, dmix0), r = _bwd_norm(dz, T["sgu_w_in"], x2, mix0, dh1, "sgu_bwd_in", comm=sc(["sgu_w_in"]))
    sc_done(["sgu_w_in"], r)

    g = dict(mix_norm=jnp.concatenate([dmix0, dmix1], axis=0), ffn_norm=jnp.concatenate([dfn0, dfn1], axis=0),
             sgu_v_gain=dvg, sgu_w_s=dws, sgu_b_s=dbT.T, attn_q_gain=dqg[:, :HEAD_DIM], attn_k_gain=dkg[:, :HEAD_DIM],
             attn_sinks=dsink[:, :N_KV_HEADS].T.reshape(1, N_HEADS),
             rel_bias=_rel_bias_reduce(dbias.reshape(N_HEADS, CHUNK * 2 * CHUNK), bucket),
             ffn_conv_b=jnp.concatenate([dcb0, dcb1], axis=0))
    small_list = [g[k].reshape(w[k].shape) for k in SMALL] + [jnp.stack([dcw0, dcw1])]
    small_shapes = [a.shape for a in small_list]
    done = ["sgu_w_in", "sgu_w_out"]
    red, r = _allreduce_small(_pack(small_list, 8 * SMALL_PIECE_ROWS),
                              comm=_half_exchange_comm([halves[k] for k in done]))
    for k, a in zip(done, r[0]):
        grads[k] = a
    red = _unpack(red, small_shapes)
    for k, a in zip(SMALL, red[:-1]):
        grads[k] = a
    grads["ffn_conv_w"] = lax.dynamic_slice_in_dim(red[-1], chip * ffn_conv_w.shape[2], ffn_conv_w.shape[2], axis=2)

    delta, new_m, new_v = {}, {}, {}
    two = lambda a: a.reshape(-1, a.shape[-1])
    for k in BIG:
        (d2, m2, v2, g2), _ = _adamw(two(grads[k]), two(w[k]), two(mom[k]), two(var[k]), "adamw_" + k)
        delta[k], new_m[k], new_v[k], grads[k] = (a.reshape(w[k].shape) for a in (d2, m2, v2, g2))
    sm = SMALL + ["ffn_conv_w"]
    sm_shapes = [w[k].shape for k in sm]
    rows = sum(_rows_of(s) for s in sm_shapes)
    (d2, m2, v2, _), _ = _adamw(_pack([grads[k] for k in sm], rows), _pack([w[k] for k in sm], rows),
                                _pack([mom[k] for k in sm], rows), _pack([var[k] for k in sm], rows), "adamw_small")
    for dst, buf in ((delta, d2), (new_m, m2), (new_v, v2)):
        for k, a in zip(sm, _unpack(buf, sm_shapes)):
            dst[k] = a

    return (loss, grad_x[None], *[grads[k] for k in ORDER], *[delta[k] for k in ORDER],
            *[new_m[k] for k in ORDER], *[new_v[k] for k in ORDER])
```

```python
import functools
import math

import numpy as np
import jax
import jax.numpy as jnp
from jax import lax
from jax.experimental import pallas as pl
from jax.experimental.pallas import tpu as pltpu

F32 = jnp.float32
BF16 = jnp.bfloat16

EPS = 1e-6
CHUNK = 128
SGU_GROUPS = 16
HEAD_DIM = 64
N_HEADS = 16
N_KV_HEADS = 4
KV_GROUP = N_HEADS // N_KV_HEADS
REL_BUCKETS = 32
REL_MAX_DIST = 128
LANES = 128
HALO = 16

ADAM_LR = 0.001
ADAM_B1 = 0.9
ADAM_B2 = 0.999
ADAM_EPS = 1e-08
ADAM_WD = 0.01
ADAM_STEP = 10

VMEM_LIMIT_V7X = 56 * 1024 * 1024

_SQRT_HALF = math.sqrt(0.5)
_INV_SQRT_2PI = 1.0 / math.sqrt(2.0 * math.pi)


def _cp(sem):
    return pltpu.CompilerParams(dimension_semantics=sem, vmem_limit_bytes=VMEM_LIMIT_V7X)


def _resident(shape):
    nd = len(shape)
    return pl.BlockSpec(shape, lambda *_: (0,) * nd, pipeline_mode=pl.Buffered(1))


class _Comm:
    def __init__(self, srcs, zones, news, sems, start, finish, middle=None):
        self.srcs, self.zones, self.news, self.sems = list(srcs), list(zones), list(news), list(sems)
        self.start, self.finish = start, finish
        self.middle = middle if middle is not None else (lambda srcs, zones, news, sems: None)


def _join(comms):
    comms = [c for c in comms if c is not None]
    if not comms:
        return None

    def part(seq, attr):
        out, k = [], 0
        for c in comms:
            n = len(getattr(c, attr))
            out.append(seq[k:k + n])
            k += n
        return out

    def run(which):
        def f(srcs, zones, news, sems):
            for c, a, b, d, e in zip(comms, part(srcs, "srcs"), part(zones, "zones"), part(news, "news"), part(sems, "sems")):
                getattr(c, which)(a, b, d, e)
        return f

    cat = lambda attr: [v for c in comms for v in getattr(c, attr)]
    return _Comm(cat("srcs"), cat("zones"), cat("news"), cat("sems"), run("start"), run("finish"), run("middle"))


def _split(comms, res):
    zones, news = res
    out, kz, kn = [], 0, 0
    for c in comms:
        out.append((zones[kz:kz + len(c.zones)], news[kn:kn + len(c.news)]))
        kz += len(c.zones)
        kn += len(c.news)
    return out


def _call(body, *, name, grid, in_specs, out_specs, out_shape, args, scratch_shapes=(), sem=None, comm=None):
    if comm is None:
        res = pl.pallas_call(body, name=name, grid=grid, in_specs=in_specs, out_specs=out_specs, out_shape=out_shape,
                             scratch_shapes=list(scratch_shapes), compiler_params=_cp(sem))(*args)
        return res, None
    single = not isinstance(out_shape, (list, tuple))
    out_specs_l = [out_specs] if single else list(out_specs)
    out_shape_l = [out_shape] if single else list(out_shape)
    n_in, n_out, n_scr = len(in_specs), len(out_shape_l), len(scratch_shapes)
    ns, nz, nn = len(comm.srcs), len(comm.zones), len(comm.news)

    def wrapped(*refs):
        k = n_in
        ins, srcs = refs[:k], refs[k:k + ns]
        k += ns + nz
        outs, zones, news = refs[k:k + n_out], refs[k + n_out:k + n_out + nz], refs[k + n_out + nz:k + n_out + nz + nn]
        k += n_out + nz + nn
        scr, sems = refs[k:k + n_scr], refs[k + n_scr:]
        if not grid:
            comm.start(srcs, zones, news, sems)
            body(*ins, *outs, *scr)
            comm.middle(srcs, zones, news, sems)
            comm.finish(srcs, zones, news, sems)
            return
        first = functools.reduce(jnp.logical_and, [pl.program_id(a) == 0 for a in range(len(grid))])
        last = functools.reduce(jnp.logical_and, [pl.program_id(a) == grid[a] - 1 for a in range(len(grid))])
        early = len(grid) == 1 and grid[0] >= 2
        mid_step = grid[0] - (2 if grid[0] >= 8 else 1)

        @pl.when(first)
        def _():
            comm.start(srcs, zones, news, sems)

        if early:
            @pl.when(pl.program_id(0) == mid_step)
            def _():
                comm.middle(srcs, zones, news, sems)

        body(*ins, *outs, *scr)

        @pl.when(last)
        def _():
            if not early:
                comm.middle(srcs, zones, news, sems)
            comm.finish(srcs, zones, news, sems)

    res = pl.pallas_call(
        wrapped, name=name, grid=grid,
        in_specs=list(in_specs) + [ANY] * (ns + nz), out_specs=out_specs_l + [ANY] * (nz + nn),
        out_shape=out_shape_l + [jax.ShapeDtypeStruct(z.shape, z.dtype) for z in comm.zones] + comm.news,
        input_output_aliases={n_in + ns + i: n_out + i for i in range(nz)},
        scratch_shapes=list(scratch_shapes) + comm.sems,
        compiler_params=_cp(("arbitrary",) * len(grid)),
    )(*args, *comm.srcs, *comm.zones)
    main = res[0] if single else list(res[:n_out])
    return main, (list(res[n_out:n_out + nz]), list(res[n_out + nz:]))


def _run_comm(comm, name):
    ns, nz, nn = len(comm.srcs), len(comm.zones), len(comm.news)

    def body(*refs):
        srcs, zones, news, sems = refs[:ns], refs[ns + nz:ns + 2 * nz], refs[ns + 2 * nz:ns + 2 * nz + nn], refs[ns + 2 * nz + nn:]
        comm.start(srcs, zones, news, sems)
        comm.middle(srcs, zones, news, sems)
        comm.finish(srcs, zones, news, sems)

    res = pl.pallas_call(
        body, name=name, in_specs=[ANY] * (ns + nz), out_specs=[ANY] * (nz + nn),
        out_shape=[jax.ShapeDtypeStruct(z.shape, z.dtype) for z in comm.zones] + comm.news,
        input_output_aliases={ns + i: i for i in range(nz)}, scratch_shapes=comm.sems,
    )(*comm.srcs, *comm.zones)
    return list(res[:nz]), list(res[nz:])


def _dot(a, b):
    return jnp.dot(a, b, preferred_element_type=F32)


def _dot_nt(a, b):
    return lax.dot_general(a, b, (((1,), (1,)), ((), ())), preferred_element_type=F32)


def _dot_tn(a, b):
    return lax.dot_general(a, b, (((0,), (0,)), ((), ())), preferred_element_type=F32)


def _normal_cdf(x):
    return 0.5 * (1.0 + lax.erf(x * _SQRT_HALF))


def _gelu_and_grad(x, cdf):
    return x * cdf, cdf + x * jnp.exp(-0.5 * x * x) * _INV_SQRT_2PI


def _sigmoid(x):
    return 0.5 * jnp.tanh(0.5 * x) + 0.5


def _rms_bwd(dy, x, gain):
    r = lax.rsqrt(jnp.mean(x * x, axis=-1, keepdims=True) + EPS)
    xhat = x * r
    gdy = dy * gain
    dx = r * (gdy - xhat * jnp.mean(gdy * xhat, axis=-1, keepdims=True))
    return dx, jnp.sum(dy * xhat, axis=0, keepdims=True)


def _norm_matmul(x, gain, w4, name, comm=None, tm=1024):
    S, D = x.shape
    nsh, _, ns = w4.shape
    tm = min(tm, S)

    def body(x_ref, g_ref, w_ref, hn_ref, o_ref):
        xf = x_ref[...]
        r = lax.rsqrt(jnp.mean(xf * xf, axis=-1, keepdims=True) + EPS)
        hn = (xf * r * g_ref[...]).astype(BF16)
        hn_ref[...] = hn
        for j in range(nsh):
            o_ref[:, j * ns:(j + 1) * ns] = _dot(hn, w_ref[j]).astype(BF16)

    return _call(
        body, name=name, grid=(S // tm,),
        in_specs=[pl.BlockSpec((tm, D), lambda i: (i, 0)), _resident((1, D)), _resident(w4.shape)],
        out_specs=[pl.BlockSpec((tm, D), lambda i: (i, 0)), pl.BlockSpec((tm, nsh * ns), lambda i: (i, 0))],
        out_shape=[jax.ShapeDtypeStruct((S, D), BF16), jax.ShapeDtypeStruct((S, nsh * ns), BF16)],
        sem=("parallel",), args=(x, gain, w4), comm=comm)


def _matmul_res(a, w, res, name, comm=None, tm=1024):
    S, K = a.shape
    N = w.shape[1]
    tm = min(tm, S)

    def body(a_ref, w_ref, r_ref, o_ref):
        o_ref[...] = r_ref[...] + _dot(a_ref[...], w_ref[...])

    return _call(
        body, name=name, grid=(S // tm,),
        in_specs=[pl.BlockSpec((tm, K), lambda i: (i, 0)), _resident(w.shape), pl.BlockSpec((tm, N), lambda i: (i, 0))],
        out_specs=pl.BlockSpec((tm, N), lambda i: (i, 0)),
        out_shape=jax.ShapeDtypeStruct((S, N), F32),
        sem=("parallel",), args=(a, w, res), comm=comm)


def _matmul_nt(dh, w, name, comm=None, tm=1024):
    S, N = dh.shape
    K = w.shape[0]
    tm = min(tm, S)

    def body(d_ref, w_ref, o_ref):
        o_ref[...] = _dot_nt(d_ref[...].astype(BF16), w_ref[...]).astype(BF16)

    return _call(
        body, name=name, grid=(S // tm,),
        in_specs=[pl.BlockSpec((tm, N), lambda i: (i, 0)), _resident(w.shape)],
        out_specs=pl.BlockSpec((tm, K), lambda i: (i, 0)),
        out_shape=jax.ShapeDtypeStruct((S, K), BF16),
        sem=("parallel",), args=(dh, w), comm=comm)


def _matmul_tn(a, b, name, *, ka, nb, out_dtype, comm=None, ts=1024):
    S, KA = a.shape
    NB = b.shape[1]
    ts = min(ts, S)
    J = max(KA // ka, NB // nb)
    a_map = (lambda j, s: (s, j)) if KA // ka > 1 else (lambda j, s: (s, 0))
    b_map = (lambda j, s: (s, j)) if NB // nb > 1 else (lambda j, s: (s, 0))
    last = S // ts - 1

    def body(a_ref, b_ref, o_ref, acc):
        s = pl.program_id(1)

        @pl.when(s == 0)
        def _():
            acc[...] = jnp.zeros_like(acc)

        acc[...] += _dot_tn(a_ref[...].astype(BF16), b_ref[...].astype(BF16))

        @pl.when(s == last)
        def _():
            o_ref[0] = acc[...].astype(out_dtype)

    return _call(
        body, name=name, grid=(J, S // ts),
        in_specs=[pl.BlockSpec((ts, ka), a_map), pl.BlockSpec((ts, nb), b_map)],
        out_specs=pl.BlockSpec((1, ka, nb), lambda j, s: (j, 0, 0)),
        out_shape=jax.ShapeDtypeStruct((J, ka, nb), out_dtype),
        scratch_shapes=[pltpu.VMEM((ka, nb), F32)],
        sem=("parallel", "arbitrary"), args=(a, b), comm=comm)


def _sgu_prep(w_s):
    G = w_s.shape[0]

    def body(w_ref, t_ref, tt_ref):
        tri = lax.broadcasted_iota(jnp.int32, (CHUNK, CHUNK), 0) >= lax.broadcasted_iota(jnp.int32, (CHUNK, CHUNK), 1)
        for g in range(G):
            t = jnp.where(tri, w_ref[g], 0.0)
            t_ref[g] = t.astype(BF16)
            tt_ref[g] = t.T.astype(BF16)

    return pl.pallas_call(
        body, name="sgu_prep",
        out_shape=[jax.ShapeDtypeStruct(w_s.shape, BF16), jax.ShapeDtypeStruct(w_s.shape, BF16)],
        compiler_params=_cp(None),
    )(w_s)


def _sgu_fwd(z, x, vg, wtril, bT, wout, comm=None, tm=512):
    S = z.shape[0]
    W = z.shape[1] // 2
    D = x.shape[1]
    tm = min(tm, S)

    def body(z_ref, x_ref, vg_ref, wt_ref, bT_ref, wo_ref, yp_ref, h_ref, cdf_ref):
        def chunk(c, carry):
            r0 = pl.multiple_of(c * CHUNK, CHUNK)
            zc = z_ref[pl.ds(r0, CHUNK), :].astype(F32)
            cdf = _normal_cdf(zc)
            cdf_ref[pl.ds(r0, CHUNK), :] = cdf.astype(BF16)
            u = zc[:, :W] * cdf[:, :W]
            v = zc[:, W:] * cdf[:, W:]
            rv = lax.rsqrt(jnp.mean(v * v, axis=-1, keepdims=True) + EPS)
            vn = (v * rv * vg_ref[...]).astype(BF16)
            for g in range(SGU_GROUPS):
                sl = slice(g * LANES, (g + 1) * LANES)
                s = _dot(wt_ref[g], vn[:, sl]) + bT_ref[:, g:g + 1]
                yp_ref[pl.ds(r0, CHUNK), sl] = (u[:, sl] * s).astype(BF16)
            return carry

        lax.fori_loop(0, tm // CHUNK, chunk, 0)
        h_ref[...] = x_ref[...] + _dot(yp_ref[...], wo_ref[...])

    return _call(
        body, name="sgu_fwd", grid=(S // tm,),
        in_specs=[pl.BlockSpec((tm, 2 * W), lambda i: (i, 0)), pl.BlockSpec((tm, D), lambda i: (i, 0)),
                  _resident((1, W)), _resident(wtril.shape), _resident(bT.shape), _resident(wout.shape)],
        out_specs=[pl.BlockSpec((tm, W), lambda i: (i, 0)), pl.BlockSpec((tm, D), lambda i: (i, 0)),
                   pl.BlockSpec((tm, 2 * W), lambda i: (i, 0))],
        out_shape=[jax.ShapeDtypeStruct((S, W), BF16), jax.ShapeDtypeStruct((S, D), F32),
                   jax.ShapeDtypeStruct((S, 2 * W), BF16)],
        sem=("parallel",), args=(z, x, vg, wtril, bT, wout), comm=comm)


def _sgu_bwd(dh, z, cdf, vg, wtril, wtrilT, bT, wout, comm=None, tm=512):
    S = z.shape[0]
    W = z.shape[1] // 2
    D = dh.shape[1]
    G = SGU_GROUPS
    tm = min(tm, S)
    last = S // tm - 1

    def body(dh_ref, z_ref, cdf_ref, vg_ref, wt_ref, wtT_ref, bT_ref, wo_ref,
             dz_ref, dws_ref, dbT_ref, dvg_ref, dyp_s, du_s, dvn_s, dsacc):
        i = pl.program_id(0)

        @pl.when(i == 0)
        def _():
            dws_ref[...] = jnp.zeros_like(dws_ref)
            dvg_ref[...] = jnp.zeros_like(dvg_ref)
            dsacc[...] = jnp.zeros_like(dsacc)

        dyp_s[...] = _dot_nt(dh_ref[...].astype(BF16), wo_ref[...])
        tri = lax.broadcasted_iota(jnp.int32, (CHUNK, CHUNK), 0) >= lax.broadcasted_iota(jnp.int32, (CHUNK, CHUNK), 1)

        def chunk(c, carry):
            r0 = pl.multiple_of(c * CHUNK, CHUNK)
            zc = z_ref[pl.ds(r0, CHUNK), :].astype(F32)
            cdf = cdf_ref[pl.ds(r0, CHUNK), :].astype(F32)
            u, gu = _gelu_and_grad(zc[:, :W], cdf[:, :W])
            v, gv = _gelu_and_grad(zc[:, W:], cdf[:, W:])
            rv = lax.rsqrt(jnp.mean(v * v, axis=-1, keepdims=True) + EPS)
            vhat = v * rv
            vgain = vg_ref[...]
            vn = (vhat * vgain).astype(BF16)
            dyp = dyp_s[pl.ds(r0, CHUNK), :]
            for g in range(G):
                sl = slice(g * LANES, (g + 1) * LANES)
                vng = vn[:, sl]
                s = _dot(wt_ref[g], vng) + bT_ref[:, g:g + 1]
                ds = dyp[:, sl] * u[:, sl]
                du_s[:, sl] = dyp[:, sl] * s
                dsb = ds.astype(BF16)
                dvn_s[:, sl] = _dot(wtT_ref[g], dsb)
                dws_ref[g] += jnp.where(tri, _dot_nt(dsb, vng), 0.0)
                dsacc[g] += ds
            dvn = dvn_s[...]
            dvg_ref[...] += jnp.sum(dvn * vhat, axis=0, keepdims=True)
            gdy = dvn * vgain
            dv = rv * (gdy - vhat * jnp.mean(gdy * vhat, axis=-1, keepdims=True))
            dz_ref[pl.ds(r0, CHUNK), :W] = (du_s[...] * gu).astype(BF16)
            dz_ref[pl.ds(r0, CHUNK), W:] = (dv * gv).astype(BF16)
            return carry

        lax.fori_loop(0, tm // CHUNK, chunk, 0)

        @pl.when(i == last)
        def _():
            for g in range(G):
                dbT_ref[:, g:g + 1] = jnp.sum(dsacc[g], axis=1, keepdims=True)

    return _call(
        body, name="sgu_bwd", grid=(S // tm,),
        in_specs=[pl.BlockSpec((tm, D), lambda i: (i, 0)), pl.BlockSpec((tm, 2 * W), lambda i: (i, 0)),
                  pl.BlockSpec((tm, 2 * W), lambda i: (i, 0)),
                  _resident((1, W)), _resident(wtril.shape), _resident(wtrilT.shape), _resident(bT.shape),
                  _resident(wout.shape)],
        out_specs=[pl.BlockSpec((tm, 2 * W), lambda i: (i, 0)),
                   pl.BlockSpec((G, CHUNK, CHUNK), lambda i: (0, 0, 0)),
                   pl.BlockSpec((CHUNK, G), lambda i: (0, 0)),
                   pl.BlockSpec((1, W), lambda i: (0, 0))],
        out_shape=[jax.ShapeDtypeStruct((S, 2 * W), BF16), jax.ShapeDtypeStruct((G, CHUNK, CHUNK), F32),
                   jax.ShapeDtypeStruct((CHUNK, G), F32), jax.ShapeDtypeStruct((1, W), F32)],
        scratch_shapes=[pltpu.VMEM((tm, W), F32), pltpu.VMEM((CHUNK, W), F32), pltpu.VMEM((CHUNK, W), F32),
                        pltpu.VMEM((G, CHUNK, CHUNK), F32)],
        sem=("arbitrary",), args=(dh, z, cdf, vg, wtril, wtrilT, bT, wout), comm=comm)


def _conv_taps(a32, r0, R, reuse=False):
    base = r0 + HALO
    if not reuse:
        return a32[base:base + R, :], a32[base - 1:base - 1 + R, :], a32[base - 2:base - 2 + R, :]
    X = a32[base - 8:base + R, :]
    return X[8:], pltpu.roll(X, 1, 0)[8:], pltpu.roll(X, 2, 0)[8:]


def _ffn_fwd(a, h_in, cw, cb, wdown, name, comm=None, target=None, tm=512, R=64):
    S, C = a.shape
    F = C // 2
    D = h_in.shape[1]
    tm = min(tm, S)
    head = target is not None

    def body(a_ref, halo_ref, h_ref, cw_ref, cb_ref, wd_ref, *rest):
        if head:
            t_ref, f_ref, dy_ref, l_ref, a32 = rest
        else:
            f_ref, ho_ref, a32 = rest
        i = pl.program_id(0)
        a32[0:HALO, :] = jnp.where(i > 0, halo_ref[...].astype(F32), 0.0)
        a32[HALO:, :] = a_ref[...].astype(F32)

        for r0 in range(0, tm, R):
            a0, a1, a2 = _conv_taps(a32, r0, R)
            cpre = cw_ref[0:1, :] * a2 + cw_ref[1:2, :] * a1 + cw_ref[2:3, :] * a0 + cb_ref[...]
            g = cpre[:, :F]
            f_ref[r0:r0 + R, :] = (g * _sigmoid(g) * cpre[:, F:]).astype(BF16)
        h_out = h_ref[...] + _dot(f_ref[...], wd_ref[...])
        if not head:
            ho_ref[...] = h_out
            return

        @pl.when(i == 0)
        def _():
            l_ref[...] = jnp.zeros_like(l_ref)

        e = h_out - t_ref[...]
        dy_ref[...] = e * (1.0 / D)
        rows = jnp.sum(e * e, axis=-1, keepdims=True) * (1.0 / D)
        l_ref[...] += 0.5 * jnp.sum(rows, axis=0, keepdims=True)

    hb = tm // HALO
    row = pl.BlockSpec((tm, D), lambda i: (i, 0))
    in_specs = [pl.BlockSpec((tm, C), lambda i: (i, 0)),
                pl.BlockSpec((HALO, C), lambda i: (jnp.maximum(i * hb - 1, 0), 0)),
                row, _resident((3, C)), _resident((1, C)), _resident(wdown.shape)]
    out_specs = [pl.BlockSpec((tm, F), lambda i: (i, 0)), row]
    out_shape = [jax.ShapeDtypeStruct((S, F), BF16), jax.ShapeDtypeStruct((S, D), F32)]
    args = (a, a, h_in, cw, cb, wdown)
    if head:
        in_specs.append(row)
        out_specs.append(pl.BlockSpec((1, 1), lambda i: (0, 0)))
        out_shape.append(jax.ShapeDtypeStruct((1, 1), F32))
        args += (target,)
    return _call(
        body, name=name, grid=(S // tm,), in_specs=in_specs, out_specs=out_specs, out_shape=out_shape,
        scratch_shapes=[pltpu.VMEM((HALO + tm, C), F32)],
        sem=("arbitrary",) if head else ("parallel",), args=args, comm=comm)


def _ffn_bwd_dc(dh, a, cw, cb, wdown, name, comm=None, tm=256, R=64):
    S, C = a.shape
    F = C // 2
    D = dh.shape[1]
    tm = min(tm, S)

    def body(dh_ref, a_ref, halo_ref, cw_ref, cb_ref, wd_ref, dc_ref, dcw_ref, dcb_ref, a32, df_s, acc):
        i = pl.program_id(0)

        @pl.when(i == 0)
        def _():
            acc[...] = jnp.zeros_like(acc)

        a32[0:HALO, :] = jnp.where(i > 0, halo_ref[...].astype(F32), 0.0)
        a32[HALO:, :] = a_ref[...].astype(F32)
        df_s[...] = _dot_nt(dh_ref[...].astype(BF16), wd_ref[...])
        rows8 = lambda v: functools.reduce(jnp.add, [v[8 * k:8 * k + 8] for k in range(R // 8)])

        for r0 in range(0, tm, R):
            a0, a1, a2 = _conv_taps(a32, r0, R, reuse=True)
            cpre = cw_ref[0:1, :] * a2 + cw_ref[1:2, :] * a1 + cw_ref[2:3, :] * a0 + cb_ref[...]
            g = cpre[:, :F]
            val = cpre[:, F:]
            sg = _sigmoid(g)
            df = df_s[r0:r0 + R, :]
            dg = df * val * (sg * (1.0 + g * (1.0 - sg)))
            dval = df * (g * sg)
            dc = jnp.concatenate([dg, dval], axis=1)
            dc_ref[r0:r0 + R, :] = dc.astype(BF16)
            acc[0] += rows8(dc * a2)
            acc[1] += rows8(dc * a1)
            acc[2] += rows8(dc * a0)
            acc[3] += rows8(dc)

        @pl.when(i == S // tm - 1)
        def _():
            for k in range(3):
                dcw_ref[k:k + 1, :] = jnp.sum(acc[k], axis=0, keepdims=True)
            dcb_ref[...] = jnp.sum(acc[3], axis=0, keepdims=True)

    hb = tm // HALO
    return _call(
        body, name=name, grid=(S // tm,),
        in_specs=[pl.BlockSpec((tm, D), lambda i: (i, 0)),
                  pl.BlockSpec((tm, C), lambda i: (i, 0)),
                  pl.BlockSpec((HALO, C), lambda i: (jnp.maximum(i * hb - 1, 0), 0)),
                  _resident((3, C)), _resident((1, C)), _resident(wdown.shape)],
        out_specs=[pl.BlockSpec((tm, C), lambda i: (i, 0)),
                   pl.BlockSpec((3, C), lambda i: (0, 0)), pl.BlockSpec((1, C), lambda i: (0, 0))],
        out_shape=[jax.ShapeDtypeStruct((S, C), BF16), jax.ShapeDtypeStruct((3, C), F32),
                   jax.ShapeDtypeStruct((1, C), F32)],
        scratch_shapes=[pltpu.VMEM((HALO + tm, C), F32), pltpu.VMEM((tm, F), F32), pltpu.VMEM((4, 8, C), F32)],
        sem=("arbitrary",), args=(dh, a, a, cw, cb, wdown), comm=comm)


def _bwd_norm(dA, w4, h_in, gain, dh_out, name, conv_w=None, comm=None, R=64):
    S, N = dA.shape
    nsh, D, ns = w4.shape
    conv = conv_w is not None
    tm = min(256 if conv else 512, S)
    nt = S // tm

    def finish(src_ref, w_ref, h_ref, g_ref, dho_ref, dhi_ref, dg_ref):
        dhn = _dot_nt(src_ref[:, 0:ns], w_ref[0])
        for j in range(1, nsh):
            dhn += _dot_nt(src_ref[:, j * ns:(j + 1) * ns], w_ref[j])
        dx, dgain = _rms_bwd(dhn, h_ref[...], g_ref[...])
        dg_ref[...] += dgain
        dhi_ref[...] = dho_ref[...] + dx

    def body_plain(dA_ref, w_ref, h_ref, g_ref, dho_ref, dhi_ref, dg_ref):
        @pl.when(pl.program_id(0) == 0)
        def _():
            dg_ref[...] = jnp.zeros_like(dg_ref)

        finish(dA_ref, w_ref, h_ref, g_ref, dho_ref, dhi_ref, dg_ref)

    def body_conv(dc_ref, halo_ref, cw_ref, w_ref, h_ref, g_ref, dho_ref, dhi_ref, dg_ref, da_ref, dc32):
        i = pl.program_id(0)

        @pl.when(i == 0)
        def _():
            dg_ref[...] = jnp.zeros_like(dg_ref)

        dc32[0:tm, :] = dc_ref[...].astype(F32)
        dc32[tm:, :] = jnp.where(i < nt - 1, halo_ref[...].astype(F32), 0.0)

        for r0 in range(0, tm, R):
            d0 = dc32[r0:r0 + R, :]
            d1 = dc32[r0 + 1:r0 + 1 + R, :]
            d2 = dc32[r0 + 2:r0 + 2 + R, :]
            da = cw_ref[2:3, :] * d0 + cw_ref[1:2, :] * d1 + cw_ref[0:1, :] * d2
            da_ref[r0:r0 + R, :] = da.astype(BF16)
        finish(da_ref, w_ref, h_ref, g_ref, dho_ref, dhi_ref, dg_ref)

    row = lambda width: pl.BlockSpec((tm, width), lambda i: (i, 0))
    common_in = [_resident(w4.shape), row(D), _resident((1, D)), row(D)]
    common_out = [row(D), pl.BlockSpec((1, D), lambda i: (0, 0))]
    common_shape = [jax.ShapeDtypeStruct((S, D), F32), jax.ShapeDtypeStruct((1, D), F32)]
    if not conv:
        return _call(
            body_plain, name=name, grid=(nt,),
            in_specs=[row(N)] + common_in, out_specs=common_out, out_shape=common_shape,
            sem=("arbitrary",), args=(dA, w4, h_in, gain, dh_out), comm=comm)
    hb = tm // HALO
    nhb = S // HALO
    return _call(
        body_conv, name=name, grid=(nt,),
        in_specs=[row(N), pl.BlockSpec((HALO, N), lambda i: (jnp.minimum((i + 1) * hb, nhb - 1), 0)),
                  _resident((3, N))] + common_in,
        out_specs=common_out + [row(N)],
        out_shape=common_shape + [jax.ShapeDtypeStruct((S, N), BF16)],
        scratch_shapes=[pltpu.VMEM((tm + HALO, N), F32)],
        sem=("arbitrary",), args=(dA, dA, conv_w, w4, h_in, gain, dh_out), comm=comm)


def _rel_buckets_flat():
    q = np.arange(CHUNK)[:, None] + CHUNK
    k = np.arange(2 * CHUNK)[None, :]
    n = np.maximum(q - k, 0)
    max_exact = REL_BUCKETS // 2
    large = max_exact + (np.log(np.maximum(n, 1).astype(np.float32) / max_exact)
                         / math.log(REL_MAX_DIST / max_exact) * (REL_BUCKETS - max_exact)).astype(np.int32)
    large = np.minimum(large, REL_BUCKETS - 1)
    return np.where(n < max_exact, n, large).astype(np.int32).reshape(1, CHUNK * 2 * CHUNK)


def _split_bf16(x):
    hi = x.astype(BF16)
    return hi, (x - hi.astype(F32)).astype(BF16)


def _rel_bias_expand(rel_bias, bucket):
    B, H = rel_bias.shape
    n = bucket.shape[1]

    def body(rb_ref, bk_ref, o_ref):
        oh = (bk_ref[...] == lax.broadcasted_iota(jnp.int32, (B, n), 0)).astype(BF16)
        hi, lo = _split_bf16(rb_ref[...])
        o_ref[...] = _dot_tn(hi, oh) + _dot_tn(lo, oh)

    return pl.pallas_call(body, name="rel_bias_expand", out_shape=jax.ShapeDtypeStruct((H, n), F32),
                          compiler_params=_cp(None))(rel_bias, bucket)


def _rel_bias_reduce(dbias, bucket):
    H, n = dbias.shape
    B = REL_BUCKETS

    def body(db_ref, bk_ref, o_ref):
        oh = (bk_ref[...] == lax.broadcasted_iota(jnp.int32, (B, n), 0)).astype(BF16)
        hi, lo = _split_bf16(db_ref[...])
        o_ref[...] = _dot_nt(oh, hi) + _dot_nt(oh, lo)

    return pl.pallas_call(body, name="rel_bias_reduce", out_shape=jax.ShapeDtypeStruct((B, H), F32),
                          compiler_params=_cp(None))(dbias, bucket)


def _lo_mask(rows):
    return lax.broadcasted_iota(jnp.int32, (rows, LANES), 1) < HEAD_DIM


def _half_sums(y, lo):
    s_lo = jnp.sum(jnp.where(lo, y, 0.0), axis=-1, keepdims=True)
    s_hi = jnp.sum(jnp.where(lo, 0.0, y), axis=-1, keepdims=True)
    return jnp.where(lo, s_lo, s_hi)


def _half_rms(x, gain, lo):
    r = lax.rsqrt(_half_sums(x * x, lo) * (1.0 / HEAD_DIM) + EPS)
    xhat = x * r
    return xhat * gain, xhat, r


def _half_rms_bwd(dy, xhat, r, gain, lo):
    gdy = dy * gain
    dx = r * (gdy - xhat * (_half_sums(gdy * xhat, lo) * (1.0 / HEAD_DIM)))
    return dx, jnp.sum(dy * xhat, axis=0, keepdims=True)


def _dup_half(pair, e, lo):
    sw = pltpu.roll(pair, HEAD_DIM, 1)
    return jnp.where(lo, pair, sw) if e == 0 else jnp.where(lo, sw, pair)


def _band_valid(n):
    qi = lax.broadcasted_iota(jnp.int32, (KV_GROUP * CHUNK, 2 * CHUNK), 0) & (CHUNK - 1)
    ki = lax.broadcasted_iota(jnp.int32, (KV_GROUP * CHUNK, 2 * CHUNK), 1)
    dist = qi + CHUNK - ki
    return (dist >= 0) & (dist < CHUNK) & ((n > 0) | (ki >= CHUNK))


def _stack_heads(a, b, lo):
    return jnp.concatenate([jnp.where(lo, a, 0.0), jnp.where(lo, 0.0, a), jnp.where(lo, b, 0.0), jnp.where(lo, 0.0, b)],
                           axis=0)


def _unstack_heads(x4, lo):
    return (jnp.where(lo, x4[0:CHUNK], x4[CHUNK:2 * CHUNK]),
            jnp.where(lo, x4[2 * CHUNK:3 * CHUNK], x4[3 * CHUNK:]))


def _sink_col(sink_ref, hk):
    row = lax.broadcasted_iota(jnp.int32, (KV_GROUP * CHUNK, 1), 0)
    col = jnp.full((KV_GROUP * CHUNK, 1), sink_ref[KV_GROUP * hk + KV_GROUP - 1], F32)
    for r in range(KV_GROUP - 2, -1, -1):
        col = jnp.where(row < (r + 1) * CHUNK, sink_ref[KV_GROUP * hk + r], col)
    return col


def _softmax_sink(s, valid, sink):
    s = jnp.where(valid, s, -jnp.inf)
    m = jnp.maximum(jnp.max(s, axis=-1, keepdims=True), sink)
    p = jnp.exp(s - m)
    esink = jnp.exp(sink - m)
    inv = 1.0 / (jnp.sum(p, axis=-1, keepdims=True) + esink)
    return p * inv, esink * inv


QW = N_HEADS * HEAD_DIM
KVW = N_KV_HEADS * HEAD_DIM


def _attn_fwd(qkv, qg2, kg2, sinks, bias, comm=None):
    S = qkv.shape[0]
    nb = S // CHUNK

    def body(cur_ref, prev_ref, qg_ref, kg_ref, sink_ref, bias_ref, o_ref):
        n = pl.program_id(0)
        lo = _lo_mask(CHUNK)
        lo2 = _lo_mask(2 * CHUNK)
        valid = _band_valid(n)
        for j in range(N_KV_HEADS // 2):
            kc = slice(QW + j * LANES, QW + (j + 1) * LANES)
            vc = slice(QW + KVW + j * LANES, QW + KVW + (j + 1) * LANES)
            kpair = jnp.concatenate([prev_ref[:, j * LANES:(j + 1) * LANES], cur_ref[:, kc]], axis=0).astype(F32)
            vpair = jnp.concatenate([prev_ref[:, KVW + j * LANES:KVW + (j + 1) * LANES], cur_ref[:, vc]], axis=0).astype(F32)
            knpair, _, _ = _half_rms(kpair, kg_ref[...], lo2)
            for e in range(2):
                hk = 2 * j + e
                kdup = _dup_half(knpair, e, lo2).astype(BF16)
                vdup = _dup_half(vpair, e, lo2).astype(BF16)
                ca = slice(2 * hk * LANES, (2 * hk + 1) * LANES)
                cb = slice((2 * hk + 1) * LANES, (2 * hk + 2) * LANES)
                qna, _, _ = _half_rms(cur_ref[:, ca].astype(F32), qg_ref[...], lo)
                qnb, _, _ = _half_rms(cur_ref[:, cb].astype(F32), qg_ref[...], lo)
                qm4 = _stack_heads(qna, qnb, lo).astype(BF16)
                s = _dot_nt(qm4, kdup) * (HEAD_DIM ** -0.5) + bias_ref[hk]
                p, _ = _softmax_sink(s, valid, _sink_col(sink_ref, hk))
                oa, ob = _unstack_heads(_dot(p.astype(BF16), vdup), lo)
                o_ref[:, ca] = oa.astype(BF16)
                o_ref[:, cb] = ob.astype(BF16)

    return _call(
        body, name="attn_fwd", grid=(nb,),
        in_specs=[pl.BlockSpec((CHUNK, QW + 2 * KVW), lambda n: (n, 0)),
                  pl.BlockSpec((CHUNK, 2 * KVW), lambda n: (jnp.maximum(n - 1, 0), QW // (2 * KVW))),
                  _resident((1, LANES)), _resident((1, LANES)),
                  pl.BlockSpec(memory_space=pltpu.SMEM),
                  _resident(bias.shape)],
        out_specs=pl.BlockSpec((CHUNK, QW), lambda n: (n, 0)),
        out_shape=jax.ShapeDtypeStruct((S, QW), BF16),
        sem=("parallel",), args=(qkv, qkv, qg2, kg2, sinks, bias), comm=comm)


def _attn_bwd(qkv, do, qg2, kg2, sinks, bias, comm=None):
    S = qkv.shape[0]
    nb = S // CHUNK

    def body(cur_ref, prev_ref, do_ref, qg_ref, kg_ref, sink_ref, bias_ref,
             dqkv_ref, dbias_ref, dqg_ref, dkg_ref, dsink_ref, carry, band, dsacc, gacc):
        i = pl.program_id(0)
        n = nb - 1 - i
        lo = _lo_mask(CHUNK)
        lo2 = _lo_mask(2 * CHUNK)
        lane = lax.broadcasted_iota(jnp.int32, (KV_GROUP * CHUNK, LANES), 1)
        valid = _band_valid(n)

        @pl.when(i == 0)
        def _():
            dbias_ref[...] = jnp.zeros_like(dbias_ref)
            carry[...] = jnp.zeros_like(carry)
            dsacc[...] = jnp.zeros_like(dsacc)
            gacc[...] = jnp.zeros_like(gacc)

        qgain = qg_ref[...]
        kgain = kg_ref[...]
        for j in range(N_KV_HEADS // 2):
            kc = slice(QW + j * LANES, QW + (j + 1) * LANES)
            vc = slice(QW + KVW + j * LANES, QW + KVW + (j + 1) * LANES)
            kpair = jnp.concatenate([prev_ref[:, j * LANES:(j + 1) * LANES], cur_ref[:, kc]], axis=0).astype(F32)
            vpair = jnp.concatenate([prev_ref[:, KVW + j * LANES:KVW + (j + 1) * LANES], cur_ref[:, vc]], axis=0).astype(F32)
            knpair, khat, kr = _half_rms(kpair, kgain, lo2)
            dk_folds = []
            dv_folds = []
            for e in range(2):
                hk = 2 * j + e
                kdup = _dup_half(knpair, e, lo2).astype(BF16)
                vdup = _dup_half(vpair, e, lo2).astype(BF16)
                ca = slice(2 * hk * LANES, (2 * hk + 1) * LANES)
                cb = slice((2 * hk + 1) * LANES, (2 * hk + 2) * LANES)
                qna, qhata, qra = _half_rms(cur_ref[:, ca].astype(F32), qgain, lo)
                qnb, qhatb, qrb = _half_rms(cur_ref[:, cb].astype(F32), qgain, lo)
                qm4 = _stack_heads(qna, qnb, lo).astype(BF16)
                dom4 = _stack_heads(do_ref[:, ca].astype(F32), do_ref[:, cb].astype(F32), lo).astype(BF16)
                s = _dot_nt(qm4, kdup) * (HEAD_DIM ** -0.5) + bias_ref[hk]
                p, psink = _softmax_sink(s, valid, _sink_col(sink_ref, hk))
                dp = _dot_nt(dom4, vdup)
                delta = jnp.sum(p * dp, axis=-1, keepdims=True)
                ds = p * (dp - delta)
                dbias_ref[hk] += ds
                dsacc[...] += jnp.where(lane == hk, -(psink * delta), 0.0)
                dsr = (ds * (HEAD_DIM ** -0.5)).astype(BF16)
                dqna, dqnb = _unstack_heads(_dot(dsr, kdup), lo)
                dkd = _dot_tn(dsr, qm4)
                dvd = _dot_tn(p.astype(BF16), dom4)
                dqa, dqga = _half_rms_bwd(dqna, qhata, qra, qgain, lo)
                dqb, dqgb = _half_rms_bwd(dqnb, qhatb, qrb, qgain, lo)
                gacc[0:1, :] += dqga + dqgb
                dqkv_ref[:, ca] = dqa.astype(BF16)
                dqkv_ref[:, cb] = dqb.astype(BF16)
                dk_folds.append(dkd + pltpu.roll(dkd, HEAD_DIM, 1))
                dv_folds.append(dvd + pltpu.roll(dvd, HEAD_DIM, 1))
            dkn = jnp.where(lo2, dk_folds[0], dk_folds[1])
            dk, dkg = _half_rms_bwd(dkn, khat, kr, kgain, lo2)
            gacc[1:2, :] += dkg
            band[:, j * LANES:(j + 1) * LANES] = dk
            band[:, KVW + j * LANES:KVW + (j + 1) * LANES] = jnp.where(lo2, dv_folds[0], dv_folds[1])
        dqkv_ref[:, QW:] = (band[CHUNK:, :] + carry[...]).astype(BF16)
        carry[...] = band[0:CHUNK, :]

        @pl.when(i == nb - 1)
        def _():
            g = gacc[...]
            g = g + pltpu.roll(g, HEAD_DIM, 1)
            dqg_ref[...] = g[0:1, :]
            dkg_ref[...] = g[1:2, :]
            for r in range(KV_GROUP):
                dsink_ref[r:r + 1, :] = jnp.sum(dsacc[r * CHUNK:(r + 1) * CHUNK, :], axis=0, keepdims=True)

    vec = pl.BlockSpec((1, LANES), lambda i: (0, 0))
    return _call(
        body, name="attn_bwd", grid=(nb,),
        in_specs=[pl.BlockSpec((CHUNK, QW + 2 * KVW), lambda i: (nb - 1 - i, 0)),
                  pl.BlockSpec((CHUNK, 2 * KVW), lambda i: (jnp.maximum(nb - 2 - i, 0), QW // (2 * KVW))),
                  pl.BlockSpec((CHUNK, QW), lambda i: (nb - 1 - i, 0)),
                  _resident((1, LANES)), _resident((1, LANES)),
                  pl.BlockSpec(memory_space=pltpu.SMEM),
                  _resident(bias.shape)],
        out_specs=[pl.BlockSpec((CHUNK, QW + 2 * KVW), lambda i: (nb - 1 - i, 0)),
                   pl.BlockSpec(bias.shape, lambda i: (0, 0, 0)), vec, vec,
                   pl.BlockSpec((KV_GROUP, LANES), lambda i: (0, 0))],
        out_shape=[jax.ShapeDtypeStruct((S, QW + 2 * KVW), BF16),
                   jax.ShapeDtypeStruct(bias.shape, F32),
                   jax.ShapeDtypeStruct((1, LANES), F32), jax.ShapeDtypeStruct((1, LANES), F32),
                   jax.ShapeDtypeStruct((KV_GROUP, LANES), F32)],
        scratch_shapes=[pltpu.VMEM((CHUNK, 2 * KVW), F32), pltpu.VMEM((2 * CHUNK, 2 * KVW), F32),
                        pltpu.VMEM((KV_GROUP * CHUNK, LANES), F32), pltpu.VMEM((8, LANES), F32)],
        sem=("arbitrary",), args=(qkv, qkv, do, qg2, kg2, sinks, bias), comm=comm)


def _row_tile(rows, cols, n_arrays):
    budget = VMEM_LIMIT_V7X // 4 // (2 * n_arrays * 4 * cols)
    best = 8
    for n in range(1, rows // 8 + 1):
        if rows % n == 0 and (rows // n) % 8 == 0 and rows // n <= budget:
            best = rows // n
            break
    return best


def _adamw(g, w, m, v, name, comm=None):
    R, C = g.shape
    tr = _row_tile(R, C, 8)

    def body(g_ref, w_ref, m_ref, v_ref, d_ref, mo_ref, vo_ref, go_ref):
        gg = g_ref[...]
        go_ref[...] = gg
        mn = ADAM_B1 * m_ref[...] + (1.0 - ADAM_B1) * gg
        vn = ADAM_B2 * v_ref[...] + (1.0 - ADAM_B2) * jnp.square(gg)
        m_hat = mn / (1.0 - ADAM_B1 ** ADAM_STEP)
        v_hat = vn / (1.0 - ADAM_B2 ** ADAM_STEP)
        d_ref[...] = -ADAM_LR * (m_hat / (jnp.sqrt(v_hat) + ADAM_EPS) + ADAM_WD * w_ref[...])
        mo_ref[...] = mn
        vo_ref[...] = vn

    spec = pl.BlockSpec((tr, C), lambda i: (i, 0))
    return _call(
        body, name=name, grid=(R // tr,), in_specs=[spec] * 4, out_specs=[spec] * 4,
        out_shape=[jax.ShapeDtypeStruct((R, C), F32)] * 4, sem=("parallel",), args=(g, w, m, v), comm=comm)


def _place_shard(shards, layer, where, dtype, name):
    _, R, C = shards.shape
    tr = _row_tile(R, C, 2) if R % 8 == 0 else R

    def body(s_ref, x_ref, o_ref):
        o_ref[...] = x_ref[...].astype(dtype)

    return pl.pallas_call(
        body, name=name,
        grid_spec=pltpu.PrefetchScalarGridSpec(
            num_scalar_prefetch=1, grid=(R // tr,),
            in_specs=[pl.BlockSpec((1, tr, C), lambda i, s_ref: (layer, i, 0))],
            out_specs=pl.BlockSpec((1, tr, C), lambda i, s_ref: (s_ref[1], i, 0))),
        out_shape=jax.ShapeDtypeStruct((4, R, C), dtype),
        compiler_params=_cp(("parallel",)),
    )(where, shards)


def _pair_add(g4, rsib, where, name):
    J, R, C = g4.shape
    Rh = R // 2
    tr = _row_tile(Rh, C, 4)
    g5 = g4.reshape(J, 2, Rh, C)

    def body(s_ref, g_ref, r_ref, p_ref, q_ref):
        val = (g_ref[...].astype(F32)[0] + r_ref[...].astype(F32)).astype(BF16)
        p_ref[...] = val

        @pl.when(pl.program_id(1) == s_ref[1])
        def _():
            q_ref[...] = val

    return pl.pallas_call(
        body, name=name,
        grid_spec=pltpu.PrefetchScalarGridSpec(
            num_scalar_prefetch=1, grid=(Rh // tr, J),
            in_specs=[pl.BlockSpec((1, 1, tr, C), lambda i, j, s_ref: (j, s_ref[0], i, 0)),
                      pl.BlockSpec((1, tr, C), lambda i, j, s_ref: (j, i, 0))],
            out_specs=[pl.BlockSpec((1, tr, C), lambda i, j, s_ref: (j, i, 0)),
                       pl.BlockSpec((1, tr, C), lambda i, j, s_ref: (s_ref[1], i, 0))]),
        out_shape=[jax.ShapeDtypeStruct((J, Rh, C), BF16)] * 2,
        compiler_params=_cp(("parallel", "arbitrary")),
    )(where, g5, rsib)


def _sum_chips(q, where, dest, layer, out_shape, name):
    J, Rh, C = q.shape
    tr = _row_tile(Rh, C, 3)
    nb = Rh // tr

    def body(s_ref, q_ref, *rest):
        qq = q_ref[...].astype(F32)
        rest[-1][0] = ((qq[0] + qq[1]) + qq[2]) + qq[3]

    have = dest is not None
    return pl.pallas_call(
        body, name=name,
        grid_spec=pltpu.PrefetchScalarGridSpec(
            num_scalar_prefetch=1, grid=(nb,),
            in_specs=[pl.BlockSpec((J, tr, C), lambda i, s_ref: (0, i, 0))] + ([ANY] if have else []),
            out_specs=pl.BlockSpec((1, tr, C), lambda i, s_ref: (layer, s_ref[0] * nb + i, 0))),
        out_shape=jax.ShapeDtypeStruct(out_shape, F32),
        input_output_aliases={2: 0} if have else {},
        compiler_params=_cp(("parallel",)),
    )(*((where, q, dest) if have else (where, q)))


MESH = pl.DeviceIdType.MESH
ANY = pl.BlockSpec(memory_space=pl.ANY)


def _place():
    x, y, c = lax.axis_index("x"), lax.axis_index("y"), lax.axis_index("c")
    others = [(1 - x, y), (x, 1 - y), (1 - x, 1 - y)]
    return x, y, c, 2 * x + y, others, [2 * ox + oy for ox, oy in others]


def _gather_comm(items):
    n = len(items)
    placed = [it[0] for it in items]
    split = [it[3] for it in items]

    def rows(t, ref, half):
        _, lo, hi, _ = items[t]
        if not split[t]:
            return ref if (lo, hi) == (0, placed[t].shape[1]) else ref.at[pl.ds(lo, hi - lo), :]
        rh = (hi - lo) // 2
        return ref.at[pl.ds(lo + half * rh, rh), :]

    def sends(outs, sems):
        send, recv = sems[0], sems[1]
        x, y, c, me, others, okey = _place()
        cps = []
        for t in range(n):
            mine = rows(t, outs[t].at[me], c)
            for j, (ox, oy) in enumerate(others):
                cps.append(pltpu.make_async_remote_copy(
                    src_ref=mine, dst_ref=mine,
                    send_sem=send.at[t, j], recv_sem=recv.at[t, j], device_id=(ox, oy, c), device_id_type=MESH))
        return cps

    def start(srcs, outs, news, sems):
        for cp in sends(outs, sems):
            cp.start()

    def forwards(outs, sems):
        fsend, frecv = sems[2], sems[3]
        x, y, c, me, others, okey = _place()
        cps = []
        for t in range(n):
            if split[t]:
                for j in range(3):
                    landed = rows(t, outs[t].at[okey[j]], c)
                    cps.append(pltpu.make_async_remote_copy(
                        src_ref=landed, dst_ref=landed, send_sem=fsend.at[t, j], recv_sem=frecv.at[t, j],
                        device_id=(x, y, 1 - c), device_id_type=MESH))
        return cps

    def middle(srcs, outs, news, sems):
        send, recv = sems[0], sems[1]
        x, y, c, me, others, okey = _place()
        for t in range(n):
            for j in range(3):
                landed = rows(t, outs[t].at[okey[j]], c)
                pltpu.make_async_remote_copy(
                    src_ref=landed, dst_ref=landed, send_sem=send.at[t, j], recv_sem=recv.at[t, j],
                    device_id=(x, y, 1 - c), device_id_type=MESH).wait_recv()
        for cp in forwards(outs, sems):
            cp.start()

    def finish(srcs, outs, news, sems):
        fsend, frecv = sems[2], sems[3]
        x, y, c, me, others, okey = _place()
        for t in range(n):
            if split[t]:
                for j in range(3):
                    theirs = rows(t, outs[t].at[okey[j]], 1 - c)
                    pltpu.make_async_remote_copy(
                        src_ref=theirs, dst_ref=theirs, send_sem=fsend.at[t, j], recv_sem=frecv.at[t, j],
                        device_id=(x, y, 1 - c), device_id_type=MESH).wait_recv()
        for cp in sends(outs, sems) + forwards(outs, sems):
            cp.wait_send()

    return _Comm([], placed, [], [pltpu.SemaphoreType.DMA((n, 3))] * 4, start, finish, middle)


def _pair_exchange_comm(gs):
    n = len(gs)

    def copies(ins, outs, sems):
        send, recv = sems
        x, y, c, _, _, _ = _place()
        cps = []
        for t in range(n):
            rh = gs[t].shape[1] // 2
            cps.append(pltpu.make_async_remote_copy(
                src_ref=ins[t].at[:, pl.ds((1 - c) * rh, rh), :], dst_ref=outs[t],
                send_sem=send.at[t], recv_sem=recv.at[t], device_id=(x, y, 1 - c), device_id_type=MESH))
        return cps

    def start(ins, zones, outs, sems):
        for cp in copies(ins, outs, sems):
            cp.start()

    def finish(ins, zones, outs, sems):
        for cp in copies(ins, outs, sems):
            cp.wait()

    news = [jax.ShapeDtypeStruct((4, g.shape[1] // 2, g.shape[2]), g.dtype) for g in gs]
    return _Comm(gs, [], news, [pltpu.SemaphoreType.DMA((n,))] * 2, start, finish)


def _chip_scatter_comm(ps, qs):
    n = len(ps)

    def sends(ins, outs, sems):
        send, recv = sems
        x, y, c, me, others, okey = _place()
        return [pltpu.make_async_remote_copy(
            src_ref=ins[t].at[okey[j]], dst_ref=outs[t].at[me],
            send_sem=send.at[t, j], recv_sem=recv.at[t, j], device_id=(ox, oy, c), device_id_type=MESH)
            for t in range(n) for j, (ox, oy) in enumerate(others)]

    def start(ins, outs, news, sems):
        for cp in sends(ins, outs, sems):
            cp.start()

    def finish(ins, outs, news, sems):
        send, recv = sems
        x, y, c, me, others, okey = _place()
        for t in range(n):
            for j in range(3):
                slot = outs[t].at[okey[j]]
                pltpu.make_async_remote_copy(
                    src_ref=slot, dst_ref=slot, send_sem=send.at[t, j], recv_sem=recv.at[t, j],
                    device_id=(x, y, c), device_id_type=MESH).wait_recv()
        for cp in sends(ins, outs, sems):
            cp.wait_send()

    return _Comm(ps, qs, [], [pltpu.SemaphoreType.DMA((n, 3))] * 2, start, finish)


def _half_exchange_comm(arrs, layers=None):
    n = len(arrs)
    items = [(t, layer) for t in range(n) for layer in (range(arrs[t].shape[0]) if layers is None else layers[t])]

    def sends(outs, sems):
        send, recv = sems
        x, y, c, _, _, _ = _place()
        cps = []
        for k, (t, layer) in enumerate(items):
            rh = arrs[t].shape[1] // 2
            mine = outs[t].at[layer, pl.ds(c * rh, rh), :]
            cps.append(pltpu.make_async_remote_copy(
                src_ref=mine, dst_ref=mine, send_sem=send.at[k], recv_sem=recv.at[k],
                device_id=(x, y, 1 - c), device_id_type=MESH))
        return cps

    def start(srcs, outs, news, sems):
        for cp in sends(outs, sems):
            cp.start()

    def finish(srcs, outs, news, sems):
        send, recv = sems
        x, y, c, _, _, _ = _place()
        for k, (t, layer) in enumerate(items):
            rh = arrs[t].shape[1] // 2
            theirs = outs[t].at[layer, pl.ds((1 - c) * rh, rh), :]
            pltpu.make_async_remote_copy(
                src_ref=theirs, dst_ref=theirs, send_sem=send.at[k], recv_sem=recv.at[k],
                device_id=(x, y, 1 - c), device_id_type=MESH).wait_recv()
        for cp in sends(outs, sems):
            cp.wait_send()

    return _Comm([], arrs, [], [pltpu.SemaphoreType.DMA((len(items),))] * 2, start, finish)


SMALL_COLS = 1024
SMALL_PIECE_ROWS = 48


def _allreduce_small(buf, comm=None):
    pr = SMALL_PIECE_ROWS
    flips = [(d >> 2 & 1, d >> 1 & 1, d & 1) for d in range(1, 8)]

    def body(x_ref, o_ref, rbuf, send1, recv1, send2, recv2):
        x, y, c = lax.axis_index("x"), lax.axis_index("y"), lax.axis_index("c")
        me = 4 * x + 2 * y + c
        peers = [(x ^ fx, y ^ fy, c ^ fc) for fx, fy, fc in flips]
        pid = [4 * px + 2 * py + pc for px, py, pc in peers]

        def piece(ref, p):
            return ref.at[pl.ds(pl.multiple_of(p * pr, 8), pr), :]

        cps = []
        for d in range(7):
            cp = pltpu.make_async_remote_copy(
                src_ref=piece(x_ref, pid[d]), dst_ref=rbuf.at[d + 1],
                send_sem=send1.at[d], recv_sem=recv1.at[d], device_id=peers[d], device_id_type=MESH)
            cp.start()
            cps.append(cp)
        acc = piece(x_ref, me)[...]
        for d in range(7):
            cps[d].wait_recv()
            acc = acc + rbuf[d + 1]
        piece(o_ref, me)[...] = acc
        out = []
        for d in range(7):
            cp = pltpu.make_async_remote_copy(
                src_ref=piece(o_ref, me), dst_ref=piece(o_ref, me),
                send_sem=send2.at[d], recv_sem=recv2.at[d], device_id=peers[d], device_id_type=MESH)
            cp.start()
            out.append(cp)
        for d in range(7):
            pltpu.make_async_remote_copy(
                src_ref=piece(o_ref, pid[d]), dst_ref=piece(o_ref, pid[d]),
                send_sem=send2.at[d], recv_sem=recv2.at[d], device_id=peers[d], device_id_type=MESH).wait_recv()
        for cp in cps + out:
            cp.wait_send()

    vm = pl.BlockSpec(memory_space=pltpu.VMEM)
    return _call(
        body, name="small_allreduce", grid=(), in_specs=[vm], out_specs=vm,
        out_shape=jax.ShapeDtypeStruct(buf.shape, F32),
        scratch_shapes=[pltpu.VMEM((8, pr, SMALL_COLS), F32)] + [pltpu.SemaphoreType.DMA((7,))] * 4,
        args=(buf,), comm=comm)


def _rows_of(shape):
    return -(-math.prod(shape) // (8 * SMALL_COLS)) * 8


def _pack(arrays, rows):
    parts = []
    for a in arrays:
        r = _rows_of(a.shape)
        parts.append(jnp.pad(a.reshape(-1), (0, r * SMALL_COLS - a.size)).reshape(r, SMALL_COLS))
    used = sum(p.shape[0] for p in parts)
    if rows > used:
        parts.append(jnp.zeros((rows - used, SMALL_COLS), F32))
    return jnp.concatenate(parts, axis=0)


def _unpack(buf, shapes):
    out, off = [], 0
    for s in shapes:
        r = _rows_of(s)
        out.append(buf[off:off + r].reshape(-1)[:math.prod(s)].reshape(s))
        off += r
    return out


BIG = ["sgu_w_in", "sgu_w_out", "attn_w_qkv", "attn_w_o", "ffn_w_up", "ffn_w_down"]
SMALL = ["mix_norm", "ffn_norm", "sgu_v_gain", "sgu_w_s", "sgu_b_s", "attn_q_gain", "attn_k_gain", "attn_sinks",
         "rel_bias", "ffn_conv_b"]
ORDER = ["mix_norm", "ffn_norm", "sgu_w_in", "sgu_v_gain", "sgu_w_s", "sgu_b_s", "sgu_w_out", "attn_w_qkv",
         "attn_q_gain", "attn_k_gain", "attn_sinks", "attn_w_o", "rel_bias", "ffn_w_up", "ffn_conv_w", "ffn_conv_b",
         "ffn_w_down"]


def kernel(x, mix_norm, ffn_norm, sgu_w_in, sgu_v_gain, sgu_w_s, sgu_b_s, sgu_w_out, attn_w_qkv, attn_q_gain, attn_k_gain, attn_sinks, attn_w_o, rel_bias, ffn_w_up, ffn_conv_w, ffn_conv_b, ffn_w_down, loss_target, m_mix_norm, m_ffn_norm, m_sgu_w_in, m_sgu_v_gain, m_sgu_w_s, m_sgu_b_s, m_sgu_w_out, m_attn_w_qkv, m_attn_q_gain, m_attn_k_gain, m_attn_sinks, m_attn_w_o, m_rel_bias, m_ffn_w_up, m_ffn_conv_w, m_ffn_conv_b, m_ffn_w_down, v_mix_norm, v_ffn_norm, v_sgu_w_in, v_sgu_v_gain, v_sgu_w_s, v_sgu_b_s, v_sgu_w_out, v_attn_w_qkv, v_attn_q_gain, v_attn_k_gain, v_attn_sinks, v_attn_w_o, v_rel_bias, v_ffn_w_up, v_ffn_conv_w, v_ffn_conv_b, v_ffn_w_down):
    w = dict(mix_norm=mix_norm, ffn_norm=ffn_norm, sgu_w_in=sgu_w_in, sgu_v_gain=sgu_v_gain, sgu_w_s=sgu_w_s,
             sgu_b_s=sgu_b_s, sgu_w_out=sgu_w_out, attn_w_qkv=attn_w_qkv, attn_q_gain=attn_q_gain,
             attn_k_gain=attn_k_gain, attn_sinks=attn_sinks, attn_w_o=attn_w_o, rel_bias=rel_bias, ffn_w_up=ffn_w_up,
             ffn_conv_w=ffn_conv_w, ffn_conv_b=ffn_conv_b, ffn_w_down=ffn_w_down)
    mom = dict(mix_norm=m_mix_norm, ffn_norm=m_ffn_norm, sgu_w_in=m_sgu_w_in, sgu_v_gain=m_sgu_v_gain,
               sgu_w_s=m_sgu_w_s, sgu_b_s=m_sgu_b_s, sgu_w_out=m_sgu_w_out, attn_w_qkv=m_attn_w_qkv,
               attn_q_gain=m_attn_q_gain, attn_k_gain=m_attn_k_gain, attn_sinks=m_attn_sinks, attn_w_o=m_attn_w_o,
               rel_bias=m_rel_bias, ffn_w_up=m_ffn_w_up, ffn_conv_w=m_ffn_conv_w, ffn_conv_b=m_ffn_conv_b,
               ffn_w_down=m_ffn_w_down)
    var = dict(mix_norm=v_mix_norm, ffn_norm=v_ffn_norm, sgu_w_in=v_sgu_w_in, sgu_v_gain=v_sgu_v_gain,
               sgu_w_s=v_sgu_w_s, sgu_b_s=v_sgu_b_s, sgu_w_out=v_sgu_w_out, attn_w_qkv=v_attn_w_qkv,
               attn_q_gain=v_attn_q_gain, attn_k_gain=v_attn_k_gain, attn_sinks=v_attn_sinks, attn_w_o=v_attn_w_o,
               rel_bias=v_rel_bias, ffn_w_up=v_ffn_w_up, ffn_conv_w=v_ffn_conv_w, ffn_conv_b=v_ffn_conv_b,
               ffn_w_down=v_ffn_w_down)
    chip = 2 * lax.axis_index("x") + lax.axis_index("y")
    core = lax.axis_index("c")

    where = jnp.stack([core, chip]).astype(jnp.int32)
    names = ["sgu_w_in", "sgu_w_out", "attn_w_qkv", "attn_w_o", "ffn_w_up0", "ffn_w_up1", "ffn_w_down0", "ffn_w_down1"]
    shards = [(sgu_w_in, 0), (sgu_w_out, 0), (attn_w_qkv, 0), (attn_w_o, 0), (ffn_w_up, 0), (ffn_w_up, 1),
              (ffn_w_down, 0), (ffn_w_down, 1)]
    T = {nm: _place_shard(s, l, where, BF16, "place_" + nm) for (s, l), nm in zip(shards, names)}
    for l in range(2):
        T["conv_w%d" % l] = _place_shard(ffn_conv_w, l, where, F32, "place_conv_w%d" % l)

    def gather(keys):
        items = []
        for k in keys:
            k, lo, hi = (k, 0, None) if isinstance(k, str) else k
            items.append((T[k], lo, T[k].shape[1] if hi is None else hi, not k.startswith("conv")))
        return _gather_comm(items)

    def gathered(keys, res):
        for k, a in zip(keys, res[0]):
            T[k if isinstance(k, str) else k[0]] = a

    D = x.shape[2]
    first = ["sgu_w_in", "conv_w0", "conv_w1"]
    gathered(first, _run_comm(gather(first), "gather_first"))
    unshard_cols = lambda a: jnp.transpose(a, (1, 0, 2)).reshape(a.shape[1], -1)
    cw = [unshard_cols(T["conv_w0"]), unshard_cols(T["conv_w1"])]
    cb = ffn_conv_b
    flat = lambda k: T[k].reshape(-1, D)
    x2, target = x[0], loss_target[0]
    bucket = jnp.asarray(_rel_buckets_flat())
    wtril, wtrilT = _sgu_prep(sgu_w_s[0])
    bT = sgu_b_s[0].T
    bias = _rel_bias_expand(rel_bias, bucket).reshape(N_KV_HEADS, KV_GROUP * CHUNK, 2 * CHUNK)
    qg2 = jnp.tile(attn_q_gain, (1, 2))
    kg2 = jnp.tile(attn_k_gain, (1, 2))
    sinks = attn_sinks.reshape(N_HEADS)
    mix0, mix1 = mix_norm[0:1], mix_norm[1:2]
    fn0, fn1 = ffn_norm[0:1], ffn_norm[1:2]

    ks = ["sgu_w_out", ("ffn_w_up0", 0, 256)]
    (hn0, z), r = _norm_matmul(x2, mix0, T["sgu_w_in"], "sgu_in", comm=gather(ks))
    gathered(ks, r)
    ks = [("ffn_w_up0", 256, None)]
    (yp, h1, cdf), r = _sgu_fwd(z, x2, sgu_v_gain, wtril, bT, flat("sgu_w_out"), comm=gather(ks))
    gathered(ks, r)
    ks = ["ffn_w_down0", "attn_w_qkv"]
    (hf0, a0), r = _norm_matmul(h1, fn0, T["ffn_w_up0"], "ffn0_up", comm=gather(ks))
    gathered(ks, r)
    ks = ["attn_w_o", ("ffn_w_up1", 0, 640)]
    (f0, h2), r = _ffn_fwd(a0, h1, cw[0], cb[0:1], flat("ffn_w_down0"), "ffn0_fwd", comm=gather(ks))
    gathered(ks, r)
    (hn1, qkv), _ = _norm_matmul(h2, mix1, T["attn_w_qkv"], "attn_qkv")
    ks = [("ffn_w_up1", 640, None), "ffn_w_down1"]
    o, r = _attn_fwd(qkv, qg2, kg2, sinks, bias, comm=gather(ks))
    gathered(ks, r)
    h3, _ = _matmul_res(o, flat("attn_w_o"), h2, "attn_out")
    (hf1, a1), _ = _norm_matmul(h3, fn1, T["ffn_w_up1"], "ffn1_up")
    (f1, dh4, loss_local), _ = _ffn_fwd(a1, h3, cw[1], cb[1:2], flat("ffn_w_down1"), "ffn1_fwd", target=target)
    loss = lax.psum(loss_local[0, 0], ("x", "y", "c"))

    G, RS, PQ, QD, halves, grads = {}, {}, {}, {}, {}, {}
    dest_of = {"sgu_w_in": ("sgu_w_in", 0), "sgu_w_out": ("sgu_w_out", 0), "attn_w_qkv": ("attn_w_qkv", 0),
               "attn_w_o": ("attn_w_o", 0), "ffn_w_up0": ("ffn_w_up", 0), "ffn_w_up1": ("ffn_w_up", 1),
               "ffn_w_down0": ("ffn_w_down", 0), "ffn_w_down1": ("ffn_w_down", 1)}

    def px(keys):
        return _pair_exchange_comm([G[k] for k in keys])

    def px_done(keys, res):
        for k, a in zip(keys, res[1]):
            RS[k] = a
            PQ[k] = _pair_add(G[k], a, where, "pair_add_" + k)

    def sc(keys):
        return _chip_scatter_comm([PQ[k][0] for k in keys], [PQ[k][1] for k in keys])

    def sc_done(keys, res):
        for k, a in zip(keys, res[0]):
            wk, layer = dest_of[k]
            halves[wk] = _sum_chips(a, where, halves.get(wk), layer, w[wk].shape, "sum_chips_" + k)

    nup = ffn_w_up.shape[2]
    ndown = ffn_w_down.shape[1]
    rows4 = lambda a: a.reshape(4, ndown, D)
    (dc1, dcw1, dcb1), _ = _ffn_bwd_dc(dh4, a1, cw[1], cb[1:2], flat("ffn_w_down1"), "ffn1_bwd_dc")
    gw, _ = _matmul_tn(f1, dh4, "ffn1_dw_down", ka=2 * ndown, nb=D, out_dtype=BF16)
    G["ffn_w_down1"] = rows4(gw)
    (dh3, dfn1, da1), r = _bwd_norm(dc1, T["ffn_w_up1"], h3, fn1, dh4, "ffn1_bwd_in", conv_w=cw[1],
                                    comm=px(["ffn_w_down1"]))
    px_done(["ffn_w_down1"], r)
    G["ffn_w_up1"], r = _matmul_tn(hf1, da1, "ffn1_dw_up", ka=D, nb=nup, out_dtype=BF16, comm=sc(["ffn_w_down1"]))
    sc_done(["ffn_w_down1"], r)
    G["attn_w_o"], r = _matmul_tn(o, dh3, "attn_dw_o", ka=QW // 4, nb=D, out_dtype=BF16, comm=px(["ffn_w_up1"]))
    px_done(["ffn_w_up1"], r)
    do, r = _matmul_nt(dh3, flat("attn_w_o"), "attn_bwd_out", comm=px(["attn_w_o"]))
    px_done(["attn_w_o"], r)
    ks = ["ffn_w_up1", "attn_w_o"]
    (dqkv, dbias, dqg, dkg, dsink), r = _attn_bwd(qkv, do, qg2, kg2, sinks, bias, comm=sc(ks))
    sc_done(ks, r)
    G["attn_w_qkv"], _ = _matmul_tn(hn1, dqkv, "attn_dw_qkv", ka=D, nb=dqkv.shape[1] // 4, out_dtype=BF16)
    (dh2, dmix1), r = _bwd_norm(dqkv, T["attn_w_qkv"], h2, mix1, dh3, "attn_bwd_in", comm=px(["attn_w_qkv"]))
    px_done(["attn_w_qkv"], r)
    (dc0, dcw0, dcb0), r = _ffn_bwd_dc(dh2, a0, cw[0], cb[0:1], flat("ffn_w_down0"), "ffn0_bwd_dc",
                                       comm=sc(["attn_w_qkv"]))
    sc_done(["attn_w_qkv"], r)
    gw, _ = _matmul_tn(f0, dh2, "ffn0_dw_down", ka=2 * ndown, nb=D, out_dtype=BF16)
    G["ffn_w_down0"] = rows4(gw)
    (dh1, dfn0, da0), r = _bwd_norm(dc0, T["ffn_w_up0"], h1, fn0, dh2, "ffn0_bwd_in", conv_w=cw[0],
                                    comm=px(["ffn_w_down0"]))
    px_done(["ffn_w_down0"], r)
    G["ffn_w_up0"], r = _matmul_tn(hf0, da0, "ffn0_dw_up", ka=D, nb=nup, out_dtype=BF16, comm=sc(["ffn_w_down0"]))
    sc_done(["ffn_w_down0"], r)
    G["sgu_w_out"], r = _matmul_tn(yp, dh1, "sgu_dw_out", ka=yp.shape[1] // 4, nb=D, out_dtype=BF16,
                                   comm=px(["ffn_w_up0"]))
    px_done(["ffn_w_up0"], r)
    both = [sc(["ffn_w_up0"]), px(["sgu_w_out"])]
    (dz, dws, dbT, dvg), r = _sgu_bwd(dh1, z, cdf, sgu_v_gain, wtril, wtrilT, bT, flat("sgu_w_out"),
                                      comm=_join(both))
    r = _split(both, r)
    sc_done(["ffn_w_up0"], r[0])
    px_done(["sgu_w_out"], r[1])
    done = ["attn_w_qkv", "attn_w_o", "ffn_w_up", "ffn_w_down"]
    both = [sc(["sgu_w_out"]), _half_exchange_comm([halves[k] for k in done])]
    G["sgu_w_in"], r = _matmul_tn(hn0, dz, "sgu_dw_in", ka=D, nb=dz.shape[1] // 4, out_dtype=BF16, comm=_join(both))
    r = _split(both, r)
    sc_done(["sgu_w_out"], r[0])
    for k, a in zip(done, r[1][0]):
        grads[k] = a
    px_done(["sgu_w_in"], _run_comm(px(["sgu_w_in"]), "grad_last_pair_exchange"))
    (grad_x, dmix0), r = _bwd_norm(dz, T["sgu_w_in"], x2, mix0, dh1, "sgu_bwd_in", comm=sc(["sgu_w_in"]))
    sc_done(["sgu_w_in"], r)

    g = dict(mix_norm=jnp.concatenate([dmix0, dmix1], axis=0), ffn_norm=jnp.concatenate([dfn0, dfn1], axis=0),
             sgu_v_gain=dvg, sgu_w_s=dws, sgu_b_s=dbT.T, attn_q_gain=dqg[:, :HEAD_DIM], attn_k_gain=dkg[:, :HEAD_DIM],
             attn_sinks=dsink[:, :N_KV_HEADS].T.reshape(1, N_HEADS),
             rel_bias=_rel_bias_reduce(dbias.reshape(N_HEADS, CHUNK * 2 * CHUNK), bucket),
             ffn_conv_b=jnp.concatenate([dcb0, dcb1], axis=0))
    small_list = [g[k].reshape(w[k].shape) for k in SMALL] + [jnp.stack([dcw0, dcw1])]
    small_shapes = [a.shape for a in small_list]
    done = ["sgu_w_in", "sgu_w_out"]
    red, r = _allreduce_small(_pack(small_list, 8 * SMALL_PIECE_ROWS),
                              comm=_half_exchange_comm([halves[k] for k in done]))
    for k, a in zip(done, r[0]):
        grads[k] = a
    red = _unpack(red, small_shapes)
    for k, a in zip(SMALL, red[:-1]):
        grads[k] = a
    grads["ffn_conv_w"] = lax.dynamic_slice_in_dim(red[-1], chip * ffn_conv_w.shape[2], ffn_conv_w.shape[2], axis=2)

    delta, new_m, new_v = {}, {}, {}
    two = lambda a: a.reshape(-1, a.shape[-1])
    for k in BIG:
        (d2, m2, v2, g2), _ = _adamw(two(grads[k]), two(w[k]), two(mom[k]), two(var[k]), "adamw_" + k)
        delta[k], new_m[k], new_v[k], grads[k] = (a.reshape(w[k].shape) for a in (d2, m2, v2, g2))
    sm = SMALL + ["ffn_conv_w"]
    sm_shapes = [w[k].shape for k in sm]
    rows = sum(_rows_of(s) for s in sm_shapes)
    (d2, m2, v2, _), _ = _adamw(_pack([grads[k] for k in sm], rows), _pack([w[k] for k in sm], rows),
                                _pack([mom[k] for k in sm], rows), _pack([var[k] for k in sm], rows), "adamw_small")
    for dst, buf in ((delta, d2), (new_m, m2), (new_v, v2)):
        for k, a in zip(sm, _unpack(buf, sm_shapes)):
            dst[k] = a

    return (loss, grad_x[None], *[grads[k] for k in ORDER], *[delta[k] for k in ORDER],
            *[new_m[k] for k in ORDER], *[new_v[k] for k in ORDER])
```

```python
import functools
import math

import numpy as np
import jax
import jax.numpy as jnp
from jax import lax
from jax.experimental import pallas as pl
from jax.experimental.pallas import tpu as pltpu

F32 = jnp.float32
BF16 = jnp.bfloat16

EPS = 1e-6
CHUNK = 128
SGU_GROUPS = 16
HEAD_DIM = 64
N_HEADS = 16
N_KV_HEADS = 4
KV_GROUP = N_HEADS // N_KV_HEADS
REL_BUCKETS = 32
REL_MAX_DIST = 128
LANES = 128
HALO = 16

ADAM_LR = 0.001
ADAM_B1 = 0.9
ADAM_B2 = 0.999
ADAM_EPS = 1e-08
ADAM_WD = 0.01
ADAM_STEP = 10

VMEM_LIMIT_V7X = 56 * 1024 * 1024

_SQRT_HALF = math.sqrt(0.5)
_INV_SQRT_2PI = 1.0 / math.sqrt(2.0 * math.pi)


def _cp(sem):
    return pltpu.CompilerParams(dimension_semantics=sem, vmem_limit_bytes=VMEM_LIMIT_V7X)


def _resident(shape):
    nd = len(shape)
    return pl.BlockSpec(shape, lambda *_: (0,) * nd, pipeline_mode=pl.Buffered(1))


class _Comm:
    def __init__(self, srcs, zones, news, sems, start, finish, middle=None):
        self.srcs, self.zones, self.news, self.sems = list(srcs), list(zones), list(news), list(sems)
        self.start, self.finish = start, finish
        self.middle = middle if middle is not None else (lambda srcs, zones, news, sems: None)


def _join(comms):
    comms = [c for c in comms if c is not None]
    if not comms:
        return None

    def part(seq, attr):
        out, k = [], 0
        for c in comms:
            n = len(getattr(c, attr))
            out.append(seq[k:k + n])
            k += n
        return out

    def run(which):
        def f(srcs, zones, news, sems):
            for c, a, b, d, e in zip(comms, part(srcs, "srcs"), part(zones, "zones"), part(news, "news"), part(sems, "sems")):
                getattr(c, which)(a, b, d, e)
        return f

    cat = lambda attr: [v for c in comms for v in getattr(c, attr)]
    return _Comm(cat("srcs"), cat("zones"), cat("news"), cat("sems"), run("start"), run("finish"), run("middle"))


def _split(comms, res):
    zones, news = res
    out, kz, kn = [], 0, 0
    for c in comms:
        out.append((zones[kz:kz + len(c.zones)], news[kn:kn + len(c.news)]))
        kz += len(c.zones)
        kn += len(c.news)
    return out


def _call(body, *, name, grid, in_specs, out_specs, out_shape, args, scratch_shapes=(), sem=None, comm=None):
    if comm is None:
        res = pl.pallas_call(body, name=name, grid=grid, in_specs=in_specs, out_specs=out_specs, out_shape=out_shape,
                             scratch_shapes=list(scratch_shapes), compiler_params=_cp(sem))(*args)
        return res, None
    single = not isinstance(out_shape, (list, tuple))
    out_specs_l = [out_specs] if single else list(out_specs)
    out_shape_l = [out_shape] if single else list(out_shape)
    n_in, n_out, n_scr = len(in_specs), len(out_shape_l), len(scratch_shapes)
    ns, nz, nn = len(comm.srcs), len(comm.zones), len(comm.news)

    def wrapped(*refs):
        k = n_in
        ins, srcs = refs[:k], refs[k:k + ns]
        k += ns + nz
        outs, zones, news = refs[k:k + n_out], refs[k + n_out:k + n_out + nz], refs[k + n_out + nz:k + n_out + nz + nn]
        k += n_out + nz + nn
        scr, sems = refs[k:k + n_scr], refs[k + n_scr:]
        if not grid:
            comm.start(srcs, zones, news, sems)
            body(*ins, *outs, *scr)
            comm.middle(srcs, zones, news, sems)
            comm.finish(srcs, zones, news, sems)
            return
        first = functools.reduce(jnp.logical_and, [pl.program_id(a) == 0 for a in range(len(grid))])
        last = functools.reduce(jnp.logical_and, [pl.program_id(a) == grid[a] - 1 for a in range(len(grid))])
        early = len(grid) == 1 and grid[0] >= 2
        mid_step = grid[0] - (2 if grid[0] >= 8 else 1)

        @pl.when(first)
        def _():
            comm.start(srcs, zones, news, sems)

        if early:
            @pl.when(pl.program_id(0) == mid_step)
            def _():
                comm.middle(srcs, zones, news, sems)

        body(*ins, *outs, *scr)

        @pl.when(last)
        def _():
            if not early:
                comm.middle(srcs, zones, news, sems)
            comm.finish(srcs, zones, news, sems)

    res = pl.pallas_call(
        wrapped, name=name, grid=grid,
        in_specs=list(in_specs) + [ANY] * (ns + nz), out_specs=out_specs_l + [ANY] * (nz + nn),
        out_shape=out_shape_l + [jax.ShapeDtypeStruct(z.shape, z.dtype) for z in comm.zones] + comm.news,
        input_output_aliases={n_in + ns + i: n_out + i for i in range(nz)},
        scratch_shapes=list(scratch_shapes) + comm.sems,
        compiler_params=_cp(("arbitrary",) * len(grid)),
    )(*args, *comm.srcs, *comm.zones)
    main = res[0] if single else list(res[:n_out])
    return main, (list(res[n_out:n_out + nz]), list(res[n_out + nz:]))


def _run_comm(comm, name):
    ns, nz, nn = len(comm.srcs), len(comm.zones), len(comm.news)

    def body(*refs):
        srcs, zones, news, sems = refs[:ns], refs[ns + nz:ns + 2 * nz], refs[ns + 2 * nz:ns + 2 * nz + nn], refs[ns + 2 * nz + nn:]
        comm.start(srcs, zones, news, sems)
        comm.middle(srcs, zones, news, sems)
        comm.finish(srcs, zones, news, sems)

    res = pl.pallas_call(
        body, name=name, in_specs=[ANY] * (ns + nz), out_specs=[ANY] * (nz + nn),
        out_shape=[jax.ShapeDtypeStruct(z.shape, z.dtype) for z in comm.zones] + comm.news,
        input_output_aliases={ns + i: i for i in range(nz)}, scratch_shapes=comm.sems,
    )(*comm.srcs, *comm.zones)
    return list(res[:nz]), list(res[nz:])


def _dot(a, b):
    return jnp.dot(a, b, preferred_element_type=F32)


def _dot_nt(a, b):
    return lax.dot_general(a, b, (((1,), (1,)), ((), ())), preferred_element_type=F32)


def _dot_tn(a, b):
    return lax.dot_general(a, b, (((0,), (0,)), ((), ())), preferred_element_type=F32)


def _normal_cdf(x):
    return 0.5 * (1.0 + lax.erf(x * _SQRT_HALF))


def _gelu_and_grad(x, cdf):
    return x * cdf, cdf + x * jnp.exp(-0.5 * x * x) * _INV_SQRT_2PI


def _sigmoid(x):
    return 0.5 * jnp.tanh(0.5 * x) + 0.5


def _rms_bwd(dy, x, gain):
    r = lax.rsqrt(jnp.mean(x * x, axis=-1, keepdims=True) + EPS)
    xhat = x * r
    gdy = dy * gain
    dx = r * (gdy - xhat * jnp.mean(gdy * xhat, axis=-1, keepdims=True))
    return dx, jnp.sum(dy * xhat, axis=0, keepdims=True)


def _norm_matmul(x, gain, w4, name, comm=None, tm=1024):
    S, D = x.shape
    nsh, _, ns = w4.shape
    tm = min(tm, S)

    def body(x_ref, g_ref, w_ref, hn_ref, o_ref):
        xf = x_ref[...]
        r = lax.rsqrt(jnp.mean(xf * xf, axis=-1, keepdims=True) + EPS)
        hn = (xf * r * g_ref[...]).astype(BF16)
        hn_ref[...] = hn
        for j in range(nsh):
            o_ref[:, j * ns:(j + 1) * ns] = _dot(hn, w_ref[j]).astype(BF16)

    return _call(
        body, name=name, grid=(S // tm,),
        in_specs=[pl.BlockSpec((tm, D), lambda i: (i, 0)), _resident((1, D)), _resident(w4.shape)],
        out_specs=[pl.BlockSpec((tm, D), lambda i: (i, 0)), pl.BlockSpec((tm, nsh * ns), lambda i: (i, 0))],
        out_shape=[jax.ShapeDtypeStruct((S, D), BF16), jax.ShapeDtypeStruct((S, nsh * ns), BF16)],
        sem=("parallel",), args=(x, gain, w4), comm=comm)


def _matmul_res(a, w, res, name, comm=None, tm=1024):
    S, K = a.shape
    N = w.shape[1]
    tm = min(tm, S)

    def body(a_ref, w_ref, r_ref, o_ref):
        o_ref[...] = r_ref[...] + _dot(a_ref[...], w_ref[...])

    return _call(
        body, name=name, grid=(S // tm,),
        in_specs=[pl.BlockSpec((tm, K), lambda i: (i, 0)), _resident(w.shape), pl.BlockSpec((tm, N), lambda i: (i, 0))],
        out_specs=pl.BlockSpec((tm, N), lambda i: (i, 0)),
        out_shape=jax.ShapeDtypeStruct((S, N), F32),
        sem=("parallel",), args=(a, w, res), comm=comm)


def _matmul_nt(dh, w, name, comm=None, tm=1024):
    S, N = dh.shape
    K = w.shape[0]
    tm = min(tm, S)

    def body(d_ref, w_ref, o_ref):
        o_ref[...] = _dot_nt(d_ref[...].astype(BF16), w_ref[...]).astype(BF16)

    return _call(
        body, name=name, grid=(S // tm,),
        in_specs=[pl.BlockSpec((tm, N), lambda i: (i, 0)), _resident(w.shape)],
        out_specs=pl.BlockSpec((tm, K), lambda i: (i, 0)),
        out_shape=jax.ShapeDtypeStruct((S, K), BF16),
        sem=("parallel",), args=(dh, w), comm=comm)


def _matmul_tn(a, b, name, *, ka, nb, out_dtype, comm=None, ts=1024):
    S, KA = a.shape
    NB = b.shape[1]
    ts = min(ts, S)
    J = max(KA // ka, NB // nb)
    a_map = (lambda j, s: (s, j)) if KA // ka > 1 else (lambda j, s: (s, 0))
    b_map = (lambda j, s: (s, j)) if NB // nb > 1 else (lambda j, s: (s, 0))
    last = S // ts - 1

    def body(a_ref, b_ref, o_ref, acc):
        s = pl.program_id(1)

        @pl.when(s == 0)
        def _():
            acc[...] = jnp.zeros_like(acc)

        acc[...] += _dot_tn(a_ref[...].astype(BF16), b_ref[...].astype(BF16))

        @pl.when(s == last)
        def _():
            o_ref[0] = acc[...].astype(out_dtype)

    return _call(
        body, name=name, grid=(J, S // ts),
        in_specs=[pl.BlockSpec((ts, ka), a_map), pl.BlockSpec((ts, nb), b_map)],
        out_specs=pl.BlockSpec((1, ka, nb), lambda j, s: (j, 0, 0)),
        out_shape=jax.ShapeDtypeStruct((J, ka, nb), out_dtype),
        scratch_shapes=[pltpu.VMEM((ka, nb), F32)],
        sem=("parallel", "arbitrary"), args=(a, b), comm=comm)


def _sgu_prep(w_s):
    G = w_s.shape[0]

    def body(w_ref, t_ref, tt_ref):
        tri = lax.broadcasted_iota(jnp.int32, (CHUNK, CHUNK), 0) >= lax.broadcasted_iota(jnp.int32, (CHUNK, CHUNK), 1)
        for g in range(G):
            t = jnp.where(tri, w_ref[g], 0.0)
            t_ref[g] = t.astype(BF16)
            tt_ref[g] = t.T.astype(BF16)

    return pl.pallas_call(
        body, name="sgu_prep",
        out_shape=[jax.ShapeDtypeStruct(w_s.shape, BF16), jax.ShapeDtypeStruct(w_s.shape, BF16)],
        compiler_params=_cp(None),
    )(w_s)


def _sgu_fwd(z, x, vg, wtril, bT, wout, comm=None, tm=512):
    S = z.shape[0]
    W = z.shape[1] // 2
    D = x.shape[1]
    tm = min(tm, S)

    def body(z_ref, x_ref, vg_ref, wt_ref, bT_ref, wo_ref, yp_ref, h_ref, cdf_ref):
        def chunk(c, carry):
            r0 = pl.multiple_of(c * CHUNK, CHUNK)
            zc = z_ref[pl.ds(r0, CHUNK), :].astype(F32)
            cdf = _normal_cdf(zc)
            cdf_ref[pl.ds(r0, CHUNK), :] = cdf.astype(BF16)
            u = zc[:, :W] * cdf[:, :W]
            v = zc[:, W:] * cdf[:, W:]
            rv = lax.rsqrt(jnp.mean(v * v, axis=-1, keepdims=True) + EPS)
            vn = (v * rv * vg_ref[...]).astype(BF16)
            for g in range(SGU_GROUPS):
                sl = slice(g * LANES, (g + 1) * LANES)
                s = _dot(wt_ref[g], vn[:, sl]) + bT_ref[:, g:g + 1]
                yp_ref[pl.ds(r0, CHUNK), sl] = (u[:, sl] * s).astype(BF16)
            return carry

        lax.fori_loop(0, tm // CHUNK, chunk, 0)
        h_ref[...] = x_ref[...] + _dot(yp_ref[...], wo_ref[...])

    return _call(
        body, name="sgu_fwd", grid=(S // tm,),
        in_specs=[pl.BlockSpec((tm, 2 * W), lambda i: (i, 0)), pl.BlockSpec((tm, D), lambda i: (i, 0)),
                  _resident((1, W)), _resident(wtril.shape), _resident(bT.shape), _resident(wout.shape)],
        out_specs=[pl.BlockSpec((tm, W), lambda i: (i, 0)), pl.BlockSpec((tm, D), lambda i: (i, 0)),
                   pl.BlockSpec((tm, 2 * W), lambda i: (i, 0))],
        out_shape=[jax.ShapeDtypeStruct((S, W), BF16), jax.ShapeDtypeStruct((S, D), F32),
                   jax.ShapeDtypeStruct((S, 2 * W), BF16)],
        sem=("parallel",), args=(z, x, vg, wtril, bT, wout), comm=comm)


def _sgu_bwd(dh, z, cdf, vg, wtril, wtrilT, bT, wout, comm=None, tm=512):
    S = z.shape[0]
    W = z.shape[1] // 2
    D = dh.shape[1]
    G = SGU_GROUPS
    tm = min(tm, S)
    last = S // tm - 1

    def body(dh_ref, z_ref, cdf_ref, vg_ref, wt_ref, wtT_ref, bT_ref, wo_ref,
             dz_ref, dws_ref, dbT_ref, dvg_ref, dyp_s, du_s, dvn_s, dsacc):
        i = pl.program_id(0)

        @pl.when(i == 0)
        def _():
            dws_ref[...] = jnp.zeros_like(dws_ref)
            dvg_ref[...] = jnp.zeros_like(dvg_ref)
            dsacc[...] = jnp.zeros_like(dsacc)

        dyp_s[...] = _dot_nt(dh_ref[...].astype(BF16), wo_ref[...])
        tri = lax.broadcasted_iota(jnp.int32, (CHUNK, CHUNK), 0) >= lax.broadcasted_iota(jnp.int32, (CHUNK, CHUNK), 1)

        def chunk(c, carry):
            r0 = pl.multiple_of(c * CHUNK, CHUNK)
            zc = z_ref[pl.ds(r0, CHUNK), :].astype(F32)
            cdf = cdf_ref[pl.ds(r0, CHUNK), :].astype(F32)
            u, gu = _gelu_and_grad(zc[:, :W], cdf[:, :W])
            v, gv = _gelu_and_grad(zc[:, W:], cdf[:, W:])
            rv = lax.rsqrt(jnp.mean(v * v, axis=-1, keepdims=True) + EPS)
            vhat = v * rv
            vgain = vg_ref[...]
            vn = (vhat * vgain).astype(BF16)
            dyp = dyp_s[pl.ds(r0, CHUNK), :]
            for g in range(G):
                sl = slice(g * LANES, (g + 1) * LANES)
                vng = vn[:, sl]
                s = _dot(wt_ref[g], vng) + bT_ref[:, g:g + 1]
                ds = dyp[:, sl] * u[:, sl]
                du_s[:, sl] = dyp[:, sl] * s
                dsb = ds.astype(BF16)
                dvn_s[:, sl] = _dot(wtT_ref[g], dsb)
                dws_ref[g] += jnp.where(tri, _dot_nt(dsb, vng), 0.0)
                dsacc[g] += ds
            dvn = dvn_s[...]
            dvg_ref[...] += jnp.sum(dvn * vhat, axis=0, keepdims=True)
            gdy = dvn * vgain
            dv = rv * (gdy - vhat * jnp.mean(gdy * vhat, axis=-1, keepdims=True))
            dz_ref[pl.ds(r0, CHUNK), :W] = (du_s[...] * gu).astype(BF16)
            dz_ref[pl.ds(r0, CHUNK), W:] = (dv * gv).astype(BF16)
            return carry

        lax.fori_loop(0, tm // CHUNK, chunk, 0)

        @pl.when(i == last)
        def _():
            for g in range(G):
                dbT_ref[:, g:g + 1] = jnp.sum(dsacc[g], axis=1, keepdims=True)

    return _call(
        body, name="sgu_bwd", grid=(S // tm,),
        in_specs=[pl.BlockSpec((tm, D), lambda i: (i, 0)), pl.BlockSpec((tm, 2 * W), lambda i: (i, 0)),
                  pl.BlockSpec((tm, 2 * W), lambda i: (i, 0)),
                  _resident((1, W)), _resident(wtril.shape), _resident(wtrilT.shape), _resident(bT.shape),
                  _resident(wout.shape)],
        out_specs=[pl.BlockSpec((tm, 2 * W), lambda i: (i, 0)),
                   pl.BlockSpec((G, CHUNK, CHUNK), lambda i: (0, 0, 0)),
                   pl.BlockSpec((CHUNK, G), lambda i: (0, 0)),
                   pl.BlockSpec((1, W), lambda i: (0, 0))],
        out_shape=[jax.ShapeDtypeStruct((S, 2 * W), BF16), jax.ShapeDtypeStruct((G, CHUNK, CHUNK), F32),
                   jax.ShapeDtypeStruct((CHUNK, G), F32), jax.ShapeDtypeStruct((1, W), F32)],
        scratch_shapes=[pltpu.VMEM((tm, W), F32), pltpu.VMEM((CHUNK, W), F32), pltpu.VMEM((CHUNK, W), F32),
                        pltpu.VMEM((G, CHUNK, CHUNK), F32)],
        sem=("arbitrary",), args=(dh, z, cdf, vg, wtril, wtrilT, bT, wout), comm=comm)


def _conv_taps(a32, r0, R, reuse=False):
    base = r0 + HALO
    if not reuse:
        return a32[base:base + R, :], a32[base - 1:base - 1 + R, :], a32[base - 2:base - 2 + R, :]
    X = a32[base - 8:base + R, :]
    return X[8:], pltpu.roll(X, 1, 0)[8:], pltpu.roll(X, 2, 0)[8:]


def _ffn_fwd(a, h_in, cw, cb, wdown, name, comm=None, target=None, tm=512, R=64):
    S, C = a.shape
    F = C // 2
    D = h_in.shape[1]
    tm = min(tm, S)
    head = target is not None

    def body(a_ref, halo_ref, h_ref, cw_ref, cb_ref, wd_ref, *rest):
        if head:
            t_ref, f_ref, dy_ref, l_ref, a32 = rest
        else:
            f_ref, ho_ref, a32 = rest
        i = pl.program_id(0)
        a32[0:HALO, :] = jnp.where(i > 0, halo_ref[...].astype(F32), 0.0)
        a32[HALO:, :] = a_ref[...].astype(F32)

        for r0 in range(0, tm, R):
            a0, a1, a2 = _conv_taps(a32, r0, R)
            cpre = cw_ref[0:1, :] * a2 + cw_ref[1:2, :] * a1 + cw_ref[2:3, :] * a0 + cb_ref[...]
            g = cpre[:, :F]
            f_ref[r0:r0 + R, :] = (g * _sigmoid(g) * cpre[:, F:]).astype(BF16)
        h_out = h_ref[...] + _dot(f_ref[...], wd_ref[...])
        if not head:
            ho_ref[...] = h_out
            return

        @pl.when(i == 0)
        def _():
            l_ref[...] = jnp.zeros_like(l_ref)

        e = h_out - t_ref[...]
        dy_ref[...] = e * (1.0 / D)
        rows = jnp.sum(e * e, axis=-1, keepdims=True) * (1.0 / D)
        l_ref[...] += 0.5 * jnp.sum(rows, axis=0, keepdims=True)

    hb = tm // HALO
    row = pl.BlockSpec((tm, D), lambda i: (i, 0))
    in_specs = [pl.BlockSpec((tm, C), lambda i: (i, 0)),
                pl.BlockSpec((HALO, C), lambda i: (jnp.maximum(i * hb - 1, 0), 0)),
                row, _resident((3, C)), _resident((1, C)), _resident(wdown.shape)]
    out_specs = [pl.BlockSpec((tm, F), lambda i: (i, 0)), row]
    out_shape = [jax.ShapeDtypeStruct((S, F), BF16), jax.ShapeDtypeStruct((S, D), F32)]
    args = (a, a, h_in, cw, cb, wdown)
    if head:
        in_specs.append(row)
        out_specs.append(pl.BlockSpec((1, 1), lambda i: (0, 0)))
        out_shape.append(jax.ShapeDtypeStruct((1, 1), F32))
        args += (target,)
    return _call(
        body, name=name, grid=(S // tm,), in_specs=in_specs, out_specs=out_specs, out_shape=out_shape,
        scratch_shapes=[pltpu.VMEM((HALO + tm, C), F32)],
        sem=("arbitrary",) if head else ("parallel",), args=args, comm=comm)


def _ffn_bwd_dc(dh, a, cw, cb, wdown, name, comm=None, tm=256, R=64):
    S, C = a.shape
    F = C // 2
    D = dh.shape[1]
    tm = min(tm, S)

    def body(dh_ref, a_ref, halo_ref, cw_ref, cb_ref, wd_ref, dc_ref, dcw_ref, dcb_ref, a32, df_s, acc):
        i = pl.program_id(0)

        @pl.when(i == 0)
        def _():
            acc[...] = jnp.zeros_like(acc)

        a32[0:HALO, :] = jnp.where(i > 0, halo_ref[...].astype(F32), 0.0)
        a32[HALO:, :] = a_ref[...].astype(F32)
        df_s[...] = _dot_nt(dh_ref[...].astype(BF16), wd_ref[...])
        rows8 = lambda v: functools.reduce(jnp.add, [v[8 * k:8 * k + 8] for k in range(R // 8)])

        for r0 in range(0, tm, R):
            a0, a1, a2 = _conv_taps(a32, r0, R, reuse=True)
            cpre = cw_ref[0:1, :] * a2 + cw_ref[1:2, :] * a1 + cw_ref[2:3, :] * a0 + cb_ref[...]
            g = cpre[:, :F]
            val = cpre[:, F:]
            sg = _sigmoid(g)
            df = df_s[r0:r0 + R, :]
            dg = df * val * (sg * (1.0 + g * (1.0 - sg)))
            dval = df * (g * sg)
            dc = jnp.concatenate([dg, dval], axis=1)
            dc_ref[r0:r0 + R, :] = dc.astype(BF16)
            acc[0] += rows8(dc * a2)
            acc[1] += rows8(dc * a1)
            acc[2] += rows8(dc * a0)
            acc[3] += rows8(dc)

        @pl.when(i == S // tm - 1)
        def _():
            for k in range(3):
                dcw_ref[k:k + 1, :] = jnp.sum(acc[k], axis=0, keepdims=True)
            dcb_ref[...] = jnp.sum(acc[3], axis=0, keepdims=True)

    hb = tm // HALO
    return _call(
        body, name=name, grid=(S // tm,),
        in_specs=[pl.BlockSpec((tm, D), lambda i: (i, 0)),
                  pl.BlockSpec((tm, C), lambda i: (i, 0)),
                  pl.BlockSpec((HALO, C), lambda i: (jnp.maximum(i * hb - 1, 0), 0)),
                  _resident((3, C)), _resident((1, C)), _resident(wdown.shape)],
        out_specs=[pl.BlockSpec((tm, C), lambda i: (i, 0)),
                   pl.BlockSpec((3, C), lambda i: (0, 0)), pl.BlockSpec((1, C), lambda i: (0, 0))],
        out_shape=[jax.ShapeDtypeStruct((S, C), BF16), jax.ShapeDtypeStruct((3, C), F32),
                   jax.ShapeDtypeStruct((1, C), F32)],
        scratch_shapes=[pltpu.VMEM((HALO + tm, C), F32), pltpu.VMEM((tm, F), F32), pltpu.VMEM((4, 8, C), F32)],
        sem=("arbitrary",), args=(dh, a, a, cw, cb, wdown), comm=comm)


def _bwd_norm(dA, w4, h_in, gain, dh_out, name, conv_w=None, comm=None, R=64):
    S, N = dA.shape
    nsh, D, ns = w4.shape
    conv = conv_w is not None
    tm = min(256 if conv else 512, S)
    nt = S // tm

    def finish(src_ref, w_ref, h_ref, g_ref, dho_ref, dhi_ref, dg_ref):
        dhn = _dot_nt(src_ref[:, 0:ns], w_ref[0])
        for j in range(1, nsh):
            dhn += _dot_nt(src_ref[:, j * ns:(j + 1) * ns], w_ref[j])
        dx, dgain = _rms_bwd(dhn, h_ref[...], g_ref[...])
        dg_ref[...] += dgain
        dhi_ref[...] = dho_ref[...] + dx

    def body_plain(dA_ref, w_ref, h_ref, g_ref, dho_ref, dhi_ref, dg_ref):
        @pl.when(pl.program_id(0) == 0)
        def _():
            dg_ref[...] = jnp.zeros_like(dg_ref)

        finish(dA_ref, w_ref, h_ref, g_ref, dho_ref, dhi_ref, dg_ref)

    def body_conv(dc_ref, halo_ref, cw_ref, w_ref, h_ref, g_ref, dho_ref, dhi_ref, dg_ref, da_ref, dc32):
        i = pl.program_id(0)

        @pl.when(i == 0)
        def _():
            dg_ref[...] = jnp.zeros_like(dg_ref)

        dc32[0:tm, :] = dc_ref[...].astype(F32)
        dc32[tm:, :] = jnp.where(i < nt - 1, halo_ref[...].astype(F32), 0.0)

        for r0 in range(0, tm, R):
            d0 = dc32[r0:r0 + R, :]
            d1 = dc32[r0 + 1:r0 + 1 + R, :]
            d2 = dc32[r0 + 2:r0 + 2 + R, :]
            da = cw_ref[2:3, :] * d0 + cw_ref[1:2, :] * d1 + cw_ref[0:1, :] * d2
            da_ref[r0:r0 + R, :] = da.astype(BF16)
        finish(da_ref, w_ref, h_ref, g_ref, dho_ref, dhi_ref, dg_ref)

    row = lambda width: pl.BlockSpec((tm, width), lambda i: (i, 0))
    common_in = [_resident(w4.shape), row(D), _resident((1, D)), row(D)]
    common_out = [row(D), pl.BlockSpec((1, D), lambda i: (0, 0))]
    common_shape = [jax.ShapeDtypeStruct((S, D), F32), jax.ShapeDtypeStruct((1, D), F32)]
    if not conv:
        return _call(
            body_plain, name=name, grid=(nt,),
            in_specs=[row(N)] + common_in, out_specs=common_out, out_shape=common_shape,
            sem=("arbitrary",), args=(dA, w4, h_in, gain, dh_out), comm=comm)
    hb = tm // HALO
    nhb = S // HALO
    return _call(
        body_conv, name=name, grid=(nt,),
        in_specs=[row(N), pl.BlockSpec((HALO, N), lambda i: (jnp.minimum((i + 1) * hb, nhb - 1), 0)),
                  _resident((3, N))] + common_in,
        out_specs=common_out + [row(N)],
        out_shape=common_shape + [jax.ShapeDtypeStruct((S, N), BF16)],
        scratch_shapes=[pltpu.VMEM((tm + HALO, N), F32)],
        sem=("arbitrary",), args=(dA, dA, conv_w, w4, h_in, gain, dh_out), comm=comm)


def _rel_buckets_flat():
    q = np.arange(CHUNK)[:, None] + CHUNK
    k = np.arange(2 * CHUNK)[None, :]
    n = np.maximum(q - k, 0)
    max_exact = REL_BUCKETS // 2
    large = max_exact + (np.log(np.maximum(n, 1).astype(np.float32) / max_exact)
                         / math.log(REL_MAX_DIST / max_exact) * (REL_BUCKETS - max_exact)).astype(np.int32)
    large = np.minimum(large, REL_BUCKETS - 1)
    return np.where(n < max_exact, n, large).astype(np.int32).reshape(1, CHUNK * 2 * CHUNK)


def _split_bf16(x):
    hi = x.astype(BF16)
    return hi, (x - hi.astype(F32)).astype(BF16)


def _rel_bias_expand(rel_bias, bucket):
    B, H = rel_bias.shape
    n = bucket.shape[1]

    def body(rb_ref, bk_ref, o_ref):
        oh = (bk_ref[...] == lax.broadcasted_iota(jnp.int32, (B, n), 0)).astype(BF16)
        hi, lo = _split_bf16(rb_ref[...])
        o_ref[...] = _dot_tn(hi, oh) + _dot_tn(lo, oh)

    return pl.pallas_call(body, name="rel_bias_expand", out_shape=jax.ShapeDtypeStruct((H, n), F32),
                          compiler_params=_cp(None))(rel_bias, bucket)


def _rel_bias_reduce(dbias, bucket):
    H, n = dbias.shape
    B = REL_BUCKETS

    def body(db_ref, bk_ref, o_ref):
        oh = (bk_ref[...] == lax.broadcasted_iota(jnp.int32, (B, n), 0)).astype(BF16)
        hi, lo = _split_bf16(db_ref[...])
        o_ref[...] = _dot_nt(oh, hi) + _dot_nt(oh, lo)

    return pl.pallas_call(body, name="rel_bias_reduce", out_shape=jax.ShapeDtypeStruct((B, H), F32),
                          compiler_params=_cp(None))(dbias, bucket)


def _lo_mask(rows):
    return lax.broadcasted_iota(jnp.int32, (rows, LANES), 1) < HEAD_DIM


def _half_sums(y, lo):
    s_lo = jnp.sum(jnp.where(lo, y, 0.0), axis=-1, keepdims=True)
    s_hi = jnp.sum(jnp.where(lo, 0.0, y), axis=-1, keepdims=True)
    return jnp.where(lo, s_lo, s_hi)


def _half_rms(x, gain, lo):
    r = lax.rsqrt(_half_sums(x * x, lo) * (1.0 / HEAD_DIM) + EPS)
    xhat = x * r
    return xhat * gain, xhat, r


def _half_rms_bwd(dy, xhat, r, gain, lo):
    gdy = dy * gain
    dx = r * (gdy - xhat * (_half_sums(gdy * xhat, lo) * (1.0 / HEAD_DIM)))
    return dx, jnp.sum(dy * xhat, axis=0, keepdims=True)


def _dup_half(pair, e, lo):
    sw = pltpu.roll(pair, HEAD_DIM, 1)
    return jnp.where(lo, pair, sw) if e == 0 else jnp.where(lo, sw, pair)


def _band_valid(n):
    qi = lax.broadcasted_iota(jnp.int32, (KV_GROUP * CHUNK, 2 * CHUNK), 0) & (CHUNK - 1)
    ki = lax.broadcasted_iota(jnp.int32, (KV_GROUP * CHUNK, 2 * CHUNK), 1)
    dist = qi + CHUNK - ki
    return (dist >= 0) & (dist < CHUNK) & ((n > 0) | (ki >= CHUNK))


def _stack_heads(a, b, lo):
    return jnp.concatenate([jnp.where(lo, a, 0.0), jnp.where(lo, 0.0, a), jnp.where(lo, b, 0.0), jnp.where(lo, 0.0, b)],
                           axis=0)


def _unstack_heads(x4, lo):
    return (jnp.where(lo, x4[0:CHUNK], x4[CHUNK:2 * CHUNK]),
            jnp.where(lo, x4[2 * CHUNK:3 * CHUNK], x4[3 * CHUNK:]))


def _sink_col(sink_ref, hk):
    row = lax.broadcasted_iota(jnp.int32, (KV_GROUP * CHUNK, 1), 0)
    col = jnp.full((KV_GROUP * CHUNK, 1), sink_ref[KV_GROUP * hk + KV_GROUP - 1], F32)
    for r in range(KV_GROUP - 2, -1, -1):
        col = jnp.where(row < (r + 1) * CHUNK, sink_ref[KV_GROUP * hk + r], col)
    return col


def _softmax_sink(s, valid, sink):
    s = jnp.where(valid, s, -jnp.inf)
    m = jnp.maximum(jnp.max(s, axis=-1, keepdims=True), sink)
    p = jnp.exp(s - m)
    esink = jnp.exp(sink - m)
    inv = 1.0 / (jnp.sum(p, axis=-1, keepdims=True) + esink)
    return p * inv, esink * inv


QW = N_HEADS * HEAD_DIM
KVW = N_KV_HEADS * HEAD_DIM


def _attn_fwd(qkv, qg2, kg2, sinks, bias, comm=None):
    S = qkv.shape[0]
    nb = S // CHUNK

    def body(cur_ref, prev_ref, qg_ref, kg_ref, sink_ref, bias_ref, o_ref):
        n = pl.program_id(0)
        lo = _lo_mask(CHUNK)
        lo2 = _lo_mask(2 * CHUNK)
        valid = _band_valid(n)
        for j in range(N_KV_HEADS // 2):
            kc = slice(QW + j * LANES, QW + (j + 1) * LANES)
            vc = slice(QW + KVW + j * LANES, QW + KVW + (j + 1) * LANES)
            kpair = jnp.concatenate([prev_ref[:, j * LANES:(j + 1) * LANES], cur_ref[:, kc]], axis=0).astype(F32)
            vpair = jnp.concatenate([prev_ref[:, KVW + j * LANES:KVW + (j + 1) * LANES], cur_ref[:, vc]], axis=0).astype(F32)
            knpair, _, _ = _half_rms(kpair, kg_ref[...], lo2)
            for e in range(2):
                hk = 2 * j + e
                kdup = _dup_half(knpair, e, lo2).astype(BF16)
                vdup = _dup_half(vpair, e, lo2).astype(BF16)
                ca = slice(2 * hk * LANES, (2 * hk + 1) * LANES)
                cb = slice((2 * hk + 1) * LANES, (2 * hk + 2) * LANES)
                qna, _, _ = _half_rms(cur_ref[:, ca].astype(F32), qg_ref[...], lo)
                qnb, _, _ = _half_rms(cur_ref[:, cb].astype(F32), qg_ref[...], lo)
                qm4 = _stack_heads(qna, qnb, lo).astype(BF16)
                s = _dot_nt(qm4, kdup) * (HEAD_DIM ** -0.5) + bias_ref[hk]
                p, _ = _softmax_sink(s, valid, _sink_col(sink_ref, hk))
                oa, ob = _unstack_heads(_dot(p.astype(BF16), vdup), lo)
                o_ref[:, ca] = oa.astype(BF16)
                o_ref[:, cb] = ob.astype(BF16)

    return _call(
        body, name="attn_fwd", grid=(nb,),
        in_specs=[pl.BlockSpec((CHUNK, QW + 2 * KVW), lambda n: (n, 0)),
                  pl.BlockSpec((CHUNK, 2 * KVW), lambda n: (jnp.maximum(n - 1, 0), QW // (2 * KVW))),
                  _resident((1, LANES)), _resident((1, LANES)),
                  pl.BlockSpec(memory_space=pltpu.SMEM),
                  _resident(bias.shape)],
        out_specs=pl.BlockSpec((CHUNK, QW), lambda n: (n, 0)),
        out_shape=jax.ShapeDtypeStruct((S, QW), BF16),
        sem=("parallel",), args=(qkv, qkv, qg2, kg2, sinks, bias), comm=comm)


def _attn_bwd(qkv, do, qg2, kg2, sinks, bias, comm=None):
    S = qkv.shape[0]
    nb = S // CHUNK

    def body(cur_ref, prev_ref, do_ref, qg_ref, kg_ref, sink_ref, bias_ref,
             dqkv_ref, dbias_ref, dqg_ref, dkg_ref, dsink_ref, carry, band, dsacc, gacc):
        i = pl.program_id(0)
        n = nb - 1 - i
        lo = _lo_mask(CHUNK)
        lo2 = _lo_mask(2 * CHUNK)
        lane = lax.broadcasted_iota(jnp.int32, (KV_GROUP * CHUNK, LANES), 1)
        valid = _band_valid(n)

        @pl.when(i == 0)
        def _():
            dbias_ref[...] = jnp.zeros_like(dbias_ref)
            carry[...] = jnp.zeros_like(carry)
            dsacc[...] = jnp.zeros_like(dsacc)
            gacc[...] = jnp.zeros_like(gacc)

        qgain = qg_ref[...]
        kgain = kg_ref[...]
        for j in range(N_KV_HEADS // 2):
            kc = slice(QW + j * LANES, QW + (j + 1) * LANES)
            vc = slice(QW + KVW + j * LANES, QW + KVW + (j + 1) * LANES)
            kpair = jnp.concatenate([prev_ref[:, j * LANES:(j + 1) * LANES], cur_ref[:, kc]], axis=0).astype(F32)
            vpair = jnp.concatenate([prev_ref[:, KVW + j * LANES:KVW + (j + 1) * LANES], cur_ref[:, vc]], axis=0).astype(F32)
            knpair, khat, kr = _half_rms(kpair, kgain, lo2)
            dk_folds = []
            dv_folds = []
            for e in range(2):
                hk = 2 * j + e
                kdup = _dup_half(knpair, e, lo2).astype(BF16)
                vdup = _dup_half(vpair, e, lo2).astype(BF16)
                ca = slice(2 * hk * LANES, (2 * hk + 1) * LANES)
                cb = slice((2 * hk + 1) * LANES, (2 * hk + 2) * LANES)
                qna, qhata, qra = _half_rms(cur_ref[:, ca].astype(F32), qgain, lo)
                qnb, qhatb, qrb = _half_rms(cur_ref[:, cb].astype(F32), qgain, lo)
                qm4 = _stack_heads(qna, qnb, lo).astype(BF16)
                dom4 = _stack_heads(do_ref[:, ca].astype(F32), do_ref[:, cb].astype(F32), lo).astype(BF16)
                s = _dot_nt(qm4, kdup) * (HEAD_DIM ** -0.5) + bias_ref[hk]
                p, psink = _softmax_sink(s, valid, _sink_col(sink_ref, hk))
                dp = _dot_nt(dom4, vdup)
                delta = jnp.sum(p * dp, axis=-1, keepdims=True)
                ds = p * (dp - delta)
                dbias_ref[hk] += ds
                dsacc[...] += jnp.where(lane == hk, -(psink * delta), 0.0)
                dsr = (ds * (HEAD_DIM ** -0.5)).astype(BF16)
                dqna, dqnb = _unstack_heads(_dot(dsr, kdup), lo)
                dkd = _dot_tn(dsr, qm4)
                dvd = _dot_tn(p.astype(BF16), dom4)
                dqa, dqga = _half_rms_bwd(dqna, qhata, qra, qgain, lo)
                dqb, dqgb = _half_rms_bwd(dqnb, qhatb, qrb, qgain, lo)
                gacc[0:1, :] += dqga + dqgb
                dqkv_ref[:, ca] = dqa.astype(BF16)
                dqkv_ref[:, cb] = dqb.astype(BF16)
                dk_folds.append(dkd + pltpu.roll(dkd, HEAD_DIM, 1))
                dv_folds.append(dvd + pltpu.roll(dvd, HEAD_DIM, 1))
            dkn = jnp.where(lo2, dk_folds[0], dk_folds[1])
            dk, dkg = _half_rms_bwd(dkn, khat, kr, kgain, lo2)
            gacc[1:2, :] += dkg
            band[:, j * LANES:(j + 1) * LANES] = dk
            band[:, KVW + j * LANES:KVW + (j + 1) * LANES] = jnp.where(lo2, dv_folds[0], dv_folds[1])
        dqkv_ref[:, QW:] = (band[CHUNK:, :] + carry[...]).astype(BF16)
        carry[...] = band[0:CHUNK, :]

        @pl.when(i == nb - 1)
        def _():
            g = gacc[...]
            g = g + pltpu.roll(g, HEAD_DIM, 1)
            dqg_ref[...] = g[0:1, :]
            dkg_ref[...] = g[1:2, :]
            for r in range(KV_GROUP):
                dsink_ref[r:r + 1, :] = jnp.sum(dsacc[r * CHUNK:(r + 1) * CHUNK, :], axis=0, keepdims=True)

    vec = pl.BlockSpec((1, LANES), lambda i: (0, 0))
    return _call(
        body, name="attn_bwd", grid=(nb,),
        in_specs=[pl.BlockSpec((CHUNK, QW + 2 * KVW), lambda i: (nb - 1 - i, 0)),
                  pl.BlockSpec((CHUNK, 2 * KVW), lambda i: (jnp.maximum(nb - 2 - i, 0), QW // (2 * KVW))),
                  pl.BlockSpec((CHUNK, QW), lambda i: (nb - 1 - i, 0)),
                  _resident((1, LANES)), _resident((1, LANES)),
                  pl.BlockSpec(memory_space=pltpu.SMEM),
                  _resident(bias.shape)],
        out_specs=[pl.BlockSpec((CHUNK, QW + 2 * KVW), lambda i: (nb - 1 - i, 0)),
                   pl.BlockSpec(bias.shape, lambda i: (0, 0, 0)), vec, vec,
                   pl.BlockSpec((KV_GROUP, LANES), lambda i: (0, 0))],
        out_shape=[jax.ShapeDtypeStruct((S, QW + 2 * KVW), BF16),
                   jax.ShapeDtypeStruct(bias.shape, F32),
                   jax.ShapeDtypeStruct((1, LANES), F32), jax.ShapeDtypeStruct((1, LANES), F32),
                   jax.ShapeDtypeStruct((KV_GROUP, LANES), F32)],
        scratch_shapes=[pltpu.VMEM((CHUNK, 2 * KVW), F32), pltpu.VMEM((2 * CHUNK, 2 * KVW), F32),
                        pltpu.VMEM((KV_GROUP * CHUNK, LANES), F32), pltpu.VMEM((8, LANES), F32)],
        sem=("arbitrary",), args=(qkv, qkv, do, qg2, kg2, sinks, bias), comm=comm)


def _row_tile(rows, cols, n_arrays):
    budget = VMEM_LIMIT_V7X // 4 // (2 * n_arrays * 4 * cols)
    best = 8
    for n in range(1, rows // 8 + 1):
        if rows % n == 0 and (rows // n) % 8 == 0 and rows // n <= budget:
            best = rows // n
            break
    return best


def _adamw(g, w, m, v, name, comm=None):
    R, C = g.shape
    tr = _row_tile(R, C, 8)

    def body(g_ref, w_ref, m_ref, v_ref, d_ref, mo_ref, vo_ref, go_ref):
        gg = g_ref[...]
        go_ref[...] = gg
        mn = ADAM_B1 * m_ref[...] + (1.0 - ADAM_B1) * gg
        vn = ADAM_B2 * v_ref[...] + (1.0 - ADAM_B2) * jnp.square(gg)
        m_hat = mn / (1.0 - ADAM_B1 ** ADAM_STEP)
        v_hat = vn / (1.0 - ADAM_B2 ** ADAM_STEP)
        d_ref[...] = -ADAM_LR * (m_hat / (jnp.sqrt(v_hat) + ADAM_EPS) + ADAM_WD * w_ref[...])
        mo_ref[...] = mn
        vo_ref[...] = vn

    spec = pl.BlockSpec((tr, C), lambda i: (i, 0))
    return _call(
        body, name=name, grid=(R // tr,), in_specs=[spec] * 4, out_specs=[spec] * 4,
        out_shape=[jax.ShapeDtypeStruct((R, C), F32)] * 4, sem=("parallel",), args=(g, w, m, v), comm=comm)


def _place_shard(shards, layer, where, dtype, name):
    _, R, C = shards.shape
    tr = _row_tile(R, C, 2) if R % 8 == 0 else R

    def body(s_ref, x_ref, o_ref):
        o_ref[...] = x_ref[...].astype(dtype)

    return pl.pallas_call(
        body, name=name,
        grid_spec=pltpu.PrefetchScalarGridSpec(
            num_scalar_prefetch=1, grid=(R // tr,),
            in_specs=[pl.BlockSpec((1, tr, C), lambda i, s_ref: (layer, i, 0))],
            out_specs=pl.BlockSpec((1, tr, C), lambda i, s_ref: (s_ref[1], i, 0))),
        out_shape=jax.ShapeDtypeStruct((4, R, C), dtype),
        compiler_params=_cp(("parallel",)),
    )(where, shards)


def _pair_add(g4, rsib, where, name):
    J, R, C = g4.shape
    Rh = R // 2
    tr = _row_tile(Rh, C, 4)
    g5 = g4.reshape(J, 2, Rh, C)

    def body(s_ref, g_ref, r_ref, p_ref, q_ref):
        val = (g_ref[...].astype(F32)[0] + r_ref[...].astype(F32)).astype(BF16)
        p_ref[...] = val

        @pl.when(pl.program_id(1) == s_ref[1])
        def _():
            q_ref[...] = val

    return pl.pallas_call(
        body, name=name,
        grid_spec=pltpu.PrefetchScalarGridSpec(
            num_scalar_prefetch=1, grid=(Rh // tr, J),
            in_specs=[pl.BlockSpec((1, 1, tr, C), lambda i, j, s_ref: (j, s_ref[0], i, 0)),
                      pl.BlockSpec((1, tr, C), lambda i, j, s_ref: (j, i, 0))],
            out_specs=[pl.BlockSpec((1, tr, C), lambda i, j, s_ref: (j, i, 0)),
                       pl.BlockSpec((1, tr, C), lambda i, j, s_ref: (s_ref[1], i, 0))]),
        out_shape=[jax.ShapeDtypeStruct((J, Rh, C), BF16)] * 2,
        compiler_params=_cp(("parallel", "arbitrary")),
    )(where, g5, rsib)


def _sum_chips(q, where, dest, layer, out_shape, name):
    J, Rh, C = q.shape
    tr = _row_tile(Rh, C, 3)
    nb = Rh // tr

    def body(s_ref, q_ref, *rest):
        qq = q_ref[...].astype(F32)
        rest[-1][0] = ((qq[0] + qq[1]) + qq[2]) + qq[3]

    have = dest is not None
    return pl.pallas_call(
        body, name=name,
        grid_spec=pltpu.PrefetchScalarGridSpec(
            num_scalar_prefetch=1, grid=(nb,),
            in_specs=[pl.BlockSpec((J, tr, C), lambda i, s_ref: (0, i, 0))] + ([ANY] if have else []),
            out_specs=pl.BlockSpec((1, tr, C), lambda i, s_ref: (layer, s_ref[0] * nb + i, 0))),
        out_shape=jax.ShapeDtypeStruct(out_shape, F32),
        input_output_aliases={2: 0} if have else {},
        compiler_params=_cp(("parallel",)),
    )(*((where, q, dest) if have else (where, q)))


MESH = pl.DeviceIdType.MESH
ANY = pl.BlockSpec(memory_space=pl.ANY)


def _place():
    x, y, c = lax.axis_index("x"), lax.axis_index("y"), lax.axis_index("c")
    others = [(1 - x, y), (x, 1 - y), (1 - x, 1 - y)]
    return x, y, c, 2 * x + y, others, [2 * ox + oy for ox, oy in others]


def _gather_comm(items):
    n = len(items)
    placed = [it[0] for it in items]
    split = [it[3] for it in items]

    def rows(t, ref, half):
        _, lo, hi, _ = items[t]
        if not split[t]:
            return ref if (lo, hi) == (0, placed[t].shape[1]) else ref.at[pl.ds(lo, hi - lo), :]
        rh = (hi - lo) // 2
        return ref.at[pl.ds(lo + half * rh, rh), :]

    def sends(outs, sems):
        send, recv = sems[0], sems[1]
        x, y, c, me, others, okey = _place()
        cps = []
        for t in range(n):
            mine = rows(t, outs[t].at[me], c)
            for j, (ox, oy) in enumerate(others):
                cps.append(pltpu.make_async_remote_copy(
                    src_ref=mine, dst_ref=mine,
                    send_sem=send.at[t, j], recv_sem=recv.at[t, j], device_id=(ox, oy, c), device_id_type=MESH))
        return cps

    def start(srcs, outs, news, sems):
        for cp in sends(outs, sems):
            cp.start()

    def forwards(outs, sems):
        fsend, frecv = sems[2], sems[3]
        x, y, c, me, others, okey = _place()
        cps = []
        for t in range(n):
            if split[t]:
                for j in range(3):
                    landed = rows(t, outs[t].at[okey[j]], c)
                    cps.append(pltpu.make_async_remote_copy(
                        src_ref=landed, dst_ref=landed, send_sem=fsend.at[t, j], recv_sem=frecv.at[t, j],
                        device_id=(x, y, 1 - c), device_id_type=MESH))
        return cps

    def middle(srcs, outs, news, sems):
        send, recv = sems[0], sems[1]
        x, y, c, me, others, okey = _place()
        for t in range(n):
            for j in range(3):
                landed = rows(t, outs[t].at[okey[j]], c)
                pltpu.make_async_remote_copy(
                    src_ref=landed, dst_ref=landed, send_sem=send.at[t, j], recv_sem=recv.at[t, j],
                    device_id=(x, y, 1 - c), device_id_type=MESH).wait_recv()
        for cp in forwards(outs, sems):
            cp.start()

    def finish(srcs, outs, news, sems):
        fsend, frecv = sems[2], sems[3]
        x, y, c, me, others, okey = _place()
        for t in range(n):
            if split[t]:
                for j in range(3):
                    theirs = rows(t, outs[t].at[okey[j]], 1 - c)
                    pltpu.make_async_remote_copy(
                        src_ref=theirs, dst_ref=theirs, send_sem=fsend.at[t, j], recv_sem=frecv.at[t, j],
                        device_id=(x, y, 1 - c), device_id_type=MESH).wait_recv()
        for cp in sends(outs, sems) + forwards(outs, sems):
            cp.wait_send()

    return _Comm([], placed, [], [pltpu.SemaphoreType.DMA((n, 3))] * 4, start, finish, middle)


def _pair_exchange_comm(gs):
    n = len(gs)

    def copies(ins, outs, sems):
        send, recv = sems
        x, y, c, _, _, _ = _place()
        cps = []
        for t in range(n):
            rh = gs[t].shape[1] // 2
            cps.append(pltpu.make_async_remote_copy(
                src_ref=ins[t].at[:, pl.ds((1 - c) * rh, rh), :], dst_ref=outs[t],
                send_sem=send.at[t], recv_sem=recv.at[t], device_id=(x, y, 1 - c), device_id_type=MESH))
        return cps

    def start(ins, zones, outs, sems):
        for cp in copies(ins, outs, sems):
            cp.start()

    def finish(ins, zones, outs, sems):
        for cp in copies(ins, outs, sems):
            cp.wait()

    news = [jax.ShapeDtypeStruct((4, g.shape[1] // 2, g.shape[2]), g.dtype) for g in gs]
    return _Comm(gs, [], news, [pltpu.SemaphoreType.DMA((n,))] * 2, start, finish)


def _chip_scatter_comm(ps, qs):
    n = len(ps)

    def sends(ins, outs, sems):
        send, recv = sems
        x, y, c, me, others, okey = _place()
        return [pltpu.make_async_remote_copy(
            src_ref=ins[t].at[okey[j]], dst_ref=outs[t].at[me],
            send_sem=send.at[t, j], recv_sem=recv.at[t, j], device_id=(ox, oy, c), device_id_type=MESH)
            for t in range(n) for j, (ox, oy) in enumerate(others)]

    def start(ins, outs, news, sems):
        for cp in sends(ins, outs, sems):
            cp.start()

    def finish(ins, outs, news, sems):
        send, recv = sems
        x, y, c, me, others, okey = _place()
        for t in range(n):
            for j in range(3):
                slot = outs[t].at[okey[j]]
                pltpu.make_async_remote_copy(
                    src_ref=slot, dst_ref=slot, send_sem=send.at[t, j], recv_sem=recv.at[t, j],
                    device_id=(x, y, c), device_id_type=MESH).wait_recv()
        for cp in sends(ins, outs, sems):
            cp.wait_send()

    return _Comm(ps, qs, [], [pltpu.SemaphoreType.DMA((n, 3))] * 2, start, finish)


def _half_exchange_comm(arrs, layers=None):
    n = len(arrs)
    items = [(t, layer) for t in range(n) for layer in (range(arrs[t].shape[0]) if layers is None else layers[t])]

    def sends(outs, sems):
        send, recv = sems
        x, y, c, _, _, _ = _place()
        cps = []
        for k, (t, layer) in enumerate(items):
            rh = arrs[t].shape[1] // 2
            mine = outs[t].at[layer, pl.ds(c * rh, rh), :]
            cps.append(pltpu.make_async_remote_copy(
                src_ref=mine, dst_ref=mine, send_sem=send.at[k], recv_sem=recv.at[k],
                device_id=(x, y, 1 - c), device_id_type=MESH))
        return cps

    def start(srcs, outs, news, sems):
        for cp in sends(outs, sems):
            cp.start()

    def finish(srcs, outs, news, sems):
        send, recv = sems
        x, y, c, _, _, _ = _place()
        for k, (t, layer) in enumerate(items):
            rh = arrs[t].shape[1] // 2
            theirs = outs[t].at[layer, pl.ds((1 - c) * rh, rh), :]
            pltpu.make_async_remote_copy(
                src_ref=theirs, dst_ref=theirs, send_sem=send.at[k], recv_sem=recv.at[k],
                device_id=(x, y, 1 - c), device_id_type=MESH).wait_recv()
        for cp in sends(outs, sems):
            cp.wait_send()

    return _Comm([], arrs, [], [pltpu.SemaphoreType.DMA((len(items),))] * 2, start, finish)


SMALL_COLS = 1024
SMALL_PIECE_ROWS = 48


def _allreduce_small(buf, comm=None):
    pr = SMALL_PIECE_ROWS
    flips = [(d >> 2 & 1, d >> 1 & 1, d & 1) for d in range(1, 8)]

    def body(x_ref, o_ref, rbuf, send1, recv1, send2, recv2):
        x, y, c = lax.axis_index("x"), lax.axis_index("y"), lax.axis_index("c")
        me = 4 * x + 2 * y + c
        peers = [(x ^ fx, y ^ fy, c ^ fc) for fx, fy, fc in flips]
        pid = [4 * px + 2 * py + pc for px, py, pc in peers]

        def piece(ref, p):
            return ref.at[pl.ds(pl.multiple_of(p * pr, 8), pr), :]

        cps = []
        for d in range(7):
            cp = pltpu.make_async_remote_copy(
                src_ref=piece(x_ref, pid[d]), dst_ref=rbuf.at[d + 1],
                send_sem=send1.at[d], recv_sem=recv1.at[d], device_id=peers[d], device_id_type=MESH)
            cp.start()
            cps.append(cp)
        acc = piece(x_ref, me)[...]
        for d in range(7):
            cps[d].wait_recv()
            acc = acc + rbuf[d + 1]
        piece(o_ref, me)[...] = acc
        out = []
        for d in range(7):
            cp = pltpu.make_async_remote_copy(
                src_ref=piece(o_ref, me), dst_ref=piece(o_ref, me),
                send_sem=send2.at[d], recv_sem=recv2.at[d], device_id=peers[d], device_id_type=MESH)
            cp.start()
            out.append(cp)
        for d in range(7):
            pltpu.make_async_remote_copy(
                src_ref=piece(o_ref, pid[d]), dst_ref=piece(o_ref, pid[d]),
                send_sem=send2.at[d], recv_sem=recv2.at[d], device_id=peers[d], device_id_type=MESH).wait_recv()
        for cp in cps + out:
            cp.wait_send()

    vm = pl.BlockSpec(memory_space=pltpu.VMEM)
    return _call(
        body, name="small_allreduce", grid=(), in_specs=[vm], out_specs=vm,
        out_shape=jax.ShapeDtypeStruct(buf.shape, F32),
        scratch_shapes=[pltpu.VMEM((8, pr, SMALL_COLS), F32)] + [pltpu.SemaphoreType.DMA((7,))] * 4,
        args=(buf,), comm=comm)


def _rows_of(shape):
    return -(-math.prod(shape) // (8 * SMALL_COLS)) * 8


def _pack(arrays, rows):
    parts = []
    for a in arrays:
        r = _rows_of(a.shape)
        parts.append(jnp.pad(a.reshape(-1), (0, r * SMALL_COLS - a.size)).reshape(r, SMALL_COLS))
    used = sum(p.shape[0] for p in parts)
    if rows > used:
        parts.append(jnp.zeros((rows - used, SMALL_COLS), F32))
    return jnp.concatenate(parts, axis=0)


def _unpack(buf, shapes):
    out, off = [], 0
    for s in shapes:
        r = _rows_of(s)
        out.append(buf[off:off + r].reshape(-1)[:math.prod(s)].reshape(s))
        off += r
    return out


BIG = ["sgu_w_in", "sgu_w_out", "attn_w_qkv", "attn_w_o", "ffn_w_up", "ffn_w_down"]
SMALL = ["mix_norm", "ffn_norm", "sgu_v_gain", "sgu_w_s", "sgu_b_s", "attn_q_gain", "attn_k_gain", "attn_sinks",
         "rel_bias", "ffn_conv_b"]
ORDER = ["mix_norm", "ffn_norm", "sgu_w_in", "sgu_v_gain", "sgu_w_s", "sgu_b_s", "sgu_w_out", "attn_w_qkv",
         "attn_q_gain", "attn_k_gain", "attn_sinks", "attn_w_o", "rel_bias", "ffn_w_up", "ffn_conv_w", "ffn_conv_b",
         "ffn_w_down"]


def kernel(x, mix_norm, ffn_norm, sgu_w_in, sgu_v_gain, sgu_w_s, sgu_b_s, sgu_w_out, attn_w_qkv, attn_q_gain, attn_k_gain, attn_sinks, attn_w_o, rel_bias, ffn_w_up, ffn_conv_w, ffn_conv_b, ffn_w_down, loss_target, m_mix_norm, m_ffn_norm, m_sgu_w_in, m_sgu_v_gain, m_sgu_w_s, m_sgu_b_s, m_sgu_w_out, m_attn_w_qkv, m_attn_q_gain, m_attn_k_gain, m_attn_sinks, m_attn_w_o, m_rel_bias, m_ffn_w_up, m_ffn_conv_w, m_ffn_conv_b, m_ffn_w_down, v_mix_norm, v_ffn_norm, v_sgu_w_in, v_sgu_v_gain, v_sgu_w_s, v_sgu_b_s, v_sgu_w_out, v_attn_w_qkv, v_attn_q_gain, v_attn_k_gain, v_attn_sinks, v_attn_w_o, v_rel_bias, v_ffn_w_up, v_ffn_conv_w, v_ffn_conv_b, v_ffn_w_down):
    w = dict(mix_norm=mix_norm, ffn_norm=ffn_norm, sgu_w_in=sgu_w_in, sgu_v_gain=sgu_v_gain, sgu_w_s=sgu_w_s,
             sgu_b_s=sgu_b_s, sgu_w_out=sgu_w_out, attn_w_qkv=attn_w_qkv, attn_q_gain=attn_q_gain,
             attn_k_gain=attn_k_gain, attn_sinks=attn_sinks, attn_w_o=attn_w_o, rel_bias=rel_bias, ffn_w_up=ffn_w_up,
             ffn_conv_w=ffn_conv_w, ffn_conv_b=ffn_conv_b, ffn_w_down=ffn_w_down)
    mom = dict(mix_norm=m_mix_norm, ffn_norm=m_ffn_norm, sgu_w_in=m_sgu_w_in, sgu_v_gain=m_sgu_v_gain,
               sgu_w_s=m_sgu_w_s, sgu_b_s=m_sgu_b_s, sgu_w_out=m_sgu_w_out, attn_w_qkv=m_attn_w_qkv,
               attn_q_gain=m_attn_q_gain, attn_k_gain=m_attn_k_gain, attn_sinks=m_attn_sinks, attn_w_o=m_attn_w_o,
               rel_bias=m_rel_bias, ffn_w_up=m_ffn_w_up, ffn_conv_w=m_ffn_conv_w, ffn_conv_b=m_ffn_conv_b,
               ffn_w_down=m_ffn_w_down)
    var = dict(mix_norm=v_mix_norm, ffn_norm=v_ffn_norm, sgu_w_in=v_sgu_w_in, sgu_v_gain=v_sgu_v_gain,
               sgu_w_s=v_sgu_w_s, sgu_b_s=v_sgu_b_s, sgu_w_out=v_sgu_w_out, attn_w_qkv=v_attn_w_qkv,
               attn_q_gain=v_attn_q_gain, attn_k_gain=v_attn_k_gain, attn_sinks=v_attn_sinks, attn_w_o=v_attn_w_o,
               rel_bias=v_rel_bias, ffn_w_up=v_ffn_w_up, ffn_conv_w=v_ffn_conv_w, ffn_conv_b=v_ffn_conv_b,
               ffn_w_down=v_ffn_w_down)
    chip = 2 * lax.axis_index("x") + lax.axis_index("y")
    core = lax.axis_index("c")

    where = jnp.stack([core, chip]).astype(jnp.int32)
    names = ["sgu_w_in", "sgu_w_out", "attn_w_qkv", "attn_w_o", "ffn_w_up0", "ffn_w_up1", "ffn_w_down0", "ffn_w_down1"]
    shards = [(sgu_w_in, 0), (sgu_w_out, 0), (attn_w_qkv, 0), (attn_w_o, 0), (ffn_w_up, 0), (ffn_w_up, 1),
              (ffn_w_down, 0), (ffn_w_down, 1)]
    T = {nm: _place_shard(s, l, where, BF16, "place_" + nm) for (s, l), nm in zip(shards, names)}
    for l in range(2):
        T["conv_w%d" % l] = _place_shard(ffn_conv_w, l, where, F32, "place_conv_w%d" % l)

    def gather(keys):
        items = []
        for k in keys:
            k, lo, hi = (k, 0, None) if isinstance(k, str) else k
            items.append((T[k], lo, T[k].shape[1] if hi is None else hi, not k.startswith("conv")))
        return _gather_comm(items)

    def gathered(keys, res):
        for k, a in zip(keys, res[0]):
            T[k if isinstance(k, str) else k[0]] = a

    D = x.shape[2]
    first = ["sgu_w_in", "conv_w0", "conv_w1"]
    gathered(first, _run_comm(gather(first), "gather_first"))
    unshard_cols = lambda a: jnp.transpose(a, (1, 0, 2)).reshape(a.shape[1], -1)
    cw = [unshard_cols(T["conv_w0"]), unshard_cols(T["conv_w1"])]
    cb = ffn_conv_b
    flat = lambda k: T[k].reshape(-1, D)
    x2, target = x[0], loss_target[0]
    bucket = jnp.asarray(_rel_buckets_flat())
    wtril, wtrilT = _sgu_prep(sgu_w_s[0])
    bT = sgu_b_s[0].T
    bias = _rel_bias_expand(rel_bias, bucket).reshape(N_KV_HEADS, KV_GROUP * CHUNK, 2 * CHUNK)
    qg2 = jnp.tile(attn_q_gain, (1, 2))
    kg2 = jnp.tile(attn_k_gain, (1, 2))
    sinks = attn_sinks.reshape(N_HEADS)
    mix0, mix1 = mix_norm[0:1], mix_norm[1:2]
    fn0, fn1 = ffn_norm[0:1], ffn_norm[1:2]

    ks = ["sgu_w_out", ("ffn_w_up0", 0, 256)]
    (hn0, z), r = _norm_matmul(x2, mix0, T["sgu_w_in"], "sgu_in", comm=gather(ks))
    gathered(ks, r)
    ks = [("ffn_w_up0", 256, None)]
    (yp, h1, cdf), r = _sgu_fwd(z, x2, sgu_v_gain, wtril, bT, flat("sgu_w_out"), comm=gather(ks))
    gathered(ks, r)
    ks = ["ffn_w_down0", "attn_w_qkv"]
    (hf0, a0), r = _norm_matmul(h1, fn0, T["ffn_w_up0"], "ffn0_up", comm=gather(ks))
    gathered(ks, r)
    ks = ["attn_w_o", ("ffn_w_up1", 0, 640)]
    (f0, h2), r = _ffn_fwd(a0, h1, cw[0], cb[0:1], flat("ffn_w_down0"), "ffn0_fwd", comm=gather(ks))
    gathered(ks, r)
    (hn1, qkv), _ = _norm_matmul(h2, mix1, T["attn_w_qkv"], "attn_qkv")
    ks = [("ffn_w_up1", 640, None), "ffn_w_down1"]
    o, r = _attn_fwd(qkv, qg2, kg2, sinks, bias, comm=gather(ks))
    gathered(ks, r)
    h3, _ = _matmul_res(o, flat("attn_w_o"), h2, "attn_out")
    (hf1, a1), _ = _norm_matmul(h3, fn1, T["ffn_w_up1"], "ffn1_up")
    (f1, dh4, loss_local), _ = _ffn_fwd(a1, h3, cw[1], cb[1:2], flat("ffn_w_down1"), "ffn1_fwd", target=target)

    G, RS, PQ, QD, halves, grads = {}, {}, {}, {}, {}, {}
    dest_of = {"sgu_w_in": ("sgu_w_in", 0), "sgu_w_out": ("sgu_w_out", 0), "attn_w_qkv": ("attn_w_qkv", 0),
               "attn_w_o": ("attn_w_o", 0), "ffn_w_up0": ("ffn_w_up", 0), "ffn_w_up1": ("ffn_w_up", 1),
               "ffn_w_down0": ("ffn_w_down", 0), "ffn_w_down1": ("ffn_w_down", 1)}

    def px(keys):
        return _pair_exchange_comm([G[k] for k in keys])

    def px_done(keys, res):
        for k, a in zip(keys, res[1]):
            RS[k] = a
            PQ[k] = _pair_add(G[k], a, where, "pair_add_" + k)

    def sc(keys):
        return _chip_scatter_comm([PQ[k][0] for k in keys], [PQ[k][1] for k in keys])

    def sc_done(keys, res):
        for k, a in zip(keys, res[0]):
            wk, layer = dest_of[k]
            halves[wk] = _sum_chips(a, where, halves.get(wk), layer, w[wk].shape, "sum_chips_" + k)

    nup = ffn_w_up.shape[2]
    ndown = ffn_w_down.shape[1]
    rows4 = lambda a: a.reshape(4, ndown, D)
    (dc1, dcw1, dcb1), _ = _ffn_bwd_dc(dh4, a1, cw[1], cb[1:2], flat("ffn_w_down1"), "ffn1_bwd_dc")
    gw, _ = _matmul_tn(f1, dh4, "ffn1_dw_down", ka=2 * ndown, nb=D, out_dtype=BF16)
    G["ffn_w_down1"] = rows4(gw)
    (dh3, dfn1, da1), _ = _bwd_norm(dc1, T["ffn_w_up1"], h3, fn1, dh4, "ffn1_bwd_in", conv_w=cw[1])
    G["ffn_w_up1"], _ = _matmul_tn(hf1, da1, "ffn1_dw_up", ka=D, nb=nup, out_dtype=BF16)
    ks = ["ffn_w_down1", "ffn_w_up1"]
    G["attn_w_o"], r = _matmul_tn(o, dh3, "attn_dw_o", ka=QW // 4, nb=D, out_dtype=BF16, comm=px(ks))
    px_done(ks, r)
    do, _ = _matmul_nt(dh3, flat("attn_w_o"), "attn_bwd_out")
    (dqkv, dbias, dqg, dkg, dsink), r = _attn_bwd(qkv, do, qg2, kg2, sinks, bias, comm=sc(ks))
    sc_done(ks, r)
    G["attn_w_qkv"], _ = _matmul_tn(hn1, dqkv, "attn_dw_qkv", ka=D, nb=dqkv.shape[1] // 4, out_dtype=BF16)
    ks = ["attn_w_o", "attn_w_qkv"]
    (dh2, dmix1), r = _bwd_norm(dqkv, T["attn_w_qkv"], h2, mix1, dh3, "attn_bwd_in", comm=px(ks))
    px_done(ks, r)
    (dc0, dcw0, dcb0), r = _ffn_bwd_dc(dh2, a0, cw[0], cb[0:1], flat("ffn_w_down0"), "ffn0_bwd_dc", comm=sc(ks))
    sc_done(ks, r)
    gw, _ = _matmul_tn(f0, dh2, "ffn0_dw_down", ka=2 * ndown, nb=D, out_dtype=BF16)
    G["ffn_w_down0"] = rows4(gw)
    (dh1, dfn0, da0), _ = _bwd_norm(dc0, T["ffn_w_up0"], h1, fn0, dh2, "ffn0_bwd_in", conv_w=cw[0])
    G["ffn_w_up0"], _ = _matmul_tn(hf0, da0, "ffn0_dw_up", ka=D, nb=nup, out_dtype=BF16)
    ks = ["ffn_w_down0", "ffn_w_up0"]
    G["sgu_w_out"], r = _matmul_tn(yp, dh1, "sgu_dw_out", ka=yp.shape[1] // 4, nb=D, out_dtype=BF16, comm=px(ks))
    px_done(ks, r)
    both = [sc(ks), px(["sgu_w_out"])]
    (dz, dws, dbT, dvg), r = _sgu_bwd(dh1, z, cdf, sgu_v_gain, wtril, wtrilT, bT, flat("sgu_w_out"),
                                      comm=_join(both))
    r = _split(both, r)
    sc_done(ks, r[0])
    px_done(["sgu_w_out"], r[1])
    done = ["attn_w_qkv", "attn_w_o", "ffn_w_up", "ffn_w_down"]
    both = [sc(["sgu_w_out"]), _half_exchange_comm([halves[k] for k in done])]
    G["sgu_w_in"], r = _matmul_tn(hn0, dz, "sgu_dw_in", ka=D, nb=dz.shape[1] // 4, out_dtype=BF16, comm=_join(both))
    r = _split(both, r)
    sc_done(["sgu_w_out"], r[0])
    for k, a in zip(done, r[1][0]):
        grads[k] = a
    px_done(["sgu_w_in"], _run_comm(px(["sgu_w_in"]), "grad_last_pair_exchange"))
    (grad_x, dmix0), r = _bwd_norm(dz, T["sgu_w_in"], x2, mix0, dh1, "sgu_bwd_in", comm=sc(["sgu_w_in"]))
    sc_done(["sgu_w_in"], r)

    g = dict(mix_norm=jnp.concatenate([dmix0, dmix1], axis=0), ffn_norm=jnp.concatenate([dfn0, dfn1], axis=0),
             sgu_v_gain=dvg, sgu_w_s=dws, sgu_b_s=dbT.T, attn_q_gain=dqg[:, :HEAD_DIM], attn_k_gain=dkg[:, :HEAD_DIM],
             attn_sinks=dsink[:, :N_KV_HEADS].T.reshape(1, N_HEADS),
             rel_bias=_rel_bias_reduce(dbias.reshape(N_HEADS, CHUNK * 2 * CHUNK), bucket),
             ffn_conv_b=jnp.concatenate([dcb0, dcb1], axis=0))
    small_list = [g[k].reshape(w[k].shape) for k in SMALL] + [jnp.stack([dcw0, dcw1]), loss_local]
    small_shapes = [a.shape for a in small_list]
    done = ["sgu_w_in", "sgu_w_out"]
    red, r = _allreduce_small(_pack(small_list, 8 * SMALL_PIECE_ROWS),
                              comm=_half_exchange_comm([halves[k] for k in done]))
    for k, a in zip(done, r[0]):
        grads[k] = a
    red = _unpack(red, small_shapes)
    for k, a in zip(SMALL, red):
        grads[k] = a
    grads["ffn_conv_w"] = lax.dynamic_slice_in_dim(red[-2], chip * ffn_conv_w.shape[2], ffn_conv_w.shape[2], axis=2)
    loss = red[-1][0, 0]

    delta, new_m, new_v = {}, {}, {}
    two = lambda a: a.reshape(-1, a.shape[-1])
    for k in BIG:
        (d2, m2, v2, g2), _ = _adamw(two(grads[k]), two(w[k]), two(mom[k]), two(var[k]), "adamw_" + k)
        delta[k], new_m[k], new_v[k], grads[k] = (a.reshape(w[k].shape) for a in (d2, m2, v2, g2))
    sm = SMALL + ["ffn_conv_w"]
    sm_shapes = [w[k].shape for k in sm]
    rows = sum(_rows_of(s) for s in sm_shapes)
    (d2, m2, v2, _), _ = _adamw(_pack([grads[k] for k in sm], rows), _pack([w[k] for k in sm], rows),
                                _pack([mom[k] for k in sm], rows), _pack([var[k] for k in sm], rows), "adamw_small")
    for dst, buf in ((delta, d2), (new_m, m2), (new_v, v2)):
        for k, a in zip(sm, _unpack(buf, sm_shapes)):
            dst[k] = a

    return (loss, grad_x[None], *[grads[k] for k in ORDER], *[delta[k] for k in ORDER],
            *[new_m[k] for k in ORDER], *[new_v[k] for k in ORDER])
```

```python
import functools
import math

import numpy as np
import jax
import jax.numpy as jnp
from jax import lax
from jax.experimental import pallas as pl
from jax.experimental.pallas import tpu as pltpu

F32 = jnp.float32
BF16 = jnp.bfloat16

EPS = 1e-6
CHUNK = 128
SGU_GROUPS = 16
HEAD_DIM = 64
N_HEADS = 16
N_KV_HEADS = 4
KV_GROUP = N_HEADS // N_KV_HEADS
REL_BUCKETS = 32
REL_MAX_DIST = 128
LANES = 128
HALO = 16

ADAM_LR = 0.001
ADAM_B1 = 0.9
ADAM_B2 = 0.999
ADAM_EPS = 1e-08
ADAM_WD = 0.01
ADAM_STEP = 10

VMEM_LIMIT_V7X = 56 * 1024 * 1024

_SQRT_HALF = math.sqrt(0.5)
_INV_SQRT_2PI = 1.0 / math.sqrt(2.0 * math.pi)


def _cp(sem):
    return pltpu.CompilerParams(dimension_semantics=sem, vmem_limit_bytes=VMEM_LIMIT_V7X)


def _resident(shape):
    nd = len(shape)
    return pl.BlockSpec(shape, lambda *_: (0,) * nd, pipeline_mode=pl.Buffered(1))


class _Comm:
    def __init__(self, srcs, zones, news, sems, start, finish, middle=None):
        self.srcs, self.zones, self.news, self.sems = list(srcs), list(zones), list(news), list(sems)
        self.start, self.finish = start, finish
        self.middle = middle if middle is not None else (lambda srcs, zones, news, sems: None)


def _join(comms):
    comms = [c for c in comms if c is not None]
    if not comms:
        return None

    def part(seq, attr):
        out, k = [], 0
        for c in comms:
            n = len(getattr(c, attr))
            out.append(seq[k:k + n])
            k += n
        return out

    def run(which):
        def f(srcs, zones, news, sems):
            for c, a, b, d, e in zip(comms, part(srcs, "srcs"), part(zones, "zones"), part(news, "news"), part(sems, "sems")):
                getattr(c, which)(a, b, d, e)
        return f

    cat = lambda attr: [v for c in comms for v in getattr(c, attr)]
    return _Comm(cat("srcs"), cat("zones"), cat("news"), cat("sems"), run("start"), run("finish"), run("middle"))


def _split(comms, res):
    zones, news = res
    out, kz, kn = [], 0, 0
    for c in comms:
        out.append((zones[kz:kz + len(c.zones)], news[kn:kn + len(c.news)]))
        kz += len(c.zones)
        kn += len(c.news)
    return out


def _call(body, *, name, grid, in_specs, out_specs, out_shape, args, scratch_shapes=(), sem=None, comm=None):
    if comm is None:
        res = pl.pallas_call(body, name=name, grid=grid, in_specs=in_specs, out_specs=out_specs, out_shape=out_shape,
                             scratch_shapes=list(scratch_shapes), compiler_params=_cp(sem))(*args)
        return res, None
    single = not isinstance(out_shape, (list, tuple))
    out_specs_l = [out_specs] if single else list(out_specs)
    out_shape_l = [out_shape] if single else list(out_shape)
    n_in, n_out, n_scr = len(in_specs), len(out_shape_l), len(scratch_shapes)
    ns, nz, nn = len(comm.srcs), len(comm.zones), len(comm.news)

    def wrapped(*refs):
        k = n_in
        ins, srcs = refs[:k], refs[k:k + ns]
        k += ns + nz
        outs, zones, news = refs[k:k + n_out], refs[k + n_out:k + n_out + nz], refs[k + n_out + nz:k + n_out + nz + nn]
        k += n_out + nz + nn
        scr, sems = refs[k:k + n_scr], refs[k + n_scr:]
        if not grid:
            comm.start(srcs, zones, news, sems)
            body(*ins, *outs, *scr)
            comm.middle(srcs, zones, news, sems)
            comm.finish(srcs, zones, news, sems)
            return
        first = functools.reduce(jnp.logical_and, [pl.program_id(a) == 0 for a in range(len(grid))])
        last = functools.reduce(jnp.logical_and, [pl.program_id(a) == grid[a] - 1 for a in range(len(grid))])
        early = len(grid) == 1 and grid[0] >= 2
        mid_step = grid[0] - (2 if grid[0] >= 8 else 1)

        @pl.when(first)
        def _():
            comm.start(srcs, zones, news, sems)

        if early:
            @pl.when(pl.program_id(0) == mid_step)
            def _():
                comm.middle(srcs, zones, news, sems)

        body(*ins, *outs, *scr)

        @pl.when(last)
        def _():
            if not early:
                comm.middle(srcs, zones, news, sems)
            comm.finish(srcs, zones, news, sems)

    res = pl.pallas_call(
        wrapped, name=name, grid=grid,
        in_specs=list(in_specs) + [ANY] * (ns + nz), out_specs=out_specs_l + [ANY] * (nz + nn),
        out_shape=out_shape_l + [jax.ShapeDtypeStruct(z.shape, z.dtype) for z in comm.zones] + comm.news,
        input_output_aliases={n_in + ns + i: n_out + i for i in range(nz)},
        scratch_shapes=list(scratch_shapes) + comm.sems,
        compiler_params=_cp(("arbitrary",) * len(grid)),
    )(*args, *comm.srcs, *comm.zones)
    main = res[0] if single else list(res[:n_out])
    return main, (list(res[n_out:n_out + nz]), list(res[n_out + nz:]))


def _run_comm(comm, name):
    ns, nz, nn = len(comm.srcs), len(comm.zones), len(comm.news)

    def body(*refs):
        srcs, zones, news, sems = refs[:ns], refs[ns + nz:ns + 2 * nz], refs[ns + 2 * nz:ns + 2 * nz + nn], refs[ns + 2 * nz + nn:]
        comm.start(srcs, zones, news, sems)
        comm.middle(srcs, zones, news, sems)
        comm.finish(srcs, zones, news, sems)

    res = pl.pallas_call(
        body, name=name, in_specs=[ANY] * (ns + nz), out_specs=[ANY] * (nz + nn),
        out_shape=[jax.ShapeDtypeStruct(z.shape, z.dtype) for z in comm.zones] + comm.news,
        input_output_aliases={ns + i: i for i in range(nz)}, scratch_shapes=comm.sems,
    )(*comm.srcs, *comm.zones)
    return list(res[:nz]), list(res[nz:])


def _dot(a, b):
    return jnp.dot(a, b, preferred_element_type=F32)


def _dot_nt(a, b):
    return lax.dot_general(a, b, (((1,), (1,)), ((), ())), preferred_element_type=F32)


def _dot_tn(a, b):
    return lax.dot_general(a, b, (((0,), (0,)), ((), ())), preferred_element_type=F32)


def _normal_cdf(x):
    return 0.5 * (1.0 + lax.erf(x * _SQRT_HALF))


def _gelu_and_grad(x, cdf):
    return x * cdf, cdf + x * jnp.exp(-0.5 * x * x) * _INV_SQRT_2PI


def _sigmoid(x):
    return 0.5 * jnp.tanh(0.5 * x) + 0.5


def _rms_bwd(dy, x, gain):
    r = lax.rsqrt(jnp.mean(x * x, axis=-1, keepdims=True) + EPS)
    xhat = x * r
    gdy = dy * gain
    dx = r * (gdy - xhat * jnp.mean(gdy * xhat, axis=-1, keepdims=True))
    return dx, jnp.sum(dy * xhat, axis=0, keepdims=True)


def _norm_matmul(x, gain, w4, name, comm=None, tm=1024):
    S, D = x.shape
    nsh, _, ns = w4.shape
    tm = min(tm, S)

    def body(x_ref, g_ref, w_ref, hn_ref, o_ref):
        xf = x_ref[...]
        r = lax.rsqrt(jnp.mean(xf * xf, axis=-1, keepdims=True) + EPS)
        hn = (xf * r * g_ref[...]).astype(BF16)
        hn_ref[...] = hn
        for j in range(nsh):
            o_ref[:, j * ns:(j + 1) * ns] = _dot(hn, w_ref[j]).astype(BF16)

    return _call(
        body, name=name, grid=(S // tm,),
        in_specs=[pl.BlockSpec((tm, D), lambda i: (i, 0)), _resident((1, D)), _resident(w4.shape)],
        out_specs=[pl.BlockSpec((tm, D), lambda i: (i, 0)), pl.BlockSpec((tm, nsh * ns), lambda i: (i, 0))],
        out_shape=[jax.ShapeDtypeStruct((S, D), BF16), jax.ShapeDtypeStruct((S, nsh * ns), BF16)],
        sem=("parallel",), args=(x, gain, w4), comm=comm)


def _matmul_res(a, w, res, name, comm=None, tm=1024):
    S, K = a.shape
    N = w.shape[1]
    tm = min(tm, S)

    def body(a_ref, w_ref, r_ref, o_ref):
        o_ref[...] = r_ref[...] + _dot(a_ref[...], w_ref[...])

    return _call(
        body, name=name, grid=(S // tm,),
        in_specs=[pl.BlockSpec((tm, K), lambda i: (i, 0)), _resident(w.shape), pl.BlockSpec((tm, N), lambda i: (i, 0))],
        out_specs=pl.BlockSpec((tm, N), lambda i: (i, 0)),
        out_shape=jax.ShapeDtypeStruct((S, N), F32),
        sem=("parallel",), args=(a, w, res), comm=comm)


def _matmul_nt(dh, w, name, comm=None, tm=1024):
    S, N = dh.shape
    K = w.shape[0]
    tm = min(tm, S)

    def body(d_ref, w_ref, o_ref):
        o_ref[...] = _dot_nt(d_ref[...].astype(BF16), w_ref[...]).astype(BF16)

    return _call(
        body, name=name, grid=(S // tm,),
        in_specs=[pl.BlockSpec((tm, N), lambda i: (i, 0)), _resident(w.shape)],
        out_specs=pl.BlockSpec((tm, K), lambda i: (i, 0)),
        out_shape=jax.ShapeDtypeStruct((S, K), BF16),
        sem=("parallel",), args=(dh, w), comm=comm)


def _matmul_tn(a, b, name, *, ka, nb, out_dtype, comm=None, ts=1024):
    S, KA = a.shape
    NB = b.shape[1]
    ts = min(ts, S)
    J = max(KA // ka, NB // nb)
    a_map = (lambda j, s: (s, j)) if KA // ka > 1 else (lambda j, s: (s, 0))
    b_map = (lambda j, s: (s, j)) if NB // nb > 1 else (lambda j, s: (s, 0))
    last = S // ts - 1

    def body(a_ref, b_ref, o_ref, acc):
        s = pl.program_id(1)

        @pl.when(s == 0)
        def _():
            acc[...] = jnp.zeros_like(acc)

        acc[...] += _dot_tn(a_ref[...].astype(BF16), b_ref[...].astype(BF16))

        @pl.when(s == last)
        def _():
            o_ref[0] = acc[...].astype(out_dtype)

    return _call(
        body, name=name, grid=(J, S // ts),
        in_specs=[pl.BlockSpec((ts, ka), a_map), pl.BlockSpec((ts, nb), b_map)],
        out_specs=pl.BlockSpec((1, ka, nb), lambda j, s: (j, 0, 0)),
        out_shape=jax.ShapeDtypeStruct((J, ka, nb), out_dtype),
        scratch_shapes=[pltpu.VMEM((ka, nb), F32)],
        sem=("parallel", "arbitrary"), args=(a, b), comm=comm)


def _sgu_prep(w_s):
    G = w_s.shape[0]

    def body(w_ref, t_ref, tt_ref):
        tri = lax.broadcasted_iota(jnp.int32, (CHUNK, CHUNK), 0) >= lax.broadcasted_iota(jnp.int32, (CHUNK, CHUNK), 1)
        for g in range(G):
            t = jnp.where(tri, w_ref[g], 0.0)
            t_ref[g] = t.astype(BF16)
            tt_ref[g] = t.T.astype(BF16)

    return pl.pallas_call(
        body, name="sgu_prep",
        out_shape=[jax.ShapeDtypeStruct(w_s.shape, BF16), jax.ShapeDtypeStruct(w_s.shape, BF16)],
        compiler_params=_cp(None),
    )(w_s)


def _sgu_fwd(z, x, vg, wtril, bT, wout, comm=None, tm=512):
    S = z.shape[0]
    W = z.shape[1] // 2
    D = x.shape[1]
    tm = min(tm, S)

    def body(z_ref, x_ref, vg_ref, wt_ref, bT_ref, wo_ref, yp_ref, h_ref, cdf_ref):
        def chunk(c, carry):
            r0 = pl.multiple_of(c * CHUNK, CHUNK)
            zc = z_ref[pl.ds(r0, CHUNK), :].astype(F32)
            cdf = _normal_cdf(zc)
            cdf_ref[pl.ds(r0, CHUNK), :] = cdf.astype(BF16)
            u = zc[:, :W] * cdf[:, :W]
            v = zc[:, W:] * cdf[:, W:]
            rv = lax.rsqrt(jnp.mean(v * v, axis=-1, keepdims=True) + EPS)
            vn = (v * rv * vg_ref[...]).astype(BF16)
            for g in range(SGU_GROUPS):
                sl = slice(g * LANES, (g + 1) * LANES)
                s = _dot(wt_ref[g], vn[:, sl]) + bT_ref[:, g:g + 1]
                yp_ref[pl.ds(r0, CHUNK), sl] = (u[:, sl] * s).astype(BF16)
            return carry

        lax.fori_loop(0, tm // CHUNK, chunk, 0)
        h_ref[...] = x_ref[...] + _dot(yp_ref[...], wo_ref[...])

    return _call(
        body, name="sgu_fwd", grid=(S // tm,),
        in_specs=[pl.BlockSpec((tm, 2 * W), lambda i: (i, 0)), pl.BlockSpec((tm, D), lambda i: (i, 0)),
                  _resident((1, W)), _resident(wtril.shape), _resident(bT.shape), _resident(wout.shape)],
        out_specs=[pl.BlockSpec((tm, W), lambda i: (i, 0)), pl.BlockSpec((tm, D), lambda i: (i, 0)),
                   pl.BlockSpec((tm, 2 * W), lambda i: (i, 0))],
        out_shape=[jax.ShapeDtypeStruct((S, W), BF16), jax.ShapeDtypeStruct((S, D), F32),
                   jax.ShapeDtypeStruct((S, 2 * W), BF16)],
        sem=("parallel",), args=(z, x, vg, wtril, bT, wout), comm=comm)


def _sgu_bwd(dh, z, cdf, vg, wtril, wtrilT, bT, wout, comm=None, tm=512):
    S = z.shape[0]
    W = z.shape[1] // 2
    D = dh.shape[1]
    G = SGU_GROUPS
    tm = min(tm, S)
    last = S // tm - 1

    def body(dh_ref, z_ref, cdf_ref, vg_ref, wt_ref, wtT_ref, bT_ref, wo_ref,
             dz_ref, dws_ref, dbT_ref, dvg_ref, dyp_s, du_s, dvn_s, dsacc):
        i = pl.program_id(0)

        @pl.when(i == 0)
        def _():
            dws_ref[...] = jnp.zeros_like(dws_ref)
            dvg_ref[...] = jnp.zeros_like(dvg_ref)
            dsacc[...] = jnp.zeros_like(dsacc)

        dyp_s[...] = _dot_nt(dh_ref[...].astype(BF16), wo_ref[...])
        tri = lax.broadcasted_iota(jnp.int32, (CHUNK, CHUNK), 0) >= lax.broadcasted_iota(jnp.int32, (CHUNK, CHUNK), 1)

        def chunk(c, carry):
            r0 = pl.multiple_of(c * CHUNK, CHUNK)
            zc = z_ref[pl.ds(r0, CHUNK), :].astype(F32)
            cdf = cdf_ref[pl.ds(r0, CHUNK), :].astype(F32)
            u, gu = _gelu_and_grad(zc[:, :W], cdf[:, :W])
            v, gv = _gelu_and_grad(zc[:, W:], cdf[:, W:])
            rv = lax.rsqrt(jnp.mean(v * v, axis=-1, keepdims=True) + EPS)
            vhat = v * rv
            vgain = vg_ref[...]
            vn = (vhat * vgain).astype(BF16)
            dyp = dyp_s[pl.ds(r0, CHUNK), :]
            for g in range(G):
                sl = slice(g * LANES, (g + 1) * LANES)
                vng = vn[:, sl]
                s = _dot(wt_ref[g], vng) + bT_ref[:, g:g + 1]
                ds = dyp[:, sl] * u[:, sl]
                du_s[:, sl] = dyp[:, sl] * s
                dsb = ds.astype(BF16)
                dvn_s[:, sl] = _dot(wtT_ref[g], dsb)
                dws_ref[g] += jnp.where(tri, _dot_nt(dsb, vng), 0.0)
                dsacc[g] += ds
            dvn = dvn_s[...]
            dvg_ref[...] += jnp.sum(dvn * vhat, axis=0, keepdims=True)
            gdy = dvn * vgain
            dv = rv * (gdy - vhat * jnp.mean(gdy * vhat, axis=-1, keepdims=True))
            dz_ref[pl.ds(r0, CHUNK), :W] = (du_s[...] * gu).astype(BF16)
            dz_ref[pl.ds(r0, CHUNK), W:] = (dv * gv).astype(BF16)
            return carry

        lax.fori_loop(0, tm // CHUNK, chunk, 0)

        @pl.when(i == last)
        def _():
            for g in range(G):
                dbT_ref[:, g:g + 1] = jnp.sum(dsacc[g], axis=1, keepdims=True)

    return _call(
        body, name="sgu_bwd", grid=(S // tm,),
        in_specs=[pl.BlockSpec((tm, D), lambda i: (i, 0)), pl.BlockSpec((tm, 2 * W), lambda i: (i, 0)),
                  pl.BlockSpec((tm, 2 * W), lambda i: (i, 0)),
                  _resident((1, W)), _resident(wtril.shape), _resident(wtrilT.shape), _resident(bT.shape),
                  _resident(wout.shape)],
        out_specs=[pl.BlockSpec((tm, 2 * W), lambda i: (i, 0)),
                   pl.BlockSpec((G, CHUNK, CHUNK), lambda i: (0, 0, 0)),
                   pl.BlockSpec((CHUNK, G), lambda i: (0, 0)),
                   pl.BlockSpec((1, W), lambda i: (0, 0))],
        out_shape=[jax.ShapeDtypeStruct((S, 2 * W), BF16), jax.ShapeDtypeStruct((G, CHUNK, CHUNK), F32),
                   jax.ShapeDtypeStruct((CHUNK, G), F32), jax.ShapeDtypeStruct((1, W), F32)],
        scratch_shapes=[pltpu.VMEM((tm, W), F32), pltpu.VMEM((CHUNK, W), F32), pltpu.VMEM((CHUNK, W), F32),
                        pltpu.VMEM((G, CHUNK, CHUNK), F32)],
        sem=("arbitrary",), args=(dh, z, cdf, vg, wtril, wtrilT, bT, wout), comm=comm)


def _conv_taps(a32, r0, R, reuse=False):
    base = r0 + HALO
    if not reuse:
        return a32[base:base + R, :], a32[base - 1:base - 1 + R, :], a32[base - 2:base - 2 + R, :]
    X = a32[base - 8:base + R, :]
    return X[8:], pltpu.roll(X, 1, 0)[8:], pltpu.roll(X, 2, 0)[8:]


def _ffn_fwd(a, h_in, cw, cb, wdown, name, comm=None, target=None, tm=512, R=64):
    S, C = a.shape
    F = C // 2
    D = h_in.shape[1]
    tm = min(tm, S)
    head = target is not None

    def body(a_ref, halo_ref, h_ref, cw_ref, cb_ref, wd_ref, *rest):
        if head:
            t_ref, f_ref, dy_ref, l_ref, a32 = rest
        else:
            f_ref, ho_ref, a32 = rest
        i = pl.program_id(0)
        a32[0:HALO, :] = jnp.where(i > 0, halo_ref[...].astype(F32), 0.0)
        a32[HALO:, :] = a_ref[...].astype(F32)

        for r0 in range(0, tm, R):
            a0, a1, a2 = _conv_taps(a32, r0, R)
            cpre = cw_ref[0:1, :] * a2 + cw_ref[1:2, :] * a1 + cw_ref[2:3, :] * a0 + cb_ref[...]
            g = cpre[:, :F]
            f_ref[r0:r0 + R, :] = (g * _sigmoid(g) * cpre[:, F:]).astype(BF16)
        h_out = h_ref[...] + _dot(f_ref[...], wd_ref[...])
        if not head:
            ho_ref[...] = h_out
            return

        @pl.when(i == 0)
        def _():
            l_ref[...] = jnp.zeros_like(l_ref)

        e = h_out - t_ref[...]
        dy_ref[...] = e * (1.0 / D)
        rows = jnp.sum(e * e, axis=-1, keepdims=True) * (1.0 / D)
        l_ref[...] += 0.5 * jnp.sum(rows, axis=0, keepdims=True)

    hb = tm // HALO
    row = pl.BlockSpec((tm, D), lambda i: (i, 0))
    in_specs = [pl.BlockSpec((tm, C), lambda i: (i, 0)),
                pl.BlockSpec((HALO, C), lambda i: (jnp.maximum(i * hb - 1, 0), 0)),
                row, _resident((3, C)), _resident((1, C)), _resident(wdown.shape)]
    out_specs = [pl.BlockSpec((tm, F), lambda i: (i, 0)), row]
    out_shape = [jax.ShapeDtypeStruct((S, F), BF16), jax.ShapeDtypeStruct((S, D), F32)]
    args = (a, a, h_in, cw, cb, wdown)
    if head:
        in_specs.append(row)
        out_specs.append(pl.BlockSpec((1, 1), lambda i: (0, 0)))
        out_shape.append(jax.ShapeDtypeStruct((1, 1), F32))
        args += (target,)
    return _call(
        body, name=name, grid=(S // tm,), in_specs=in_specs, out_specs=out_specs, out_shape=out_shape,
        scratch_shapes=[pltpu.VMEM((HALO + tm, C), F32)],
        sem=("arbitrary",) if head else ("parallel",), args=args, comm=comm)


def _ffn_bwd_dc(dh, a, cw, cb, wdown, name, comm=None, tm=256, R=64):
    S, C = a.shape
    F = C // 2
    D = dh.shape[1]
    tm = min(tm, S)

    def body(dh_ref, a_ref, halo_ref, cw_ref, cb_ref, wd_ref, dc_ref, dcw_ref, dcb_ref, a32, df_s, acc):
        i = pl.program_id(0)

        @pl.when(i == 0)
        def _():
            acc[...] = jnp.zeros_like(acc)

        a32[0:HALO, :] = jnp.where(i > 0, halo_ref[...].astype(F32), 0.0)
        a32[HALO:, :] = a_ref[...].astype(F32)
        df_s[...] = _dot_nt(dh_ref[...].astype(BF16), wd_ref[...])
        rows8 = lambda v: functools.reduce(jnp.add, [v[8 * k:8 * k + 8] for k in range(R // 8)])

        for r0 in range(0, tm, R):
            a0, a1, a2 = _conv_taps(a32, r0, R, reuse=True)
            cpre = cw_ref[0:1, :] * a2 + cw_ref[1:2, :] * a1 + cw_ref[2:3, :] * a0 + cb_ref[...]
            g = cpre[:, :F]
            val = cpre[:, F:]
            sg = _sigmoid(g)
            df = df_s[r0:r0 + R, :]
            dg = df * val * (sg * (1.0 + g * (1.0 - sg)))
            dval = df * (g * sg)
            dc = jnp.concatenate([dg, dval], axis=1)
            dc_ref[r0:r0 + R, :] = dc.astype(BF16)
            acc[0] += rows8(dc * a2)
            acc[1] += rows8(dc * a1)
            acc[2] += rows8(dc * a0)
            acc[3] += rows8(dc)

        @pl.when(i == S // tm - 1)
        def _():
            for k in range(3):
                dcw_ref[k:k + 1, :] = jnp.sum(acc[k], axis=0, keepdims=True)
            dcb_ref[...] = jnp.sum(acc[3], axis=0, keepdims=True)

    hb = tm // HALO
    return _call(
        body, name=name, grid=(S // tm,),
        in_specs=[pl.BlockSpec((tm, D), lambda i: (i, 0)),
                  pl.BlockSpec((tm, C), lambda i: (i, 0)),
                  pl.BlockSpec((HALO, C), lambda i: (jnp.maximum(i * hb - 1, 0), 0)),
                  _resident((3, C)), _resident((1, C)), _resident(wdown.shape)],
        out_specs=[pl.BlockSpec((tm, C), lambda i: (i, 0)),
                   pl.BlockSpec((3, C), lambda i: (0, 0)), pl.BlockSpec((1, C), lambda i: (0, 0))],
        out_shape=[jax.ShapeDtypeStruct((S, C), BF16), jax.ShapeDtypeStruct((3, C), F32),
                   jax.ShapeDtypeStruct((1, C), F32)],
        scratch_shapes=[pltpu.VMEM((HALO + tm, C), F32), pltpu.VMEM((tm, F), F32), pltpu.VMEM((4, 8, C), F32)],
        sem=("arbitrary",), args=(dh, a, a, cw, cb, wdown), comm=comm)


def _bwd_norm(dA, w4, h_in, gain, dh_out, name, conv_w=None, comm=None, R=64):
    S, N = dA.shape
    nsh, D, ns = w4.shape
    conv = conv_w is not None
    tm = min(256 if conv else 512, S)
    nt = S // tm

    def finish(src_ref, w_ref, h_ref, g_ref, dho_ref, dhi_ref, dg_ref):
        dhn = _dot_nt(src_ref[:, 0:ns], w_ref[0])
        for j in range(1, nsh):
            dhn += _dot_nt(src_ref[:, j * ns:(j + 1) * ns], w_ref[j])
        dx, dgain = _rms_bwd(dhn, h_ref[...], g_ref[...])
        dg_ref[...] += dgain
        dhi_ref[...] = dho_ref[...] + dx

    def body_plain(dA_ref, w_ref, h_ref, g_ref, dho_ref, dhi_ref, dg_ref):
        @pl.when(pl.program_id(0) == 0)
        def _():
            dg_ref[...] = jnp.zeros_like(dg_ref)

        finish(dA_ref, w_ref, h_ref, g_ref, dho_ref, dhi_ref, dg_ref)

    def body_conv(dc_ref, halo_ref, cw_ref, w_ref, h_ref, g_ref, dho_ref, dhi_ref, dg_ref, da_ref, dc32):
        i = pl.program_id(0)

        @pl.when(i == 0)
        def _():
            dg_ref[...] = jnp.zeros_like(dg_ref)

        dc32[0:tm, :] = dc_ref[...].astype(F32)
        dc32[tm:, :] = jnp.where(i < nt - 1, halo_ref[...].astype(F32), 0.0)

        for r0 in range(0, tm, R):
            d0 = dc32[r0:r0 + R, :]
            d1 = dc32[r0 + 1:r0 + 1 + R, :]
            d2 = dc32[r0 + 2:r0 + 2 + R, :]
            da = cw_ref[2:3, :] * d0 + cw_ref[1:2, :] * d1 + cw_ref[0:1, :] * d2
            da_ref[r0:r0 + R, :] = da.astype(BF16)
        finish(da_ref, w_ref, h_ref, g_ref, dho_ref, dhi_ref, dg_ref)

    row = lambda width: pl.BlockSpec((tm, width), lambda i: (i, 0))
    common_in = [_resident(w4.shape), row(D), _resident((1, D)), row(D)]
    common_out = [row(D), pl.BlockSpec((1, D), lambda i: (0, 0))]
    common_shape = [jax.ShapeDtypeStruct((S, D), F32), jax.ShapeDtypeStruct((1, D), F32)]
    if not conv:
        return _call(
            body_plain, name=name, grid=(nt,),
            in_specs=[row(N)] + common_in, out_specs=common_out, out_shape=common_shape,
            sem=("arbitrary",), args=(dA, w4, h_in, gain, dh_out), comm=comm)
    hb = tm // HALO
    nhb = S // HALO
    return _call(
        body_conv, name=name, grid=(nt,),
        in_specs=[row(N), pl.BlockSpec((HALO, N), lambda i: (jnp.minimum((i + 1) * hb, nhb - 1), 0)),
                  _resident((3, N))] + common_in,
        out_specs=common_out + [row(N)],
        out_shape=common_shape + [jax.ShapeDtypeStruct((S, N), BF16)],
        scratch_shapes=[pltpu.VMEM((tm + HALO, N), F32)],
        sem=("arbitrary",), args=(dA, dA, conv_w, w4, h_in, gain, dh_out), comm=comm)


def _rel_buckets_flat():
    q = np.arange(CHUNK)[:, None] + CHUNK
    k = np.arange(2 * CHUNK)[None, :]
    n = np.maximum(q - k, 0)
    max_exact = REL_BUCKETS // 2
    large = max_exact + (np.log(np.maximum(n, 1).astype(np.float32) / max_exact)
                         / math.log(REL_MAX_DIST / max_exact) * (REL_BUCKETS - max_exact)).astype(np.int32)
    large = np.minimum(large, REL_BUCKETS - 1)
    return np.where(n < max_exact, n, large).astype(np.int32).reshape(1, CHUNK * 2 * CHUNK)


def _split_bf16(x):
    hi = x.astype(BF16)
    return hi, (x - hi.astype(F32)).astype(BF16)


def _rel_bias_expand(rel_bias, bucket):
    B, H = rel_bias.shape
    n = bucket.shape[1]

    def body(rb_ref, bk_ref, o_ref):
        oh = (bk_ref[...] == lax.broadcasted_iota(jnp.int32, (B, n), 0)).astype(BF16)
        hi, lo = _split_bf16(rb_ref[...])
        o_ref[...] = _dot_tn(hi, oh) + _dot_tn(lo, oh)

    return pl.pallas_call(body, name="rel_bias_expand", out_shape=jax.ShapeDtypeStruct((H, n), F32),
                          compiler_params=_cp(None))(rel_bias, bucket)


def _rel_bias_reduce(dbias, bucket):
    H, n = dbias.shape
    B = REL_BUCKETS

    def body(db_ref, bk_ref, o_ref):
        oh = (bk_ref[...] == lax.broadcasted_iota(jnp.int32, (B, n), 0)).astype(BF16)
        hi, lo = _split_bf16(db_ref[...])
        o_ref[...] = _dot_nt(oh, hi) + _dot_nt(oh, lo)

    return pl.pallas_call(body, name="rel_bias_reduce", out_shape=jax.ShapeDtypeStruct((B, H), F32),
                          compiler_params=_cp(None))(dbias, bucket)


def _lo_mask(rows):
    return lax.broadcasted_iota(jnp.int32, (rows, LANES), 1) < HEAD_DIM


def _half_sums(y, lo):
    s_lo = jnp.sum(jnp.where(lo, y, 0.0), axis=-1, keepdims=True)
    s_hi = jnp.sum(jnp.where(lo, 0.0, y), axis=-1, keepdims=True)
    return jnp.where(lo, s_lo, s_hi)


def _half_rms(x, gain, lo):
    r = lax.rsqrt(_half_sums(x * x, lo) * (1.0 / HEAD_DIM) + EPS)
    xhat = x * r
    return xhat * gain, xhat, r


def _half_rms_bwd(dy, xhat, r, gain, lo):
    gdy = dy * gain
    dx = r * (gdy - xhat * (_half_sums(gdy * xhat, lo) * (1.0 / HEAD_DIM)))
    return dx, jnp.sum(dy * xhat, axis=0, keepdims=True)


def _dup_half(pair, e, lo):
    sw = pltpu.roll(pair, HEAD_DIM, 1)
    return jnp.where(lo, pair, sw) if e == 0 else jnp.where(lo, sw, pair)


def _band_valid(n):
    qi = lax.broadcasted_iota(jnp.int32, (KV_GROUP * CHUNK, 2 * CHUNK), 0) & (CHUNK - 1)
    ki = lax.broadcasted_iota(jnp.int32, (KV_GROUP * CHUNK, 2 * CHUNK), 1)
    dist = qi + CHUNK - ki
    return (dist >= 0) & (dist < CHUNK) & ((n > 0) | (ki >= CHUNK))


def _stack_heads(a, b, lo):
    return jnp.concatenate([jnp.where(lo, a, 0.0), jnp.where(lo, 0.0, a), jnp.where(lo, b, 0.0), jnp.where(lo, 0.0, b)],
                           axis=0)


def _unstack_heads(x4, lo):
    return (jnp.where(lo, x4[0:CHUNK], x4[CHUNK:2 * CHUNK]),
            jnp.where(lo, x4[2 * CHUNK:3 * CHUNK], x4[3 * CHUNK:]))


def _sink_col(sink_ref, hk):
    row = lax.broadcasted_iota(jnp.int32, (KV_GROUP * CHUNK, 1), 0)
    col = jnp.full((KV_GROUP * CHUNK, 1), sink_ref[KV_GROUP * hk + KV_GROUP - 1], F32)
    for r in range(KV_GROUP - 2, -1, -1):
        col = jnp.where(row < (r + 1) * CHUNK, sink_ref[KV_GROUP * hk + r], col)
    return col


def _softmax_sink(s, valid, sink):
    s = jnp.where(valid, s, -jnp.inf)
    m = jnp.maximum(jnp.max(s, axis=-1, keepdims=True), sink)
    p = jnp.exp(s - m)
    esink = jnp.exp(sink - m)
    inv = 1.0 / (jnp.sum(p, axis=-1, keepdims=True) + esink)
    return p * inv, esink * inv


QW = N_HEADS * HEAD_DIM
KVW = N_KV_HEADS * HEAD_DIM


def _attn_fwd(qkv, qg2, kg2, sinks, bias, comm=None):
    S = qkv.shape[0]
    nb = S // CHUNK

    def body(cur_ref, prev_ref, qg_ref, kg_ref, sink_ref, bias_ref, o_ref):
        n = pl.program_id(0)
        lo = _lo_mask(CHUNK)
        lo2 = _lo_mask(2 * CHUNK)
        valid = _band_valid(n)
        for j in range(N_KV_HEADS // 2):
            kc = slice(QW + j * LANES, QW + (j + 1) * LANES)
            vc = slice(QW + KVW + j * LANES, QW + KVW + (j + 1) * LANES)
            kpair = jnp.concatenate([prev_ref[:, j * LANES:(j + 1) * LANES], cur_ref[:, kc]], axis=0).astype(F32)
            vpair = jnp.concatenate([prev_ref[:, KVW + j * LANES:KVW + (j + 1) * LANES], cur_ref[:, vc]], axis=0).astype(F32)
            knpair, _, _ = _half_rms(kpair, kg_ref[...], lo2)
            for e in range(2):
                hk = 2 * j + e
                kdup = _dup_half(knpair, e, lo2).astype(BF16)
                vdup = _dup_half(vpair, e, lo2).astype(BF16)
                ca = slice(2 * hk * LANES, (2 * hk + 1) * LANES)
                cb = slice((2 * hk + 1) * LANES, (2 * hk + 2) * LANES)
                qna, _, _ = _half_rms(cur_ref[:, ca].astype(F32), qg_ref[...], lo)
                qnb, _, _ = _half_rms(cur_ref[:, cb].astype(F32), qg_ref[...], lo)
                qm4 = _stack_heads(qna, qnb, lo).astype(BF16)
                s = _dot_nt(qm4, kdup) * (HEAD_DIM ** -0.5) + bias_ref[hk]
                p, _ = _softmax_sink(s, valid, _sink_col(sink_ref, hk))
                oa, ob = _unstack_heads(_dot(p.astype(BF16), vdup), lo)
                o_ref[:, ca] = oa.astype(BF16)
                o_ref[:, cb] = ob.astype(BF16)

    return _call(
        body, name="attn_fwd", grid=(nb,),
        in_specs=[pl.BlockSpec((CHUNK, QW + 2 * KVW), lambda n: (n, 0)),
                  pl.BlockSpec((CHUNK, 2 * KVW), lambda n: (jnp.maximum(n - 1, 0), QW // (2 * KVW))),
                  _resident((1, LANES)), _resident((1, LANES)),
                  pl.BlockSpec(memory_space=pltpu.SMEM),
                  _resident(bias.shape)],
        out_specs=pl.BlockSpec((CHUNK, QW), lambda n: (n, 0)),
        out_shape=jax.ShapeDtypeStruct((S, QW), BF16),
        sem=("parallel",), args=(qkv, qkv, qg2, kg2, sinks, bias), comm=comm)


def _attn_bwd(qkv, do, qg2, kg2, sinks, bias, comm=None):
    S = qkv.shape[0]
    nb = S // CHUNK

    def body(cur_ref, prev_ref, do_ref, qg_ref, kg_ref, sink_ref, bias_ref,
             dqkv_ref, dbias_ref, dqg_ref, dkg_ref, dsink_ref, carry, band, dsacc, gacc):
        i = pl.program_id(0)
        n = nb - 1 - i
        lo = _lo_mask(CHUNK)
        lo2 = _lo_mask(2 * CHUNK)
        lane = lax.broadcasted_iota(jnp.int32, (KV_GROUP * CHUNK, LANES), 1)
        valid = _band_valid(n)

        @pl.when(i == 0)
        def _():
            dbias_ref[...] = jnp.zeros_like(dbias_ref)
            carry[...] = jnp.zeros_like(carry)
            dsacc[...] = jnp.zeros_like(dsacc)
            gacc[...] = jnp.zeros_like(gacc)

        qgain = qg_ref[...]
        kgain = kg_ref[...]
        for j in range(N_KV_HEADS // 2):
            kc = slice(QW + j * LANES, QW + (j + 1) * LANES)
            vc = slice(QW + KVW + j * LANES, QW + KVW + (j + 1) * LANES)
            kpair = jnp.concatenate([prev_ref[:, j * LANES:(j + 1) * LANES], cur_ref[:, kc]], axis=0).astype(F32)
            vpair = jnp.concatenate([prev_ref[:, KVW + j * LANES:KVW + (j + 1) * LANES], cur_ref[:, vc]], axis=0).astype(F32)
            knpair, khat, kr = _half_rms(kpair, kgain, lo2)
            dk_folds = []
            dv_folds = []
            for e in range(2):
                hk = 2 * j + e
                kdup = _dup_half(knpair, e, lo2).astype(BF16)
                vdup = _dup_half(vpair, e, lo2).astype(BF16)
                ca = slice(2 * hk * LANES, (2 * hk + 1) * LANES)
                cb = slice((2 * hk + 1) * LANES, (2 * hk + 2) * LANES)
                qna, qhata, qra = _half_rms(cur_ref[:, ca].astype(F32), qgain, lo)
                qnb, qhatb, qrb = _half_rms(cur_ref[:, cb].astype(F32), qgain, lo)
                qm4 = _stack_heads(qna, qnb, lo).astype(BF16)
                dom4 = _stack_heads(do_ref[:, ca].astype(F32), do_ref[:, cb].astype(F32), lo).astype(BF16)
                s = _dot_nt(qm4, kdup) * (HEAD_DIM ** -0.5) + bias_ref[hk]
                p, psink = _softmax_sink(s, valid, _sink_col(sink_ref, hk))
                dp = _dot_nt(dom4, vdup)
                delta = jnp.sum(p * dp, axis=-1, keepdims=True)
                ds = p * (dp - delta)
                dbias_ref[hk] += ds
                dsacc[...] += jnp.where(lane == hk, -(psink * delta), 0.0)
                dsr = (ds * (HEAD_DIM ** -0.5)).astype(BF16)
                dqna, dqnb = _unstack_heads(_dot(dsr, kdup), lo)
                dkd = _dot_tn(dsr, qm4)
                dvd = _dot_tn(p.astype(BF16), dom4)
                dqa, dqga = _half_rms_bwd(dqna, qhata, qra, qgain, lo)
                dqb, dqgb = _half_rms_bwd(dqnb, qhatb, qrb, qgain, lo)
                gacc[0:1, :] += dqga + dqgb
                dqkv_ref[:, ca] = dqa.astype(BF16)
                dqkv_ref[:, cb] = dqb.astype(BF16)
                dk_folds.append(dkd + pltpu.roll(dkd, HEAD_DIM, 1))
                dv_folds.append(dvd + pltpu.roll(dvd, HEAD_DIM, 1))
            dkn = jnp.where(lo2, dk_folds[0], dk_folds[1])
            dk, dkg = _half_rms_bwd(dkn, khat, kr, kgain, lo2)
            gacc[1:2, :] += dkg
            band[:, j * LANES:(j + 1) * LANES] = dk
            band[:, KVW + j * LANES:KVW + (j + 1) * LANES] = jnp.where(lo2, dv_folds[0], dv_folds[1])
        dqkv_ref[:, QW:] = (band[CHUNK:, :] + carry[...]).astype(BF16)
        carry[...] = band[0:CHUNK, :]

        @pl.when(i == nb - 1)
        def _():
            g = gacc[...]
            g = g + pltpu.roll(g, HEAD_DIM, 1)
            dqg_ref[...] = g[0:1, :]
            dkg_ref[...] = g[1:2, :]
            for r in range(KV_GROUP):
                dsink_ref[r:r + 1, :] = jnp.sum(dsacc[r * CHUNK:(r + 1) * CHUNK, :], axis=0, keepdims=True)

    vec = pl.BlockSpec((1, LANES), lambda i: (0, 0))
    return _call(
        body, name="attn_bwd", grid=(nb,),
        in_specs=[pl.BlockSpec((CHUNK, QW + 2 * KVW), lambda i: (nb - 1 - i, 0)),
                  pl.BlockSpec((CHUNK, 2 * KVW), lambda i: (jnp.maximum(nb - 2 - i, 0), QW // (2 * KVW))),
                  pl.BlockSpec((CHUNK, QW), lambda i: (nb - 1 - i, 0)),
                  _resident((1, LANES)), _resident((1, LANES)),
                  pl.BlockSpec(memory_space=pltpu.SMEM),
                  _resident(bias.shape)],
        out_specs=[pl.BlockSpec((CHUNK, QW + 2 * KVW), lambda i: (nb - 1 - i, 0)),
                   pl.BlockSpec(bias.shape, lambda i: (0, 0, 0)), vec, vec,
                   pl.BlockSpec((KV_GROUP, LANES), lambda i: (0, 0))],
        out_shape=[jax.ShapeDtypeStruct((S, QW + 2 * KVW), BF16),
                   jax.ShapeDtypeStruct(bias.shape, F32),
                   jax.ShapeDtypeStruct((1, LANES), F32), jax.ShapeDtypeStruct((1, LANES), F32),
                   jax.ShapeDtypeStruct((KV_GROUP, LANES), F32)],
        scratch_shapes=[pltpu.VMEM((CHUNK, 2 * KVW), F32), pltpu.VMEM((2 * CHUNK, 2 * KVW), F32),
                        pltpu.VMEM((KV_GROUP * CHUNK, LANES), F32), pltpu.VMEM((8, LANES), F32)],
        sem=("arbitrary",), args=(qkv, qkv, do, qg2, kg2, sinks, bias), comm=comm)


def _row_tile(rows, cols, n_arrays):
    budget = VMEM_LIMIT_V7X // 4 // (2 * n_arrays * 4 * cols)
    best = 8
    for n in range(1, rows // 8 + 1):
        if rows % n == 0 and (rows // n) % 8 == 0 and rows // n <= budget:
            best = rows // n
            break
    return best


def _adamw(g, w, m, v, name, comm=None):
    R, C = g.shape
    tr = _row_tile(R, C, 8)

    def body(g_ref, w_ref, m_ref, v_ref, d_ref, mo_ref, vo_ref, go_ref):
        gg = g_ref[...]
        go_ref[...] = gg
        mn = ADAM_B1 * m_ref[...] + (1.0 - ADAM_B1) * gg
        vn = ADAM_B2 * v_ref[...] + (1.0 - ADAM_B2) * jnp.square(gg)
        m_hat = mn / (1.0 - ADAM_B1 ** ADAM_STEP)
        v_hat = vn / (1.0 - ADAM_B2 ** ADAM_STEP)
        d_ref[...] = -ADAM_LR * (m_hat / (jnp.sqrt(v_hat) + ADAM_EPS) + ADAM_WD * w_ref[...])
        mo_ref[...] = mn
        vo_ref[...] = vn

    spec = pl.BlockSpec((tr, C), lambda i: (i, 0))
    return _call(
        body, name=name, grid=(R // tr,), in_specs=[spec] * 4, out_specs=[spec] * 4,
        out_shape=[jax.ShapeDtypeStruct((R, C), F32)] * 4, sem=("parallel",), args=(g, w, m, v), comm=comm)


def _place_shard(shards, layer, where, dtype, name):
    _, R, C = shards.shape
    tr = _row_tile(R, C, 2) if R % 8 == 0 else R

    def body(s_ref, x_ref, o_ref):
        o_ref[...] = x_ref[...].astype(dtype)

    return pl.pallas_call(
        body, name=name,
        grid_spec=pltpu.PrefetchScalarGridSpec(
            num_scalar_prefetch=1, grid=(R // tr,),
            in_specs=[pl.BlockSpec((1, tr, C), lambda i, s_ref: (layer, i, 0))],
            out_specs=pl.BlockSpec((1, tr, C), lambda i, s_ref: (s_ref[1], i, 0))),
        out_shape=jax.ShapeDtypeStruct((4, R, C), dtype),
        compiler_params=_cp(("parallel",)),
    )(where, shards)


def _pair_add(g4, rsib, where, name):
    J, R, C = g4.shape
    Rh = R // 2
    tr = _row_tile(Rh, C, 4)
    g5 = g4.reshape(J, 2, Rh, C)

    def body(s_ref, g_ref, r_ref, p_ref, q_ref):
        val = (g_ref[...].astype(F32)[0] + r_ref[...].astype(F32)).astype(BF16)
        p_ref[...] = val

        @pl.when(pl.program_id(1) == s_ref[1])
        def _():
            q_ref[...] = val

    return pl.pallas_call(
        body, name=name,
        grid_spec=pltpu.PrefetchScalarGridSpec(
            num_scalar_prefetch=1, grid=(Rh // tr, J),
            in_specs=[pl.BlockSpec((1, 1, tr, C), lambda i, j, s_ref: (j, s_ref[0], i, 0)),
                      pl.BlockSpec((1, tr, C), lambda i, j, s_ref: (j, i, 0))],
            out_specs=[pl.BlockSpec((1, tr, C), lambda i, j, s_ref: (j, i, 0)),
                       pl.BlockSpec((1, tr, C), lambda i, j, s_ref: (s_ref[1], i, 0))]),
        out_shape=[jax.ShapeDtypeStruct((J, Rh, C), BF16)] * 2,
        compiler_params=_cp(("parallel", "arbitrary")),
    )(where, g5, rsib)


def _sum_chips(q, where, dest, layer, out_shape, name):
    J, Rh, C = q.shape
    tr = _row_tile(Rh, C, 3)
    nb = Rh // tr

    def body(s_ref, q_ref, *rest):
        qq = q_ref[...].astype(F32)
        rest[-1][0] = ((qq[0] + qq[1]) + qq[2]) + qq[3]

    have = dest is not None
    return pl.pallas_call(
        body, name=name,
        grid_spec=pltpu.PrefetchScalarGridSpec(
            num_scalar_prefetch=1, grid=(nb,),
            in_specs=[pl.BlockSpec((J, tr, C), lambda i, s_ref: (0, i, 0))] + ([ANY] if have else []),
            out_specs=pl.BlockSpec((1, tr, C), lambda i, s_ref: (layer, s_ref[0] * nb + i, 0))),
        out_shape=jax.ShapeDtypeStruct(out_shape, F32),
        input_output_aliases={2: 0} if have else {},
        compiler_params=_cp(("parallel",)),
    )(*((where, q, dest) if have else (where, q)))


MESH = pl.DeviceIdType.MESH
ANY = pl.BlockSpec(memory_space=pl.ANY)


def _place():
    x, y, c = lax.axis_index("x"), lax.axis_index("y"), lax.axis_index("c")
    others = [(1 - x, y), (x, 1 - y), (1 - x, 1 - y)]
    return x, y, c, 2 * x + y, others, [2 * ox + oy for ox, oy in others]


def _gather_comm(items):
    n = len(items)
    placed = [it[0] for it in items]
    split = [it[3] for it in items]

    def rows(t, ref, half):
        _, lo, hi, _ = items[t]
        if not split[t]:
            return ref if (lo, hi) == (0, placed[t].shape[1]) else ref.at[pl.ds(lo, hi - lo), :]
        rh = (hi - lo) // 2
        return ref.at[pl.ds(lo + half * rh, rh), :]

    def sends(outs, sems):
        send, recv = sems[0], sems[1]
        x, y, c, me, others, okey = _place()
        cps = []
        for t in range(n):
            mine = rows(t, outs[t].at[me], c)
            for j, (ox, oy) in enumerate(others):
                cps.append(pltpu.make_async_remote_copy(
                    src_ref=mine, dst_ref=mine,
                    send_sem=send.at[t, j], recv_sem=recv.at[t, j], device_id=(ox, oy, c), device_id_type=MESH))
        return cps

    def start(srcs, outs, news, sems):
        for cp in sends(outs, sems):
            cp.start()

    def forwards(outs, sems):
        fsend, frecv = sems[2], sems[3]
        x, y, c, me, others, okey = _place()
        cps = []
        for t in range(n):
            if split[t]:
                for j in range(3):
                    landed = rows(t, outs[t].at[okey[j]], c)
                    cps.append(pltpu.make_async_remote_copy(
                        src_ref=landed, dst_ref=landed, send_sem=fsend.at[t, j], recv_sem=frecv.at[t, j],
                        device_id=(x, y, 1 - c), device_id_type=MESH))
        return cps

    def middle(srcs, outs, news, sems):
        send, recv = sems[0], sems[1]
        x, y, c, me, others, okey = _place()
        for t in range(n):
            for j in range(3):
                landed = rows(t, outs[t].at[okey[j]], c)
                pltpu.make_async_remote_copy(
                    src_ref=landed, dst_ref=landed, send_sem=send.at[t, j], recv_sem=recv.at[t, j],
                    device_id=(x, y, 1 - c), device_id_type=MESH).wait_recv()
        for cp in forwards(outs, sems):
            cp.start()

    def finish(srcs, outs, news, sems):
        fsend, frecv = sems[2], sems[3]
        x, y, c, me, others, okey = _place()
        for t in range(n):
            if split[t]:
                for j in range(3):
                    theirs = rows(t, outs[t].at[okey[j]], 1 - c)
                    pltpu.make_async_remote_copy(
                        src_ref=theirs, dst_ref=theirs, send_sem=fsend.at[t, j], recv_sem=frecv.at[t, j],
                        device_id=(x, y, 1 - c), device_id_type=MESH).wait_recv()
        for cp in sends(outs, sems) + forwards(outs, sems):
            cp.wait_send()

    return _Comm([], placed, [], [pltpu.SemaphoreType.DMA((n, 3))] * 4, start, finish, middle)


def _pair_exchange_comm(gs):
    n = len(gs)

    def copies(ins, outs, sems):
        send, recv = sems
        x, y, c, _, _, _ = _place()
        cps = []
        for t in range(n):
            rh = gs[t].shape[1] // 2
            cps.append(pltpu.make_async_remote_copy(
                src_ref=ins[t].at[:, pl.ds((1 - c) * rh, rh), :], dst_ref=outs[t],
                send_sem=send.at[t], recv_sem=recv.at[t], device_id=(x, y, 1 - c), device_id_type=MESH))
        return cps

    def start(ins, zones, outs, sems):
        for cp in copies(ins, outs, sems):
            cp.start()

    def finish(ins, zones, outs, sems):
        for cp in copies(ins, outs, sems):
            cp.wait()

    news = [jax.ShapeDtypeStruct((4, g.shape[1] // 2, g.shape[2]), g.dtype) for g in gs]
    return _Comm(gs, [], news, [pltpu.SemaphoreType.DMA((n,))] * 2, start, finish)


def _chip_scatter_comm(ps, qs):
    n = len(ps)

    def sends(ins, outs, sems):
        send, recv = sems
        x, y, c, me, others, okey = _place()
        return [pltpu.make_async_remote_copy(
            src_ref=ins[t].at[okey[j]], dst_ref=outs[t].at[me],
            send_sem=send.at[t, j], recv_sem=recv.at[t, j], device_id=(ox, oy, c), device_id_type=MESH)
            for t in range(n) for j, (ox, oy) in enumerate(others)]

    def start(ins, outs, news, sems):
        for cp in sends(ins, outs, sems):
            cp.start()

    def finish(ins, outs, news, sems):
        send, recv = sems
        x, y, c, me, others, okey = _place()
        for t in range(n):
            for j in range(3):
                slot = outs[t].at[okey[j]]
                pltpu.make_async_remote_copy(
                    src_ref=slot, dst_ref=slot, send_sem=send.at[t, j], recv_sem=recv.at[t, j],
                    device_id=(x, y, c), device_id_type=MESH).wait_recv()
        for cp in sends(ins, outs, sems):
            cp.wait_send()

    return _Comm(ps, qs, [], [pltpu.SemaphoreType.DMA((n, 3))] * 2, start, finish)


def _half_exchange_comm(arrs, layers=None):
    n = len(arrs)
    items = [(t, layer) for t in range(n) for layer in (range(arrs[t].shape[0]) if layers is None else layers[t])]

    def sends(outs, sems):
        send, recv = sems
        x, y, c, _, _, _ = _place()
        cps = []
        for k, (t, layer) in enumerate(items):
            rh = arrs[t].shape[1] // 2
            mine = outs[t].at[layer, pl.ds(c * rh, rh), :]
            cps.append(pltpu.make_async_remote_copy(
                src_ref=mine, dst_ref=mine, send_sem=send.at[k], recv_sem=recv.at[k],
                device_id=(x, y, 1 - c), device_id_type=MESH))
        return cps

    def start(srcs, outs, news, sems):
        for cp in sends(outs, sems):
            cp.start()

    def finish(srcs, outs, news, sems):
        send, recv = sems
        x, y, c, _, _, _ = _place()
        for k, (t, layer) in enumerate(items):
            rh = arrs[t].shape[1] // 2
            theirs = outs[t].at[layer, pl.ds((1 - c) * rh, rh), :]
            pltpu.make_async_remote_copy(
                src_ref=theirs, dst_ref=theirs, send_sem=send.at[k], recv_sem=recv.at[k],
                device_id=(x, y, 1 - c), device_id_type=MESH).wait_recv()
        for cp in sends(outs, sems):
            cp.wait_send()

    return _Comm([], arrs, [], [pltpu.SemaphoreType.DMA((len(items),))] * 2, start, finish)


SMALL_COLS = 1024
SMALL_PIECE_ROWS = 48


def _allreduce_small(buf, comm=None):
    pr = SMALL_PIECE_ROWS
    flips = [(d >> 2 & 1, d >> 1 & 1, d & 1) for d in range(1, 8)]

    def body(x_ref, o_ref, rbuf, send1, recv1, send2, recv2):
        x, y, c = lax.axis_index("x"), lax.axis_index("y"), lax.axis_index("c")
        me = 4 * x + 2 * y + c
        peers = [(x ^ fx, y ^ fy, c ^ fc) for fx, fy, fc in flips]
        pid = [4 * px + 2 * py + pc for px, py, pc in peers]

        def piece(ref, p):
            return ref.at[pl.ds(pl.multiple_of(p * pr, 8), pr), :]

        cps = []
        for d in range(7):
            cp = pltpu.make_async_remote_copy(
                src_ref=piece(x_ref, pid[d]), dst_ref=rbuf.at[d + 1],
                send_sem=send1.at[d], recv_sem=recv1.at[d], device_id=peers[d], device_id_type=MESH)
            cp.start()
            cps.append(cp)
        acc = piece(x_ref, me)[...]
        for d in range(7):
            cps[d].wait_recv()
            acc = acc + rbuf[d + 1]
        piece(o_ref, me)[...] = acc
        out = []
        for d in range(7):
            cp = pltpu.make_async_remote_copy(
                src_ref=piece(o_ref, me), dst_ref=piece(o_ref, me),
                send_sem=send2.at[d], recv_sem=recv2.at[d], device_id=peers[d], device_id_type=MESH)
            cp.start()
            out.append(cp)
        for d in range(7):
            pltpu.make_async_remote_copy(
                src_ref=piece(o_ref, pid[d]), dst_ref=piece(o_ref, pid[d]),
                send_sem=send2.at[d], recv_sem=recv2.at[d], device_id=peers[d], device_id_type=MESH).wait_recv()
        for cp in cps + out:
            cp.wait_send()

    vm = pl.BlockSpec(memory_space=pltpu.VMEM)
    return _call(
        body, name="small_allreduce", grid=(), in_specs=[vm], out_specs=vm,
        out_shape=jax.ShapeDtypeStruct(buf.shape, F32),
        scratch_shapes=[pltpu.VMEM((8, pr, SMALL_COLS), F32)] + [pltpu.SemaphoreType.DMA((7,))] * 4,
        args=(buf,), comm=comm)


def _rows_of(shape):
    return -(-math.prod(shape) // (8 * SMALL_COLS)) * 8


def _pack(arrays, rows):
    parts = []
    for a in arrays:
        r = _rows_of(a.shape)
        parts.append(jnp.pad(a.reshape(-1), (0, r * SMALL_COLS - a.size)).reshape(r, SMALL_COLS))
    used = sum(p.shape[0] for p in parts)
    if rows > used:
        parts.append(jnp.zeros((rows - used, SMALL_COLS), F32))
    return jnp.concatenate(parts, axis=0)


def _unpack(buf, shapes):
    out, off = [], 0
    for s in shapes:
        r = _rows_of(s)
        out.append(buf[off:off + r].reshape(-1)[:math.prod(s)].reshape(s))
        off += r
    return out


BIG = ["sgu_w_in", "sgu_w_out", "attn_w_qkv", "attn_w_o", "ffn_w_up", "ffn_w_down"]
SMALL = ["mix_norm", "ffn_norm", "sgu_v_gain", "sgu_w_s", "sgu_b_s", "attn_q_gain", "attn_k_gain", "attn_sinks",
         "rel_bias", "ffn_conv_b"]
ORDER = ["mix_norm", "ffn_norm", "sgu_w_in", "sgu_v_gain", "sgu_w_s", "sgu_b_s", "sgu_w_out", "attn_w_qkv",
         "attn_q_gain", "attn_k_gain", "attn_sinks", "attn_w_o", "rel_bias", "ffn_w_up", "ffn_conv_w", "ffn_conv_b",
         "ffn_w_down"]


def kernel(x, mix_norm, ffn_norm, sgu_w_in, sgu_v_gain, sgu_w_s, sgu_b_s, sgu_w_out, attn_w_qkv, attn_q_gain, attn_k_gain, attn_sinks, attn_w_o, rel_bias, ffn_w_up, ffn_conv_w, ffn_conv_b, ffn_w_down, loss_target, m_mix_norm, m_ffn_norm, m_sgu_w_in, m_sgu_v_gain, m_sgu_w_s, m_sgu_b_s, m_sgu_w_out, m_attn_w_qkv, m_attn_q_gain, m_attn_k_gain, m_attn_sinks, m_attn_w_o, m_rel_bias, m_ffn_w_up, m_ffn_conv_w, m_ffn_conv_b, m_ffn_w_down, v_mix_norm, v_ffn_norm, v_sgu_w_in, v_sgu_v_gain, v_sgu_w_s, v_sgu_b_s, v_sgu_w_out, v_attn_w_qkv, v_attn_q_gain, v_attn_k_gain, v_attn_sinks, v_attn_w_o, v_rel_bias, v_ffn_w_up, v_ffn_conv_w, v_ffn_conv_b, v_ffn_w_down):
    w = dict(mix_norm=mix_norm, ffn_norm=ffn_norm, sgu_w_in=sgu_w_in, sgu_v_gain=sgu_v_gain, sgu_w_s=sgu_w_s,
             sgu_b_s=sgu_b_s, sgu_w_out=sgu_w_out, attn_w_qkv=attn_w_qkv, attn_q_gain=attn_q_gain,
             attn_k_gain=attn_k_gain, attn_sinks=attn_sinks, attn_w_o=attn_w_o, rel_bias=rel_bias, ffn_w_up=ffn_w_up,
             ffn_conv_w=ffn_conv_w, ffn_conv_b=ffn_conv_b, ffn_w_down=ffn_w_down)
    mom = dict(mix_norm=m_mix_norm, ffn_norm=m_ffn_norm, sgu_w_in=m_sgu_w_in, sgu_v_gain=m_sgu_v_gain,
               sgu_w_s=m_sgu_w_s, sgu_b_s=m_sgu_b_s, sgu_w_out=m_sgu_w_out, attn_w_qkv=m_attn_w_qkv,
               attn_q_gain=m_attn_q_gain, attn_k_gain=m_attn_k_gain, attn_sinks=m_attn_sinks, attn_w_o=m_attn_w_o,
               rel_bias=m_rel_bias, ffn_w_up=m_ffn_w_up, ffn_conv_w=m_ffn_conv_w, ffn_conv_b=m_ffn_conv_b,
               ffn_w_down=m_ffn_w_down)
    var = dict(mix_norm=v_mix_norm, ffn_norm=v_ffn_norm, sgu_w_in=v_sgu_w_in, sgu_v_gain=v_sgu_v_gain,
               sgu_w_s=v_sgu_w_s, sgu_b_s=v_sgu_b_s, sgu_w_out=v_sgu_w_out, attn_w_qkv=v_attn_w_qkv,
               attn_q_gain=v_attn_q_gain, attn_k_gain=v_attn_k_gain, attn_sinks=v_attn_sinks, attn_w_o=v_attn_w_o,
               rel_bias=v_rel_bias, ffn_w_up=v_ffn_w_up, ffn_conv_w=v_ffn_conv_w, ffn_conv_b=v_ffn_conv_b,
               ffn_w_down=v_ffn_w_down)
    chip = 2 * lax.axis_index("x") + lax.axis_index("y")
    core = lax.axis_index("c")

    where = jnp.stack([core, chip]).astype(jnp.int32)
    names = ["sgu_w_in", "sgu_w_out", "attn_w_qkv", "attn_w_o", "ffn_w_up0", "ffn_w_up1", "ffn_w_down0", "ffn_w_down1"]
    shards = [(sgu_w_in, 0), (sgu_w_out, 0), (attn_w_qkv, 0), (attn_w_o, 0), (ffn_w_up, 0), (ffn_w_up, 1),
              (ffn_w_down, 0), (ffn_w_down, 1)]
    T = {nm: _place_shard(s, l, where, BF16, "place_" + nm) for (s, l), nm in zip(shards, names)}
    for l in range(2):
        T["conv_w%d" % l] = _place_shard(ffn_conv_w, l, where, F32, "place_conv_w%d" % l)

    def gather(keys):
        items = []
        for k in keys:
            k, lo, hi = (k, 0, None) if isinstance(k, str) else k
            items.append((T[k], lo, T[k].shape[1] if hi is None else hi, not k.startswith("conv")))
        return _gather_comm(items)

    def gathered(keys, res):
        for k, a in zip(keys, res[0]):
            T[k if isinstance(k, str) else k[0]] = a

    D = x.shape[2]
    first = ["sgu_w_in", "conv_w0", "conv_w1"]
    gathered(first, _run_comm(gather(first), "gather_first"))
    unshard_cols = lambda a: jnp.transpose(a, (1, 0, 2)).reshape(a.shape[1], -1)
    cw = [unshard_cols(T["conv_w0"]), unshard_cols(T["conv_w1"])]
    cb = ffn_conv_b
    flat = lambda k: T[k].reshape(-1, D)
    x2, target = x[0], loss_target[0]
    bucket = jnp.asarray(_rel_buckets_flat())
    wtril, wtrilT = _sgu_prep(sgu_w_s[0])
    bT = sgu_b_s[0].T
    bias = _rel_bias_expand(rel_bias, bucket).reshape(N_KV_HEADS, KV_GROUP * CHUNK, 2 * CHUNK)
    qg2 = jnp.tile(attn_q_gain, (1, 2))
    kg2 = jnp.tile(attn_k_gain, (1, 2))
    sinks = attn_sinks.reshape(N_HEADS)
    mix0, mix1 = mix_norm[0:1], mix_norm[1:2]
    fn0, fn1 = ffn_norm[0:1], ffn_norm[1:2]

    ks = ["sgu_w_out", ("ffn_w_up0", 0, 256)]
    (hn0, z), r = _norm_matmul(x2, mix0, T["sgu_w_in"], "sgu_in", comm=gather(ks))
    gathered(ks, r)
    ks = [("ffn_w_up0", 256, None)]
    (yp, h1, cdf), r = _sgu_fwd(z, x2, sgu_v_gain, wtril, bT, flat("sgu_w_out"), comm=gather(ks))
    gathered(ks, r)
    ks = ["ffn_w_down0", "attn_w_qkv"]
    (hf0, a0), r = _norm_matmul(h1, fn0, T["ffn_w_up0"], "ffn0_up", comm=gather(ks))
    gathered(ks, r)
    ks = ["attn_w_o", ("ffn_w_up1", 0, 640)]
    (f0, h2), r = _ffn_fwd(a0, h1, cw[0], cb[0:1], flat("ffn_w_down0"), "ffn0_fwd", comm=gather(ks))
    gathered(ks, r)
    (hn1, qkv), _ = _norm_matmul(h2, mix1, T["attn_w_qkv"], "attn_qkv")
    ks = [("ffn_w_up1", 640, None), "ffn_w_down1"]
    o, r = _attn_fwd(qkv, qg2, kg2, sinks, bias, comm=gather(ks))
    gathered(ks, r)
    h3, _ = _matmul_res(o, flat("attn_w_o"), h2, "attn_out")
    (hf1, a1), _ = _norm_matmul(h3, fn1, T["ffn_w_up1"], "ffn1_up")
    (f1, dh4, loss_local), _ = _ffn_fwd(a1, h3, cw[1], cb[1:2], flat("ffn_w_down1"), "ffn1_fwd", target=target)

    G, RS, PQ, QD, halves, grads = {}, {}, {}, {}, {}, {}
    dest_of = {"sgu_w_in": ("sgu_w_in", 0), "sgu_w_out": ("sgu_w_out", 0), "attn_w_qkv": ("attn_w_qkv", 0),
               "attn_w_o": ("attn_w_o", 0), "ffn_w_up0": ("ffn_w_up", 0), "ffn_w_up1": ("ffn_w_up", 1),
               "ffn_w_down0": ("ffn_w_down", 0), "ffn_w_down1": ("ffn_w_down", 1)}

    def px(keys):
        return _pair_exchange_comm([G[k] for k in keys])

    def px_done(keys, res):
        for k, a in zip(keys, res[1]):
            RS[k] = a
            PQ[k] = _pair_add(G[k], a, where, "pair_add_" + k)

    def sc(keys):
        return _chip_scatter_comm([PQ[k][0] for k in keys], [PQ[k][1] for k in keys])

    def sc_done(keys, res):
        for k, a in zip(keys, res[0]):
            wk, layer = dest_of[k]
            halves[wk] = _sum_chips(a, where, halves.get(wk), layer, w[wk].shape, "sum_chips_" + k)

    nup = ffn_w_up.shape[2]
    ndown = ffn_w_down.shape[1]
    rows4 = lambda a: a.reshape(4, ndown, D)
    (dc1, dcw1, dcb1), _ = _ffn_bwd_dc(dh4, a1, cw[1], cb[1:2], flat("ffn_w_down1"), "ffn1_bwd_dc")
    gw, _ = _matmul_tn(f1, dh4, "ffn1_dw_down", ka=2 * ndown, nb=D, out_dtype=BF16)
    G["ffn_w_down1"] = rows4(gw)
    (dh3, dfn1, da1), _ = _bwd_norm(dc1, T["ffn_w_up1"], h3, fn1, dh4, "ffn1_bwd_in", conv_w=cw[1])
    G["ffn_w_up1"], _ = _matmul_tn(hf1, da1, "ffn1_dw_up", ka=D, nb=nup, out_dtype=BF16)
    ks = ["ffn_w_down1", "ffn_w_up1"]
    G["attn_w_o"], r = _matmul_tn(o, dh3, "attn_dw_o", ka=QW // 4, nb=D, out_dtype=BF16, comm=px(ks))
    px_done(ks, r)
    do, _ = _matmul_nt(dh3, flat("attn_w_o"), "attn_bwd_out")
    (dqkv, dbias, dqg, dkg, dsink), r = _attn_bwd(qkv, do, qg2, kg2, sinks, bias, comm=sc(ks))
    sc_done(ks, r)
    G["attn_w_qkv"], _ = _matmul_tn(hn1, dqkv, "attn_dw_qkv", ka=D, nb=dqkv.shape[1] // 4, out_dtype=BF16)
    ks = ["attn_w_o", "attn_w_qkv"]
    (dh2, dmix1), r = _bwd_norm(dqkv, T["attn_w_qkv"], h2, mix1, dh3, "attn_bwd_in", comm=px(ks))
    px_done(ks, r)
    (dc0, dcw0, dcb0), r = _ffn_bwd_dc(dh2, a0, cw[0], cb[0:1], flat("ffn_w_down0"), "ffn0_bwd_dc", comm=sc(ks))
    sc_done(ks, r)
    gw, _ = _matmul_tn(f0, dh2, "ffn0_dw_down", ka=2 * ndown, nb=D, out_dtype=BF16)
    G["ffn_w_down0"] = rows4(gw)
    (dh1, dfn0, da0), r = _bwd_norm(dc0, T["ffn_w_up0"], h1, fn0, dh2, "ffn0_bwd_in", conv_w=cw[0],
                                    comm=px(["ffn_w_down0"]))
    px_done(["ffn_w_down0"], r)
    G["ffn_w_up0"], r = _matmul_tn(hf0, da0, "ffn0_dw_up", ka=D, nb=nup, out_dtype=BF16, comm=sc(["ffn_w_down0"]))
    sc_done(["ffn_w_down0"], r)
    ks = ["ffn_w_up0"]
    G["sgu_w_out"], r = _matmul_tn(yp, dh1, "sgu_dw_out", ka=yp.shape[1] // 4, nb=D, out_dtype=BF16, comm=px(ks))
    px_done(ks, r)
    both = [sc(ks), px(["sgu_w_out"])]
    (dz, dws, dbT, dvg), r = _sgu_bwd(dh1, z, cdf, sgu_v_gain, wtril, wtrilT, bT, flat("sgu_w_out"),
                                      comm=_join(both))
    r = _split(both, r)
    sc_done(ks, r[0])
    px_done(["sgu_w_out"], r[1])
    done = ["attn_w_qkv", "attn_w_o", "ffn_w_up", "ffn_w_down"]
    both = [sc(["sgu_w_out"]), _half_exchange_comm([halves[k] for k in done])]
    G["sgu_w_in"], r = _matmul_tn(hn0, dz, "sgu_dw_in", ka=D, nb=dz.shape[1] // 4, out_dtype=BF16, comm=_join(both))
    r = _split(both, r)
    sc_done(["sgu_w_out"], r[0])
    for k, a in zip(done, r[1][0]):
        grads[k] = a
    px_done(["sgu_w_in"], _run_comm(px(["sgu_w_in"]), "grad_last_pair_exchange"))
    (grad_x, dmix0), r = _bwd_norm(dz, T["sgu_w_in"], x2, mix0, dh1, "sgu_bwd_in", comm=sc(["sgu_w_in"]))
    sc_done(["sgu_w_in"], r)

    g = dict(mix_norm=jnp.concatenate([dmix0, dmix1], axis=0), ffn_norm=jnp.concatenate([dfn0, dfn1], axis=0),
             sgu_v_gain=dvg, sgu_w_s=dws, sgu_b_s=dbT.T, attn_q_gain=dqg[:, :HEAD_DIM], attn_k_gain=dkg[:, :HEAD_DIM],
             attn_sinks=dsink[:, :N_KV_HEADS].T.reshape(1, N_HEADS),
             rel_bias=_rel_bias_reduce(dbias.reshape(N_HEADS, CHUNK * 2 * CHUNK), bucket),
             ffn_conv_b=jnp.concatenate([dcb0, dcb1], axis=0))
    small_list = [g[k].reshape(w[k].shape) for k in SMALL] + [jnp.stack([dcw0, dcw1]), loss_local]
    small_shapes = [a.shape for a in small_list]
    done = ["sgu_w_in", "sgu_w_out"]
    red, r = _allreduce_small(_pack(small_list, 8 * SMALL_PIECE_ROWS),
                              comm=_half_exchange_comm([halves[k] for k in done]))
    for k, a in zip(done, r[0]):
        grads[k] = a
    red = _unpack(red, small_shapes)
    for k, a in zip(SMALL, red):
        grads[k] = a
    grads["ffn_conv_w"] = lax.dynamic_slice_in_dim(red[-2], chip * ffn_conv_w.shape[2], ffn_conv_w.shape[2], axis=2)
    loss = red[-1][0, 0]

    delta, new_m, new_v = {}, {}, {}
    two = lambda a: a.reshape(-1, a.shape[-1])
    for k in BIG:
        (d2, m2, v2, g2), _ = _adamw(two(grads[k]), two(w[k]), two(mom[k]), two(var[k]), "adamw_" + k)
        delta[k], new_m[k], new_v[k], grads[k] = (a.reshape(w[k].shape) for a in (d2, m2, v2, g2))
    sm = SMALL + ["ffn_conv_w"]
    sm_shapes = [w[k].shape for k in sm]
    rows = sum(_rows_of(s) for s in sm_shapes)
    (d2, m2, v2, _), _ = _adamw(_pack([grads[k] for k in sm], rows), _pack([w[k] for k in sm], rows),
                                _pack([mom[k] for k in sm], rows), _pack([var[k] for k in sm], rows), "adamw_small")
    for dst, buf in ((delta, d2), (new_m, m2), (new_v, v2)):
        for k, a in zip(sm, _unpack(buf, sm_shapes)):
            dst[k] = a

    return (loss, grad_x[None], *[grads[k] for k in ORDER], *[delta[k] for k in ORDER],
            *[new_m[k] for k in ORDER], *[new_v[k] for k in ORDER])
```

```python
import functools
import math

import numpy as np
import jax
import jax.numpy as jnp
from jax import lax
from jax.experimental import pallas as pl
from jax.experimental.pallas import tpu as pltpu

F32 = jnp.float32
BF16 = jnp.bfloat16

EPS = 1e-6
CHUNK = 128
SGU_GROUPS = 16
HEAD_DIM = 64
N_HEADS = 16
N_KV_HEADS = 4
KV_GROUP = N_HEADS // N_KV_HEADS
REL_BUCKETS = 32
REL_MAX_DIST = 128
LANES = 128
HALO = 16

ADAM_LR = 0.001
ADAM_B1 = 0.9
ADAM_B2 = 0.999
ADAM_EPS = 1e-08
ADAM_WD = 0.01
ADAM_STEP = 10

VMEM_LIMIT_V7X = 56 * 1024 * 1024

_SQRT_HALF = math.sqrt(0.5)
_INV_SQRT_2PI = 1.0 / math.sqrt(2.0 * math.pi)


def _cp(sem):
    return pltpu.CompilerParams(dimension_semantics=sem, vmem_limit_bytes=VMEM_LIMIT_V7X)


def _resident(shape):
    nd = len(shape)
    return pl.BlockSpec(shape, lambda *_: (0,) * nd, pipeline_mode=pl.Buffered(1))


class _Comm:
    def __init__(self, srcs, zones, news, sems, start, finish, middle=None):
        self.srcs, self.zones, self.news, self.sems = list(srcs), list(zones), list(news), list(sems)
        self.start, self.finish = start, finish
        self.middle = middle if middle is not None else (lambda srcs, zones, news, sems: None)


def _join(comms):
    comms = [c for c in comms if c is not None]
    if not comms:
        return None

    def part(seq, attr):
        out, k = [], 0
        for c in comms:
            n = len(getattr(c, attr))
            out.append(seq[k:k + n])
            k += n
        return out

    def run(which):
        def f(srcs, zones, news, sems):
            for c, a, b, d, e in zip(comms, part(srcs, "srcs"), part(zones, "zones"), part(news, "news"), part(sems, "sems")):
                getattr(c, which)(a, b, d, e)
        return f

    cat = lambda attr: [v for c in comms for v in getattr(c, attr)]
    return _Comm(cat("srcs"), cat("zones"), cat("news"), cat("sems"), run("start"), run("finish"), run("middle"))


def _split(comms, res):
    zones, news = res
    out, kz, kn = [], 0, 0
    for c in comms:
        out.append((zones[kz:kz + len(c.zones)], news[kn:kn + len(c.news)]))
        kz += len(c.zones)
        kn += len(c.news)
    return out


def _call(body, *, name, grid, in_specs, out_specs, out_shape, args, scratch_shapes=(), sem=None, comm=None):
    if comm is None:
        res = pl.pallas_call(body, name=name, grid=grid, in_specs=in_specs, out_specs=out_specs, out_shape=out_shape,
                             scratch_shapes=list(scratch_shapes), compiler_params=_cp(sem))(*args)
        return res, None
    single = not isinstance(out_shape, (list, tuple))
    out_specs_l = [out_specs] if single else list(out_specs)
    out_shape_l = [out_shape] if single else list(out_shape)
    n_in, n_out, n_scr = len(in_specs), len(out_shape_l), len(scratch_shapes)
    ns, nz, nn = len(comm.srcs), len(comm.zones), len(comm.news)

    def wrapped(*refs):
        k = n_in
        ins, srcs = refs[:k], refs[k:k + ns]
        k += ns + nz
        outs, zones, news = refs[k:k + n_out], refs[k + n_out:k + n_out + nz], refs[k + n_out + nz:k + n_out + nz + nn]
        k += n_out + nz + nn
        scr, sems = refs[k:k + n_scr], refs[k + n_scr:]
        if not grid:
            comm.start(srcs, zones, news, sems)
            body(*ins, *outs, *scr)
            comm.middle(srcs, zones, news, sems)
            comm.finish(srcs, zones, news, sems)
            return
        first = functools.reduce(jnp.logical_and, [pl.program_id(a) == 0 for a in range(len(grid))])
        last = functools.reduce(jnp.logical_and, [pl.program_id(a) == grid[a] - 1 for a in range(len(grid))])
        early = len(grid) == 1 and grid[0] >= 2
        mid_step = grid[0] - (2 if grid[0] >= 8 else 1)

        @pl.when(first)
        def _():
            comm.start(srcs, zones, news, sems)

        if early:
            @pl.when(pl.program_id(0) == mid_step)
            def _():
                comm.middle(srcs, zones, news, sems)

        body(*ins, *outs, *scr)

        @pl.when(last)
        def _():
            if not early:
                comm.middle(srcs, zones, news, sems)
            comm.finish(srcs, zones, news, sems)

    res = pl.pallas_call(
        wrapped, name=name, grid=grid,
        in_specs=list(in_specs) + [ANY] * (ns + nz), out_specs=out_specs_l + [ANY] * (nz + nn),
        out_shape=out_shape_l + [jax.ShapeDtypeStruct(z.shape, z.dtype) for z in comm.zones] + comm.news,
        input_output_aliases={n_in + ns + i: n_out + i for i in range(nz)},
        scratch_shapes=list(scratch_shapes) + comm.sems,
        compiler_params=_cp(("arbitrary",) * len(grid)),
    )(*args, *comm.srcs, *comm.zones)
    main = res[0] if single else list(res[:n_out])
    return main, (list(res[n_out:n_out + nz]), list(res[n_out + nz:]))


def _run_comm(comm, name):
    ns, nz, nn = len(comm.srcs), len(comm.zones), len(comm.news)

    def body(*refs):
        srcs, zones, news, sems = refs[:ns], refs[ns + nz:ns + 2 * nz], refs[ns + 2 * nz:ns + 2 * nz + nn], refs[ns + 2 * nz + nn:]
        comm.start(srcs, zones, news, sems)
        comm.middle(srcs, zones, news, sems)
        comm.finish(srcs, zones, news, sems)

    res = pl.pallas_call(
        body, name=name, in_specs=[ANY] * (ns + nz), out_specs=[ANY] * (nz + nn),
        out_shape=[jax.ShapeDtypeStruct(z.shape, z.dtype) for z in comm.zones] + comm.news,
        input_output_aliases={ns + i: i for i in range(nz)}, scratch_shapes=comm.sems,
    )(*comm.srcs, *comm.zones)
    return list(res[:nz]), list(res[nz:])


def _dot(a, b):
    return jnp.dot(a, b, preferred_element_type=F32)


def _dot_nt(a, b):
    return lax.dot_general(a, b, (((1,), (1,)), ((), ())), preferred_element_type=F32)


def _dot_tn(a, b):
    return lax.dot_general(a, b, (((0,), (0,)), ((), ())), preferred_element_type=F32)


def _normal_cdf(x):
    return 0.5 * (1.0 + lax.erf(x * _SQRT_HALF))


def _gelu_and_grad(x, cdf):
    return x * cdf, cdf + x * jnp.exp(-0.5 * x * x) * _INV_SQRT_2PI


def _sigmoid(x):
    return 0.5 * jnp.tanh(0.5 * x) + 0.5


def _rms_bwd(dy, x, gain):
    r = lax.rsqrt(jnp.mean(x * x, axis=-1, keepdims=True) + EPS)
    xhat = x * r
    gdy = dy * gain
    dx = r * (gdy - xhat * jnp.mean(gdy * xhat, axis=-1, keepdims=True))
    return dx, jnp.sum(dy * xhat, axis=0, keepdims=True)


def _norm_matmul(x, gain, w4, name, comm=None, tm=1024):
    S, D = x.shape
    nsh, _, ns = w4.shape
    tm = min(tm, S)

    def body(x_ref, g_ref, w_ref, hn_ref, o_ref):
        xf = x_ref[...]
        r = lax.rsqrt(jnp.mean(xf * xf, axis=-1, keepdims=True) + EPS)
        hn = (xf * r * g_ref[...]).astype(BF16)
        hn_ref[...] = hn
        for j in range(nsh):
            o_ref[:, j * ns:(j + 1) * ns] = _dot(hn, w_ref[j]).astype(BF16)

    return _call(
        body, name=name, grid=(S // tm,),
        in_specs=[pl.BlockSpec((tm, D), lambda i: (i, 0)), _resident((1, D)), _resident(w4.shape)],
        out_specs=[pl.BlockSpec((tm, D), lambda i: (i, 0)), pl.BlockSpec((tm, nsh * ns), lambda i: (i, 0))],
        out_shape=[jax.ShapeDtypeStruct((S, D), BF16), jax.ShapeDtypeStruct((S, nsh * ns), BF16)],
        sem=("parallel",), args=(x, gain, w4), comm=comm)


def _matmul_res(a, w, res, name, comm=None, tm=1024):
    S, K = a.shape
    N = w.shape[1]
    tm = min(tm, S)

    def body(a_ref, w_ref, r_ref, o_ref):
        o_ref[...] = r_ref[...] + _dot(a_ref[...], w_ref[...])

    return _call(
        body, name=name, grid=(S // tm,),
        in_specs=[pl.BlockSpec((tm, K), lambda i: (i, 0)), _resident(w.shape), pl.BlockSpec((tm, N), lambda i: (i, 0))],
        out_specs=pl.BlockSpec((tm, N), lambda i: (i, 0)),
        out_shape=jax.ShapeDtypeStruct((S, N), F32),
        sem=("parallel",), args=(a, w, res), comm=comm)


def _matmul_nt(dh, w, name, comm=None, tm=1024):
    S, N = dh.shape
    K = w.shape[0]
    tm = min(tm, S)

    def body(d_ref, w_ref, o_ref):
        o_ref[...] = _dot_nt(d_ref[...].astype(BF16), w_ref[...]).astype(BF16)

    return _call(
        body, name=name, grid=(S // tm,),
        in_specs=[pl.BlockSpec((tm, N), lambda i: (i, 0)), _resident(w.shape)],
        out_specs=pl.BlockSpec((tm, K), lambda i: (i, 0)),
        out_shape=jax.ShapeDtypeStruct((S, K), BF16),
        sem=("parallel",), args=(dh, w), comm=comm)


def _matmul_tn(a, b, name, *, ka, nb, out_dtype, comm=None, ts=1024):
    S, KA = a.shape
    NB = b.shape[1]
    ts = min(ts, S)
    J = max(KA // ka, NB // nb)
    a_map = (lambda j, s: (s, j)) if KA // ka > 1 else (lambda j, s: (s, 0))
    b_map = (lambda j, s: (s, j)) if NB // nb > 1 else (lambda j, s: (s, 0))
    last = S // ts - 1

    def body(a_ref, b_ref, o_ref, acc):
        s = pl.program_id(1)

        @pl.when(s == 0)
        def _():
            acc[...] = jnp.zeros_like(acc)

        acc[...] += _dot_tn(a_ref[...].astype(BF16), b_ref[...].astype(BF16))

        @pl.when(s == last)
        def _():
            o_ref[0] = acc[...].astype(out_dtype)

    return _call(
        body, name=name, grid=(J, S // ts),
        in_specs=[pl.BlockSpec((ts, ka), a_map), pl.BlockSpec((ts, nb), b_map)],
        out_specs=pl.BlockSpec((1, ka, nb), lambda j, s: (j, 0, 0)),
        out_shape=jax.ShapeDtypeStruct((J, ka, nb), out_dtype),
        scratch_shapes=[pltpu.VMEM((ka, nb), F32)],
        sem=("parallel", "arbitrary"), args=(a, b), comm=comm)


def _sgu_prep(w_s):
    G = w_s.shape[0]

    def body(w_ref, t_ref, tt_ref):
        tri = lax.broadcasted_iota(jnp.int32, (CHUNK, CHUNK), 0) >= lax.broadcasted_iota(jnp.int32, (CHUNK, CHUNK), 1)
        for g in range(G):
            t = jnp.where(tri, w_ref[g], 0.0)
            t_ref[g] = t.astype(BF16)
            tt_ref[g] = t.T.astype(BF16)

    return pl.pallas_call(
        body, name="sgu_prep",
        out_shape=[jax.ShapeDtypeStruct(w_s.shape, BF16), jax.ShapeDtypeStruct(w_s.shape, BF16)],
        compiler_params=_cp(None),
    )(w_s)


def _sgu_fwd(z, x, vg, wtril, bT, wout, comm=None, tm=512):
    S = z.shape[0]
    W = z.shape[1] // 2
    D = x.shape[1]
    tm = min(tm, S)

    def body(z_ref, x_ref, vg_ref, wt_ref, bT_ref, wo_ref, yp_ref, h_ref, cdf_ref):
        def chunk(c, carry):
            r0 = pl.multiple_of(c * CHUNK, CHUNK)
            zc = z_ref[pl.ds(r0, CHUNK), :].astype(F32)
            cdf = _normal_cdf(zc)
            cdf_ref[pl.ds(r0, CHUNK), :] = cdf.astype(BF16)
            u = zc[:, :W] * cdf[:, :W]
            v = zc[:, W:] * cdf[:, W:]
            rv = lax.rsqrt(jnp.mean(v * v, axis=-1, keepdims=True) + EPS)
            vn = (v * rv * vg_ref[...]).astype(BF16)
            for g in range(SGU_GROUPS):
                sl = slice(g * LANES, (g + 1) * LANES)
                s = _dot(wt_ref[g], vn[:, sl]) + bT_ref[:, g:g + 1]
                yp_ref[pl.ds(r0, CHUNK), sl] = (u[:, sl] * s).astype(BF16)
            return carry

        lax.fori_loop(0, tm // CHUNK, chunk, 0)
        h_ref[...] = x_ref[...] + _dot(yp_ref[...], wo_ref[...])

    return _call(
        body, name="sgu_fwd", grid=(S // tm,),
        in_specs=[pl.BlockSpec((tm, 2 * W), lambda i: (i, 0)), pl.BlockSpec((tm, D), lambda i: (i, 0)),
                  _resident((1, W)), _resident(wtril.shape), _resident(bT.shape), _resident(wout.shape)],
        out_specs=[pl.BlockSpec((tm, W), lambda i: (i, 0)), pl.BlockSpec((tm, D), lambda i: (i, 0)),
                   pl.BlockSpec((tm, 2 * W), lambda i: (i, 0))],
        out_shape=[jax.ShapeDtypeStruct((S, W), BF16), jax.ShapeDtypeStruct((S, D), F32),
                   jax.ShapeDtypeStruct((S, 2 * W), BF16)],
        sem=("parallel",), args=(z, x, vg, wtril, bT, wout), comm=comm)


def _sgu_bwd(dh, z, cdf, vg, wtril, wtrilT, bT, wout, comm=None, tm=512):
    S = z.shape[0]
    W = z.shape[1] // 2
    D = dh.shape[1]
    G = SGU_GROUPS
    tm = min(tm, S)
    last = S // tm - 1

    def body(dh_ref, z_ref, cdf_ref, vg_ref, wt_ref, wtT_ref, bT_ref, wo_ref,
             dz_ref, dws_ref, dbT_ref, dvg_ref, dyp_s, du_s, dvn_s, dsacc):
        i = pl.program_id(0)

        @pl.when(i == 0)
        def _():
            dws_ref[...] = jnp.zeros_like(dws_ref)
            dvg_ref[...] = jnp.zeros_like(dvg_ref)
            dsacc[...] = jnp.zeros_like(dsacc)

        dyp_s[...] = _dot_nt(dh_ref[...].astype(BF16), wo_ref[...])
        tri = lax.broadcasted_iota(jnp.int32, (CHUNK, CHUNK), 0) >= lax.broadcasted_iota(jnp.int32, (CHUNK, CHUNK), 1)

        def chunk(c, carry):
            r0 = pl.multiple_of(c * CHUNK, CHUNK)
            zc = z_ref[pl.ds(r0, CHUNK), :].astype(F32)
            cdf = cdf_ref[pl.ds(r0, CHUNK), :].astype(F32)
            u, gu = _gelu_and_grad(zc[:, :W], cdf[:, :W])
            v, gv = _gelu_and_grad(zc[:, W:], cdf[:, W:])
            rv = lax.rsqrt(jnp.mean(v * v, axis=-1, keepdims=True) + EPS)
            vhat = v * rv
            vgain = vg_ref[...]
            vn = (vhat * vgain).astype(BF16)
            dyp = dyp_s[pl.ds(r0, CHUNK), :]
            for g in range(G):
                sl = slice(g * LANES, (g + 1) * LANES)
                vng = vn[:, sl]
                s = _dot(wt_ref[g], vng) + bT_ref[:, g:g + 1]
                ds = dyp[:, sl] * u[:, sl]
                du_s[:, sl] = dyp[:, sl] * s
                dsb = ds.astype(BF16)
                dvn_s[:, sl] = _dot(wtT_ref[g], dsb)
                dws_ref[g] += jnp.where(tri, _dot_nt(dsb, vng), 0.0)
                dsacc[g] += ds
            dvn = dvn_s[...]
            dvg_ref[...] += jnp.sum(dvn * vhat, axis=0, keepdims=True)
            gdy = dvn * vgain
            dv = rv * (gdy - vhat * jnp.mean(gdy * vhat, axis=-1, keepdims=True))
            dz_ref[pl.ds(r0, CHUNK), :W] = (du_s[...] * gu).astype(BF16)
            dz_ref[pl.ds(r0, CHUNK), W:] = (dv * gv).astype(BF16)
            return carry

        lax.fori_loop(0, tm // CHUNK, chunk, 0)

        @pl.when(i == last)
        def _():
            for g in range(G):
                dbT_ref[:, g:g + 1] = jnp.sum(dsacc[g], axis=1, keepdims=True)

    return _call(
        body, name="sgu_bwd", grid=(S // tm,),
        in_specs=[pl.BlockSpec((tm, D), lambda i: (i, 0)), pl.BlockSpec((tm, 2 * W), lambda i: (i, 0)),
                  pl.BlockSpec((tm, 2 * W), lambda i: (i, 0)),
                  _resident((1, W)), _resident(wtril.shape), _resident(wtrilT.shape), _resident(bT.shape),
                  _resident(wout.shape)],
        out_specs=[pl.BlockSpec((tm, 2 * W), lambda i: (i, 0)),
                   pl.BlockSpec((G, CHUNK, CHUNK), lambda i: (0, 0, 0)),
                   pl.BlockSpec((CHUNK, G), lambda i: (0, 0)),
                   pl.BlockSpec((1, W), lambda i: (0, 0))],
        out_shape=[jax.ShapeDtypeStruct((S, 2 * W), BF16), jax.ShapeDtypeStruct((G, CHUNK, CHUNK), F32),
                   jax.ShapeDtypeStruct((CHUNK, G), F32), jax.ShapeDtypeStruct((1, W), F32)],
        scratch_shapes=[pltpu.VMEM((tm, W), F32), pltpu.VMEM((CHUNK, W), F32), pltpu.VMEM((CHUNK, W), F32),
                        pltpu.VMEM((G, CHUNK, CHUNK), F32)],
        sem=("arbitrary",), args=(dh, z, cdf, vg, wtril, wtrilT, bT, wout), comm=comm)


def _conv_taps(a32, r0, R, reuse=False):
    base = r0 + HALO
    if not reuse:
        return a32[base:base + R, :], a32[base - 1:base - 1 + R, :], a32[base - 2:base - 2 + R, :]
    X = a32[base - 8:base + R, :]
    return X[8:], pltpu.roll(X, 1, 0)[8:], pltpu.roll(X, 2, 0)[8:]


def _ffn_fwd(a, h_in, cw, cb, wdown, name, comm=None, target=None, tm=512, R=64):
    S, C = a.shape
    F = C // 2
    D = h_in.shape[1]
    tm = min(tm, S)
    head = target is not None

    def body(a_ref, halo_ref, h_ref, cw_ref, cb_ref, wd_ref, *rest):
        if head:
            t_ref, f_ref, dy_ref, l_ref, a32 = rest
        else:
            f_ref, ho_ref, a32 = rest
        i = pl.program_id(0)
        a32[0:HALO, :] = jnp.where(i > 0, halo_ref[...].astype(F32), 0.0)
        a32[HALO:, :] = a_ref[...].astype(F32)

        for r0 in range(0, tm, R):
            a0, a1, a2 = _conv_taps(a32, r0, R)
            cpre = cw_ref[0:1, :] * a2 + cw_ref[1:2, :] * a1 + cw_ref[2:3, :] * a0 + cb_ref[...]
            g = cpre[:, :F]
            f_ref[r0:r0 + R, :] = (g * _sigmoid(g) * cpre[:, F:]).astype(BF16)
        h_out = h_ref[...] + _dot(f_ref[...], wd_ref[...])
        if not head:
            ho_ref[...] = h_out
            return

        @pl.when(i == 0)
        def _():
            l_ref[...] = jnp.zeros_like(l_ref)

        e = h_out - t_ref[...]
        dy_ref[...] = e * (1.0 / D)
        rows = jnp.sum(e * e, axis=-1, keepdims=True) * (1.0 / D)
        l_ref[...] += 0.5 * jnp.sum(rows, axis=0, keepdims=True)

    hb = tm // HALO
    row = pl.BlockSpec((tm, D), lambda i: (i, 0))
    in_specs = [pl.BlockSpec((tm, C), lambda i: (i, 0)),
                pl.BlockSpec((HALO, C), lambda i: (jnp.maximum(i * hb - 1, 0), 0)),
                row, _resident((3, C)), _resident((1, C)), _resident(wdown.shape)]
    out_specs = [pl.BlockSpec((tm, F), lambda i: (i, 0)), row]
    out_shape = [jax.ShapeDtypeStruct((S, F), BF16), jax.ShapeDtypeStruct((S, D), F32)]
    args = (a, a, h_in, cw, cb, wdown)
    if head:
        in_specs.append(row)
        out_specs.append(pl.BlockSpec((1, 1), lambda i: (0, 0)))
        out_shape.append(jax.ShapeDtypeStruct((1, 1), F32))
        args += (target,)
    return _call(
        body, name=name, grid=(S // tm,), in_specs=in_specs, out_specs=out_specs, out_shape=out_shape,
        scratch_shapes=[pltpu.VMEM((HALO + tm, C), F32)],
        sem=("arbitrary",) if head else ("parallel",), args=args, comm=comm)


def _ffn_bwd_dc(dh, a, cw, cb, wdown, name, comm=None, tm=256, R=64):
    S, C = a.shape
    F = C // 2
    D = dh.shape[1]
    tm = min(tm, S)

    def body(dh_ref, a_ref, halo_ref, cw_ref, cb_ref, wd_ref, dc_ref, dcw_ref, dcb_ref, a32, df_s, acc):
        i = pl.program_id(0)

        @pl.when(i == 0)
        def _():
            acc[...] = jnp.zeros_like(acc)

        a32[0:HALO, :] = jnp.where(i > 0, halo_ref[...].astype(F32), 0.0)
        a32[HALO:, :] = a_ref[...].astype(F32)
        df_s[...] = _dot_nt(dh_ref[...].astype(BF16), wd_ref[...])
        rows8 = lambda v: functools.reduce(jnp.add, [v[8 * k:8 * k + 8] for k in range(R // 8)])

        for r0 in range(0, tm, R):
            a0, a1, a2 = _conv_taps(a32, r0, R, reuse=True)
            cpre = cw_ref[0:1, :] * a2 + cw_ref[1:2, :] * a1 + cw_ref[2:3, :] * a0 + cb_ref[...]
            g = cpre[:, :F]
            val = cpre[:, F:]
            sg = _sigmoid(g)
            df = df_s[r0:r0 + R, :]
            dg = df * val * (sg * (1.0 + g * (1.0 - sg)))
            dval = df * (g * sg)
            dc = jnp.concatenate([dg, dval], axis=1)
            dc_ref[r0:r0 + R, :] = dc.astype(BF16)
            acc[0] += rows8(dc * a2)
            acc[1] += rows8(dc * a1)
            acc[2] += rows8(dc * a0)
            acc[3] += rows8(dc)

        @pl.when(i == S // tm - 1)
        def _():
            for k in range(3):
                dcw_ref[k:k + 1, :] = jnp.sum(acc[k], axis=0, keepdims=True)
            dcb_ref[...] = jnp.sum(acc[3], axis=0, keepdims=True)

    hb = tm // HALO
    return _call(
        body, name=name, grid=(S // tm,),
        in_specs=[pl.BlockSpec((tm, D), lambda i: (i, 0)),
                  pl.BlockSpec((tm, C), lambda i: (i, 0)),
                  pl.BlockSpec((HALO, C), lambda i: (jnp.maximum(i * hb - 1, 0), 0)),
                  _resident((3, C)), _resident((1, C)), _resident(wdown.shape)],
        out_specs=[pl.BlockSpec((tm, C), lambda i: (i, 0)),
                   pl.BlockSpec((3, C), lambda i: (0, 0)), pl.BlockSpec((1, C), lambda i: (0, 0))],
        out_shape=[jax.ShapeDtypeStruct((S, C), BF16), jax.ShapeDtypeStruct((3, C), F32),
                   jax.ShapeDtypeStruct((1, C), F32)],
        scratch_shapes=[pltpu.VMEM((HALO + tm, C), F32), pltpu.VMEM((tm, F), F32), pltpu.VMEM((4, 8, C), F32)],
        sem=("arbitrary",), args=(dh, a, a, cw, cb, wdown), comm=comm)


def _bwd_norm(dA, w4, h_in, gain, dh_out, name, conv_w=None, comm=None, R=64):
    S, N = dA.shape
    nsh, D, ns = w4.shape
    conv = conv_w is not None
    tm = min(256 if conv else 512, S)
    nt = S // tm

    def finish(src_ref, w_ref, h_ref, g_ref, dho_ref, dhi_ref, dg_ref):
        dhn = _dot_nt(src_ref[:, 0:ns], w_ref[0])
        for j in range(1, nsh):
            dhn += _dot_nt(src_ref[:, j * ns:(j + 1) * ns], w_ref[j])
        dx, dgain = _rms_bwd(dhn, h_ref[...], g_ref[...])
        dg_ref[...] += dgain
        dhi_ref[...] = dho_ref[...] + dx

    def body_plain(dA_ref, w_ref, h_ref, g_ref, dho_ref, dhi_ref, dg_ref):
        @pl.when(pl.program_id(0) == 0)
        def _():
            dg_ref[...] = jnp.zeros_like(dg_ref)

        finish(dA_ref, w_ref, h_ref, g_ref, dho_ref, dhi_ref, dg_ref)

    def body_conv(dc_ref, halo_ref, cw_ref, w_ref, h_ref, g_ref, dho_ref, dhi_ref, dg_ref, da_ref, dc32):
        i = pl.program_id(0)

        @pl.when(i == 0)
        def _():
            dg_ref[...] = jnp.zeros_like(dg_ref)

        dc32[0:tm, :] = dc_ref[...].astype(F32)
        dc32[tm:, :] = jnp.where(i < nt - 1, halo_ref[...].astype(F32), 0.0)

        for r0 in range(0, tm, R):
            d0 = dc32[r0:r0 + R, :]
            d1 = dc32[r0 + 1:r0 + 1 + R, :]
            d2 = dc32[r0 + 2:r0 + 2 + R, :]
            da = cw_ref[2:3, :] * d0 + cw_ref[1:2, :] * d1 + cw_ref[0:1, :] * d2
            da_ref[r0:r0 + R, :] = da.astype(BF16)
        finish(da_ref, w_ref, h_ref, g_ref, dho_ref, dhi_ref, dg_ref)

    row = lambda width: pl.BlockSpec((tm, width), lambda i: (i, 0))
    common_in = [_resident(w4.shape), row(D), _resident((1, D)), row(D)]
    common_out = [row(D), pl.BlockSpec((1, D), lambda i: (0, 0))]
    common_shape = [jax.ShapeDtypeStruct((S, D), F32), jax.ShapeDtypeStruct((1, D), F32)]
    if not conv:
        return _call(
            body_plain, name=name, grid=(nt,),
            in_specs=[row(N)] + common_in, out_specs=common_out, out_shape=common_shape,
            sem=("arbitrary",), args=(dA, w4, h_in, gain, dh_out), comm=comm)
    hb = tm // HALO
    nhb = S // HALO
    return _call(
        body_conv, name=name, grid=(nt,),
        in_specs=[row(N), pl.BlockSpec((HALO, N), lambda i: (jnp.minimum((i + 1) * hb, nhb - 1), 0)),
                  _resident((3, N))] + common_in,
        out_specs=common_out + [row(N)],
        out_shape=common_shape + [jax.ShapeDtypeStruct((S, N), BF16)],
        scratch_shapes=[pltpu.VMEM((tm + HALO, N), F32)],
        sem=("arbitrary",), args=(dA, dA, conv_w, w4, h_in, gain, dh_out), comm=comm)


def _rel_buckets_flat():
    q = np.arange(CHUNK)[:, None] + CHUNK
    k = np.arange(2 * CHUNK)[None, :]
    n = np.maximum(q - k, 0)
    max_exact = REL_BUCKETS // 2
    large = max_exact + (np.log(np.maximum(n, 1).astype(np.float32) / max_exact)
                         / math.log(REL_MAX_DIST / max_exact) * (REL_BUCKETS - max_exact)).astype(np.int32)
    large = np.minimum(large, REL_BUCKETS - 1)
    return np.where(n < max_exact, n, large).astype(np.int32).reshape(1, CHUNK * 2 * CHUNK)


def _split_bf16(x):
    hi = x.astype(BF16)
    return hi, (x - hi.astype(F32)).astype(BF16)


def _rel_bias_expand(rel_bias, bucket):
    B, H = rel_bias.shape
    n = bucket.shape[1]

    def body(rb_ref, bk_ref, o_ref):
        oh = (bk_ref[...] == lax.broadcasted_iota(jnp.int32, (B, n), 0)).astype(BF16)
        hi, lo = _split_bf16(rb_ref[...])
        o_ref[...] = _dot_tn(hi, oh) + _dot_tn(lo, oh)

    return pl.pallas_call(body, name="rel_bias_expand", out_shape=jax.ShapeDtypeStruct((H, n), F32),
                          compiler_params=_cp(None))(rel_bias, bucket)


def _rel_bias_reduce(dbias, bucket):
    H, n = dbias.shape
    B = REL_BUCKETS

    def body(db_ref, bk_ref, o_ref):
        oh = (bk_ref[...] == lax.broadcasted_iota(jnp.int32, (B, n), 0)).astype(BF16)
        hi, lo = _split_bf16(db_ref[...])
        o_ref[...] = _dot_nt(oh, hi) + _dot_nt(oh, lo)

    return pl.pallas_call(body, name="rel_bias_reduce", out_shape=jax.ShapeDtypeStruct((B, H), F32),
                          compiler_params=_cp(None))(dbias, bucket)


def _lo_mask(rows):
    return lax.broadcasted_iota(jnp.int32, (rows, LANES), 1) < HEAD_DIM


def _half_sums(y, lo):
    s_lo = jnp.sum(jnp.where(lo, y, 0.0), axis=-1, keepdims=True)
    s_hi = jnp.sum(jnp.where(lo, 0.0, y), axis=-1, keepdims=True)
    return jnp.where(lo, s_lo, s_hi)


def _half_rms(x, gain, lo):
    r = lax.rsqrt(_half_sums(x * x, lo) * (1.0 / HEAD_DIM) + EPS)
    xhat = x * r
    return xhat * gain, xhat, r


def _half_rms_bwd(dy, xhat, r, gain, lo):
    gdy = dy * gain
    dx = r * (gdy - xhat * (_half_sums(gdy * xhat, lo) * (1.0 / HEAD_DIM)))
    return dx, jnp.sum(dy * xhat, axis=0, keepdims=True)


def _dup_half(pair, e, lo):
    sw = pltpu.roll(pair, HEAD_DIM, 1)
    return jnp.where(lo, pair, sw) if e == 0 else jnp.where(lo, sw, pair)


def _band_valid(n):
    qi = lax.broadcasted_iota(jnp.int32, (KV_GROUP * CHUNK, 2 * CHUNK), 0) & (CHUNK - 1)
    ki = lax.broadcasted_iota(jnp.int32, (KV_GROUP * CHUNK, 2 * CHUNK), 1)
    dist = qi + CHUNK - ki
    return (dist >= 0) & (dist < CHUNK) & ((n > 0) | (ki >= CHUNK))


def _stack_heads(a, b, lo):
    return jnp.concatenate([jnp.where(lo, a, 0.0), jnp.where(lo, 0.0, a), jnp.where(lo, b, 0.0), jnp.where(lo, 0.0, b)],
                           axis=0)


def _unstack_heads(x4, lo):
    return (jnp.where(lo, x4[0:CHUNK], x4[CHUNK:2 * CHUNK]),
            jnp.where(lo, x4[2 * CHUNK:3 * CHUNK], x4[3 * CHUNK:]))


def _sink_col(sink_ref, hk):
    row = lax.broadcasted_iota(jnp.int32, (KV_GROUP * CHUNK, 1), 0)
    col = jnp.full((KV_GROUP * CHUNK, 1), sink_ref[KV_GROUP * hk + KV_GROUP - 1], F32)
    for r in range(KV_GROUP - 2, -1, -1):
        col = jnp.where(row < (r + 1) * CHUNK, sink_ref[KV_GROUP * hk + r], col)
    return col


def _softmax_sink(s, valid, sink):
    s = jnp.where(valid, s, -jnp.inf)
    m = jnp.maximum(jnp.max(s, axis=-1, keepdims=True), sink)
    p = jnp.exp(s - m)
    esink = jnp.exp(sink - m)
    inv = 1.0 / (jnp.sum(p, axis=-1, keepdims=True) + esink)
    return p * inv, esink * inv


QW = N_HEADS * HEAD_DIM
KVW = N_KV_HEADS * HEAD_DIM


def _attn_fwd(qkv, qg2, kg2, sinks, bias, comm=None):
    S = qkv.shape[0]
    nb = S // CHUNK

    def body(cur_ref, prev_ref, qg_ref, kg_ref, sink_ref, bias_ref, o_ref, p_ref, ps_ref):
        n = pl.program_id(0)
        lo = _lo_mask(CHUNK)
        lo2 = _lo_mask(2 * CHUNK)
        valid = _band_valid(n)
        for j in range(N_KV_HEADS // 2):
            kc = slice(QW + j * LANES, QW + (j + 1) * LANES)
            vc = slice(QW + KVW + j * LANES, QW + KVW + (j + 1) * LANES)
            kpair = jnp.concatenate([prev_ref[:, j * LANES:(j + 1) * LANES], cur_ref[:, kc]], axis=0).astype(F32)
            vpair = jnp.concatenate([prev_ref[:, KVW + j * LANES:KVW + (j + 1) * LANES], cur_ref[:, vc]], axis=0).astype(F32)
            knpair, _, _ = _half_rms(kpair, kg_ref[...], lo2)
            for e in range(2):
                hk = 2 * j + e
                kdup = _dup_half(knpair, e, lo2).astype(BF16)
                vdup = _dup_half(vpair, e, lo2).astype(BF16)
                ca = slice(2 * hk * LANES, (2 * hk + 1) * LANES)
                cb = slice((2 * hk + 1) * LANES, (2 * hk + 2) * LANES)
                qna, _, _ = _half_rms(cur_ref[:, ca].astype(F32), qg_ref[...], lo)
                qnb, _, _ = _half_rms(cur_ref[:, cb].astype(F32), qg_ref[...], lo)
                qm4 = _stack_heads(qna, qnb, lo).astype(BF16)
                s = _dot_nt(qm4, kdup) * (HEAD_DIM ** -0.5) + bias_ref[hk]
                p, psink = _softmax_sink(s, valid, _sink_col(sink_ref, hk))
                pb = p.astype(BF16)
                p_ref[0, hk] = pb
                for r in range(KV_GROUP):
                    h = KV_GROUP * hk + r
                    ps_ref[:, h:h + 1] = psink[r * CHUNK:(r + 1) * CHUNK]
                oa, ob = _unstack_heads(_dot(pb, vdup), lo)
                o_ref[:, ca] = oa.astype(BF16)
                o_ref[:, cb] = ob.astype(BF16)

    pshape = (nb, N_KV_HEADS, KV_GROUP * CHUNK, 2 * CHUNK)
    return _call(
        body, name="attn_fwd", grid=(nb,),
        in_specs=[pl.BlockSpec((CHUNK, QW + 2 * KVW), lambda n: (n, 0)),
                  pl.BlockSpec((CHUNK, 2 * KVW), lambda n: (jnp.maximum(n - 1, 0), QW // (2 * KVW))),
                  _resident((1, LANES)), _resident((1, LANES)),
                  pl.BlockSpec(memory_space=pltpu.SMEM),
                  _resident(bias.shape)],
        out_specs=[pl.BlockSpec((CHUNK, QW), lambda n: (n, 0)),
                   pl.BlockSpec((1,) + pshape[1:], lambda n: (n, 0, 0, 0)),
                   pl.BlockSpec((CHUNK, LANES), lambda n: (n, 0))],
        out_shape=[jax.ShapeDtypeStruct((S, QW), BF16), jax.ShapeDtypeStruct(pshape, BF16),
                   jax.ShapeDtypeStruct((S, LANES), F32)],
        sem=("parallel",), args=(qkv, qkv, qg2, kg2, sinks, bias), comm=comm)


def _attn_bwd(qkv, do, probs, psinks, qg2, kg2, comm=None):
    S = qkv.shape[0]
    nb = S // CHUNK

    def body(cur_ref, prev_ref, do_ref, p_ref, ps_ref, qg_ref, kg_ref,
             dqkv_ref, dbias_ref, dqg_ref, dkg_ref, dsink_ref, carry, band, dsacc, gacc):
        i = pl.program_id(0)
        lo = _lo_mask(CHUNK)
        lo2 = _lo_mask(2 * CHUNK)
        lane = lax.broadcasted_iota(jnp.int32, (KV_GROUP * CHUNK, LANES), 1)

        @pl.when(i == 0)
        def _():
            dbias_ref[...] = jnp.zeros_like(dbias_ref)
            carry[...] = jnp.zeros_like(carry)
            dsacc[...] = jnp.zeros_like(dsacc)
            gacc[...] = jnp.zeros_like(gacc)

        qgain = qg_ref[...]
        kgain = kg_ref[...]
        for j in range(N_KV_HEADS // 2):
            kc = slice(QW + j * LANES, QW + (j + 1) * LANES)
            vc = slice(QW + KVW + j * LANES, QW + KVW + (j + 1) * LANES)
            kpair = jnp.concatenate([prev_ref[:, j * LANES:(j + 1) * LANES], cur_ref[:, kc]], axis=0).astype(F32)
            vpair = jnp.concatenate([prev_ref[:, KVW + j * LANES:KVW + (j + 1) * LANES], cur_ref[:, vc]], axis=0).astype(F32)
            knpair, khat, kr = _half_rms(kpair, kgain, lo2)
            dk_folds = []
            dv_folds = []
            for e in range(2):
                hk = 2 * j + e
                kdup = _dup_half(knpair, e, lo2).astype(BF16)
                vdup = _dup_half(vpair, e, lo2).astype(BF16)
                ca = slice(2 * hk * LANES, (2 * hk + 1) * LANES)
                cb = slice((2 * hk + 1) * LANES, (2 * hk + 2) * LANES)
                qna, qhata, qra = _half_rms(cur_ref[:, ca].astype(F32), qgain, lo)
                qnb, qhatb, qrb = _half_rms(cur_ref[:, cb].astype(F32), qgain, lo)
                qm4 = _stack_heads(qna, qnb, lo).astype(BF16)
                dom4 = _stack_heads(do_ref[:, ca].astype(F32), do_ref[:, cb].astype(F32), lo).astype(BF16)
                pb = p_ref[0, hk]
                p = pb.astype(F32)
                psink = jnp.concatenate([ps_ref[:, KV_GROUP * hk + r:KV_GROUP * hk + r + 1] for r in range(KV_GROUP)],
                                        axis=0)
                dp = _dot_nt(dom4, vdup)
                delta = jnp.sum(p * dp, axis=-1, keepdims=True)
                ds = p * (dp - delta)
                dbias_ref[hk] += ds
                dsacc[...] += jnp.where(lane == hk, -(psink * delta), 0.0)
                dsr = (ds * (HEAD_DIM ** -0.5)).astype(BF16)
                dqna, dqnb = _unstack_heads(_dot(dsr, kdup), lo)
                dkd = _dot_tn(dsr, qm4)
                dvd = _dot_tn(pb, dom4)
                dqa, dqga = _half_rms_bwd(dqna, qhata, qra, qgain, lo)
                dqb, dqgb = _half_rms_bwd(dqnb, qhatb, qrb, qgain, lo)
                gacc[0:1, :] += dqga + dqgb
                dqkv_ref[:, ca] = dqa.astype(BF16)
                dqkv_ref[:, cb] = dqb.astype(BF16)
                dk_folds.append(dkd + pltpu.roll(dkd, HEAD_DIM, 1))
                dv_folds.append(dvd + pltpu.roll(dvd, HEAD_DIM, 1))
            dkn = jnp.where(lo2, dk_folds[0], dk_folds[1])
            dk, dkg = _half_rms_bwd(dkn, khat, kr, kgain, lo2)
            gacc[1:2, :] += dkg
            band[:, j * LANES:(j + 1) * LANES] = dk
            band[:, KVW + j * LANES:KVW + (j + 1) * LANES] = jnp.where(lo2, dv_folds[0], dv_folds[1])
        dqkv_ref[:, QW:] = (band[CHUNK:, :] + carry[...]).astype(BF16)
        carry[...] = band[0:CHUNK, :]

        @pl.when(i == nb - 1)
        def _():
            g = gacc[...]
            g = g + pltpu.roll(g, HEAD_DIM, 1)
            dqg_ref[...] = g[0:1, :]
            dkg_ref[...] = g[1:2, :]
            for r in range(KV_GROUP):
                dsink_ref[r:r + 1, :] = jnp.sum(dsacc[r * CHUNK:(r + 1) * CHUNK, :], axis=0, keepdims=True)

    vec = pl.BlockSpec((1, LANES), lambda i: (0, 0))
    bshape = probs.shape[1:]
    return _call(
        body, name="attn_bwd", grid=(nb,),
        in_specs=[pl.BlockSpec((CHUNK, QW + 2 * KVW), lambda i: (nb - 1 - i, 0)),
                  pl.BlockSpec((CHUNK, 2 * KVW), lambda i: (jnp.maximum(nb - 2 - i, 0), QW // (2 * KVW))),
                  pl.BlockSpec((CHUNK, QW), lambda i: (nb - 1 - i, 0)),
                  pl.BlockSpec((1,) + bshape, lambda i: (nb - 1 - i, 0, 0, 0)),
                  pl.BlockSpec((CHUNK, LANES), lambda i: (nb - 1 - i, 0)),
                  _resident((1, LANES)), _resident((1, LANES))],
        out_specs=[pl.BlockSpec((CHUNK, QW + 2 * KVW), lambda i: (nb - 1 - i, 0)),
                   pl.BlockSpec(bshape, lambda i: (0, 0, 0)), vec, vec,
                   pl.BlockSpec((KV_GROUP, LANES), lambda i: (0, 0))],
        out_shape=[jax.ShapeDtypeStruct((S, QW + 2 * KVW), BF16),
                   jax.ShapeDtypeStruct(bshape, F32),
                   jax.ShapeDtypeStruct((1, LANES), F32), jax.ShapeDtypeStruct((1, LANES), F32),
                   jax.ShapeDtypeStruct((KV_GROUP, LANES), F32)],
        scratch_shapes=[pltpu.VMEM((CHUNK, 2 * KVW), F32), pltpu.VMEM((2 * CHUNK, 2 * KVW), F32),
                        pltpu.VMEM((KV_GROUP * CHUNK, LANES), F32), pltpu.VMEM((8, LANES), F32)],
        sem=("arbitrary",), args=(qkv, qkv, do, probs, psinks, qg2, kg2), comm=comm)


def _row_tile(rows, cols, n_arrays):
    budget = VMEM_LIMIT_V7X // 4 // (2 * n_arrays * 4 * cols)
    best = 8
    for n in range(1, rows // 8 + 1):
        if rows % n == 0 and (rows // n) % 8 == 0 and rows // n <= budget:
            best = rows // n
            break
    return best


def _adamw(g, w, m, v, name, comm=None):
    R, C = g.shape
    tr = _row_tile(R, C, 8)

    def body(g_ref, w_ref, m_ref, v_ref, d_ref, mo_ref, vo_ref, go_ref):
        gg = g_ref[...]
        go_ref[...] = gg
        mn = ADAM_B1 * m_ref[...] + (1.0 - ADAM_B1) * gg
        vn = ADAM_B2 * v_ref[...] + (1.0 - ADAM_B2) * jnp.square(gg)
        m_hat = mn / (1.0 - ADAM_B1 ** ADAM_STEP)
        v_hat = vn / (1.0 - ADAM_B2 ** ADAM_STEP)
        d_ref[...] = -ADAM_LR * (m_hat / (jnp.sqrt(v_hat) + ADAM_EPS) + ADAM_WD * w_ref[...])
        mo_ref[...] = mn
        vo_ref[...] = vn

    spec = pl.BlockSpec((tr, C), lambda i: (i, 0))
    return _call(
        body, name=name, grid=(R // tr,), in_specs=[spec] * 4, out_specs=[spec] * 4,
        out_shape=[jax.ShapeDtypeStruct((R, C), F32)] * 4, sem=("parallel",), args=(g, w, m, v), comm=comm)


def _place_shard(shards, layer, where, dtype, name):
    _, R, C = shards.shape
    tr = _row_tile(R, C, 2) if R % 8 == 0 else R

    def body(s_ref, x_ref, o_ref):
        o_ref[...] = x_ref[...].astype(dtype)

    return pl.pallas_call(
        body, name=name,
        grid_spec=pltpu.PrefetchScalarGridSpec(
            num_scalar_prefetch=1, grid=(R // tr,),
            in_specs=[pl.BlockSpec((1, tr, C), lambda i, s_ref: (layer, i, 0))],
            out_specs=pl.BlockSpec((1, tr, C), lambda i, s_ref: (s_ref[1], i, 0))),
        out_shape=jax.ShapeDtypeStruct((4, R, C), dtype),
        compiler_params=_cp(("parallel",)),
    )(where, shards)


def _pair_add(g4, rsib, where, name):
    J, R, C = g4.shape
    Rh = R // 2
    tr = _row_tile(Rh, C, 4)
    g5 = g4.reshape(J, 2, Rh, C)

    def body(s_ref, g_ref, r_ref, p_ref, q_ref):
        val = (g_ref[...].astype(F32)[0] + r_ref[...].astype(F32)).astype(BF16)
        p_ref[...] = val

        @pl.when(pl.program_id(1) == s_ref[1])
        def _():
            q_ref[...] = val

    return pl.pallas_call(
        body, name=name,
        grid_spec=pltpu.PrefetchScalarGridSpec(
            num_scalar_prefetch=1, grid=(Rh // tr, J),
            in_specs=[pl.BlockSpec((1, 1, tr, C), lambda i, j, s_ref: (j, s_ref[0], i, 0)),
                      pl.BlockSpec((1, tr, C), lambda i, j, s_ref: (j, i, 0))],
            out_specs=[pl.BlockSpec((1, tr, C), lambda i, j, s_ref: (j, i, 0)),
                       pl.BlockSpec((1, tr, C), lambda i, j, s_ref: (s_ref[1], i, 0))]),
        out_shape=[jax.ShapeDtypeStruct((J, Rh, C), BF16)] * 2,
        compiler_params=_cp(("parallel", "arbitrary")),
    )(where, g5, rsib)


def _sum_chips(q, where, dest, layer, out_shape, name):
    J, Rh, C = q.shape
    tr = _row_tile(Rh, C, 3)
    nb = Rh // tr

    def body(s_ref, q_ref, *rest):
        qq = q_ref[...].astype(F32)
        rest[-1][0] = ((qq[0] + qq[1]) + qq[2]) + qq[3]

    have = dest is not None
    return pl.pallas_call(
        body, name=name,
        grid_spec=pltpu.PrefetchScalarGridSpec(
            num_scalar_prefetch=1, grid=(nb,),
            in_specs=[pl.BlockSpec((J, tr, C), lambda i, s_ref: (0, i, 0))] + ([ANY] if have else []),
            out_specs=pl.BlockSpec((1, tr, C), lambda i, s_ref: (layer, s_ref[0] * nb + i, 0))),
        out_shape=jax.ShapeDtypeStruct(out_shape, F32),
        input_output_aliases={2: 0} if have else {},
        compiler_params=_cp(("parallel",)),
    )(*((where, q, dest) if have else (where, q)))


MESH = pl.DeviceIdType.MESH
ANY = pl.BlockSpec(memory_space=pl.ANY)


def _place():
    x, y, c = lax.axis_index("x"), lax.axis_index("y"), lax.axis_index("c")
    others = [(1 - x, y), (x, 1 - y), (1 - x, 1 - y)]
    return x, y, c, 2 * x + y, others, [2 * ox + oy for ox, oy in others]


def _gather_comm(items):
    n = len(items)
    placed = [it[0] for it in items]
    split = [it[3] for it in items]

    def rows(t, ref, half):
        _, lo, hi, _ = items[t]
        if not split[t]:
            return ref if (lo, hi) == (0, placed[t].shape[1]) else ref.at[pl.ds(lo, hi - lo), :]
        rh = (hi - lo) // 2
        return ref.at[pl.ds(lo + half * rh, rh), :]

    def sends(outs, sems):
        send, recv = sems[0], sems[1]
        x, y, c, me, others, okey = _place()
        cps = []
        for t in range(n):
            mine = rows(t, outs[t].at[me], c)
            for j, (ox, oy) in enumerate(others):
                cps.append(pltpu.make_async_remote_copy(
                    src_ref=mine, dst_ref=mine,
                    send_sem=send.at[t, j], recv_sem=recv.at[t, j], device_id=(ox, oy, c), device_id_type=MESH))
        return cps

    def start(srcs, outs, news, sems):
        for cp in sends(outs, sems):
            cp.start()

    def forwards(outs, sems):
        fsend, frecv = sems[2], sems[3]
        x, y, c, me, others, okey = _place()
        cps = []
        for t in range(n):
            if split[t]:
                for j in range(3):
                    landed = rows(t, outs[t].at[okey[j]], c)
                    cps.append(pltpu.make_async_remote_copy(
                        src_ref=landed, dst_ref=landed, send_sem=fsend.at[t, j], recv_sem=frecv.at[t, j],
                        device_id=(x, y, 1 - c), device_id_type=MESH))
        return cps

    def middle(srcs, outs, news, sems):
        send, recv = sems[0], sems[1]
        x, y, c, me, others, okey = _place()
        for t in range(n):
            for j in range(3):
                landed = rows(t, outs[t].at[okey[j]], c)
                pltpu.make_async_remote_copy(
                    src_ref=landed, dst_ref=landed, send_sem=send.at[t, j], recv_sem=recv.at[t, j],
                    device_id=(x, y, 1 - c), device_id_type=MESH).wait_recv()
        for cp in forwards(outs, sems):
            cp.start()

    def finish(srcs, outs, news, sems):
        fsend, frecv = sems[2], sems[3]
        x, y, c, me, others, okey = _place()
        for t in range(n):
            if split[t]:
                for j in range(3):
                    theirs = rows(t, outs[t].at[okey[j]], 1 - c)
                    pltpu.make_async_remote_copy(
                        src_ref=theirs, dst_ref=theirs, send_sem=fsend.at[t, j], recv_sem=frecv.at[t, j],
                        device_id=(x, y, 1 - c), device_id_type=MESH).wait_recv()
        for cp in sends(outs, sems) + forwards(outs, sems):
            cp.wait_send()

    return _Comm([], placed, [], [pltpu.SemaphoreType.DMA((n, 3))] * 4, start, finish, middle)


def _pair_exchange_comm(gs):
    n = len(gs)

    def copies(ins, outs, sems):
        send, recv = sems
        x, y, c, _, _, _ = _place()
        cps = []
        for t in range(n):
            rh = gs[t].shape[1] // 2
            cps.append(pltpu.make_async_remote_copy(
                src_ref=ins[t].at[:, pl.ds((1 - c) * rh, rh), :], dst_ref=outs[t],
                send_sem=send.at[t], recv_sem=recv.at[t], device_id=(x, y, 1 - c), device_id_type=MESH))
        return cps

    def start(ins, zones, outs, sems):
        for cp in copies(ins, outs, sems):
            cp.start()

    def finish(ins, zones, outs, sems):
        for cp in copies(ins, outs, sems):
            cp.wait()

    news = [jax.ShapeDtypeStruct((4, g.shape[1] // 2, g.shape[2]), g.dtype) for g in gs]
    return _Comm(gs, [], news, [pltpu.SemaphoreType.DMA((n,))] * 2, start, finish)


def _chip_scatter_comm(ps, qs):
    n = len(ps)

    def sends(ins, outs, sems):
        send, recv = sems
        x, y, c, me, others, okey = _place()
        return [pltpu.make_async_remote_copy(
            src_ref=ins[t].at[okey[j]], dst_ref=outs[t].at[me],
            send_sem=send.at[t, j], recv_sem=recv.at[t, j], device_id=(ox, oy, c), device_id_type=MESH)
            for t in range(n) for j, (ox, oy) in enumerate(others)]

    def start(ins, outs, news, sems):
        for cp in sends(ins, outs, sems):
            cp.start()

    def finish(ins, outs, news, sems):
        send, recv = sems
        x, y, c, me, others, okey = _place()
        for t in range(n):
            for j in range(3):
                slot = outs[t].at[okey[j]]
                pltpu.make_async_remote_copy(
                    src_ref=slot, dst_ref=slot, send_sem=send.at[t, j], recv_sem=recv.at[t, j],
                    device_id=(x, y, c), device_id_type=MESH).wait_recv()
        for cp in sends(ins, outs, sems):
            cp.wait_send()

    return _Comm(ps, qs, [], [pltpu.SemaphoreType.DMA((n, 3))] * 2, start, finish)


def _half_exchange_comm(arrs, layers=None):
    n = len(arrs)
    items = [(t, layer) for t in range(n) for layer in (range(arrs[t].shape[0]) if layers is None else layers[t])]

    def sends(outs, sems):
        send, recv = sems
        x, y, c, _, _, _ = _place()
        cps = []
        for k, (t, layer) in enumerate(items):
            rh = arrs[t].shape[1] // 2
            mine = outs[t].at[layer, pl.ds(c * rh, rh), :]
            cps.append(pltpu.make_async_remote_copy(
                src_ref=mine, dst_ref=mine, send_sem=send.at[k], recv_sem=recv.at[k],
                device_id=(x, y, 1 - c), device_id_type=MESH))
        return cps

    def start(srcs, outs, news, sems):
        for cp in sends(outs, sems):
            cp.start()

    def finish(srcs, outs, news, sems):
        send, recv = sems
        x, y, c, _, _, _ = _place()
        for k, (t, layer) in enumerate(items):
            rh = arrs[t].shape[1] // 2
            theirs = outs[t].at[layer, pl.ds((1 - c) * rh, rh), :]
            pltpu.make_async_remote_copy(
                src_ref=theirs, dst_ref=theirs, send_sem=send.at[k], recv_sem=recv.at[k],
                device_id=(x, y, 1 - c), device_id_type=MESH).wait_recv()
        for cp in sends(outs, sems):
            cp.wait_send()

    return _Comm([], arrs, [], [pltpu.SemaphoreType.DMA((len(items),))] * 2, start, finish)


SMALL_COLS = 1024
SMALL_PIECE_ROWS = 48


def _allreduce_small(buf, comm=None):
    pr = SMALL_PIECE_ROWS
    flips = [(d >> 2 & 1, d >> 1 & 1, d & 1) for d in range(1, 8)]

    def body(x_ref, o_ref, rbuf, send1, recv1, send2, recv2):
        x, y, c = lax.axis_index("x"), lax.axis_index("y"), lax.axis_index("c")
        me = 4 * x + 2 * y + c
        peers = [(x ^ fx, y ^ fy, c ^ fc) for fx, fy, fc in flips]
        pid = [4 * px + 2 * py + pc for px, py, pc in peers]

        def piece(ref, p):
            return ref.at[pl.ds(pl.multiple_of(p * pr, 8), pr), :]

        cps = []
        for d in range(7):
            cp = pltpu.make_async_remote_copy(
                src_ref=piece(x_ref, pid[d]), dst_ref=rbuf.at[d + 1],
                send_sem=send1.at[d], recv_sem=recv1.at[d], device_id=peers[d], device_id_type=MESH)
            cp.start()
            cps.append(cp)
        acc = piece(x_ref, me)[...]
        for d in range(7):
            cps[d].wait_recv()
            acc = acc + rbuf[d + 1]
        piece(o_ref, me)[...] = acc
        out = []
        for d in range(7):
            cp = pltpu.make_async_remote_copy(
                src_ref=piece(o_ref, me), dst_ref=piece(o_ref, me),
                send_sem=send2.at[d], recv_sem=recv2.at[d], device_id=peers[d], device_id_type=MESH)
            cp.start()
            out.append(cp)
        for d in range(7):
            pltpu.make_async_remote_copy(
                src_ref=piece(o_ref, pid[d]), dst_ref=piece(o_ref, pid[d]),
                send_sem=send2.at[d], recv_sem=recv2.at[d], device_id=peers[d], device_id_type=MESH).wait_recv()
        for cp in cps + out:
            cp.wait_send()

    vm = pl.BlockSpec(memory_space=pltpu.VMEM)
    return _call(
        body, name="small_allreduce", grid=(), in_specs=[vm], out_specs=vm,
        out_shape=jax.ShapeDtypeStruct(buf.shape, F32),
        scratch_shapes=[pltpu.VMEM((8, pr, SMALL_COLS), F32)] + [pltpu.SemaphoreType.DMA((7,))] * 4,
        args=(buf,), comm=comm)


def _rows_of(shape):
    return -(-math.prod(shape) // (8 * SMALL_COLS)) * 8


def _pack(arrays, rows):
    parts = []
    for a in arrays:
        r = _rows_of(a.shape)
        parts.append(jnp.pad(a.reshape(-1), (0, r * SMALL_COLS - a.size)).reshape(r, SMALL_COLS))
    used = sum(p.shape[0] for p in parts)
    if rows > used:
        parts.append(jnp.zeros((rows - used, SMALL_COLS), F32))
    return jnp.concatenate(parts, axis=0)


def _unpack(buf, shapes):
    out, off = [], 0
    for s in shapes:
        r = _rows_of(s)
        out.append(buf[off:off + r].reshape(-1)[:math.prod(s)].reshape(s))
        off += r
    return out


BIG = ["sgu_w_in", "sgu_w_out", "attn_w_qkv", "attn_w_o", "ffn_w_up", "ffn_w_down"]
SMALL = ["mix_norm", "ffn_norm", "sgu_v_gain", "sgu_w_s", "sgu_b_s", "attn_q_gain", "attn_k_gain", "attn_sinks",
         "rel_bias", "ffn_conv_b"]
ORDER = ["mix_norm", "ffn_norm", "sgu_w_in", "sgu_v_gain", "sgu_w_s", "sgu_b_s", "sgu_w_out", "attn_w_qkv",
         "attn_q_gain", "attn_k_gain", "attn_sinks", "attn_w_o", "rel_bias", "ffn_w_up", "ffn_conv_w", "ffn_conv_b",
         "ffn_w_down"]


def kernel(x, mix_norm, ffn_norm, sgu_w_in, sgu_v_gain, sgu_w_s, sgu_b_s, sgu_w_out, attn_w_qkv, attn_q_gain, attn_k_gain, attn_sinks, attn_w_o, rel_bias, ffn_w_up, ffn_conv_w, ffn_conv_b, ffn_w_down, loss_target, m_mix_norm, m_ffn_norm, m_sgu_w_in, m_sgu_v_gain, m_sgu_w_s, m_sgu_b_s, m_sgu_w_out, m_attn_w_qkv, m_attn_q_gain, m_attn_k_gain, m_attn_sinks, m_attn_w_o, m_rel_bias, m_ffn_w_up, m_ffn_conv_w, m_ffn_conv_b, m_ffn_w_down, v_mix_norm, v_ffn_norm, v_sgu_w_in, v_sgu_v_gain, v_sgu_w_s, v_sgu_b_s, v_sgu_w_out, v_attn_w_qkv, v_attn_q_gain, v_attn_k_gain, v_attn_sinks, v_attn_w_o, v_rel_bias, v_ffn_w_up, v_ffn_conv_w, v_ffn_conv_b, v_ffn_w_down):
    w = dict(mix_norm=mix_norm, ffn_norm=ffn_norm, sgu_w_in=sgu_w_in, sgu_v_gain=sgu_v_gain, sgu_w_s=sgu_w_s,
             sgu_b_s=sgu_b_s, sgu_w_out=sgu_w_out, attn_w_qkv=attn_w_qkv, attn_q_gain=attn_q_gain,
             attn_k_gain=attn_k_gain, attn_sinks=attn_sinks, attn_w_o=attn_w_o, rel_bias=rel_bias, ffn_w_up=ffn_w_up,
             ffn_conv_w=ffn_conv_w, ffn_conv_b=ffn_conv_b, ffn_w_down=ffn_w_down)
    mom = dict(mix_norm=m_mix_norm, ffn_norm=m_ffn_norm, sgu_w_in=m_sgu_w_in, sgu_v_gain=m_sgu_v_gain,
               sgu_w_s=m_sgu_w_s, sgu_b_s=m_sgu_b_s, sgu_w_out=m_sgu_w_out, attn_w_qkv=m_attn_w_qkv,
               attn_q_gain=m_attn_q_gain, attn_k_gain=m_attn_k_gain, attn_sinks=m_attn_sinks, attn_w_o=m_attn_w_o,
               rel_bias=m_rel_bias, ffn_w_up=m_ffn_w_up, ffn_conv_w=m_ffn_conv_w, ffn_conv_b=m_ffn_conv_b,
               ffn_w_down=m_ffn_w_down)
    var = dict(mix_norm=v_mix_norm, ffn_norm=v_ffn_norm, sgu_w_in=v_sgu_w_in, sgu_v_gain=v_sgu_v_gain,
               sgu_w_s=v_sgu_w_s, sgu_b_s=v_sgu_b_s, sgu_w_out=v_sgu_w_out, attn_w_qkv=v_attn_w_qkv,
               attn_q_gain=v_attn_q_gain, attn_k_gain=v_attn_k_gain, attn_sinks=v_attn_sinks, attn_w_o=v_attn_w_o,
               rel_bias=v_rel_bias, ffn_w_up=v_ffn_w_up, ffn_conv_w=v_ffn_conv_w, ffn_conv_b=v_ffn_conv_b,
               ffn_w_down=v_ffn_w_down)
    chip = 2 * lax.axis_index("x") + lax.axis_index("y")
    core = lax.axis_index("c")

    where = jnp.stack([core, chip]).astype(jnp.int32)
    names = ["sgu_w_in", "sgu_w_out", "attn_w_qkv", "attn_w_o", "ffn_w_up0", "ffn_w_up1", "ffn_w_down0", "ffn_w_down1"]
    shards = [(sgu_w_in, 0), (sgu_w_out, 0), (attn_w_qkv, 0), (attn_w_o, 0), (ffn_w_up, 0), (ffn_w_up, 1),
              (ffn_w_down, 0), (ffn_w_down, 1)]
    T = {nm: _place_shard(s, l, where, BF16, "place_" + nm) for (s, l), nm in zip(shards, names)}
    for l in range(2):
        T["conv_w%d" % l] = _place_shard(ffn_conv_w, l, where, F32, "place_conv_w%d" % l)

    def gather(keys):
        items = []
        for k in keys:
            k, lo, hi = (k, 0, None) if isinstance(k, str) else k
            items.append((T[k], lo, T[k].shape[1] if hi is None else hi, not k.startswith("conv")))
        return _gather_comm(items)

    def gathered(keys, res):
        for k, a in zip(keys, res[0]):
            T[k if isinstance(k, str) else k[0]] = a

    D = x.shape[2]
    first = ["sgu_w_in", "conv_w0", "conv_w1"]
    gathered(first, _run_comm(gather(first), "gather_first"))
    unshard_cols = lambda a: jnp.transpose(a, (1, 0, 2)).reshape(a.shape[1], -1)
    cw = [unshard_cols(T["conv_w0"]), unshard_cols(T["conv_w1"])]
    cb = ffn_conv_b
    flat = lambda k: T[k].reshape(-1, D)
    x2, target = x[0], loss_target[0]
    bucket = jnp.asarray(_rel_buckets_flat())
    wtril, wtrilT = _sgu_prep(sgu_w_s[0])
    bT = sgu_b_s[0].T
    bias = _rel_bias_expand(rel_bias, bucket).reshape(N_KV_HEADS, KV_GROUP * CHUNK, 2 * CHUNK)
    qg2 = jnp.tile(attn_q_gain, (1, 2))
    kg2 = jnp.tile(attn_k_gain, (1, 2))
    sinks = attn_sinks.reshape(N_HEADS)
    mix0, mix1 = mix_norm[0:1], mix_norm[1:2]
    fn0, fn1 = ffn_norm[0:1], ffn_norm[1:2]

    ks = ["sgu_w_out", ("ffn_w_up0", 0, 256)]
    (hn0, z), r = _norm_matmul(x2, mix0, T["sgu_w_in"], "sgu_in", comm=gather(ks))
    gathered(ks, r)
    ks = [("ffn_w_up0", 256, None)]
    (yp, h1, cdf), r = _sgu_fwd(z, x2, sgu_v_gain, wtril, bT, flat("sgu_w_out"), comm=gather(ks))
    gathered(ks, r)
    ks = ["ffn_w_down0", "attn_w_qkv"]
    (hf0, a0), r = _norm_matmul(h1, fn0, T["ffn_w_up0"], "ffn0_up", comm=gather(ks))
    gathered(ks, r)
    ks = ["attn_w_o", ("ffn_w_up1", 0, 640)]
    (f0, h2), r = _ffn_fwd(a0, h1, cw[0], cb[0:1], flat("ffn_w_down0"), "ffn0_fwd", comm=gather(ks))
    gathered(ks, r)
    (hn1, qkv), _ = _norm_matmul(h2, mix1, T["attn_w_qkv"], "attn_qkv")
    ks = [("ffn_w_up1", 640, None), "ffn_w_down1"]
    (o, probs, psinks), r = _attn_fwd(qkv, qg2, kg2, sinks, bias, comm=gather(ks))
    gathered(ks, r)
    h3, _ = _matmul_res(o, flat("attn_w_o"), h2, "attn_out")
    (hf1, a1), _ = _norm_matmul(h3, fn1, T["ffn_w_up1"], "ffn1_up")
    (f1, dh4, loss_local), _ = _ffn_fwd(a1, h3, cw[1], cb[1:2], flat("ffn_w_down1"), "ffn1_fwd", target=target)

    G, RS, PQ, QD, halves, grads = {}, {}, {}, {}, {}, {}
    dest_of = {"sgu_w_in": ("sgu_w_in", 0), "sgu_w_out": ("sgu_w_out", 0), "attn_w_qkv": ("attn_w_qkv", 0),
               "attn_w_o": ("attn_w_o", 0), "ffn_w_up0": ("ffn_w_up", 0), "ffn_w_up1": ("ffn_w_up", 1),
               "ffn_w_down0": ("ffn_w_down", 0), "ffn_w_down1": ("ffn_w_down", 1)}

    def px(keys):
        return _pair_exchange_comm([G[k] for k in keys])

    def px_done(keys, res):
        for k, a in zip(keys, res[1]):
            RS[k] = a
            PQ[k] = _pair_add(G[k], a, where, "pair_add_" + k)

    def sc(keys):
        return _chip_scatter_comm([PQ[k][0] for k in keys], [PQ[k][1] for k in keys])

    def sc_done(keys, res):
        for k, a in zip(keys, res[0]):
            wk, layer = dest_of[k]
            halves[wk] = _sum_chips(a, where, halves.get(wk), layer, w[wk].shape, "sum_chips_" + k)

    nup = ffn_w_up.shape[2]
    ndown = ffn_w_down.shape[1]
    rows4 = lambda a: a.reshape(4, ndown, D)
    (dc1, dcw1, dcb1), _ = _ffn_bwd_dc(dh4, a1, cw[1], cb[1:2], flat("ffn_w_down1"), "ffn1_bwd_dc")
    gw, _ = _matmul_tn(f1, dh4, "ffn1_dw_down", ka=2 * ndown, nb=D, out_dtype=BF16)
    G["ffn_w_down1"] = rows4(gw)
    (dh3, dfn1, da1), _ = _bwd_norm(dc1, T["ffn_w_up1"], h3, fn1, dh4, "ffn1_bwd_in", conv_w=cw[1])
    G["ffn_w_up1"], _ = _matmul_tn(hf1, da1, "ffn1_dw_up", ka=D, nb=nup, out_dtype=BF16)
    ks = ["ffn_w_down1", "ffn_w_up1"]
    G["attn_w_o"], r = _matmul_tn(o, dh3, "attn_dw_o", ka=QW // 4, nb=D, out_dtype=BF16, comm=px(ks))
    px_done(ks, r)
    do, _ = _matmul_nt(dh3, flat("attn_w_o"), "attn_bwd_out")
    (dqkv, dbias, dqg, dkg, dsink), r = _attn_bwd(qkv, do, probs, psinks, qg2, kg2, comm=sc(ks))
    sc_done(ks, r)
    G["attn_w_qkv"], _ = _matmul_tn(hn1, dqkv, "attn_dw_qkv", ka=D, nb=dqkv.shape[1] // 4, out_dtype=BF16)
    ks = ["attn_w_o", "attn_w_qkv"]
    (dh2, dmix1), r = _bwd_norm(dqkv, T["attn_w_qkv"], h2, mix1, dh3, "attn_bwd_in", comm=px(ks))
    px_done(ks, r)
    (dc0, dcw0, dcb0), r = _ffn_bwd_dc(dh2, a0, cw[0], cb[0:1], flat("ffn_w_down0"), "ffn0_bwd_dc", comm=sc(ks))
    sc_done(ks, r)
    gw, _ = _matmul_tn(f0, dh2, "ffn0_dw_down", ka=2 * ndown, nb=D, out_dtype=BF16)
    G["ffn_w_down0"] = rows4(gw)
    (dh1, dfn0, da0), r = _bwd_norm(dc0, T["ffn_w_up0"], h1, fn0, dh2, "ffn0_bwd_in", conv_w=cw[0],
                                    comm=px(["ffn_w_down0"]))
    px_done(["ffn_w_down0"], r)
    G["ffn_w_up0"], r = _matmul_tn(hf0, da0, "ffn0_dw_up", ka=D, nb=nup, out_dtype=BF16, comm=sc(["ffn_w_down0"]))
    sc_done(["ffn_w_down0"], r)
    ks = ["ffn_w_up0"]
    G["sgu_w_out"], r = _matmul_tn(yp, dh1, "sgu_dw_out", ka=yp.shape[1] // 4, nb=D, out_dtype=BF16, comm=px(ks))
    px_done(ks, r)
    both = [sc(ks), px(["sgu_w_out"])]
    (dz, dws, dbT, dvg), r = _sgu_bwd(dh1, z, cdf, sgu_v_gain, wtril, wtrilT, bT, flat("sgu_w_out"),
                                      comm=_join(both))
    r = _split(both, r)
    sc_done(ks, r[0])
    px_done(["sgu_w_out"], r[1])
    done = ["attn_w_qkv", "attn_w_o", "ffn_w_up", "ffn_w_down"]
    both = [sc(["sgu_w_out"]), _half_exchange_comm([halves[k] for k in done])]
    G["sgu_w_in"], r = _matmul_tn(hn0, dz, "sgu_dw_in", ka=D, nb=dz.shape[1] // 4, out_dtype=BF16, comm=_join(both))
    r = _split(both, r)
    sc_done(["sgu_w_out"], r[0])
    for k, a in zip(done, r[1][0]):
        grads[k] = a
    px_done(["sgu_w_in"], _run_comm(px(["sgu_w_in"]), "grad_last_pair_exchange"))
    (grad_x, dmix0), r = _bwd_norm(dz, T["sgu_w_in"], x2, mix0, dh1, "sgu_bwd_in", comm=sc(["sgu_w_in"]))
    sc_done(["sgu_w_in"], r)

    g = dict(mix_norm=jnp.concatenate([dmix0, dmix1], axis=0), ffn_norm=jnp.concatenate([dfn0, dfn1], axis=0),
             sgu_v_gain=dvg, sgu_w_s=dws, sgu_b_s=dbT.T, attn_q_gain=dqg[:, :HEAD_DIM], attn_k_gain=dkg[:, :HEAD_DIM],
             attn_sinks=dsink[:, :N_KV_HEADS].T.reshape(1, N_HEADS),
             rel_bias=_rel_bias_reduce(dbias.reshape(N_HEADS, CHUNK * 2 * CHUNK), bucket),
             ffn_conv_b=jnp.concatenate([dcb0, dcb1], axis=0))
    small_list = [g[k].reshape(w[k].shape) for k in SMALL] + [jnp.stack([dcw0, dcw1]), loss_local]
    small_shapes = [a.shape for a in small_list]
    done = ["sgu_w_in", "sgu_w_out"]
    red, r = _allreduce_small(_pack(small_list, 8 * SMALL_PIECE_ROWS),
                              comm=_half_exchange_comm([halves[k] for k in done]))
    for k, a in zip(done, r[0]):
        grads[k] = a
    red = _unpack(red, small_shapes)
    for k, a in zip(SMALL, red):
        grads[k] = a
    grads["ffn_conv_w"] = lax.dynamic_slice_in_dim(red[-2], chip * ffn_conv_w.shape[2], ffn_conv_w.shape[2], axis=2)
    loss = red[-1][0, 0]

    delta, new_m, new_v = {}, {}, {}
    two = lambda a: a.reshape(-1, a.shape[-1])
    for k in BIG:
        (d2, m2, v2, g2), _ = _adamw(two(grads[k]), two(w[k]), two(mom[k]), two(var[k]), "adamw_" + k)
        delta[k], new_m[k], new_v[k], grads[k] = (a.reshape(w[k].shape) for a in (d2, m2, v2, g2))
    sm = SMALL + ["ffn_conv_w"]
    sm_shapes = [w[k].shape for k in sm]
    rows = sum(_rows_of(s) for s in sm_shapes)
    (d2, m2, v2, _), _ = _adamw(_pack([grads[k] for k in sm], rows), _pack([w[k] for k in sm], rows),
                                _pack([mom[k] for k in sm], rows), _pack([var[k] for k in sm], rows), "adamw_small")
    for dst, buf in ((delta, d2), (new_m, m2), (new_v, v2)):
        for k, a in zip(sm, _unpack(buf, sm_shapes)):
            dst[k] = a

    return (loss, grad_x[None], *[grads[k] for k in ORDER], *[delta[k] for k in ORDER],
            *[new_m[k] for k in ORDER], *[new_v[k] for k in ORDER])
```

```python
import functools
import math

import numpy as np
import jax
import jax.numpy as jnp
from jax import lax
from jax.experimental import pallas as pl
from jax.experimental.pallas import tpu as pltpu

F32 = jnp.float32
BF16 = jnp.bfloat16

EPS = 1e-6
CHUNK = 128
SGU_GROUPS = 16
HEAD_DIM = 64
N_HEADS = 16
N_KV_HEADS = 4
KV_GROUP = N_HEADS // N_KV_HEADS
REL_BUCKETS = 32
REL_MAX_DIST = 128
LANES = 128
HALO = 16

ADAM_LR = 0.001
ADAM_B1 = 0.9
ADAM_B2 = 0.999
ADAM_EPS = 1e-08
ADAM_WD = 0.01
ADAM_STEP = 10

VMEM_LIMIT_V7X = 56 * 1024 * 1024

_SQRT_HALF = math.sqrt(0.5)
_INV_SQRT_2PI = 1.0 / math.sqrt(2.0 * math.pi)


def _cp(sem):
    return pltpu.CompilerParams(dimension_semantics=sem, vmem_limit_bytes=VMEM_LIMIT_V7X)


def _resident(shape):
    nd = len(shape)
    return pl.BlockSpec(shape, lambda *_: (0,) * nd, pipeline_mode=pl.Buffered(1))


class _Comm:
    def __init__(self, srcs, zones, news, sems, start, finish, middle=None):
        self.srcs, self.zones, self.news, self.sems = list(srcs), list(zones), list(news), list(sems)
        self.start, self.finish = start, finish
        self.middle = middle if middle is not None else (lambda srcs, zones, news, sems: None)


def _join(comms):
    comms = [c for c in comms if c is not None]
    if not comms:
        return None

    def part(seq, attr):
        out, k = [], 0
        for c in comms:
            n = len(getattr(c, attr))
            out.append(seq[k:k + n])
            k += n
        return out

    def run(which):
        def f(srcs, zones, news, sems):
            for c, a, b, d, e in zip(comms, part(srcs, "srcs"), part(zones, "zones"), part(news, "news"), part(sems, "sems")):
                getattr(c, which)(a, b, d, e)
        return f

    cat = lambda attr: [v for c in comms for v in getattr(c, attr)]
    return _Comm(cat("srcs"), cat("zones"), cat("news"), cat("sems"), run("start"), run("finish"), run("middle"))


def _split(comms, res):
    zones, news = res
    out, kz, kn = [], 0, 0
    for c in comms:
        out.append((zones[kz:kz + len(c.zones)], news[kn:kn + len(c.news)]))
        kz += len(c.zones)
        kn += len(c.news)
    return out


def _call(body, *, name, grid, in_specs, out_specs, out_shape, args, scratch_shapes=(), sem=None, comm=None):
    if comm is None:
        res = pl.pallas_call(body, name=name, grid=grid, in_specs=in_specs, out_specs=out_specs, out_shape=out_shape,
                             scratch_shapes=list(scratch_shapes), compiler_params=_cp(sem))(*args)
        return res, None
    single = not isinstance(out_shape, (list, tuple))
    out_specs_l = [out_specs] if single else list(out_specs)
    out_shape_l = [out_shape] if single else list(out_shape)
    n_in, n_out, n_scr = len(in_specs), len(out_shape_l), len(scratch_shapes)
    ns, nz, nn = len(comm.srcs), len(comm.zones), len(comm.news)

    def wrapped(*refs):
        k = n_in
        ins, srcs = refs[:k], refs[k:k + ns]
        k += ns + nz
        outs, zones, news = refs[k:k + n_out], refs[k + n_out:k + n_out + nz], refs[k + n_out + nz:k + n_out + nz + nn]
        k += n_out + nz + nn
        scr, sems = refs[k:k + n_scr], refs[k + n_scr:]
        if not grid:
            comm.start(srcs, zones, news, sems)
            body(*ins, *outs, *scr)
            comm.middle(srcs, zones, news, sems)
            comm.finish(srcs, zones, news, sems)
            return
        first = functools.reduce(jnp.logical_and, [pl.program_id(a) == 0 for a in range(len(grid))])
        last = functools.reduce(jnp.logical_and, [pl.program_id(a) == grid[a] - 1 for a in range(len(grid))])
        early = len(grid) == 1 and grid[0] >= 2
        mid_step = grid[0] - (2 if grid[0] >= 8 else 1)

        @pl.when(first)
        def _():
            comm.start(srcs, zones, news, sems)

        if early:
            @pl.when(pl.program_id(0) == mid_step)
            def _():
                comm.middle(srcs, zones, news, sems)

        body(*ins, *outs, *scr)

        @pl.when(last)
        def _():
            if not early:
                comm.middle(srcs, zones, news, sems)
            comm.finish(srcs, zones, news, sems)

    res = pl.pallas_call(
        wrapped, name=name, grid=grid,
        in_specs=list(in_specs) + [ANY] * (ns + nz), out_specs=out_specs_l + [ANY] * (nz + nn),
        out_shape=out_shape_l + [jax.ShapeDtypeStruct(z.shape, z.dtype) for z in comm.zones] + comm.news,
        input_output_aliases={n_in + ns + i: n_out + i for i in range(nz)},
        scratch_shapes=list(scratch_shapes) + comm.sems,
        compiler_params=_cp(("arbitrary",) * len(grid)),
    )(*args, *comm.srcs, *comm.zones)
    main = res[0] if single else list(res[:n_out])
    return main, (list(res[n_out:n_out + nz]), list(res[n_out + nz:]))


def _run_comm(comm, name):
    ns, nz, nn = len(comm.srcs), len(comm.zones), len(comm.news)

    def body(*refs):
        srcs, zones, news, sems = refs[:ns], refs[ns + nz:ns + 2 * nz], refs[ns + 2 * nz:ns + 2 * nz + nn], refs[ns + 2 * nz + nn:]
        comm.start(srcs, zones, news, sems)
        comm.middle(srcs, zones, news, sems)
        comm.finish(srcs, zones, news, sems)

    res = pl.pallas_call(
        body, name=name, in_specs=[ANY] * (ns + nz), out_specs=[ANY] * (nz + nn),
        out_shape=[jax.ShapeDtypeStruct(z.shape, z.dtype) for z in comm.zones] + comm.news,
        input_output_aliases={ns + i: i for i in range(nz)}, scratch_shapes=comm.sems,
    )(*comm.srcs, *comm.zones)
    return list(res[:nz]), list(res[nz:])


def _dot(a, b):
    return jnp.dot(a, b, preferred_element_type=F32)


def _dot_nt(a, b):
    return lax.dot_general(a, b, (((1,), (1,)), ((), ())), preferred_element_type=F32)


def _dot_tn(a, b):
    return lax.dot_general(a, b, (((0,), (0,)), ((), ())), preferred_element_type=F32)


def _normal_cdf(x):
    return 0.5 * (1.0 + lax.erf(x * _SQRT_HALF))


def _gelu_and_grad(x, cdf):
    return x * cdf, cdf + x * jnp.exp(-0.5 * x * x) * _INV_SQRT_2PI


def _sigmoid(x):
    return 0.5 * jnp.tanh(0.5 * x) + 0.5


def _rms_bwd(dy, x, gain):
    r = lax.rsqrt(jnp.mean(x * x, axis=-1, keepdims=True) + EPS)
    xhat = x * r
    gdy = dy * gain
    dx = r * (gdy - xhat * jnp.mean(gdy * xhat, axis=-1, keepdims=True))
    return dx, jnp.sum(dy * xhat, axis=0, keepdims=True)


def _norm_matmul(x, gain, w4, name, comm=None, tm=1024):
    S, D = x.shape
    nsh, _, ns = w4.shape
    tm = min(tm, S)

    def body(x_ref, g_ref, w_ref, hn_ref, o_ref):
        xf = x_ref[...]
        r = lax.rsqrt(jnp.mean(xf * xf, axis=-1, keepdims=True) + EPS)
        hn = (xf * r * g_ref[...]).astype(BF16)
        hn_ref[...] = hn
        for j in range(nsh):
            o_ref[:, j * ns:(j + 1) * ns] = _dot(hn, w_ref[j]).astype(BF16)

    return _call(
        body, name=name, grid=(S // tm,),
        in_specs=[pl.BlockSpec((tm, D), lambda i: (i, 0)), _resident((1, D)), _resident(w4.shape)],
        out_specs=[pl.BlockSpec((tm, D), lambda i: (i, 0)), pl.BlockSpec((tm, nsh * ns), lambda i: (i, 0))],
        out_shape=[jax.ShapeDtypeStruct((S, D), BF16), jax.ShapeDtypeStruct((S, nsh * ns), BF16)],
        sem=("parallel",), args=(x, gain, w4), comm=comm)


def _matmul_res(a, w, res, name, comm=None, tm=1024):
    S, K = a.shape
    N = w.shape[1]
    tm = min(tm, S)

    def body(a_ref, w_ref, r_ref, o_ref):
        o_ref[...] = r_ref[...] + _dot(a_ref[...], w_ref[...])

    return _call(
        body, name=name, grid=(S // tm,),
        in_specs=[pl.BlockSpec((tm, K), lambda i: (i, 0)), _resident(w.shape), pl.BlockSpec((tm, N), lambda i: (i, 0))],
        out_specs=pl.BlockSpec((tm, N), lambda i: (i, 0)),
        out_shape=jax.ShapeDtypeStruct((S, N), F32),
        sem=("parallel",), args=(a, w, res), comm=comm)


def _matmul_nt(dh, w, name, comm=None, tm=1024):
    S, N = dh.shape
    K = w.shape[0]
    tm = min(tm, S)

    def body(d_ref, w_ref, o_ref):
        o_ref[...] = _dot_nt(d_ref[...].astype(BF16), w_ref[...]).astype(BF16)

    return _call(
        body, name=name, grid=(S // tm,),
        in_specs=[pl.BlockSpec((tm, N), lambda i: (i, 0)), _resident(w.shape)],
        out_specs=pl.BlockSpec((tm, K), lambda i: (i, 0)),
        out_shape=jax.ShapeDtypeStruct((S, K), BF16),
        sem=("parallel",), args=(dh, w), comm=comm)


def _matmul_tn(a, b, name, *, ka, nb, out_dtype, comm=None, ts=1024):
    S, KA = a.shape
    NB = b.shape[1]
    ts = min(ts, S)
    J = max(KA // ka, NB // nb)
    a_map = (lambda j, s: (s, j)) if KA // ka > 1 else (lambda j, s: (s, 0))
    b_map = (lambda j, s: (s, j)) if NB // nb > 1 else (lambda j, s: (s, 0))
    last = S // ts - 1

    def body(a_ref, b_ref, o_ref, acc):
        s = pl.program_id(1)

        @pl.when(s == 0)
        def _():
            acc[...] = jnp.zeros_like(acc)

        acc[...] += _dot_tn(a_ref[...].astype(BF16), b_ref[...].astype(BF16))

        @pl.when(s == last)
        def _():
            o_ref[0] = acc[...].astype(out_dtype)

    return _call(
        body, name=name, grid=(J, S // ts),
        in_specs=[pl.BlockSpec((ts, ka), a_map), pl.BlockSpec((ts, nb), b_map)],
        out_specs=pl.BlockSpec((1, ka, nb), lambda j, s: (j, 0, 0)),
        out_shape=jax.ShapeDtypeStruct((J, ka, nb), out_dtype),
        scratch_shapes=[pltpu.VMEM((ka, nb), F32)],
        sem=("parallel", "arbitrary"), args=(a, b), comm=comm)


def _sgu_prep(w_s):
    G = w_s.shape[0]

    def body(w_ref, t_ref, tt_ref):
        tri = lax.broadcasted_iota(jnp.int32, (CHUNK, CHUNK), 0) >= lax.broadcasted_iota(jnp.int32, (CHUNK, CHUNK), 1)
        for g in range(G):
            t = jnp.where(tri, w_ref[g], 0.0)
            t_ref[g] = t.astype(BF16)
            tt_ref[g] = t.T.astype(BF16)

    return pl.pallas_call(
        body, name="sgu_prep",
        out_shape=[jax.ShapeDtypeStruct(w_s.shape, BF16), jax.ShapeDtypeStruct(w_s.shape, BF16)],
        compiler_params=_cp(None),
    )(w_s)


def _sgu_fwd(z, x, vg, wtril, bT, wout, comm=None, tm=512):
    S = z.shape[0]
    W = z.shape[1] // 2
    D = x.shape[1]
    tm = min(tm, S)

    def body(z_ref, x_ref, vg_ref, wt_ref, bT_ref, wo_ref, yp_ref, h_ref, cdf_ref):
        def chunk(c, carry):
            r0 = pl.multiple_of(c * CHUNK, CHUNK)
            zc = z_ref[pl.ds(r0, CHUNK), :].astype(F32)
            cdf = _normal_cdf(zc)
            cdf_ref[pl.ds(r0, CHUNK), :] = cdf.astype(BF16)
            u = zc[:, :W] * cdf[:, :W]
            v = zc[:, W:] * cdf[:, W:]
            rv = lax.rsqrt(jnp.mean(v * v, axis=-1, keepdims=True) + EPS)
            vn = (v * rv * vg_ref[...]).astype(BF16)
            for g in range(SGU_GROUPS):
                sl = slice(g * LANES, (g + 1) * LANES)
                s = _dot(wt_ref[g], vn[:, sl]) + bT_ref[:, g:g + 1]
                yp_ref[pl.ds(r0, CHUNK), sl] = (u[:, sl] * s).astype(BF16)
            return carry

        lax.fori_loop(0, tm // CHUNK, chunk, 0)
        h_ref[...] = x_ref[...] + _dot(yp_ref[...], wo_ref[...])

    return _call(
        body, name="sgu_fwd", grid=(S // tm,),
        in_specs=[pl.BlockSpec((tm, 2 * W), lambda i: (i, 0)), pl.BlockSpec((tm, D), lambda i: (i, 0)),
                  _resident((1, W)), _resident(wtril.shape), _resident(bT.shape), _resident(wout.shape)],
        out_specs=[pl.BlockSpec((tm, W), lambda i: (i, 0)), pl.BlockSpec((tm, D), lambda i: (i, 0)),
                   pl.BlockSpec((tm, 2 * W), lambda i: (i, 0))],
        out_shape=[jax.ShapeDtypeStruct((S, W), BF16), jax.ShapeDtypeStruct((S, D), F32),
                   jax.ShapeDtypeStruct((S, 2 * W), BF16)],
        sem=("parallel",), args=(z, x, vg, wtril, bT, wout), comm=comm)


def _sgu_bwd(dh, z, cdf, vg, wtril, wtrilT, bT, wout, comm=None, tm=512):
    S = z.shape[0]
    W = z.shape[1] // 2
    D = dh.shape[1]
    G = SGU_GROUPS
    tm = min(tm, S)
    last = S // tm - 1

    def body(dh_ref, z_ref, cdf_ref, vg_ref, wt_ref, wtT_ref, bT_ref, wo_ref,
             dz_ref, dws_ref, dbT_ref, dvg_ref, dyp_s, du_s, dvn_s, dsacc):
        i = pl.program_id(0)

        @pl.when(i == 0)
        def _():
            dws_ref[...] = jnp.zeros_like(dws_ref)
            dvg_ref[...] = jnp.zeros_like(dvg_ref)
            dsacc[...] = jnp.zeros_like(dsacc)

        dyp_s[...] = _dot_nt(dh_ref[...].astype(BF16), wo_ref[...])
        tri = lax.broadcasted_iota(jnp.int32, (CHUNK, CHUNK), 0) >= lax.broadcasted_iota(jnp.int32, (CHUNK, CHUNK), 1)

        def chunk(c, carry):
            r0 = pl.multiple_of(c * CHUNK, CHUNK)
            zc = z_ref[pl.ds(r0, CHUNK), :].astype(F32)
            cdf = cdf_ref[pl.ds(r0, CHUNK), :].astype(F32)
            u, gu = _gelu_and_grad(zc[:, :W], cdf[:, :W])
            v, gv = _gelu_and_grad(zc[:, W:], cdf[:, W:])
            rv = lax.rsqrt(jnp.mean(v * v, axis=-1, keepdims=True) + EPS)
            vhat = v * rv
            vgain = vg_ref[...]
            vn = (vhat * vgain).astype(BF16)
            dyp = dyp_s[pl.ds(r0, CHUNK), :]
            for g in range(G):
                sl = slice(g * LANES, (g + 1) * LANES)
                vng = vn[:, sl]
                s = _dot(wt_ref[g], vng) + bT_ref[:, g:g + 1]
                ds = dyp[:, sl] * u[:, sl]
                du_s[:, sl] = dyp[:, sl] * s
                dsb = ds.astype(BF16)
                dvn_s[:, sl] = _dot(wtT_ref[g], dsb)
                dws_ref[g] += jnp.where(tri, _dot_nt(dsb, vng), 0.0)
                dsacc[g] += ds
            dvn = dvn_s[...]
            dvg_ref[...] += jnp.sum(dvn * vhat, axis=0, keepdims=True)
            gdy = dvn * vgain
            dv = rv * (gdy - vhat * jnp.mean(gdy * vhat, axis=-1, keepdims=True))
            dz_ref[pl.ds(r0, CHUNK), :W] = (du_s[...] * gu).astype(BF16)
            dz_ref[pl.ds(r0, CHUNK), W:] = (dv * gv).astype(BF16)
            return carry

        lax.fori_loop(0, tm // CHUNK, chunk, 0)

        @pl.when(i == last)
        def _():
            for g in range(G):
                dbT_ref[:, g:g + 1] = jnp.sum(dsacc[g], axis=1, keepdims=True)

    return _call(
        body, name="sgu_bwd", grid=(S // tm,),
        in_specs=[pl.BlockSpec((tm, D), lambda i: (i, 0)), pl.BlockSpec((tm, 2 * W), lambda i: (i, 0)),
                  pl.BlockSpec((tm, 2 * W), lambda i: (i, 0)),
                  _resident((1, W)), _resident(wtril.shape), _resident(wtrilT.shape), _resident(bT.shape),
                  _resident(wout.shape)],
        out_specs=[pl.BlockSpec((tm, 2 * W), lambda i: (i, 0)),
                   pl.BlockSpec((G, CHUNK, CHUNK), lambda i: (0, 0, 0)),
                   pl.BlockSpec((CHUNK, G), lambda i: (0, 0)),
                   pl.BlockSpec((1, W), lambda i: (0, 0))],
        out_shape=[jax.ShapeDtypeStruct((S, 2 * W), BF16), jax.ShapeDtypeStruct((G, CHUNK, CHUNK), F32),
                   jax.ShapeDtypeStruct((CHUNK, G), F32), jax.ShapeDtypeStruct((1, W), F32)],
        scratch_shapes=[pltpu.VMEM((tm, W), F32), pltpu.VMEM((CHUNK, W), F32), pltpu.VMEM((CHUNK, W), F32),
                        pltpu.VMEM((G, CHUNK, CHUNK), F32)],
        sem=("arbitrary",), args=(dh, z, cdf, vg, wtril, wtrilT, bT, wout), comm=comm)


def _conv_taps(a32, r0, R):
    base = r0 + HALO
    return a32[base:base + R, :], a32[base - 1:base - 1 + R, :], a32[base - 2:base - 2 + R, :]


def _ffn_fwd(a, h_in, cw, cb, wdown, name, comm=None, target=None, tm=256, R=64):
    S, C = a.shape
    F = C // 2
    D = h_in.shape[1]
    tm = min(tm, S)
    head = target is not None

    def body(a_ref, halo_ref, h_ref, cw_ref, cb_ref, wd_ref, *rest):
        if head:
            t_ref, f_ref, cp_ref, dy_ref, l_ref, a32 = rest
        else:
            f_ref, cp_ref, ho_ref, a32 = rest
        i = pl.program_id(0)
        a32[0:HALO, :] = jnp.where(i > 0, halo_ref[...].astype(F32), 0.0)
        a32[HALO:, :] = a_ref[...].astype(F32)

        for r0 in range(0, tm, R):
            a0, a1, a2 = _conv_taps(a32, r0, R)
            cpre = cw_ref[0:1, :] * a2 + cw_ref[1:2, :] * a1 + cw_ref[2:3, :] * a0 + cb_ref[...]
            cp_ref[r0:r0 + R, :] = cpre.astype(BF16)
            g = cpre[:, :F]
            f_ref[r0:r0 + R, :] = (g * _sigmoid(g) * cpre[:, F:]).astype(BF16)
        h_out = h_ref[...] + _dot(f_ref[...], wd_ref[...])
        if not head:
            ho_ref[...] = h_out
            return

        @pl.when(i == 0)
        def _():
            l_ref[...] = jnp.zeros_like(l_ref)

        e = h_out - t_ref[...]
        dy_ref[...] = e * (1.0 / D)
        rows = jnp.sum(e * e, axis=-1, keepdims=True) * (1.0 / D)
        l_ref[...] += 0.5 * jnp.sum(rows, axis=0, keepdims=True)

    hb = tm // HALO
    row = pl.BlockSpec((tm, D), lambda i: (i, 0))
    in_specs = [pl.BlockSpec((tm, C), lambda i: (i, 0)),
                pl.BlockSpec((HALO, C), lambda i: (jnp.maximum(i * hb - 1, 0), 0)),
                row, _resident((3, C)), _resident((1, C)), _resident(wdown.shape)]
    out_specs = [pl.BlockSpec((tm, F), lambda i: (i, 0)), pl.BlockSpec((tm, C), lambda i: (i, 0)), row]
    out_shape = [jax.ShapeDtypeStruct((S, F), BF16), jax.ShapeDtypeStruct((S, C), BF16),
                 jax.ShapeDtypeStruct((S, D), F32)]
    args = (a, a, h_in, cw, cb, wdown)
    if head:
        in_specs.append(row)
        out_specs.append(pl.BlockSpec((1, 1), lambda i: (0, 0)))
        out_shape.append(jax.ShapeDtypeStruct((1, 1), F32))
        args += (target,)
    return _call(
        body, name=name, grid=(S // tm,), in_specs=in_specs, out_specs=out_specs, out_shape=out_shape,
        scratch_shapes=[pltpu.VMEM((HALO + tm, C), F32)],
        sem=("arbitrary",) if head else ("parallel",), args=args, comm=comm)


def _rows8(v):
    return functools.reduce(jnp.add, [v[8 * k:8 * k + 8] for k in range(v.shape[0] // 8)])


def _ffn_bwd_dc(dh, cp, wdown, name, comm=None, tm=512, R=64):
    S, C = cp.shape
    F = C // 2
    D = dh.shape[1]
    tm = min(tm, S)

    def body(dh_ref, cp_ref, wd_ref, dc_ref, dcb_ref, df_s, acc):
        i = pl.program_id(0)

        @pl.when(i == 0)
        def _():
            acc[...] = jnp.zeros_like(acc)

        df_s[...] = _dot_nt(dh_ref[...].astype(BF16), wd_ref[...])
        for r0 in range(0, tm, R):
            cpre = cp_ref[r0:r0 + R, :].astype(F32)
            g = cpre[:, :F]
            val = cpre[:, F:]
            sg = _sigmoid(g)
            df = df_s[r0:r0 + R, :]
            dg = df * val * (sg * (1.0 + g * (1.0 - sg)))
            dval = df * (g * sg)
            dc = jnp.concatenate([dg, dval], axis=1)
            dc_ref[r0:r0 + R, :] = dc.astype(BF16)
            acc[...] += _rows8(dc)

        @pl.when(i == S // tm - 1)
        def _():
            dcb_ref[...] = jnp.sum(acc[...], axis=0, keepdims=True)

    return _call(
        body, name=name, grid=(S // tm,),
        in_specs=[pl.BlockSpec((tm, D), lambda i: (i, 0)), pl.BlockSpec((tm, C), lambda i: (i, 0)),
                  _resident(wdown.shape)],
        out_specs=[pl.BlockSpec((tm, C), lambda i: (i, 0)), pl.BlockSpec((1, C), lambda i: (0, 0))],
        out_shape=[jax.ShapeDtypeStruct((S, C), BF16), jax.ShapeDtypeStruct((1, C), F32)],
        scratch_shapes=[pltpu.VMEM((tm, F), F32), pltpu.VMEM((8, C), F32)],
        sem=("arbitrary",), args=(dh, cp, wdown), comm=comm)


def _bwd_norm(dA, w4, h_in, gain, dh_out, name, conv_w=None, conv_in=None, comm=None, R=64):
    S, N = dA.shape
    nsh, D, ns = w4.shape
    conv = conv_w is not None
    tm = min(256 if conv else 512, S)
    nt = S // tm

    def finish(src_ref, w_ref, h_ref, g_ref, dho_ref, dhi_ref, dg_ref):
        dhn = _dot_nt(src_ref[:, 0:ns], w_ref[0])
        for j in range(1, nsh):
            dhn += _dot_nt(src_ref[:, j * ns:(j + 1) * ns], w_ref[j])
        dx, dgain = _rms_bwd(dhn, h_ref[...], g_ref[...])
        dg_ref[...] += dgain
        dhi_ref[...] = dho_ref[...] + dx

    def body_plain(dA_ref, w_ref, h_ref, g_ref, dho_ref, dhi_ref, dg_ref):
        @pl.when(pl.program_id(0) == 0)
        def _():
            dg_ref[...] = jnp.zeros_like(dg_ref)

        finish(dA_ref, w_ref, h_ref, g_ref, dho_ref, dhi_ref, dg_ref)

    def body_conv(dc_ref, halo_ref, cw_ref, a_ref, w_ref, h_ref, g_ref, dho_ref, dhi_ref, dg_ref, da_ref, dcw_ref,
                  dc32, acc):
        i = pl.program_id(0)

        @pl.when(i == 0)
        def _():
            dg_ref[...] = jnp.zeros_like(dg_ref)
            acc[...] = jnp.zeros_like(acc)

        dc32[0:tm, :] = dc_ref[...].astype(F32)
        dc32[tm:, :] = jnp.where(i < nt - 1, halo_ref[...].astype(F32), 0.0)

        for r0 in range(0, tm, R):
            X = dc32[r0:r0 + R + 8, :]
            d0 = X[:R]
            d1 = pltpu.roll(X, R + 7, 0)[:R]
            d2 = pltpu.roll(X, R + 6, 0)[:R]
            da = cw_ref[2:3, :] * d0 + cw_ref[1:2, :] * d1 + cw_ref[0:1, :] * d2
            da_ref[r0:r0 + R, :] = da.astype(BF16)
            a0 = a_ref[r0:r0 + R, :].astype(F32)
            acc[0] += _rows8(d2 * a0)
            acc[1] += _rows8(d1 * a0)
            acc[2] += _rows8(d0 * a0)
        finish(da_ref, w_ref, h_ref, g_ref, dho_ref, dhi_ref, dg_ref)

        @pl.when(i == nt - 1)
        def _():
            for k in range(3):
                dcw_ref[k:k + 1, :] = jnp.sum(acc[k], axis=0, keepdims=True)

    row = lambda width: pl.BlockSpec((tm, width), lambda i: (i, 0))
    common_in = [_resident(w4.shape), row(D), _resident((1, D)), row(D)]
    common_out = [row(D), pl.BlockSpec((1, D), lambda i: (0, 0))]
    common_shape = [jax.ShapeDtypeStruct((S, D), F32), jax.ShapeDtypeStruct((1, D), F32)]
    if not conv:
        return _call(
            body_plain, name=name, grid=(nt,),
            in_specs=[row(N)] + common_in, out_specs=common_out, out_shape=common_shape,
            sem=("arbitrary",), args=(dA, w4, h_in, gain, dh_out), comm=comm)
    hb = tm // HALO
    nhb = S // HALO
    return _call(
        body_conv, name=name, grid=(nt,),
        in_specs=[row(N), pl.BlockSpec((HALO, N), lambda i: (jnp.minimum((i + 1) * hb, nhb - 1), 0)),
                  _resident((3, N)), row(N)] + common_in,
        out_specs=common_out + [row(N), pl.BlockSpec((3, N), lambda i: (0, 0))],
        out_shape=common_shape + [jax.ShapeDtypeStruct((S, N), BF16), jax.ShapeDtypeStruct((3, N), F32)],
        scratch_shapes=[pltpu.VMEM((tm + HALO, N), F32), pltpu.VMEM((3, 8, N), F32)],
        sem=("arbitrary",), args=(dA, dA, conv_w, conv_in, w4, h_in, gain, dh_out), comm=comm)


def _rel_buckets_flat():
    q = np.arange(CHUNK)[:, None] + CHUNK
    k = np.arange(2 * CHUNK)[None, :]
    n = np.maximum(q - k, 0)
    max_exact = REL_BUCKETS // 2
    large = max_exact + (np.log(np.maximum(n, 1).astype(np.float32) / max_exact)
                         / math.log(REL_MAX_DIST / max_exact) * (REL_BUCKETS - max_exact)).astype(np.int32)
    large = np.minimum(large, REL_BUCKETS - 1)
    return np.where(n < max_exact, n, large).astype(np.int32).reshape(1, CHUNK * 2 * CHUNK)


def _split_bf16(x):
    hi = x.astype(BF16)
    return hi, (x - hi.astype(F32)).astype(BF16)


def _rel_bias_expand(rel_bias, bucket):
    B, H = rel_bias.shape
    n = bucket.shape[1]

    def body(rb_ref, bk_ref, o_ref):
        oh = (bk_ref[...] == lax.broadcasted_iota(jnp.int32, (B, n), 0)).astype(BF16)
        hi, lo = _split_bf16(rb_ref[...])
        o_ref[...] = _dot_tn(hi, oh) + _dot_tn(lo, oh)

    return pl.pallas_call(body, name="rel_bias_expand", out_shape=jax.ShapeDtypeStruct((H, n), F32),
                          compiler_params=_cp(None))(rel_bias, bucket)


def _rel_bias_reduce(dbias, bucket):
    H, n = dbias.shape
    B = REL_BUCKETS

    def body(db_ref, bk_ref, o_ref):
        oh = (bk_ref[...] == lax.broadcasted_iota(jnp.int32, (B, n), 0)).astype(BF16)
        hi, lo = _split_bf16(db_ref[...])
        o_ref[...] = _dot_nt(oh, hi) + _dot_nt(oh, lo)

    return pl.pallas_call(body, name="rel_bias_reduce", out_shape=jax.ShapeDtypeStruct((B, H), F32),
                          compiler_params=_cp(None))(dbias, bucket)


def _lo_mask(rows):
    return lax.broadcasted_iota(jnp.int32, (rows, LANES), 1) < HEAD_DIM


def _half_sums(y, lo):
    s_lo = jnp.sum(jnp.where(lo, y, 0.0), axis=-1, keepdims=True)
    s_hi = jnp.sum(jnp.where(lo, 0.0, y), axis=-1, keepdims=True)
    return jnp.where(lo, s_lo, s_hi)


def _half_rms(x, gain, lo):
    r = lax.rsqrt(_half_sums(x * x, lo) * (1.0 / HEAD_DIM) + EPS)
    xhat = x * r
    return xhat * gain, xhat, r


def _half_rms_bwd(dy, xhat, r, gain, lo):
    gdy = dy * gain
    dx = r * (gdy - xhat * (_half_sums(gdy * xhat, lo) * (1.0 / HEAD_DIM)))
    return dx, jnp.sum(dy * xhat, axis=0, keepdims=True)


def _dup_half(pair, e, lo):
    sw = pltpu.roll(pair, HEAD_DIM, 1)
    return jnp.where(lo, pair, sw) if e == 0 else jnp.where(lo, sw, pair)


def _band_valid(n):
    qi = lax.broadcasted_iota(jnp.int32, (KV_GROUP * CHUNK, 2 * CHUNK), 0) & (CHUNK - 1)
    ki = lax.broadcasted_iota(jnp.int32, (KV_GROUP * CHUNK, 2 * CHUNK), 1)
    dist = qi + CHUNK - ki
    return (dist >= 0) & (dist < CHUNK) & ((n > 0) | (ki >= CHUNK))


def _stack_heads(a, b, lo):
    return jnp.concatenate([jnp.where(lo, a, 0.0), jnp.where(lo, 0.0, a), jnp.where(lo, b, 0.0), jnp.where(lo, 0.0, b)],
                           axis=0)


def _unstack_heads(x4, lo):
    return (jnp.where(lo, x4[0:CHUNK], x4[CHUNK:2 * CHUNK]),
            jnp.where(lo, x4[2 * CHUNK:3 * CHUNK], x4[3 * CHUNK:]))


def _sink_col(sink_ref, hk):
    row = lax.broadcasted_iota(jnp.int32, (KV_GROUP * CHUNK, 1), 0)
    col = jnp.full((KV_GROUP * CHUNK, 1), sink_ref[KV_GROUP * hk + KV_GROUP - 1], F32)
    for r in range(KV_GROUP - 2, -1, -1):
        col = jnp.where(row < (r + 1) * CHUNK, sink_ref[KV_GROUP * hk + r], col)
    return col


def _softmax_sink(s, valid, sink):
    s = jnp.where(valid, s, -jnp.inf)
    m = jnp.maximum(jnp.max(s, axis=-1, keepdims=True), sink)
    p = jnp.exp(s - m)
    esink = jnp.exp(sink - m)
    inv = 1.0 / (jnp.sum(p, axis=-1, keepdims=True) + esink)
    return p * inv, esink * inv


QW = N_HEADS * HEAD_DIM
KVW = N_KV_HEADS * HEAD_DIM


def _attn_fwd(qkv, qg2, kg2, sinks, bias, comm=None):
    S = qkv.shape[0]
    nb = S // CHUNK

    def body(cur_ref, prev_ref, qg_ref, kg_ref, sink_ref, bias_ref, o_ref, p_ref, ps_ref):
        n = pl.program_id(0)
        lo = _lo_mask(CHUNK)
        lo2 = _lo_mask(2 * CHUNK)
        valid = _band_valid(n)
        for j in range(N_KV_HEADS // 2):
            kc = slice(QW + j * LANES, QW + (j + 1) * LANES)
            vc = slice(QW + KVW + j * LANES, QW + KVW + (j + 1) * LANES)
            kpair = jnp.concatenate([prev_ref[:, j * LANES:(j + 1) * LANES], cur_ref[:, kc]], axis=0).astype(F32)
            vpair = jnp.concatenate([prev_ref[:, KVW + j * LANES:KVW + (j + 1) * LANES], cur_ref[:, vc]], axis=0).astype(F32)
            knpair, _, _ = _half_rms(kpair, kg_ref[...], lo2)
            for e in range(2):
                hk = 2 * j + e
                kdup = _dup_half(knpair, e, lo2).astype(BF16)
                vdup = _dup_half(vpair, e, lo2).astype(BF16)
                ca = slice(2 * hk * LANES, (2 * hk + 1) * LANES)
                cb = slice((2 * hk + 1) * LANES, (2 * hk + 2) * LANES)
                qna, _, _ = _half_rms(cur_ref[:, ca].astype(F32), qg_ref[...], lo)
                qnb, _, _ = _half_rms(cur_ref[:, cb].astype(F32), qg_ref[...], lo)
                qm4 = _stack_heads(qna, qnb, lo).astype(BF16)
                s = _dot_nt(qm4, kdup) * (HEAD_DIM ** -0.5) + bias_ref[hk]
                p, psink = _softmax_sink(s, valid, _sink_col(sink_ref, hk))
                pb = p.astype(BF16)
                p_ref[0, hk] = pb
                for r in range(KV_GROUP):
                    h = KV_GROUP * hk + r
                    ps_ref[:, h:h + 1] = psink[r * CHUNK:(r + 1) * CHUNK]
                oa, ob = _unstack_heads(_dot(pb, vdup), lo)
                o_ref[:, ca] = oa.astype(BF16)
                o_ref[:, cb] = ob.astype(BF16)

    pshape = (nb, N_KV_HEADS, KV_GROUP * CHUNK, 2 * CHUNK)
    return _call(
        body, name="attn_fwd", grid=(nb,),
        in_specs=[pl.BlockSpec((CHUNK, QW + 2 * KVW), lambda n: (n, 0)),
                  pl.BlockSpec((CHUNK, 2 * KVW), lambda n: (jnp.maximum(n - 1, 0), QW // (2 * KVW))),
                  _resident((1, LANES)), _resident((1, LANES)),
                  pl.BlockSpec(memory_space=pltpu.SMEM),
                  _resident(bias.shape)],
        out_specs=[pl.BlockSpec((CHUNK, QW), lambda n: (n, 0)),
                   pl.BlockSpec((1,) + pshape[1:], lambda n: (n, 0, 0, 0)),
                   pl.BlockSpec((CHUNK, LANES), lambda n: (n, 0))],
        out_shape=[jax.ShapeDtypeStruct((S, QW), BF16), jax.ShapeDtypeStruct(pshape, BF16),
                   jax.ShapeDtypeStruct((S, LANES), F32)],
        sem=("parallel",), args=(qkv, qkv, qg2, kg2, sinks, bias), comm=comm)


def _attn_bwd(qkv, do, probs, psinks, qg2, kg2, comm=None):
    S = qkv.shape[0]
    nb = S // CHUNK

    def body(cur_ref, prev_ref, do_ref, p_ref, ps_ref, qg_ref, kg_ref,
             dqkv_ref, dbias_ref, dqg_ref, dkg_ref, dsink_ref, carry, band, dsacc, gacc):
        i = pl.program_id(0)
        lo = _lo_mask(CHUNK)
        lo2 = _lo_mask(2 * CHUNK)
        lane = lax.broadcasted_iota(jnp.int32, (KV_GROUP * CHUNK, LANES), 1)

        @pl.when(i == 0)
        def _():
            dbias_ref[...] = jnp.zeros_like(dbias_ref)
            carry[...] = jnp.zeros_like(carry)
            dsacc[...] = jnp.zeros_like(dsacc)
            gacc[...] = jnp.zeros_like(gacc)

        qgain = qg_ref[...]
        kgain = kg_ref[...]
        for j in range(N_KV_HEADS // 2):
            kc = slice(QW + j * LANES, QW + (j + 1) * LANES)
            vc = slice(QW + KVW + j * LANES, QW + KVW + (j + 1) * LANES)
            kpair = jnp.concatenate([prev_ref[:, j * LANES:(j + 1) * LANES], cur_ref[:, kc]], axis=0).astype(F32)
            vpair = jnp.concatenate([prev_ref[:, KVW + j * LANES:KVW + (j + 1) * LANES], cur_ref[:, vc]], axis=0).astype(F32)
            knpair, khat, kr = _half_rms(kpair, kgain, lo2)
            dk_folds = []
            dv_folds = []
            for e in range(2):
                hk = 2 * j + e
                kdup = _dup_half(knpair, e, lo2).astype(BF16)
                vdup = _dup_half(vpair, e, lo2).astype(BF16)
                ca = slice(2 * hk * LANES, (2 * hk + 1) * LANES)
                cb = slice((2 * hk + 1) * LANES, (2 * hk + 2) * LANES)
                qna, qhata, qra = _half_rms(cur_ref[:, ca].astype(F32), qgain, lo)
                qnb, qhatb, qrb = _half_rms(cur_ref[:, cb].astype(F32), qgain, lo)
                qm4 = _stack_heads(qna, qnb, lo).astype(BF16)
                dom4 = _stack_heads(do_ref[:, ca].astype(F32), do_ref[:, cb].astype(F32), lo).astype(BF16)
                pb = p_ref[0, hk]
                p = pb.astype(F32)
                psink = jnp.concatenate([ps_ref[:, KV_GROUP * hk + r:KV_GROUP * hk + r + 1] for r in range(KV_GROUP)],
                                        axis=0)
                dp = _dot_nt(dom4, vdup)
                delta = jnp.sum(p * dp, axis=-1, keepdims=True)
                ds = p * (dp - delta)
                dbias_ref[hk] += ds
                dsacc[...] += jnp.where(lane == hk, -(psink * delta), 0.0)
                dsr = (ds * (HEAD_DIM ** -0.5)).astype(BF16)
                dqna, dqnb = _unstack_heads(_dot(dsr, kdup), lo)
                dkd = _dot_tn(dsr, qm4)
                dvd = _dot_tn(pb, dom4)
                dqa, dqga = _half_rms_bwd(dqna, qhata, qra, qgain, lo)
                dqb, dqgb = _half_rms_bwd(dqnb, qhatb, qrb, qgain, lo)
                gacc[0:1, :] += dqga + dqgb
                dqkv_ref[:, ca] = dqa.astype(BF16)
                dqkv_ref[:, cb] = dqb.astype(BF16)
                dk_folds.append(dkd + pltpu.roll(dkd, HEAD_DIM, 1))
                dv_folds.append(dvd + pltpu.roll(dvd, HEAD_DIM, 1))
            dkn = jnp.where(lo2, dk_folds[0], dk_folds[1])
            dk, dkg = _half_rms_bwd(dkn, khat, kr, kgain, lo2)
            gacc[1:2, :] += dkg
            band[:, j * LANES:(j + 1) * LANES] = dk
            band[:, KVW + j * LANES:KVW + (j + 1) * LANES] = jnp.where(lo2, dv_folds[0], dv_folds[1])
        dqkv_ref[:, QW:] = (band[CHUNK:, :] + carry[...]).astype(BF16)
        carry[...] = band[0:CHUNK, :]

        @pl.when(i == nb - 1)
        def _():
            g = gacc[...]
            g = g + pltpu.roll(g, HEAD_DIM, 1)
            dqg_ref[...] = g[0:1, :]
            dkg_ref[...] = g[1:2, :]
            for r in range(KV_GROUP):
                dsink_ref[r:r + 1, :] = jnp.sum(dsacc[r * CHUNK:(r + 1) * CHUNK, :], axis=0, keepdims=True)

    vec = pl.BlockSpec((1, LANES), lambda i: (0, 0))
    bshape = probs.shape[1:]
    return _call(
        body, name="attn_bwd", grid=(nb,),
        in_specs=[pl.BlockSpec((CHUNK, QW + 2 * KVW), lambda i: (nb - 1 - i, 0)),
                  pl.BlockSpec((CHUNK, 2 * KVW), lambda i: (jnp.maximum(nb - 2 - i, 0), QW // (2 * KVW))),
                  pl.BlockSpec((CHUNK, QW), lambda i: (nb - 1 - i, 0)),
                  pl.BlockSpec((1,) + bshape, lambda i: (nb - 1 - i, 0, 0, 0)),
                  pl.BlockSpec((CHUNK, LANES), lambda i: (nb - 1 - i, 0)),
                  _resident((1, LANES)), _resident((1, LANES))],
        out_specs=[pl.BlockSpec((CHUNK, QW + 2 * KVW), lambda i: (nb - 1 - i, 0)),
                   pl.BlockSpec(bshape, lambda i: (0, 0, 0)), vec, vec,
                   pl.BlockSpec((KV_GROUP, LANES), lambda i: (0, 0))],
        out_shape=[jax.ShapeDtypeStruct((S, QW + 2 * KVW), BF16),
                   jax.ShapeDtypeStruct(bshape, F32),
                   jax.ShapeDtypeStruct((1, LANES), F32), jax.ShapeDtypeStruct((1, LANES), F32),
                   jax.ShapeDtypeStruct((KV_GROUP, LANES), F32)],
        scratch_shapes=[pltpu.VMEM((CHUNK, 2 * KVW), F32), pltpu.VMEM((2 * CHUNK, 2 * KVW), F32),
                        pltpu.VMEM((KV_GROUP * CHUNK, LANES), F32), pltpu.VMEM((8, LANES), F32)],
        sem=("arbitrary",), args=(qkv, qkv, do, probs, psinks, qg2, kg2), comm=comm)


def _row_tile(rows, cols, n_arrays):
    budget = VMEM_LIMIT_V7X // 4 // (2 * n_arrays * 4 * cols)
    best = 8
    for n in range(1, rows // 8 + 1):
        if rows % n == 0 and (rows // n) % 8 == 0 and rows // n <= budget:
            best = rows // n
            break
    return best


def _adamw(g, w, m, v, name, comm=None):
    R, C = g.shape
    tr = _row_tile(R, C, 8)

    def body(g_ref, w_ref, m_ref, v_ref, d_ref, mo_ref, vo_ref, go_ref):
        gg = g_ref[...]
        go_ref[...] = gg
        mn = ADAM_B1 * m_ref[...] + (1.0 - ADAM_B1) * gg
        vn = ADAM_B2 * v_ref[...] + (1.0 - ADAM_B2) * jnp.square(gg)
        m_hat = mn / (1.0 - ADAM_B1 ** ADAM_STEP)
        v_hat = vn / (1.0 - ADAM_B2 ** ADAM_STEP)
        d_ref[...] = -ADAM_LR * (m_hat / (jnp.sqrt(v_hat) + ADAM_EPS) + ADAM_WD * w_ref[...])
        mo_ref[...] = mn
        vo_ref[...] = vn

    spec = pl.BlockSpec((tr, C), lambda i: (i, 0))
    return _call(
        body, name=name, grid=(R // tr,), in_specs=[spec] * 4, out_specs=[spec] * 4,
        out_shape=[jax.ShapeDtypeStruct((R, C), F32)] * 4, sem=("parallel",), args=(g, w, m, v), comm=comm)


def _place_shard(shards, layer, where, dtype, name):
    _, R, C = shards.shape
    tr = _row_tile(R, C, 2) if R % 8 == 0 else R

    def body(s_ref, x_ref, o_ref):
        o_ref[...] = x_ref[...].astype(dtype)

    return pl.pallas_call(
        body, name=name,
        grid_spec=pltpu.PrefetchScalarGridSpec(
            num_scalar_prefetch=1, grid=(R // tr,),
            in_specs=[pl.BlockSpec((1, tr, C), lambda i, s_ref: (layer, i, 0))],
            out_specs=pl.BlockSpec((1, tr, C), lambda i, s_ref: (s_ref[1], i, 0))),
        out_shape=jax.ShapeDtypeStruct((4, R, C), dtype),
        compiler_params=_cp(("parallel",)),
    )(where, shards)


def _pair_add(g4, rsib, where, name):
    J, R, C = g4.shape
    Rh = R // 2
    tr = _row_tile(Rh, C, 4)
    g5 = g4.reshape(J, 2, Rh, C)

    def body(s_ref, g_ref, r_ref, p_ref, q_ref):
        val = (g_ref[...].astype(F32)[0] + r_ref[...].astype(F32)).astype(BF16)
        p_ref[...] = val

        @pl.when(pl.program_id(1) == s_ref[1])
        def _():
            q_ref[...] = val

    return pl.pallas_call(
        body, name=name,
        grid_spec=pltpu.PrefetchScalarGridSpec(
            num_scalar_prefetch=1, grid=(Rh // tr, J),
            in_specs=[pl.BlockSpec((1, 1, tr, C), lambda i, j, s_ref: (j, s_ref[0], i, 0)),
                      pl.BlockSpec((1, tr, C), lambda i, j, s_ref: (j, i, 0))],
            out_specs=[pl.BlockSpec((1, tr, C), lambda i, j, s_ref: (j, i, 0)),
                       pl.BlockSpec((1, tr, C), lambda i, j, s_ref: (s_ref[1], i, 0))]),
        out_shape=[jax.ShapeDtypeStruct((J, Rh, C), BF16)] * 2,
        compiler_params=_cp(("parallel", "arbitrary")),
    )(where, g5, rsib)


def _sum_chips(q, where, dest, layer, out_shape, name):
    J, Rh, C = q.shape
    tr = _row_tile(Rh, C, 3)
    nb = Rh // tr

    def body(s_ref, q_ref, *rest):
        qq = q_ref[...].astype(F32)
        rest[-1][0] = ((qq[0] + qq[1]) + qq[2]) + qq[3]

    have = dest is not None
    return pl.pallas_call(
        body, name=name,
        grid_spec=pltpu.PrefetchScalarGridSpec(
            num_scalar_prefetch=1, grid=(nb,),
            in_specs=[pl.BlockSpec((J, tr, C), lambda i, s_ref: (0, i, 0))] + ([ANY] if have else []),
            out_specs=pl.BlockSpec((1, tr, C), lambda i, s_ref: (layer, s_ref[0] * nb + i, 0))),
        out_shape=jax.ShapeDtypeStruct(out_shape, F32),
        input_output_aliases={2: 0} if have else {},
        compiler_params=_cp(("parallel",)),
    )(*((where, q, dest) if have else (where, q)))


MESH = pl.DeviceIdType.MESH
ANY = pl.BlockSpec(memory_space=pl.ANY)


def _place():
    x, y, c = lax.axis_index("x"), lax.axis_index("y"), lax.axis_index("c")
    others = [(1 - x, y), (x, 1 - y), (1 - x, 1 - y)]
    return x, y, c, 2 * x + y, others, [2 * ox + oy for ox, oy in others]


def _gather_comm(items):
    n = len(items)
    placed = [it[0] for it in items]
    split = [it[3] for it in items]

    def rows(t, ref, half):
        _, lo, hi, _ = items[t]
        if not split[t]:
            return ref if (lo, hi) == (0, placed[t].shape[1]) else ref.at[pl.ds(lo, hi - lo), :]
        rh = (hi - lo) // 2
        return ref.at[pl.ds(lo + half * rh, rh), :]

    def sends(outs, sems):
        send, recv = sems[0], sems[1]
        x, y, c, me, others, okey = _place()
        cps = []
        for t in range(n):
            mine = rows(t, outs[t].at[me], c)
            for j, (ox, oy) in enumerate(others):
                cps.append(pltpu.make_async_remote_copy(
                    src_ref=mine, dst_ref=mine,
                    send_sem=send.at[t, j], recv_sem=recv.at[t, j], device_id=(ox, oy, c), device_id_type=MESH))
        return cps

    def start(srcs, outs, news, sems):
        for cp in sends(outs, sems):
            cp.start()

    def forwards(outs, sems):
        fsend, frecv = sems[2], sems[3]
        x, y, c, me, others, okey = _place()
        cps = []
        for t in range(n):
            if split[t]:
                for j in range(3):
                    landed = rows(t, outs[t].at[okey[j]], c)
                    cps.append(pltpu.make_async_remote_copy(
                        src_ref=landed, dst_ref=landed, send_sem=fsend.at[t, j], recv_sem=frecv.at[t, j],
                        device_id=(x, y, 1 - c), device_id_type=MESH))
        return cps

    def middle(srcs, outs, news, sems):
        send, recv = sems[0], sems[1]
        x, y, c, me, others, okey = _place()
        for t in range(n):
            for j in range(3):
                landed = rows(t, outs[t].at[okey[j]], c)
                pltpu.make_async_remote_copy(
                    src_ref=landed, dst_ref=landed, send_sem=send.at[t, j], recv_sem=recv.at[t, j],
                    device_id=(x, y, 1 - c), device_id_type=MESH).wait_recv()
        for cp in forwards(outs, sems):
            cp.start()

    def finish(srcs, outs, news, sems):
        fsend, frecv = sems[2], sems[3]
        x, y, c, me, others, okey = _place()
        for t in range(n):
            if split[t]:
                for j in range(3):
                    theirs = rows(t, outs[t].at[okey[j]], 1 - c)
                    pltpu.make_async_remote_copy(
                        src_ref=theirs, dst_ref=theirs, send_sem=fsend.at[t, j], recv_sem=frecv.at[t, j],
                        device_id=(x, y, 1 - c), device_id_type=MESH).wait_recv()
        for cp in sends(outs, sems) + forwards(outs, sems):
            cp.wait_send()

    return _Comm([], placed, [], [pltpu.SemaphoreType.DMA((n, 3))] * 4, start, finish, middle)


def _pair_exchange_comm(gs):
    n = len(gs)

    def copies(ins, outs, sems):
        send, recv = sems
        x, y, c, _, _, _ = _place()
        cps = []
        for t in range(n):
            rh = gs[t].shape[1] // 2
            cps.append(pltpu.make_async_remote_copy(
                src_ref=ins[t].at[:, pl.ds((1 - c) * rh, rh), :], dst_ref=outs[t],
                send_sem=send.at[t], recv_sem=recv.at[t], device_id=(x, y, 1 - c), device_id_type=MESH))
        return cps

    def start(ins, zones, outs, sems):
        for cp in copies(ins, outs, sems):
            cp.start()

    def finish(ins, zones, outs, sems):
        for cp in copies(ins, outs, sems):
            cp.wait()

    news = [jax.ShapeDtypeStruct((4, g.shape[1] // 2, g.shape[2]), g.dtype) for g in gs]
    return _Comm(gs, [], news, [pltpu.SemaphoreType.DMA((n,))] * 2, start, finish)


def _chip_scatter_comm(ps, qs):
    n = len(ps)

    def sends(ins, outs, sems):
        send, recv = sems
        x, y, c, me, others, okey = _place()
        return [pltpu.make_async_remote_copy(
            src_ref=ins[t].at[okey[j]], dst_ref=outs[t].at[me],
            send_sem=send.at[t, j], recv_sem=recv.at[t, j], device_id=(ox, oy, c), device_id_type=MESH)
            for t in range(n) for j, (ox, oy) in enumerate(others)]

    def start(ins, outs, news, sems):
        for cp in sends(ins, outs, sems):
            cp.start()

    def finish(ins, outs, news, sems):
        send, recv = sems
        x, y, c, me, others, okey = _place()
        for t in range(n):
            for j in range(3):
                slot = outs[t].at[okey[j]]
                pltpu.make_async_remote_copy(
                    src_ref=slot, dst_ref=slot, send_sem=send.at[t, j], recv_sem=recv.at[t, j],
                    device_id=(x, y, c), device_id_type=MESH).wait_recv()
        for cp in sends(ins, outs, sems):
            cp.wait_send()

    return _Comm(ps, qs, [], [pltpu.SemaphoreType.DMA((n, 3))] * 2, start, finish)


def _half_exchange_comm(arrs, layers=None):
    n = len(arrs)
    items = [(t, layer) for t in range(n) for layer in (range(arrs[t].shape[0]) if layers is None else layers[t])]

    def sends(outs, sems):
        send, recv = sems
        x, y, c, _, _, _ = _place()
        cps = []
        for k, (t, layer) in enumerate(items):
            rh = arrs[t].shape[1] // 2
            mine = outs[t].at[layer, pl.ds(c * rh, rh), :]
            cps.append(pltpu.make_async_remote_copy(
                src_ref=mine, dst_ref=mine, send_sem=send.at[k], recv_sem=recv.at[k],
                device_id=(x, y, 1 - c), device_id_type=MESH))
        return cps

    def start(srcs, outs, news, sems):
        for cp in sends(outs, sems):
            cp.start()

    def finish(srcs, outs, news, sems):
        send, recv = sems
        x, y, c, _, _, _ = _place()
        for k, (t, layer) in enumerate(items):
            rh = arrs[t].shape[1] // 2
            theirs = outs[t].at[layer, pl.ds((1 - c) * rh, rh), :]
            pltpu.make_async_remote_copy(
                src_ref=theirs, dst_ref=theirs, send_sem=send.at[k], recv_sem=recv.at[k],
                device_id=(x, y, 1 - c), device_id_type=MESH).wait_recv()
        for cp in sends(outs, sems):
            cp.wait_send()

    return _Comm([], arrs, [], [pltpu.SemaphoreType.DMA((len(items),))] * 2, start, finish)


SMALL_COLS = 1024
SMALL_PIECE_ROWS = 48


def _allreduce_small(buf, comm=None):
    pr = SMALL_PIECE_ROWS
    flips = [(d >> 2 & 1, d >> 1 & 1, d & 1) for d in range(1, 8)]

    def body(x_ref, o_ref, rbuf, send1, recv1, send2, recv2):
        x, y, c = lax.axis_index("x"), lax.axis_index("y"), lax.axis_index("c")
        me = 4 * x + 2 * y + c
        peers = [(x ^ fx, y ^ fy, c ^ fc) for fx, fy, fc in flips]
        pid = [4 * px + 2 * py + pc for px, py, pc in peers]

        def piece(ref, p):
            return ref.at[pl.ds(pl.multiple_of(p * pr, 8), pr), :]

        cps = []
        for d in range(7):
            cp = pltpu.make_async_remote_copy(
                src_ref=piece(x_ref, pid[d]), dst_ref=rbuf.at[d + 1],
                send_sem=send1.at[d], recv_sem=recv1.at[d], device_id=peers[d], device_id_type=MESH)
            cp.start()
            cps.append(cp)
        acc = piece(x_ref, me)[...]
        for d in range(7):
            cps[d].wait_recv()
            acc = acc + rbuf[d + 1]
        piece(o_ref, me)[...] = acc
        out = []
        for d in range(7):
            cp = pltpu.make_async_remote_copy(
                src_ref=piece(o_ref, me), dst_ref=piece(o_ref, me),
                send_sem=send2.at[d], recv_sem=recv2.at[d], device_id=peers[d], device_id_type=MESH)
            cp.start()
            out.append(cp)
        for d in range(7):
            pltpu.make_async_remote_copy(
                src_ref=piece(o_ref, pid[d]), dst_ref=piece(o_ref, pid[d]),
                send_sem=send2.at[d], recv_sem=recv2.at[d], device_id=peers[d], device_id_type=MESH).wait_recv()
        for cp in cps + out:
            cp.wait_send()

    vm = pl.BlockSpec(memory_space=pltpu.VMEM)
    return _call(
        body, name="small_allreduce", grid=(), in_specs=[vm], out_specs=vm,
        out_shape=jax.ShapeDtypeStruct(buf.shape, F32),
        scratch_shapes=[pltpu.VMEM((8, pr, SMALL_COLS), F32)] + [pltpu.SemaphoreType.DMA((7,))] * 4,
        args=(buf,), comm=comm)


def _rows_of(shape):
    return -(-math.prod(shape) // (8 * SMALL_COLS)) * 8


def _pack(arrays, rows):
    parts = []
    for a in arrays:
        r = _rows_of(a.shape)
        parts.append(jnp.pad(a.reshape(-1), (0, r * SMALL_COLS - a.size)).reshape(r, SMALL_COLS))
    used = sum(p.shape[0] for p in parts)
    if rows > used:
        parts.append(jnp.zeros((rows - used, SMALL_COLS), F32))
    return jnp.concatenate(parts, axis=0)


def _unpack(buf, shapes):
    out, off = [], 0
    for s in shapes:
        r = _rows_of(s)
        out.append(buf[off:off + r].reshape(-1)[:math.prod(s)].reshape(s))
        off += r
    return out


BIG = ["sgu_w_in", "sgu_w_out", "attn_w_qkv", "attn_w_o", "ffn_w_up", "ffn_w_down"]
SMALL = ["mix_norm", "ffn_norm", "sgu_v_gain", "sgu_w_s", "sgu_b_s", "attn_q_gain", "attn_k_gain", "attn_sinks",
         "rel_bias", "ffn_conv_b"]
ORDER = ["mix_norm", "ffn_norm", "sgu_w_in", "sgu_v_gain", "sgu_w_s", "sgu_b_s", "sgu_w_out", "attn_w_qkv",
         "attn_q_gain", "attn_k_gain", "attn_sinks", "attn_w_o", "rel_bias", "ffn_w_up", "ffn_conv_w", "ffn_conv_b",
         "ffn_w_down"]


def kernel(x, mix_norm, ffn_norm, sgu_w_in, sgu_v_gain, sgu_w_s, sgu_b_s, sgu_w_out, attn_w_qkv, attn_q_gain, attn_k_gain, attn_sinks, attn_w_o, rel_bias, ffn_w_up, ffn_conv_w, ffn_conv_b, ffn_w_down, loss_target, m_mix_norm, m_ffn_norm, m_sgu_w_in, m_sgu_v_gain, m_sgu_w_s, m_sgu_b_s, m_sgu_w_out, m_attn_w_qkv, m_attn_q_gain, m_attn_k_gain, m_attn_sinks, m_attn_w_o, m_rel_bias, m_ffn_w_up, m_ffn_conv_w, m_ffn_conv_b, m_ffn_w_down, v_mix_norm, v_ffn_norm, v_sgu_w_in, v_sgu_v_gain, v_sgu_w_s, v_sgu_b_s, v_sgu_w_out, v_attn_w_qkv, v_attn_q_gain, v_attn_k_gain, v_attn_sinks, v_attn_w_o, v_rel_bias, v_ffn_w_up, v_ffn_conv_w, v_ffn_conv_b, v_ffn_w_down):
    w = dict(mix_norm=mix_norm, ffn_norm=ffn_norm, sgu_w_in=sgu_w_in, sgu_v_gain=sgu_v_gain, sgu_w_s=sgu_w_s,
             sgu_b_s=sgu_b_s, sgu_w_out=sgu_w_out, attn_w_qkv=attn_w_qkv, attn_q_gain=attn_q_gain,
             attn_k_gain=attn_k_gain, attn_sinks=attn_sinks, attn_w_o=attn_w_o, rel_bias=rel_bias, ffn_w_up=ffn_w_up,
             ffn_conv_w=ffn_conv_w, ffn_conv_b=ffn_conv_b, ffn_w_down=ffn_w_down)
    mom = dict(mix_norm=m_mix_norm, ffn_norm=m_ffn_norm, sgu_w_in=m_sgu_w_in, sgu_v_gain=m_sgu_v_gain,
               sgu_w_s=m_sgu_w_s, sgu_b_s=m_sgu_b_s, sgu_w_out=m_sgu_w_out, attn_w_qkv=m_attn_w_qkv,
               attn_q_gain=m_attn_q_gain, attn_k_gain=m_attn_k_gain, attn_sinks=m_attn_sinks, attn_w_o=m_attn_w_o,
               rel_bias=m_rel_bias, ffn_w_up=m_ffn_w_up, ffn_conv_w=m_ffn_conv_w, ffn_conv_b=m_ffn_conv_b,
               ffn_w_down=m_ffn_w_down)
    var = dict(mix_norm=v_mix_norm, ffn_norm=v_ffn_norm, sgu_w_in=v_sgu_w_in, sgu_v_gain=v_sgu_v_gain,
               sgu_w_s=v_sgu_w_s, sgu_b_s=v_sgu_b_s, sgu_w_out=v_sgu_w_out, attn_w_qkv=v_attn_w_qkv,
               attn_q_gain=v_attn_q_gain, attn_k_gain=v_attn_k_gain, attn_sinks=v_attn_sinks, attn_w_o=v_attn_w_o,
               rel_bias=v_rel_bias, ffn_w_up=v_ffn_w_up, ffn_conv_w=v_ffn_conv_w, ffn_conv_b=v_ffn_conv_b,
               ffn_w_down=v_ffn_w_down)
    chip = 2 * lax.axis_index("x") + lax.axis_index("y")
    core = lax.axis_index("c")

    where = jnp.stack([core, chip]).astype(jnp.int32)
    names = ["sgu_w_in", "sgu_w_out", "attn_w_qkv", "attn_w_o", "ffn_w_up0", "ffn_w_up1", "ffn_w_down0", "ffn_w_down1"]
    shards = [(sgu_w_in, 0), (sgu_w_out, 0), (attn_w_qkv, 0), (attn_w_o, 0), (ffn_w_up, 0), (ffn_w_up, 1),
              (ffn_w_down, 0), (ffn_w_down, 1)]
    T = {nm: _place_shard(s, l, where, BF16, "place_" + nm) for (s, l), nm in zip(shards, names)}
    for l in range(2):
        T["conv_w%d" % l] = _place_shard(ffn_conv_w, l, where, F32, "place_conv_w%d" % l)

    def gather(keys):
        items = []
        for k in keys:
            k, lo, hi = (k, 0, None) if isinstance(k, str) else k
            items.append((T[k], lo, T[k].shape[1] if hi is None else hi, not k.startswith("conv")))
        return _gather_comm(items)

    def gathered(keys, res):
        for k, a in zip(keys, res[0]):
            T[k if isinstance(k, str) else k[0]] = a

    D = x.shape[2]
    first = ["sgu_w_in", "conv_w0", "conv_w1"]
    gathered(first, _run_comm(gather(first), "gather_first"))
    unshard_cols = lambda a: jnp.transpose(a, (1, 0, 2)).reshape(a.shape[1], -1)
    cw = [unshard_cols(T["conv_w0"]), unshard_cols(T["conv_w1"])]
    cb = ffn_conv_b
    flat = lambda k: T[k].reshape(-1, D)
    x2, target = x[0], loss_target[0]
    bucket = jnp.asarray(_rel_buckets_flat())
    wtril, wtrilT = _sgu_prep(sgu_w_s[0])
    bT = sgu_b_s[0].T
    bias = _rel_bias_expand(rel_bias, bucket).reshape(N_KV_HEADS, KV_GROUP * CHUNK, 2 * CHUNK)
    qg2 = jnp.tile(attn_q_gain, (1, 2))
    kg2 = jnp.tile(attn_k_gain, (1, 2))
    sinks = attn_sinks.reshape(N_HEADS)
    mix0, mix1 = mix_norm[0:1], mix_norm[1:2]
    fn0, fn1 = ffn_norm[0:1], ffn_norm[1:2]

    ks = ["sgu_w_out", ("ffn_w_up0", 0, 256)]
    (hn0, z), r = _norm_matmul(x2, mix0, T["sgu_w_in"], "sgu_in", comm=gather(ks))
    gathered(ks, r)
    ks = [("ffn_w_up0", 256, None)]
    (yp, h1, cdf), r = _sgu_fwd(z, x2, sgu_v_gain, wtril, bT, flat("sgu_w_out"), comm=gather(ks))
    gathered(ks, r)
    ks = ["ffn_w_down0", "attn_w_qkv"]
    (hf0, a0), r = _norm_matmul(h1, fn0, T["ffn_w_up0"], "ffn0_up", comm=gather(ks))
    gathered(ks, r)
    ks = ["attn_w_o", ("ffn_w_up1", 0, 640)]
    (f0, cp0, h2), r = _ffn_fwd(a0, h1, cw[0], cb[0:1], flat("ffn_w_down0"), "ffn0_fwd", comm=gather(ks))
    gathered(ks, r)
    (hn1, qkv), _ = _norm_matmul(h2, mix1, T["attn_w_qkv"], "attn_qkv")
    ks = [("ffn_w_up1", 640, None), "ffn_w_down1"]
    (o, probs, psinks), r = _attn_fwd(qkv, qg2, kg2, sinks, bias, comm=gather(ks))
    gathered(ks, r)
    h3, _ = _matmul_res(o, flat("attn_w_o"), h2, "attn_out")
    (hf1, a1), _ = _norm_matmul(h3, fn1, T["ffn_w_up1"], "ffn1_up")
    (f1, cp1, dh4, loss_local), _ = _ffn_fwd(a1, h3, cw[1], cb[1:2], flat("ffn_w_down1"), "ffn1_fwd", target=target)

    G, RS, PQ, QD, halves, grads = {}, {}, {}, {}, {}, {}
    dest_of = {"sgu_w_in": ("sgu_w_in", 0), "sgu_w_out": ("sgu_w_out", 0), "attn_w_qkv": ("attn_w_qkv", 0),
               "attn_w_o": ("attn_w_o", 0), "ffn_w_up0": ("ffn_w_up", 0), "ffn_w_up1": ("ffn_w_up", 1),
               "ffn_w_down0": ("ffn_w_down", 0), "ffn_w_down1": ("ffn_w_down", 1)}

    def px(keys):
        return _pair_exchange_comm([G[k] for k in keys])

    def px_done(keys, res):
        for k, a in zip(keys, res[1]):
            RS[k] = a
            PQ[k] = _pair_add(G[k], a, where, "pair_add_" + k)

    def sc(keys):
        return _chip_scatter_comm([PQ[k][0] for k in keys], [PQ[k][1] for k in keys])

    def sc_done(keys, res):
        for k, a in zip(keys, res[0]):
            wk, layer = dest_of[k]
            halves[wk] = _sum_chips(a, where, halves.get(wk), layer, w[wk].shape, "sum_chips_" + k)

    nup = ffn_w_up.shape[2]
    ndown = ffn_w_down.shape[1]
    rows4 = lambda a: a.reshape(4, ndown, D)
    (dc1, dcb1), _ = _ffn_bwd_dc(dh4, cp1, flat("ffn_w_down1"), "ffn1_bwd_dc")
    gw, _ = _matmul_tn(f1, dh4, "ffn1_dw_down", ka=2 * ndown, nb=D, out_dtype=BF16)
    G["ffn_w_down1"] = rows4(gw)
    (dh3, dfn1, da1, dcw1), _ = _bwd_norm(dc1, T["ffn_w_up1"], h3, fn1, dh4, "ffn1_bwd_in", conv_w=cw[1], conv_in=a1)
    G["ffn_w_up1"], _ = _matmul_tn(hf1, da1, "ffn1_dw_up", ka=D, nb=nup, out_dtype=BF16)
    ks = ["ffn_w_down1", "ffn_w_up1"]
    G["attn_w_o"], r = _matmul_tn(o, dh3, "attn_dw_o", ka=QW // 4, nb=D, out_dtype=BF16, comm=px(ks))
    px_done(ks, r)
    do, _ = _matmul_nt(dh3, flat("attn_w_o"), "attn_bwd_out")
    (dqkv, dbias, dqg, dkg, dsink), r = _attn_bwd(qkv, do, probs, psinks, qg2, kg2, comm=sc(ks))
    sc_done(ks, r)
    G["attn_w_qkv"], _ = _matmul_tn(hn1, dqkv, "attn_dw_qkv", ka=D, nb=dqkv.shape[1] // 4, out_dtype=BF16)
    ks = ["attn_w_o", "attn_w_qkv"]
    (dh2, dmix1), r = _bwd_norm(dqkv, T["attn_w_qkv"], h2, mix1, dh3, "attn_bwd_in", comm=px(ks))
    px_done(ks, r)
    (dc0, dcb0), r = _ffn_bwd_dc(dh2, cp0, flat("ffn_w_down0"), "ffn0_bwd_dc", comm=sc(ks))
    sc_done(ks, r)
    gw, _ = _matmul_tn(f0, dh2, "ffn0_dw_down", ka=2 * ndown, nb=D, out_dtype=BF16)
    G["ffn_w_down0"] = rows4(gw)
    (dh1, dfn0, da0, dcw0), r = _bwd_norm(dc0, T["ffn_w_up0"], h1, fn0, dh2, "ffn0_bwd_in", conv_w=cw[0], conv_in=a0,
                                          comm=px(["ffn_w_down0"]))
    px_done(["ffn_w_down0"], r)
    G["ffn_w_up0"], r = _matmul_tn(hf0, da0, "ffn0_dw_up", ka=D, nb=nup, out_dtype=BF16, comm=sc(["ffn_w_down0"]))
    sc_done(["ffn_w_down0"], r)
    ks = ["ffn_w_up0"]
    G["sgu_w_out"], r = _matmul_tn(yp, dh1, "sgu_dw_out", ka=yp.shape[1] // 4, nb=D, out_dtype=BF16, comm=px(ks))
    px_done(ks, r)
    both = [sc(ks), px(["sgu_w_out"])]
    (dz, dws, dbT, dvg), r = _sgu_bwd(dh1, z, cdf, sgu_v_gain, wtril, wtrilT, bT, flat("sgu_w_out"),
                                      comm=_join(both))
    r = _split(both, r)
    sc_done(ks, r[0])
    px_done(["sgu_w_out"], r[1])
    done = ["attn_w_qkv", "attn_w_o", "ffn_w_up", "ffn_w_down"]
    both = [sc(["sgu_w_out"]), _half_exchange_comm([halves[k] for k in done])]
    G["sgu_w_in"], r = _matmul_tn(hn0, dz, "sgu_dw_in", ka=D, nb=dz.shape[1] // 4, out_dtype=BF16, comm=_join(both))
    r = _split(both, r)
    sc_done(["sgu_w_out"], r[0])
    for k, a in zip(done, r[1][0]):
        grads[k] = a
    px_done(["sgu_w_in"], _run_comm(px(["sgu_w_in"]), "grad_last_pair_exchange"))
    (grad_x, dmix0), r = _bwd_norm(dz, T["sgu_w_in"], x2, mix0, dh1, "sgu_bwd_in", comm=sc(["sgu_w_in"]))
    sc_done(["sgu_w_in"], r)

    g = dict(mix_norm=jnp.concatenate([dmix0, dmix1], axis=0), ffn_norm=jnp.concatenate([dfn0, dfn1], axis=0),
             sgu_v_gain=dvg, sgu_w_s=dws, sgu_b_s=dbT.T, attn_q_gain=dqg[:, :HEAD_DIM], attn_k_gain=dkg[:, :HEAD_DIM],
             attn_sinks=dsink[:, :N_KV_HEADS].T.reshape(1, N_HEADS),
             rel_bias=_rel_bias_reduce(dbias.reshape(N_HEADS, CHUNK * 2 * CHUNK), bucket),
             ffn_conv_b=jnp.concatenate([dcb0, dcb1], axis=0))
    small_list = [g[k].reshape(w[k].shape) for k in SMALL] + [jnp.stack([dcw0, dcw1]), loss_local]
    small_shapes = [a.shape for a in small_list]
    done = ["sgu_w_in", "sgu_w_out"]
    red, r = _allreduce_small(_pack(small_list, 8 * SMALL_PIECE_ROWS),
                              comm=_half_exchange_comm([halves[k] for k in done]))
    for k, a in zip(done, r[0]):
        grads[k] = a
    red = _unpack(red, small_shapes)
    for k, a in zip(SMALL, red):
        grads[k] = a
    grads["ffn_conv_w"] = lax.dynamic_slice_in_dim(red[-2], chip * ffn_conv_w.shape[2], ffn_conv_w.shape[2], axis=2)
    loss = red[-1][0, 0]

    delta, new_m, new_v = {}, {}, {}
    two = lambda a: a.reshape(-1, a.shape[-1])
    for k in BIG:
        (d2, m2, v2, g2), _ = _adamw(two(grads[k]), two(w[k]), two(mom[k]), two(var[k]), "adamw_" + k)
        delta[k], new_m[k], new_v[k], grads[k] = (a.reshape(w[k].shape) for a in (d2, m2, v2, g2))
    sm = SMALL + ["ffn_conv_w"]
    sm_shapes = [w[k].shape for k in sm]
    rows = sum(_rows_of(s) for s in sm_shapes)
    (d2, m2, v2, _), _ = _adamw(_pack([grads[k] for k in sm], rows), _pack([w[k] for k in sm], rows),
                                _pack([mom[k] for k in sm], rows), _pack([var[k] for k in sm], rows), "adamw_small")
    for dst, buf in ((delta, d2), (new_m, m2), (new_v, v2)):
        for k, a in zip(sm, _unpack(buf, sm_shapes)):
            dst[k] = a

    return (loss, grad_x[None], *[grads[k] for k in ORDER], *[delta[k] for k in ORDER],
            *[new_m[k] for k in ORDER], *[new_v[k] for k in ORDER])
```

```python
import functools
import math

import numpy as np
import jax
import jax.numpy as jnp
from jax import lax
from jax.experimental import pallas as pl
from jax.experimental.pallas import tpu as pltpu

F32 = jnp.float32
BF16 = jnp.bfloat16

EPS = 1e-6
CHUNK = 128
SGU_GROUPS = 16
HEAD_DIM = 64
N_HEADS = 16
N_KV_HEADS = 4
KV_GROUP = N_HEADS // N_KV_HEADS
REL_BUCKETS = 32
REL_MAX_DIST = 128
LANES = 128
HALO = 16

ADAM_LR = 0.001
ADAM_B1 = 0.9
ADAM_B2 = 0.999
ADAM_EPS = 1e-08
ADAM_WD = 0.01
ADAM_STEP = 10

VMEM_LIMIT_V7X = 56 * 1024 * 1024

_SQRT_HALF = math.sqrt(0.5)
_INV_SQRT_2PI = 1.0 / math.sqrt(2.0 * math.pi)


def _cp(sem):
    return pltpu.CompilerParams(dimension_semantics=sem, vmem_limit_bytes=VMEM_LIMIT_V7X)


def _resident(shape):
    nd = len(shape)
    return pl.BlockSpec(shape, lambda *_: (0,) * nd, pipeline_mode=pl.Buffered(1))


class _Comm:
    def __init__(self, srcs, zones, news, sems, start, finish, middle=None):
        self.srcs, self.zones, self.news, self.sems = list(srcs), list(zones), list(news), list(sems)
        self.start, self.finish = start, finish
        self.middle = middle if middle is not None else (lambda srcs, zones, news, sems: None)


def _join(comms):
    comms = [c for c in comms if c is not None]
    if not comms:
        return None

    def part(seq, attr):
        out, k = [], 0
        for c in comms:
            n = len(getattr(c, attr))
            out.append(seq[k:k + n])
            k += n
        return out

    def run(which):
        def f(srcs, zones, news, sems):
            for c, a, b, d, e in zip(comms, part(srcs, "srcs"), part(zones, "zones"), part(news, "news"), part(sems, "sems")):
                getattr(c, which)(a, b, d, e)
        return f

    cat = lambda attr: [v for c in comms for v in getattr(c, attr)]
    return _Comm(cat("srcs"), cat("zones"), cat("news"), cat("sems"), run("start"), run("finish"), run("middle"))


def _split(comms, res):
    zones, news = res
    out, kz, kn = [], 0, 0
    for c in comms:
        out.append((zones[kz:kz + len(c.zones)], news[kn:kn + len(c.news)]))
        kz += len(c.zones)
        kn += len(c.news)
    return out


def _call(body, *, name, grid, in_specs, out_specs, out_shape, args, scratch_shapes=(), sem=None, comm=None):
    if comm is None:
        res = pl.pallas_call(body, name=name, grid=grid, in_specs=in_specs, out_specs=out_specs, out_shape=out_shape,
                             scratch_shapes=list(scratch_shapes), compiler_params=_cp(sem))(*args)
        return res, None
    single = not isinstance(out_shape, (list, tuple))
    out_specs_l = [out_specs] if single else list(out_specs)
    out_shape_l = [out_shape] if single else list(out_shape)
    n_in, n_out, n_scr = len(in_specs), len(out_shape_l), len(scratch_shapes)
    ns, nz, nn = len(comm.srcs), len(comm.zones), len(comm.news)

    def wrapped(*refs):
        k = n_in
        ins, srcs = refs[:k], refs[k:k + ns]
        k += ns + nz
        outs, zones, news = refs[k:k + n_out], refs[k + n_out:k + n_out + nz], refs[k + n_out + nz:k + n_out + nz + nn]
        k += n_out + nz + nn
        scr, sems = refs[k:k + n_scr], refs[k + n_scr:]
        if not grid:
            comm.start(srcs, zones, news, sems)
            body(*ins, *outs, *scr)
            comm.middle(srcs, zones, news, sems)
            comm.finish(srcs, zones, news, sems)
            return
        first = functools.reduce(jnp.logical_and, [pl.program_id(a) == 0 for a in range(len(grid))])
        last = functools.reduce(jnp.logical_and, [pl.program_id(a) == grid[a] - 1 for a in range(len(grid))])
        early = len(grid) == 1 and grid[0] >= 2
        mid_step = grid[0] - (2 if grid[0] >= 8 else 1)

        @pl.when(first)
        def _():
            comm.start(srcs, zones, news, sems)

        if early:
            @pl.when(pl.program_id(0) == mid_step)
            def _():
                comm.middle(srcs, zones, news, sems)

        body(*ins, *outs, *scr)

        @pl.when(last)
        def _():
            if not early:
                comm.middle(srcs, zones, news, sems)
            comm.finish(srcs, zones, news, sems)

    res = pl.pallas_call(
        wrapped, name=name, grid=grid,
        in_specs=list(in_specs) + [ANY] * (ns + nz), out_specs=out_specs_l + [ANY] * (nz + nn),
        out_shape=out_shape_l + [jax.ShapeDtypeStruct(z.shape, z.dtype) for z in comm.zones] + comm.news,
        input_output_aliases={n_in + ns + i: n_out + i for i in range(nz)},
        scratch_shapes=list(scratch_shapes) + comm.sems,
        compiler_params=_cp(("arbitrary",) * len(grid)),
    )(*args, *comm.srcs, *comm.zones)
    main = res[0] if single else list(res[:n_out])
    return main, (list(res[n_out:n_out + nz]), list(res[n_out + nz:]))


def _run_comm(comm, name):
    ns, nz, nn = len(comm.srcs), len(comm.zones), len(comm.news)

    def body(*refs):
        srcs, zones, news, sems = refs[:ns], refs[ns + nz:ns + 2 * nz], refs[ns + 2 * nz:ns + 2 * nz + nn], refs[ns + 2 * nz + nn:]
        comm.start(srcs, zones, news, sems)
        comm.middle(srcs, zones, news, sems)
        comm.finish(srcs, zones, news, sems)

    res = pl.pallas_call(
        body, name=name, in_specs=[ANY] * (ns + nz), out_specs=[ANY] * (nz + nn),
        out_shape=[jax.ShapeDtypeStruct(z.shape, z.dtype) for z in comm.zones] + comm.news,
        input_output_aliases={ns + i: i for i in range(nz)}, scratch_shapes=comm.sems,
    )(*comm.srcs, *comm.zones)
    return list(res[:nz]), list(res[nz:])


def _dot(a, b):
    return jnp.dot(a, b, preferred_element_type=F32)


def _dot_nt(a, b):
    return lax.dot_general(a, b, (((1,), (1,)), ((), ())), preferred_element_type=F32)


def _dot_tn(a, b):
    return lax.dot_general(a, b, (((0,), (0,)), ((), ())), preferred_element_type=F32)


def _normal_cdf(x):
    return 0.5 * (1.0 + lax.erf(x * _SQRT_HALF))


def _gelu_and_grad(x, cdf):
    return x * cdf, cdf + x * jnp.exp(-0.5 * x * x) * _INV_SQRT_2PI


def _sigmoid(x):
    return 0.5 * jnp.tanh(0.5 * x) + 0.5


def _rms_bwd(dy, x, gain):
    r = lax.rsqrt(jnp.mean(x * x, axis=-1, keepdims=True) + EPS)
    xhat = x * r
    gdy = dy * gain
    dx = r * (gdy - xhat * jnp.mean(gdy * xhat, axis=-1, keepdims=True))
    return dx, jnp.sum(dy * xhat, axis=0, keepdims=True)


def _norm_matmul(x, gain, w4, name, comm=None, with_cdf=False, conv=None, tm=1024):
    S, D = x.shape
    nsh, _, ns = w4.shape
    extra = with_cdf or conv is not None
    tm = min(512 if extra else tm, S)
    T = 16

    def body(x_ref, g_ref, w_ref, *rest):
        if conv is not None:
            cw_ref, cb_ref, hn_ref, o_ref, e_ref, tail = rest
        else:
            hn_ref, o_ref = rest[:2]
            e_ref = rest[2] if with_cdf else None
        xf = x_ref[...]
        r = lax.rsqrt(jnp.mean(xf * xf, axis=-1, keepdims=True) + EPS)
        hn = (xf * r * g_ref[...]).astype(BF16)
        hn_ref[...] = hn
        if conv is not None:
            @pl.when(pl.program_id(0) == 0)
            def _():
                tail[...] = jnp.zeros_like(tail)

        for j in range(nsh):
            cols = slice(j * ns, (j + 1) * ns)
            out = _dot(hn, w_ref[j])
            o_ref[:, cols] = out.astype(BF16)
            if with_cdf:
                e_ref[:, cols] = _normal_cdf(out).astype(BF16)
            if conv is not None:
                w0, w1, w2, b = cw_ref[0:1, cols], cw_ref[1:2, cols], cw_ref[2:3, cols], cb_ref[:, cols]
                taps = lambda v: w0 * pltpu.roll(v, 2, 0) + w1 * pltpu.roll(v, 1, 0) + w2 * v + b
                e_ref[:, cols] = taps(out).astype(BF16)
                head = jnp.concatenate([tail[:, cols], out[0:T]], axis=0)
                e_ref[0:T, cols] = taps(head)[T:].astype(BF16)
                tail[:, cols] = out[tm - T:tm]

    row = lambda width: pl.BlockSpec((tm, width), lambda i: (i, 0))
    N = nsh * ns
    in_specs = [row(D), _resident((1, D)), _resident(w4.shape)]
    args = (x, gain, w4)
    scratch = []
    if conv is not None:
        in_specs += [_resident((3, N)), _resident((1, N))]
        args += tuple(conv)
        scratch = [pltpu.VMEM((T, N), F32)]
    return _call(
        body, name=name, grid=(S // tm,), in_specs=in_specs,
        out_specs=[row(D), row(N)] + [row(N)] * extra,
        out_shape=[jax.ShapeDtypeStruct((S, D), BF16)] + [jax.ShapeDtypeStruct((S, N), BF16)] * (1 + extra),
        scratch_shapes=scratch, sem=("arbitrary",) if conv is not None else ("parallel",), args=args, comm=comm)


def _matmul_res(a, w, res, name, comm=None, tm=1024):
    S, K = a.shape
    N = w.shape[1]
    tm = min(tm, S)

    def body(a_ref, w_ref, r_ref, o_ref):
        o_ref[...] = r_ref[...] + _dot(a_ref[...], w_ref[...])

    return _call(
        body, name=name, grid=(S // tm,),
        in_specs=[pl.BlockSpec((tm, K), lambda i: (i, 0)), _resident(w.shape), pl.BlockSpec((tm, N), lambda i: (i, 0))],
        out_specs=pl.BlockSpec((tm, N), lambda i: (i, 0)),
        out_shape=jax.ShapeDtypeStruct((S, N), F32),
        sem=("parallel",), args=(a, w, res), comm=comm)


def _matmul_nt(dh, w, name, comm=None, tm=1024):
    S, N = dh.shape
    K = w.shape[0]
    tm = min(tm, S)

    def body(d_ref, w_ref, o_ref):
        o_ref[...] = _dot_nt(d_ref[...].astype(BF16), w_ref[...]).astype(BF16)

    return _call(
        body, name=name, grid=(S // tm,),
        in_specs=[pl.BlockSpec((tm, N), lambda i: (i, 0)), _resident(w.shape)],
        out_specs=pl.BlockSpec((tm, K), lambda i: (i, 0)),
        out_shape=jax.ShapeDtypeStruct((S, K), BF16),
        sem=("parallel",), args=(dh, w), comm=comm)


def _matmul_tn(a, b, name, *, ka, nb, out_dtype, comm=None, ts=1024):
    S, KA = a.shape
    NB = b.shape[1]
    ts = min(ts, S)
    J = max(KA // ka, NB // nb)
    a_map = (lambda j, s: (s, j)) if KA // ka > 1 else (lambda j, s: (s, 0))
    b_map = (lambda j, s: (s, j)) if NB // nb > 1 else (lambda j, s: (s, 0))
    last = S // ts - 1

    def body(a_ref, b_ref, o_ref, acc):
        s = pl.program_id(1)

        @pl.when(s == 0)
        def _():
            acc[...] = jnp.zeros_like(acc)

        acc[...] += _dot_tn(a_ref[...].astype(BF16), b_ref[...].astype(BF16))

        @pl.when(s == last)
        def _():
            o_ref[0] = acc[...].astype(out_dtype)

    return _call(
        body, name=name, grid=(J, S // ts),
        in_specs=[pl.BlockSpec((ts, ka), a_map), pl.BlockSpec((ts, nb), b_map)],
        out_specs=pl.BlockSpec((1, ka, nb), lambda j, s: (j, 0, 0)),
        out_shape=jax.ShapeDtypeStruct((J, ka, nb), out_dtype),
        scratch_shapes=[pltpu.VMEM((ka, nb), F32)],
        sem=("parallel", "arbitrary"), args=(a, b), comm=comm)


def _sgu_prep(w_s):
    G = w_s.shape[0]

    def body(w_ref, t_ref, tt_ref):
        tri = lax.broadcasted_iota(jnp.int32, (CHUNK, CHUNK), 0) >= lax.broadcasted_iota(jnp.int32, (CHUNK, CHUNK), 1)
        for g in range(G):
            t = jnp.where(tri, w_ref[g], 0.0)
            t_ref[g] = t.astype(BF16)
            tt_ref[g] = t.T.astype(BF16)

    return pl.pallas_call(
        body, name="sgu_prep",
        out_shape=[jax.ShapeDtypeStruct(w_s.shape, BF16), jax.ShapeDtypeStruct(w_s.shape, BF16)],
        compiler_params=_cp(None),
    )(w_s)


def _sgu_fwd(z, cdf, x, vg, wtril, bT, wout, comm=None, tm=512):
    S = z.shape[0]
    W = z.shape[1] // 2
    D = x.shape[1]
    tm = min(tm, S)

    def body(z_ref, cdf_ref, x_ref, vg_ref, wt_ref, bT_ref, wo_ref, yp_ref, h_ref):
        def chunk(c, carry):
            r0 = pl.multiple_of(c * CHUNK, CHUNK)
            zc = z_ref[pl.ds(r0, CHUNK), :].astype(F32)
            cdf = cdf_ref[pl.ds(r0, CHUNK), :].astype(F32)
            u = zc[:, :W] * cdf[:, :W]
            v = zc[:, W:] * cdf[:, W:]
            rv = lax.rsqrt(jnp.mean(v * v, axis=-1, keepdims=True) + EPS)
            vn = (v * rv * vg_ref[...]).astype(BF16)
            for g in range(SGU_GROUPS):
                sl = slice(g * LANES, (g + 1) * LANES)
                s = _dot(wt_ref[g], vn[:, sl]) + bT_ref[:, g:g + 1]
                yp_ref[pl.ds(r0, CHUNK), sl] = (u[:, sl] * s).astype(BF16)
            return carry

        lax.fori_loop(0, tm // CHUNK, chunk, 0)
        h_ref[...] = x_ref[...] + _dot(yp_ref[...], wo_ref[...])

    return _call(
        body, name="sgu_fwd", grid=(S // tm,),
        in_specs=[pl.BlockSpec((tm, 2 * W), lambda i: (i, 0)), pl.BlockSpec((tm, 2 * W), lambda i: (i, 0)),
                  pl.BlockSpec((tm, D), lambda i: (i, 0)),
                  _resident((1, W)), _resident(wtril.shape), _resident(bT.shape), _resident(wout.shape)],
        out_specs=[pl.BlockSpec((tm, W), lambda i: (i, 0)), pl.BlockSpec((tm, D), lambda i: (i, 0))],
        out_shape=[jax.ShapeDtypeStruct((S, W), BF16), jax.ShapeDtypeStruct((S, D), F32)],
        sem=("parallel",), args=(z, cdf, x, vg, wtril, bT, wout), comm=comm)


def _sgu_bwd(dh, z, cdf, vg, wtril, wtrilT, bT, wout, comm=None, tm=512):
    S = z.shape[0]
    W = z.shape[1] // 2
    D = dh.shape[1]
    G = SGU_GROUPS
    tm = min(tm, S)
    last = S // tm - 1

    def body(dh_ref, z_ref, cdf_ref, vg_ref, wt_ref, wtT_ref, bT_ref, wo_ref,
             dz_ref, dws_ref, dbT_ref, dvg_ref, dyp_s, du_s, dvn_s, dsacc):
        i = pl.program_id(0)

        @pl.when(i == 0)
        def _():
            dws_ref[...] = jnp.zeros_like(dws_ref)
            dvg_ref[...] = jnp.zeros_like(dvg_ref)
            dsacc[...] = jnp.zeros_like(dsacc)

        dyp_s[...] = _dot_nt(dh_ref[...].astype(BF16), wo_ref[...])
        tri = lax.broadcasted_iota(jnp.int32, (CHUNK, CHUNK), 0) >= lax.broadcasted_iota(jnp.int32, (CHUNK, CHUNK), 1)

        def chunk(c, carry):
            r0 = pl.multiple_of(c * CHUNK, CHUNK)
            zc = z_ref[pl.ds(r0, CHUNK), :].astype(F32)
            cdf = cdf_ref[pl.ds(r0, CHUNK), :].astype(F32)
            u, gu = _gelu_and_grad(zc[:, :W], cdf[:, :W])
            v, gv = _gelu_and_grad(zc[:, W:], cdf[:, W:])
            rv = lax.rsqrt(jnp.mean(v * v, axis=-1, keepdims=True) + EPS)
            vhat = v * rv
            vgain = vg_ref[...]
            vn = (vhat * vgain).astype(BF16)
            dyp = dyp_s[pl.ds(r0, CHUNK), :]
            for g in range(G):
                sl = slice(g * LANES, (g + 1) * LANES)
                vng = vn[:, sl]
                s = _dot(wt_ref[g], vng) + bT_ref[:, g:g + 1]
                ds = dyp[:, sl] * u[:, sl]
                du_s[:, sl] = dyp[:, sl] * s
                dsb = ds.astype(BF16)
                dvn_s[:, sl] = _dot(wtT_ref[g], dsb)
                dws_ref[g] += jnp.where(tri, _dot_nt(dsb, vng), 0.0)
                dsacc[g] += ds
            dvn = dvn_s[...]
            dvg_ref[...] += jnp.sum(dvn * vhat, axis=0, keepdims=True)
            gdy = dvn * vgain
            dv = rv * (gdy - vhat * jnp.mean(gdy * vhat, axis=-1, keepdims=True))
            dz_ref[pl.ds(r0, CHUNK), :W] = (du_s[...] * gu).astype(BF16)
            dz_ref[pl.ds(r0, CHUNK), W:] = (dv * gv).astype(BF16)
            return carry

        lax.fori_loop(0, tm // CHUNK, chunk, 0)

        @pl.when(i == last)
        def _():
            for g in range(G):
                dbT_ref[:, g:g + 1] = jnp.sum(dsacc[g], axis=1, keepdims=True)

    return _call(
        body, name="sgu_bwd", grid=(S // tm,),
        in_specs=[pl.BlockSpec((tm, D), lambda i: (i, 0)), pl.BlockSpec((tm, 2 * W), lambda i: (i, 0)),
                  pl.BlockSpec((tm, 2 * W), lambda i: (i, 0)),
                  _resident((1, W)), _resident(wtril.shape), _resident(wtrilT.shape), _resident(bT.shape),
                  _resident(wout.shape)],
        out_specs=[pl.BlockSpec((tm, 2 * W), lambda i: (i, 0)),
                   pl.BlockSpec((G, CHUNK, CHUNK), lambda i: (0, 0, 0)),
                   pl.BlockSpec((CHUNK, G), lambda i: (0, 0)),
                   pl.BlockSpec((1, W), lambda i: (0, 0))],
        out_shape=[jax.ShapeDtypeStruct((S, 2 * W), BF16), jax.ShapeDtypeStruct((G, CHUNK, CHUNK), F32),
                   jax.ShapeDtypeStruct((CHUNK, G), F32), jax.ShapeDtypeStruct((1, W), F32)],
        scratch_shapes=[pltpu.VMEM((tm, W), F32), pltpu.VMEM((CHUNK, W), F32), pltpu.VMEM((CHUNK, W), F32),
                        pltpu.VMEM((G, CHUNK, CHUNK), F32)],
        sem=("arbitrary",), args=(dh, z, cdf, vg, wtril, wtrilT, bT, wout), comm=comm)


def _ffn_fwd(cp, h_in, wdown, name, comm=None, target=None, tm=512, R=128):
    S, C = cp.shape
    F = C // 2
    D = h_in.shape[1]
    tm = min(tm, S)
    head = target is not None

    def body(cp_ref, h_ref, wd_ref, *rest):
        if head:
            t_ref, f_ref, dy_ref, l_ref = rest
        else:
            f_ref, ho_ref = rest
        for r0 in range(0, tm, R):
            g = cp_ref[r0:r0 + R, :F].astype(F32)
            f_ref[r0:r0 + R, :] = (g * _sigmoid(g) * cp_ref[r0:r0 + R, F:].astype(F32)).astype(BF16)
        h_out = h_ref[...] + _dot(f_ref[...], wd_ref[...])
        if not head:
            ho_ref[...] = h_out
            return

        @pl.when(pl.program_id(0) == 0)
        def _():
            l_ref[...] = jnp.zeros_like(l_ref)

        e = h_out - t_ref[...]
        dy_ref[...] = e * (1.0 / D)
        rows = jnp.sum(e * e, axis=-1, keepdims=True) * (1.0 / D)
        l_ref[...] += 0.5 * jnp.sum(rows, axis=0, keepdims=True)

    row = pl.BlockSpec((tm, D), lambda i: (i, 0))
    in_specs = [pl.BlockSpec((tm, C), lambda i: (i, 0)), row, _resident(wdown.shape)]
    out_specs = [pl.BlockSpec((tm, F), lambda i: (i, 0)), row]
    out_shape = [jax.ShapeDtypeStruct((S, F), BF16), jax.ShapeDtypeStruct((S, D), F32)]
    args = (cp, h_in, wdown)
    if head:
        in_specs.append(row)
        out_specs.append(pl.BlockSpec((1, 1), lambda i: (0, 0)))
        out_shape.append(jax.ShapeDtypeStruct((1, 1), F32))
        args += (target,)
    return _call(
        body, name=name, grid=(S // tm,), in_specs=in_specs, out_specs=out_specs, out_shape=out_shape,
        sem=("arbitrary",) if head else ("parallel",), args=args, comm=comm)


def _rows8(v):
    return functools.reduce(jnp.add, [v[8 * k:8 * k + 8] for k in range(v.shape[0] // 8)])


def _ffn_bwd_dc(dh, cp, wdown, name, comm=None, tm=512, R=64):
    S, C = cp.shape
    F = C // 2
    D = dh.shape[1]
    tm = min(tm, S)

    def body(dh_ref, cp_ref, wd_ref, dc_ref, dcb_ref, df_s, acc):
        i = pl.program_id(0)

        @pl.when(i == 0)
        def _():
            acc[...] = jnp.zeros_like(acc)

        df_s[...] = _dot_nt(dh_ref[...].astype(BF16), wd_ref[...])
        for r0 in range(0, tm, R):
            cpre = cp_ref[r0:r0 + R, :].astype(F32)
            g = cpre[:, :F]
            val = cpre[:, F:]
            sg = _sigmoid(g)
            df = df_s[r0:r0 + R, :]
            dg = df * val * (sg * (1.0 + g * (1.0 - sg)))
            dval = df * (g * sg)
            dc = jnp.concatenate([dg, dval], axis=1)
            dc_ref[r0:r0 + R, :] = dc.astype(BF16)
            acc[...] += _rows8(dc)

        @pl.when(i == S // tm - 1)
        def _():
            dcb_ref[...] = jnp.sum(acc[...], axis=0, keepdims=True)

    return _call(
        body, name=name, grid=(S // tm,),
        in_specs=[pl.BlockSpec((tm, D), lambda i: (i, 0)), pl.BlockSpec((tm, C), lambda i: (i, 0)),
                  _resident(wdown.shape)],
        out_specs=[pl.BlockSpec((tm, C), lambda i: (i, 0)), pl.BlockSpec((1, C), lambda i: (0, 0))],
        out_shape=[jax.ShapeDtypeStruct((S, C), BF16), jax.ShapeDtypeStruct((1, C), F32)],
        scratch_shapes=[pltpu.VMEM((tm, F), F32), pltpu.VMEM((8, C), F32)],
        sem=("arbitrary",), args=(dh, cp, wdown), comm=comm)


def _bwd_norm(dA, w4, h_in, gain, dh_out, name, conv_w=None, conv_in=None, comm=None, R=64):
    S, N = dA.shape
    nsh, D, ns = w4.shape
    conv = conv_w is not None
    tm = min(256 if conv else 512, S)
    nt = S // tm

    def finish(src_ref, w_ref, h_ref, g_ref, dho_ref, dhi_ref, dg_ref):
        dhn = _dot_nt(src_ref[:, 0:ns], w_ref[0])
        for j in range(1, nsh):
            dhn += _dot_nt(src_ref[:, j * ns:(j + 1) * ns], w_ref[j])
        dx, dgain = _rms_bwd(dhn, h_ref[...], g_ref[...])
        dg_ref[...] += dgain
        dhi_ref[...] = dho_ref[...] + dx

    def body_plain(dA_ref, w_ref, h_ref, g_ref, dho_ref, dhi_ref, dg_ref):
        @pl.when(pl.program_id(0) == 0)
        def _():
            dg_ref[...] = jnp.zeros_like(dg_ref)

        finish(dA_ref, w_ref, h_ref, g_ref, dho_ref, dhi_ref, dg_ref)

    def body_conv(dc_ref, halo_ref, cw_ref, a_ref, w_ref, h_ref, g_ref, dho_ref, dhi_ref, dg_ref, da_ref, dcw_ref,
                  dc32, acc):
        i = pl.program_id(0)

        @pl.when(i == 0)
        def _():
            dg_ref[...] = jnp.zeros_like(dg_ref)
            acc[...] = jnp.zeros_like(acc)

        dc32[0:tm, :] = dc_ref[...].astype(F32)
        dc32[tm:, :] = jnp.where(i < nt - 1, halo_ref[...].astype(F32), 0.0)

        for r0 in range(0, tm, R):
            X = dc32[r0:r0 + R + 8, :]
            d0 = X[:R]
            d1 = pltpu.roll(X, R + 7, 0)[:R]
            d2 = pltpu.roll(X, R + 6, 0)[:R]
            da = cw_ref[2:3, :] * d0 + cw_ref[1:2, :] * d1 + cw_ref[0:1, :] * d2
            da_ref[r0:r0 + R, :] = da.astype(BF16)
            a0 = a_ref[r0:r0 + R, :].astype(F32)
            acc[0] += _rows8(d2 * a0)
            acc[1] += _rows8(d1 * a0)
            acc[2] += _rows8(d0 * a0)
        finish(da_ref, w_ref, h_ref, g_ref, dho_ref, dhi_ref, dg_ref)

        @pl.when(i == nt - 1)
        def _():
            for k in range(3):
                dcw_ref[k:k + 1, :] = jnp.sum(acc[k], axis=0, keepdims=True)

    row = lambda width: pl.BlockSpec((tm, width), lambda i: (i, 0))
    common_in = [_resident(w4.shape), row(D), _resident((1, D)), row(D)]
    common_out = [row(D), pl.BlockSpec((1, D), lambda i: (0, 0))]
    common_shape = [jax.ShapeDtypeStruct((S, D), F32), jax.ShapeDtypeStruct((1, D), F32)]
    if not conv:
        return _call(
            body_plain, name=name, grid=(nt,),
            in_specs=[row(N)] + common_in, out_specs=common_out, out_shape=common_shape,
            sem=("arbitrary",), args=(dA, w4, h_in, gain, dh_out), comm=comm)
    hb = tm // HALO
    nhb = S // HALO
    return _call(
        body_conv, name=name, grid=(nt,),
        in_specs=[row(N), pl.BlockSpec((HALO, N), lambda i: (jnp.minimum((i + 1) * hb, nhb - 1), 0)),
                  _resident((3, N)), row(N)] + common_in,
        out_specs=common_out + [row(N), pl.BlockSpec((3, N), lambda i: (0, 0))],
        out_shape=common_shape + [jax.ShapeDtypeStruct((S, N), BF16), jax.ShapeDtypeStruct((3, N), F32)],
        scratch_shapes=[pltpu.VMEM((tm + HALO, N), F32), pltpu.VMEM((3, 8, N), F32)],
        sem=("arbitrary",), args=(dA, dA, conv_w, conv_in, w4, h_in, gain, dh_out), comm=comm)


def _rel_buckets_flat():
    q = np.arange(CHUNK)[:, None] + CHUNK
    k = np.arange(2 * CHUNK)[None, :]
    n = np.maximum(q - k, 0)
    max_exact = REL_BUCKETS // 2
    large = max_exact + (np.log(np.maximum(n, 1).astype(np.float32) / max_exact)
                         / math.log(REL_MAX_DIST / max_exact) * (REL_BUCKETS - max_exact)).astype(np.int32)
    large = np.minimum(large, REL_BUCKETS - 1)
    return np.where(n < max_exact, n, large).astype(np.int32).reshape(1, CHUNK * 2 * CHUNK)


def _split_bf16(x):
    hi = x.astype(BF16)
    return hi, (x - hi.astype(F32)).astype(BF16)


def _rel_bias_expand(rel_bias, bucket):
    B, H = rel_bias.shape
    n = bucket.shape[1]

    def body(rb_ref, bk_ref, o_ref):
        oh = (bk_ref[...] == lax.broadcasted_iota(jnp.int32, (B, n), 0)).astype(BF16)
        hi, lo = _split_bf16(rb_ref[...])
        o_ref[...] = _dot_tn(hi, oh) + _dot_tn(lo, oh)

    return pl.pallas_call(body, name="rel_bias_expand", out_shape=jax.ShapeDtypeStruct((H, n), F32),
                          compiler_params=_cp(None))(rel_bias, bucket)


def _rel_bias_reduce(dbias, bucket):
    H, n = dbias.shape
    B = REL_BUCKETS

    def body(db_ref, bk_ref, o_ref):
        oh = (bk_ref[...] == lax.broadcasted_iota(jnp.int32, (B, n), 0)).astype(BF16)
        hi, lo = _split_bf16(db_ref[...])
        o_ref[...] = _dot_nt(oh, hi) + _dot_nt(oh, lo)

    return pl.pallas_call(body, name="rel_bias_reduce", out_shape=jax.ShapeDtypeStruct((B, H), F32),
                          compiler_params=_cp(None))(dbias, bucket)


def _lo_mask(rows):
    return lax.broadcasted_iota(jnp.int32, (rows, LANES), 1) < HEAD_DIM


def _half_sums(y, lo):
    s_lo = jnp.sum(jnp.where(lo, y, 0.0), axis=-1, keepdims=True)
    s_hi = jnp.sum(jnp.where(lo, 0.0, y), axis=-1, keepdims=True)
    return jnp.where(lo, s_lo, s_hi)


def _half_rms(x, gain, lo):
    r = lax.rsqrt(_half_sums(x * x, lo) * (1.0 / HEAD_DIM) + EPS)
    xhat = x * r
    return xhat * gain, xhat, r


def _half_rms_bwd(dy, xhat, r, gain, lo):
    gdy = dy * gain
    dx = r * (gdy - xhat * (_half_sums(gdy * xhat, lo) * (1.0 / HEAD_DIM)))
    return dx, jnp.sum(dy * xhat, axis=0, keepdims=True)


def _dup_half(pair, e, lo):
    sw = pltpu.roll(pair, HEAD_DIM, 1)
    return jnp.where(lo, pair, sw) if e == 0 else jnp.where(lo, sw, pair)


def _band_valid(n):
    qi = lax.broadcasted_iota(jnp.int32, (KV_GROUP * CHUNK, 2 * CHUNK), 0) & (CHUNK - 1)
    ki = lax.broadcasted_iota(jnp.int32, (KV_GROUP * CHUNK, 2 * CHUNK), 1)
    dist = qi + CHUNK - ki
    return (dist >= 0) & (dist < CHUNK) & ((n > 0) | (ki >= CHUNK))


def _stack_heads(a, b, lo):
    return jnp.concatenate([jnp.where(lo, a, 0.0), jnp.where(lo, 0.0, a), jnp.where(lo, b, 0.0), jnp.where(lo, 0.0, b)],
                           axis=0)


def _unstack_heads(x4, lo):
    return (jnp.where(lo, x4[0:CHUNK], x4[CHUNK:2 * CHUNK]),
            jnp.where(lo, x4[2 * CHUNK:3 * CHUNK], x4[3 * CHUNK:]))


def _sink_col(sink_ref, hk):
    row = lax.broadcasted_iota(jnp.int32, (KV_GROUP * CHUNK, 1), 0)
    col = jnp.full((KV_GROUP * CHUNK, 1), sink_ref[KV_GROUP * hk + KV_GROUP - 1], F32)
    for r in range(KV_GROUP - 2, -1, -1):
        col = jnp.where(row < (r + 1) * CHUNK, sink_ref[KV_GROUP * hk + r], col)
    return col


def _softmax_sink(s, valid, sink):
    s = jnp.where(valid, s, -jnp.inf)
    m = jnp.maximum(jnp.max(s, axis=-1, keepdims=True), sink)
    p = jnp.exp(s - m)
    esink = jnp.exp(sink - m)
    inv = 1.0 / (jnp.sum(p, axis=-1, keepdims=True) + esink)
    return p * inv, esink * inv


QW = N_HEADS * HEAD_DIM
KVW = N_KV_HEADS * HEAD_DIM


def _attn_fwd(qkv, qg2, kg2, sinks, bias, comm=None):
    S = qkv.shape[0]
    nb = S // CHUNK

    def body(cur_ref, prev_ref, qg_ref, kg_ref, sink_ref, bias_ref, o_ref, p_ref, ps_ref):
        n = pl.program_id(0)
        lo = _lo_mask(CHUNK)
        lo2 = _lo_mask(2 * CHUNK)
        valid = _band_valid(n)
        for j in range(N_KV_HEADS // 2):
            kc = slice(QW + j * LANES, QW + (j + 1) * LANES)
            vc = slice(QW + KVW + j * LANES, QW + KVW + (j + 1) * LANES)
            kpair = jnp.concatenate([prev_ref[:, j * LANES:(j + 1) * LANES], cur_ref[:, kc]], axis=0).astype(F32)
            vpair = jnp.concatenate([prev_ref[:, KVW + j * LANES:KVW + (j + 1) * LANES], cur_ref[:, vc]], axis=0).astype(F32)
            knpair, _, _ = _half_rms(kpair, kg_ref[...], lo2)
            for e in range(2):
                hk = 2 * j + e
                kdup = _dup_half(knpair, e, lo2).astype(BF16)
                vdup = _dup_half(vpair, e, lo2).astype(BF16)
                ca = slice(2 * hk * LANES, (2 * hk + 1) * LANES)
                cb = slice((2 * hk + 1) * LANES, (2 * hk + 2) * LANES)
                qna, _, _ = _half_rms(cur_ref[:, ca].astype(F32), qg_ref[...], lo)
                qnb, _, _ = _half_rms(cur_ref[:, cb].astype(F32), qg_ref[...], lo)
                qm4 = _stack_heads(qna, qnb, lo).astype(BF16)
                s = _dot_nt(qm4, kdup) * (HEAD_DIM ** -0.5) + bias_ref[hk]
                p, psink = _softmax_sink(s, valid, _sink_col(sink_ref, hk))
                pb = p.astype(BF16)
                p_ref[0, hk] = pb
                for r in range(KV_GROUP):
                    h = KV_GROUP * hk + r
                    ps_ref[:, h:h + 1] = psink[r * CHUNK:(r + 1) * CHUNK]
                oa, ob = _unstack_heads(_dot(pb, vdup), lo)
                o_ref[:, ca] = oa.astype(BF16)
                o_ref[:, cb] = ob.astype(BF16)

    pshape = (nb, N_KV_HEADS, KV_GROUP * CHUNK, 2 * CHUNK)
    return _call(
        body, name="attn_fwd", grid=(nb,),
        in_specs=[pl.BlockSpec((CHUNK, QW + 2 * KVW), lambda n: (n, 0)),
                  pl.BlockSpec((CHUNK, 2 * KVW), lambda n: (jnp.maximum(n - 1, 0), QW // (2 * KVW))),
                  _resident((1, LANES)), _resident((1, LANES)),
                  pl.BlockSpec(memory_space=pltpu.SMEM),
                  _resident(bias.shape)],
        out_specs=[pl.BlockSpec((CHUNK, QW), lambda n: (n, 0)),
                   pl.BlockSpec((1,) + pshape[1:], lambda n: (n, 0, 0, 0)),
                   pl.BlockSpec((CHUNK, LANES), lambda n: (n, 0))],
        out_shape=[jax.ShapeDtypeStruct((S, QW), BF16), jax.ShapeDtypeStruct(pshape, BF16),
                   jax.ShapeDtypeStruct((S, LANES), F32)],
        sem=("parallel",), args=(qkv, qkv, qg2, kg2, sinks, bias), comm=comm)


def _attn_bwd(qkv, do, probs, psinks, qg2, kg2, comm=None):
    S = qkv.shape[0]
    nb = S // CHUNK

    def body(cur_ref, prev_ref, do_ref, p_ref, ps_ref, qg_ref, kg_ref,
             dqkv_ref, dbias_ref, dqg_ref, dkg_ref, dsink_ref, carry, band, dsacc, gacc):
        i = pl.program_id(0)
        lo = _lo_mask(CHUNK)
        lo2 = _lo_mask(2 * CHUNK)
        lane = lax.broadcasted_iota(jnp.int32, (KV_GROUP * CHUNK, LANES), 1)

        @pl.when(i == 0)
        def _():
            dbias_ref[...] = jnp.zeros_like(dbias_ref)
            carry[...] = jnp.zeros_like(carry)
            dsacc[...] = jnp.zeros_like(dsacc)
            gacc[...] = jnp.zeros_like(gacc)

        qgain = qg_ref[...]
        kgain = kg_ref[...]
        for j in range(N_KV_HEADS // 2):
            kc = slice(QW + j * LANES, QW + (j + 1) * LANES)
            vc = slice(QW + KVW + j * LANES, QW + KVW + (j + 1) * LANES)
            kpair = jnp.concatenate([prev_ref[:, j * LANES:(j + 1) * LANES], cur_ref[:, kc]], axis=0).astype(F32)
            vpair = jnp.concatenate([prev_ref[:, KVW + j * LANES:KVW + (j + 1) * LANES], cur_ref[:, vc]], axis=0).astype(F32)
            knpair, khat, kr = _half_rms(kpair, kgain, lo2)
            dk_folds = []
            dv_folds = []
            for e in range(2):
                hk = 2 * j + e
                kdup = _dup_half(knpair, e, lo2).astype(BF16)
                vdup = _dup_half(vpair, e, lo2).astype(BF16)
                ca = slice(2 * hk * LANES, (2 * hk + 1) * LANES)
                cb = slice((2 * hk + 1) * LANES, (2 * hk + 2) * LANES)
                qna, qhata, qra = _half_rms(cur_ref[:, ca].astype(F32), qgain, lo)
                qnb, qhatb, qrb = _half_rms(cur_ref[:, cb].astype(F32), qgain, lo)
                qm4 = _stack_heads(qna, qnb, lo).astype(BF16)
                dom4 = _stack_heads(do_ref[:, ca].astype(F32), do_ref[:, cb].astype(F32), lo).astype(BF16)
                pb = p_ref[0, hk]
                p = pb.astype(F32)
                psink = jnp.concatenate([ps_ref[:, KV_GROUP * hk + r:KV_GROUP * hk + r + 1] for r in range(KV_GROUP)],
                                        axis=0)
                dp = _dot_nt(dom4, vdup)
                delta = jnp.sum(p * dp, axis=-1, keepdims=True)
                ds = p * (dp - delta)
                dbias_ref[hk] += ds
                dsacc[...] += jnp.where(lane == hk, -(psink * delta), 0.0)
                dsr = (ds * (HEAD_DIM ** -0.5)).astype(BF16)
                dqna, dqnb = _unstack_heads(_dot(dsr, kdup), lo)
                dkd = _dot_tn(dsr, qm4)
                dvd = _dot_tn(pb, dom4)
                dqa, dqga = _half_rms_bwd(dqna, qhata, qra, qgain, lo)
                dqb, dqgb = _half_rms_bwd(dqnb, qhatb, qrb, qgain, lo)
                gacc[0:1, :] += dqga + dqgb
                dqkv_ref[:, ca] = dqa.astype(BF16)
                dqkv_ref[:, cb] = dqb.astype(BF16)
                dk_folds.append(dkd + pltpu.roll(dkd, HEAD_DIM, 1))
                dv_folds.append(dvd + pltpu.roll(dvd, HEAD_DIM, 1))
            dkn = jnp.where(lo2, dk_folds[0], dk_folds[1])
            dk, dkg = _half_rms_bwd(dkn, khat, kr, kgain, lo2)
            gacc[1:2, :] += dkg
            band[:, j * LANES:(j + 1) * LANES] = dk
            band[:, KVW + j * LANES:KVW + (j + 1) * LANES] = jnp.where(lo2, dv_folds[0], dv_folds[1])
        dqkv_ref[:, QW:] = (band[CHUNK:, :] + carry[...]).astype(BF16)
        carry[...] = band[0:CHUNK, :]

        @pl.when(i == nb - 1)
        def _():
            g = gacc[...]
            g = g + pltpu.roll(g, HEAD_DIM, 1)
            dqg_ref[...] = g[0:1, :]
            dkg_ref[...] = g[1:2, :]
            for r in range(KV_GROUP):
                dsink_ref[r:r + 1, :] = jnp.sum(dsacc[r * CHUNK:(r + 1) * CHUNK, :], axis=0, keepdims=True)

    vec = pl.BlockSpec((1, LANES), lambda i: (0, 0))
    bshape = probs.shape[1:]
    return _call(
        body, name="attn_bwd", grid=(nb,),
        in_specs=[pl.BlockSpec((CHUNK, QW + 2 * KVW), lambda i: (nb - 1 - i, 0)),
                  pl.BlockSpec((CHUNK, 2 * KVW), lambda i: (jnp.maximum(nb - 2 - i, 0), QW // (2 * KVW))),
                  pl.BlockSpec((CHUNK, QW), lambda i: (nb - 1 - i, 0)),
                  pl.BlockSpec((1,) + bshape, lambda i: (nb - 1 - i, 0, 0, 0)),
                  pl.BlockSpec((CHUNK, LANES), lambda i: (nb - 1 - i, 0)),
                  _resident((1, LANES)), _resident((1, LANES))],
        out_specs=[pl.BlockSpec((CHUNK, QW + 2 * KVW), lambda i: (nb - 1 - i, 0)),
                   pl.BlockSpec(bshape, lambda i: (0, 0, 0)), vec, vec,
                   pl.BlockSpec((KV_GROUP, LANES), lambda i: (0, 0))],
        out_shape=[jax.ShapeDtypeStruct((S, QW + 2 * KVW), BF16),
                   jax.ShapeDtypeStruct(bshape, F32),
                   jax.ShapeDtypeStruct((1, LANES), F32), jax.ShapeDtypeStruct((1, LANES), F32),
                   jax.ShapeDtypeStruct((KV_GROUP, LANES), F32)],
        scratch_shapes=[pltpu.VMEM((CHUNK, 2 * KVW), F32), pltpu.VMEM((2 * CHUNK, 2 * KVW), F32),
                        pltpu.VMEM((KV_GROUP * CHUNK, LANES), F32), pltpu.VMEM((8, LANES), F32)],
        sem=("arbitrary",), args=(qkv, qkv, do, probs, psinks, qg2, kg2), comm=comm)


def _row_tile(rows, cols, n_arrays):
    budget = VMEM_LIMIT_V7X // 4 // (2 * n_arrays * 4 * cols)
    best = 8
    for n in range(1, rows // 8 + 1):
        if rows % n == 0 and (rows // n) % 8 == 0 and rows // n <= budget:
            best = rows // n
            break
    return best


def _adamw(g, w, m, v, name, comm=None):
    R, C = g.shape
    tr = _row_tile(R, C, 8)

    def body(g_ref, w_ref, m_ref, v_ref, d_ref, mo_ref, vo_ref, go_ref):
        gg = g_ref[...]
        go_ref[...] = gg
        mn = ADAM_B1 * m_ref[...] + (1.0 - ADAM_B1) * gg
        vn = ADAM_B2 * v_ref[...] + (1.0 - ADAM_B2) * jnp.square(gg)
        m_hat = mn / (1.0 - ADAM_B1 ** ADAM_STEP)
        v_hat = vn / (1.0 - ADAM_B2 ** ADAM_STEP)
        d_ref[...] = -ADAM_LR * (m_hat / (jnp.sqrt(v_hat) + ADAM_EPS) + ADAM_WD * w_ref[...])
        mo_ref[...] = mn
        vo_ref[...] = vn

    spec = pl.BlockSpec((tr, C), lambda i: (i, 0))
    return _call(
        body, name=name, grid=(R // tr,), in_specs=[spec] * 4, out_specs=[spec] * 4,
        out_shape=[jax.ShapeDtypeStruct((R, C), F32)] * 4, sem=("parallel",), args=(g, w, m, v), comm=comm)


def _place_shard(shards, layer, where, dtype, name):
    _, R, C = shards.shape
    tr = _row_tile(R, C, 2) if R % 8 == 0 else R

    def body(s_ref, x_ref, o_ref):
        o_ref[...] = x_ref[...].astype(dtype)

    return pl.pallas_call(
        body, name=name,
        grid_spec=pltpu.PrefetchScalarGridSpec(
            num_scalar_prefetch=1, grid=(R // tr,),
            in_specs=[pl.BlockSpec((1, tr, C), lambda i, s_ref: (layer, i, 0))],
            out_specs=pl.BlockSpec((1, tr, C), lambda i, s_ref: (s_ref[1], i, 0))),
        out_shape=jax.ShapeDtypeStruct((4, R, C), dtype),
        compiler_params=_cp(("parallel",)),
    )(where, shards)


def _pair_add(g4, rsib, where, name):
    J, R, C = g4.shape
    Rh = R // 2
    tr = _row_tile(Rh, C, 4)
    g5 = g4.reshape(J, 2, Rh, C)

    def body(s_ref, g_ref, r_ref, p_ref, q_ref):
        val = (g_ref[...].astype(F32)[0] + r_ref[...].astype(F32)).astype(BF16)
        p_ref[...] = val

        @pl.when(pl.program_id(1) == s_ref[1])
        def _():
            q_ref[...] = val

    return pl.pallas_call(
        body, name=name,
        grid_spec=pltpu.PrefetchScalarGridSpec(
            num_scalar_prefetch=1, grid=(Rh // tr, J),
            in_specs=[pl.BlockSpec((1, 1, tr, C), lambda i, j, s_ref: (j, s_ref[0], i, 0)),
                      pl.BlockSpec((1, tr, C), lambda i, j, s_ref: (j, i, 0))],
            out_specs=[pl.BlockSpec((1, tr, C), lambda i, j, s_ref: (j, i, 0)),
                       pl.BlockSpec((1, tr, C), lambda i, j, s_ref: (s_ref[1], i, 0))]),
        out_shape=[jax.ShapeDtypeStruct((J, Rh, C), BF16)] * 2,
        compiler_params=_cp(("parallel", "arbitrary")),
    )(where, g5, rsib)


def _sum_chips(q, where, dest, layer, out_shape, name):
    J, Rh, C = q.shape
    tr = _row_tile(Rh, C, 3)
    nb = Rh // tr

    def body(s_ref, q_ref, *rest):
        qq = q_ref[...].astype(F32)
        rest[-1][0] = ((qq[0] + qq[1]) + qq[2]) + qq[3]

    have = dest is not None
    return pl.pallas_call(
        body, name=name,
        grid_spec=pltpu.PrefetchScalarGridSpec(
            num_scalar_prefetch=1, grid=(nb,),
            in_specs=[pl.BlockSpec((J, tr, C), lambda i, s_ref: (0, i, 0))] + ([ANY] if have else []),
            out_specs=pl.BlockSpec((1, tr, C), lambda i, s_ref: (layer, s_ref[0] * nb + i, 0))),
        out_shape=jax.ShapeDtypeStruct(out_shape, F32),
        input_output_aliases={2: 0} if have else {},
        compiler_params=_cp(("parallel",)),
    )(*((where, q, dest) if have else (where, q)))


MESH = pl.DeviceIdType.MESH
ANY = pl.BlockSpec(memory_space=pl.ANY)


def _place():
    x, y, c = lax.axis_index("x"), lax.axis_index("y"), lax.axis_index("c")
    others = [(1 - x, y), (x, 1 - y), (1 - x, 1 - y)]
    return x, y, c, 2 * x + y, others, [2 * ox + oy for ox, oy in others]


def _gather_comm(items):
    n = len(items)
    placed = [it[0] for it in items]
    split = [it[3] for it in items]

    def rows(t, ref, half):
        _, lo, hi, _ = items[t]
        if not split[t]:
            return ref if (lo, hi) == (0, placed[t].shape[1]) else ref.at[pl.ds(lo, hi - lo), :]
        rh = (hi - lo) // 2
        return ref.at[pl.ds(lo + half * rh, rh), :]

    def sends(outs, sems):
        send, recv = sems[0], sems[1]
        x, y, c, me, others, okey = _place()
        cps = []
        for t in range(n):
            mine = rows(t, outs[t].at[me], c)
            for j, (ox, oy) in enumerate(others):
                cps.append(pltpu.make_async_remote_copy(
                    src_ref=mine, dst_ref=mine,
                    send_sem=send.at[t, j], recv_sem=recv.at[t, j], device_id=(ox, oy, c), device_id_type=MESH))
        return cps

    def start(srcs, outs, news, sems):
        for cp in sends(outs, sems):
            cp.start()

    def forwards(outs, sems):
        fsend, frecv = sems[2], sems[3]
        x, y, c, me, others, okey = _place()
        cps = []
        for t in range(n):
            if split[t]:
                for j in range(3):
                    landed = rows(t, outs[t].at[okey[j]], c)
                    cps.append(pltpu.make_async_remote_copy(
                        src_ref=landed, dst_ref=landed, send_sem=fsend.at[t, j], recv_sem=frecv.at[t, j],
                        device_id=(x, y, 1 - c), device_id_type=MESH))
        return cps

    def middle(srcs, outs, news, sems):
        send, recv = sems[0], sems[1]
        x, y, c, me, others, okey = _place()
        for t in range(n):
            for j in range(3):
                landed = rows(t, outs[t].at[okey[j]], c)
                pltpu.make_async_remote_copy(
                    src_ref=landed, dst_ref=landed, send_sem=send.at[t, j], recv_sem=recv.at[t, j],
                    device_id=(x, y, 1 - c), device_id_type=MESH).wait_recv()
        for cp in forwards(outs, sems):
            cp.start()

    def finish(srcs, outs, news, sems):
        fsend, frecv = sems[2], sems[3]
        x, y, c, me, others, okey = _place()
        for t in range(n):
            if split[t]:
                for j in range(3):
                    theirs = rows(t, outs[t].at[okey[j]], 1 - c)
                    pltpu.make_async_remote_copy(
                        src_ref=theirs, dst_ref=theirs, send_sem=fsend.at[t, j], recv_sem=frecv.at[t, j],
                        device_id=(x, y, 1 - c), device_id_type=MESH).wait_recv()
        for cp in sends(outs, sems) + forwards(outs, sems):
            cp.wait_send()

    return _Comm([], placed, [], [pltpu.SemaphoreType.DMA((n, 3))] * 4, start, finish, middle)


def _pair_exchange_comm(gs):
    n = len(gs)

    def copies(ins, outs, sems):
        send, recv = sems
        x, y, c, _, _, _ = _place()
        cps = []
        for t in range(n):
            rh = gs[t].shape[1] // 2
            cps.append(pltpu.make_async_remote_copy(
                src_ref=ins[t].at[:, pl.ds((1 - c) * rh, rh), :], dst_ref=outs[t],
                send_sem=send.at[t], recv_sem=recv.at[t], device_id=(x, y, 1 - c), device_id_type=MESH))
        return cps

    def start(ins, zones, outs, sems):
        for cp in copies(ins, outs, sems):
            cp.start()

    def finish(ins, zones, outs, sems):
        for cp in copies(ins, outs, sems):
            cp.wait()

    news = [jax.ShapeDtypeStruct((4, g.shape[1] // 2, g.shape[2]), g.dtype) for g in gs]
    return _Comm(gs, [], news, [pltpu.SemaphoreType.DMA((n,))] * 2, start, finish)


def _chip_scatter_comm(ps, qs):
    n = len(ps)

    def sends(ins, outs, sems):
        send, recv = sems
        x, y, c, me, others, okey = _place()
        return [pltpu.make_async_remote_copy(
            src_ref=ins[t].at[okey[j]], dst_ref=outs[t].at[me],
            send_sem=send.at[t, j], recv_sem=recv.at[t, j], device_id=(ox, oy, c), device_id_type=MESH)
            for t in range(n) for j, (ox, oy) in enumerate(others)]

    def start(ins, outs, news, sems):
        for cp in sends(ins, outs, sems):
            cp.start()

    def finish(ins, outs, news, sems):
        send, recv = sems
        x, y, c, me, others, okey = _place()
        for t in range(n):
            for j in range(3):
                slot = outs[t].at[okey[j]]
                pltpu.make_async_remote_copy(
                    src_ref=slot, dst_ref=slot, send_sem=send.at[t, j], recv_sem=recv.at[t, j],
                    device_id=(x, y, c), device_id_type=MESH).wait_recv()
        for cp in sends(ins, outs, sems):
            cp.wait_send()

    return _Comm(ps, qs, [], [pltpu.SemaphoreType.DMA((n, 3))] * 2, start, finish)


def _half_exchange_comm(arrs, layers=None):
    n = len(arrs)
    items = [(t, layer) for t in range(n) for layer in (range(arrs[t].shape[0]) if layers is None else layers[t])]

    def sends(outs, sems):
        send, recv = sems
        x, y, c, _, _, _ = _place()
        cps = []
        for k, (t, layer) in enumerate(items):
            rh = arrs[t].shape[1] // 2
            mine = outs[t].at[layer, pl.ds(c * rh, rh), :]
            cps.append(pltpu.make_async_remote_copy(
                src_ref=mine, dst_ref=mine, send_sem=send.at[k], recv_sem=recv.at[k],
                device_id=(x, y, 1 - c), device_id_type=MESH))
        return cps

    def start(srcs, outs, news, sems):
        for cp in sends(outs, sems):
            cp.start()

    def finish(srcs, outs, news, sems):
        send, recv = sems
        x, y, c, _, _, _ = _place()
        for k, (t, layer) in enumerate(items):
            rh = arrs[t].shape[1] // 2
            theirs = outs[t].at[layer, pl.ds((1 - c) * rh, rh), :]
            pltpu.make_async_remote_copy(
                src_ref=theirs, dst_ref=theirs, send_sem=send.at[k], recv_sem=recv.at[k],
                device_id=(x, y, 1 - c), device_id_type=MESH).wait_recv()
        for cp in sends(outs, sems):
            cp.wait_send()

    return _Comm([], arrs, [], [pltpu.SemaphoreType.DMA((len(items),))] * 2, start, finish)


SMALL_COLS = 1024
SMALL_PIECE_ROWS = 48


def _allreduce_small(buf, comm=None):
    pr = SMALL_PIECE_ROWS
    flips = [(d >> 2 & 1, d >> 1 & 1, d & 1) for d in range(1, 8)]

    def body(x_ref, o_ref, rbuf, send1, recv1, send2, recv2):
        x, y, c = lax.axis_index("x"), lax.axis_index("y"), lax.axis_index("c")
        me = 4 * x + 2 * y + c
        peers = [(x ^ fx, y ^ fy, c ^ fc) for fx, fy, fc in flips]
        pid = [4 * px + 2 * py + pc for px, py, pc in peers]

        def piece(ref, p):
            return ref.at[pl.ds(pl.multiple_of(p * pr, 8), pr), :]

        cps = []
        for d in range(7):
            cp = pltpu.make_async_remote_copy(
                src_ref=piece(x_ref, pid[d]), dst_ref=rbuf.at[d + 1],
                send_sem=send1.at[d], recv_sem=recv1.at[d], device_id=peers[d], device_id_type=MESH)
            cp.start()
            cps.append(cp)
        acc = piece(x_ref, me)[...]
        for d in range(7):
            cps[d].wait_recv()
            acc = acc + rbuf[d + 1]
        piece(o_ref, me)[...] = acc
        out = []
        for d in range(7):
            cp = pltpu.make_async_remote_copy(
                src_ref=piece(o_ref, me), dst_ref=piece(o_ref, me),
                send_sem=send2.at[d], recv_sem=recv2.at[d], device_id=peers[d], device_id_type=MESH)
            cp.start()
            out.append(cp)
        for d in range(7):
            pltpu.make_async_remote_copy(
                src_ref=piece(o_ref, pid[d]), dst_ref=piece(o_ref, pid[d]),
                send_sem=send2.at[d], recv_sem=recv2.at[d], device_id=peers[d], device_id_type=MESH).wait_recv()
        for cp in cps + out:
            cp.wait_send()

    vm = pl.BlockSpec(memory_space=pltpu.VMEM)
    return _call(
        body, name="small_allreduce", grid=(), in_specs=[vm], out_specs=vm,
        out_shape=jax.ShapeDtypeStruct(buf.shape, F32),
        scratch_shapes=[pltpu.VMEM((8, pr, SMALL_COLS), F32)] + [pltpu.SemaphoreType.DMA((7,))] * 4,
        args=(buf,), comm=comm)


def _rows_of(shape):
    return -(-math.prod(shape) // (8 * SMALL_COLS)) * 8


def _pack(arrays, rows):
    parts = []
    for a in arrays:
        r = _rows_of(a.shape)
        parts.append(jnp.pad(a.reshape(-1), (0, r * SMALL_COLS - a.size)).reshape(r, SMALL_COLS))
    used = sum(p.shape[0] for p in parts)
    if rows > used:
        parts.append(jnp.zeros((rows - used, SMALL_COLS), F32))
    return jnp.concatenate(parts, axis=0)


def _unpack(buf, shapes):
    out, off = [], 0
    for s in shapes:
        r = _rows_of(s)
        out.append(buf[off:off + r].reshape(-1)[:math.prod(s)].reshape(s))
        off += r
    return out


BIG = ["sgu_w_in", "sgu_w_out", "attn_w_qkv", "attn_w_o", "ffn_w_up", "ffn_w_down"]
SMALL = ["mix_norm", "ffn_norm", "sgu_v_gain", "sgu_w_s", "sgu_b_s", "attn_q_gain", "attn_k_gain", "attn_sinks",
         "rel_bias", "ffn_conv_b"]
ORDER = ["mix_norm", "ffn_norm", "sgu_w_in", "sgu_v_gain", "sgu_w_s", "sgu_b_s", "sgu_w_out", "attn_w_qkv",
         "attn_q_gain", "attn_k_gain", "attn_sinks", "attn_w_o", "rel_bias", "ffn_w_up", "ffn_conv_w", "ffn_conv_b",
         "ffn_w_down"]


def kernel(x, mix_norm, ffn_norm, sgu_w_in, sgu_v_gain, sgu_w_s, sgu_b_s, sgu_w_out, attn_w_qkv, attn_q_gain, attn_k_gain, attn_sinks, attn_w_o, rel_bias, ffn_w_up, ffn_conv_w, ffn_conv_b, ffn_w_down, loss_target, m_mix_norm, m_ffn_norm, m_sgu_w_in, m_sgu_v_gain, m_sgu_w_s, m_sgu_b_s, m_sgu_w_out, m_attn_w_qkv, m_attn_q_gain, m_attn_k_gain, m_attn_sinks, m_attn_w_o, m_rel_bias, m_ffn_w_up, m_ffn_conv_w, m_ffn_conv_b, m_ffn_w_down, v_mix_norm, v_ffn_norm, v_sgu_w_in, v_sgu_v_gain, v_sgu_w_s, v_sgu_b_s, v_sgu_w_out, v_attn_w_qkv, v_attn_q_gain, v_attn_k_gain, v_attn_sinks, v_attn_w_o, v_rel_bias, v_ffn_w_up, v_ffn_conv_w, v_ffn_conv_b, v_ffn_w_down):
    w = dict(mix_norm=mix_norm, ffn_norm=ffn_norm, sgu_w_in=sgu_w_in, sgu_v_gain=sgu_v_gain, sgu_w_s=sgu_w_s,
             sgu_b_s=sgu_b_s, sgu_w_out=sgu_w_out, attn_w_qkv=attn_w_qkv, attn_q_gain=attn_q_gain,
             attn_k_gain=attn_k_gain, attn_sinks=attn_sinks, attn_w_o=attn_w_o, rel_bias=rel_bias, ffn_w_up=ffn_w_up,
             ffn_conv_w=ffn_conv_w, ffn_conv_b=ffn_conv_b, ffn_w_down=ffn_w_down)
    mom = dict(mix_norm=m_mix_norm, ffn_norm=m_ffn_norm, sgu_w_in=m_sgu_w_in, sgu_v_gain=m_sgu_v_gain,
               sgu_w_s=m_sgu_w_s, sgu_b_s=m_sgu_b_s, sgu_w_out=m_sgu_w_out, attn_w_qkv=m_attn_w_qkv,
               attn_q_gain=m_attn_q_gain, attn_k_gain=m_attn_k_gain, attn_sinks=m_attn_sinks, attn_w_o=m_attn_w_o,
               rel_bias=m_rel_bias, ffn_w_up=m_ffn_w_up, ffn_conv_w=m_ffn_conv_w, ffn_conv_b=m_ffn_conv_b,
               ffn_w_down=m_ffn_w_down)
    var = dict(mix_norm=v_mix_norm, ffn_norm=v_ffn_norm, sgu_w_in=v_sgu_w_in, sgu_v_gain=v_sgu_v_gain,
               sgu_w_s=v_sgu_w_s, sgu_b_s=v_sgu_b_s, sgu_w_out=v_sgu_w_out, attn_w_qkv=v_attn_w_qkv,
               attn_q_gain=v_attn_q_gain, attn_k_gain=v_attn_k_gain, attn_sinks=v_attn_sinks, attn_w_o=v_attn_w_o,
               rel_bias=v_rel_bias, ffn_w_up=v_ffn_w_up, ffn_conv_w=v_ffn_conv_w, ffn_conv_b=v_ffn_conv_b,
               ffn_w_down=v_ffn_w_down)
    chip = 2 * lax.axis_index("x") + lax.axis_index("y")
    core = lax.axis_index("c")

    where = jnp.stack([core, chip]).astype(jnp.int32)
    names = ["sgu_w_in", "sgu_w_out", "attn_w_qkv", "attn_w_o", "ffn_w_up0", "ffn_w_up1", "ffn_w_down0", "ffn_w_down1"]
    shards = [(sgu_w_in, 0), (sgu_w_out, 0), (attn_w_qkv, 0), (attn_w_o, 0), (ffn_w_up, 0), (ffn_w_up, 1),
              (ffn_w_down, 0), (ffn_w_down, 1)]
    T = {nm: _place_shard(s, l, where, BF16, "place_" + nm) for (s, l), nm in zip(shards, names)}
    for l in range(2):
        T["conv_w%d" % l] = _place_shard(ffn_conv_w, l, where, F32, "place_conv_w%d" % l)

    def gather(keys):
        items = []
        for k in keys:
            k, lo, hi = (k, 0, None) if isinstance(k, str) else k
            items.append((T[k], lo, T[k].shape[1] if hi is None else hi, not k.startswith("conv")))
        return _gather_comm(items)

    def gathered(keys, res):
        for k, a in zip(keys, res[0]):
            T[k if isinstance(k, str) else k[0]] = a

    D = x.shape[2]
    first = ["sgu_w_in", "conv_w0", "conv_w1"]
    gathered(first, _run_comm(gather(first), "gather_first"))
    unshard_cols = lambda a: jnp.transpose(a, (1, 0, 2)).reshape(a.shape[1], -1)
    cw = [unshard_cols(T["conv_w0"]), unshard_cols(T["conv_w1"])]
    cb = ffn_conv_b
    flat = lambda k: T[k].reshape(-1, D)
    x2, target = x[0], loss_target[0]
    bucket = jnp.asarray(_rel_buckets_flat())
    wtril, wtrilT = _sgu_prep(sgu_w_s[0])
    bT = sgu_b_s[0].T
    bias = _rel_bias_expand(rel_bias, bucket).reshape(N_KV_HEADS, KV_GROUP * CHUNK, 2 * CHUNK)
    qg2 = jnp.tile(attn_q_gain, (1, 2))
    kg2 = jnp.tile(attn_k_gain, (1, 2))
    sinks = attn_sinks.reshape(N_HEADS)
    mix0, mix1 = mix_norm[0:1], mix_norm[1:2]
    fn0, fn1 = ffn_norm[0:1], ffn_norm[1:2]

    ks = ["sgu_w_out", ("ffn_w_up0", 0, 256)]
    (hn0, z, cdf), r = _norm_matmul(x2, mix0, T["sgu_w_in"], "sgu_in", comm=gather(ks), with_cdf=True)
    gathered(ks, r)
    ks = [("ffn_w_up0", 256, None)]
    (yp, h1), r = _sgu_fwd(z, cdf, x2, sgu_v_gain, wtril, bT, flat("sgu_w_out"), comm=gather(ks))
    gathered(ks, r)
    ks = ["ffn_w_down0", "attn_w_qkv"]
    (hf0, a0, cp0), r = _norm_matmul(h1, fn0, T["ffn_w_up0"], "ffn0_up", comm=gather(ks), conv=(cw[0], cb[0:1]))
    gathered(ks, r)
    ks = ["attn_w_o", ("ffn_w_up1", 0, 640)]
    (f0, h2), r = _ffn_fwd(cp0, h1, flat("ffn_w_down0"), "ffn0_fwd", comm=gather(ks))
    gathered(ks, r)
    (hn1, qkv), _ = _norm_matmul(h2, mix1, T["attn_w_qkv"], "attn_qkv")
    ks = [("ffn_w_up1", 640, None), "ffn_w_down1"]
    (o, probs, psinks), r = _attn_fwd(qkv, qg2, kg2, sinks, bias, comm=gather(ks))
    gathered(ks, r)
    h3, _ = _matmul_res(o, flat("attn_w_o"), h2, "attn_out")
    (hf1, a1, cp1), _ = _norm_matmul(h3, fn1, T["ffn_w_up1"], "ffn1_up", conv=(cw[1], cb[1:2]))
    (f1, dh4, loss_local), _ = _ffn_fwd(cp1, h3, flat("ffn_w_down1"), "ffn1_fwd", target=target)

    G, RS, PQ, QD, halves, grads = {}, {}, {}, {}, {}, {}
    dest_of = {"sgu_w_in": ("sgu_w_in", 0), "sgu_w_out": ("sgu_w_out", 0), "attn_w_qkv": ("attn_w_qkv", 0),
               "attn_w_o": ("attn_w_o", 0), "ffn_w_up0": ("ffn_w_up", 0), "ffn_w_up1": ("ffn_w_up", 1),
               "ffn_w_down0": ("ffn_w_down", 0), "ffn_w_down1": ("ffn_w_down", 1)}

    def px(keys):
        return _pair_exchange_comm([G[k] for k in keys])

    def px_done(keys, res):
        for k, a in zip(keys, res[1]):
            RS[k] = a
            PQ[k] = _pair_add(G[k], a, where, "pair_add_" + k)

    def sc(keys):
        return _chip_scatter_comm([PQ[k][0] for k in keys], [PQ[k][1] for k in keys])

    def sc_done(keys, res):
        for k, a in zip(keys, res[0]):
            wk, layer = dest_of[k]
            halves[wk] = _sum_chips(a, where, halves.get(wk), layer, w[wk].shape, "sum_chips_" + k)

    nup = ffn_w_up.shape[2]
    ndown = ffn_w_down.shape[1]
    rows4 = lambda a: a.reshape(4, ndown, D)
    (dc1, dcb1), _ = _ffn_bwd_dc(dh4, cp1, flat("ffn_w_down1"), "ffn1_bwd_dc")
    gw, _ = _matmul_tn(f1, dh4, "ffn1_dw_down", ka=2 * ndown, nb=D, out_dtype=BF16)
    G["ffn_w_down1"] = rows4(gw)
    (dh3, dfn1, da1, dcw1), _ = _bwd_norm(dc1, T["ffn_w_up1"], h3, fn1, dh4, "ffn1_bwd_in", conv_w=cw[1], conv_in=a1)
    G["ffn_w_up1"], _ = _matmul_tn(hf1, da1, "ffn1_dw_up", ka=D, nb=nup, out_dtype=BF16)
    ks = ["ffn_w_down1", "ffn_w_up1"]
    G["attn_w_o"], r = _matmul_tn(o, dh3, "attn_dw_o", ka=QW // 4, nb=D, out_dtype=BF16, comm=px(ks))
    px_done(ks, r)
    do, _ = _matmul_nt(dh3, flat("attn_w_o"), "attn_bwd_out")
    (dqkv, dbias, dqg, dkg, dsink), r = _attn_bwd(qkv, do, probs, psinks, qg2, kg2, comm=sc(ks))
    sc_done(ks, r)
    G["attn_w_qkv"], _ = _matmul_tn(hn1, dqkv, "attn_dw_qkv", ka=D, nb=dqkv.shape[1] // 4, out_dtype=BF16)
    ks = ["attn_w_o", "attn_w_qkv"]
    (dh2, dmix1), r = _bwd_norm(dqkv, T["attn_w_qkv"], h2, mix1, dh3, "attn_bwd_in", comm=px(ks))
    px_done(ks, r)
    (dc0, dcb0), r = _ffn_bwd_dc(dh2, cp0, flat("ffn_w_down0"), "ffn0_bwd_dc", comm=sc(ks))
    sc_done(ks, r)
    gw, _ = _matmul_tn(f0, dh2, "ffn0_dw_down", ka=2 * ndown, nb=D, out_dtype=BF16)
    G["ffn_w_down0"] = rows4(gw)
    (dh1, dfn0, da0, dcw0), r = _bwd_norm(dc0, T["ffn_w_up0"], h1, fn0, dh2, "ffn0_bwd_in", conv_w=cw[0], conv_in=a0,
                                          comm=px(["ffn_w_down0"]))
    px_done(["ffn_w_down0"], r)
    G["ffn_w_up0"], r = _matmul_tn(hf0, da0, "ffn0_dw_up", ka=D, nb=nup, out_dtype=BF16, comm=sc(["ffn_w_down0"]))
    sc_done(["ffn_w_down0"], r)
    ks = ["ffn_w_up0"]
    G["sgu_w_out"], r = _matmul_tn(yp, dh1, "sgu_dw_out", ka=yp.shape[1] // 4, nb=D, out_dtype=BF16, comm=px(ks))
    px_done(ks, r)
    both = [sc(ks), px(["sgu_w_out"])]
    (dz, dws, dbT, dvg), r = _sgu_bwd(dh1, z, cdf, sgu_v_gain, wtril, wtrilT, bT, flat("sgu_w_out"),
                                      comm=_join(both))
    r = _split(both, r)
    sc_done(ks, r[0])
    px_done(["sgu_w_out"], r[1])
    done = ["attn_w_qkv", "attn_w_o", "ffn_w_up", "ffn_w_down"]
    both = [sc(["sgu_w_out"]), _half_exchange_comm([halves[k] for k in done])]
    G["sgu_w_in"], r = _matmul_tn(hn0, dz, "sgu_dw_in", ka=D, nb=dz.shape[1] // 4, out_dtype=BF16, comm=_join(both))
    r = _split(both, r)
    sc_done(["sgu_w_out"], r[0])
    for k, a in zip(done, r[1][0]):
        grads[k] = a
    px_done(["sgu_w_in"], _run_comm(px(["sgu_w_in"]), "grad_last_pair_exchange"))
    (grad_x, dmix0), r = _bwd_norm(dz, T["sgu_w_in"], x2, mix0, dh1, "sgu_bwd_in", comm=sc(["sgu_w_in"]))
    sc_done(["sgu_w_in"], r)

    g = dict(mix_norm=jnp.concatenate([dmix0, dmix1], axis=0), ffn_norm=jnp.concatenate([dfn0, dfn1], axis=0),
             sgu_v_gain=dvg, sgu_w_s=dws, sgu_b_s=dbT.T, attn_q_gain=dqg[:, :HEAD_DIM], attn_k_gain=dkg[:, :HEAD_DIM],
             attn_sinks=dsink[:, :N_KV_HEADS].T.reshape(1, N_HEADS),
             rel_bias=_rel_bias_reduce(dbias.reshape(N_HEADS, CHUNK * 2 * CHUNK), bucket),
             ffn_conv_b=jnp.concatenate([dcb0, dcb1], axis=0))
    small_list = [g[k].reshape(w[k].shape) for k in SMALL] + [jnp.stack([dcw0, dcw1]), loss_local]
    small_shapes = [a.shape for a in small_list]
    done = ["sgu_w_in", "sgu_w_out"]
    red, r = _allreduce_small(_pack(small_list, 8 * SMALL_PIECE_ROWS),
                              comm=_half_exchange_comm([halves[k] for k in done]))
    for k, a in zip(done, r[0]):
        grads[k] = a
    red = _unpack(red, small_shapes)
    for k, a in zip(SMALL, red):
        grads[k] = a
    grads["ffn_conv_w"] = lax.dynamic_slice_in_dim(red[-2], chip * ffn_conv_w.shape[2], ffn_conv_w.shape[2], axis=2)
    loss = red[-1][0, 0]

    delta, new_m, new_v = {}, {}, {}
    two = lambda a: a.reshape(-1, a.shape[-1])
    for k in BIG:
        (d2, m2, v2, g2), _ = _adamw(two(grads[k]), two(w[k]), two(mom[k]), two(var[k]), "adamw_" + k)
        delta[k], new_m[k], new_v[k], grads[k] = (a.reshape(w[k].shape) for a in (d2, m2, v2, g2))
    sm = SMALL + ["ffn_conv_w"]
    sm_shapes = [w[k].shape for k in sm]
    rows = sum(_rows_of(s) for s in sm_shapes)
    (d2, m2, v2, _), _ = _adamw(_pack([grads[k] for k in sm], rows), _pack([w[k] for k in sm], rows),
                                _pack([mom[k] for k in sm], rows), _pack([var[k] for k in sm], rows), "adamw_small")
    for dst, buf in ((delta, d2), (new_m, m2), (new_v, v2)):
        for k, a in zip(sm, _unpack(buf, sm_shapes)):
            dst[k] = a

    return (loss, grad_x[None], *[grads[k] for k in ORDER], *[delta[k] for k in ORDER],
            *[new_m[k] for k in ORDER], *[new_v[k] for k in ORDER])
```

```python
import functools
import math

import numpy as np
import jax
import jax.numpy as jnp
from jax import lax
from jax.experimental import pallas as pl
from jax.experimental.pallas import tpu as pltpu

F32 = jnp.float32
BF16 = jnp.bfloat16

EPS = 1e-6
CHUNK = 128
SGU_GROUPS = 16
HEAD_DIM = 64
N_HEADS = 16
N_KV_HEADS = 4
KV_GROUP = N_HEADS // N_KV_HEADS
REL_BUCKETS = 32
REL_MAX_DIST = 128
LANES = 128
HALO = 16

ADAM_LR = 0.001
ADAM_B1 = 0.9
ADAM_B2 = 0.999
ADAM_EPS = 1e-08
ADAM_WD = 0.01
ADAM_STEP = 10

VMEM_LIMIT_V7X = 56 * 1024 * 1024

_SQRT_HALF = math.sqrt(0.5)
_INV_SQRT_2PI = 1.0 / math.sqrt(2.0 * math.pi)


def _cp(sem):
    return pltpu.CompilerParams(dimension_semantics=sem, vmem_limit_bytes=VMEM_LIMIT_V7X)


def _resident(shape):
    nd = len(shape)
    return pl.BlockSpec(shape, lambda *_: (0,) * nd, pipeline_mode=pl.Buffered(1))


class _Comm:
    def __init__(self, srcs, zones, news, sems, start, finish, middle=None):
        self.srcs, self.zones, self.news, self.sems = list(srcs), list(zones), list(news), list(sems)
        self.start, self.finish = start, finish
        self.middle = middle if middle is not None else (lambda srcs, zones, news, sems: None)


def _join(comms):
    comms = [c for c in comms if c is not None]
    if not comms:
        return None

    def part(seq, attr):
        out, k = [], 0
        for c in comms:
            n = len(getattr(c, attr))
            out.append(seq[k:k + n])
            k += n
        return out

    def run(which):
        def f(srcs, zones, news, sems):
            for c, a, b, d, e in zip(comms, part(srcs, "srcs"), part(zones, "zones"), part(news, "news"), part(sems, "sems")):
                getattr(c, which)(a, b, d, e)
        return f

    cat = lambda attr: [v for c in comms for v in getattr(c, attr)]
    return _Comm(cat("srcs"), cat("zones"), cat("news"), cat("sems"), run("start"), run("finish"), run("middle"))


def _split(comms, res):
    zones, news = res
    out, kz, kn = [], 0, 0
    for c in comms:
        out.append((zones[kz:kz + len(c.zones)], news[kn:kn + len(c.news)]))
        kz += len(c.zones)
        kn += len(c.news)
    return out


def _call(body, *, name, grid, in_specs, out_specs, out_shape, args, scratch_shapes=(), sem=None, comm=None):
    if comm is None:
        res = pl.pallas_call(body, name=name, grid=grid, in_specs=in_specs, out_specs=out_specs, out_shape=out_shape,
                             scratch_shapes=list(scratch_shapes), compiler_params=_cp(sem))(*args)
        return res, None
    single = not isinstance(out_shape, (list, tuple))
    out_specs_l = [out_specs] if single else list(out_specs)
    out_shape_l = [out_shape] if single else list(out_shape)
    n_in, n_out, n_scr = len(in_specs), len(out_shape_l), len(scratch_shapes)
    ns, nz, nn = len(comm.srcs), len(comm.zones), len(comm.news)

    def wrapped(*refs):
        k = n_in
        ins, srcs = refs[:k], refs[k:k + ns]
        k += ns + nz
        outs, zones, news = refs[k:k + n_out], refs[k + n_out:k + n_out + nz], refs[k + n_out + nz:k + n_out + nz + nn]
        k += n_out + nz + nn
        scr, sems = refs[k:k + n_scr], refs[k + n_scr:]
        if not grid:
            comm.start(srcs, zones, news, sems)
            body(*ins, *outs, *scr)
            comm.middle(srcs, zones, news, sems)
            comm.finish(srcs, zones, news, sems)
            return
        first = functools.reduce(jnp.logical_and, [pl.program_id(a) == 0 for a in range(len(grid))])
        last = functools.reduce(jnp.logical_and, [pl.program_id(a) == grid[a] - 1 for a in range(len(grid))])
        early = len(grid) == 1 and grid[0] >= 2
        mid_step = grid[0] - (2 if grid[0] >= 8 else 1)

        @pl.when(first)
        def _():
            comm.start(srcs, zones, news, sems)

        if early:
            @pl.when(pl.program_id(0) == mid_step)
            def _():
                comm.middle(srcs, zones, news, sems)

        body(*ins, *outs, *scr)

        @pl.when(last)
        def _():
            if not early:
                comm.middle(srcs, zones, news, sems)
            comm.finish(srcs, zones, news, sems)

    res = pl.pallas_call(
        wrapped, name=name, grid=grid,
        in_specs=list(in_specs) + [ANY] * (ns + nz), out_specs=out_specs_l + [ANY] * (nz + nn),
        out_shape=out_shape_l + [jax.ShapeDtypeStruct(z.shape, z.dtype) for z in comm.zones] + comm.news,
        input_output_aliases={n_in + ns + i: n_out + i for i in range(nz)},
        scratch_shapes=list(scratch_shapes) + comm.sems,
        compiler_params=_cp(("arbitrary",) * len(grid)),
    )(*args, *comm.srcs, *comm.zones)
    main = res[0] if single else list(res[:n_out])
    return main, (list(res[n_out:n_out + nz]), list(res[n_out + nz:]))


def _run_comm(comm, name):
    ns, nz, nn = len(comm.srcs), len(comm.zones), len(comm.news)

    def body(*refs):
        srcs, zones, news, sems = refs[:ns], refs[ns + nz:ns + 2 * nz], refs[ns + 2 * nz:ns + 2 * nz + nn], refs[ns + 2 * nz + nn:]
        comm.start(srcs, zones, news, sems)
        comm.middle(srcs, zones, news, sems)
        comm.finish(srcs, zones, news, sems)

    res = pl.pallas_call(
        body, name=name, in_specs=[ANY] * (ns + nz), out_specs=[ANY] * (nz + nn),
        out_shape=[jax.ShapeDtypeStruct(z.shape, z.dtype) for z in comm.zones] + comm.news,
        input_output_aliases={ns + i: i for i in range(nz)}, scratch_shapes=comm.sems,
    )(*comm.srcs, *comm.zones)
    return list(res[:nz]), list(res[nz:])


def _dot(a, b):
    return jnp.dot(a, b, preferred_element_type=F32)


def _dot_nt(a, b):
    return lax.dot_general(a, b, (((1,), (1,)), ((), ())), preferred_element_type=F32)


def _dot_tn(a, b):
    return lax.dot_general(a, b, (((0,), (0,)), ((), ())), preferred_element_type=F32)


def _normal_cdf(x):
    return 0.5 * (1.0 + lax.erf(x * _SQRT_HALF))


def _gelu_and_grad(x, cdf):
    return x * cdf, cdf + x * jnp.exp(-0.5 * x * x) * _INV_SQRT_2PI


def _sigmoid(x):
    return 0.5 * jnp.tanh(0.5 * x) + 0.5


def _rms_bwd(dy, x, gain):
    r = lax.rsqrt(jnp.mean(x * x, axis=-1, keepdims=True) + EPS)
    xhat = x * r
    gdy = dy * gain
    dx = r * (gdy - xhat * jnp.mean(gdy * xhat, axis=-1, keepdims=True))
    return dx, jnp.sum(dy * xhat, axis=0, keepdims=True)


def _norm_matmul(x, gain, w4, name, comm=None, with_cdf=False, conv=None, tm=1024):
    S, D = x.shape
    nsh, _, ns = w4.shape
    extra = with_cdf or conv is not None
    tm = min(512 if extra else tm, S)
    T = 16

    def body(x_ref, g_ref, w_ref, *rest):
        if conv is not None:
            cw_ref, cb_ref, hn_ref, o_ref, e_ref, tail = rest
        else:
            hn_ref, o_ref = rest[:2]
            e_ref = rest[2] if with_cdf else None
        xf = x_ref[...]
        r = lax.rsqrt(jnp.mean(xf * xf, axis=-1, keepdims=True) + EPS)
        hn = (xf * r * g_ref[...]).astype(BF16)
        hn_ref[...] = hn
        if conv is not None:
            @pl.when(pl.program_id(0) == 0)
            def _():
                tail[...] = jnp.zeros_like(tail)

        for j in range(nsh):
            cols = slice(j * ns, (j + 1) * ns)
            out = _dot(hn, w_ref[j])
            o_ref[:, cols] = out.astype(BF16)
            if with_cdf:
                e_ref[:, cols] = _normal_cdf(out).astype(BF16)
            if conv is not None:
                w0, w1, w2, b = cw_ref[0:1, cols], cw_ref[1:2, cols], cw_ref[2:3, cols], cb_ref[:, cols]
                taps = lambda v: w0 * pltpu.roll(v, 2, 0) + w1 * pltpu.roll(v, 1, 0) + w2 * v + b
                e_ref[:, cols] = taps(out).astype(BF16)
                head = jnp.concatenate([tail[:, cols], out[0:T]], axis=0)
                e_ref[0:T, cols] = taps(head)[T:].astype(BF16)
                tail[:, cols] = out[tm - T:tm]

    row = lambda width: pl.BlockSpec((tm, width), lambda i: (i, 0))
    N = nsh * ns
    in_specs = [row(D), _resident((1, D)), _resident(w4.shape)]
    args = (x, gain, w4)
    scratch = []
    if conv is not None:
        in_specs += [_resident((3, N)), _resident((1, N))]
        args += tuple(conv)
        scratch = [pltpu.VMEM((T, N), F32)]
    return _call(
        body, name=name, grid=(S // tm,), in_specs=in_specs,
        out_specs=[row(D), row(N)] + [row(N)] * extra,
        out_shape=[jax.ShapeDtypeStruct((S, D), BF16)] + [jax.ShapeDtypeStruct((S, N), BF16)] * (1 + extra),
        scratch_shapes=scratch, sem=("arbitrary",) if conv is not None else ("parallel",), args=args, comm=comm)


def _matmul_res(a, w, res, name, comm=None, tm=1024):
    S, K = a.shape
    N = w.shape[1]
    tm = min(tm, S)

    def body(a_ref, w_ref, r_ref, o_ref):
        o_ref[...] = r_ref[...] + _dot(a_ref[...], w_ref[...])

    return _call(
        body, name=name, grid=(S // tm,),
        in_specs=[pl.BlockSpec((tm, K), lambda i: (i, 0)), _resident(w.shape), pl.BlockSpec((tm, N), lambda i: (i, 0))],
        out_specs=pl.BlockSpec((tm, N), lambda i: (i, 0)),
        out_shape=jax.ShapeDtypeStruct((S, N), F32),
        sem=("parallel",), args=(a, w, res), comm=comm)


def _matmul_nt(dh, w, name, comm=None, tm=1024):
    S, N = dh.shape
    K = w.shape[0]
    tm = min(tm, S)

    def body(d_ref, w_ref, o_ref):
        o_ref[...] = _dot_nt(d_ref[...].astype(BF16), w_ref[...]).astype(BF16)

    return _call(
        body, name=name, grid=(S // tm,),
        in_specs=[pl.BlockSpec((tm, N), lambda i: (i, 0)), _resident(w.shape)],
        out_specs=pl.BlockSpec((tm, K), lambda i: (i, 0)),
        out_shape=jax.ShapeDtypeStruct((S, K), BF16),
        sem=("parallel",), args=(dh, w), comm=comm)


def _matmul_tn(a, b, name, *, ka, nb, out_dtype, comm=None, ts=1024):
    S, KA = a.shape
    NB = b.shape[1]
    ts = min(ts, S)
    J = max(KA // ka, NB // nb)
    a_map = (lambda j, s: (s, j)) if KA // ka > 1 else (lambda j, s: (s, 0))
    b_map = (lambda j, s: (s, j)) if NB // nb > 1 else (lambda j, s: (s, 0))
    last = S // ts - 1

    def body(a_ref, b_ref, o_ref, acc):
        s = pl.program_id(1)

        @pl.when(s == 0)
        def _():
            acc[...] = jnp.zeros_like(acc)

        acc[...] += _dot_tn(a_ref[...].astype(BF16), b_ref[...].astype(BF16))

        @pl.when(s == last)
        def _():
            o_ref[0] = acc[...].astype(out_dtype)

    return _call(
        body, name=name, grid=(J, S // ts),
        in_specs=[pl.BlockSpec((ts, ka), a_map), pl.BlockSpec((ts, nb), b_map)],
        out_specs=pl.BlockSpec((1, ka, nb), lambda j, s: (j, 0, 0)),
        out_shape=jax.ShapeDtypeStruct((J, ka, nb), out_dtype),
        scratch_shapes=[pltpu.VMEM((ka, nb), F32)],
        sem=("parallel", "arbitrary"), args=(a, b), comm=comm)


def _sgu_prep(w_s):
    G = w_s.shape[0]

    def body(w_ref, t_ref, tt_ref):
        tri = lax.broadcasted_iota(jnp.int32, (CHUNK, CHUNK), 0) >= lax.broadcasted_iota(jnp.int32, (CHUNK, CHUNK), 1)
        for g in range(G):
            t = jnp.where(tri, w_ref[g], 0.0)
            t_ref[g] = t.astype(BF16)
            tt_ref[g] = t.T.astype(BF16)

    return pl.pallas_call(
        body, name="sgu_prep",
        out_shape=[jax.ShapeDtypeStruct(w_s.shape, BF16), jax.ShapeDtypeStruct(w_s.shape, BF16)],
        compiler_params=_cp(None),
    )(w_s)


def _sgu_fwd(z, cdf, x, vg, wtril, bT, wout, comm=None, tm=512):
    S = z.shape[0]
    W = z.shape[1] // 2
    D = x.shape[1]
    tm = min(tm, S)

    def body(z_ref, cdf_ref, x_ref, vg_ref, wt_ref, bT_ref, wo_ref, yp_ref, h_ref):
        def chunk(c, carry):
            r0 = pl.multiple_of(c * CHUNK, CHUNK)
            zc = z_ref[pl.ds(r0, CHUNK), :].astype(F32)
            cdf = cdf_ref[pl.ds(r0, CHUNK), :].astype(F32)
            u = zc[:, :W] * cdf[:, :W]
            v = zc[:, W:] * cdf[:, W:]
            rv = lax.rsqrt(jnp.mean(v * v, axis=-1, keepdims=True) + EPS)
            vn = (v * rv * vg_ref[...]).astype(BF16)
            for g in range(SGU_GROUPS):
                sl = slice(g * LANES, (g + 1) * LANES)
                s = _dot(wt_ref[g], vn[:, sl]) + bT_ref[:, g:g + 1]
                yp_ref[pl.ds(r0, CHUNK), sl] = (u[:, sl] * s).astype(BF16)
            return carry

        lax.fori_loop(0, tm // CHUNK, chunk, 0)
        h_ref[...] = x_ref[...] + _dot(yp_ref[...], wo_ref[...])

    return _call(
        body, name="sgu_fwd", grid=(S // tm,),
        in_specs=[pl.BlockSpec((tm, 2 * W), lambda i: (i, 0)), pl.BlockSpec((tm, 2 * W), lambda i: (i, 0)),
                  pl.BlockSpec((tm, D), lambda i: (i, 0)),
                  _resident((1, W)), _resident(wtril.shape), _resident(bT.shape), _resident(wout.shape)],
        out_specs=[pl.BlockSpec((tm, W), lambda i: (i, 0)), pl.BlockSpec((tm, D), lambda i: (i, 0))],
        out_shape=[jax.ShapeDtypeStruct((S, W), BF16), jax.ShapeDtypeStruct((S, D), F32)],
        sem=("parallel",), args=(z, cdf, x, vg, wtril, bT, wout), comm=comm)


def _sgu_bwd(dh, z, cdf, vg, wtril, wtrilT, bT, wout, comm=None, tm=512):
    S = z.shape[0]
    W = z.shape[1] // 2
    D = dh.shape[1]
    G = SGU_GROUPS
    tm = min(tm, S)
    last = S // tm - 1

    def body(dh_ref, z_ref, cdf_ref, vg_ref, wt_ref, wtT_ref, bT_ref, wo_ref,
             dz_ref, dws_ref, dbT_ref, dvg_ref, dyp_s, du_s, dvn_s, dsacc):
        i = pl.program_id(0)

        @pl.when(i == 0)
        def _():
            dws_ref[...] = jnp.zeros_like(dws_ref)
            dvg_ref[...] = jnp.zeros_like(dvg_ref)
            dsacc[...] = jnp.zeros_like(dsacc)

        dyp_s[...] = _dot_nt(dh_ref[...].astype(BF16), wo_ref[...])
        tri = lax.broadcasted_iota(jnp.int32, (CHUNK, CHUNK), 0) >= lax.broadcasted_iota(jnp.int32, (CHUNK, CHUNK), 1)

        def chunk(c, carry):
            r0 = pl.multiple_of(c * CHUNK, CHUNK)
            zc = z_ref[pl.ds(r0, CHUNK), :].astype(F32)
            cdf = cdf_ref[pl.ds(r0, CHUNK), :].astype(F32)
            u, gu = _gelu_and_grad(zc[:, :W], cdf[:, :W])
            v, gv = _gelu_and_grad(zc[:, W:], cdf[:, W:])
            rv = lax.rsqrt(jnp.mean(v * v, axis=-1, keepdims=True) + EPS)
            vhat = v * rv
            vgain = vg_ref[...]
            vn = (vhat * vgain).astype(BF16)
            dyp = dyp_s[pl.ds(r0, CHUNK), :]
            for g in range(G):
                sl = slice(g * LANES, (g + 1) * LANES)
                vng = vn[:, sl]
                s = _dot(wt_ref[g], vng) + bT_ref[:, g:g + 1]
                ds = dyp[:, sl] * u[:, sl]
                du_s[:, sl] = dyp[:, sl] * s
                dsb = ds.astype(BF16)
                dvn_s[:, sl] = _dot(wtT_ref[g], dsb)
                dws_ref[g] += jnp.where(tri, _dot_nt(dsb, vng), 0.0)
                dsacc[g] += ds
            dvn = dvn_s[...]
            dvg_ref[...] += jnp.sum(dvn * vhat, axis=0, keepdims=True)
            gdy = dvn * vgain
            dv = rv * (gdy - vhat * jnp.mean(gdy * vhat, axis=-1, keepdims=True))
            dz_ref[pl.ds(r0, CHUNK), :W] = (du_s[...] * gu).astype(BF16)
            dz_ref[pl.ds(r0, CHUNK), W:] = (dv * gv).astype(BF16)
            return carry

        lax.fori_loop(0, tm // CHUNK, chunk, 0)

        @pl.when(i == last)
        def _():
            for g in range(G):
                dbT_ref[:, g:g + 1] = jnp.sum(dsacc[g], axis=1, keepdims=True)

    return _call(
        body, name="sgu_bwd", grid=(S // tm,),
        in_specs=[pl.BlockSpec((tm, D), lambda i: (i, 0)), pl.BlockSpec((tm, 2 * W), lambda i: (i, 0)),
                  pl.BlockSpec((tm, 2 * W), lambda i: (i, 0)),
                  _resident((1, W)), _resident(wtril.shape), _resident(wtrilT.shape), _resident(bT.shape),
                  _resident(wout.shape)],
        out_specs=[pl.BlockSpec((tm, 2 * W), lambda i: (i, 0)),
                   pl.BlockSpec((G, CHUNK, CHUNK), lambda i: (0, 0, 0)),
                   pl.BlockSpec((CHUNK, G), lambda i: (0, 0)),
                   pl.BlockSpec((1, W), lambda i: (0, 0))],
        out_shape=[jax.ShapeDtypeStruct((S, 2 * W), BF16), jax.ShapeDtypeStruct((G, CHUNK, CHUNK), F32),
                   jax.ShapeDtypeStruct((CHUNK, G), F32), jax.ShapeDtypeStruct((1, W), F32)],
        scratch_shapes=[pltpu.VMEM((tm, W), F32), pltpu.VMEM((CHUNK, W), F32), pltpu.VMEM((CHUNK, W), F32),
                        pltpu.VMEM((G, CHUNK, CHUNK), F32)],
        sem=("arbitrary",), args=(dh, z, cdf, vg, wtril, wtrilT, bT, wout), comm=comm)


def _ffn_fwd(cp, h_in, wdown, name, comm=None, target=None, tm=512, R=128):
    S, C = cp.shape
    F = C // 2
    D = h_in.shape[1]
    tm = min(tm, S)
    head = target is not None

    def body(cp_ref, h_ref, wd_ref, *rest):
        if head:
            t_ref, f_ref, dy_ref, l_ref = rest
        else:
            f_ref, ho_ref = rest
        for r0 in range(0, tm, R):
            g = cp_ref[r0:r0 + R, :F].astype(F32)
            f_ref[r0:r0 + R, :] = (g * _sigmoid(g) * cp_ref[r0:r0 + R, F:].astype(F32)).astype(BF16)
        h_out = h_ref[...] + _dot(f_ref[...], wd_ref[...])
        if not head:
            ho_ref[...] = h_out
            return

        @pl.when(pl.program_id(0) == 0)
        def _():
            l_ref[...] = jnp.zeros_like(l_ref)

        e = h_out - t_ref[...]
        dy_ref[...] = e * (1.0 / D)
        rows = jnp.sum(e * e, axis=-1, keepdims=True) * (1.0 / D)
        l_ref[...] += 0.5 * jnp.sum(rows, axis=0, keepdims=True)

    row = pl.BlockSpec((tm, D), lambda i: (i, 0))
    in_specs = [pl.BlockSpec((tm, C), lambda i: (i, 0)), row, _resident(wdown.shape)]
    out_specs = [pl.BlockSpec((tm, F), lambda i: (i, 0)), row]
    out_shape = [jax.ShapeDtypeStruct((S, F), BF16), jax.ShapeDtypeStruct((S, D), F32)]
    args = (cp, h_in, wdown)
    if head:
        in_specs.append(row)
        out_specs.append(pl.BlockSpec((1, 1), lambda i: (0, 0)))
        out_shape.append(jax.ShapeDtypeStruct((1, 1), F32))
        args += (target,)
    return _call(
        body, name=name, grid=(S // tm,), in_specs=in_specs, out_specs=out_specs, out_shape=out_shape,
        sem=("arbitrary",) if head else ("parallel",), args=args, comm=comm)


def _rows8(v):
    return functools.reduce(jnp.add, [v[8 * k:8 * k + 8] for k in range(v.shape[0] // 8)])


def _ffn_bwd_dc(dh, cp, wdown, name, comm=None, tm=512, R=64):
    S, C = cp.shape
    F = C // 2
    D = dh.shape[1]
    tm = min(tm, S)

    def body(dh_ref, cp_ref, wd_ref, dc_ref, dcb_ref, df_s, acc):
        i = pl.program_id(0)

        @pl.when(i == 0)
        def _():
            acc[...] = jnp.zeros_like(acc)

        df_s[...] = _dot_nt(dh_ref[...].astype(BF16), wd_ref[...])
        for r0 in range(0, tm, R):
            cpre = cp_ref[r0:r0 + R, :].astype(F32)
            g = cpre[:, :F]
            val = cpre[:, F:]
            sg = _sigmoid(g)
            df = df_s[r0:r0 + R, :]
            dg = df * val * (sg * (1.0 + g * (1.0 - sg)))
            dval = df * (g * sg)
            dc = jnp.concatenate([dg, dval], axis=1)
            dc_ref[r0:r0 + R, :] = dc.astype(BF16)
            acc[...] += _rows8(dc)

        @pl.when(i == S // tm - 1)
        def _():
            dcb_ref[...] = jnp.sum(acc[...], axis=0, keepdims=True)

    return _call(
        body, name=name, grid=(S // tm,),
        in_specs=[pl.BlockSpec((tm, D), lambda i: (i, 0)), pl.BlockSpec((tm, C), lambda i: (i, 0)),
                  _resident(wdown.shape)],
        out_specs=[pl.BlockSpec((tm, C), lambda i: (i, 0)), pl.BlockSpec((1, C), lambda i: (0, 0))],
        out_shape=[jax.ShapeDtypeStruct((S, C), BF16), jax.ShapeDtypeStruct((1, C), F32)],
        scratch_shapes=[pltpu.VMEM((tm, F), F32), pltpu.VMEM((8, C), F32)],
        sem=("arbitrary",), args=(dh, cp, wdown), comm=comm)


def _bwd_norm(dA, w4, h_in, gain, dh_out, name, conv_w=None, conv_in=None, comm=None, R=64):
    S, N = dA.shape
    nsh, D, ns = w4.shape
    conv = conv_w is not None
    tm = min(256 if conv else 512, S)
    nt = S // tm

    def finish(src_ref, w_ref, h_ref, g_ref, dho_ref, dhi_ref, dg_ref):
        dhn = _dot_nt(src_ref[:, 0:ns], w_ref[0])
        for j in range(1, nsh):
            dhn += _dot_nt(src_ref[:, j * ns:(j + 1) * ns], w_ref[j])
        dx, dgain = _rms_bwd(dhn, h_ref[...], g_ref[...])
        dg_ref[...] += dgain
        dhi_ref[...] = dho_ref[...] + dx

    def body_plain(dA_ref, w_ref, h_ref, g_ref, dho_ref, dhi_ref, dg_ref):
        @pl.when(pl.program_id(0) == 0)
        def _():
            dg_ref[...] = jnp.zeros_like(dg_ref)

        finish(dA_ref, w_ref, h_ref, g_ref, dho_ref, dhi_ref, dg_ref)

    def body_conv(dc_ref, halo_ref, cw_ref, a_ref, w_ref, h_ref, g_ref, dho_ref, dhi_ref, dg_ref, da_ref, dcw_ref,
                  dc32, acc):
        i = pl.program_id(0)

        @pl.when(i == 0)
        def _():
            dg_ref[...] = jnp.zeros_like(dg_ref)
            acc[...] = jnp.zeros_like(acc)

        dc32[0:tm, :] = dc_ref[...].astype(F32)
        dc32[tm:, :] = jnp.where(i < nt - 1, halo_ref[...].astype(F32), 0.0)

        for r0 in range(0, tm, R):
            X = dc32[r0:r0 + R + 8, :]
            d0 = X[:R]
            d1 = pltpu.roll(X, R + 7, 0)[:R]
            d2 = pltpu.roll(X, R + 6, 0)[:R]
            da = cw_ref[2:3, :] * d0 + cw_ref[1:2, :] * d1 + cw_ref[0:1, :] * d2
            da_ref[r0:r0 + R, :] = da.astype(BF16)
            a0 = a_ref[r0:r0 + R, :].astype(F32)
            acc[0] += _rows8(d2 * a0)
            acc[1] += _rows8(d1 * a0)
            acc[2] += _rows8(d0 * a0)
        finish(da_ref, w_ref, h_ref, g_ref, dho_ref, dhi_ref, dg_ref)

        @pl.when(i == nt - 1)
        def _():
            for k in range(3):
                dcw_ref[k:k + 1, :] = jnp.sum(acc[k], axis=0, keepdims=True)

    row = lambda width: pl.BlockSpec((tm, width), lambda i: (i, 0))
    common_in = [_resident(w4.shape), row(D), _resident((1, D)), row(D)]
    common_out = [row(D), pl.BlockSpec((1, D), lambda i: (0, 0))]
    common_shape = [jax.ShapeDtypeStruct((S, D), F32), jax.ShapeDtypeStruct((1, D), F32)]
    if not conv:
        return _call(
            body_plain, name=name, grid=(nt,),
            in_specs=[row(N)] + common_in, out_specs=common_out, out_shape=common_shape,
            sem=("arbitrary",), args=(dA, w4, h_in, gain, dh_out), comm=comm)
    hb = tm // HALO
    nhb = S // HALO
    return _call(
        body_conv, name=name, grid=(nt,),
        in_specs=[row(N), pl.BlockSpec((HALO, N), lambda i: (jnp.minimum((i + 1) * hb, nhb - 1), 0)),
                  _resident((3, N)), row(N)] + common_in,
        out_specs=common_out + [row(N), pl.BlockSpec((3, N), lambda i: (0, 0))],
        out_shape=common_shape + [jax.ShapeDtypeStruct((S, N), BF16), jax.ShapeDtypeStruct((3, N), F32)],
        scratch_shapes=[pltpu.VMEM((tm + HALO, N), F32), pltpu.VMEM((3, 8, N), F32)],
        sem=("arbitrary",), args=(dA, dA, conv_w, conv_in, w4, h_in, gain, dh_out), comm=comm)


def _rel_buckets_flat():
    q = np.arange(CHUNK)[:, None] + CHUNK
    k = np.arange(2 * CHUNK)[None, :]
    n = np.maximum(q - k, 0)
    max_exact = REL_BUCKETS // 2
    large = max_exact + (np.log(np.maximum(n, 1).astype(np.float32) / max_exact)
                         / math.log(REL_MAX_DIST / max_exact) * (REL_BUCKETS - max_exact)).astype(np.int32)
    large = np.minimum(large, REL_BUCKETS - 1)
    return np.where(n < max_exact, n, large).astype(np.int32).reshape(1, CHUNK * 2 * CHUNK)


def _split_bf16(x):
    hi = x.astype(BF16)
    return hi, (x - hi.astype(F32)).astype(BF16)


def _rel_bias_expand(rel_bias, bucket):
    B, H = rel_bias.shape
    n = bucket.shape[1]

    def body(rb_ref, bk_ref, o_ref):
        oh = (bk_ref[...] == lax.broadcasted_iota(jnp.int32, (B, n), 0)).astype(BF16)
        hi, lo = _split_bf16(rb_ref[...])
        o_ref[...] = _dot_tn(hi, oh) + _dot_tn(lo, oh)

    return pl.pallas_call(body, name="rel_bias_expand", out_shape=jax.ShapeDtypeStruct((H, n), F32),
                          compiler_params=_cp(None))(rel_bias, bucket)


def _rel_bias_reduce(dbias, bucket):
    H, n = dbias.shape
    B = REL_BUCKETS

    def body(db_ref, bk_ref, o_ref):
        oh = (bk_ref[...] == lax.broadcasted_iota(jnp.int32, (B, n), 0)).astype(BF16)
        hi, lo = _split_bf16(db_ref[...])
        o_ref[...] = _dot_nt(oh, hi) + _dot_nt(oh, lo)

    return pl.pallas_call(body, name="rel_bias_reduce", out_shape=jax.ShapeDtypeStruct((B, H), F32),
                          compiler_params=_cp(None))(dbias, bucket)


def _lo_mask(rows):
    return lax.broadcasted_iota(jnp.int32, (rows, LANES), 1) < HEAD_DIM


def _half_sums(y, lo):
    s_lo = jnp.sum(jnp.where(lo, y, 0.0), axis=-1, keepdims=True)
    s_hi = jnp.sum(jnp.where(lo, 0.0, y), axis=-1, keepdims=True)
    return jnp.where(lo, s_lo, s_hi)


def _half_rms(x, gain, lo):
    r = lax.rsqrt(_half_sums(x * x, lo) * (1.0 / HEAD_DIM) + EPS)
    xhat = x * r
    return xhat * gain, xhat, r


def _half_rms_bwd(dy, xhat, r, gain, lo):
    gdy = dy * gain
    dx = r * (gdy - xhat * (_half_sums(gdy * xhat, lo) * (1.0 / HEAD_DIM)))
    return dx, jnp.sum(dy * xhat, axis=0, keepdims=True)


def _dup_half(pair, e, lo):
    sw = pltpu.roll(pair, HEAD_DIM, 1)
    return jnp.where(lo, pair, sw) if e == 0 else jnp.where(lo, sw, pair)


def _band_valid(n):
    qi = lax.broadcasted_iota(jnp.int32, (KV_GROUP * CHUNK, 2 * CHUNK), 0) & (CHUNK - 1)
    ki = lax.broadcasted_iota(jnp.int32, (KV_GROUP * CHUNK, 2 * CHUNK), 1)
    dist = qi + CHUNK - ki
    return (dist >= 0) & (dist < CHUNK) & ((n > 0) | (ki >= CHUNK))


def _stack_heads(a, b, lo):
    return jnp.concatenate([jnp.where(lo, a, 0.0), jnp.where(lo, 0.0, a), jnp.where(lo, b, 0.0), jnp.where(lo, 0.0, b)],
                           axis=0)


def _unstack_heads(x4, lo):
    return (jnp.where(lo, x4[0:CHUNK], x4[CHUNK:2 * CHUNK]),
            jnp.where(lo, x4[2 * CHUNK:3 * CHUNK], x4[3 * CHUNK:]))


def _sink_col(sink_ref, hk):
    row = lax.broadcasted_iota(jnp.int32, (KV_GROUP * CHUNK, 1), 0)
    col = jnp.full((KV_GROUP * CHUNK, 1), sink_ref[KV_GROUP * hk + KV_GROUP - 1], F32)
    for r in range(KV_GROUP - 2, -1, -1):
        col = jnp.where(row < (r + 1) * CHUNK, sink_ref[KV_GROUP * hk + r], col)
    return col


def _softmax_sink(s, valid, sink):
    s = jnp.where(valid, s, -jnp.inf)
    m = jnp.maximum(jnp.max(s, axis=-1, keepdims=True), sink)
    p = jnp.exp(s - m)
    esink = jnp.exp(sink - m)
    inv = 1.0 / (jnp.sum(p, axis=-1, keepdims=True) + esink)
    return p * inv, esink * inv


QW = N_HEADS * HEAD_DIM
KVW = N_KV_HEADS * HEAD_DIM


def _attn_fwd(qkv, qg2, kg2, sinks, bias, comm=None):
    S = qkv.shape[0]
    nb = S // CHUNK

    def body(cur_ref, prev_ref, qg_ref, kg_ref, sink_ref, bias_ref, o_ref, p_ref, ps_ref):
        n = pl.program_id(0)
        lo = _lo_mask(CHUNK)
        lo2 = _lo_mask(2 * CHUNK)
        valid = _band_valid(n)
        for j in range(N_KV_HEADS // 2):
            kc = slice(QW + j * LANES, QW + (j + 1) * LANES)
            vc = slice(QW + KVW + j * LANES, QW + KVW + (j + 1) * LANES)
            kpair = jnp.concatenate([prev_ref[:, j * LANES:(j + 1) * LANES], cur_ref[:, kc]], axis=0).astype(F32)
            vpair = jnp.concatenate([prev_ref[:, KVW + j * LANES:KVW + (j + 1) * LANES], cur_ref[:, vc]], axis=0).astype(F32)
            knpair, _, _ = _half_rms(kpair, kg_ref[...], lo2)
            for e in range(2):
                hk = 2 * j + e
                kdup = _dup_half(knpair, e, lo2).astype(BF16)
                vdup = _dup_half(vpair, e, lo2).astype(BF16)
                ca = slice(2 * hk * LANES, (2 * hk + 1) * LANES)
                cb = slice((2 * hk + 1) * LANES, (2 * hk + 2) * LANES)
                qna, _, _ = _half_rms(cur_ref[:, ca].astype(F32), qg_ref[...], lo)
                qnb, _, _ = _half_rms(cur_ref[:, cb].astype(F32), qg_ref[...], lo)
                qm4 = _stack_heads(qna, qnb, lo).astype(BF16)
                s = _dot_nt(qm4, kdup) * (HEAD_DIM ** -0.5) + bias_ref[hk]
                p, psink = _softmax_sink(s, valid, _sink_col(sink_ref, hk))
                pb = p.astype(BF16)
                p_ref[0, hk] = pb
                for r in range(KV_GROUP):
                    h = KV_GROUP * hk + r
                    ps_ref[:, h:h + 1] = psink[r * CHUNK:(r + 1) * CHUNK]
                oa, ob = _unstack_heads(_dot(pb, vdup), lo)
                o_ref[:, ca] = oa.astype(BF16)
                o_ref[:, cb] = ob.astype(BF16)

    pshape = (nb, N_KV_HEADS, KV_GROUP * CHUNK, 2 * CHUNK)
    return _call(
        body, name="attn_fwd", grid=(nb,),
        in_specs=[pl.BlockSpec((CHUNK, QW + 2 * KVW), lambda n: (n, 0)),
                  pl.BlockSpec((CHUNK, 2 * KVW), lambda n: (jnp.maximum(n - 1, 0), QW // (2 * KVW))),
                  _resident((1, LANES)), _resident((1, LANES)),
                  pl.BlockSpec(memory_space=pltpu.SMEM),
                  _resident(bias.shape)],
        out_specs=[pl.BlockSpec((CHUNK, QW), lambda n: (n, 0)),
                   pl.BlockSpec((1,) + pshape[1:], lambda n: (n, 0, 0, 0)),
                   pl.BlockSpec((CHUNK, LANES), lambda n: (n, 0))],
        out_shape=[jax.ShapeDtypeStruct((S, QW), BF16), jax.ShapeDtypeStruct(pshape, BF16),
                   jax.ShapeDtypeStruct((S, LANES), F32)],
        sem=("parallel",), args=(qkv, qkv, qg2, kg2, sinks, bias), comm=comm)


def _attn_bwd(qkv, do, probs, psinks, qg2, kg2, comm=None):
    S = qkv.shape[0]
    nb = S // CHUNK

    def body(cur_ref, prev_ref, do_ref, p_ref, ps_ref, qg_ref, kg_ref,
             dqkv_ref, dbias_ref, dqg_ref, dkg_ref, dsink_ref, carry, band, dsacc, gacc):
        i = pl.program_id(0)
        lo = _lo_mask(CHUNK)
        lo2 = _lo_mask(2 * CHUNK)
        lane = lax.broadcasted_iota(jnp.int32, (KV_GROUP * CHUNK, LANES), 1)

        @pl.when(i == 0)
        def _():
            dbias_ref[...] = jnp.zeros_like(dbias_ref)
            carry[...] = jnp.zeros_like(carry)
            dsacc[...] = jnp.zeros_like(dsacc)
            gacc[...] = jnp.zeros_like(gacc)

        qgain = qg_ref[...]
        kgain = kg_ref[...]
        for j in range(N_KV_HEADS // 2):
            kc = slice(QW + j * LANES, QW + (j + 1) * LANES)
            vc = slice(QW + KVW + j * LANES, QW + KVW + (j + 1) * LANES)
            kpair = jnp.concatenate([prev_ref[:, j * LANES:(j + 1) * LANES], cur_ref[:, kc]], axis=0).astype(F32)
            vpair = jnp.concatenate([prev_ref[:, KVW + j * LANES:KVW + (j + 1) * LANES], cur_ref[:, vc]], axis=0).astype(F32)
            knpair, khat, kr = _half_rms(kpair, kgain, lo2)
            dk_folds = []
            dv_folds = []
            for e in range(2):
                hk = 2 * j + e
                kdup = _dup_half(knpair, e, lo2).astype(BF16)
                vdup = _dup_half(vpair, e, lo2).astype(BF16)
                ca = slice(2 * hk * LANES, (2 * hk + 1) * LANES)
                cb = slice((2 * hk + 1) * LANES, (2 * hk + 2) * LANES)
                qna, qhata, qra = _half_rms(cur_ref[:, ca].astype(F32), qgain, lo)
                qnb, qhatb, qrb = _half_rms(cur_ref[:, cb].astype(F32), qgain, lo)
                qm4 = _stack_heads(qna, qnb, lo).astype(BF16)
                dom4 = _stack_heads(do_ref[:, ca].astype(F32), do_ref[:, cb].astype(F32), lo).astype(BF16)
                pb = p_ref[0, hk]
                p = pb.astype(F32)
                psink = jnp.concatenate([ps_ref[:, KV_GROUP * hk + r:KV_GROUP * hk + r + 1] for r in range(KV_GROUP)],
                                        axis=0)
                dp = _dot_nt(dom4, vdup)
                delta = jnp.sum(p * dp, axis=-1, keepdims=True)
                ds = p * (dp - delta)
                dbias_ref[hk] += ds
                dsacc[...] += jnp.where(lane == hk, -(psink * delta), 0.0)
                dsr = (ds * (HEAD_DIM ** -0.5)).astype(BF16)
                dqna, dqnb = _unstack_heads(_dot(dsr, kdup), lo)
                dkd = _dot_tn(dsr, qm4)
                dvd = _dot_tn(pb, dom4)
                dqa, dqga = _half_rms_bwd(dqna, qhata, qra, qgain, lo)
                dqb, dqgb = _half_rms_bwd(dqnb, qhatb, qrb, qgain, lo)
                gacc[0:1, :] += dqga + dqgb
                dqkv_ref[:, ca] = dqa.astype(BF16)
                dqkv_ref[:, cb] = dqb.astype(BF16)
                dk_folds.append(dkd + pltpu.roll(dkd, HEAD_DIM, 1))
                dv_folds.append(dvd + pltpu.roll(dvd, HEAD_DIM, 1))
            dkn = jnp.where(lo2, dk_folds[0], dk_folds[1])
            dk, dkg = _half_rms_bwd(dkn, khat, kr, kgain, lo2)
            gacc[1:2, :] += dkg
            band[:, j * LANES:(j + 1) * LANES] = dk
            band[:, KVW + j * LANES:KVW + (j + 1) * LANES] = jnp.where(lo2, dv_folds[0], dv_folds[1])
        dqkv_ref[:, QW:] = (band[CHUNK:, :] + carry[...]).astype(BF16)
        carry[...] = band[0:CHUNK, :]

        @pl.when(i == nb - 1)
        def _():
            g = gacc[...]
            g = g + pltpu.roll(g, HEAD_DIM, 1)
            dqg_ref[...] = g[0:1, :]
            dkg_ref[...] = g[1:2, :]
            for r in range(KV_GROUP):
                dsink_ref[r:r + 1, :] = jnp.sum(dsacc[r * CHUNK:(r + 1) * CHUNK, :], axis=0, keepdims=True)

    vec = pl.BlockSpec((1, LANES), lambda i: (0, 0))
    bshape = probs.shape[1:]
    return _call(
        body, name="attn_bwd", grid=(nb,),
        in_specs=[pl.BlockSpec((CHUNK, QW + 2 * KVW), lambda i: (nb - 1 - i, 0)),
                  pl.BlockSpec((CHUNK, 2 * KVW), lambda i: (jnp.maximum(nb - 2 - i, 0), QW // (2 * KVW))),
                  pl.BlockSpec((CHUNK, QW), lambda i: (nb - 1 - i, 0)),
                  pl.BlockSpec((1,) + bshape, lambda i: (nb - 1 - i, 0, 0, 0)),
                  pl.BlockSpec((CHUNK, LANES), lambda i: (nb - 1 - i, 0)),
                  _resident((1, LANES)), _resident((1, LANES))],
        out_specs=[pl.BlockSpec((CHUNK, QW + 2 * KVW), lambda i: (nb - 1 - i, 0)),
                   pl.BlockSpec(bshape, lambda i: (0, 0, 0)), vec, vec,
                   pl.BlockSpec((KV_GROUP, LANES), lambda i: (0, 0))],
        out_shape=[jax.ShapeDtypeStruct((S, QW + 2 * KVW), BF16),
                   jax.ShapeDtypeStruct(bshape, F32),
                   jax.ShapeDtypeStruct((1, LANES), F32), jax.ShapeDtypeStruct((1, LANES), F32),
                   jax.ShapeDtypeStruct((KV_GROUP, LANES), F32)],
        scratch_shapes=[pltpu.VMEM((CHUNK, 2 * KVW), F32), pltpu.VMEM((2 * CHUNK, 2 * KVW), F32),
                        pltpu.VMEM((KV_GROUP * CHUNK, LANES), F32), pltpu.VMEM((8, LANES), F32)],
        sem=("arbitrary",), args=(qkv, qkv, do, probs, psinks, qg2, kg2), comm=comm)


def _row_tile(rows, cols, n_arrays):
    budget = VMEM_LIMIT_V7X // 4 // (2 * n_arrays * 4 * cols)
    best = 8
    for n in range(1, rows // 8 + 1):
        if rows % n == 0 and (rows // n) % 8 == 0 and rows // n <= budget:
            best = rows // n
            break
    return best


def _adamw(g, w, m, v, name, comm=None):
    R, C = g.shape
    tr = _row_tile(R, C, 8)

    def body(g_ref, w_ref, m_ref, v_ref, d_ref, mo_ref, vo_ref, go_ref):
        gg = g_ref[...]
        go_ref[...] = gg
        mn = ADAM_B1 * m_ref[...] + (1.0 - ADAM_B1) * gg
        vn = ADAM_B2 * v_ref[...] + (1.0 - ADAM_B2) * jnp.square(gg)
        m_hat = mn / (1.0 - ADAM_B1 ** ADAM_STEP)
        v_hat = vn / (1.0 - ADAM_B2 ** ADAM_STEP)
        d_ref[...] = -ADAM_LR * (m_hat / (jnp.sqrt(v_hat) + ADAM_EPS) + ADAM_WD * w_ref[...])
        mo_ref[...] = mn
        vo_ref[...] = vn

    spec = pl.BlockSpec((tr, C), lambda i: (i, 0))
    return _call(
        body, name=name, grid=(R // tr,), in_specs=[spec] * 4, out_specs=[spec] * 4,
        out_shape=[jax.ShapeDtypeStruct((R, C), F32)] * 4, sem=("parallel",), args=(g, w, m, v), comm=comm)


def _place_shard(shards, layer, where, dtype, name):
    _, R, C = shards.shape
    tr = _row_tile(R, C, 2) if R % 8 == 0 else R

    def body(s_ref, x_ref, o_ref):
        o_ref[...] = x_ref[...].astype(dtype)

    return pl.pallas_call(
        body, name=name,
        grid_spec=pltpu.PrefetchScalarGridSpec(
            num_scalar_prefetch=1, grid=(R // tr,),
            in_specs=[pl.BlockSpec((1, tr, C), lambda i, s_ref: (layer, i, 0))],
            out_specs=pl.BlockSpec((1, tr, C), lambda i, s_ref: (s_ref[1], i, 0))),
        out_shape=jax.ShapeDtypeStruct((4, R, C), dtype),
        compiler_params=_cp(("parallel",)),
    )(where, shards)


def _pair_add(g4, rsib, where, name):
    J, R, C = g4.shape
    Rh = R // 2
    tr = _row_tile(Rh, C, 4)
    g5 = g4.reshape(J, 2, Rh, C)

    def body(s_ref, g_ref, r_ref, p_ref, q_ref):
        val = (g_ref[...].astype(F32)[0] + r_ref[...].astype(F32)).astype(BF16)
        p_ref[...] = val

        @pl.when(pl.program_id(1) == s_ref[1])
        def _():
            q_ref[...] = val

    return pl.pallas_call(
        body, name=name,
        grid_spec=pltpu.PrefetchScalarGridSpec(
            num_scalar_prefetch=1, grid=(Rh // tr, J),
            in_specs=[pl.BlockSpec((1, 1, tr, C), lambda i, j, s_ref: (j, s_ref[0], i, 0)),
                      pl.BlockSpec((1, tr, C), lambda i, j, s_ref: (j, i, 0))],
            out_specs=[pl.BlockSpec((1, tr, C), lambda i, j, s_ref: (j, i, 0)),
                       pl.BlockSpec((1, tr, C), lambda i, j, s_ref: (s_ref[1], i, 0))]),
        out_shape=[jax.ShapeDtypeStruct((J, Rh, C), BF16)] * 2,
        compiler_params=_cp(("parallel", "arbitrary")),
    )(where, g5, rsib)


def _sum_chips(q, where, dest, layer, out_shape, name):
    J, Rh, C = q.shape
    tr = _row_tile(Rh, C, 3)
    nb = Rh // tr

    def body(s_ref, q_ref, *rest):
        qq = q_ref[...].astype(F32)
        rest[-1][0] = ((qq[0] + qq[1]) + qq[2]) + qq[3]

    have = dest is not None
    return pl.pallas_call(
        body, name=name,
        grid_spec=pltpu.PrefetchScalarGridSpec(
            num_scalar_prefetch=1, grid=(nb,),
            in_specs=[pl.BlockSpec((J, tr, C), lambda i, s_ref: (0, i, 0))] + ([ANY] if have else []),
            out_specs=pl.BlockSpec((1, tr, C), lambda i, s_ref: (layer, s_ref[0] * nb + i, 0))),
        out_shape=jax.ShapeDtypeStruct(out_shape, F32),
        input_output_aliases={2: 0} if have else {},
        compiler_params=_cp(("parallel",)),
    )(*((where, q, dest) if have else (where, q)))


MESH = pl.DeviceIdType.MESH
ANY = pl.BlockSpec(memory_space=pl.ANY)


def _place():
    x, y, c = lax.axis_index("x"), lax.axis_index("y"), lax.axis_index("c")
    others = [(1 - x, y), (x, 1 - y), (1 - x, 1 - y)]
    return x, y, c, 2 * x + y, others, [2 * ox + oy for ox, oy in others]


def _gather_comm(items):
    n = len(items)
    placed = [it[0] for it in items]
    split = [it[3] for it in items]

    def rows(t, ref, half):
        _, lo, hi, _ = items[t]
        if not split[t]:
            return ref if (lo, hi) == (0, placed[t].shape[1]) else ref.at[pl.ds(lo, hi - lo), :]
        rh = (hi - lo) // 2
        return ref.at[pl.ds(lo + half * rh, rh), :]

    def sends(outs, sems):
        send, recv = sems[0], sems[1]
        x, y, c, me, others, okey = _place()
        cps = []
        for t in range(n):
            mine = rows(t, outs[t].at[me], c)
            for j, (ox, oy) in enumerate(others):
                cps.append(pltpu.make_async_remote_copy(
                    src_ref=mine, dst_ref=mine,
                    send_sem=send.at[t, j], recv_sem=recv.at[t, j], device_id=(ox, oy, c), device_id_type=MESH))
        return cps

    def start(srcs, outs, news, sems):
        for cp in sends(outs, sems):
            cp.start()

    def forwards(outs, sems):
        fsend, frecv = sems[2], sems[3]
        x, y, c, me, others, okey = _place()
        cps = []
        for t in range(n):
            if split[t]:
                for j in range(3):
                    landed = rows(t, outs[t].at[okey[j]], c)
                    cps.append(pltpu.make_async_remote_copy(
                        src_ref=landed, dst_ref=landed, send_sem=fsend.at[t, j], recv_sem=frecv.at[t, j],
                        device_id=(x, y, 1 - c), device_id_type=MESH))
        return cps

    def middle(srcs, outs, news, sems):
        send, recv = sems[0], sems[1]
        x, y, c, me, others, okey = _place()
        for t in range(n):
            for j in range(3):
                landed = rows(t, outs[t].at[okey[j]], c)
                pltpu.make_async_remote_copy(
                    src_ref=landed, dst_ref=landed, send_sem=send.at[t, j], recv_sem=recv.at[t, j],
                    device_id=(x, y, 1 - c), device_id_type=MESH).wait_recv()
        for cp in forwards(outs, sems):
            cp.start()

    def finish(srcs, outs, news, sems):
        fsend, frecv = sems[2], sems[3]
        x, y, c, me, others, okey = _place()
        for t in range(n):
            if split[t]:
                for j in range(3):
                    theirs = rows(t, outs[t].at[okey[j]], 1 - c)
                    pltpu.make_async_remote_copy(
                        src_ref=theirs, dst_ref=theirs, send_sem=fsend.at[t, j], recv_sem=frecv.at[t, j],
                        device_id=(x, y, 1 - c), device_id_type=MESH).wait_recv()
        for cp in sends(outs, sems) + forwards(outs, sems):
            cp.wait_send()

    return _Comm([], placed, [], [pltpu.SemaphoreType.DMA((n, 3))] * 4, start, finish, middle)


def _pair_exchange_comm(gs):
    n = len(gs)

    def copies(ins, outs, sems):
        send, recv = sems
        x, y, c, _, _, _ = _place()
        cps = []
        for t in range(n):
            rh = gs[t].shape[1] // 2
            cps.append(pltpu.make_async_remote_copy(
                src_ref=ins[t].at[:, pl.ds((1 - c) * rh, rh), :], dst_ref=outs[t],
                send_sem=send.at[t], recv_sem=recv.at[t], device_id=(x, y, 1 - c), device_id_type=MESH))
        return cps

    def start(ins, zones, outs, sems):
        for cp in copies(ins, outs, sems):
            cp.start()

    def finish(ins, zones, outs, sems):
        for cp in copies(ins, outs, sems):
            cp.wait()

    news = [jax.ShapeDtypeStruct((4, g.shape[1] // 2, g.shape[2]), g.dtype) for g in gs]
    return _Comm(gs, [], news, [pltpu.SemaphoreType.DMA((n,))] * 2, start, finish)


def _chip_scatter_comm(ps, qs):
    n = len(ps)

    def sends(ins, outs, sems):
        send, recv = sems
        x, y, c, me, others, okey = _place()
        return [pltpu.make_async_remote_copy(
            src_ref=ins[t].at[okey[j]], dst_ref=outs[t].at[me],
            send_sem=send.at[t, j], recv_sem=recv.at[t, j], device_id=(ox, oy, c), device_id_type=MESH)
            for t in range(n) for j, (ox, oy) in enumerate(others)]

    def start(ins, outs, news, sems):
        for cp in sends(ins, outs, sems):
            cp.start()

    def finish(ins, outs, news, sems):
        send, recv = sems
        x, y, c, me, others, okey = _place()
        for t in range(n):
            for j in range(3):
                slot = outs[t].at[okey[j]]
                pltpu.make_async_remote_copy(
                    src_ref=slot, dst_ref=slot, send_sem=send.at[t, j], recv_sem=recv.at[t, j],
                    device_id=(x, y, c), device_id_type=MESH).wait_recv()
        for cp in sends(ins, outs, sems):
            cp.wait_send()

    return _Comm(ps, qs, [], [pltpu.SemaphoreType.DMA((n, 3))] * 2, start, finish)


def _half_exchange_comm(arrs, layers=None):
    n = len(arrs)
    items = [(t, layer) for t in range(n) for layer in (range(arrs[t].shape[0]) if layers is None else layers[t])]

    def sends(outs, sems):
        send, recv = sems
        x, y, c, _, _, _ = _place()
        cps = []
        for k, (t, layer) in enumerate(items):
            rh = arrs[t].shape[1] // 2
            mine = outs[t].at[layer, pl.ds(c * rh, rh), :]
            cps.append(pltpu.make_async_remote_copy(
                src_ref=mine, dst_ref=mine, send_sem=send.at[k], recv_sem=recv.at[k],
                device_id=(x, y, 1 - c), device_id_type=MESH))
        return cps

    def start(srcs, outs, news, sems):
        for cp in sends(outs, sems):
            cp.start()

    def finish(srcs, outs, news, sems):
        send, recv = sems
        x, y, c, _, _, _ = _place()
        for k, (t, layer) in enumerate(items):
            rh = arrs[t].shape[1] // 2
            theirs = outs[t].at[layer, pl.ds((1 - c) * rh, rh), :]
            pltpu.make_async_remote_copy(
                src_ref=theirs, dst_ref=theirs, send_sem=send.at[k], recv_sem=recv.at[k],
                device_id=(x, y, 1 - c), device_id_type=MESH).wait_recv()
        for cp in sends(outs, sems):
            cp.wait_send()

    return _Comm([], arrs, [], [pltpu.SemaphoreType.DMA((len(items),))] * 2, start, finish)


SMALL_COLS = 1024
SMALL_PIECE_ROWS = 48


def _allreduce_small(buf, comm=None):
    pr = SMALL_PIECE_ROWS
    flips = [(d >> 2 & 1, d >> 1 & 1, d & 1) for d in range(1, 8)]

    def body(x_ref, o_ref, rbuf, send1, recv1, send2, recv2):
        x, y, c = lax.axis_index("x"), lax.axis_index("y"), lax.axis_index("c")
        me = 4 * x + 2 * y + c
        peers = [(x ^ fx, y ^ fy, c ^ fc) for fx, fy, fc in flips]
        pid = [4 * px + 2 * py + pc for px, py, pc in peers]

        def piece(ref, p):
            return ref.at[pl.ds(pl.multiple_of(p * pr, 8), pr), :]

        cps = []
        for d in range(7):
            cp = pltpu.make_async_remote_copy(
                src_ref=piece(x_ref, pid[d]), dst_ref=rbuf.at[d + 1],
                send_sem=send1.at[d], recv_sem=recv1.at[d], device_id=peers[d], device_id_type=MESH)
            cp.start()
            cps.append(cp)
        acc = piece(x_ref, me)[...]
        for d in range(7):
            cps[d].wait_recv()
            acc = acc + rbuf[d + 1]
        piece(o_ref, me)[...] = acc
        out = []
        for d in range(7):
            cp = pltpu.make_async_remote_copy(
                src_ref=piece(o_ref, me), dst_ref=piece(o_ref, me),
                send_sem=send2.at[d], recv_sem=recv2.at[d], device_id=peers[d], device_id_type=MESH)
            cp.start()
            out.append(cp)
        for d in range(7):
            pltpu.make_async_remote_copy(
                src_ref=piece(o_ref, pid[d]), dst_ref=piece(o_ref, pid[d]),
                send_sem=send2.at[d], recv_sem=recv2.at[d], device_id=peers[d], device_id_type=MESH).wait_recv()
        for cp in cps + out:
            cp.wait_send()

    vm = pl.BlockSpec(memory_space=pltpu.VMEM)
    return _call(
        body, name="small_allreduce", grid=(), in_specs=[vm], out_specs=vm,
        out_shape=jax.ShapeDtypeStruct(buf.shape, F32),
        scratch_shapes=[pltpu.VMEM((8, pr, SMALL_COLS), F32)] + [pltpu.SemaphoreType.DMA((7,))] * 4,
        args=(buf,), comm=comm)


def _rows_of(shape):
    return -(-math.prod(shape) // (8 * SMALL_COLS)) * 8


def _pack(arrays, rows):
    parts = []
    for a in arrays:
        r = _rows_of(a.shape)
        parts.append(jnp.pad(a.reshape(-1), (0, r * SMALL_COLS - a.size)).reshape(r, SMALL_COLS))
    used = sum(p.shape[0] for p in parts)
    if rows > used:
        parts.append(jnp.zeros((rows - used, SMALL_COLS), F32))
    return jnp.concatenate(parts, axis=0)


def _unpack(buf, shapes):
    out, off = [], 0
    for s in shapes:
        r = _rows_of(s)
        out.append(buf[off:off + r].reshape(-1)[:math.prod(s)].reshape(s))
        off += r
    return out


BIG = ["sgu_w_in", "sgu_w_out", "attn_w_qkv", "attn_w_o", "ffn_w_up", "ffn_w_down"]
SMALL = ["mix_norm", "ffn_norm", "sgu_v_gain", "sgu_w_s", "sgu_b_s", "attn_q_gain", "attn_k_gain", "attn_sinks",
         "rel_bias", "ffn_conv_b"]
ORDER = ["mix_norm", "ffn_norm", "sgu_w_in", "sgu_v_gain", "sgu_w_s", "sgu_b_s", "sgu_w_out", "attn_w_qkv",
         "attn_q_gain", "attn_k_gain", "attn_sinks", "attn_w_o", "rel_bias", "ffn_w_up", "ffn_conv_w", "ffn_conv_b",
         "ffn_w_down"]


def kernel(x, mix_norm, ffn_norm, sgu_w_in, sgu_v_gain, sgu_w_s, sgu_b_s, sgu_w_out, attn_w_qkv, attn_q_gain, attn_k_gain, attn_sinks, attn_w_o, rel_bias, ffn_w_up, ffn_conv_w, ffn_conv_b, ffn_w_down, loss_target, m_mix_norm, m_ffn_norm, m_sgu_w_in, m_sgu_v_gain, m_sgu_w_s, m_sgu_b_s, m_sgu_w_out, m_attn_w_qkv, m_attn_q_gain, m_attn_k_gain, m_attn_sinks, m_attn_w_o, m_rel_bias, m_ffn_w_up, m_ffn_conv_w, m_ffn_conv_b, m_ffn_w_down, v_mix_norm, v_ffn_norm, v_sgu_w_in, v_sgu_v_gain, v_sgu_w_s, v_sgu_b_s, v_sgu_w_out, v_attn_w_qkv, v_attn_q_gain, v_attn_k_gain, v_attn_sinks, v_attn_w_o, v_rel_bias, v_ffn_w_up, v_ffn_conv_w, v_ffn_conv_b, v_ffn_w_down):
    w = dict(mix_norm=mix_norm, ffn_norm=ffn_norm, sgu_w_in=sgu_w_in, sgu_v_gain=sgu_v_gain, sgu_w_s=sgu_w_s,
             sgu_b_s=sgu_b_s, sgu_w_out=sgu_w_out, attn_w_qkv=attn_w_qkv, attn_q_gain=attn_q_gain,
             attn_k_gain=attn_k_gain, attn_sinks=attn_sinks, attn_w_o=attn_w_o, rel_bias=rel_bias, ffn_w_up=ffn_w_up,
             ffn_conv_w=ffn_conv_w, ffn_conv_b=ffn_conv_b, ffn_w_down=ffn_w_down)
    mom = dict(mix_norm=m_mix_norm, ffn_norm=m_ffn_norm, sgu_w_in=m_sgu_w_in, sgu_v_gain=m_sgu_v_gain,
               sgu_w_s=m_sgu_w_s, sgu_b_s=m_sgu_b_s, sgu_w_out=m_sgu_w_out, attn_w_qkv=m_attn_w_qkv,
               attn_q_gain=m_attn_q_gain, attn_k_gain=m_attn_k_gain, attn_sinks=m_attn_sinks, attn_w_o=m_attn_w_o,
               rel_bias=m_rel_bias, ffn_w_up=m_ffn_w_up, ffn_conv_w=m_ffn_conv_w, ffn_conv_b=m_ffn_conv_b,
               ffn_w_down=m_ffn_w_down)
    var = dict(mix_norm=v_mix_norm, ffn_norm=v_ffn_norm, sgu_w_in=v_sgu_w_in, sgu_v_gain=v_sgu_v_gain,
               sgu_w_s=v_sgu_w_s, sgu_b_s=v_sgu_b_s, sgu_w_out=v_sgu_w_out, attn_w_qkv=v_attn_w_qkv,
               attn_q_gain=v_attn_q_gain, attn_k_gain=v_attn_k_gain, attn_sinks=v_attn_sinks, attn_w_o=v_attn_w_o,
               rel_bias=v_rel_bias, ffn_w_up=v_ffn_w_up, ffn_conv_w=v_ffn_conv_w, ffn_conv_b=v_ffn_conv_b,
               ffn_w_down=v_ffn_w_down)
    chip = 2 * lax.axis_index("x") + lax.axis_index("y")
    core = lax.axis_index("c")

    where = jnp.stack([core, chip]).astype(jnp.int32)
    names = ["sgu_w_in", "sgu_w_out", "attn_w_qkv", "attn_w_o", "ffn_w_up0", "ffn_w_up1", "ffn_w_down0", "ffn_w_down1"]
    shards = [(sgu_w_in, 0), (sgu_w_out, 0), (attn_w_qkv, 0), (attn_w_o, 0), (ffn_w_up, 0), (ffn_w_up, 1),
              (ffn_w_down, 0), (ffn_w_down, 1)]
    T = {nm: _place_shard(s, l, where, BF16, "place_" + nm) for (s, l), nm in zip(shards, names)}
    for l in range(2):
        T["conv_w%d" % l] = _place_shard(ffn_conv_w, l, where, F32, "place_conv_w%d" % l)

    def gather(keys):
        items = []
        for k in keys:
            k, lo, hi = (k, 0, None) if isinstance(k, str) else k
            items.append((T[k], lo, T[k].shape[1] if hi is None else hi, not k.startswith("conv")))
        return _gather_comm(items)

    def gathered(keys, res):
        for k, a in zip(keys, res[0]):
            T[k if isinstance(k, str) else k[0]] = a

    D = x.shape[2]
    first = ["sgu_w_in", "conv_w0", "conv_w1"]
    gathered(first, _run_comm(gather(first), "gather_first"))
    unshard_cols = lambda a: jnp.transpose(a, (1, 0, 2)).reshape(a.shape[1], -1)
    cw = [unshard_cols(T["conv_w0"]), unshard_cols(T["conv_w1"])]
    cb = ffn_conv_b
    flat = lambda k: T[k].reshape(-1, D)
    x2, target = x[0], loss_target[0]
    bucket = jnp.asarray(_rel_buckets_flat())
    wtril, wtrilT = _sgu_prep(sgu_w_s[0])
    bT = sgu_b_s[0].T
    bias = _rel_bias_expand(rel_bias, bucket).reshape(N_KV_HEADS, KV_GROUP * CHUNK, 2 * CHUNK)
    qg2 = jnp.tile(attn_q_gain, (1, 2))
    kg2 = jnp.tile(attn_k_gain, (1, 2))
    sinks = attn_sinks.reshape(N_HEADS)
    mix0, mix1 = mix_norm[0:1], mix_norm[1:2]
    fn0, fn1 = ffn_norm[0:1], ffn_norm[1:2]

    ks = ["sgu_w_out", ("ffn_w_up0", 0, 384)]
    (hn0, z, cdf), r = _norm_matmul(x2, mix0, T["sgu_w_in"], "sgu_in", comm=gather(ks), with_cdf=True)
    gathered(ks, r)
    ks = [("ffn_w_up0", 384, None)]
    (yp, h1), r = _sgu_fwd(z, cdf, x2, sgu_v_gain, wtril, bT, flat("sgu_w_out"), comm=gather(ks))
    gathered(ks, r)
    ks = ["ffn_w_down0", "attn_w_qkv", "attn_w_o"]
    (hf0, a0, cp0), r = _norm_matmul(h1, fn0, T["ffn_w_up0"], "ffn0_up", comm=gather(ks), conv=(cw[0], cb[0:1]))
    gathered(ks, r)
    ks = [("ffn_w_up1", 0, 384)]
    (f0, h2), r = _ffn_fwd(cp0, h1, flat("ffn_w_down0"), "ffn0_fwd", comm=gather(ks))
    gathered(ks, r)
    (hn1, qkv), _ = _norm_matmul(h2, mix1, T["attn_w_qkv"], "attn_qkv")
    ks = [("ffn_w_up1", 384, None)]
    (o, probs, psinks), r = _attn_fwd(qkv, qg2, kg2, sinks, bias, comm=gather(ks))
    gathered(ks, r)
    h3, _ = _matmul_res(o, flat("attn_w_o"), h2, "attn_out")
    ks = ["ffn_w_down1"]
    (hf1, a1, cp1), r = _norm_matmul(h3, fn1, T["ffn_w_up1"], "ffn1_up", comm=gather(ks), conv=(cw[1], cb[1:2]))
    gathered(ks, r)
    (f1, dh4, loss_local), _ = _ffn_fwd(cp1, h3, flat("ffn_w_down1"), "ffn1_fwd", target=target)

    G, RS, PQ, QD, halves, grads = {}, {}, {}, {}, {}, {}
    dest_of = {"sgu_w_in": ("sgu_w_in", 0), "sgu_w_out": ("sgu_w_out", 0), "attn_w_qkv": ("attn_w_qkv", 0),
               "attn_w_o": ("attn_w_o", 0), "ffn_w_up0": ("ffn_w_up", 0), "ffn_w_up1": ("ffn_w_up", 1),
               "ffn_w_down0": ("ffn_w_down", 0), "ffn_w_down1": ("ffn_w_down", 1)}

    def px(keys):
        return _pair_exchange_comm([G[k] for k in keys])

    def px_done(keys, res):
        for k, a in zip(keys, res[1]):
            RS[k] = a
            PQ[k] = _pair_add(G[k], a, where, "pair_add_" + k)

    def sc(keys):
        return _chip_scatter_comm([PQ[k][0] for k in keys], [PQ[k][1] for k in keys])

    def sc_done(keys, res):
        for k, a in zip(keys, res[0]):
            wk, layer = dest_of[k]
            halves[wk] = _sum_chips(a, where, halves.get(wk), layer, w[wk].shape, "sum_chips_" + k)

    nup = ffn_w_up.shape[2]
    ndown = ffn_w_down.shape[1]
    rows4 = lambda a: a.reshape(4, ndown, D)
    (dc1, dcb1), _ = _ffn_bwd_dc(dh4, cp1, flat("ffn_w_down1"), "ffn1_bwd_dc")
    gw, _ = _matmul_tn(f1, dh4, "ffn1_dw_down", ka=2 * ndown, nb=D, out_dtype=BF16)
    G["ffn_w_down1"] = rows4(gw)
    (dh3, dfn1, da1, dcw1), _ = _bwd_norm(dc1, T["ffn_w_up1"], h3, fn1, dh4, "ffn1_bwd_in", conv_w=cw[1], conv_in=a1)
    G["ffn_w_up1"], _ = _matmul_tn(hf1, da1, "ffn1_dw_up", ka=D, nb=nup, out_dtype=BF16)
    ks = ["ffn_w_down1", "ffn_w_up1"]
    G["attn_w_o"], r = _matmul_tn(o, dh3, "attn_dw_o", ka=QW // 4, nb=D, out_dtype=BF16, comm=px(ks))
    px_done(ks, r)
    do, _ = _matmul_nt(dh3, flat("attn_w_o"), "attn_bwd_out")
    (dqkv, dbias, dqg, dkg, dsink), r = _attn_bwd(qkv, do, probs, psinks, qg2, kg2, comm=sc(ks))
    sc_done(ks, r)
    G["attn_w_qkv"], _ = _matmul_tn(hn1, dqkv, "attn_dw_qkv", ka=D, nb=dqkv.shape[1] // 4, out_dtype=BF16)
    ks = ["attn_w_o", "attn_w_qkv"]
    (dh2, dmix1), r = _bwd_norm(dqkv, T["attn_w_qkv"], h2, mix1, dh3, "attn_bwd_in", comm=px(ks))
    px_done(ks, r)
    (dc0, dcb0), r = _ffn_bwd_dc(dh2, cp0, flat("ffn_w_down0"), "ffn0_bwd_dc", comm=sc(ks))
    sc_done(ks, r)
    gw, _ = _matmul_tn(f0, dh2, "ffn0_dw_down", ka=2 * ndown, nb=D, out_dtype=BF16)
    G["ffn_w_down0"] = rows4(gw)
    (dh1, dfn0, da0, dcw0), r = _bwd_norm(dc0, T["ffn_w_up0"], h1, fn0, dh2, "ffn0_bwd_in", conv_w=cw[0], conv_in=a0,
                                          comm=px(["ffn_w_down0"]))
    px_done(["ffn_w_down0"], r)
    G["ffn_w_up0"], r = _matmul_tn(hf0, da0, "ffn0_dw_up", ka=D, nb=nup, out_dtype=BF16, comm=sc(["ffn_w_down0"]))
    sc_done(["ffn_w_down0"], r)
    ks = ["ffn_w_up0"]
    G["sgu_w_out"], r = _matmul_tn(yp, dh1, "sgu_dw_out", ka=yp.shape[1] // 4, nb=D, out_dtype=BF16, comm=px(ks))
    px_done(ks, r)
    both = [sc(ks), px(["sgu_w_out"])]
    (dz, dws, dbT, dvg), r = _sgu_bwd(dh1, z, cdf, sgu_v_gain, wtril, wtrilT, bT, flat("sgu_w_out"),
                                      comm=_join(both))
    r = _split(both, r)
    sc_done(ks, r[0])
    px_done(["sgu_w_out"], r[1])
    done = ["attn_w_qkv", "attn_w_o", "ffn_w_up", "ffn_w_down"]
    both = [sc(["sgu_w_out"]), _half_exchange_comm([halves[k] for k in done])]
    G["sgu_w_in"], r = _matmul_tn(hn0, dz, "sgu_dw_in", ka=D, nb=dz.shape[1] // 4, out_dtype=BF16, comm=_join(both))
    r = _split(both, r)
    sc_done(["sgu_w_out"], r[0])
    for k, a in zip(done, r[1][0]):
        grads[k] = a
    px_done(["sgu_w_in"], _run_comm(px(["sgu_w_in"]), "grad_last_pair_exchange"))
    (grad_x, dmix0), r = _bwd_norm(dz, T["sgu_w_in"], x2, mix0, dh1, "sgu_bwd_in", comm=sc(["sgu_w_in"]))
    sc_done(["sgu_w_in"], r)

    g = dict(mix_norm=jnp.concatenate([dmix0, dmix1], axis=0), ffn_norm=jnp.concatenate([dfn0, dfn1], axis=0),
             sgu_v_gain=dvg, sgu_w_s=dws, sgu_b_s=dbT.T, attn_q_gain=dqg[:, :HEAD_DIM], attn_k_gain=dkg[:, :HEAD_DIM],
             attn_sinks=dsink[:, :N_KV_HEADS].T.reshape(1, N_HEADS),
             rel_bias=_rel_bias_reduce(dbias.reshape(N_HEADS, CHUNK * 2 * CHUNK), bucket),
             ffn_conv_b=jnp.concatenate([dcb0, dcb1], axis=0))
    small_list = [g[k].reshape(w[k].shape) for k in SMALL] + [jnp.stack([dcw0, dcw1]), loss_local]
    small_shapes = [a.shape for a in small_list]
    done = ["sgu_w_in", "sgu_w_out"]
    red, r = _allreduce_small(_pack(small_list, 8 * SMALL_PIECE_ROWS),
                              comm=_half_exchange_comm([halves[k] for k in done]))
    for k, a in zip(done, r[0]):
        grads[k] = a
    red = _unpack(red, small_shapes)
    for k, a in zip(SMALL, red):
        grads[k] = a
    grads["ffn_conv_w"] = lax.dynamic_slice_in_dim(red[-2], chip * ffn_conv_w.shape[2], ffn_conv_w.shape[2], axis=2)
    loss = red[-1][0, 0]

    delta, new_m, new_v = {}, {}, {}
    two = lambda a: a.reshape(-1, a.shape[-1])
    for k in BIG:
        (d2, m2, v2, g2), _ = _adamw(two(grads[k]), two(w[k]), two(mom[k]), two(var[k]), "adamw_" + k)
        delta[k], new_m[k], new_v[k], grads[k] = (a.reshape(w[k].shape) for a in (d2, m2, v2, g2))
    sm = SMALL + ["ffn_conv_w"]
    sm_shapes = [w[k].shape for k in sm]
    rows = sum(_rows_of(s) for s in sm_shapes)
    (d2, m2, v2, _), _ = _adamw(_pack([grads[k] for k in sm], rows), _pack([w[k] for k in sm], rows),
                                _pack([mom[k] for k in sm], rows), _pack([var[k] for k in sm], rows), "adamw_small")
    for dst, buf in ((delta, d2), (new_m, m2), (new_v, v2)):
        for k, a in zip(sm, _unpack(buf, sm_shapes)):
            dst[k] = a

    return (loss, grad_x[None], *[grads[k] for k in ORDER], *[delta[k] for k in ORDER],
            *[new_m[k] for k in ORDER], *[new_v[k] for k in ORDER])
```

```python
import functools
import math

import numpy as np
import jax
import jax.numpy as jnp
from jax import lax
from jax.experimental import pallas as pl
from jax.experimental.pallas import tpu as pltpu

F32 = jnp.float32
BF16 = jnp.bfloat16

EPS = 1e-6
CHUNK = 128
SGU_GROUPS = 16
HEAD_DIM = 64
N_HEADS = 16
N_KV_HEADS = 4
KV_GROUP = N_HEADS // N_KV_HEADS
REL_BUCKETS = 32
REL_MAX_DIST = 128
LANES = 128
HALO = 16

ADAM_LR = 0.001
ADAM_B1 = 0.9
ADAM_B2 = 0.999
ADAM_EPS = 1e-08
ADAM_WD = 0.01
ADAM_STEP = 10

VMEM_LIMIT_V7X = 56 * 1024 * 1024

_SQRT_HALF = math.sqrt(0.5)
_INV_SQRT_2PI = 1.0 / math.sqrt(2.0 * math.pi)


def _cp(sem):
    return pltpu.CompilerParams(dimension_semantics=sem, vmem_limit_bytes=VMEM_LIMIT_V7X)


def _resident(shape):
    nd = len(shape)
    return pl.BlockSpec(shape, lambda *_: (0,) * nd, pipeline_mode=pl.Buffered(1))


class _Comm:
    def __init__(self, srcs, zones, news, sems, start, finish, middle=None):
        self.srcs, self.zones, self.news, self.sems = list(srcs), list(zones), list(news), list(sems)
        self.start, self.finish = start, finish
        self.middle = middle if middle is not None else (lambda srcs, zones, news, sems: None)


def _join(comms):
    comms = [c for c in comms if c is not None]
    if not comms:
        return None

    def part(seq, attr):
        out, k = [], 0
        for c in comms:
            n = len(getattr(c, attr))
            out.append(seq[k:k + n])
            k += n
        return out

    def run(which):
        def f(srcs, zones, news, sems):
            for c, a, b, d, e in zip(comms, part(srcs, "srcs"), part(zones, "zones"), part(news, "news"), part(sems, "sems")):
                getattr(c, which)(a, b, d, e)
        return f

    cat = lambda attr: [v for c in comms for v in getattr(c, attr)]
    return _Comm(cat("srcs"), cat("zones"), cat("news"), cat("sems"), run("start"), run("finish"), run("middle"))


def _split(comms, res):
    zones, news = res
    out, kz, kn = [], 0, 0
    for c in comms:
        out.append((zones[kz:kz + len(c.zones)], news[kn:kn + len(c.news)]))
        kz += len(c.zones)
        kn += len(c.news)
    return out


def _call(body, *, name, grid, in_specs, out_specs, out_shape, args, scratch_shapes=(), sem=None, comm=None):
    if comm is None:
        res = pl.pallas_call(body, name=name, grid=grid, in_specs=in_specs, out_specs=out_specs, out_shape=out_shape,
                             scratch_shapes=list(scratch_shapes), compiler_params=_cp(sem))(*args)
        return res, None
    single = not isinstance(out_shape, (list, tuple))
    out_specs_l = [out_specs] if single else list(out_specs)
    out_shape_l = [out_shape] if single else list(out_shape)
    n_in, n_out, n_scr = len(in_specs), len(out_shape_l), len(scratch_shapes)
    ns, nz, nn = len(comm.srcs), len(comm.zones), len(comm.news)

    def wrapped(*refs):
        k = n_in
        ins, srcs = refs[:k], refs[k:k + ns]
        k += ns + nz
        outs, zones, news = refs[k:k + n_out], refs[k + n_out:k + n_out + nz], refs[k + n_out + nz:k + n_out + nz + nn]
        k += n_out + nz + nn
        scr, sems = refs[k:k + n_scr], refs[k + n_scr:]
        if not grid:
            comm.start(srcs, zones, news, sems)
            body(*ins, *outs, *scr)
            comm.middle(srcs, zones, news, sems)
            comm.finish(srcs, zones, news, sems)
            return
        first = functools.reduce(jnp.logical_and, [pl.program_id(a) == 0 for a in range(len(grid))])
        last = functools.reduce(jnp.logical_and, [pl.program_id(a) == grid[a] - 1 for a in range(len(grid))])
        early = len(grid) == 1 and grid[0] >= 2
        mid_step = grid[0] - (2 if grid[0] >= 8 else 1)

        @pl.when(first)
        def _():
            comm.start(srcs, zones, news, sems)

        if early:
            @pl.when(pl.program_id(0) == mid_step)
            def _():
                comm.middle(srcs, zones, news, sems)

        body(*ins, *outs, *scr)

        @pl.when(last)
        def _():
            if not early:
                comm.middle(srcs, zones, news, sems)
            comm.finish(srcs, zones, news, sems)

    res = pl.pallas_call(
        wrapped, name=name, grid=grid,
        in_specs=list(in_specs) + [ANY] * (ns + nz), out_specs=out_specs_l + [ANY] * (nz + nn),
        out_shape=out_shape_l + [jax.ShapeDtypeStruct(z.shape, z.dtype) for z in comm.zones] + comm.news,
        input_output_aliases={n_in + ns + i: n_out + i for i in range(nz)},
        scratch_shapes=list(scratch_shapes) + comm.sems,
        compiler_params=_cp(("arbitrary",) * len(grid)),
    )(*args, *comm.srcs, *comm.zones)
    main = res[0] if single else list(res[:n_out])
    return main, (list(res[n_out:n_out + nz]), list(res[n_out + nz:]))


def _run_comm(comm, name):
    ns, nz, nn = len(comm.srcs), len(comm.zones), len(comm.news)

    def body(*refs):
        srcs, zones, news, sems = refs[:ns], refs[ns + nz:ns + 2 * nz], refs[ns + 2 * nz:ns + 2 * nz + nn], refs[ns + 2 * nz + nn:]
        comm.start(srcs, zones, news, sems)
        comm.middle(srcs, zones, news, sems)
        comm.finish(srcs, zones, news, sems)

    res = pl.pallas_call(
        body, name=name, in_specs=[ANY] * (ns + nz), out_specs=[ANY] * (nz + nn),
        out_shape=[jax.ShapeDtypeStruct(z.shape, z.dtype) for z in comm.zones] + comm.news,
        input_output_aliases={ns + i: i for i in range(nz)}, scratch_shapes=comm.sems,
    )(*comm.srcs, *comm.zones)
    return list(res[:nz]), list(res[nz:])


def _dot(a, b):
    return jnp.dot(a, b, preferred_element_type=F32)


def _dot_nt(a, b):
    return lax.dot_general(a, b, (((1,), (1,)), ((), ())), preferred_element_type=F32)


def _dot_tn(a, b):
    return lax.dot_general(a, b, (((0,), (0,)), ((), ())), preferred_element_type=F32)


def _normal_cdf(x):
    return 0.5 * (1.0 + lax.erf(x * _SQRT_HALF))


def _gelu_and_grad(x, cdf):
    return x * cdf, cdf + x * jnp.exp(-0.5 * x * x) * _INV_SQRT_2PI


def _sigmoid(x):
    return 0.5 * jnp.tanh(0.5 * x) + 0.5


def _rms_bwd(dy, x, gain):
    r = lax.rsqrt(jnp.mean(x * x, axis=-1, keepdims=True) + EPS)
    xhat = x * r
    gdy = dy * gain
    dx = r * (gdy - xhat * jnp.mean(gdy * xhat, axis=-1, keepdims=True))
    return dx, jnp.sum(dy * xhat, axis=0, keepdims=True)


def _norm_matmul(x, gain, w4, name, comm=None, with_cdf=False, conv=None, tm=1024):
    S, D = x.shape
    nsh, _, ns = w4.shape
    extra = with_cdf or conv is not None
    tm = min(512 if extra else tm, S)

    def body(x_ref, g_ref, w_ref, *rest):
        if conv is not None:
            cw_ref, cb_ref, hn_ref, o_ref, e_ref, tail = rest
        else:
            hn_ref, o_ref = rest[:2]
            e_ref = rest[2] if with_cdf else None
        xf = x_ref[...]
        r = lax.rsqrt(jnp.mean(xf * xf, axis=-1, keepdims=True) + EPS)
        hn = (xf * r * g_ref[...]).astype(BF16)
        hn_ref[...] = hn
        if conv is not None:
            @pl.when(pl.program_id(0) == 0)
            def _():
                tail[...] = jnp.zeros_like(tail)

        for j in range(nsh):
            cols = slice(j * ns, (j + 1) * ns)
            out = _dot(hn, w_ref[j])
            o_ref[:, cols] = out.astype(BF16)
            if with_cdf:
                e_ref[:, cols] = _normal_cdf(out).astype(BF16)
            if conv is not None:
                w0, w1, w2, b = cw_ref[0:1, cols], cw_ref[1:2, cols], cw_ref[2:3, cols], cb_ref[:, cols]
                taps = lambda v: w0 * pltpu.roll(v, 2, 0) + w1 * pltpu.roll(v, 1, 0) + w2 * v + b
                e_ref[:, cols] = taps(out).astype(BF16)
                head = jnp.concatenate([tail[:, cols], out[0:HALO]], axis=0)
                e_ref[0:HALO, cols] = taps(head)[HALO:].astype(BF16)
                tail[:, cols] = out[tm - HALO:tm]

    row = lambda width: pl.BlockSpec((tm, width), lambda i: (i, 0))
    N = nsh * ns
    in_specs = [row(D), _resident((1, D)), _resident(w4.shape)]
    args = (x, gain, w4)
    scratch = []
    if conv is not None:
        in_specs += [_resident((3, N)), _resident((1, N))]
        args += tuple(conv)
        scratch = [pltpu.VMEM((HALO, N), F32)]
    return _call(
        body, name=name, grid=(S // tm,), in_specs=in_specs,
        out_specs=[row(D), row(N)] + [row(N)] * extra,
        out_shape=[jax.ShapeDtypeStruct((S, D), BF16)] + [jax.ShapeDtypeStruct((S, N), BF16)] * (1 + extra),
        scratch_shapes=scratch, sem=("arbitrary",) if conv is not None else ("parallel",), args=args, comm=comm)


def _matmul_res(a, w, res, name, comm=None, tm=1024):
    S, K = a.shape
    N = w.shape[1]
    tm = min(tm, S)

    def body(a_ref, w_ref, r_ref, o_ref):
        o_ref[...] = r_ref[...] + _dot(a_ref[...], w_ref[...])

    return _call(
        body, name=name, grid=(S // tm,),
        in_specs=[pl.BlockSpec((tm, K), lambda i: (i, 0)), _resident(w.shape), pl.BlockSpec((tm, N), lambda i: (i, 0))],
        out_specs=pl.BlockSpec((tm, N), lambda i: (i, 0)),
        out_shape=jax.ShapeDtypeStruct((S, N), F32),
        sem=("parallel",), args=(a, w, res), comm=comm)


def _matmul_nt(dh, w, name, comm=None, tm=1024):
    S, N = dh.shape
    K = w.shape[0]
    tm = min(tm, S)

    def body(d_ref, w_ref, o_ref):
        o_ref[...] = _dot_nt(d_ref[...].astype(BF16), w_ref[...]).astype(BF16)

    return _call(
        body, name=name, grid=(S // tm,),
        in_specs=[pl.BlockSpec((tm, N), lambda i: (i, 0)), _resident(w.shape)],
        out_specs=pl.BlockSpec((tm, K), lambda i: (i, 0)),
        out_shape=jax.ShapeDtypeStruct((S, K), BF16),
        sem=("parallel",), args=(dh, w), comm=comm)


def _matmul_tn(a, b, name, *, ka, nb, out_dtype, comm=None, ts=1024):
    S, KA = a.shape
    NB = b.shape[1]
    ts = min(ts, S)
    J = max(KA // ka, NB // nb)
    a_map = (lambda j, s: (s, j)) if KA // ka > 1 else (lambda j, s: (s, 0))
    b_map = (lambda j, s: (s, j)) if NB // nb > 1 else (lambda j, s: (s, 0))
    last = S // ts - 1

    def body(a_ref, b_ref, o_ref, acc):
        s = pl.program_id(1)

        @pl.when(s == 0)
        def _():
            acc[...] = jnp.zeros_like(acc)

        acc[...] += _dot_tn(a_ref[...].astype(BF16), b_ref[...].astype(BF16))

        @pl.when(s == last)
        def _():
            o_ref[0] = acc[...].astype(out_dtype)

    return _call(
        body, name=name, grid=(J, S // ts),
        in_specs=[pl.BlockSpec((ts, ka), a_map), pl.BlockSpec((ts, nb), b_map)],
        out_specs=pl.BlockSpec((1, ka, nb), lambda j, s: (j, 0, 0)),
        out_shape=jax.ShapeDtypeStruct((J, ka, nb), out_dtype),
        scratch_shapes=[pltpu.VMEM((ka, nb), F32)],
        sem=("parallel", "arbitrary"), args=(a, b), comm=comm)


def _sgu_prep(w_s):
    G = w_s.shape[0]

    def body(w_ref, t_ref, tt_ref):
        tri = lax.broadcasted_iota(jnp.int32, (CHUNK, CHUNK), 0) >= lax.broadcasted_iota(jnp.int32, (CHUNK, CHUNK), 1)
        for g in range(G):
            t = jnp.where(tri, w_ref[g], 0.0)
            t_ref[g] = t.astype(BF16)
            tt_ref[g] = t.T.astype(BF16)

    return pl.pallas_call(
        body, name="sgu_prep",
        out_shape=[jax.ShapeDtypeStruct(w_s.shape, BF16), jax.ShapeDtypeStruct(w_s.shape, BF16)],
        compiler_params=_cp(None),
    )(w_s)


def _sgu_fwd(z, cdf, x, vg, wtril, bT, wout, comm=None, tm=512):
    S = z.shape[0]
    W = z.shape[1] // 2
    D = x.shape[1]
    tm = min(tm, S)

    def body(z_ref, cdf_ref, x_ref, vg_ref, wt_ref, bT_ref, wo_ref, yp_ref, h_ref):
        def chunk(c, carry):
            r0 = pl.multiple_of(c * CHUNK, CHUNK)
            zc = z_ref[pl.ds(r0, CHUNK), :].astype(F32)
            cdf = cdf_ref[pl.ds(r0, CHUNK), :].astype(F32)
            u = zc[:, :W] * cdf[:, :W]
            v = zc[:, W:] * cdf[:, W:]
            rv = lax.rsqrt(jnp.mean(v * v, axis=-1, keepdims=True) + EPS)
            vn = (v * rv * vg_ref[...]).astype(BF16)
            for g in range(SGU_GROUPS):
                sl = slice(g * LANES, (g + 1) * LANES)
                s = _dot(wt_ref[g], vn[:, sl]) + bT_ref[:, g:g + 1]
                yp_ref[pl.ds(r0, CHUNK), sl] = (u[:, sl] * s).astype(BF16)
            return carry

        lax.fori_loop(0, tm // CHUNK, chunk, 0)
        h_ref[...] = x_ref[...] + _dot(yp_ref[...], wo_ref[...])

    return _call(
        body, name="sgu_fwd", grid=(S // tm,),
        in_specs=[pl.BlockSpec((tm, 2 * W), lambda i: (i, 0)), pl.BlockSpec((tm, 2 * W), lambda i: (i, 0)),
                  pl.BlockSpec((tm, D), lambda i: (i, 0)),
                  _resident((1, W)), _resident(wtril.shape), _resident(bT.shape), _resident(wout.shape)],
        out_specs=[pl.BlockSpec((tm, W), lambda i: (i, 0)), pl.BlockSpec((tm, D), lambda i: (i, 0))],
        out_shape=[jax.ShapeDtypeStruct((S, W), BF16), jax.ShapeDtypeStruct((S, D), F32)],
        sem=("parallel",), args=(z, cdf, x, vg, wtril, bT, wout), comm=comm)


def _sgu_bwd(dh, z, cdf, vg, wtril, wtrilT, bT, wout, comm=None, tm=512):
    S = z.shape[0]
    W = z.shape[1] // 2
    D = dh.shape[1]
    G = SGU_GROUPS
    tm = min(tm, S)
    last = S // tm - 1

    def body(dh_ref, z_ref, cdf_ref, vg_ref, wt_ref, wtT_ref, bT_ref, wo_ref,
             dz_ref, dws_ref, dbT_ref, dvg_ref, dyp_s, du_s, dvn_s, dsacc):
        i = pl.program_id(0)

        @pl.when(i == 0)
        def _():
            dws_ref[...] = jnp.zeros_like(dws_ref)
            dvg_ref[...] = jnp.zeros_like(dvg_ref)
            dsacc[...] = jnp.zeros_like(dsacc)

        dyp_s[...] = _dot_nt(dh_ref[...].astype(BF16), wo_ref[...])
        tri = lax.broadcasted_iota(jnp.int32, (CHUNK, CHUNK), 0) >= lax.broadcasted_iota(jnp.int32, (CHUNK, CHUNK), 1)

        def chunk(c, carry):
            r0 = pl.multiple_of(c * CHUNK, CHUNK)
            zc = z_ref[pl.ds(r0, CHUNK), :].astype(F32)
            cdf = cdf_ref[pl.ds(r0, CHUNK), :].astype(F32)
            u, gu = _gelu_and_grad(zc[:, :W], cdf[:, :W])
            v, gv = _gelu_and_grad(zc[:, W:], cdf[:, W:])
            rv = lax.rsqrt(jnp.mean(v * v, axis=-1, keepdims=True) + EPS)
            vhat = v * rv
            vgain = vg_ref[...]
            vn = (vhat * vgain).astype(BF16)
            dyp = dyp_s[pl.ds(r0, CHUNK), :]
            for g in range(G):
                sl = slice(g * LANES, (g + 1) * LANES)
                vng = vn[:, sl]
                s = _dot(wt_ref[g], vng) + bT_ref[:, g:g + 1]
                ds = dyp[:, sl] * u[:, sl]
                du_s[:, sl] = dyp[:, sl] * s
                dsb = ds.astype(BF16)
                dvn_s[:, sl] = _dot(wtT_ref[g], dsb)
                dws_ref[g] += jnp.where(tri, _dot_nt(dsb, vng), 0.0)
                dsacc[g] += ds
            dvn = dvn_s[...]
            dvg_ref[...] += jnp.sum(dvn * vhat, axis=0, keepdims=True)
            gdy = dvn * vgain
            dv = rv * (gdy - vhat * jnp.mean(gdy * vhat, axis=-1, keepdims=True))
            dz_ref[pl.ds(r0, CHUNK), :W] = (du_s[...] * gu).astype(BF16)
            dz_ref[pl.ds(r0, CHUNK), W:] = (dv * gv).astype(BF16)
            return carry

        lax.fori_loop(0, tm // CHUNK, chunk, 0)

        @pl.when(i == last)
        def _():
            for g in range(G):
                dbT_ref[:, g:g + 1] = jnp.sum(dsacc[g], axis=1, keepdims=True)

    return _call(
        body, name="sgu_bwd", grid=(S // tm,),
        in_specs=[pl.BlockSpec((tm, D), lambda i: (i, 0)), pl.BlockSpec((tm, 2 * W), lambda i: (i, 0)),
                  pl.BlockSpec((tm, 2 * W), lambda i: (i, 0)),
                  _resident((1, W)), _resident(wtril.shape), _resident(wtrilT.shape), _resident(bT.shape),
                  _resident(wout.shape)],
        out_specs=[pl.BlockSpec((tm, 2 * W), lambda i: (i, 0)),
                   pl.BlockSpec((G, CHUNK, CHUNK), lambda i: (0, 0, 0)),
                   pl.BlockSpec((CHUNK, G), lambda i: (0, 0)),
                   pl.BlockSpec((1, W), lambda i: (0, 0))],
        out_shape=[jax.ShapeDtypeStruct((S, 2 * W), BF16), jax.ShapeDtypeStruct((G, CHUNK, CHUNK), F32),
                   jax.ShapeDtypeStruct((CHUNK, G), F32), jax.ShapeDtypeStruct((1, W), F32)],
        scratch_shapes=[pltpu.VMEM((tm, W), F32), pltpu.VMEM((CHUNK, W), F32), pltpu.VMEM((CHUNK, W), F32),
                        pltpu.VMEM((G, CHUNK, CHUNK), F32)],
        sem=("arbitrary",), args=(dh, z, cdf, vg, wtril, wtrilT, bT, wout), comm=comm)


def _ffn_fwd(cp, h_in, wdown, name, comm=None, target=None, tm=512, R=128):
    S, C = cp.shape
    F = C // 2
    D = h_in.shape[1]
    tm = min(tm, S)
    head = target is not None

    def body(cp_ref, h_ref, wd_ref, *rest):
        if head:
            t_ref, f_ref, dy_ref, l_ref = rest
        else:
            f_ref, ho_ref = rest
        for r0 in range(0, tm, R):
            g = cp_ref[r0:r0 + R, :F].astype(F32)
            f_ref[r0:r0 + R, :] = (g * _sigmoid(g) * cp_ref[r0:r0 + R, F:].astype(F32)).astype(BF16)
        h_out = h_ref[...] + _dot(f_ref[...], wd_ref[...])
        if not head:
            ho_ref[...] = h_out
            return

        @pl.when(pl.program_id(0) == 0)
        def _():
            l_ref[...] = jnp.zeros_like(l_ref)

        e = h_out - t_ref[...]
        dy_ref[...] = e * (1.0 / D)
        rows = jnp.sum(e * e, axis=-1, keepdims=True) * (1.0 / D)
        l_ref[...] += 0.5 * jnp.sum(rows, axis=0, keepdims=True)

    row = pl.BlockSpec((tm, D), lambda i: (i, 0))
    in_specs = [pl.BlockSpec((tm, C), lambda i: (i, 0)), row, _resident(wdown.shape)]
    out_specs = [pl.BlockSpec((tm, F), lambda i: (i, 0)), row]
    out_shape = [jax.ShapeDtypeStruct((S, F), BF16), jax.ShapeDtypeStruct((S, D), F32)]
    args = (cp, h_in, wdown)
    if head:
        in_specs.append(row)
        out_specs.append(pl.BlockSpec((1, 1), lambda i: (0, 0)))
        out_shape.append(jax.ShapeDtypeStruct((1, 1), F32))
        args += (target,)
    return _call(
        body, name=name, grid=(S // tm,), in_specs=in_specs, out_specs=out_specs, out_shape=out_shape,
        sem=("arbitrary",) if head else ("parallel",), args=args, comm=comm)


def _rows8(v):
    return functools.reduce(jnp.add, [v[8 * k:8 * k + 8] for k in range(v.shape[0] // 8)])


def _ffn_bwd_dc(dh, cp, wdown, name, comm=None, tm=512, R=64):
    S, C = cp.shape
    F = C // 2
    D = dh.shape[1]
    tm = min(tm, S)

    def body(dh_ref, cp_ref, wd_ref, dc_ref, dcb_ref, df_s, acc):
        i = pl.program_id(0)

        @pl.when(i == 0)
        def _():
            acc[...] = jnp.zeros_like(acc)

        df_s[...] = _dot_nt(dh_ref[...].astype(BF16), wd_ref[...])
        for r0 in range(0, tm, R):
            cpre = cp_ref[r0:r0 + R, :].astype(F32)
            g = cpre[:, :F]
            val = cpre[:, F:]
            sg = _sigmoid(g)
            df = df_s[r0:r0 + R, :]
            dg = df * val * (sg * (1.0 + g * (1.0 - sg)))
            dval = df * (g * sg)
            dc = jnp.concatenate([dg, dval], axis=1)
            dc_ref[r0:r0 + R, :] = dc.astype(BF16)
            acc[...] += _rows8(dc)

        @pl.when(i == S // tm - 1)
        def _():
            dcb_ref[...] = jnp.sum(acc[...], axis=0, keepdims=True)

    return _call(
        body, name=name, grid=(S // tm,),
        in_specs=[pl.BlockSpec((tm, D), lambda i: (i, 0)), pl.BlockSpec((tm, C), lambda i: (i, 0)),
                  _resident(wdown.shape)],
        out_specs=[pl.BlockSpec((tm, C), lambda i: (i, 0)), pl.BlockSpec((1, C), lambda i: (0, 0))],
        out_shape=[jax.ShapeDtypeStruct((S, C), BF16), jax.ShapeDtypeStruct((1, C), F32)],
        scratch_shapes=[pltpu.VMEM((tm, F), F32), pltpu.VMEM((8, C), F32)],
        sem=("arbitrary",), args=(dh, cp, wdown), comm=comm)


def _bwd_norm(dA, w4, h_in, gain, dh_out, name, conv_w=None, conv_in=None, comm=None, R=64):
    S, N = dA.shape
    nsh, D, ns = w4.shape
    conv = conv_w is not None
    tm = min(256 if conv else 512, S)
    nt = S // tm

    def finish(src_ref, w_ref, h_ref, g_ref, dho_ref, dhi_ref, dg_ref):
        dhn = _dot_nt(src_ref[:, 0:ns], w_ref[0])
        for j in range(1, nsh):
            dhn += _dot_nt(src_ref[:, j * ns:(j + 1) * ns], w_ref[j])
        dx, dgain = _rms_bwd(dhn, h_ref[...], g_ref[...])
        dg_ref[...] += dgain
        dhi_ref[...] = dho_ref[...] + dx

    def body_plain(dA_ref, w_ref, h_ref, g_ref, dho_ref, dhi_ref, dg_ref):
        @pl.when(pl.program_id(0) == 0)
        def _():
            dg_ref[...] = jnp.zeros_like(dg_ref)

        finish(dA_ref, w_ref, h_ref, g_ref, dho_ref, dhi_ref, dg_ref)

    def body_conv(dc_ref, halo_ref, cw_ref, a_ref, w_ref, h_ref, g_ref, dho_ref, dhi_ref, dg_ref, da_ref, dcw_ref,
                  dc32, acc):
        i = pl.program_id(0)

        @pl.when(i == 0)
        def _():
            dg_ref[...] = jnp.zeros_like(dg_ref)
            acc[...] = jnp.zeros_like(acc)

        dc32[0:tm, :] = dc_ref[...].astype(F32)
        dc32[tm:, :] = jnp.where(i < nt - 1, halo_ref[...].astype(F32), 0.0)

        for r0 in range(0, tm, R):
            X = dc32[r0:r0 + R + 8, :]
            d0 = X[:R]
            d1 = pltpu.roll(X, R + 7, 0)[:R]
            d2 = pltpu.roll(X, R + 6, 0)[:R]
            da = cw_ref[2:3, :] * d0 + cw_ref[1:2, :] * d1 + cw_ref[0:1, :] * d2
            da_ref[r0:r0 + R, :] = da.astype(BF16)
            a0 = a_ref[r0:r0 + R, :].astype(F32)
            acc[0] += _rows8(d2 * a0)
            acc[1] += _rows8(d1 * a0)
            acc[2] += _rows8(d0 * a0)
        finish(da_ref, w_ref, h_ref, g_ref, dho_ref, dhi_ref, dg_ref)

        @pl.when(i == nt - 1)
        def _():
            for k in range(3):
                dcw_ref[k:k + 1, :] = jnp.sum(acc[k], axis=0, keepdims=True)

    row = lambda width: pl.BlockSpec((tm, width), lambda i: (i, 0))
    common_in = [_resident(w4.shape), row(D), _resident((1, D)), row(D)]
    common_out = [row(D), pl.BlockSpec((1, D), lambda i: (0, 0))]
    common_shape = [jax.ShapeDtypeStruct((S, D), F32), jax.ShapeDtypeStruct((1, D), F32)]
    if not conv:
        return _call(
            body_plain, name=name, grid=(nt,),
            in_specs=[row(N)] + common_in, out_specs=common_out, out_shape=common_shape,
            sem=("arbitrary",), args=(dA, w4, h_in, gain, dh_out), comm=comm)
    hb = tm // HALO
    nhb = S // HALO
    return _call(
        body_conv, name=name, grid=(nt,),
        in_specs=[row(N), pl.BlockSpec((HALO, N), lambda i: (jnp.minimum((i + 1) * hb, nhb - 1), 0)),
                  _resident((3, N)), row(N)] + common_in,
        out_specs=common_out + [row(N), pl.BlockSpec((3, N), lambda i: (0, 0))],
        out_shape=common_shape + [jax.ShapeDtypeStruct((S, N), BF16), jax.ShapeDtypeStruct((3, N), F32)],
        scratch_shapes=[pltpu.VMEM((tm + HALO, N), F32), pltpu.VMEM((3, 8, N), F32)],
        sem=("arbitrary",), args=(dA, dA, conv_w, conv_in, w4, h_in, gain, dh_out), comm=comm)


def _rel_buckets_flat():
    q = np.arange(CHUNK)[:, None] + CHUNK
    k = np.arange(2 * CHUNK)[None, :]
    n = np.maximum(q - k, 0)
    max_exact = REL_BUCKETS // 2
    large = max_exact + (np.log(np.maximum(n, 1).astype(np.float32) / max_exact)
                         / math.log(REL_MAX_DIST / max_exact) * (REL_BUCKETS - max_exact)).astype(np.int32)
    large = np.minimum(large, REL_BUCKETS - 1)
    return np.where(n < max_exact, n, large).astype(np.int32).reshape(1, CHUNK * 2 * CHUNK)


def _split_bf16(x):
    hi = x.astype(BF16)
    return hi, (x - hi.astype(F32)).astype(BF16)


def _rel_bias_expand(rel_bias, bucket):
    B, H = rel_bias.shape
    n = bucket.shape[1]

    def body(rb_ref, bk_ref, o_ref):
        oh = (bk_ref[...] == lax.broadcasted_iota(jnp.int32, (B, n), 0)).astype(BF16)
        hi, lo = _split_bf16(rb_ref[...])
        o_ref[...] = _dot_tn(hi, oh) + _dot_tn(lo, oh)

    return pl.pallas_call(body, name="rel_bias_expand", out_shape=jax.ShapeDtypeStruct((H, n), F32),
                          compiler_params=_cp(None))(rel_bias, bucket)


def _rel_bias_reduce(dbias, bucket):
    H, n = dbias.shape
    B = REL_BUCKETS

    def body(db_ref, bk_ref, o_ref):
        oh = (bk_ref[...] == lax.broadcasted_iota(jnp.int32, (B, n), 0)).astype(BF16)
        hi, lo = _split_bf16(db_ref[...])
        o_ref[...] = _dot_nt(oh, hi) + _dot_nt(oh, lo)

    return pl.pallas_call(body, name="rel_bias_reduce", out_shape=jax.ShapeDtypeStruct((B, H), F32),
                          compiler_params=_cp(None))(dbias, bucket)


def _lo_mask(rows):
    return lax.broadcasted_iota(jnp.int32, (rows, LANES), 1) < HEAD_DIM


def _half_sums(y, lo, mxu=False):
    if not mxu:
        s_lo = jnp.sum(jnp.where(lo, y, 0.0), axis=-1, keepdims=True)
        s_hi = jnp.sum(jnp.where(lo, 0.0, y), axis=-1, keepdims=True)
        return jnp.where(lo, s_lo, s_hi)
    i = lax.broadcasted_iota(jnp.int32, (LANES, LANES), 0) < HEAD_DIM
    j = lax.broadcasted_iota(jnp.int32, (LANES, LANES), 1) < HEAD_DIM
    same = (i == j).astype(BF16)
    hi, lo_part = _split_bf16(y)
    return _dot(hi, same) + _dot(lo_part, same)


def _half_rms(x, gain, lo, mxu=False):
    r = lax.rsqrt(_half_sums(x * x, lo, mxu) * (1.0 / HEAD_DIM) + EPS)
    xhat = x * r
    return xhat * gain, xhat, r


def _half_rms_bwd(dy, xhat, r, gain, lo, mxu=False):
    gdy = dy * gain
    dx = r * (gdy - xhat * (_half_sums(gdy * xhat, lo, mxu) * (1.0 / HEAD_DIM)))
    return dx, jnp.sum(dy * xhat, axis=0, keepdims=True)


def _dup_half(pair, e, lo):
    sw = pltpu.roll(pair, HEAD_DIM, 1)
    return jnp.where(lo, pair, sw) if e == 0 else jnp.where(lo, sw, pair)


def _band_valid(n):
    qi = lax.broadcasted_iota(jnp.int32, (KV_GROUP * CHUNK, 2 * CHUNK), 0) & (CHUNK - 1)
    ki = lax.broadcasted_iota(jnp.int32, (KV_GROUP * CHUNK, 2 * CHUNK), 1)
    dist = qi + CHUNK - ki
    return (dist >= 0) & (dist < CHUNK) & ((n > 0) | (ki >= CHUNK))


def _stack_heads(a, b, lo):
    return jnp.concatenate([jnp.where(lo, a, 0.0), jnp.where(lo, 0.0, a), jnp.where(lo, b, 0.0), jnp.where(lo, 0.0, b)],
                           axis=0)


def _unstack_heads(x4, lo):
    return (jnp.where(lo, x4[0:CHUNK], x4[CHUNK:2 * CHUNK]),
            jnp.where(lo, x4[2 * CHUNK:3 * CHUNK], x4[3 * CHUNK:]))


def _sink_col(sink_ref, hk):
    row = lax.broadcasted_iota(jnp.int32, (KV_GROUP * CHUNK, 1), 0)
    col = jnp.full((KV_GROUP * CHUNK, 1), sink_ref[KV_GROUP * hk + KV_GROUP - 1], F32)
    for r in range(KV_GROUP - 2, -1, -1):
        col = jnp.where(row < (r + 1) * CHUNK, sink_ref[KV_GROUP * hk + r], col)
    return col


def _softmax_sink(s, valid, sink):
    s = jnp.where(valid, s, -jnp.inf)
    m = jnp.maximum(jnp.max(s, axis=-1, keepdims=True), sink)
    p = jnp.exp(s - m)
    esink = jnp.exp(sink - m)
    inv = 1.0 / (jnp.sum(p, axis=-1, keepdims=True) + esink)
    return p * inv, esink * inv


QW = N_HEADS * HEAD_DIM
KVW = N_KV_HEADS * HEAD_DIM


def _attn_fwd(qkv, qg2, kg2, sinks, bias, comm=None):
    S = qkv.shape[0]
    nb = S // CHUNK

    def body(cur_ref, prev_ref, qg_ref, kg_ref, sink_ref, bias_ref, o_ref, p_ref, ps_ref):
        n = pl.program_id(0)
        lo = _lo_mask(CHUNK)
        lo2 = _lo_mask(2 * CHUNK)
        valid = _band_valid(n)
        for j in range(N_KV_HEADS // 2):
            kc = slice(QW + j * LANES, QW + (j + 1) * LANES)
            vc = slice(QW + KVW + j * LANES, QW + KVW + (j + 1) * LANES)
            kpair = jnp.concatenate([prev_ref[:, j * LANES:(j + 1) * LANES], cur_ref[:, kc]], axis=0).astype(F32)
            vpair = jnp.concatenate([prev_ref[:, KVW + j * LANES:KVW + (j + 1) * LANES], cur_ref[:, vc]], axis=0).astype(F32)
            knpair, _, _ = _half_rms(kpair, kg_ref[...], lo2)
            for e in range(2):
                hk = 2 * j + e
                kdup = _dup_half(knpair, e, lo2).astype(BF16)
                vdup = _dup_half(vpair, e, lo2).astype(BF16)
                ca = slice(2 * hk * LANES, (2 * hk + 1) * LANES)
                cb = slice((2 * hk + 1) * LANES, (2 * hk + 2) * LANES)
                qna, _, _ = _half_rms(cur_ref[:, ca].astype(F32), qg_ref[...], lo)
                qnb, _, _ = _half_rms(cur_ref[:, cb].astype(F32), qg_ref[...], lo)
                qm4 = _stack_heads(qna, qnb, lo).astype(BF16)
                s = _dot_nt(qm4, kdup) * (HEAD_DIM ** -0.5) + bias_ref[hk]
                p, psink = _softmax_sink(s, valid, _sink_col(sink_ref, hk))
                pb = p.astype(BF16)
                p_ref[0, hk] = pb
                for r in range(KV_GROUP):
                    h = KV_GROUP * hk + r
                    ps_ref[:, h:h + 1] = psink[r * CHUNK:(r + 1) * CHUNK]
                oa, ob = _unstack_heads(_dot(pb, vdup), lo)
                o_ref[:, ca] = oa.astype(BF16)
                o_ref[:, cb] = ob.astype(BF16)

    pshape = (nb, N_KV_HEADS, KV_GROUP * CHUNK, 2 * CHUNK)
    return _call(
        body, name="attn_fwd", grid=(nb,),
        in_specs=[pl.BlockSpec((CHUNK, QW + 2 * KVW), lambda n: (n, 0)),
                  pl.BlockSpec((CHUNK, 2 * KVW), lambda n: (jnp.maximum(n - 1, 0), QW // (2 * KVW))),
                  _resident((1, LANES)), _resident((1, LANES)),
                  pl.BlockSpec(memory_space=pltpu.SMEM),
                  _resident(bias.shape)],
        out_specs=[pl.BlockSpec((CHUNK, QW), lambda n: (n, 0)),
                   pl.BlockSpec((1,) + pshape[1:], lambda n: (n, 0, 0, 0)),
                   pl.BlockSpec((CHUNK, LANES), lambda n: (n, 0))],
        out_shape=[jax.ShapeDtypeStruct((S, QW), BF16), jax.ShapeDtypeStruct(pshape, BF16),
                   jax.ShapeDtypeStruct((S, LANES), F32)],
        sem=("parallel",), args=(qkv, qkv, qg2, kg2, sinks, bias), comm=comm)


def _attn_bwd(qkv, do, probs, psinks, qg2, kg2, comm=None):
    S = qkv.shape[0]
    nb = S // CHUNK

    def body(cur_ref, prev_ref, do_ref, p_ref, ps_ref, qg_ref, kg_ref,
             dqkv_ref, dbias_ref, dqg_ref, dkg_ref, dsink_ref, carry, band, dsacc, gacc):
        i = pl.program_id(0)
        lo = _lo_mask(CHUNK)
        lo2 = _lo_mask(2 * CHUNK)
        lane = lax.broadcasted_iota(jnp.int32, (KV_GROUP * CHUNK, LANES), 1)

        @pl.when(i == 0)
        def _():
            dbias_ref[...] = jnp.zeros_like(dbias_ref)
            carry[...] = jnp.zeros_like(carry)
            dsacc[...] = jnp.zeros_like(dsacc)
            gacc[...] = jnp.zeros_like(gacc)

        qgain = qg_ref[...]
        kgain = kg_ref[...]
        for j in range(N_KV_HEADS // 2):
            kc = slice(QW + j * LANES, QW + (j + 1) * LANES)
            vc = slice(QW + KVW + j * LANES, QW + KVW + (j + 1) * LANES)
            kpair = jnp.concatenate([prev_ref[:, j * LANES:(j + 1) * LANES], cur_ref[:, kc]], axis=0).astype(F32)
            vpair = jnp.concatenate([prev_ref[:, KVW + j * LANES:KVW + (j + 1) * LANES], cur_ref[:, vc]], axis=0).astype(F32)
            knpair, khat, kr = _half_rms(kpair, kgain, lo2, mxu=True)
            dk_folds = []
            dv_folds = []
            for e in range(2):
                hk = 2 * j + e
                kdup = _dup_half(knpair, e, lo2).astype(BF16)
                vdup = _dup_half(vpair, e, lo2).astype(BF16)
                ca = slice(2 * hk * LANES, (2 * hk + 1) * LANES)
                cb = slice((2 * hk + 1) * LANES, (2 * hk + 2) * LANES)
                qna, qhata, qra = _half_rms(cur_ref[:, ca].astype(F32), qgain, lo, mxu=True)
                qnb, qhatb, qrb = _half_rms(cur_ref[:, cb].astype(F32), qgain, lo, mxu=True)
                qm4 = _stack_heads(qna, qnb, lo).astype(BF16)
                dom4 = _stack_heads(do_ref[:, ca].astype(F32), do_ref[:, cb].astype(F32), lo).astype(BF16)
                pb = p_ref[0, hk]
                p = pb.astype(F32)
                psink = jnp.concatenate([ps_ref[:, KV_GROUP * hk + r:KV_GROUP * hk + r + 1] for r in range(KV_GROUP)],
                                        axis=0)
                dp = _dot_nt(dom4, vdup)
                delta = jnp.sum(p * dp, axis=-1, keepdims=True)
                ds = p * (dp - delta)
                dbias_ref[hk] += ds
                dsacc[...] += jnp.where(lane == hk, -(psink * delta), 0.0)
                dsr = (ds * (HEAD_DIM ** -0.5)).astype(BF16)
                dqna, dqnb = _unstack_heads(_dot(dsr, kdup), lo)
                dkd = _dot_tn(dsr, qm4)
                dvd = _dot_tn(pb, dom4)
                dqa, dqga = _half_rms_bwd(dqna, qhata, qra, qgain, lo, mxu=True)
                dqb, dqgb = _half_rms_bwd(dqnb, qhatb, qrb, qgain, lo, mxu=True)
                gacc[0:1, :] += dqga + dqgb
                dqkv_ref[:, ca] = dqa.astype(BF16)
                dqkv_ref[:, cb] = dqb.astype(BF16)
                dk_folds.append(dkd + pltpu.roll(dkd, HEAD_DIM, 1))
                dv_folds.append(dvd + pltpu.roll(dvd, HEAD_DIM, 1))
            dkn = jnp.where(lo2, dk_folds[0], dk_folds[1])
            dk, dkg = _half_rms_bwd(dkn, khat, kr, kgain, lo2, mxu=True)
            gacc[1:2, :] += dkg
            band[:, j * LANES:(j + 1) * LANES] = dk
            band[:, KVW + j * LANES:KVW + (j + 1) * LANES] = jnp.where(lo2, dv_folds[0], dv_folds[1])
        dqkv_ref[:, QW:] = (band[CHUNK:, :] + carry[...]).astype(BF16)
        carry[...] = band[0:CHUNK, :]

        @pl.when(i == nb - 1)
        def _():
            g = gacc[...]
            g = g + pltpu.roll(g, HEAD_DIM, 1)
            dqg_ref[...] = g[0:1, :]
            dkg_ref[...] = g[1:2, :]
            for r in range(KV_GROUP):
                dsink_ref[r:r + 1, :] = jnp.sum(dsacc[r * CHUNK:(r + 1) * CHUNK, :], axis=0, keepdims=True)

    vec = pl.BlockSpec((1, LANES), lambda i: (0, 0))
    bshape = probs.shape[1:]
    return _call(
        body, name="attn_bwd", grid=(nb,),
        in_specs=[pl.BlockSpec((CHUNK, QW + 2 * KVW), lambda i: (nb - 1 - i, 0)),
                  pl.BlockSpec((CHUNK, 2 * KVW), lambda i: (jnp.maximum(nb - 2 - i, 0), QW // (2 * KVW))),
                  pl.BlockSpec((CHUNK, QW), lambda i: (nb - 1 - i, 0)),
                  pl.BlockSpec((1,) + bshape, lambda i: (nb - 1 - i, 0, 0, 0)),
                  pl.BlockSpec((CHUNK, LANES), lambda i: (nb - 1 - i, 0)),
                  _resident((1, LANES)), _resident((1, LANES))],
        out_specs=[pl.BlockSpec((CHUNK, QW + 2 * KVW), lambda i: (nb - 1 - i, 0)),
                   pl.BlockSpec(bshape, lambda i: (0, 0, 0)), vec, vec,
                   pl.BlockSpec((KV_GROUP, LANES), lambda i: (0, 0))],
        out_shape=[jax.ShapeDtypeStruct((S, QW + 2 * KVW), BF16),
                   jax.ShapeDtypeStruct(bshape, F32),
                   jax.ShapeDtypeStruct((1, LANES), F32), jax.ShapeDtypeStruct((1, LANES), F32),
                   jax.ShapeDtypeStruct((KV_GROUP, LANES), F32)],
        scratch_shapes=[pltpu.VMEM((CHUNK, 2 * KVW), F32), pltpu.VMEM((2 * CHUNK, 2 * KVW), F32),
                        pltpu.VMEM((KV_GROUP * CHUNK, LANES), F32), pltpu.VMEM((8, LANES), F32)],
        sem=("arbitrary",), args=(qkv, qkv, do, probs, psinks, qg2, kg2), comm=comm)


def _row_tile(rows, cols, n_arrays):
    budget = VMEM_LIMIT_V7X // 4 // (2 * n_arrays * 4 * cols)
    best = 8
    for n in range(1, rows // 8 + 1):
        if rows % n == 0 and (rows // n) % 8 == 0 and rows // n <= budget:
            best = rows // n
            break
    return best


def _adamw(g, w, m, v, name, comm=None):
    R, C = g.shape
    tr = _row_tile(R, C, 8)

    def body(g_ref, w_ref, m_ref, v_ref, d_ref, mo_ref, vo_ref, go_ref):
        gg = g_ref[...]
        go_ref[...] = gg
        mn = ADAM_B1 * m_ref[...] + (1.0 - ADAM_B1) * gg
        vn = ADAM_B2 * v_ref[...] + (1.0 - ADAM_B2) * jnp.square(gg)
        m_hat = mn / (1.0 - ADAM_B1 ** ADAM_STEP)
        v_hat = vn / (1.0 - ADAM_B2 ** ADAM_STEP)
        d_ref[...] = -ADAM_LR * (m_hat / (jnp.sqrt(v_hat) + ADAM_EPS) + ADAM_WD * w_ref[...])
        mo_ref[...] = mn
        vo_ref[...] = vn

    spec = pl.BlockSpec((tr, C), lambda i: (i, 0))
    return _call(
        body, name=name, grid=(R // tr,), in_specs=[spec] * 4, out_specs=[spec] * 4,
        out_shape=[jax.ShapeDtypeStruct((R, C), F32)] * 4, sem=("parallel",), args=(g, w, m, v), comm=comm)


def _place_shard(shards, layer, where, dtype, name):
    _, R, C = shards.shape
    tr = _row_tile(R, C, 2) if R % 8 == 0 else R

    def body(s_ref, x_ref, o_ref):
        o_ref[...] = x_ref[...].astype(dtype)

    return pl.pallas_call(
        body, name=name,
        grid_spec=pltpu.PrefetchScalarGridSpec(
            num_scalar_prefetch=1, grid=(R // tr,),
            in_specs=[pl.BlockSpec((1, tr, C), lambda i, s_ref: (layer, i, 0))],
            out_specs=pl.BlockSpec((1, tr, C), lambda i, s_ref: (s_ref[1], i, 0))),
        out_shape=jax.ShapeDtypeStruct((4, R, C), dtype),
        compiler_params=_cp(("parallel",)),
    )(where, shards)


def _pair_add(g4, rsib, where, name):
    J, R, C = g4.shape
    Rh = R // 2
    tr = _row_tile(Rh, C, 4)
    g5 = g4.reshape(J, 2, Rh, C)

    def body(s_ref, g_ref, r_ref, p_ref, q_ref):
        val = (g_ref[...].astype(F32)[0] + r_ref[...].astype(F32)).astype(BF16)
        p_ref[...] = val

        @pl.when(pl.program_id(1) == s_ref[1])
        def _():
            q_ref[...] = val

    return pl.pallas_call(
        body, name=name,
        grid_spec=pltpu.PrefetchScalarGridSpec(
            num_scalar_prefetch=1, grid=(Rh // tr, J),
            in_specs=[pl.BlockSpec((1, 1, tr, C), lambda i, j, s_ref: (j, s_ref[0], i, 0)),
                      pl.BlockSpec((1, tr, C), lambda i, j, s_ref: (j, i, 0))],
            out_specs=[pl.BlockSpec((1, tr, C), lambda i, j, s_ref: (j, i, 0)),
                       pl.BlockSpec((1, tr, C), lambda i, j, s_ref: (s_ref[1], i, 0))]),
        out_shape=[jax.ShapeDtypeStruct((J, Rh, C), BF16)] * 2,
        compiler_params=_cp(("parallel", "arbitrary")),
    )(where, g5, rsib)


def _sum_chips(q, where, dest, layer, out_shape, name):
    J, Rh, C = q.shape
    tr = _row_tile(Rh, C, 3)
    nb = Rh // tr

    def body(s_ref, q_ref, *rest):
        qq = q_ref[...].astype(F32)
        rest[-1][0] = ((qq[0] + qq[1]) + qq[2]) + qq[3]

    have = dest is not None
    return pl.pallas_call(
        body, name=name,
        grid_spec=pltpu.PrefetchScalarGridSpec(
            num_scalar_prefetch=1, grid=(nb,),
            in_specs=[pl.BlockSpec((J, tr, C), lambda i, s_ref: (0, i, 0))] + ([ANY] if have else []),
            out_specs=pl.BlockSpec((1, tr, C), lambda i, s_ref: (layer, s_ref[0] * nb + i, 0))),
        out_shape=jax.ShapeDtypeStruct(out_shape, F32),
        input_output_aliases={2: 0} if have else {},
        compiler_params=_cp(("parallel",)),
    )(*((where, q, dest) if have else (where, q)))


MESH = pl.DeviceIdType.MESH
ANY = pl.BlockSpec(memory_space=pl.ANY)


def _place():
    x, y, c = lax.axis_index("x"), lax.axis_index("y"), lax.axis_index("c")
    others = [(1 - x, y), (x, 1 - y), (1 - x, 1 - y)]
    return x, y, c, 2 * x + y, others, [2 * ox + oy for ox, oy in others]


def _gather_comm(items):
    n = len(items)
    placed = [it[0] for it in items]
    split = [it[3] for it in items]

    def rows(t, ref, half):
        _, lo, hi, _ = items[t]
        if not split[t]:
            return ref if (lo, hi) == (0, placed[t].shape[1]) else ref.at[pl.ds(lo, hi - lo), :]
        rh = (hi - lo) // 2
        return ref.at[pl.ds(lo + half * rh, rh), :]

    def sends(outs, sems):
        send, recv = sems[0], sems[1]
        x, y, c, me, others, okey = _place()
        cps = []
        for t in range(n):
            mine = rows(t, outs[t].at[me], c)
            for j, (ox, oy) in enumerate(others):
                cps.append(pltpu.make_async_remote_copy(
                    src_ref=mine, dst_ref=mine,
                    send_sem=send.at[t, j], recv_sem=recv.at[t, j], device_id=(ox, oy, c), device_id_type=MESH))
        return cps

    def start(srcs, outs, news, sems):
        for cp in sends(outs, sems):
            cp.start()

    def forwards(outs, sems):
        fsend, frecv = sems[2], sems[3]
        x, y, c, me, others, okey = _place()
        cps = []
        for t in range(n):
            if split[t]:
                for j in range(3):
                    landed = rows(t, outs[t].at[okey[j]], c)
                    cps.append(pltpu.make_async_remote_copy(
                        src_ref=landed, dst_ref=landed, send_sem=fsend.at[t, j], recv_sem=frecv.at[t, j],
                        device_id=(x, y, 1 - c), device_id_type=MESH))
        return cps

    def middle(srcs, outs, news, sems):
        send, recv = sems[0], sems[1]
        x, y, c, me, others, okey = _place()
        for t in range(n):
            for j in range(3):
                landed = rows(t, outs[t].at[okey[j]], c)
                pltpu.make_async_remote_copy(
                    src_ref=landed, dst_ref=landed, send_sem=send.at[t, j], recv_sem=recv.at[t, j],
                    device_id=(x, y, 1 - c), device_id_type=MESH).wait_recv()
        for cp in forwards(outs, sems):
            cp.start()

    def finish(srcs, outs, news, sems):
        fsend, frecv = sems[2], sems[3]
        x, y, c, me, others, okey = _place()
        for t in range(n):
            if split[t]:
                for j in range(3):
                    theirs = rows(t, outs[t].at[okey[j]], 1 - c)
                    pltpu.make_async_remote_copy(
                        src_ref=theirs, dst_ref=theirs, send_sem=fsend.at[t, j], recv_sem=frecv.at[t, j],
                        device_id=(x, y, 1 - c), device_id_type=MESH).wait_recv()
        for cp in sends(outs, sems) + forwards(outs, sems):
            cp.wait_send()

    return _Comm([], placed, [], [pltpu.SemaphoreType.DMA((n, 3))] * 4, start, finish, middle)


def _pair_exchange_comm(gs):
    n = len(gs)

    def copies(ins, outs, sems):
        send, recv = sems
        x, y, c, _, _, _ = _place()
        cps = []
        for t in range(n):
            rh = gs[t].shape[1] // 2
            cps.append(pltpu.make_async_remote_copy(
                src_ref=ins[t].at[:, pl.ds((1 - c) * rh, rh), :], dst_ref=outs[t],
                send_sem=send.at[t], recv_sem=recv.at[t], device_id=(x, y, 1 - c), device_id_type=MESH))
        return cps

    def start(ins, zones, outs, sems):
        for cp in copies(ins, outs, sems):
            cp.start()

    def finish(ins, zones, outs, sems):
        for cp in copies(ins, outs, sems):
            cp.wait()

    news = [jax.ShapeDtypeStruct((4, g.shape[1] // 2, g.shape[2]), g.dtype) for g in gs]
    return _Comm(gs, [], news, [pltpu.SemaphoreType.DMA((n,))] * 2, start, finish)


def _chip_scatter_comm(ps, qs):
    n = len(ps)

    def sends(ins, outs, sems):
        send, recv = sems
        x, y, c, me, others, okey = _place()
        return [pltpu.make_async_remote_copy(
            src_ref=ins[t].at[okey[j]], dst_ref=outs[t].at[me],
            send_sem=send.at[t, j], recv_sem=recv.at[t, j], device_id=(ox, oy, c), device_id_type=MESH)
            for t in range(n) for j, (ox, oy) in enumerate(others)]

    def start(ins, outs, news, sems):
        for cp in sends(ins, outs, sems):
            cp.start()

    def finish(ins, outs, news, sems):
        send, recv = sems
        x, y, c, me, others, okey = _place()
        for t in range(n):
            for j in range(3):
                slot = outs[t].at[okey[j]]
                pltpu.make_async_remote_copy(
                    src_ref=slot, dst_ref=slot, send_sem=send.at[t, j], recv_sem=recv.at[t, j],
                    device_id=(x, y, c), device_id_type=MESH).wait_recv()
        for cp in sends(ins, outs, sems):
            cp.wait_send()

    return _Comm(ps, qs, [], [pltpu.SemaphoreType.DMA((n, 3))] * 2, start, finish)


def _half_exchange_comm(arrs, layers=None):
    n = len(arrs)
    items = [(t, layer) for t in range(n) for layer in (range(arrs[t].shape[0]) if layers is None else layers[t])]

    def sends(outs, sems):
        send, recv = sems
        x, y, c, _, _, _ = _place()
        cps = []
        for k, (t, layer) in enumerate(items):
            rh = arrs[t].shape[1] // 2
            mine = outs[t].at[layer, pl.ds(c * rh, rh), :]
            cps.append(pltpu.make_async_remote_copy(
                src_ref=mine, dst_ref=mine, send_sem=send.at[k], recv_sem=recv.at[k],
                device_id=(x, y, 1 - c), device_id_type=MESH))
        return cps

    def start(srcs, outs, news, sems):
        for cp in sends(outs, sems):
            cp.start()

    def finish(srcs, outs, news, sems):
        send, recv = sems
        x, y, c, _, _, _ = _place()
        for k, (t, layer) in enumerate(items):
            rh = arrs[t].shape[1] // 2
            theirs = outs[t].at[layer, pl.ds((1 - c) * rh, rh), :]
            pltpu.make_async_remote_copy(
                src_ref=theirs, dst_ref=theirs, send_sem=send.at[k], recv_sem=recv.at[k],
                device_id=(x, y, 1 - c), device_id_type=MESH).wait_recv()
        for cp in sends(outs, sems):
            cp.wait_send()

    return _Comm([], arrs, [], [pltpu.SemaphoreType.DMA((len(items),))] * 2, start, finish)


SMALL_COLS = 1024
SMALL_PIECE_ROWS = 48


def _allreduce_small(buf, comm=None):
    pr = SMALL_PIECE_ROWS
    flips = [(d >> 2 & 1, d >> 1 & 1, d & 1) for d in range(1, 8)]

    def body(x_ref, o_ref, rbuf, send1, recv1, send2, recv2):
        x, y, c = lax.axis_index("x"), lax.axis_index("y"), lax.axis_index("c")
        me = 4 * x + 2 * y + c
        peers = [(x ^ fx, y ^ fy, c ^ fc) for fx, fy, fc in flips]
        pid = [4 * px + 2 * py + pc for px, py, pc in peers]

        def piece(ref, p):
            return ref.at[pl.ds(pl.multiple_of(p * pr, 8), pr), :]

        cps = []
        for d in range(7):
            cp = pltpu.make_async_remote_copy(
                src_ref=piece(x_ref, pid[d]), dst_ref=rbuf.at[d + 1],
                send_sem=send1.at[d], recv_sem=recv1.at[d], device_id=peers[d], device_id_type=MESH)
            cp.start()
            cps.append(cp)
        acc = piece(x_ref, me)[...]
        for d in range(7):
            cps[d].wait_recv()
            acc = acc + rbuf[d + 1]
        piece(o_ref, me)[...] = acc
        out = []
        for d in range(7):
            cp = pltpu.make_async_remote_copy(
                src_ref=piece(o_ref, me), dst_ref=piece(o_ref, me),
                send_sem=send2.at[d], recv_sem=recv2.at[d], device_id=peers[d], device_id_type=MESH)
            cp.start()
            out.append(cp)
        for d in range(7):
            pltpu.make_async_remote_copy(
                src_ref=piece(o_ref, pid[d]), dst_ref=piece(o_ref, pid[d]),
                send_sem=send2.at[d], recv_sem=recv2.at[d], device_id=peers[d], device_id_type=MESH).wait_recv()
        for cp in cps + out:
            cp.wait_send()

    vm = pl.BlockSpec(memory_space=pltpu.VMEM)
    return _call(
        body, name="small_allreduce", grid=(), in_specs=[vm], out_specs=vm,
        out_shape=jax.ShapeDtypeStruct(buf.shape, F32),
        scratch_shapes=[pltpu.VMEM((8, pr, SMALL_COLS), F32)] + [pltpu.SemaphoreType.DMA((7,))] * 4,
        args=(buf,), comm=comm)


def _rows_of(shape):
    return -(-math.prod(shape) // (8 * SMALL_COLS)) * 8


def _pack(arrays, rows):
    parts = []
    for a in arrays:
        r = _rows_of(a.shape)
        parts.append(jnp.pad(a.reshape(-1), (0, r * SMALL_COLS - a.size)).reshape(r, SMALL_COLS))
    used = sum(p.shape[0] for p in parts)
    if rows > used:
        parts.append(jnp.zeros((rows - used, SMALL_COLS), F32))
    return jnp.concatenate(parts, axis=0)


def _unpack(buf, shapes):
    out, off = [], 0
    for s in shapes:
        r = _rows_of(s)
        out.append(buf[off:off + r].reshape(-1)[:math.prod(s)].reshape(s))
        off += r
    return out


BIG = ["sgu_w_in", "sgu_w_out", "attn_w_qkv", "attn_w_o", "ffn_w_up", "ffn_w_down"]
SMALL = ["mix_norm", "ffn_norm", "sgu_v_gain", "sgu_w_s", "sgu_b_s", "attn_q_gain", "attn_k_gain", "attn_sinks",
         "rel_bias", "ffn_conv_b"]
ORDER = ["mix_norm", "ffn_norm", "sgu_w_in", "sgu_v_gain", "sgu_w_s", "sgu_b_s", "sgu_w_out", "attn_w_qkv",
         "attn_q_gain", "attn_k_gain", "attn_sinks", "attn_w_o", "rel_bias", "ffn_w_up", "ffn_conv_w", "ffn_conv_b",
         "ffn_w_down"]


def kernel(x, mix_norm, ffn_norm, sgu_w_in, sgu_v_gain, sgu_w_s, sgu_b_s, sgu_w_out, attn_w_qkv, attn_q_gain, attn_k_gain, attn_sinks, attn_w_o, rel_bias, ffn_w_up, ffn_conv_w, ffn_conv_b, ffn_w_down, loss_target, m_mix_norm, m_ffn_norm, m_sgu_w_in, m_sgu_v_gain, m_sgu_w_s, m_sgu_b_s, m_sgu_w_out, m_attn_w_qkv, m_attn_q_gain, m_attn_k_gain, m_attn_sinks, m_attn_w_o, m_rel_bias, m_ffn_w_up, m_ffn_conv_w, m_ffn_conv_b, m_ffn_w_down, v_mix_norm, v_ffn_norm, v_sgu_w_in, v_sgu_v_gain, v_sgu_w_s, v_sgu_b_s, v_sgu_w_out, v_attn_w_qkv, v_attn_q_gain, v_attn_k_gain, v_attn_sinks, v_attn_w_o, v_rel_bias, v_ffn_w_up, v_ffn_conv_w, v_ffn_conv_b, v_ffn_w_down):
    w = dict(mix_norm=mix_norm, ffn_norm=ffn_norm, sgu_w_in=sgu_w_in, sgu_v_gain=sgu_v_gain, sgu_w_s=sgu_w_s,
             sgu_b_s=sgu_b_s, sgu_w_out=sgu_w_out, attn_w_qkv=attn_w_qkv, attn_q_gain=attn_q_gain,
             attn_k_gain=attn_k_gain, attn_sinks=attn_sinks, attn_w_o=attn_w_o, rel_bias=rel_bias, ffn_w_up=ffn_w_up,
             ffn_conv_w=ffn_conv_w, ffn_conv_b=ffn_conv_b, ffn_w_down=ffn_w_down)
    mom = dict(mix_norm=m_mix_norm, ffn_norm=m_ffn_norm, sgu_w_in=m_sgu_w_in, sgu_v_gain=m_sgu_v_gain,
               sgu_w_s=m_sgu_w_s, sgu_b_s=m_sgu_b_s, sgu_w_out=m_sgu_w_out, attn_w_qkv=m_attn_w_qkv,
               attn_q_gain=m_attn_q_gain, attn_k_gain=m_attn_k_gain, attn_sinks=m_attn_sinks, attn_w_o=m_attn_w_o,
               rel_bias=m_rel_bias, ffn_w_up=m_ffn_w_up, ffn_conv_w=m_ffn_conv_w, ffn_conv_b=m_ffn_conv_b,
               ffn_w_down=m_ffn_w_down)
    var = dict(mix_norm=v_mix_norm, ffn_norm=v_ffn_norm, sgu_w_in=v_sgu_w_in, sgu_v_gain=v_sgu_v_gain,
               sgu_w_s=v_sgu_w_s, sgu_b_s=v_sgu_b_s, sgu_w_out=v_sgu_w_out, attn_w_qkv=v_attn_w_qkv,
               attn_q_gain=v_attn_q_gain, attn_k_gain=v_attn_k_gain, attn_sinks=v_attn_sinks, attn_w_o=v_attn_w_o,
               rel_bias=v_rel_bias, ffn_w_up=v_ffn_w_up, ffn_conv_w=v_ffn_conv_w, ffn_conv_b=v_ffn_conv_b,
               ffn_w_down=v_ffn_w_down)
    chip = 2 * lax.axis_index("x") + lax.axis_index("y")
    core = lax.axis_index("c")

    where = jnp.stack([core, chip]).astype(jnp.int32)
    names = ["sgu_w_in", "sgu_w_out", "attn_w_qkv", "attn_w_o", "ffn_w_up0", "ffn_w_up1", "ffn_w_down0", "ffn_w_down1"]
    shards = [(sgu_w_in, 0), (sgu_w_out, 0), (attn_w_qkv, 0), (attn_w_o, 0), (ffn_w_up, 0), (ffn_w_up, 1),
              (ffn_w_down, 0), (ffn_w_down, 1)]
    T = {nm: _place_shard(s, l, where, BF16, "place_" + nm) for (s, l), nm in zip(shards, names)}
    for l in range(2):
        T["conv_w%d" % l] = _place_shard(ffn_conv_w, l, where, F32, "place_conv_w%d" % l)

    def gather(keys):
        items = []
        for k in keys:
            k, lo, hi = (k, 0, None) if isinstance(k, str) else k
            items.append((T[k], lo, T[k].shape[1] if hi is None else hi, not k.startswith("conv")))
        return _gather_comm(items)

    def gathered(keys, res):
        for k, a in zip(keys, res[0]):
            T[k if isinstance(k, str) else k[0]] = a

    D = x.shape[2]
    first = ["sgu_w_in", "conv_w0", "conv_w1"]
    gathered(first, _run_comm(gather(first), "gather_first"))
    unshard_cols = lambda a: jnp.transpose(a, (1, 0, 2)).reshape(a.shape[1], -1)
    cw = [unshard_cols(T["conv_w0"]), unshard_cols(T["conv_w1"])]
    cb = ffn_conv_b
    flat = lambda k: T[k].reshape(-1, D)
    x2, target = x[0], loss_target[0]
    bucket = jnp.asarray(_rel_buckets_flat())
    wtril, wtrilT = _sgu_prep(sgu_w_s[0])
    bT = sgu_b_s[0].T
    bias = _rel_bias_expand(rel_bias, bucket).reshape(N_KV_HEADS, KV_GROUP * CHUNK, 2 * CHUNK)
    qg2 = jnp.tile(attn_q_gain, (1, 2))
    kg2 = jnp.tile(attn_k_gain, (1, 2))
    sinks = attn_sinks.reshape(N_HEADS)
    mix0, mix1 = mix_norm[0:1], mix_norm[1:2]
    fn0, fn1 = ffn_norm[0:1], ffn_norm[1:2]

    ks = ["sgu_w_out", ("ffn_w_up0", 0, 384)]
    (hn0, z, cdf), r = _norm_matmul(x2, mix0, T["sgu_w_in"], "sgu_in", comm=gather(ks), with_cdf=True)
    gathered(ks, r)
    ks = [("ffn_w_up0", 384, None)]
    (yp, h1), r = _sgu_fwd(z, cdf, x2, sgu_v_gain, wtril, bT, flat("sgu_w_out"), comm=gather(ks))
    gathered(ks, r)
    ks = ["ffn_w_down0", "attn_w_qkv", "attn_w_o"]
    (hf0, a0, cp0), r = _norm_matmul(h1, fn0, T["ffn_w_up0"], "ffn0_up", comm=gather(ks), conv=(cw[0], cb[0:1]))
    gathered(ks, r)
    ks = [("ffn_w_up1", 0, 384)]
    (f0, h2), r = _ffn_fwd(cp0, h1, flat("ffn_w_down0"), "ffn0_fwd", comm=gather(ks))
    gathered(ks, r)
    (hn1, qkv), _ = _norm_matmul(h2, mix1, T["attn_w_qkv"], "attn_qkv")
    ks = [("ffn_w_up1", 384, None)]
    (o, probs, psinks), r = _attn_fwd(qkv, qg2, kg2, sinks, bias, comm=gather(ks))
    gathered(ks, r)
    h3, _ = _matmul_res(o, flat("attn_w_o"), h2, "attn_out")
    ks = ["ffn_w_down1"]
    (hf1, a1, cp1), r = _norm_matmul(h3, fn1, T["ffn_w_up1"], "ffn1_up", comm=gather(ks), conv=(cw[1], cb[1:2]))
    gathered(ks, r)
    (f1, dh4, loss_local), _ = _ffn_fwd(cp1, h3, flat("ffn_w_down1"), "ffn1_fwd", target=target)

    G, RS, PQ, QD, halves, grads = {}, {}, {}, {}, {}, {}
    dest_of = {"sgu_w_in": ("sgu_w_in", 0), "sgu_w_out": ("sgu_w_out", 0), "attn_w_qkv": ("attn_w_qkv", 0),
               "attn_w_o": ("attn_w_o", 0), "ffn_w_up0": ("ffn_w_up", 0), "ffn_w_up1": ("ffn_w_up", 1),
               "ffn_w_down0": ("ffn_w_down", 0), "ffn_w_down1": ("ffn_w_down", 1)}

    def px(keys):
        return _pair_exchange_comm([G[k] for k in keys])

    def px_done(keys, res):
        for k, a in zip(keys, res[1]):
            RS[k] = a
            PQ[k] = _pair_add(G[k], a, where, "pair_add_" + k)

    def sc(keys):
        return _chip_scatter_comm([PQ[k][0] for k in keys], [PQ[k][1] for k in keys])

    def sc_done(keys, res):
        for k, a in zip(keys, res[0]):
            wk, layer = dest_of[k]
            halves[wk] = _sum_chips(a, where, halves.get(wk), layer, w[wk].shape, "sum_chips_" + k)

    nup = ffn_w_up.shape[2]
    ndown = ffn_w_down.shape[1]
    rows4 = lambda a: a.reshape(4, ndown, D)
    (dc1, dcb1), _ = _ffn_bwd_dc(dh4, cp1, flat("ffn_w_down1"), "ffn1_bwd_dc")
    gw, _ = _matmul_tn(f1, dh4, "ffn1_dw_down", ka=2 * ndown, nb=D, out_dtype=BF16)
    G["ffn_w_down1"] = rows4(gw)
    (dh3, dfn1, da1, dcw1), _ = _bwd_norm(dc1, T["ffn_w_up1"], h3, fn1, dh4, "ffn1_bwd_in", conv_w=cw[1], conv_in=a1)
    G["ffn_w_up1"], _ = _matmul_tn(hf1, da1, "ffn1_dw_up", ka=D, nb=nup, out_dtype=BF16)
    ks = ["ffn_w_down1", "ffn_w_up1"]
    G["attn_w_o"], r = _matmul_tn(o, dh3, "attn_dw_o", ka=QW // 4, nb=D, out_dtype=BF16, comm=px(ks))
    px_done(ks, r)
    do, _ = _matmul_nt(dh3, flat("attn_w_o"), "attn_bwd_out")
    (dqkv, dbias, dqg, dkg, dsink), r = _attn_bwd(qkv, do, probs, psinks, qg2, kg2, comm=sc(ks))
    sc_done(ks, r)
    G["attn_w_qkv"], _ = _matmul_tn(hn1, dqkv, "attn_dw_qkv", ka=D, nb=dqkv.shape[1] // 4, out_dtype=BF16)
    ks = ["attn_w_o", "attn_w_qkv"]
    (dh2, dmix1), r = _bwd_norm(dqkv, T["attn_w_qkv"], h2, mix1, dh3, "attn_bwd_in", comm=px(ks))
    px_done(ks, r)
    (dc0, dcb0), r = _ffn_bwd_dc(dh2, cp0, flat("ffn_w_down0"), "ffn0_bwd_dc", comm=sc(ks))
    sc_done(ks, r)
    gw, _ = _matmul_tn(f0, dh2, "ffn0_dw_down", ka=2 * ndown, nb=D, out_dtype=BF16)
    G["ffn_w_down0"] = rows4(gw)
    (dh1, dfn0, da0, dcw0), r = _bwd_norm(dc0, T["ffn_w_up0"], h1, fn0, dh2, "ffn0_bwd_in", conv_w=cw[0], conv_in=a0,
                                          comm=px(["ffn_w_down0"]))
    px_done(["ffn_w_down0"], r)
    G["ffn_w_up0"], r = _matmul_tn(hf0, da0, "ffn0_dw_up", ka=D, nb=nup, out_dtype=BF16, comm=sc(["ffn_w_down0"]))
    sc_done(["ffn_w_down0"], r)
    ks = ["ffn_w_up0"]
    G["sgu_w_out"], r = _matmul_tn(yp, dh1, "sgu_dw_out", ka=yp.shape[1] // 4, nb=D, out_dtype=BF16, comm=px(ks))
    px_done(ks, r)
    both = [sc(ks), px(["sgu_w_out"])]
    (dz, dws, dbT, dvg), r = _sgu_bwd(dh1, z, cdf, sgu_v_gain, wtril, wtrilT, bT, flat("sgu_w_out"),
                                      comm=_join(both))
    r = _split(both, r)
    sc_done(ks, r[0])
    px_done(["sgu_w_out"], r[1])
    done = ["attn_w_qkv", "attn_w_o", "ffn_w_up", "ffn_w_down"]
    both = [sc(["sgu_w_out"]), _half_exchange_comm([halves[k] for k in done])]
    G["sgu_w_in"], r = _matmul_tn(hn0, dz, "sgu_dw_in", ka=D, nb=dz.shape[1] // 4, out_dtype=BF16, comm=_join(both))
    r = _split(both, r)
    sc_done(["sgu_w_out"], r[0])
    for k, a in zip(done, r[1][0]):
        grads[k] = a
    px_done(["sgu_w_in"], _run_comm(px(["sgu_w_in"]), "grad_last_pair_exchange"))
    (grad_x, dmix0), r = _bwd_norm(dz, T["sgu_w_in"], x2, mix0, dh1, "sgu_bwd_in", comm=sc(["sgu_w_in"]))
    sc_done(["sgu_w_in"], r)

    g = dict(mix_norm=jnp.concatenate([dmix0, dmix1], axis=0), ffn_norm=jnp.concatenate([dfn0, dfn1], axis=0),
             sgu_v_gain=dvg, sgu_w_s=dws, sgu_b_s=dbT.T, attn_q_gain=dqg[:, :HEAD_DIM], attn_k_gain=dkg[:, :HEAD_DIM],
             attn_sinks=dsink[:, :N_KV_HEADS].T.reshape(1, N_HEADS),
             rel_bias=_rel_bias_reduce(dbias.reshape(N_HEADS, CHUNK * 2 * CHUNK), bucket),
             ffn_conv_b=jnp.concatenate([dcb0, dcb1], axis=0))
    small_list = [g[k].reshape(w[k].shape) for k in SMALL] + [jnp.stack([dcw0, dcw1]), loss_local]
    small_shapes = [a.shape for a in small_list]
    done = ["sgu_w_in", "sgu_w_out"]
    red, r = _allreduce_small(_pack(small_list, 8 * SMALL_PIECE_ROWS),
                              comm=_half_exchange_comm([halves[k] for k in done]))
    for k, a in zip(done, r[0]):
        grads[k] = a
    red = _unpack(red, small_shapes)
    for k, a in zip(SMALL, red):
        grads[k] = a
    grads["ffn_conv_w"] = lax.dynamic_slice_in_dim(red[-2], chip * ffn_conv_w.shape[2], ffn_conv_w.shape[2], axis=2)
    loss = red[-1][0, 0]

    delta, new_m, new_v = {}, {}, {}
    two = lambda a: a.reshape(-1, a.shape[-1])
    for k in BIG:
        (d2, m2, v2, g2), _ = _adamw(two(grads[k]), two(w[k]), two(mom[k]), two(var[k]), "adamw_" + k)
        delta[k], new_m[k], new_v[k], grads[k] = (a.reshape(w[k].shape) for a in (d2, m2, v2, g2))
    sm = SMALL + ["ffn_conv_w"]
    sm_shapes = [w[k].shape for k in sm]
    rows = sum(_rows_of(s) for s in sm_shapes)
    (d2, m2, v2, _), _ = _adamw(_pack([grads[k] for k in sm], rows), _pack([w[k] for k in sm], rows),
                                _pack([mom[k] for k in sm], rows), _pack([var[k] for k in sm], rows), "adamw_small")
    for dst, buf in ((delta, d2), (new_m, m2), (new_v, v2)):
        for k, a in zip(sm, _unpack(buf, sm_shapes)):
            dst[k] = a

    return (loss, grad_x[None], *[grads[k] for k in ORDER], *[delta[k] for k in ORDER],
            *[new_m[k] for k in ORDER], *[new_v[k] for k in ORDER])
```

```python
import functools
import math

import numpy as np
import jax
import jax.numpy as jnp
from jax import lax
from jax.experimental import pallas as pl
from jax.experimental.pallas import tpu as pltpu

F32 = jnp.float32
BF16 = jnp.bfloat16

EPS = 1e-6
CHUNK = 128
SGU_GROUPS = 16
HEAD_DIM = 64
N_HEADS = 16
N_KV_HEADS = 4
KV_GROUP = N_HEADS // N_KV_HEADS
REL_BUCKETS = 32
REL_MAX_DIST = 128
LANES = 128
HALO = 16

ADAM_LR = 0.001
ADAM_B1 = 0.9
ADAM_B2 = 0.999
ADAM_EPS = 1e-08
ADAM_WD = 0.01
ADAM_STEP = 10

VMEM_LIMIT_V7X = 56 * 1024 * 1024

_SQRT_HALF = math.sqrt(0.5)
_INV_SQRT_2PI = 1.0 / math.sqrt(2.0 * math.pi)


def _cp(sem):
    return pltpu.CompilerParams(dimension_semantics=sem, vmem_limit_bytes=VMEM_LIMIT_V7X)


def _resident(shape):
    nd = len(shape)
    return pl.BlockSpec(shape, lambda *_: (0,) * nd, pipeline_mode=pl.Buffered(1))


class _Comm:
    def __init__(self, srcs, zones, news, sems, start, finish, middle=None):
        self.srcs, self.zones, self.news, self.sems = list(srcs), list(zones), list(news), list(sems)
        self.start, self.finish = start, finish
        self.middle = middle if middle is not None else (lambda srcs, zones, news, sems: None)


def _join(comms):
    comms = [c for c in comms if c is not None]
    if not comms:
        return None

    def part(seq, attr):
        out, k = [], 0
        for c in comms:
            n = len(getattr(c, attr))
            out.append(seq[k:k + n])
            k += n
        return out

    def run(which):
        def f(srcs, zones, news, sems):
            for c, a, b, d, e in zip(comms, part(srcs, "srcs"), part(zones, "zones"), part(news, "news"), part(sems, "sems")):
                getattr(c, which)(a, b, d, e)
        return f

    cat = lambda attr: [v for c in comms for v in getattr(c, attr)]
    return _Comm(cat("srcs"), cat("zones"), cat("news"), cat("sems"), run("start"), run("finish"), run("middle"))


def _split(comms, res):
    zones, news = res
    out, kz, kn = [], 0, 0
    for c in comms:
        out.append((zones[kz:kz + len(c.zones)], news[kn:kn + len(c.news)]))
        kz += len(c.zones)
        kn += len(c.news)
    return out


def _call(body, *, name, grid, in_specs, out_specs, out_shape, args, scratch_shapes=(), sem=None, comm=None):
    if comm is None:
        res = pl.pallas_call(body, name=name, grid=grid, in_specs=in_specs, out_specs=out_specs, out_shape=out_shape,
                             scratch_shapes=list(scratch_shapes), compiler_params=_cp(sem))(*args)
        return res, None
    single = not isinstance(out_shape, (list, tuple))
    out_specs_l = [out_specs] if single else list(out_specs)
    out_shape_l = [out_shape] if single else list(out_shape)
    n_in, n_out, n_scr = len(in_specs), len(out_shape_l), len(scratch_shapes)
    ns, nz, nn = len(comm.srcs), len(comm.zones), len(comm.news)

    def wrapped(*refs):
        k = n_in
        ins, srcs = refs[:k], refs[k:k + ns]
        k += ns + nz
        outs, zones, news = refs[k:k + n_out], refs[k + n_out:k + n_out + nz], refs[k + n_out + nz:k + n_out + nz + nn]
        k += n_out + nz + nn
        scr, sems = refs[k:k + n_scr], refs[k + n_scr:]
        if not grid:
            comm.start(srcs, zones, news, sems)
            body(*ins, *outs, *scr)
            comm.middle(srcs, zones, news, sems)
            comm.finish(srcs, zones, news, sems)
            return
        first = functools.reduce(jnp.logical_and, [pl.program_id(a) == 0 for a in range(len(grid))])
        last = functools.reduce(jnp.logical_and, [pl.program_id(a) == grid[a] - 1 for a in range(len(grid))])
        early = len(grid) == 1 and grid[0] >= 2
        mid_step = grid[0] - (2 if grid[0] >= 8 else 1)

        @pl.when(first)
        def _():
            comm.start(srcs, zones, news, sems)

        if early:
            @pl.when(pl.program_id(0) == mid_step)
            def _():
                comm.middle(srcs, zones, news, sems)

        body(*ins, *outs, *scr)

        @pl.when(last)
        def _():
            if not early:
                comm.middle(srcs, zones, news, sems)
            comm.finish(srcs, zones, news, sems)

    res = pl.pallas_call(
        wrapped, name=name, grid=grid,
        in_specs=list(in_specs) + [ANY] * (ns + nz), out_specs=out_specs_l + [ANY] * (nz + nn),
        out_shape=out_shape_l + [jax.ShapeDtypeStruct(z.shape, z.dtype) for z in comm.zones] + comm.news,
        input_output_aliases={n_in + ns + i: n_out + i for i in range(nz)},
        scratch_shapes=list(scratch_shapes) + comm.sems,
        compiler_params=_cp(("arbitrary",) * len(grid)),
    )(*args, *comm.srcs, *comm.zones)
    main = res[0] if single else list(res[:n_out])
    return main, (list(res[n_out:n_out + nz]), list(res[n_out + nz:]))


def _run_comm(comm, name):
    ns, nz, nn = len(comm.srcs), len(comm.zones), len(comm.news)

    def body(*refs):
        srcs, zones, news, sems = refs[:ns], refs[ns + nz:ns + 2 * nz], refs[ns + 2 * nz:ns + 2 * nz + nn], refs[ns + 2 * nz + nn:]
        comm.start(srcs, zones, news, sems)
        comm.middle(srcs, zones, news, sems)
        comm.finish(srcs, zones, news, sems)

    res = pl.pallas_call(
        body, name=name, in_specs=[ANY] * (ns + nz), out_specs=[ANY] * (nz + nn),
        out_shape=[jax.ShapeDtypeStruct(z.shape, z.dtype) for z in comm.zones] + comm.news,
        input_output_aliases={ns + i: i for i in range(nz)}, scratch_shapes=comm.sems,
    )(*comm.srcs, *comm.zones)
    return list(res[:nz]), list(res[nz:])


def _dot(a, b):
    return jnp.dot(a, b, preferred_element_type=F32)


def _dot_nt(a, b):
    return lax.dot_general(a, b, (((1,), (1,)), ((), ())), preferred_element_type=F32)


def _dot_tn(a, b):
    return lax.dot_general(a, b, (((0,), (0,)), ((), ())), preferred_element_type=F32)


def _normal_cdf(x):
    return 0.5 * (1.0 + lax.erf(x * _SQRT_HALF))


def _gelu_and_grad(x, cdf):
    return x * cdf, cdf + x * jnp.exp(-0.5 * x * x) * _INV_SQRT_2PI


def _sigmoid(x):
    return 0.5 * jnp.tanh(0.5 * x) + 0.5


def _rms_bwd(dy, x, gain):
    r = lax.rsqrt(jnp.mean(x * x, axis=-1, keepdims=True) + EPS)
    xhat = x * r
    gdy = dy * gain
    dx = r * (gdy - xhat * jnp.mean(gdy * xhat, axis=-1, keepdims=True))
    return dx, jnp.sum(dy * xhat, axis=0, keepdims=True)


def _norm_matmul(x, gain, w4, name, comm=None, with_cdf=False, conv=None, tm=1024):
    S, D = x.shape
    nsh, _, ns = w4.shape
    extra = with_cdf or conv is not None
    tm = min(512 if extra else tm, S)

    def body(x_ref, g_ref, w_ref, *rest):
        if conv is not None:
            cw_ref, cb_ref, hn_ref, o_ref, e_ref, tail = rest
        else:
            hn_ref, o_ref = rest[:2]
            e_ref = rest[2] if with_cdf else None
        xf = x_ref[...]
        r = lax.rsqrt(jnp.mean(xf * xf, axis=-1, keepdims=True) + EPS)
        hn = (xf * r * g_ref[...]).astype(BF16)
        hn_ref[...] = hn
        if conv is not None:
            @pl.when(pl.program_id(0) == 0)
            def _():
                tail[...] = jnp.zeros_like(tail)

        for j in range(nsh):
            cols = slice(j * ns, (j + 1) * ns)
            out = _dot(hn, w_ref[j])
            o_ref[:, cols] = out.astype(BF16)
            if with_cdf:
                e_ref[:, cols] = _normal_cdf(out).astype(BF16)
            if conv is not None:
                w0, w1, w2, b = cw_ref[0:1, cols], cw_ref[1:2, cols], cw_ref[2:3, cols], cb_ref[:, cols]
                taps = lambda v: w0 * pltpu.roll(v, 2, 0) + w1 * pltpu.roll(v, 1, 0) + w2 * v + b
                e_ref[:, cols] = taps(out).astype(BF16)
                head = jnp.concatenate([tail[:, cols], out[0:HALO]], axis=0)
                e_ref[0:HALO, cols] = taps(head)[HALO:].astype(BF16)
                tail[:, cols] = out[tm - HALO:tm]

    row = lambda width: pl.BlockSpec((tm, width), lambda i: (i, 0))
    N = nsh * ns
    in_specs = [row(D), _resident((1, D)), _resident(w4.shape)]
    args = (x, gain, w4)
    scratch = []
    if conv is not None:
        in_specs += [_resident((3, N)), _resident((1, N))]
        args += tuple(conv)
        scratch = [pltpu.VMEM((HALO, N), F32)]
    return _call(
        body, name=name, grid=(S // tm,), in_specs=in_specs,
        out_specs=[row(D), row(N)] + [row(N)] * extra,
        out_shape=[jax.ShapeDtypeStruct((S, D), BF16)] + [jax.ShapeDtypeStruct((S, N), BF16)] * (1 + extra),
        scratch_shapes=scratch, sem=("arbitrary",) if conv is not None else ("parallel",), args=args, comm=comm)


def _matmul_res(a, w, res, name, comm=None, tm=1024):
    S, K = a.shape
    N = w.shape[1]
    tm = min(tm, S)

    def body(a_ref, w_ref, r_ref, o_ref):
        o_ref[...] = r_ref[...] + _dot(a_ref[...], w_ref[...])

    return _call(
        body, name=name, grid=(S // tm,),
        in_specs=[pl.BlockSpec((tm, K), lambda i: (i, 0)), _resident(w.shape), pl.BlockSpec((tm, N), lambda i: (i, 0))],
        out_specs=pl.BlockSpec((tm, N), lambda i: (i, 0)),
        out_shape=jax.ShapeDtypeStruct((S, N), F32),
        sem=("parallel",), args=(a, w, res), comm=comm)


def _matmul_nt(dh, w, name, comm=None, tm=1024):
    S, N = dh.shape
    K = w.shape[0]
    tm = min(tm, S)

    def body(d_ref, w_ref, o_ref):
        o_ref[...] = _dot_nt(d_ref[...].astype(BF16), w_ref[...]).astype(BF16)

    return _call(
        body, name=name, grid=(S // tm,),
        in_specs=[pl.BlockSpec((tm, N), lambda i: (i, 0)), _resident(w.shape)],
        out_specs=pl.BlockSpec((tm, K), lambda i: (i, 0)),
        out_shape=jax.ShapeDtypeStruct((S, K), BF16),
        sem=("parallel",), args=(dh, w), comm=comm)


def _matmul_tn(a, b, name, *, ka, nb, out_dtype, comm=None, ts=2048):
    S, KA = a.shape
    NB = b.shape[1]
    ts = min(ts, S)
    J = max(KA // ka, NB // nb)
    a_map = (lambda j, s: (s, j)) if KA // ka > 1 else (lambda j, s: (s, 0))
    b_map = (lambda j, s: (s, j)) if NB // nb > 1 else (lambda j, s: (s, 0))
    last = S // ts - 1

    def body(a_ref, b_ref, o_ref, acc):
        s = pl.program_id(1)

        @pl.when(s == 0)
        def _():
            acc[...] = jnp.zeros_like(acc)

        acc[...] += _dot_tn(a_ref[...].astype(BF16), b_ref[...].astype(BF16))

        @pl.when(s == last)
        def _():
            o_ref[0] = acc[...].astype(out_dtype)

    return _call(
        body, name=name, grid=(J, S // ts),
        in_specs=[pl.BlockSpec((ts, ka), a_map), pl.BlockSpec((ts, nb), b_map)],
        out_specs=pl.BlockSpec((1, ka, nb), lambda j, s: (j, 0, 0)),
        out_shape=jax.ShapeDtypeStruct((J, ka, nb), out_dtype),
        scratch_shapes=[pltpu.VMEM((ka, nb), F32)],
        sem=("parallel", "arbitrary"), args=(a, b), comm=comm)


def _sgu_prep(w_s):
    G = w_s.shape[0]

    def body(w_ref, t_ref, tt_ref):
        tri = lax.broadcasted_iota(jnp.int32, (CHUNK, CHUNK), 0) >= lax.broadcasted_iota(jnp.int32, (CHUNK, CHUNK), 1)
        for g in range(G):
            t = jnp.where(tri, w_ref[g], 0.0)
            t_ref[g] = t.astype(BF16)
            tt_ref[g] = t.T.astype(BF16)

    return pl.pallas_call(
        body, name="sgu_prep",
        out_shape=[jax.ShapeDtypeStruct(w_s.shape, BF16), jax.ShapeDtypeStruct(w_s.shape, BF16)],
        compiler_params=_cp(None),
    )(w_s)


def _sgu_fwd(z, cdf, x, vg, wtril, bT, wout, comm=None, tm=512):
    S = z.shape[0]
    W = z.shape[1] // 2
    D = x.shape[1]
    tm = min(tm, S)

    def body(z_ref, cdf_ref, x_ref, vg_ref, wt_ref, bT_ref, wo_ref, yp_ref, h_ref):
        def chunk(c, carry):
            r0 = pl.multiple_of(c * CHUNK, CHUNK)
            zc = z_ref[pl.ds(r0, CHUNK), :].astype(F32)
            cdf = cdf_ref[pl.ds(r0, CHUNK), :].astype(F32)
            u = zc[:, :W] * cdf[:, :W]
            v = zc[:, W:] * cdf[:, W:]
            rv = lax.rsqrt(jnp.mean(v * v, axis=-1, keepdims=True) + EPS)
            vn = (v * rv * vg_ref[...]).astype(BF16)
            for g in range(SGU_GROUPS):
                sl = slice(g * LANES, (g + 1) * LANES)
                s = _dot(wt_ref[g], vn[:, sl]) + bT_ref[:, g:g + 1]
                yp_ref[pl.ds(r0, CHUNK), sl] = (u[:, sl] * s).astype(BF16)
            return carry

        lax.fori_loop(0, tm // CHUNK, chunk, 0)
        h_ref[...] = x_ref[...] + _dot(yp_ref[...], wo_ref[...])

    return _call(
        body, name="sgu_fwd", grid=(S // tm,),
        in_specs=[pl.BlockSpec((tm, 2 * W), lambda i: (i, 0)), pl.BlockSpec((tm, 2 * W), lambda i: (i, 0)),
                  pl.BlockSpec((tm, D), lambda i: (i, 0)),
                  _resident((1, W)), _resident(wtril.shape), _resident(bT.shape), _resident(wout.shape)],
        out_specs=[pl.BlockSpec((tm, W), lambda i: (i, 0)), pl.BlockSpec((tm, D), lambda i: (i, 0))],
        out_shape=[jax.ShapeDtypeStruct((S, W), BF16), jax.ShapeDtypeStruct((S, D), F32)],
        sem=("parallel",), args=(z, cdf, x, vg, wtril, bT, wout), comm=comm)


def _sgu_bwd(dh, z, cdf, vg, wtril, wtrilT, bT, wout, comm=None, tm=512):
    S = z.shape[0]
    W = z.shape[1] // 2
    D = dh.shape[1]
    G = SGU_GROUPS
    tm = min(tm, S)
    last = S // tm - 1

    def body(dh_ref, z_ref, cdf_ref, vg_ref, wt_ref, wtT_ref, bT_ref, wo_ref,
             dz_ref, dws_ref, dbT_ref, dvg_ref, dyp_s, du_s, dvn_s, dsacc):
        i = pl.program_id(0)

        @pl.when(i == 0)
        def _():
            dws_ref[...] = jnp.zeros_like(dws_ref)
            dvg_ref[...] = jnp.zeros_like(dvg_ref)
            dsacc[...] = jnp.zeros_like(dsacc)

        dyp_s[...] = _dot_nt(dh_ref[...].astype(BF16), wo_ref[...])
        tri = lax.broadcasted_iota(jnp.int32, (CHUNK, CHUNK), 0) >= lax.broadcasted_iota(jnp.int32, (CHUNK, CHUNK), 1)

        def chunk(c, carry):
            r0 = pl.multiple_of(c * CHUNK, CHUNK)
            zc = z_ref[pl.ds(r0, CHUNK), :].astype(F32)
            cdf = cdf_ref[pl.ds(r0, CHUNK), :].astype(F32)
            u, gu = _gelu_and_grad(zc[:, :W], cdf[:, :W])
            v, gv = _gelu_and_grad(zc[:, W:], cdf[:, W:])
            rv = lax.rsqrt(jnp.mean(v * v, axis=-1, keepdims=True) + EPS)
            vhat = v * rv
            vgain = vg_ref[...]
            vn = (vhat * vgain).astype(BF16)
            dyp = dyp_s[pl.ds(r0, CHUNK), :]
            for g in range(G):
                sl = slice(g * LANES, (g + 1) * LANES)
                vng = vn[:, sl]
                s = _dot(wt_ref[g], vng) + bT_ref[:, g:g + 1]
                ds = dyp[:, sl] * u[:, sl]
                du_s[:, sl] = dyp[:, sl] * s
                dsb = ds.astype(BF16)
                dvn_s[:, sl] = _dot(wtT_ref[g], dsb)
                dws_ref[g] += jnp.where(tri, _dot_nt(dsb, vng), 0.0)
                dsacc[g] += ds
            dvn = dvn_s[...]
            dvg_ref[...] += jnp.sum(dvn * vhat, axis=0, keepdims=True)
            gdy = dvn * vgain
            dv = rv * (gdy - vhat * jnp.mean(gdy * vhat, axis=-1, keepdims=True))
            dz_ref[pl.ds(r0, CHUNK), :W] = (du_s[...] * gu).astype(BF16)
            dz_ref[pl.ds(r0, CHUNK), W:] = (dv * gv).astype(BF16)
            return carry

        lax.fori_loop(0, tm // CHUNK, chunk, 0)

        @pl.when(i == last)
        def _():
            for g in range(G):
                dbT_ref[:, g:g + 1] = jnp.sum(dsacc[g], axis=1, keepdims=True)

    return _call(
        body, name="sgu_bwd", grid=(S // tm,),
        in_specs=[pl.BlockSpec((tm, D), lambda i: (i, 0)), pl.BlockSpec((tm, 2 * W), lambda i: (i, 0)),
                  pl.BlockSpec((tm, 2 * W), lambda i: (i, 0)),
                  _resident((1, W)), _resident(wtril.shape), _resident(wtrilT.shape), _resident(bT.shape),
                  _resident(wout.shape)],
        out_specs=[pl.BlockSpec((tm, 2 * W), lambda i: (i, 0)),
                   pl.BlockSpec((G, CHUNK, CHUNK), lambda i: (0, 0, 0)),
                   pl.BlockSpec((CHUNK, G), lambda i: (0, 0)),
                   pl.BlockSpec((1, W), lambda i: (0, 0))],
        out_shape=[jax.ShapeDtypeStruct((S, 2 * W), BF16), jax.ShapeDtypeStruct((G, CHUNK, CHUNK), F32),
                   jax.ShapeDtypeStruct((CHUNK, G), F32), jax.ShapeDtypeStruct((1, W), F32)],
        scratch_shapes=[pltpu.VMEM((tm, W), F32), pltpu.VMEM((CHUNK, W), F32), pltpu.VMEM((CHUNK, W), F32),
                        pltpu.VMEM((G, CHUNK, CHUNK), F32)],
        sem=("arbitrary",), args=(dh, z, cdf, vg, wtril, wtrilT, bT, wout), comm=comm)


def _ffn_fwd(cp, h_in, wdown, name, comm=None, target=None, tm=512, R=128):
    S, C = cp.shape
    F = C // 2
    D = h_in.shape[1]
    tm = min(tm, S)
    head = target is not None

    def body(cp_ref, h_ref, wd_ref, *rest):
        if head:
            t_ref, f_ref, dy_ref, l_ref = rest
        else:
            f_ref, ho_ref = rest
        for r0 in range(0, tm, R):
            g = cp_ref[r0:r0 + R, :F].astype(F32)
            f_ref[r0:r0 + R, :] = (g * _sigmoid(g) * cp_ref[r0:r0 + R, F:].astype(F32)).astype(BF16)
        h_out = h_ref[...] + _dot(f_ref[...], wd_ref[...])
        if not head:
            ho_ref[...] = h_out
            return

        @pl.when(pl.program_id(0) == 0)
        def _():
            l_ref[...] = jnp.zeros_like(l_ref)

        e = h_out - t_ref[...]
        dy_ref[...] = e * (1.0 / D)
        rows = jnp.sum(e * e, axis=-1, keepdims=True) * (1.0 / D)
        l_ref[...] += 0.5 * jnp.sum(rows, axis=0, keepdims=True)

    row = pl.BlockSpec((tm, D), lambda i: (i, 0))
    in_specs = [pl.BlockSpec((tm, C), lambda i: (i, 0)), row, _resident(wdown.shape)]
    out_specs = [pl.BlockSpec((tm, F), lambda i: (i, 0)), row]
    out_shape = [jax.ShapeDtypeStruct((S, F), BF16), jax.ShapeDtypeStruct((S, D), F32)]
    args = (cp, h_in, wdown)
    if head:
        in_specs.append(row)
        out_specs.append(pl.BlockSpec((1, 1), lambda i: (0, 0)))
        out_shape.append(jax.ShapeDtypeStruct((1, 1), F32))
        args += (target,)
    return _call(
        body, name=name, grid=(S // tm,), in_specs=in_specs, out_specs=out_specs, out_shape=out_shape,
        sem=("arbitrary",) if head else ("parallel",), args=args, comm=comm)


def _rows8(v):
    return functools.reduce(jnp.add, [v[8 * k:8 * k + 8] for k in range(v.shape[0] // 8)])


def _ffn_bwd_dc(dh, cp, wdown, name, comm=None, tm=512, R=64):
    S, C = cp.shape
    F = C // 2
    D = dh.shape[1]
    tm = min(tm, S)

    def body(dh_ref, cp_ref, wd_ref, dc_ref, dcb_ref, df_s, acc):
        i = pl.program_id(0)

        @pl.when(i == 0)
        def _():
            acc[...] = jnp.zeros_like(acc)

        df_s[...] = _dot_nt(dh_ref[...].astype(BF16), wd_ref[...])
        for r0 in range(0, tm, R):
            cpre = cp_ref[r0:r0 + R, :].astype(F32)
            g = cpre[:, :F]
            val = cpre[:, F:]
            sg = _sigmoid(g)
            df = df_s[r0:r0 + R, :]
            dg = df * val * (sg * (1.0 + g * (1.0 - sg)))
            dval = df * (g * sg)
            dc = jnp.concatenate([dg, dval], axis=1)
            dc_ref[r0:r0 + R, :] = dc.astype(BF16)
            acc[...] += _rows8(dc)

        @pl.when(i == S // tm - 1)
        def _():
            dcb_ref[...] = jnp.sum(acc[...], axis=0, keepdims=True)

    return _call(
        body, name=name, grid=(S // tm,),
        in_specs=[pl.BlockSpec((tm, D), lambda i: (i, 0)), pl.BlockSpec((tm, C), lambda i: (i, 0)),
                  _resident(wdown.shape)],
        out_specs=[pl.BlockSpec((tm, C), lambda i: (i, 0)), pl.BlockSpec((1, C), lambda i: (0, 0))],
        out_shape=[jax.ShapeDtypeStruct((S, C), BF16), jax.ShapeDtypeStruct((1, C), F32)],
        scratch_shapes=[pltpu.VMEM((tm, F), F32), pltpu.VMEM((8, C), F32)],
        sem=("arbitrary",), args=(dh, cp, wdown), comm=comm)


def _bwd_norm(dA, w4, h_in, gain, dh_out, name, conv_w=None, conv_in=None, comm=None, R=64):
    S, N = dA.shape
    nsh, D, ns = w4.shape
    conv = conv_w is not None
    tm = min(256 if conv else 512, S)
    nt = S // tm

    def finish(src_ref, w_ref, h_ref, g_ref, dho_ref, dhi_ref, dg_ref):
        dhn = _dot_nt(src_ref[:, 0:ns], w_ref[0])
        for j in range(1, nsh):
            dhn += _dot_nt(src_ref[:, j * ns:(j + 1) * ns], w_ref[j])
        dx, dgain = _rms_bwd(dhn, h_ref[...], g_ref[...])
        dg_ref[...] += dgain
        dhi_ref[...] = dho_ref[...] + dx

    def body_plain(dA_ref, w_ref, h_ref, g_ref, dho_ref, dhi_ref, dg_ref):
        @pl.when(pl.program_id(0) == 0)
        def _():
            dg_ref[...] = jnp.zeros_like(dg_ref)

        finish(dA_ref, w_ref, h_ref, g_ref, dho_ref, dhi_ref, dg_ref)

    def body_conv(dc_ref, halo_ref, cw_ref, a_ref, w_ref, h_ref, g_ref, dho_ref, dhi_ref, dg_ref, da_ref, dcw_ref,
                  dc32, acc):
        i = pl.program_id(0)

        @pl.when(i == 0)
        def _():
            dg_ref[...] = jnp.zeros_like(dg_ref)
            acc[...] = jnp.zeros_like(acc)

        dc32[0:tm, :] = dc_ref[...].astype(F32)
        dc32[tm:, :] = jnp.where(i < nt - 1, halo_ref[...].astype(F32), 0.0)

        for r0 in range(0, tm, R):
            X = dc32[r0:r0 + R + 8, :]
            d0 = X[:R]
            d1 = pltpu.roll(X, R + 7, 0)[:R]
            d2 = pltpu.roll(X, R + 6, 0)[:R]
            da = cw_ref[2:3, :] * d0 + cw_ref[1:2, :] * d1 + cw_ref[0:1, :] * d2
            da_ref[r0:r0 + R, :] = da.astype(BF16)
            a0 = a_ref[r0:r0 + R, :].astype(F32)
            acc[0] += _rows8(d2 * a0)
            acc[1] += _rows8(d1 * a0)
            acc[2] += _rows8(d0 * a0)
        finish(da_ref, w_ref, h_ref, g_ref, dho_ref, dhi_ref, dg_ref)

        @pl.when(i == nt - 1)
        def _():
            for k in range(3):
                dcw_ref[k:k + 1, :] = jnp.sum(acc[k], axis=0, keepdims=True)

    row = lambda width: pl.BlockSpec((tm, width), lambda i: (i, 0))
    common_in = [_resident(w4.shape), row(D), _resident((1, D)), row(D)]
    common_out = [row(D), pl.BlockSpec((1, D), lambda i: (0, 0))]
    common_shape = [jax.ShapeDtypeStruct((S, D), F32), jax.ShapeDtypeStruct((1, D), F32)]
    if not conv:
        return _call(
            body_plain, name=name, grid=(nt,),
            in_specs=[row(N)] + common_in, out_specs=common_out, out_shape=common_shape,
            sem=("arbitrary",), args=(dA, w4, h_in, gain, dh_out), comm=comm)
    hb = tm // HALO
    nhb = S // HALO
    return _call(
        body_conv, name=name, grid=(nt,),
        in_specs=[row(N), pl.BlockSpec((HALO, N), lambda i: (jnp.minimum((i + 1) * hb, nhb - 1), 0)),
                  _resident((3, N)), row(N)] + common_in,
        out_specs=common_out + [row(N), pl.BlockSpec((3, N), lambda i: (0, 0))],
        out_shape=common_shape + [jax.ShapeDtypeStruct((S, N), BF16), jax.ShapeDtypeStruct((3, N), F32)],
        scratch_shapes=[pltpu.VMEM((tm + HALO, N), F32), pltpu.VMEM((3, 8, N), F32)],
        sem=("arbitrary",), args=(dA, dA, conv_w, conv_in, w4, h_in, gain, dh_out), comm=comm)


def _rel_buckets_flat():
    q = np.arange(CHUNK)[:, None] + CHUNK
    k = np.arange(2 * CHUNK)[None, :]
    n = np.maximum(q - k, 0)
    max_exact = REL_BUCKETS // 2
    large = max_exact + (np.log(np.maximum(n, 1).astype(np.float32) / max_exact)
                         / math.log(REL_MAX_DIST / max_exact) * (REL_BUCKETS - max_exact)).astype(np.int32)
    large = np.minimum(large, REL_BUCKETS - 1)
    return np.where(n < max_exact, n, large).astype(np.int32).reshape(1, CHUNK * 2 * CHUNK)


def _split_bf16(x):
    hi = x.astype(BF16)
    return hi, (x - hi.astype(F32)).astype(BF16)


def _rel_bias_expand(rel_bias, bucket):
    B, H = rel_bias.shape
    n = bucket.shape[1]

    def body(rb_ref, bk_ref, o_ref):
        oh = (bk_ref[...] == lax.broadcasted_iota(jnp.int32, (B, n), 0)).astype(BF16)
        hi, lo = _split_bf16(rb_ref[...])
        o_ref[...] = _dot_tn(hi, oh) + _dot_tn(lo, oh)

    return pl.pallas_call(body, name="rel_bias_expand", out_shape=jax.ShapeDtypeStruct((H, n), F32),
                          compiler_params=_cp(None))(rel_bias, bucket)


def _rel_bias_reduce(dbias, bucket):
    H, n = dbias.shape
    B = REL_BUCKETS

    def body(db_ref, bk_ref, o_ref):
        oh = (bk_ref[...] == lax.broadcasted_iota(jnp.int32, (B, n), 0)).astype(BF16)
        hi, lo = _split_bf16(db_ref[...])
        o_ref[...] = _dot_nt(oh, hi) + _dot_nt(oh, lo)

    return pl.pallas_call(body, name="rel_bias_reduce", out_shape=jax.ShapeDtypeStruct((B, H), F32),
                          compiler_params=_cp(None))(dbias, bucket)


def _lo_mask(rows):
    return lax.broadcasted_iota(jnp.int32, (rows, LANES), 1) < HEAD_DIM


def _half_sums(y, lo, mxu=False):
    if not mxu:
        s_lo = jnp.sum(jnp.where(lo, y, 0.0), axis=-1, keepdims=True)
        s_hi = jnp.sum(jnp.where(lo, 0.0, y), axis=-1, keepdims=True)
        return jnp.where(lo, s_lo, s_hi)
    i = lax.broadcasted_iota(jnp.int32, (LANES, LANES), 0) < HEAD_DIM
    j = lax.broadcasted_iota(jnp.int32, (LANES, LANES), 1) < HEAD_DIM
    same = (i == j).astype(BF16)
    hi, lo_part = _split_bf16(y)
    return _dot(hi, same) + _dot(lo_part, same)


def _half_rms(x, gain, lo, mxu=False):
    r = lax.rsqrt(_half_sums(x * x, lo, mxu) * (1.0 / HEAD_DIM) + EPS)
    xhat = x * r
    return xhat * gain, xhat, r


def _half_rms_bwd(dy, xhat, r, gain, lo, mxu=False):
    gdy = dy * gain
    dx = r * (gdy - xhat * (_half_sums(gdy * xhat, lo, mxu) * (1.0 / HEAD_DIM)))
    return dx, jnp.sum(dy * xhat, axis=0, keepdims=True)


def _dup_half(pair, e, lo):
    sw = pltpu.roll(pair, HEAD_DIM, 1)
    return jnp.where(lo, pair, sw) if e == 0 else jnp.where(lo, sw, pair)


def _band_valid(n):
    qi = lax.broadcasted_iota(jnp.int32, (KV_GROUP * CHUNK, 2 * CHUNK), 0) & (CHUNK - 1)
    ki = lax.broadcasted_iota(jnp.int32, (KV_GROUP * CHUNK, 2 * CHUNK), 1)
    dist = qi + CHUNK - ki
    return (dist >= 0) & (dist < CHUNK) & ((n > 0) | (ki >= CHUNK))


def _stack_heads(a, b, lo):
    return jnp.concatenate([jnp.where(lo, a, 0.0), jnp.where(lo, 0.0, a), jnp.where(lo, b, 0.0), jnp.where(lo, 0.0, b)],
                           axis=0)


def _unstack_heads(x4, lo):
    return (jnp.where(lo, x4[0:CHUNK], x4[CHUNK:2 * CHUNK]),
            jnp.where(lo, x4[2 * CHUNK:3 * CHUNK], x4[3 * CHUNK:]))


def _sink_col(sink_ref, hk):
    row = lax.broadcasted_iota(jnp.int32, (KV_GROUP * CHUNK, 1), 0)
    col = jnp.full((KV_GROUP * CHUNK, 1), sink_ref[KV_GROUP * hk + KV_GROUP - 1], F32)
    for r in range(KV_GROUP - 2, -1, -1):
        col = jnp.where(row < (r + 1) * CHUNK, sink_ref[KV_GROUP * hk + r], col)
    return col


def _softmax_sink(s, valid, sink):
    s = jnp.where(valid, s, -jnp.inf)
    m = jnp.maximum(jnp.max(s, axis=-1, keepdims=True), sink)
    p = jnp.exp(s - m)
    esink = jnp.exp(sink - m)
    inv = 1.0 / (jnp.sum(p, axis=-1, keepdims=True) + esink)
    return p * inv, esink * inv


QW = N_HEADS * HEAD_DIM
KVW = N_KV_HEADS * HEAD_DIM


def _attn_fwd(qkv, qg2, kg2, sinks, bias, comm=None):
    S = qkv.shape[0]
    nb = S // CHUNK

    def body(cur_ref, prev_ref, qg_ref, kg_ref, sink_ref, bias_ref, o_ref, p_ref, ps_ref):
        n = pl.program_id(0)
        lo = _lo_mask(CHUNK)
        lo2 = _lo_mask(2 * CHUNK)
        valid = _band_valid(n)
        for j in range(N_KV_HEADS // 2):
            kc = slice(QW + j * LANES, QW + (j + 1) * LANES)
            vc = slice(QW + KVW + j * LANES, QW + KVW + (j + 1) * LANES)
            kpair = jnp.concatenate([prev_ref[:, j * LANES:(j + 1) * LANES], cur_ref[:, kc]], axis=0).astype(F32)
            vpair = jnp.concatenate([prev_ref[:, KVW + j * LANES:KVW + (j + 1) * LANES], cur_ref[:, vc]], axis=0).astype(F32)
            knpair, _, _ = _half_rms(kpair, kg_ref[...], lo2)
            for e in range(2):
                hk = 2 * j + e
                kdup = _dup_half(knpair, e, lo2).astype(BF16)
                vdup = _dup_half(vpair, e, lo2).astype(BF16)
                ca = slice(2 * hk * LANES, (2 * hk + 1) * LANES)
                cb = slice((2 * hk + 1) * LANES, (2 * hk + 2) * LANES)
                qna, _, _ = _half_rms(cur_ref[:, ca].astype(F32), qg_ref[...], lo)
                qnb, _, _ = _half_rms(cur_ref[:, cb].astype(F32), qg_ref[...], lo)
                qm4 = _stack_heads(qna, qnb, lo).astype(BF16)
                s = _dot_nt(qm4, kdup) * (HEAD_DIM ** -0.5) + bias_ref[hk]
                p, psink = _softmax_sink(s, valid, _sink_col(sink_ref, hk))
                pb = p.astype(BF16)
                p_ref[0, hk] = pb
                for r in range(KV_GROUP):
                    h = KV_GROUP * hk + r
                    ps_ref[:, h:h + 1] = psink[r * CHUNK:(r + 1) * CHUNK]
                oa, ob = _unstack_heads(_dot(pb, vdup), lo)
                o_ref[:, ca] = oa.astype(BF16)
                o_ref[:, cb] = ob.astype(BF16)

    pshape = (nb, N_KV_HEADS, KV_GROUP * CHUNK, 2 * CHUNK)
    return _call(
        body, name="attn_fwd", grid=(nb,),
        in_specs=[pl.BlockSpec((CHUNK, QW + 2 * KVW), lambda n: (n, 0)),
                  pl.BlockSpec((CHUNK, 2 * KVW), lambda n: (jnp.maximum(n - 1, 0), QW // (2 * KVW))),
                  _resident((1, LANES)), _resident((1, LANES)),
                  pl.BlockSpec(memory_space=pltpu.SMEM),
                  _resident(bias.shape)],
        out_specs=[pl.BlockSpec((CHUNK, QW), lambda n: (n, 0)),
                   pl.BlockSpec((1,) + pshape[1:], lambda n: (n, 0, 0, 0)),
                   pl.BlockSpec((CHUNK, LANES), lambda n: (n, 0))],
        out_shape=[jax.ShapeDtypeStruct((S, QW), BF16), jax.ShapeDtypeStruct(pshape, BF16),
                   jax.ShapeDtypeStruct((S, LANES), F32)],
        sem=("parallel",), args=(qkv, qkv, qg2, kg2, sinks, bias), comm=comm)


def _attn_bwd(qkv, do, probs, psinks, qg2, kg2, comm=None):
    S = qkv.shape[0]
    nb = S // CHUNK

    def body(cur_ref, prev_ref, do_ref, p_ref, ps_ref, qg_ref, kg_ref,
             dqkv_ref, dbias_ref, dqg_ref, dkg_ref, dsink_ref, carry, band, dsacc, gacc):
        i = pl.program_id(0)
        lo = _lo_mask(CHUNK)
        lo2 = _lo_mask(2 * CHUNK)
        lane = lax.broadcasted_iota(jnp.int32, (KV_GROUP * CHUNK, LANES), 1)

        @pl.when(i == 0)
        def _():
            dbias_ref[...] = jnp.zeros_like(dbias_ref)
            carry[...] = jnp.zeros_like(carry)
            dsacc[...] = jnp.zeros_like(dsacc)
            gacc[...] = jnp.zeros_like(gacc)

        qgain = qg_ref[...]
        kgain = kg_ref[...]
        for j in range(N_KV_HEADS // 2):
            kc = slice(QW + j * LANES, QW + (j + 1) * LANES)
            vc = slice(QW + KVW + j * LANES, QW + KVW + (j + 1) * LANES)
            kpair = jnp.concatenate([prev_ref[:, j * LANES:(j + 1) * LANES], cur_ref[:, kc]], axis=0).astype(F32)
            vpair = jnp.concatenate([prev_ref[:, KVW + j * LANES:KVW + (j + 1) * LANES], cur_ref[:, vc]], axis=0).astype(F32)
            knpair, khat, kr = _half_rms(kpair, kgain, lo2, mxu=True)
            dk_folds = []
            dv_folds = []
            for e in range(2):
                hk = 2 * j + e
                kdup = _dup_half(knpair, e, lo2).astype(BF16)
                vdup = _dup_half(vpair, e, lo2).astype(BF16)
                ca = slice(2 * hk * LANES, (2 * hk + 1) * LANES)
                cb = slice((2 * hk + 1) * LANES, (2 * hk + 2) * LANES)
                qna, qhata, qra = _half_rms(cur_ref[:, ca].astype(F32), qgain, lo, mxu=True)
                qnb, qhatb, qrb = _half_rms(cur_ref[:, cb].astype(F32), qgain, lo, mxu=True)
                qm4 = _stack_heads(qna, qnb, lo).astype(BF16)
                dom4 = _stack_heads(do_ref[:, ca].astype(F32), do_ref[:, cb].astype(F32), lo).astype(BF16)
                pb = p_ref[0, hk]
                p = pb.astype(F32)
                psink = jnp.concatenate([ps_ref[:, KV_GROUP * hk + r:KV_GROUP * hk + r + 1] for r in range(KV_GROUP)],
                                        axis=0)
                dp = _dot_nt(dom4, vdup)
                delta = jnp.sum(p * dp, axis=-1, keepdims=True)
                ds = p * (dp - delta)
                dbias_ref[hk] += ds
                dsacc[...] += jnp.where(lane == hk, -(psink * delta), 0.0)
                dsr = (ds * (HEAD_DIM ** -0.5)).astype(BF16)
                dqna, dqnb = _unstack_heads(_dot(dsr, kdup), lo)
                dkd = _dot_tn(dsr, qm4)
                dvd = _dot_tn(pb, dom4)
                dqa, dqga = _half_rms_bwd(dqna, qhata, qra, qgain, lo, mxu=True)
                dqb, dqgb = _half_rms_bwd(dqnb, qhatb, qrb, qgain, lo, mxu=True)
                gacc[0:1, :] += dqga + dqgb
                dqkv_ref[:, ca] = dqa.astype(BF16)
                dqkv_ref[:, cb] = dqb.astype(BF16)
                dk_folds.append(dkd + pltpu.roll(dkd, HEAD_DIM, 1))
                dv_folds.append(dvd + pltpu.roll(dvd, HEAD_DIM, 1))
            dkn = jnp.where(lo2, dk_folds[0], dk_folds[1])
            dk, dkg = _half_rms_bwd(dkn, khat, kr, kgain, lo2, mxu=True)
            gacc[1:2, :] += dkg
            band[:, j * LANES:(j + 1) * LANES] = dk
            band[:, KVW + j * LANES:KVW + (j + 1) * LANES] = jnp.where(lo2, dv_folds[0], dv_folds[1])
        dqkv_ref[:, QW:] = (band[CHUNK:, :] + carry[...]).astype(BF16)
        carry[...] = band[0:CHUNK, :]

        @pl.when(i == nb - 1)
        def _():
            g = gacc[...]
            g = g + pltpu.roll(g, HEAD_DIM, 1)
            dqg_ref[...] = g[0:1, :]
            dkg_ref[...] = g[1:2, :]
            for r in range(KV_GROUP):
                dsink_ref[r:r + 1, :] = jnp.sum(dsacc[r * CHUNK:(r + 1) * CHUNK, :], axis=0, keepdims=True)

    vec = pl.BlockSpec((1, LANES), lambda i: (0, 0))
    bshape = probs.shape[1:]
    return _call(
        body, name="attn_bwd", grid=(nb,),
        in_specs=[pl.BlockSpec((CHUNK, QW + 2 * KVW), lambda i: (nb - 1 - i, 0)),
                  pl.BlockSpec((CHUNK, 2 * KVW), lambda i: (jnp.maximum(nb - 2 - i, 0), QW // (2 * KVW))),
                  pl.BlockSpec((CHUNK, QW), lambda i: (nb - 1 - i, 0)),
                  pl.BlockSpec((1,) + bshape, lambda i: (nb - 1 - i, 0, 0, 0)),
                  pl.BlockSpec((CHUNK, LANES), lambda i: (nb - 1 - i, 0)),
                  _resident((1, LANES)), _resident((1, LANES))],
        out_specs=[pl.BlockSpec((CHUNK, QW + 2 * KVW), lambda i: (nb - 1 - i, 0)),
                   pl.BlockSpec(bshape, lambda i: (0, 0, 0)), vec, vec,
                   pl.BlockSpec((KV_GROUP, LANES), lambda i: (0, 0))],
        out_shape=[jax.ShapeDtypeStruct((S, QW + 2 * KVW), BF16),
                   jax.ShapeDtypeStruct(bshape, F32),
                   jax.ShapeDtypeStruct((1, LANES), F32), jax.ShapeDtypeStruct((1, LANES), F32),
                   jax.ShapeDtypeStruct((KV_GROUP, LANES), F32)],
        scratch_shapes=[pltpu.VMEM((CHUNK, 2 * KVW), F32), pltpu.VMEM((2 * CHUNK, 2 * KVW), F32),
                        pltpu.VMEM((KV_GROUP * CHUNK, LANES), F32), pltpu.VMEM((8, LANES), F32)],
        sem=("arbitrary",), args=(qkv, qkv, do, probs, psinks, qg2, kg2), comm=comm)


def _row_tile(rows, cols, n_arrays):
    budget = VMEM_LIMIT_V7X // 4 // (2 * n_arrays * 4 * cols)
    best = 8
    for n in range(1, rows // 8 + 1):
        if rows % n == 0 and (rows // n) % 8 == 0 and rows // n <= budget:
            best = rows // n
            break
    return best


def _adamw(g, w, m, v, name, comm=None):
    R, C = g.shape
    tr = _row_tile(R, C, 8)

    def body(g_ref, w_ref, m_ref, v_ref, d_ref, mo_ref, vo_ref, go_ref):
        gg = g_ref[...]
        go_ref[...] = gg
        mn = ADAM_B1 * m_ref[...] + (1.0 - ADAM_B1) * gg
        vn = ADAM_B2 * v_ref[...] + (1.0 - ADAM_B2) * jnp.square(gg)
        m_hat = mn / (1.0 - ADAM_B1 ** ADAM_STEP)
        v_hat = vn / (1.0 - ADAM_B2 ** ADAM_STEP)
        d_ref[...] = -ADAM_LR * (m_hat / (jnp.sqrt(v_hat) + ADAM_EPS) + ADAM_WD * w_ref[...])
        mo_ref[...] = mn
        vo_ref[...] = vn

    spec = pl.BlockSpec((tr, C), lambda i: (i, 0))
    return _call(
        body, name=name, grid=(R // tr,), in_specs=[spec] * 4, out_specs=[spec] * 4,
        out_shape=[jax.ShapeDtypeStruct((R, C), F32)] * 4, sem=("parallel",), args=(g, w, m, v), comm=comm)


def _place_shard(shards, layer, where, dtype, name):
    _, R, C = shards.shape
    tr = _row_tile(R, C, 2) if R % 8 == 0 else R

    def body(s_ref, x_ref, o_ref):
        o_ref[...] = x_ref[...].astype(dtype)

    return pl.pallas_call(
        body, name=name,
        grid_spec=pltpu.PrefetchScalarGridSpec(
            num_scalar_prefetch=1, grid=(R // tr,),
            in_specs=[pl.BlockSpec((1, tr, C), lambda i, s_ref: (layer, i, 0))],
            out_specs=pl.BlockSpec((1, tr, C), lambda i, s_ref: (s_ref[1], i, 0))),
        out_shape=jax.ShapeDtypeStruct((4, R, C), dtype),
        compiler_params=_cp(("parallel",)),
    )(where, shards)


def _pair_add(g4, rsib, where, name):
    J, R, C = g4.shape
    Rh = R // 2
    tr = _row_tile(Rh, C, 4)
    g5 = g4.reshape(J, 2, Rh, C)

    def body(s_ref, g_ref, r_ref, p_ref, q_ref):
        val = (g_ref[...].astype(F32)[0] + r_ref[...].astype(F32)).astype(BF16)
        p_ref[...] = val

        @pl.when(pl.program_id(1) == s_ref[1])
        def _():
            q_ref[...] = val

    return pl.pallas_call(
        body, name=name,
        grid_spec=pltpu.PrefetchScalarGridSpec(
            num_scalar_prefetch=1, grid=(Rh // tr, J),
            in_specs=[pl.BlockSpec((1, 1, tr, C), lambda i, j, s_ref: (j, s_ref[0], i, 0)),
                      pl.BlockSpec((1, tr, C), lambda i, j, s_ref: (j, i, 0))],
            out_specs=[pl.BlockSpec((1, tr, C), lambda i, j, s_ref: (j, i, 0)),
                       pl.BlockSpec((1, tr, C), lambda i, j, s_ref: (s_ref[1], i, 0))]),
        out_shape=[jax.ShapeDtypeStruct((J, Rh, C), BF16)] * 2,
        compiler_params=_cp(("parallel", "arbitrary")),
    )(where, g5, rsib)


def _sum_chips(q, where, dest, layer, out_shape, name):
    J, Rh, C = q.shape
    tr = _row_tile(Rh, C, 3)
    nb = Rh // tr

    def body(s_ref, q_ref, *rest):
        qq = q_ref[...].astype(F32)
        rest[-1][0] = ((qq[0] + qq[1]) + qq[2]) + qq[3]

    have = dest is not None
    return pl.pallas_call(
        body, name=name,
        grid_spec=pltpu.PrefetchScalarGridSpec(
            num_scalar_prefetch=1, grid=(nb,),
            in_specs=[pl.BlockSpec((J, tr, C), lambda i, s_ref: (0, i, 0))] + ([ANY] if have else []),
            out_specs=pl.BlockSpec((1, tr, C), lambda i, s_ref: (layer, s_ref[0] * nb + i, 0))),
        out_shape=jax.ShapeDtypeStruct(out_shape, F32),
        input_output_aliases={2: 0} if have else {},
        compiler_params=_cp(("parallel",)),
    )(*((where, q, dest) if have else (where, q)))


MESH = pl.DeviceIdType.MESH
ANY = pl.BlockSpec(memory_space=pl.ANY)


def _place():
    x, y, c = lax.axis_index("x"), lax.axis_index("y"), lax.axis_index("c")
    others = [(1 - x, y), (x, 1 - y), (1 - x, 1 - y)]
    return x, y, c, 2 * x + y, others, [2 * ox + oy for ox, oy in others]


def _gather_comm(items):
    n = len(items)
    placed = [it[0] for it in items]
    split = [it[3] for it in items]

    def rows(t, ref, half):
        _, lo, hi, _ = items[t]
        if not split[t]:
            return ref if (lo, hi) == (0, placed[t].shape[1]) else ref.at[pl.ds(lo, hi - lo), :]
        rh = (hi - lo) // 2
        return ref.at[pl.ds(lo + half * rh, rh), :]

    def sends(outs, sems):
        send, recv = sems[0], sems[1]
        x, y, c, me, others, okey = _place()
        cps = []
        for t in range(n):
            mine = rows(t, outs[t].at[me], c)
            for j, (ox, oy) in enumerate(others):
                cps.append(pltpu.make_async_remote_copy(
                    src_ref=mine, dst_ref=mine,
                    send_sem=send.at[t, j], recv_sem=recv.at[t, j], device_id=(ox, oy, c), device_id_type=MESH))
        return cps

    def start(srcs, outs, news, sems):
        for cp in sends(outs, sems):
            cp.start()

    def forwards(outs, sems):
        fsend, frecv = sems[2], sems[3]
        x, y, c, me, others, okey = _place()
        cps = []
        for t in range(n):
            if split[t]:
                for j in range(3):
                    landed = rows(t, outs[t].at[okey[j]], c)
                    cps.append(pltpu.make_async_remote_copy(
                        src_ref=landed, dst_ref=landed, send_sem=fsend.at[t, j], recv_sem=frecv.at[t, j],
                        device_id=(x, y, 1 - c), device_id_type=MESH))
        return cps

    def middle(srcs, outs, news, sems):
        send, recv = sems[0], sems[1]
        x, y, c, me, others, okey = _place()
        for t in range(n):
            for j in range(3):
                landed = rows(t, outs[t].at[okey[j]], c)
                pltpu.make_async_remote_copy(
                    src_ref=landed, dst_ref=landed, send_sem=send.at[t, j], recv_sem=recv.at[t, j],
                    device_id=(x, y, 1 - c), device_id_type=MESH).wait_recv()
        for cp in forwards(outs, sems):
            cp.start()

    def finish(srcs, outs, news, sems):
        fsend, frecv = sems[2], sems[3]
        x, y, c, me, others, okey = _place()
        for t in range(n):
            if split[t]:
                for j in range(3):
                    theirs = rows(t, outs[t].at[okey[j]], 1 - c)
                    pltpu.make_async_remote_copy(
                        src_ref=theirs, dst_ref=theirs, send_sem=fsend.at[t, j], recv_sem=frecv.at[t, j],
                        device_id=(x, y, 1 - c), device_id_type=MESH).wait_recv()
        for cp in sends(outs, sems) + forwards(outs, sems):
            cp.wait_send()

    return _Comm([], placed, [], [pltpu.SemaphoreType.DMA((n, 3))] * 4, start, finish, middle)


def _pair_exchange_comm(gs):
    n = len(gs)

    def copies(ins, outs, sems):
        send, recv = sems
        x, y, c, _, _, _ = _place()
        cps = []
        for t in range(n):
            rh = gs[t].shape[1] // 2
            cps.append(pltpu.make_async_remote_copy(
                src_ref=ins[t].at[:, pl.ds((1 - c) * rh, rh), :], dst_ref=outs[t],
                send_sem=send.at[t], recv_sem=recv.at[t], device_id=(x, y, 1 - c), device_id_type=MESH))
        return cps

    def start(ins, zones, outs, sems):
        for cp in copies(ins, outs, sems):
            cp.start()

    def finish(ins, zones, outs, sems):
        for cp in copies(ins, outs, sems):
            cp.wait()

    news = [jax.ShapeDtypeStruct((4, g.shape[1] // 2, g.shape[2]), g.dtype) for g in gs]
    return _Comm(gs, [], news, [pltpu.SemaphoreType.DMA((n,))] * 2, start, finish)


def _chip_scatter_comm(ps, qs):
    n = len(ps)

    def sends(ins, outs, sems):
        send, recv = sems
        x, y, c, me, others, okey = _place()
        return [pltpu.make_async_remote_copy(
            src_ref=ins[t].at[okey[j]], dst_ref=outs[t].at[me],
            send_sem=send.at[t, j], recv_sem=recv.at[t, j], device_id=(ox, oy, c), device_id_type=MESH)
            for t in range(n) for j, (ox, oy) in enumerate(others)]

    def start(ins, outs, news, sems):
        for cp in sends(ins, outs, sems):
            cp.start()

    def finish(ins, outs, news, sems):
        send, recv = sems
        x, y, c, me, others, okey = _place()
        for t in range(n):
            for j in range(3):
                slot = outs[t].at[okey[j]]
                pltpu.make_async_remote_copy(
                    src_ref=slot, dst_ref=slot, send_sem=send.at[t, j], recv_sem=recv.at[t, j],
                    device_id=(x, y, c), device_id_type=MESH).wait_recv()
        for cp in sends(ins, outs, sems):
            cp.wait_send()

    return _Comm(ps, qs, [], [pltpu.SemaphoreType.DMA((n, 3))] * 2, start, finish)


def _half_exchange_comm(arrs, layers=None):
    n = len(arrs)
    items = [(t, layer) for t in range(n) for layer in (range(arrs[t].shape[0]) if layers is None else layers[t])]

    def sends(outs, sems):
        send, recv = sems
        x, y, c, _, _, _ = _place()
        cps = []
        for k, (t, layer) in enumerate(items):
            rh = arrs[t].shape[1] // 2
            mine = outs[t].at[layer, pl.ds(c * rh, rh), :]
            cps.append(pltpu.make_async_remote_copy(
                src_ref=mine, dst_ref=mine, send_sem=send.at[k], recv_sem=recv.at[k],
                device_id=(x, y, 1 - c), device_id_type=MESH))
        return cps

    def start(srcs, outs, news, sems):
        for cp in sends(outs, sems):
            cp.start()

    def finish(srcs, outs, news, sems):
        send, recv = sems
        x, y, c, _, _, _ = _place()
        for k, (t, layer) in enumerate(items):
            rh = arrs[t].shape[1] // 2
            theirs = outs[t].at[layer, pl.ds((1 - c) * rh, rh), :]
            pltpu.make_async_remote_copy(
                src_ref=theirs, dst_ref=theirs, send_sem=send.at[k], recv_sem=recv.at[k],
                device_id=(x, y, 1 - c), device_id_type=MESH).wait_recv()
        for cp in sends(outs, sems):
            cp.wait_send()

    return _Comm([], arrs, [], [pltpu.SemaphoreType.DMA((len(items),))] * 2, start, finish)


SMALL_COLS = 1024
SMALL_PIECE_ROWS = 48


def _allreduce_small(buf, comm=None):
    pr = SMALL_PIECE_ROWS
    flips = [(d >> 2 & 1, d >> 1 & 1, d & 1) for d in range(1, 8)]

    def body(x_ref, o_ref, rbuf, send1, recv1, send2, recv2):
        x, y, c = lax.axis_index("x"), lax.axis_index("y"), lax.axis_index("c")
        me = 4 * x + 2 * y + c
        peers = [(x ^ fx, y ^ fy, c ^ fc) for fx, fy, fc in flips]
        pid = [4 * px + 2 * py + pc for px, py, pc in peers]

        def piece(ref, p):
            return ref.at[pl.ds(pl.multiple_of(p * pr, 8), pr), :]

        cps = []
        for d in range(7):
            cp = pltpu.make_async_remote_copy(
                src_ref=piece(x_ref, pid[d]), dst_ref=rbuf.at[d + 1],
                send_sem=send1.at[d], recv_sem=recv1.at[d], device_id=peers[d], device_id_type=MESH)
            cp.start()
            cps.append(cp)
        acc = piece(x_ref, me)[...]
        for d in range(7):
            cps[d].wait_recv()
            acc = acc + rbuf[d + 1]
        piece(o_ref, me)[...] = acc
        out = []
        for d in range(7):
            cp = pltpu.make_async_remote_copy(
                src_ref=piece(o_ref, me), dst_ref=piece(o_ref, me),
                send_sem=send2.at[d], recv_sem=recv2.at[d], device_id=peers[d], device_id_type=MESH)
            cp.start()
            out.append(cp)
        for d in range(7):
            pltpu.make_async_remote_copy(
                src_ref=piece(o_ref, pid[d]), dst_ref=piece(o_ref, pid[d]),
                send_sem=send2.at[d], recv_sem=recv2.at[d], device_id=peers[d], device_id_type=MESH).wait_recv()
        for cp in cps + out:
            cp.wait_send()

    vm = pl.BlockSpec(memory_space=pltpu.VMEM)
    return _call(
        body, name="small_allreduce", grid=(), in_specs=[vm], out_specs=vm,
        out_shape=jax.ShapeDtypeStruct(buf.shape, F32),
        scratch_shapes=[pltpu.VMEM((8, pr, SMALL_COLS), F32)] + [pltpu.SemaphoreType.DMA((7,))] * 4,
        args=(buf,), comm=comm)


def _rows_of(shape):
    return -(-math.prod(shape) // (8 * SMALL_COLS)) * 8


def _pack(arrays, rows):
    parts = []
    for a in arrays:
        r = _rows_of(a.shape)
        parts.append(jnp.pad(a.reshape(-1), (0, r * SMALL_COLS - a.size)).reshape(r, SMALL_COLS))
    used = sum(p.shape[0] for p in parts)
    if rows > used:
        parts.append(jnp.zeros((rows - used, SMALL_COLS), F32))
    return jnp.concatenate(parts, axis=0)


def _unpack(buf, shapes):
    out, off = [], 0
    for s in shapes:
        r = _rows_of(s)
        out.append(buf[off:off + r].reshape(-1)[:math.prod(s)].reshape(s))
        off += r
    return out


BIG = ["sgu_w_in", "sgu_w_out", "attn_w_qkv", "attn_w_o", "ffn_w_up", "ffn_w_down"]
SMALL = ["mix_norm", "ffn_norm", "sgu_v_gain", "sgu_w_s", "sgu_b_s", "attn_q_gain", "attn_k_gain", "attn_sinks",
         "rel_bias", "ffn_conv_b"]
ORDER = ["mix_norm", "ffn_norm", "sgu_w_in", "sgu_v_gain", "sgu_w_s", "sgu_b_s", "sgu_w_out", "attn_w_qkv",
         "attn_q_gain", "attn_k_gain", "attn_sinks", "attn_w_o", "rel_bias", "ffn_w_up", "ffn_conv_w", "ffn_conv_b",
         "ffn_w_down"]


def kernel(x, mix_norm, ffn_norm, sgu_w_in, sgu_v_gain, sgu_w_s, sgu_b_s, sgu_w_out, attn_w_qkv, attn_q_gain, attn_k_gain, attn_sinks, attn_w_o, rel_bias, ffn_w_up, ffn_conv_w, ffn_conv_b, ffn_w_down, loss_target, m_mix_norm, m_ffn_norm, m_sgu_w_in, m_sgu_v_gain, m_sgu_w_s, m_sgu_b_s, m_sgu_w_out, m_attn_w_qkv, m_attn_q_gain, m_attn_k_gain, m_attn_sinks, m_attn_w_o, m_rel_bias, m_ffn_w_up, m_ffn_conv_w, m_ffn_conv_b, m_ffn_w_down, v_mix_norm, v_ffn_norm, v_sgu_w_in, v_sgu_v_gain, v_sgu_w_s, v_sgu_b_s, v_sgu_w_out, v_attn_w_qkv, v_attn_q_gain, v_attn_k_gain, v_attn_sinks, v_attn_w_o, v_rel_bias, v_ffn_w_up, v_ffn_conv_w, v_ffn_conv_b, v_ffn_w_down):
    w = dict(mix_norm=mix_norm, ffn_norm=ffn_norm, sgu_w_in=sgu_w_in, sgu_v_gain=sgu_v_gain, sgu_w_s=sgu_w_s,
             sgu_b_s=sgu_b_s, sgu_w_out=sgu_w_out, attn_w_qkv=attn_w_qkv, attn_q_gain=attn_q_gain,
             attn_k_gain=attn_k_gain, attn_sinks=attn_sinks, attn_w_o=attn_w_o, rel_bias=rel_bias, ffn_w_up=ffn_w_up,
             ffn_conv_w=ffn_conv_w, ffn_conv_b=ffn_conv_b, ffn_w_down=ffn_w_down)
    mom = dict(mix_norm=m_mix_norm, ffn_norm=m_ffn_norm, sgu_w_in=m_sgu_w_in, sgu_v_gain=m_sgu_v_gain,
               sgu_w_s=m_sgu_w_s, sgu_b_s=m_sgu_b_s, sgu_w_out=m_sgu_w_out, attn_w_qkv=m_attn_w_qkv,
               attn_q_gain=m_attn_q_gain, attn_k_gain=m_attn_k_gain, attn_sinks=m_attn_sinks, attn_w_o=m_attn_w_o,
               rel_bias=m_rel_bias, ffn_w_up=m_ffn_w_up, ffn_conv_w=m_ffn_conv_w, ffn_conv_b=m_ffn_conv_b,
               ffn_w_down=m_ffn_w_down)
    var = dict(mix_norm=v_mix_norm, ffn_norm=v_ffn_norm, sgu_w_in=v_sgu_w_in, sgu_v_gain=v_sgu_v_gain,
               sgu_w_s=v_sgu_w_s, sgu_b_s=v_sgu_b_s, sgu_w_out=v_sgu_w_out, attn_w_qkv=v_attn_w_qkv,
               attn_q_gain=v_attn_q_gain, attn_k_gain=v_attn_k_gain, attn_sinks=v_attn_sinks, attn_w_o=v_attn_w_o,
               rel_bias=v_rel_bias, ffn_w_up=v_ffn_w_up, ffn_conv_w=v_ffn_conv_w, ffn_conv_b=v_ffn_conv_b,
               ffn_w_down=v_ffn_w_down)
    chip = 2 * lax.axis_index("x") + lax.axis_index("y")
    core = lax.axis_index("c")

    where = jnp.stack([core, chip]).astype(jnp.int32)
    names = ["sgu_w_in", "sgu_w_out", "attn_w_qkv", "attn_w_o", "ffn_w_up0", "ffn_w_up1", "ffn_w_down0", "ffn_w_down1"]
    shards = [(sgu_w_in, 0), (sgu_w_out, 0), (attn_w_qkv, 0), (attn_w_o, 0), (ffn_w_up, 0), (ffn_w_up, 1),
              (ffn_w_down, 0), (ffn_w_down, 1)]
    T = {nm: _place_shard(s, l, where, BF16, "place_" + nm) for (s, l), nm in zip(shards, names)}
    for l in range(2):
        T["conv_w%d" % l] = _place_shard(ffn_conv_w, l, where, F32, "place_conv_w%d" % l)

    def gather(keys):
        items = []
        for k in keys:
            k, lo, hi = (k, 0, None) if isinstance(k, str) else k
            items.append((T[k], lo, T[k].shape[1] if hi is None else hi, not k.startswith("conv")))
        return _gather_comm(items)

    def gathered(keys, res):
        for k, a in zip(keys, res[0]):
            T[k if isinstance(k, str) else k[0]] = a

    D = x.shape[2]
    first = ["sgu_w_in", "conv_w0", "conv_w1"]
    gathered(first, _run_comm(gather(first), "gather_first"))
    unshard_cols = lambda a: jnp.transpose(a, (1, 0, 2)).reshape(a.shape[1], -1)
    cw = [unshard_cols(T["conv_w0"]), unshard_cols(T["conv_w1"])]
    cb = ffn_conv_b
    flat = lambda k: T[k].reshape(-1, D)
    x2, target = x[0], loss_target[0]
    bucket = jnp.asarray(_rel_buckets_flat())
    wtril, wtrilT = _sgu_prep(sgu_w_s[0])
    bT = sgu_b_s[0].T
    bias = _rel_bias_expand(rel_bias, bucket).reshape(N_KV_HEADS, KV_GROUP * CHUNK, 2 * CHUNK)
    qg2 = jnp.tile(attn_q_gain, (1, 2))
    kg2 = jnp.tile(attn_k_gain, (1, 2))
    sinks = attn_sinks.reshape(N_HEADS)
    mix0, mix1 = mix_norm[0:1], mix_norm[1:2]
    fn0, fn1 = ffn_norm[0:1], ffn_norm[1:2]

    ks = ["sgu_w_out", ("ffn_w_up0", 0, 384)]
    (hn0, z, cdf), r = _norm_matmul(x2, mix0, T["sgu_w_in"], "sgu_in", comm=gather(ks), with_cdf=True)
    gathered(ks, r)
    ks = [("ffn_w_up0", 384, None)]
    (yp, h1), r = _sgu_fwd(z, cdf, x2, sgu_v_gain, wtril, bT, flat("sgu_w_out"), comm=gather(ks))
    gathered(ks, r)
    ks = ["ffn_w_down0", "attn_w_qkv", "attn_w_o"]
    (hf0, a0, cp0), r = _norm_matmul(h1, fn0, T["ffn_w_up0"], "ffn0_up", comm=gather(ks), conv=(cw[0], cb[0:1]))
    gathered(ks, r)
    ks = [("ffn_w_up1", 0, 384)]
    (f0, h2), r = _ffn_fwd(cp0, h1, flat("ffn_w_down0"), "ffn0_fwd", comm=gather(ks))
    gathered(ks, r)
    (hn1, qkv), _ = _norm_matmul(h2, mix1, T["attn_w_qkv"], "attn_qkv")
    ks = [("ffn_w_up1", 384, None)]
    (o, probs, psinks), r = _attn_fwd(qkv, qg2, kg2, sinks, bias, comm=gather(ks))
    gathered(ks, r)
    h3, _ = _matmul_res(o, flat("attn_w_o"), h2, "attn_out")
    ks = ["ffn_w_down1"]
    (hf1, a1, cp1), r = _norm_matmul(h3, fn1, T["ffn_w_up1"], "ffn1_up", comm=gather(ks), conv=(cw[1], cb[1:2]))
    gathered(ks, r)
    (f1, dh4, loss_local), _ = _ffn_fwd(cp1, h3, flat("ffn_w_down1"), "ffn1_fwd", target=target)

    G, RS, PQ, QD, halves, grads = {}, {}, {}, {}, {}, {}
    dest_of = {"sgu_w_in": ("sgu_w_in", 0), "sgu_w_out": ("sgu_w_out", 0), "attn_w_qkv": ("attn_w_qkv", 0),
               "attn_w_o": ("attn_w_o", 0), "ffn_w_up0": ("ffn_w_up", 0), "ffn_w_up1": ("ffn_w_up", 1),
               "ffn_w_down0": ("ffn_w_down", 0), "ffn_w_down1": ("ffn_w_down", 1)}

    def px(keys):
        return _pair_exchange_comm([G[k] for k in keys])

    def px_done(keys, res):
        for k, a in zip(keys, res[1]):
            RS[k] = a
            PQ[k] = _pair_add(G[k], a, where, "pair_add_" + k)

    def sc(keys):
        return _chip_scatter_comm([PQ[k][0] for k in keys], [PQ[k][1] for k in keys])

    def sc_done(keys, res):
        for k, a in zip(keys, res[0]):
            wk, layer = dest_of[k]
            halves[wk] = _sum_chips(a, where, halves.get(wk), layer, w[wk].shape, "sum_chips_" + k)

    nup = ffn_w_up.shape[2]
    ndown = ffn_w_down.shape[1]
    rows4 = lambda a: a.reshape(4, ndown, D)
    (dc1, dcb1), _ = _ffn_bwd_dc(dh4, cp1, flat("ffn_w_down1"), "ffn1_bwd_dc")
    gw, _ = _matmul_tn(f1, dh4, "ffn1_dw_down", ka=2 * ndown, nb=D, out_dtype=BF16)
    G["ffn_w_down1"] = rows4(gw)
    (dh3, dfn1, da1, dcw1), _ = _bwd_norm(dc1, T["ffn_w_up1"], h3, fn1, dh4, "ffn1_bwd_in", conv_w=cw[1], conv_in=a1)
    G["ffn_w_up1"], _ = _matmul_tn(hf1, da1, "ffn1_dw_up", ka=D, nb=nup, out_dtype=BF16)
    ks = ["ffn_w_down1", "ffn_w_up1"]
    G["attn_w_o"], r = _matmul_tn(o, dh3, "attn_dw_o", ka=QW // 4, nb=D, out_dtype=BF16, comm=px(ks))
    px_done(ks, r)
    do, _ = _matmul_nt(dh3, flat("attn_w_o"), "attn_bwd_out")
    (dqkv, dbias, dqg, dkg, dsink), r = _attn_bwd(qkv, do, probs, psinks, qg2, kg2, comm=sc(ks))
    sc_done(ks, r)
    G["attn_w_qkv"], _ = _matmul_tn(hn1, dqkv, "attn_dw_qkv", ka=D, nb=dqkv.shape[1] // 4, out_dtype=BF16)
    ks = ["attn_w_o", "attn_w_qkv"]
    (dh2, dmix1), r = _bwd_norm(dqkv, T["attn_w_qkv"], h2, mix1, dh3, "attn_bwd_in", comm=px(ks))
    px_done(ks, r)
    (dc0, dcb0), r = _ffn_bwd_dc(dh2, cp0, flat("ffn_w_down0"), "ffn0_bwd_dc", comm=sc(ks))
    sc_done(ks, r)
    gw, _ = _matmul_tn(f0, dh2, "ffn0_dw_down", ka=2 * ndown, nb=D, out_dtype=BF16)
    G["ffn_w_down0"] = rows4(gw)
    (dh1, dfn0, da0, dcw0), r = _bwd_norm(dc0, T["ffn_w_up0"], h1, fn0, dh2, "ffn0_bwd_in", conv_w=cw[0], conv_in=a0,
                                          comm=px(["ffn_w_down0"]))
    px_done(["ffn_w_down0"], r)
    G["ffn_w_up0"], r = _matmul_tn(hf0, da0, "ffn0_dw_up", ka=D, nb=nup, out_dtype=BF16, comm=sc(["ffn_w_down0"]))
    sc_done(["ffn_w_down0"], r)
    ks = ["ffn_w_up0"]
    G["sgu_w_out"], r = _matmul_tn(yp, dh1, "sgu_dw_out", ka=yp.shape[1] // 4, nb=D, out_dtype=BF16, comm=px(ks))
    px_done(ks, r)
    both = [sc(ks), px(["sgu_w_out"])]
    (dz, dws, dbT, dvg), r = _sgu_bwd(dh1, z, cdf, sgu_v_gain, wtril, wtrilT, bT, flat("sgu_w_out"),
                                      comm=_join(both))
    r = _split(both, r)
    sc_done(ks, r[0])
    px_done(["sgu_w_out"], r[1])
    done = ["attn_w_qkv", "attn_w_o", "ffn_w_up", "ffn_w_down"]
    both = [sc(["sgu_w_out"]), _half_exchange_comm([halves[k] for k in done])]
    G["sgu_w_in"], r = _matmul_tn(hn0, dz, "sgu_dw_in", ka=D, nb=dz.shape[1] // 4, out_dtype=BF16, comm=_join(both))
    r = _split(both, r)
    sc_done(["sgu_w_out"], r[0])
    for k, a in zip(done, r[1][0]):
        grads[k] = a
    px_done(["sgu_w_in"], _run_comm(px(["sgu_w_in"]), "grad_last_pair_exchange"))
    (grad_x, dmix0), r = _bwd_norm(dz, T["sgu_w_in"], x2, mix0, dh1, "sgu_bwd_in", comm=sc(["sgu_w_in"]))
    sc_done(["sgu_w_in"], r)

    g = dict(mix_norm=jnp.concatenate([dmix0, dmix1], axis=0), ffn_norm=jnp.concatenate([dfn0, dfn1], axis=0),
             sgu_v_gain=dvg, sgu_w_s=dws, sgu_b_s=dbT.T, attn_q_gain=dqg[:, :HEAD_DIM], attn_k_gain=dkg[:, :HEAD_DIM],
             attn_sinks=dsink[:, :N_KV_HEADS].T.reshape(1, N_HEADS),
             rel_bias=_rel_bias_reduce(dbias.reshape(N_HEADS, CHUNK * 2 * CHUNK), bucket),
             ffn_conv_b=jnp.concatenate([dcb0, dcb1], axis=0))
    small_list = [g[k].reshape(w[k].shape) for k in SMALL] + [jnp.stack([dcw0, dcw1]), loss_local]
    small_shapes = [a.shape for a in small_list]
    done = ["sgu_w_in", "sgu_w_out"]
    red, r = _allreduce_small(_pack(small_list, 8 * SMALL_PIECE_ROWS),
                              comm=_half_exchange_comm([halves[k] for k in done]))
    for k, a in zip(done, r[0]):
        grads[k] = a
    red = _unpack(red, small_shapes)
    for k, a in zip(SMALL, red):
        grads[k] = a
    grads["ffn_conv_w"] = lax.dynamic_slice_in_dim(red[-2], chip * ffn_conv_w.shape[2], ffn_conv_w.shape[2], axis=2)
    loss = red[-1][0, 0]

    delta, new_m, new_v = {}, {}, {}
    two = lambda a: a.reshape(-1, a.shape[-1])
    for k in BIG:
        (d2, m2, v2, g2), _ = _adamw(two(grads[k]), two(w[k]), two(mom[k]), two(var[k]), "adamw_" + k)
        delta[k], new_m[k], new_v[k], grads[k] = (a.reshape(w[k].shape) for a in (d2, m2, v2, g2))
    sm = SMALL + ["ffn_conv_w"]
    sm_shapes = [w[k].shape for k in sm]
    rows = sum(_rows_of(s) for s in sm_shapes)
    (d2, m2, v2, _), _ = _adamw(_pack([grads[k] for k in sm], rows), _pack([w[k] for k in sm], rows),
                                _pack([mom[k] for k in sm], rows), _pack([var[k] for k in sm], rows), "adamw_small")
    for dst, buf in ((delta, d2), (new_m, m2), (new_v, v2)):
        for k, a in zip(sm, _unpack(buf, sm_shapes)):
            dst[k] = a

    return (loss, grad_x[None], *[grads[k] for k in ORDER], *[delta[k] for k in ORDER],
            *[new_m[k] for k in ORDER], *[new_v[k] for k in ORDER])
```

```python
import functools
import math

import numpy as np
import jax
import jax.numpy as jnp
from jax import lax
from jax.experimental import pallas as pl
from jax.experimental.pallas import tpu as pltpu

F32 = jnp.float32
BF16 = jnp.bfloat16

EPS = 1e-6
CHUNK = 128
SGU_GROUPS = 16
HEAD_DIM = 64
N_HEADS = 16
N_KV_HEADS = 4
KV_GROUP = N_HEADS // N_KV_HEADS
REL_BUCKETS = 32
REL_MAX_DIST = 128
LANES = 128
HALO = 16

ADAM_LR = 0.001
ADAM_B1 = 0.9
ADAM_B2 = 0.999
ADAM_EPS = 1e-08
ADAM_WD = 0.01
ADAM_STEP = 10

VMEM_LIMIT_V7X = 56 * 1024 * 1024

_SQRT_HALF = math.sqrt(0.5)
_INV_SQRT_2PI = 1.0 / math.sqrt(2.0 * math.pi)


def _cp(sem):
    return pltpu.CompilerParams(dimension_semantics=sem, vmem_limit_bytes=VMEM_LIMIT_V7X)


def _resident(shape):
    nd = len(shape)
    return pl.BlockSpec(shape, lambda *_: (0,) * nd, pipeline_mode=pl.Buffered(1))


def _fetch_shards(w_hbm, w_vmem, sems):
    first = pl.program_id(0) == 0
    copies = [pltpu.make_async_copy(w_hbm.at[j], w_vmem.at[j], sems.at[j]) for j in range(w_hbm.shape[0])]

    @pl.when(first)
    def _():
        for cp in copies:
            cp.start()

    def arrived(j):
        @pl.when(first)
        def _():
            copies[j].wait()

    return arrived


class _Comm:
    def __init__(self, srcs, zones, news, sems, start, finish, middle=None):
        self.srcs, self.zones, self.news, self.sems = list(srcs), list(zones), list(news), list(sems)
        self.start, self.finish = start, finish
        self.middle = middle if middle is not None else (lambda srcs, zones, news, sems: None)


def _join(comms):
    comms = [c for c in comms if c is not None]
    if not comms:
        return None

    def part(seq, attr):
        out, k = [], 0
        for c in comms:
            n = len(getattr(c, attr))
            out.append(seq[k:k + n])
            k += n
        return out

    def run(which):
        def f(srcs, zones, news, sems):
            for c, a, b, d, e in zip(comms, part(srcs, "srcs"), part(zones, "zones"), part(news, "news"), part(sems, "sems")):
                getattr(c, which)(a, b, d, e)
        return f

    cat = lambda attr: [v for c in comms for v in getattr(c, attr)]
    return _Comm(cat("srcs"), cat("zones"), cat("news"), cat("sems"), run("start"), run("finish"), run("middle"))


def _split(comms, res):
    zones, news = res
    out, kz, kn = [], 0, 0
    for c in comms:
        out.append((zones[kz:kz + len(c.zones)], news[kn:kn + len(c.news)]))
        kz += len(c.zones)
        kn += len(c.news)
    return out


def _call(body, *, name, grid, in_specs, out_specs, out_shape, args, scratch_shapes=(), sem=None, comm=None):
    if comm is None:
        res = pl.pallas_call(body, name=name, grid=grid, in_specs=in_specs, out_specs=out_specs, out_shape=out_shape,
                             scratch_shapes=list(scratch_shapes), compiler_params=_cp(sem))(*args)
        return res, None
    single = not isinstance(out_shape, (list, tuple))
    out_specs_l = [out_specs] if single else list(out_specs)
    out_shape_l = [out_shape] if single else list(out_shape)
    n_in, n_out, n_scr = len(in_specs), len(out_shape_l), len(scratch_shapes)
    ns, nz, nn = len(comm.srcs), len(comm.zones), len(comm.news)

    def wrapped(*refs):
        k = n_in
        ins, srcs = refs[:k], refs[k:k + ns]
        k += ns + nz
        outs, zones, news = refs[k:k + n_out], refs[k + n_out:k + n_out + nz], refs[k + n_out + nz:k + n_out + nz + nn]
        k += n_out + nz + nn
        scr, sems = refs[k:k + n_scr], refs[k + n_scr:]
        if not grid:
            comm.start(srcs, zones, news, sems)
            body(*ins, *outs, *scr)
            comm.middle(srcs, zones, news, sems)
            comm.finish(srcs, zones, news, sems)
            return
        first = functools.reduce(jnp.logical_and, [pl.program_id(a) == 0 for a in range(len(grid))])
        last = functools.reduce(jnp.logical_and, [pl.program_id(a) == grid[a] - 1 for a in range(len(grid))])
        early = len(grid) == 1 and grid[0] >= 2
        mid_step = grid[0] - (2 if grid[0] >= 8 else 1)

        @pl.when(first)
        def _():
            comm.start(srcs, zones, news, sems)

        if early:
            @pl.when(pl.program_id(0) == mid_step)
            def _():
                comm.middle(srcs, zones, news, sems)

        body(*ins, *outs, *scr)

        @pl.when(last)
        def _():
            if not early:
                comm.middle(srcs, zones, news, sems)
            comm.finish(srcs, zones, news, sems)

    res = pl.pallas_call(
        wrapped, name=name, grid=grid,
        in_specs=list(in_specs) + [ANY] * (ns + nz), out_specs=out_specs_l + [ANY] * (nz + nn),
        out_shape=out_shape_l + [jax.ShapeDtypeStruct(z.shape, z.dtype) for z in comm.zones] + comm.news,
        input_output_aliases={n_in + ns + i: n_out + i for i in range(nz)},
        scratch_shapes=list(scratch_shapes) + comm.sems,
        compiler_params=_cp(("arbitrary",) * len(grid)),
    )(*args, *comm.srcs, *comm.zones)
    main = res[0] if single else list(res[:n_out])
    return main, (list(res[n_out:n_out + nz]), list(res[n_out + nz:]))


def _run_comm(comm, name):
    ns, nz, nn = len(comm.srcs), len(comm.zones), len(comm.news)

    def body(*refs):
        srcs, zones, news, sems = refs[:ns], refs[ns + nz:ns + 2 * nz], refs[ns + 2 * nz:ns + 2 * nz + nn], refs[ns + 2 * nz + nn:]
        comm.start(srcs, zones, news, sems)
        comm.middle(srcs, zones, news, sems)
        comm.finish(srcs, zones, news, sems)

    res = pl.pallas_call(
        body, name=name, in_specs=[ANY] * (ns + nz), out_specs=[ANY] * (nz + nn),
        out_shape=[jax.ShapeDtypeStruct(z.shape, z.dtype) for z in comm.zones] + comm.news,
        input_output_aliases={ns + i: i for i in range(nz)}, scratch_shapes=comm.sems,
    )(*comm.srcs, *comm.zones)
    return list(res[:nz]), list(res[nz:])


def _dot(a, b):
    return jnp.dot(a, b, preferred_element_type=F32)


def _dot_nt(a, b):
    return lax.dot_general(a, b, (((1,), (1,)), ((), ())), preferred_element_type=F32)


def _dot_tn(a, b):
    return lax.dot_general(a, b, (((0,), (0,)), ((), ())), preferred_element_type=F32)


def _normal_cdf(x):
    return 0.5 * (1.0 + lax.erf(x * _SQRT_HALF))


def _gelu_and_grad(x, cdf):
    return x * cdf, cdf + x * jnp.exp(-0.5 * x * x) * _INV_SQRT_2PI


def _sigmoid(x):
    return 0.5 * jnp.tanh(0.5 * x) + 0.5


def _rms_bwd(dy, x, gain):
    r = lax.rsqrt(jnp.mean(x * x, axis=-1, keepdims=True) + EPS)
    xhat = x * r
    gdy = dy * gain
    dx = r * (gdy - xhat * jnp.mean(gdy * xhat, axis=-1, keepdims=True))
    return dx, jnp.sum(dy * xhat, axis=0, keepdims=True)


def _norm_matmul(x, gain, w4, name, comm=None, with_cdf=False, conv=None, tm=1024):
    S, D = x.shape
    nsh, _, ns = w4.shape
    extra = with_cdf or conv is not None
    tm = min(512 if extra else tm, S)

    def body(x_ref, g_ref, w_hbm, *rest):
        *rest, w_ref, w_sem = rest
        if conv is not None:
            cw_ref, cb_ref, hn_ref, o_ref, e_ref, tail = rest
        else:
            hn_ref, o_ref = rest[:2]
            e_ref = rest[2] if with_cdf else None
        arrived = _fetch_shards(w_hbm, w_ref, w_sem)
        xf = x_ref[...]
        r = lax.rsqrt(jnp.mean(xf * xf, axis=-1, keepdims=True) + EPS)
        hn = (xf * r * g_ref[...]).astype(BF16)
        hn_ref[...] = hn
        if conv is not None:
            @pl.when(pl.program_id(0) == 0)
            def _():
                tail[...] = jnp.zeros_like(tail)

        for j in range(nsh):
            cols = slice(j * ns, (j + 1) * ns)
            arrived(j)
            out = _dot(hn, w_ref[j])
            o_ref[:, cols] = out.astype(BF16)
            if with_cdf:
                e_ref[:, cols] = _normal_cdf(out).astype(BF16)
            if conv is not None:
                w0, w1, w2, b = cw_ref[0:1, cols], cw_ref[1:2, cols], cw_ref[2:3, cols], cb_ref[:, cols]
                taps = lambda v: w0 * pltpu.roll(v, 2, 0) + w1 * pltpu.roll(v, 1, 0) + w2 * v + b
                e_ref[:, cols] = taps(out).astype(BF16)
                head = jnp.concatenate([tail[:, cols], out[0:HALO]], axis=0)
                e_ref[0:HALO, cols] = taps(head)[HALO:].astype(BF16)
                tail[:, cols] = out[tm - HALO:tm]

    row = lambda width: pl.BlockSpec((tm, width), lambda i: (i, 0))
    N = nsh * ns
    in_specs = [row(D), _resident((1, D)), ANY]
    args = (x, gain, w4)
    scratch = []
    if conv is not None:
        in_specs += [_resident((3, N)), _resident((1, N))]
        args += tuple(conv)
        scratch = [pltpu.VMEM((HALO, N), F32)]
    scratch += [pltpu.VMEM(w4.shape, w4.dtype), pltpu.SemaphoreType.DMA((nsh,))]
    return _call(
        body, name=name, grid=(S // tm,), in_specs=in_specs,
        out_specs=[row(D), row(N)] + [row(N)] * extra,
        out_shape=[jax.ShapeDtypeStruct((S, D), BF16)] + [jax.ShapeDtypeStruct((S, N), BF16)] * (1 + extra),
        scratch_shapes=scratch, sem=("arbitrary",), args=args, comm=comm)


def _matmul_res(a, w, res, name, comm=None, tm=1024):
    S, K = a.shape
    N = w.shape[1]
    tm = min(tm, S)

    def body(a_ref, w_ref, r_ref, o_ref):
        o_ref[...] = r_ref[...] + _dot(a_ref[...], w_ref[...])

    return _call(
        body, name=name, grid=(S // tm,),
        in_specs=[pl.BlockSpec((tm, K), lambda i: (i, 0)), _resident(w.shape), pl.BlockSpec((tm, N), lambda i: (i, 0))],
        out_specs=pl.BlockSpec((tm, N), lambda i: (i, 0)),
        out_shape=jax.ShapeDtypeStruct((S, N), F32),
        sem=("parallel",), args=(a, w, res), comm=comm)


def _matmul_nt(dh, w, name, comm=None, tm=1024):
    S, N = dh.shape
    K = w.shape[0]
    tm = min(tm, S)

    def body(d_ref, w_ref, o_ref):
        o_ref[...] = _dot_nt(d_ref[...].astype(BF16), w_ref[...]).astype(BF16)

    return _call(
        body, name=name, grid=(S // tm,),
        in_specs=[pl.BlockSpec((tm, N), lambda i: (i, 0)), _resident(w.shape)],
        out_specs=pl.BlockSpec((tm, K), lambda i: (i, 0)),
        out_shape=jax.ShapeDtypeStruct((S, K), BF16),
        sem=("parallel",), args=(dh, w), comm=comm)


def _matmul_tn(a, b, name, *, ka, nb, out_dtype, comm=None, ts=2048):
    S, KA = a.shape
    NB = b.shape[1]
    ts = min(ts, S)
    J = max(KA // ka, NB // nb)
    a_map = (lambda j, s: (s, j)) if KA // ka > 1 else (lambda j, s: (s, 0))
    b_map = (lambda j, s: (s, j)) if NB // nb > 1 else (lambda j, s: (s, 0))
    last = S // ts - 1

    def body(a_ref, b_ref, o_ref, acc):
        s = pl.program_id(1)

        @pl.when(s == 0)
        def _():
            acc[...] = jnp.zeros_like(acc)

        acc[...] += _dot_tn(a_ref[...].astype(BF16), b_ref[...].astype(BF16))

        @pl.when(s == last)
        def _():
            o_ref[0] = acc[...].astype(out_dtype)

    return _call(
        body, name=name, grid=(J, S // ts),
        in_specs=[pl.BlockSpec((ts, ka), a_map), pl.BlockSpec((ts, nb), b_map)],
        out_specs=pl.BlockSpec((1, ka, nb), lambda j, s: (j, 0, 0)),
        out_shape=jax.ShapeDtypeStruct((J, ka, nb), out_dtype),
        scratch_shapes=[pltpu.VMEM((ka, nb), F32)],
        sem=("parallel", "arbitrary"), args=(a, b), comm=comm)


def _sgu_prep(w_s):
    G = w_s.shape[0]

    def body(w_ref, t_ref, tt_ref):
        tri = lax.broadcasted_iota(jnp.int32, (CHUNK, CHUNK), 0) >= lax.broadcasted_iota(jnp.int32, (CHUNK, CHUNK), 1)
        for g in range(G):
            t = jnp.where(tri, w_ref[g], 0.0)
            t_ref[g] = t.astype(BF16)
            tt_ref[g] = t.T.astype(BF16)

    return pl.pallas_call(
        body, name="sgu_prep",
        out_shape=[jax.ShapeDtypeStruct(w_s.shape, BF16), jax.ShapeDtypeStruct(w_s.shape, BF16)],
        compiler_params=_cp(None),
    )(w_s)


def _sgu_fwd(z, cdf, x, vg, wtril, bT, wout, comm=None, tm=512):
    S = z.shape[0]
    W = z.shape[1] // 2
    D = x.shape[1]
    tm = min(tm, S)

    def body(z_ref, cdf_ref, x_ref, vg_ref, wt_ref, bT_ref, wo_ref, yp_ref, h_ref):
        def chunk(c, carry):
            r0 = pl.multiple_of(c * CHUNK, CHUNK)
            zc = z_ref[pl.ds(r0, CHUNK), :].astype(F32)
            cdf = cdf_ref[pl.ds(r0, CHUNK), :].astype(F32)
            u = zc[:, :W] * cdf[:, :W]
            v = zc[:, W:] * cdf[:, W:]
            rv = lax.rsqrt(jnp.mean(v * v, axis=-1, keepdims=True) + EPS)
            vn = (v * rv * vg_ref[...]).astype(BF16)
            for g in range(SGU_GROUPS):
                sl = slice(g * LANES, (g + 1) * LANES)
                s = _dot(wt_ref[g], vn[:, sl]) + bT_ref[:, g:g + 1]
                yp_ref[pl.ds(r0, CHUNK), sl] = (u[:, sl] * s).astype(BF16)
            return carry

        lax.fori_loop(0, tm // CHUNK, chunk, 0)
        h_ref[...] = x_ref[...] + _dot(yp_ref[...], wo_ref[...])

    return _call(
        body, name="sgu_fwd", grid=(S // tm,),
        in_specs=[pl.BlockSpec((tm, 2 * W), lambda i: (i, 0)), pl.BlockSpec((tm, 2 * W), lambda i: (i, 0)),
                  pl.BlockSpec((tm, D), lambda i: (i, 0)),
                  _resident((1, W)), _resident(wtril.shape), _resident(bT.shape), _resident(wout.shape)],
        out_specs=[pl.BlockSpec((tm, W), lambda i: (i, 0)), pl.BlockSpec((tm, D), lambda i: (i, 0))],
        out_shape=[jax.ShapeDtypeStruct((S, W), BF16), jax.ShapeDtypeStruct((S, D), F32)],
        sem=("parallel",), args=(z, cdf, x, vg, wtril, bT, wout), comm=comm)


def _sgu_bwd(dh, z, cdf, vg, wtril, wtrilT, bT, wout, comm=None, tm=512):
    S = z.shape[0]
    W = z.shape[1] // 2
    D = dh.shape[1]
    G = SGU_GROUPS
    tm = min(tm, S)
    last = S // tm - 1

    def body(dh_ref, z_ref, cdf_ref, vg_ref, wt_ref, wtT_ref, bT_ref, wo_ref,
             dz_ref, dws_ref, dbT_ref, dvg_ref, dyp_s, du_s, dvn_s, dsacc):
        i = pl.program_id(0)

        @pl.when(i == 0)
        def _():
            dws_ref[...] = jnp.zeros_like(dws_ref)
            dvg_ref[...] = jnp.zeros_like(dvg_ref)
            dsacc[...] = jnp.zeros_like(dsacc)

        dyp_s[...] = _dot_nt(dh_ref[...].astype(BF16), wo_ref[...])
        tri = lax.broadcasted_iota(jnp.int32, (CHUNK, CHUNK), 0) >= lax.broadcasted_iota(jnp.int32, (CHUNK, CHUNK), 1)

        def chunk(c, carry):
            r0 = pl.multiple_of(c * CHUNK, CHUNK)
            zc = z_ref[pl.ds(r0, CHUNK), :].astype(F32)
            cdf = cdf_ref[pl.ds(r0, CHUNK), :].astype(F32)
            u, gu = _gelu_and_grad(zc[:, :W], cdf[:, :W])
            v, gv = _gelu_and_grad(zc[:, W:], cdf[:, W:])
            rv = lax.rsqrt(jnp.mean(v * v, axis=-1, keepdims=True) + EPS)
            vhat = v * rv
            vgain = vg_ref[...]
            vn = (vhat * vgain).astype(BF16)
            dyp = dyp_s[pl.ds(r0, CHUNK), :]
            for g in range(G):
                sl = slice(g * LANES, (g + 1) * LANES)
                vng = vn[:, sl]
                s = _dot(wt_ref[g], vng) + bT_ref[:, g:g + 1]
                ds = dyp[:, sl] * u[:, sl]
                du_s[:, sl] = dyp[:, sl] * s
                dsb = ds.astype(BF16)
                dvn_s[:, sl] = _dot(wtT_ref[g], dsb)
                dws_ref[g] += jnp.where(tri, _dot_nt(dsb, vng), 0.0)
                dsacc[g] += ds
            dvn = dvn_s[...]
            dvg_ref[...] += jnp.sum(dvn * vhat, axis=0, keepdims=True)
            gdy = dvn * vgain
            dv = rv * (gdy - vhat * jnp.mean(gdy * vhat, axis=-1, keepdims=True))
            dz_ref[pl.ds(r0, CHUNK), :W] = (du_s[...] * gu).astype(BF16)
            dz_ref[pl.ds(r0, CHUNK), W:] = (dv * gv).astype(BF16)
            return carry

        lax.fori_loop(0, tm // CHUNK, chunk, 0)

        @pl.when(i == last)
        def _():
            for g in range(G):
                dbT_ref[:, g:g + 1] = jnp.sum(dsacc[g], axis=1, keepdims=True)

    return _call(
        body, name="sgu_bwd", grid=(S // tm,),
        in_specs=[pl.BlockSpec((tm, D), lambda i: (i, 0)), pl.BlockSpec((tm, 2 * W), lambda i: (i, 0)),
                  pl.BlockSpec((tm, 2 * W), lambda i: (i, 0)),
                  _resident((1, W)), _resident(wtril.shape), _resident(wtrilT.shape), _resident(bT.shape),
                  _resident(wout.shape)],
        out_specs=[pl.BlockSpec((tm, 2 * W), lambda i: (i, 0)),
                   pl.BlockSpec((G, CHUNK, CHUNK), lambda i: (0, 0, 0)),
                   pl.BlockSpec((CHUNK, G), lambda i: (0, 0)),
                   pl.BlockSpec((1, W), lambda i: (0, 0))],
        out_shape=[jax.ShapeDtypeStruct((S, 2 * W), BF16), jax.ShapeDtypeStruct((G, CHUNK, CHUNK), F32),
                   jax.ShapeDtypeStruct((CHUNK, G), F32), jax.ShapeDtypeStruct((1, W), F32)],
        scratch_shapes=[pltpu.VMEM((tm, W), F32), pltpu.VMEM((CHUNK, W), F32), pltpu.VMEM((CHUNK, W), F32),
                        pltpu.VMEM((G, CHUNK, CHUNK), F32)],
        sem=("arbitrary",), args=(dh, z, cdf, vg, wtril, wtrilT, bT, wout), comm=comm)


def _ffn_fwd(cp, h_in, wdown, name, comm=None, target=None, tm=512, R=128):
    S, C = cp.shape
    F = C // 2
    D = h_in.shape[1]
    tm = min(tm, S)
    head = target is not None

    def body(cp_ref, h_ref, wd_ref, *rest):
        if head:
            t_ref, f_ref, dy_ref, l_ref = rest
        else:
            f_ref, ho_ref = rest
        for r0 in range(0, tm, R):
            g = cp_ref[r0:r0 + R, :F].astype(F32)
            f_ref[r0:r0 + R, :] = (g * _sigmoid(g) * cp_ref[r0:r0 + R, F:].astype(F32)).astype(BF16)
        h_out = h_ref[...] + _dot(f_ref[...], wd_ref[...])
        if not head:
            ho_ref[...] = h_out
            return

        @pl.when(pl.program_id(0) == 0)
        def _():
            l_ref[...] = jnp.zeros_like(l_ref)

        e = h_out - t_ref[...]
        dy_ref[...] = e * (1.0 / D)
        rows = jnp.sum(e * e, axis=-1, keepdims=True) * (1.0 / D)
        l_ref[...] += 0.5 * jnp.sum(rows, axis=0, keepdims=True)

    row = pl.BlockSpec((tm, D), lambda i: (i, 0))
    in_specs = [pl.BlockSpec((tm, C), lambda i: (i, 0)), row, _resident(wdown.shape)]
    out_specs = [pl.BlockSpec((tm, F), lambda i: (i, 0)), row]
    out_shape = [jax.ShapeDtypeStruct((S, F), BF16), jax.ShapeDtypeStruct((S, D), F32)]
    args = (cp, h_in, wdown)
    if head:
        in_specs.append(row)
        out_specs.append(pl.BlockSpec((1, 1), lambda i: (0, 0)))
        out_shape.append(jax.ShapeDtypeStruct((1, 1), F32))
        args += (target,)
    return _call(
        body, name=name, grid=(S // tm,), in_specs=in_specs, out_specs=out_specs, out_shape=out_shape,
        sem=("arbitrary",) if head else ("parallel",), args=args, comm=comm)


def _rows8(v):
    return functools.reduce(jnp.add, [v[8 * k:8 * k + 8] for k in range(v.shape[0] // 8)])


def _ffn_bwd_dc(dh, cp, wdown, name, comm=None, tm=512, R=64):
    S, C = cp.shape
    F = C // 2
    D = dh.shape[1]
    tm = min(tm, S)

    def body(dh_ref, cp_ref, wd_ref, dc_ref, dcb_ref, df_s, acc):
        i = pl.program_id(0)

        @pl.when(i == 0)
        def _():
            acc[...] = jnp.zeros_like(acc)

        df_s[...] = _dot_nt(dh_ref[...].astype(BF16), wd_ref[...])
        for r0 in range(0, tm, R):
            cpre = cp_ref[r0:r0 + R, :].astype(F32)
            g = cpre[:, :F]
            val = cpre[:, F:]
            sg = _sigmoid(g)
            df = df_s[r0:r0 + R, :]
            dg = df * val * (sg * (1.0 + g * (1.0 - sg)))
            dval = df * (g * sg)
            dc = jnp.concatenate([dg, dval], axis=1)
            dc_ref[r0:r0 + R, :] = dc.astype(BF16)
            acc[...] += _rows8(dc)

        @pl.when(i == S // tm - 1)
        def _():
            dcb_ref[...] = jnp.sum(acc[...], axis=0, keepdims=True)

    return _call(
        body, name=name, grid=(S // tm,),
        in_specs=[pl.BlockSpec((tm, D), lambda i: (i, 0)), pl.BlockSpec((tm, C), lambda i: (i, 0)),
                  _resident(wdown.shape)],
        out_specs=[pl.BlockSpec((tm, C), lambda i: (i, 0)), pl.BlockSpec((1, C), lambda i: (0, 0))],
        out_shape=[jax.ShapeDtypeStruct((S, C), BF16), jax.ShapeDtypeStruct((1, C), F32)],
        scratch_shapes=[pltpu.VMEM((tm, F), F32), pltpu.VMEM((8, C), F32)],
        sem=("arbitrary",), args=(dh, cp, wdown), comm=comm)


def _bwd_norm(dA, w4, h_in, gain, dh_out, name, conv_w=None, conv_in=None, comm=None, R=64):
    S, N = dA.shape
    nsh, D, ns = w4.shape
    conv = conv_w is not None
    tm = min(256 if conv else 512, S)
    nt = S // tm

    def finish(src_ref, w_ref, arrived, h_ref, g_ref, dho_ref, dhi_ref, dg_ref):
        dhn = None
        for j in range(nsh):
            arrived(j)
            part = _dot_nt(src_ref[:, j * ns:(j + 1) * ns], w_ref[j])
            dhn = part if dhn is None else dhn + part
        dx, dgain = _rms_bwd(dhn, h_ref[...], g_ref[...])
        dg_ref[...] += dgain
        dhi_ref[...] = dho_ref[...] + dx

    def body_plain(dA_ref, w_hbm, h_ref, g_ref, dho_ref, dhi_ref, dg_ref, w_ref, w_sem):
        arrived = _fetch_shards(w_hbm, w_ref, w_sem)

        @pl.when(pl.program_id(0) == 0)
        def _():
            dg_ref[...] = jnp.zeros_like(dg_ref)

        finish(dA_ref, w_ref, arrived, h_ref, g_ref, dho_ref, dhi_ref, dg_ref)

    def body_conv(dc_ref, halo_ref, cw_ref, a_ref, w_hbm, h_ref, g_ref, dho_ref, dhi_ref, dg_ref, da_ref, dcw_ref,
                  dc32, acc, w_ref, w_sem):
        i = pl.program_id(0)
        arrived = _fetch_shards(w_hbm, w_ref, w_sem)

        @pl.when(i == 0)
        def _():
            dg_ref[...] = jnp.zeros_like(dg_ref)
            acc[...] = jnp.zeros_like(acc)

        dc32[0:tm, :] = dc_ref[...].astype(F32)
        dc32[tm:, :] = jnp.where(i < nt - 1, halo_ref[...].astype(F32), 0.0)

        for r0 in range(0, tm, R):
            X = dc32[r0:r0 + R + 8, :]
            d0 = X[:R]
            d1 = pltpu.roll(X, R + 7, 0)[:R]
            d2 = pltpu.roll(X, R + 6, 0)[:R]
            da = cw_ref[2:3, :] * d0 + cw_ref[1:2, :] * d1 + cw_ref[0:1, :] * d2
            da_ref[r0:r0 + R, :] = da.astype(BF16)
            a0 = a_ref[r0:r0 + R, :].astype(F32)
            acc[0] += _rows8(d2 * a0)
            acc[1] += _rows8(d1 * a0)
            acc[2] += _rows8(d0 * a0)
        finish(da_ref, w_ref, arrived, h_ref, g_ref, dho_ref, dhi_ref, dg_ref)

        @pl.when(i == nt - 1)
        def _():
            for k in range(3):
                dcw_ref[k:k + 1, :] = jnp.sum(acc[k], axis=0, keepdims=True)

    row = lambda width: pl.BlockSpec((tm, width), lambda i: (i, 0))
    common_in = [ANY, row(D), _resident((1, D)), row(D)]
    common_out = [row(D), pl.BlockSpec((1, D), lambda i: (0, 0))]
    common_shape = [jax.ShapeDtypeStruct((S, D), F32), jax.ShapeDtypeStruct((1, D), F32)]
    w_scratch = [pltpu.VMEM(w4.shape, w4.dtype), pltpu.SemaphoreType.DMA((nsh,))]
    if not conv:
        return _call(
            body_plain, name=name, grid=(nt,),
            in_specs=[row(N)] + common_in, out_specs=common_out, out_shape=common_shape,
            scratch_shapes=w_scratch, sem=("arbitrary",), args=(dA, w4, h_in, gain, dh_out), comm=comm)
    hb = tm // HALO
    nhb = S // HALO
    return _call(
        body_conv, name=name, grid=(nt,),
        in_specs=[row(N), pl.BlockSpec((HALO, N), lambda i: (jnp.minimum((i + 1) * hb, nhb - 1), 0)),
                  _resident((3, N)), row(N)] + common_in,
        out_specs=common_out + [row(N), pl.BlockSpec((3, N), lambda i: (0, 0))],
        out_shape=common_shape + [jax.ShapeDtypeStruct((S, N), BF16), jax.ShapeDtypeStruct((3, N), F32)],
        scratch_shapes=[pltpu.VMEM((tm + HALO, N), F32), pltpu.VMEM((3, 8, N), F32)] + w_scratch,
        sem=("arbitrary",), args=(dA, dA, conv_w, conv_in, w4, h_in, gain, dh_out), comm=comm)


def _rel_buckets_flat():
    q = np.arange(CHUNK)[:, None] + CHUNK
    k = np.arange(2 * CHUNK)[None, :]
    n = np.maximum(q - k, 0)
    max_exact = REL_BUCKETS // 2
    large = max_exact + (np.log(np.maximum(n, 1).astype(np.float32) / max_exact)
                         / math.log(REL_MAX_DIST / max_exact) * (REL_BUCKETS - max_exact)).astype(np.int32)
    large = np.minimum(large, REL_BUCKETS - 1)
    return np.where(n < max_exact, n, large).astype(np.int32).reshape(1, CHUNK * 2 * CHUNK)


def _split_bf16(x):
    hi = x.astype(BF16)
    return hi, (x - hi.astype(F32)).astype(BF16)


def _rel_bias_expand(rel_bias, bucket):
    B, H = rel_bias.shape
    n = bucket.shape[1]

    def body(rb_ref, bk_ref, o_ref):
        oh = (bk_ref[...] == lax.broadcasted_iota(jnp.int32, (B, n), 0)).astype(BF16)
        hi, lo = _split_bf16(rb_ref[...])
        o_ref[...] = _dot_tn(hi, oh) + _dot_tn(lo, oh)

    return pl.pallas_call(body, name="rel_bias_expand", out_shape=jax.ShapeDtypeStruct((H, n), F32),
                          compiler_params=_cp(None))(rel_bias, bucket)


def _rel_bias_reduce(dbias, bucket):
    H, n = dbias.shape
    B = REL_BUCKETS

    def body(db_ref, bk_ref, o_ref):
        oh = (bk_ref[...] == lax.broadcasted_iota(jnp.int32, (B, n), 0)).astype(BF16)
        hi, lo = _split_bf16(db_ref[...])
        o_ref[...] = _dot_nt(oh, hi) + _dot_nt(oh, lo)

    return pl.pallas_call(body, name="rel_bias_reduce", out_shape=jax.ShapeDtypeStruct((B, H), F32),
                          compiler_params=_cp(None))(dbias, bucket)


def _lo_mask(rows):
    return lax.broadcasted_iota(jnp.int32, (rows, LANES), 1) < HEAD_DIM


def _half_sums(y, lo, mxu=False):
    if not mxu:
        s_lo = jnp.sum(jnp.where(lo, y, 0.0), axis=-1, keepdims=True)
        s_hi = jnp.sum(jnp.where(lo, 0.0, y), axis=-1, keepdims=True)
        return jnp.where(lo, s_lo, s_hi)
    i = lax.broadcasted_iota(jnp.int32, (LANES, LANES), 0) < HEAD_DIM
    j = lax.broadcasted_iota(jnp.int32, (LANES, LANES), 1) < HEAD_DIM
    same = (i == j).astype(BF16)
    hi, lo_part = _split_bf16(y)
    return _dot(hi, same) + _dot(lo_part, same)


def _half_rms(x, gain, lo, mxu=False):
    r = lax.rsqrt(_half_sums(x * x, lo, mxu) * (1.0 / HEAD_DIM) + EPS)
    xhat = x * r
    return xhat * gain, xhat, r


def _half_rms_bwd(dy, xhat, r, gain, lo, mxu=False):
    gdy = dy * gain
    dx = r * (gdy - xhat * (_half_sums(gdy * xhat, lo, mxu) * (1.0 / HEAD_DIM)))
    return dx, jnp.sum(dy * xhat, axis=0, keepdims=True)


def _dup_half(pair, e, lo):
    sw = pltpu.roll(pair, HEAD_DIM, 1)
    return jnp.where(lo, pair, sw) if e == 0 else jnp.where(lo, sw, pair)


def _band_valid(n):
    qi = lax.broadcasted_iota(jnp.int32, (KV_GROUP * CHUNK, 2 * CHUNK), 0) & (CHUNK - 1)
    ki = lax.broadcasted_iota(jnp.int32, (KV_GROUP * CHUNK, 2 * CHUNK), 1)
    dist = qi + CHUNK - ki
    return (dist >= 0) & (dist < CHUNK) & ((n > 0) | (ki >= CHUNK))


def _stack_heads(a, b, lo):
    return jnp.concatenate([jnp.where(lo, a, 0.0), jnp.where(lo, 0.0, a), jnp.where(lo, b, 0.0), jnp.where(lo, 0.0, b)],
                           axis=0)


def _unstack_heads(x4, lo):
    return (jnp.where(lo, x4[0:CHUNK], x4[CHUNK:2 * CHUNK]),
            jnp.where(lo, x4[2 * CHUNK:3 * CHUNK], x4[3 * CHUNK:]))


def _sink_col(sink_ref, hk):
    row = lax.broadcasted_iota(jnp.int32, (KV_GROUP * CHUNK, 1), 0)
    col = jnp.full((KV_GROUP * CHUNK, 1), sink_ref[KV_GROUP * hk + KV_GROUP - 1], F32)
    for r in range(KV_GROUP - 2, -1, -1):
        col = jnp.where(row < (r + 1) * CHUNK, sink_ref[KV_GROUP * hk + r], col)
    return col


def _softmax_sink(s, valid, sink):
    s = jnp.where(valid, s, -jnp.inf)
    m = jnp.maximum(jnp.max(s, axis=-1, keepdims=True), sink)
    p = jnp.exp(s - m)
    esink = jnp.exp(sink - m)
    inv = 1.0 / (jnp.sum(p, axis=-1, keepdims=True) + esink)
    return p * inv, esink * inv


QW = N_HEADS * HEAD_DIM
KVW = N_KV_HEADS * HEAD_DIM


def _attn_fwd(qkv, qg2, kg2, sinks, bias, comm=None):
    S = qkv.shape[0]
    nb = S // CHUNK

    def body(cur_ref, prev_ref, qg_ref, kg_ref, sink_ref, bias_ref, o_ref, p_ref, ps_ref):
        n = pl.program_id(0)
        lo = _lo_mask(CHUNK)
        lo2 = _lo_mask(2 * CHUNK)
        valid = _band_valid(n)
        for j in range(N_KV_HEADS // 2):
            kc = slice(QW + j * LANES, QW + (j + 1) * LANES)
            vc = slice(QW + KVW + j * LANES, QW + KVW + (j + 1) * LANES)
            kpair = jnp.concatenate([prev_ref[:, j * LANES:(j + 1) * LANES], cur_ref[:, kc]], axis=0).astype(F32)
            vpair = jnp.concatenate([prev_ref[:, KVW + j * LANES:KVW + (j + 1) * LANES], cur_ref[:, vc]], axis=0).astype(F32)
            knpair, _, _ = _half_rms(kpair, kg_ref[...], lo2)
            for e in range(2):
                hk = 2 * j + e
                kdup = _dup_half(knpair, e, lo2).astype(BF16)
                vdup = _dup_half(vpair, e, lo2).astype(BF16)
                ca = slice(2 * hk * LANES, (2 * hk + 1) * LANES)
                cb = slice((2 * hk + 1) * LANES, (2 * hk + 2) * LANES)
                qna, _, _ = _half_rms(cur_ref[:, ca].astype(F32), qg_ref[...], lo)
                qnb, _, _ = _half_rms(cur_ref[:, cb].astype(F32), qg_ref[...], lo)
                qm4 = _stack_heads(qna, qnb, lo).astype(BF16)
                s = _dot_nt(qm4, kdup) * (HEAD_DIM ** -0.5) + bias_ref[hk]
                p, psink = _softmax_sink(s, valid, _sink_col(sink_ref, hk))
                pb = p.astype(BF16)
                p_ref[0, hk] = pb
                for r in range(KV_GROUP):
                    h = KV_GROUP * hk + r
                    ps_ref[:, h:h + 1] = psink[r * CHUNK:(r + 1) * CHUNK]
                oa, ob = _unstack_heads(_dot(pb, vdup), lo)
                o_ref[:, ca] = oa.astype(BF16)
                o_ref[:, cb] = ob.astype(BF16)

    pshape = (nb, N_KV_HEADS, KV_GROUP * CHUNK, 2 * CHUNK)
    return _call(
        body, name="attn_fwd", grid=(nb,),
        in_specs=[pl.BlockSpec((CHUNK, QW + 2 * KVW), lambda n: (n, 0)),
                  pl.BlockSpec((CHUNK, 2 * KVW), lambda n: (jnp.maximum(n - 1, 0), QW // (2 * KVW))),
                  _resident((1, LANES)), _resident((1, LANES)),
                  pl.BlockSpec(memory_space=pltpu.SMEM),
                  _resident(bias.shape)],
        out_specs=[pl.BlockSpec((CHUNK, QW), lambda n: (n, 0)),
                   pl.BlockSpec((1,) + pshape[1:], lambda n: (n, 0, 0, 0)),
                   pl.BlockSpec((CHUNK, LANES), lambda n: (n, 0))],
        out_shape=[jax.ShapeDtypeStruct((S, QW), BF16), jax.ShapeDtypeStruct(pshape, BF16),
                   jax.ShapeDtypeStruct((S, LANES), F32)],
        sem=("parallel",), args=(qkv, qkv, qg2, kg2, sinks, bias), comm=comm)


def _attn_bwd(qkv, do, probs, psinks, qg2, kg2, comm=None):
    S = qkv.shape[0]
    nb = S // CHUNK

    def body(cur_ref, prev_ref, do_ref, p_ref, ps_ref, qg_ref, kg_ref,
             dqkv_ref, dbias_ref, dqg_ref, dkg_ref, dsink_ref, carry, band, dsacc, gacc):
        i = pl.program_id(0)
        lo = _lo_mask(CHUNK)
        lo2 = _lo_mask(2 * CHUNK)
        lane = lax.broadcasted_iota(jnp.int32, (KV_GROUP * CHUNK, LANES), 1)

        @pl.when(i == 0)
        def _():
            dbias_ref[...] = jnp.zeros_like(dbias_ref)
            carry[...] = jnp.zeros_like(carry)
            dsacc[...] = jnp.zeros_like(dsacc)
            gacc[...] = jnp.zeros_like(gacc)

        qgain = qg_ref[...]
        kgain = kg_ref[...]
        for j in range(N_KV_HEADS // 2):
            kc = slice(QW + j * LANES, QW + (j + 1) * LANES)
            vc = slice(QW + KVW + j * LANES, QW + KVW + (j + 1) * LANES)
            kpair = jnp.concatenate([prev_ref[:, j * LANES:(j + 1) * LANES], cur_ref[:, kc]], axis=0).astype(F32)
            vpair = jnp.concatenate([prev_ref[:, KVW + j * LANES:KVW + (j + 1) * LANES], cur_ref[:, vc]], axis=0).astype(F32)
            knpair, khat, kr = _half_rms(kpair, kgain, lo2, mxu=True)
            dk_folds = []
            dv_folds = []
            for e in range(2):
                hk = 2 * j + e
                kdup = _dup_half(knpair, e, lo2).astype(BF16)
                vdup = _dup_half(vpair, e, lo2).astype(BF16)
                ca = slice(2 * hk * LANES, (2 * hk + 1) * LANES)
                cb = slice((2 * hk + 1) * LANES, (2 * hk + 2) * LANES)
                qna, qhata, qra = _half_rms(cur_ref[:, ca].astype(F32), qgain, lo, mxu=True)
                qnb, qhatb, qrb = _half_rms(cur_ref[:, cb].astype(F32), qgain, lo, mxu=True)
                qm4 = _stack_heads(qna, qnb, lo).astype(BF16)
                dom4 = _stack_heads(do_ref[:, ca].astype(F32), do_ref[:, cb].astype(F32), lo).astype(BF16)
                pb = p_ref[0, hk]
                p = pb.astype(F32)
                psink = jnp.concatenate([ps_ref[:, KV_GROUP * hk + r:KV_GROUP * hk + r + 1] for r in range(KV_GROUP)],
                                        axis=0)
                dp = _dot_nt(dom4, vdup)
                delta = jnp.sum(p * dp, axis=-1, keepdims=True)
                ds = p * (dp - delta)
                dbias_ref[hk] += ds
                dsacc[...] += jnp.where(lane == hk, -(psink * delta), 0.0)
                dsr = (ds * (HEAD_DIM ** -0.5)).astype(BF16)
                dqna, dqnb = _unstack_heads(_dot(dsr, kdup), lo)
                dkd = _dot_tn(dsr, qm4)
                dvd = _dot_tn(pb, dom4)
                dqa, dqga = _half_rms_bwd(dqna, qhata, qra, qgain, lo, mxu=True)
                dqb, dqgb = _half_rms_bwd(dqnb, qhatb, qrb, qgain, lo, mxu=True)
                gacc[0:1, :] += dqga + dqgb
                dqkv_ref[:, ca] = dqa.astype(BF16)
                dqkv_ref[:, cb] = dqb.astype(BF16)
                dk_folds.append(dkd + pltpu.roll(dkd, HEAD_DIM, 1))
                dv_folds.append(dvd + pltpu.roll(dvd, HEAD_DIM, 1))
            dkn = jnp.where(lo2, dk_folds[0], dk_folds[1])
            dk, dkg = _half_rms_bwd(dkn, khat, kr, kgain, lo2, mxu=True)
            gacc[1:2, :] += dkg
            band[:, j * LANES:(j + 1) * LANES] = dk
            band[:, KVW + j * LANES:KVW + (j + 1) * LANES] = jnp.where(lo2, dv_folds[0], dv_folds[1])
        dqkv_ref[:, QW:] = (band[CHUNK:, :] + carry[...]).astype(BF16)
        carry[...] = band[0:CHUNK, :]

        @pl.when(i == nb - 1)
        def _():
            g = gacc[...]
            g = g + pltpu.roll(g, HEAD_DIM, 1)
            dqg_ref[...] = g[0:1, :]
            dkg_ref[...] = g[1:2, :]
            for r in range(KV_GROUP):
                dsink_ref[r:r + 1, :] = jnp.sum(dsacc[r * CHUNK:(r + 1) * CHUNK, :], axis=0, keepdims=True)

    vec = pl.BlockSpec((1, LANES), lambda i: (0, 0))
    bshape = probs.shape[1:]
    return _call(
        body, name="attn_bwd", grid=(nb,),
        in_specs=[pl.BlockSpec((CHUNK, QW + 2 * KVW), lambda i: (nb - 1 - i, 0)),
                  pl.BlockSpec((CHUNK, 2 * KVW), lambda i: (jnp.maximum(nb - 2 - i, 0), QW // (2 * KVW))),
                  pl.BlockSpec((CHUNK, QW), lambda i: (nb - 1 - i, 0)),
                  pl.BlockSpec((1,) + bshape, lambda i: (nb - 1 - i, 0, 0, 0)),
                  pl.BlockSpec((CHUNK, LANES), lambda i: (nb - 1 - i, 0)),
                  _resident((1, LANES)), _resident((1, LANES))],
        out_specs=[pl.BlockSpec((CHUNK, QW + 2 * KVW), lambda i: (nb - 1 - i, 0)),
                   pl.BlockSpec(bshape, lambda i: (0, 0, 0)), vec, vec,
                   pl.BlockSpec((KV_GROUP, LANES), lambda i: (0, 0))],
        out_shape=[jax.ShapeDtypeStruct((S, QW + 2 * KVW), BF16),
                   jax.ShapeDtypeStruct(bshape, F32),
                   jax.ShapeDtypeStruct((1, LANES), F32), jax.ShapeDtypeStruct((1, LANES), F32),
                   jax.ShapeDtypeStruct((KV_GROUP, LANES), F32)],
        scratch_shapes=[pltpu.VMEM((CHUNK, 2 * KVW), F32), pltpu.VMEM((2 * CHUNK, 2 * KVW), F32),
                        pltpu.VMEM((KV_GROUP * CHUNK, LANES), F32), pltpu.VMEM((8, LANES), F32)],
        sem=("arbitrary",), args=(qkv, qkv, do, probs, psinks, qg2, kg2), comm=comm)


def _row_tile(rows, cols, n_arrays):
    budget = VMEM_LIMIT_V7X // 4 // (2 * n_arrays * 4 * cols)
    best = 8
    for n in range(1, rows // 8 + 1):
        if rows % n == 0 and (rows // n) % 8 == 0 and rows // n <= budget:
            best = rows // n
            break
    return best


def _adamw(g, w, m, v, name, comm=None):
    R, C = g.shape
    tr = _row_tile(R, C, 8)

    def body(g_ref, w_ref, m_ref, v_ref, d_ref, mo_ref, vo_ref, go_ref):
        gg = g_ref[...]
        go_ref[...] = gg
        mn = ADAM_B1 * m_ref[...] + (1.0 - ADAM_B1) * gg
        vn = ADAM_B2 * v_ref[...] + (1.0 - ADAM_B2) * jnp.square(gg)
        m_hat = mn / (1.0 - ADAM_B1 ** ADAM_STEP)
        v_hat = vn / (1.0 - ADAM_B2 ** ADAM_STEP)
        d_ref[...] = -ADAM_LR * (m_hat / (jnp.sqrt(v_hat) + ADAM_EPS) + ADAM_WD * w_ref[...])
        mo_ref[...] = mn
        vo_ref[...] = vn

    spec = pl.BlockSpec((tr, C), lambda i: (i, 0))
    return _call(
        body, name=name, grid=(R // tr,), in_specs=[spec] * 4, out_specs=[spec] * 4,
        out_shape=[jax.ShapeDtypeStruct((R, C), F32)] * 4, sem=("parallel",), args=(g, w, m, v), comm=comm)


def _place_shard(shards, layer, where, dtype, name):
    _, R, C = shards.shape
    tr = _row_tile(R, C, 2) if R % 8 == 0 else R

    def body(s_ref, x_ref, o_ref):
        o_ref[...] = x_ref[...].astype(dtype)

    return pl.pallas_call(
        body, name=name,
        grid_spec=pltpu.PrefetchScalarGridSpec(
            num_scalar_prefetch=1, grid=(R // tr,),
            in_specs=[pl.BlockSpec((1, tr, C), lambda i, s_ref: (layer, i, 0))],
            out_specs=pl.BlockSpec((1, tr, C), lambda i, s_ref: (s_ref[1], i, 0))),
        out_shape=jax.ShapeDtypeStruct((4, R, C), dtype),
        compiler_params=_cp(("parallel",)),
    )(where, shards)


def _pair_add(g4, rsib, where, name):
    J, R, C = g4.shape
    Rh = R // 2
    tr = _row_tile(Rh, C, 4)
    g5 = g4.reshape(J, 2, Rh, C)

    def body(s_ref, g_ref, r_ref, p_ref, q_ref):
        val = (g_ref[...].astype(F32)[0] + r_ref[...].astype(F32)).astype(BF16)
        p_ref[...] = val

        @pl.when(pl.program_id(1) == s_ref[1])
        def _():
            q_ref[...] = val

    return pl.pallas_call(
        body, name=name,
        grid_spec=pltpu.PrefetchScalarGridSpec(
            num_scalar_prefetch=1, grid=(Rh // tr, J),
            in_specs=[pl.BlockSpec((1, 1, tr, C), lambda i, j, s_ref: (j, s_ref[0], i, 0)),
                      pl.BlockSpec((1, tr, C), lambda i, j, s_ref: (j, i, 0))],
            out_specs=[pl.BlockSpec((1, tr, C), lambda i, j, s_ref: (j, i, 0)),
                       pl.BlockSpec((1, tr, C), lambda i, j, s_ref: (s_ref[1], i, 0))]),
        out_shape=[jax.ShapeDtypeStruct((J, Rh, C), BF16)] * 2,
        compiler_params=_cp(("parallel", "arbitrary")),
    )(where, g5, rsib)


def _sum_chips(q, where, dest, layer, out_shape, name):
    J, Rh, C = q.shape
    tr = _row_tile(Rh, C, 3)
    nb = Rh // tr

    def body(s_ref, q_ref, *rest):
        qq = q_ref[...].astype(F32)
        rest[-1][0] = ((qq[0] + qq[1]) + qq[2]) + qq[3]

    have = dest is not None
    return pl.pallas_call(
        body, name=name,
        grid_spec=pltpu.PrefetchScalarGridSpec(
            num_scalar_prefetch=1, grid=(nb,),
            in_specs=[pl.BlockSpec((J, tr, C), lambda i, s_ref: (0, i, 0))] + ([ANY] if have else []),
            out_specs=pl.BlockSpec((1, tr, C), lambda i, s_ref: (layer, s_ref[0] * nb + i, 0))),
        out_shape=jax.ShapeDtypeStruct(out_shape, F32),
        input_output_aliases={2: 0} if have else {},
        compiler_params=_cp(("parallel",)),
    )(*((where, q, dest) if have else (where, q)))


MESH = pl.DeviceIdType.MESH
ANY = pl.BlockSpec(memory_space=pl.ANY)


def _place():
    x, y, c = lax.axis_index("x"), lax.axis_index("y"), lax.axis_index("c")
    others = [(1 - x, y), (x, 1 - y), (1 - x, 1 - y)]
    return x, y, c, 2 * x + y, others, [2 * ox + oy for ox, oy in others]


def _gather_comm(items):
    n = len(items)
    placed = [it[0] for it in items]
    split = [it[3] for it in items]

    def rows(t, ref, half):
        _, lo, hi, _ = items[t]
        if not split[t]:
            return ref if (lo, hi) == (0, placed[t].shape[1]) else ref.at[pl.ds(lo, hi - lo), :]
        rh = (hi - lo) // 2
        return ref.at[pl.ds(lo + half * rh, rh), :]

    def sends(outs, sems):
        send, recv = sems[0], sems[1]
        x, y, c, me, others, okey = _place()
        cps = []
        for t in range(n):
            mine = rows(t, outs[t].at[me], c)
            for j, (ox, oy) in enumerate(others):
                cps.append(pltpu.make_async_remote_copy(
                    src_ref=mine, dst_ref=mine,
                    send_sem=send.at[t, j], recv_sem=recv.at[t, j], device_id=(ox, oy, c), device_id_type=MESH))
        return cps

    def start(srcs, outs, news, sems):
        for cp in sends(outs, sems):
            cp.start()

    def forwards(outs, sems):
        fsend, frecv = sems[2], sems[3]
        x, y, c, me, others, okey = _place()
        cps = []
        for t in range(n):
            if split[t]:
                for j in range(3):
                    landed = rows(t, outs[t].at[okey[j]], c)
                    cps.append(pltpu.make_async_remote_copy(
                        src_ref=landed, dst_ref=landed, send_sem=fsend.at[t, j], recv_sem=frecv.at[t, j],
                        device_id=(x, y, 1 - c), device_id_type=MESH))
        return cps

    def middle(srcs, outs, news, sems):
        send, recv = sems[0], sems[1]
        x, y, c, me, others, okey = _place()
        for t in range(n):
            for j in range(3):
                landed = rows(t, outs[t].at[okey[j]], c)
                pltpu.make_async_remote_copy(
                    src_ref=landed, dst_ref=landed, send_sem=send.at[t, j], recv_sem=recv.at[t, j],
                    device_id=(x, y, 1 - c), device_id_type=MESH).wait_recv()
        for cp in forwards(outs, sems):
            cp.start()

    def finish(srcs, outs, news, sems):
        fsend, frecv = sems[2], sems[3]
        x, y, c, me, others, okey = _place()
        for t in range(n):
            if split[t]:
                for j in range(3):
                    theirs = rows(t, outs[t].at[okey[j]], 1 - c)
                    pltpu.make_async_remote_copy(
                        src_ref=theirs, dst_ref=theirs, send_sem=fsend.at[t, j], recv_sem=frecv.at[t, j],
                        device_id=(x, y, 1 - c), device_id_type=MESH).wait_recv()
        for cp in sends(outs, sems) + forwards(outs, sems):
            cp.wait_send()

    return _Comm([], placed, [], [pltpu.SemaphoreType.DMA((n, 3))] * 4, start, finish, middle)


def _pair_exchange_comm(gs):
    n = len(gs)

    def copies(ins, outs, sems):
        send, recv = sems
        x, y, c, _, _, _ = _place()
        cps = []
        for t in range(n):
            rh = gs[t].shape[1] // 2
            cps.append(pltpu.make_async_remote_copy(
                src_ref=ins[t].at[:, pl.ds((1 - c) * rh, rh), :], dst_ref=outs[t],
                send_sem=send.at[t], recv_sem=recv.at[t], device_id=(x, y, 1 - c), device_id_type=MESH))
        return cps

    def start(ins, zones, outs, sems):
        for cp in copies(ins, outs, sems):
            cp.start()

    def finish(ins, zones, outs, sems):
        for cp in copies(ins, outs, sems):
            cp.wait()

    news = [jax.ShapeDtypeStruct((4, g.shape[1] // 2, g.shape[2]), g.dtype) for g in gs]
    return _Comm(gs, [], news, [pltpu.SemaphoreType.DMA((n,))] * 2, start, finish)


def _chip_scatter_comm(ps, qs):
    n = len(ps)

    def sends(ins, outs, sems):
        send, recv = sems
        x, y, c, me, others, okey = _place()
        return [pltpu.make_async_remote_copy(
            src_ref=ins[t].at[okey[j]], dst_ref=outs[t].at[me],
            send_sem=send.at[t, j], recv_sem=recv.at[t, j], device_id=(ox, oy, c), device_id_type=MESH)
            for t in range(n) for j, (ox, oy) in enumerate(others)]

    def start(ins, outs, news, sems):
        for cp in sends(ins, outs, sems):
            cp.start()

    def finish(ins, outs, news, sems):
        send, recv = sems
        x, y, c, me, others, okey = _place()
        for t in range(n):
            for j in range(3):
                slot = outs[t].at[okey[j]]
                pltpu.make_async_remote_copy(
                    src_ref=slot, dst_ref=slot, send_sem=send.at[t, j], recv_sem=recv.at[t, j],
                    device_id=(x, y, c), device_id_type=MESH).wait_recv()
        for cp in sends(ins, outs, sems):
            cp.wait_send()

    return _Comm(ps, qs, [], [pltpu.SemaphoreType.DMA((n, 3))] * 2, start, finish)


def _half_exchange_comm(arrs, layers=None):
    n = len(arrs)
    items = [(t, layer) for t in range(n) for layer in (range(arrs[t].shape[0]) if layers is None else layers[t])]

    def sends(outs, sems):
        send, recv = sems
        x, y, c, _, _, _ = _place()
        cps = []
        for k, (t, layer) in enumerate(items):
            rh = arrs[t].shape[1] // 2
            mine = outs[t].at[layer, pl.ds(c * rh, rh), :]
            cps.append(pltpu.make_async_remote_copy(
                src_ref=mine, dst_ref=mine, send_sem=send.at[k], recv_sem=recv.at[k],
                device_id=(x, y, 1 - c), device_id_type=MESH))
        return cps

    def start(srcs, outs, news, sems):
        for cp in sends(outs, sems):
            cp.start()

    def finish(srcs, outs, news, sems):
        send, recv = sems
        x, y, c, _, _, _ = _place()
        for k, (t, layer) in enumerate(items):
            rh = arrs[t].shape[1] // 2
            theirs = outs[t].at[layer, pl.ds((1 - c) * rh, rh), :]
            pltpu.make_async_remote_copy(
                src_ref=theirs, dst_ref=theirs, send_sem=send.at[k], recv_sem=recv.at[k],
                device_id=(x, y, 1 - c), device_id_type=MESH).wait_recv()
        for cp in sends(outs, sems):
            cp.wait_send()

    return _Comm([], arrs, [], [pltpu.SemaphoreType.DMA((len(items),))] * 2, start, finish)


SMALL_COLS = 1024
SMALL_PIECE_ROWS = 48


def _allreduce_small(buf, comm=None):
    pr = SMALL_PIECE_ROWS
    flips = [(d >> 2 & 1, d >> 1 & 1, d & 1) for d in range(1, 8)]

    def body(x_ref, o_ref, rbuf, send1, recv1, send2, recv2):
        x, y, c = lax.axis_index("x"), lax.axis_index("y"), lax.axis_index("c")
        me = 4 * x + 2 * y + c
        peers = [(x ^ fx, y ^ fy, c ^ fc) for fx, fy, fc in flips]
        pid = [4 * px + 2 * py + pc for px, py, pc in peers]

        def piece(ref, p):
            return ref.at[pl.ds(pl.multiple_of(p * pr, 8), pr), :]

        cps = []
        for d in range(7):
            cp = pltpu.make_async_remote_copy(
                src_ref=piece(x_ref, pid[d]), dst_ref=rbuf.at[d + 1],
                send_sem=send1.at[d], recv_sem=recv1.at[d], device_id=peers[d], device_id_type=MESH)
            cp.start()
            cps.append(cp)
        acc = piece(x_ref, me)[...]
        for d in range(7):
            cps[d].wait_recv()
            acc = acc + rbuf[d + 1]
        piece(o_ref, me)[...] = acc
        out = []
        for d in range(7):
            cp = pltpu.make_async_remote_copy(
                src_ref=piece(o_ref, me), dst_ref=piece(o_ref, me),
                send_sem=send2.at[d], recv_sem=recv2.at[d], device_id=peers[d], device_id_type=MESH)
            cp.start()
            out.append(cp)
        for d in range(7):
            pltpu.make_async_remote_copy(
                src_ref=piece(o_ref, pid[d]), dst_ref=piece(o_ref, pid[d]),
                send_sem=send2.at[d], recv_sem=recv2.at[d], device_id=peers[d], device_id_type=MESH).wait_recv()
        for cp in cps + out:
            cp.wait_send()

    vm = pl.BlockSpec(memory_space=pltpu.VMEM)
    return _call(
        body, name="small_allreduce", grid=(), in_specs=[vm], out_specs=vm,
        out_shape=jax.ShapeDtypeStruct(buf.shape, F32),
        scratch_shapes=[pltpu.VMEM((8, pr, SMALL_COLS), F32)] + [pltpu.SemaphoreType.DMA((7,))] * 4,
        args=(buf,), comm=comm)


def _rows_of(shape):
    return -(-math.prod(shape) // (8 * SMALL_COLS)) * 8


def _pack(arrays, rows):
    parts = []
    for a in arrays:
        r = _rows_of(a.shape)
        parts.append(jnp.pad(a.reshape(-1), (0, r * SMALL_COLS - a.size)).reshape(r, SMALL_COLS))
    used = sum(p.shape[0] for p in parts)
    if rows > used:
        parts.append(jnp.zeros((rows - used, SMALL_COLS), F32))
    return jnp.concatenate(parts, axis=0)


def _unpack(buf, shapes):
    out, off = [], 0
    for s in shapes:
        r = _rows_of(s)
        out.append(buf[off:off + r].reshape(-1)[:math.prod(s)].reshape(s))
        off += r
    return out


BIG = ["sgu_w_in", "sgu_w_out", "attn_w_qkv", "attn_w_o", "ffn_w_up", "ffn_w_down"]
SMALL = ["mix_norm", "ffn_norm", "sgu_v_gain", "sgu_w_s", "sgu_b_s", "attn_q_gain", "attn_k_gain", "attn_sinks",
         "rel_bias", "ffn_conv_b"]
ORDER = ["mix_norm", "ffn_norm", "sgu_w_in", "sgu_v_gain", "sgu_w_s", "sgu_b_s", "sgu_w_out", "attn_w_qkv",
         "attn_q_gain", "attn_k_gain", "attn_sinks", "attn_w_o", "rel_bias", "ffn_w_up", "ffn_conv_w", "ffn_conv_b",
         "ffn_w_down"]


def kernel(x, mix_norm, ffn_norm, sgu_w_in, sgu_v_gain, sgu_w_s, sgu_b_s, sgu_w_out, attn_w_qkv, attn_q_gain, attn_k_gain, attn_sinks, attn_w_o, rel_bias, ffn_w_up, ffn_conv_w, ffn_conv_b, ffn_w_down, loss_target, m_mix_norm, m_ffn_norm, m_sgu_w_in, m_sgu_v_gain, m_sgu_w_s, m_sgu_b_s, m_sgu_w_out, m_attn_w_qkv, m_attn_q_gain, m_attn_k_gain, m_attn_sinks, m_attn_w_o, m_rel_bias, m_ffn_w_up, m_ffn_conv_w, m_ffn_conv_b, m_ffn_w_down, v_mix_norm, v_ffn_norm, v_sgu_w_in, v_sgu_v_gain, v_sgu_w_s, v_sgu_b_s, v_sgu_w_out, v_attn_w_qkv, v_attn_q_gain, v_attn_k_gain, v_attn_sinks, v_attn_w_o, v_rel_bias, v_ffn_w_up, v_ffn_conv_w, v_ffn_conv_b, v_ffn_w_down):
    w = dict(mix_norm=mix_norm, ffn_norm=ffn_norm, sgu_w_in=sgu_w_in, sgu_v_gain=sgu_v_gain, sgu_w_s=sgu_w_s,
             sgu_b_s=sgu_b_s, sgu_w_out=sgu_w_out, attn_w_qkv=attn_w_qkv, attn_q_gain=attn_q_gain,
             attn_k_gain=attn_k_gain, attn_sinks=attn_sinks, attn_w_o=attn_w_o, rel_bias=rel_bias, ffn_w_up=ffn_w_up,
             ffn_conv_w=ffn_conv_w, ffn_conv_b=ffn_conv_b, ffn_w_down=ffn_w_down)
    mom = dict(mix_norm=m_mix_norm, ffn_norm=m_ffn_norm, sgu_w_in=m_sgu_w_in, sgu_v_gain=m_sgu_v_gain,
               sgu_w_s=m_sgu_w_s, sgu_b_s=m_sgu_b_s, sgu_w_out=m_sgu_w_out, attn_w_qkv=m_attn_w_qkv,
               attn_q_gain=m_attn_q_gain, attn_k_gain=m_attn_k_gain, attn_sinks=m_attn_sinks, attn_w_o=m_attn_w_o,
               rel_bias=m_rel_bias, ffn_w_up=m_ffn_w_up, ffn_conv_w=m_ffn_conv_w, ffn_conv_b=m_ffn_conv_b,
               ffn_w_down=m_ffn_w_down)
    var = dict(mix_norm=v_mix_norm, ffn_norm=v_ffn_norm, sgu_w_in=v_sgu_w_in, sgu_v_gain=v_sgu_v_gain,
               sgu_w_s=v_sgu_w_s, sgu_b_s=v_sgu_b_s, sgu_w_out=v_sgu_w_out, attn_w_qkv=v_attn_w_qkv,
               attn_q_gain=v_attn_q_gain, attn_k_gain=v_attn_k_gain, attn_sinks=v_attn_sinks, attn_w_o=v_attn_w_o,
               rel_bias=v_rel_bias, ffn_w_up=v_ffn_w_up, ffn_conv_w=v_ffn_conv_w, ffn_conv_b=v_ffn_conv_b,
               ffn_w_down=v_ffn_w_down)
    chip = 2 * lax.axis_index("x") + lax.axis_index("y")
    core = lax.axis_index("c")

    where = jnp.stack([core, chip]).astype(jnp.int32)
    names = ["sgu_w_in", "sgu_w_out", "attn_w_qkv", "attn_w_o", "ffn_w_up0", "ffn_w_up1", "ffn_w_down0", "ffn_w_down1"]
    shards = [(sgu_w_in, 0), (sgu_w_out, 0), (attn_w_qkv, 0), (attn_w_o, 0), (ffn_w_up, 0), (ffn_w_up, 1),
              (ffn_w_down, 0), (ffn_w_down, 1)]
    T = {nm: _place_shard(s, l, where, BF16, "place_" + nm) for (s, l), nm in zip(shards, names)}
    for l in range(2):
        T["conv_w%d" % l] = _place_shard(ffn_conv_w, l, where, F32, "place_conv_w%d" % l)

    def gather(keys):
        items = []
        for k in keys:
            k, lo, hi = (k, 0, None) if isinstance(k, str) else k
            items.append((T[k], lo, T[k].shape[1] if hi is None else hi, not k.startswith("conv")))
        return _gather_comm(items)

    def gathered(keys, res):
        for k, a in zip(keys, res[0]):
            T[k if isinstance(k, str) else k[0]] = a

    D = x.shape[2]
    first = ["sgu_w_in", "conv_w0", "conv_w1"]
    gathered(first, _run_comm(gather(first), "gather_first"))
    unshard_cols = lambda a: jnp.transpose(a, (1, 0, 2)).reshape(a.shape[1], -1)
    cw = [unshard_cols(T["conv_w0"]), unshard_cols(T["conv_w1"])]
    cb = ffn_conv_b
    flat = lambda k: T[k].reshape(-1, D)
    x2, target = x[0], loss_target[0]
    bucket = jnp.asarray(_rel_buckets_flat())
    wtril, wtrilT = _sgu_prep(sgu_w_s[0])
    bT = sgu_b_s[0].T
    bias = _rel_bias_expand(rel_bias, bucket).reshape(N_KV_HEADS, KV_GROUP * CHUNK, 2 * CHUNK)
    qg2 = jnp.tile(attn_q_gain, (1, 2))
    kg2 = jnp.tile(attn_k_gain, (1, 2))
    sinks = attn_sinks.reshape(N_HEADS)
    mix0, mix1 = mix_norm[0:1], mix_norm[1:2]
    fn0, fn1 = ffn_norm[0:1], ffn_norm[1:2]

    ks = ["sgu_w_out", ("ffn_w_up0", 0, 384)]
    (hn0, z, cdf), r = _norm_matmul(x2, mix0, T["sgu_w_in"], "sgu_in", comm=gather(ks), with_cdf=True)
    gathered(ks, r)
    ks = [("ffn_w_up0", 384, None)]
    (yp, h1), r = _sgu_fwd(z, cdf, x2, sgu_v_gain, wtril, bT, flat("sgu_w_out"), comm=gather(ks))
    gathered(ks, r)
    ks = ["ffn_w_down0", "attn_w_qkv", "attn_w_o"]
    (hf0, a0, cp0), r = _norm_matmul(h1, fn0, T["ffn_w_up0"], "ffn0_up", comm=gather(ks), conv=(cw[0], cb[0:1]))
    gathered(ks, r)
    ks = [("ffn_w_up1", 0, 384)]
    (f0, h2), r = _ffn_fwd(cp0, h1, flat("ffn_w_down0"), "ffn0_fwd", comm=gather(ks))
    gathered(ks, r)
    (hn1, qkv), _ = _norm_matmul(h2, mix1, T["attn_w_qkv"], "attn_qkv")
    ks = [("ffn_w_up1", 384, None)]
    (o, probs, psinks), r = _attn_fwd(qkv, qg2, kg2, sinks, bias, comm=gather(ks))
    gathered(ks, r)
    h3, _ = _matmul_res(o, flat("attn_w_o"), h2, "attn_out")
    ks = ["ffn_w_down1"]
    (hf1, a1, cp1), r = _norm_matmul(h3, fn1, T["ffn_w_up1"], "ffn1_up", comm=gather(ks), conv=(cw[1], cb[1:2]))
    gathered(ks, r)
    (f1, dh4, loss_local), _ = _ffn_fwd(cp1, h3, flat("ffn_w_down1"), "ffn1_fwd", target=target)

    G, RS, PQ, QD, halves, grads = {}, {}, {}, {}, {}, {}
    dest_of = {"sgu_w_in": ("sgu_w_in", 0), "sgu_w_out": ("sgu_w_out", 0), "attn_w_qkv": ("attn_w_qkv", 0),
               "attn_w_o": ("attn_w_o", 0), "ffn_w_up0": ("ffn_w_up", 0), "ffn_w_up1": ("ffn_w_up", 1),
               "ffn_w_down0": ("ffn_w_down", 0), "ffn_w_down1": ("ffn_w_down", 1)}

    def px(keys):
        return _pair_exchange_comm([G[k] for k in keys])

    def px_done(keys, res):
        for k, a in zip(keys, res[1]):
            RS[k] = a
            PQ[k] = _pair_add(G[k], a, where, "pair_add_" + k)

    def sc(keys):
        return _chip_scatter_comm([PQ[k][0] for k in keys], [PQ[k][1] for k in keys])

    def sc_done(keys, res):
        for k, a in zip(keys, res[0]):
            wk, layer = dest_of[k]
            halves[wk] = _sum_chips(a, where, halves.get(wk), layer, w[wk].shape, "sum_chips_" + k)

    nup = ffn_w_up.shape[2]
    ndown = ffn_w_down.shape[1]
    rows4 = lambda a: a.reshape(4, ndown, D)
    (dc1, dcb1), _ = _ffn_bwd_dc(dh4, cp1, flat("ffn_w_down1"), "ffn1_bwd_dc")
    gw, _ = _matmul_tn(f1, dh4, "ffn1_dw_down", ka=2 * ndown, nb=D, out_dtype=BF16)
    G["ffn_w_down1"] = rows4(gw)
    (dh3, dfn1, da1, dcw1), _ = _bwd_norm(dc1, T["ffn_w_up1"], h3, fn1, dh4, "ffn1_bwd_in", conv_w=cw[1], conv_in=a1)
    G["ffn_w_up1"], _ = _matmul_tn(hf1, da1, "ffn1_dw_up", ka=D, nb=nup, out_dtype=BF16)
    ks = ["ffn_w_down1", "ffn_w_up1"]
    G["attn_w_o"], r = _matmul_tn(o, dh3, "attn_dw_o", ka=QW // 4, nb=D, out_dtype=BF16, comm=px(ks))
    px_done(ks, r)
    do, _ = _matmul_nt(dh3, flat("attn_w_o"), "attn_bwd_out")
    (dqkv, dbias, dqg, dkg, dsink), r = _attn_bwd(qkv, do, probs, psinks, qg2, kg2, comm=sc(ks))
    sc_done(ks, r)
    G["attn_w_qkv"], _ = _matmul_tn(hn1, dqkv, "attn_dw_qkv", ka=D, nb=dqkv.shape[1] // 4, out_dtype=BF16)
    ks = ["attn_w_o", "attn_w_qkv"]
    (dh2, dmix1), r = _bwd_norm(dqkv, T["attn_w_qkv"], h2, mix1, dh3, "attn_bwd_in", comm=px(ks))
    px_done(ks, r)
    (dc0, dcb0), r = _ffn_bwd_dc(dh2, cp0, flat("ffn_w_down0"), "ffn0_bwd_dc", comm=sc(ks))
    sc_done(ks, r)
    gw, _ = _matmul_tn(f0, dh2, "ffn0_dw_down", ka=2 * ndown, nb=D, out_dtype=BF16)
    G["ffn_w_down0"] = rows4(gw)
    (dh1, dfn0, da0, dcw0), r = _bwd_norm(dc0, T["ffn_w_up0"], h1, fn0, dh2, "ffn0_bwd_in", conv_w=cw[0], conv_in=a0,
                                          comm=px(["ffn_w_down0"]))
    px_done(["ffn_w_down0"], r)
    G["ffn_w_up0"], r = _matmul_tn(hf0, da0, "ffn0_dw_up", ka=D, nb=nup, out_dtype=BF16, comm=sc(["ffn_w_down0"]))
    sc_done(["ffn_w_down0"], r)
    ks = ["ffn_w_up0"]
    G["sgu_w_out"], r = _matmul_tn(yp, dh1, "sgu_dw_out", ka=yp.shape[1] // 4, nb=D, out_dtype=BF16, comm=px(ks))
    px_done(ks, r)
    both = [sc(ks), px(["sgu_w_out"])]
    (dz, dws, dbT, dvg), r = _sgu_bwd(dh1, z, cdf, sgu_v_gain, wtril, wtrilT, bT, flat("sgu_w_out"),
                                      comm=_join(both))
    r = _split(both, r)
    sc_done(ks, r[0])
    px_done(["sgu_w_out"], r[1])
    done = ["attn_w_qkv", "attn_w_o", "ffn_w_up", "ffn_w_down"]
    both = [sc(["sgu_w_out"]), _half_exchange_comm([halves[k] for k in done])]
    G["sgu_w_in"], r = _matmul_tn(hn0, dz, "sgu_dw_in", ka=D, nb=dz.shape[1] // 4, out_dtype=BF16, comm=_join(both))
    r = _split(both, r)
    sc_done(["sgu_w_out"], r[0])
    for k, a in zip(done, r[1][0]):
        grads[k] = a
    px_done(["sgu_w_in"], _run_comm(px(["sgu_w_in"]), "grad_last_pair_exchange"))
    (grad_x, dmix0), r = _bwd_norm(dz, T["sgu_w_in"], x2, mix0, dh1, "sgu_bwd_in", comm=sc(["sgu_w_in"]))
    sc_done(["sgu_w_in"], r)

    g = dict(mix_norm=jnp.concatenate([dmix0, dmix1], axis=0), ffn_norm=jnp.concatenate([dfn0, dfn1], axis=0),
             sgu_v_gain=dvg, sgu_w_s=dws, sgu_b_s=dbT.T, attn_q_gain=dqg[:, :HEAD_DIM], attn_k_gain=dkg[:, :HEAD_DIM],
             attn_sinks=dsink[:, :N_KV_HEADS].T.reshape(1, N_HEADS),
             rel_bias=_rel_bias_reduce(dbias.reshape(N_HEADS, CHUNK * 2 * CHUNK), bucket),
             ffn_conv_b=jnp.concatenate([dcb0, dcb1], axis=0))
    small_list = [g[k].reshape(w[k].shape) for k in SMALL] + [jnp.stack([dcw0, dcw1]), loss_local]
    small_shapes = [a.shape for a in small_list]
    done = ["sgu_w_in", "sgu_w_out"]
    red, r = _allreduce_small(_pack(small_list, 8 * SMALL_PIECE_ROWS),
                              comm=_half_exchange_comm([halves[k] for k in done]))
    for k, a in zip(done, r[0]):
        grads[k] = a
    red = _unpack(red, small_shapes)
    for k, a in zip(SMALL, red):
        grads[k] = a
    grads["ffn_conv_w"] = lax.dynamic_slice_in_dim(red[-2], chip * ffn_conv_w.shape[2], ffn_conv_w.shape[2], axis=2)
    loss = red[-1][0, 0]

    delta, new_m, new_v = {}, {}, {}
    two = lambda a: a.reshape(-1, a.shape[-1])
    for k in BIG:
        (d2, m2, v2, g2), _ = _adamw(two(grads[k]), two(w[k]), two(mom[k]), two(var[k]), "adamw_" + k)
        delta[k], new_m[k], new_v[k], grads[k] = (a.reshape(w[k].shape) for a in (d2, m2, v2, g2))
    sm = SMALL + ["ffn_conv_w"]
    sm_shapes = [w[k].shape for k in sm]
    rows = sum(_rows_of(s) for s in sm_shapes)
    (d2, m2, v2, _), _ = _adamw(_pack([grads[k] for k in sm], rows), _pack([w[k] for k in sm], rows),
                                _pack([mom[k] for k in sm], rows), _pack([var[k] for k in sm], rows), "adamw_small")
    for dst, buf in ((delta, d2), (new_m, m2), (new_v, v2)):
        for k, a in zip(sm, _unpack(buf, sm_shapes)):
            dst[k] = a

    return (loss, grad_x[None], *[grads[k] for k in ORDER], *[delta[k] for k in ORDER],
            *[new_m[k] for k in ORDER], *[new_v[k] for k in ORDER])
```

```python
import functools
import math

import numpy as np
import jax
import jax.numpy as jnp
from jax import lax
from jax.experimental import pallas as pl
from jax.experimental.pallas import tpu as pltpu

F32 = jnp.float32
BF16 = jnp.bfloat16

EPS = 1e-6
CHUNK = 128
SGU_GROUPS = 16
HEAD_DIM = 64
N_HEADS = 16
N_KV_HEADS = 4
KV_GROUP = N_HEADS // N_KV_HEADS
REL_BUCKETS = 32
REL_MAX_DIST = 128
LANES = 128
HALO = 16

ADAM_LR = 0.001
ADAM_B1 = 0.9
ADAM_B2 = 0.999
ADAM_EPS = 1e-08
ADAM_WD = 0.01
ADAM_STEP = 10

VMEM_LIMIT_V7X = 56 * 1024 * 1024

_SQRT_HALF = math.sqrt(0.5)
_INV_SQRT_2PI = 1.0 / math.sqrt(2.0 * math.pi)


def _cp(sem):
    return pltpu.CompilerParams(dimension_semantics=sem, vmem_limit_bytes=VMEM_LIMIT_V7X)


def _resident(shape):
    nd = len(shape)
    return pl.BlockSpec(shape, lambda *_: (0,) * nd, pipeline_mode=pl.Buffered(1))


class _Comm:
    def __init__(self, srcs, zones, news, sems, start, finish, middle=None):
        self.srcs, self.zones, self.news, self.sems = list(srcs), list(zones), list(news), list(sems)
        self.start, self.finish = start, finish
        self.middle = middle if middle is not None else (lambda srcs, zones, news, sems: None)


def _join(comms):
    comms = [c for c in comms if c is not None]
    if not comms:
        return None

    def part(seq, attr):
        out, k = [], 0
        for c in comms:
            n = len(getattr(c, attr))
            out.append(seq[k:k + n])
            k += n
        return out

    def run(which):
        def f(srcs, zones, news, sems):
            for c, a, b, d, e in zip(comms, part(srcs, "srcs"), part(zones, "zones"), part(news, "news"), part(sems, "sems")):
                getattr(c, which)(a, b, d, e)
        return f

    cat = lambda attr: [v for c in comms for v in getattr(c, attr)]
    return _Comm(cat("srcs"), cat("zones"), cat("news"), cat("sems"), run("start"), run("finish"), run("middle"))


def _split(comms, res):
    zones, news = res
    out, kz, kn = [], 0, 0
    for c in comms:
        out.append((zones[kz:kz + len(c.zones)], news[kn:kn + len(c.news)]))
        kz += len(c.zones)
        kn += len(c.news)
    return out


def _call(body, *, name, grid, in_specs, out_specs, out_shape, args, scratch_shapes=(), sem=None, comm=None):
    if comm is None:
        res = pl.pallas_call(body, name=name, grid=grid, in_specs=in_specs, out_specs=out_specs, out_shape=out_shape,
                             scratch_shapes=list(scratch_shapes), compiler_params=_cp(sem))(*args)
        return res, None
    single = not isinstance(out_shape, (list, tuple))
    out_specs_l = [out_specs] if single else list(out_specs)
    out_shape_l = [out_shape] if single else list(out_shape)
    n_in, n_out, n_scr = len(in_specs), len(out_shape_l), len(scratch_shapes)
    ns, nz, nn = len(comm.srcs), len(comm.zones), len(comm.news)

    def wrapped(*refs):
        k = n_in
        ins, srcs = refs[:k], refs[k:k + ns]
        k += ns + nz
        outs, zones, news = refs[k:k + n_out], refs[k + n_out:k + n_out + nz], refs[k + n_out + nz:k + n_out + nz + nn]
        k += n_out + nz + nn
        scr, sems = refs[k:k + n_scr], refs[k + n_scr:]
        if not grid:
            comm.start(srcs, zones, news, sems)
            body(*ins, *outs, *scr)
            comm.middle(srcs, zones, news, sems)
            comm.finish(srcs, zones, news, sems)
            return
        first = functools.reduce(jnp.logical_and, [pl.program_id(a) == 0 for a in range(len(grid))])
        last = functools.reduce(jnp.logical_and, [pl.program_id(a) == grid[a] - 1 for a in range(len(grid))])
        early = len(grid) == 1 and grid[0] >= 2
        mid_step = grid[0] - (2 if grid[0] >= 8 else 1)

        @pl.when(first)
        def _():
            comm.start(srcs, zones, news, sems)

        if early:
            @pl.when(pl.program_id(0) == mid_step)
            def _():
                comm.middle(srcs, zones, news, sems)

        body(*ins, *outs, *scr)

        @pl.when(last)
        def _():
            if not early:
                comm.middle(srcs, zones, news, sems)
            comm.finish(srcs, zones, news, sems)

    res = pl.pallas_call(
        wrapped, name=name, grid=grid,
        in_specs=list(in_specs) + [ANY] * (ns + nz), out_specs=out_specs_l + [ANY] * (nz + nn),
        out_shape=out_shape_l + [jax.ShapeDtypeStruct(z.shape, z.dtype) for z in comm.zones] + comm.news,
        input_output_aliases={n_in + ns + i: n_out + i for i in range(nz)},
        scratch_shapes=list(scratch_shapes) + comm.sems,
        compiler_params=_cp(("arbitrary",) * len(grid)),
    )(*args, *comm.srcs, *comm.zones)
    main = res[0] if single else list(res[:n_out])
    return main, (list(res[n_out:n_out + nz]), list(res[n_out + nz:]))


def _run_comm(comm, name):
    ns, nz, nn = len(comm.srcs), len(comm.zones), len(comm.news)

    def body(*refs):
        srcs, zones, news, sems = refs[:ns], refs[ns + nz:ns + 2 * nz], refs[ns + 2 * nz:ns + 2 * nz + nn], refs[ns + 2 * nz + nn:]
        comm.start(srcs, zones, news, sems)
        comm.middle(srcs, zones, news, sems)
        comm.finish(srcs, zones, news, sems)

    res = pl.pallas_call(
        body, name=name, in_specs=[ANY] * (ns + nz), out_specs=[ANY] * (nz + nn),
        out_shape=[jax.ShapeDtypeStruct(z.shape, z.dtype) for z in comm.zones] + comm.news,
        input_output_aliases={ns + i: i for i in range(nz)}, scratch_shapes=comm.sems,
    )(*comm.srcs, *comm.zones)
    return list(res[:nz]), list(res[nz:])


def _dot(a, b):
    return jnp.dot(a, b, preferred_element_type=F32)


def _dot_nt(a, b):
    return lax.dot_general(a, b, (((1,), (1,)), ((), ())), preferred_element_type=F32)


def _dot_tn(a, b):
    return lax.dot_general(a, b, (((0,), (0,)), ((), ())), preferred_element_type=F32)


def _normal_cdf(x):
    return 0.5 * (1.0 + lax.erf(x * _SQRT_HALF))


def _gelu_and_grad(x, cdf):
    return x * cdf, cdf + x * jnp.exp(-0.5 * x * x) * _INV_SQRT_2PI


def _sigmoid(x):
    return 0.5 * jnp.tanh(0.5 * x) + 0.5


def _rms_bwd(dy, x, gain):
    r = lax.rsqrt(jnp.mean(x * x, axis=-1, keepdims=True) + EPS)
    xhat = x * r
    gdy = dy * gain
    dx = r * (gdy - xhat * jnp.mean(gdy * xhat, axis=-1, keepdims=True))
    return dx, jnp.sum(dy * xhat, axis=0, keepdims=True)


def _norm_matmul(x, gain, w4, name, comm=None, with_cdf=False, conv=None, tm=1024):
    S, D = x.shape
    nsh, _, ns = w4.shape
    extra = with_cdf or conv is not None
    tm = min(512 if extra else tm, S)

    def body(x_ref, g_ref, w_ref, *rest):
        if conv is not None:
            cw_ref, cb_ref, hn_ref, o_ref, e_ref, tail = rest
        else:
            hn_ref, o_ref = rest[:2]
            e_ref = rest[2] if with_cdf else None
        xf = x_ref[...]
        r = lax.rsqrt(jnp.mean(xf * xf, axis=-1, keepdims=True) + EPS)
        hn = (xf * r * g_ref[...]).astype(BF16)
        hn_ref[...] = hn
        if conv is not None:
            @pl.when(pl.program_id(0) == 0)
            def _():
                tail[...] = jnp.zeros_like(tail)

        for j in range(nsh):
            cols = slice(j * ns, (j + 1) * ns)
            out = _dot(hn, w_ref[j])
            o_ref[:, cols] = out.astype(BF16)
            if with_cdf:
                e_ref[:, cols] = _normal_cdf(out).astype(BF16)
            if conv is not None:
                w0, w1, w2, b = cw_ref[0:1, cols], cw_ref[1:2, cols], cw_ref[2:3, cols], cb_ref[:, cols]
                taps = lambda v: w0 * pltpu.roll(v, 2, 0) + w1 * pltpu.roll(v, 1, 0) + w2 * v + b
                e_ref[:, cols] = taps(out).astype(BF16)
                head = jnp.concatenate([tail[:, cols], out[0:HALO]], axis=0)
                e_ref[0:HALO, cols] = taps(head)[HALO:].astype(BF16)
                tail[:, cols] = out[tm - HALO:tm]

    row = lambda width: pl.BlockSpec((tm, width), lambda i: (i, 0))
    N = nsh * ns
    in_specs = [row(D), _resident((1, D)), _resident(w4.shape)]
    args = (x, gain, w4)
    scratch = []
    if conv is not None:
        in_specs += [_resident((3, N)), _resident((1, N))]
        args += tuple(conv)
        scratch = [pltpu.VMEM((HALO, N), F32)]
    return _call(
        body, name=name, grid=(S // tm,), in_specs=in_specs,
        out_specs=[row(D), row(N)] + [row(N)] * extra,
        out_shape=[jax.ShapeDtypeStruct((S, D), BF16)] + [jax.ShapeDtypeStruct((S, N), BF16)] * (1 + extra),
        scratch_shapes=scratch, sem=("arbitrary",) if conv is not None else ("parallel",), args=args, comm=comm)


def _matmul_res(a, w, res, name, comm=None, tm=1024):
    S, K = a.shape
    N = w.shape[1]
    tm = min(tm, S)

    def body(a_ref, w_ref, r_ref, o_ref):
        o_ref[...] = r_ref[...] + _dot(a_ref[...], w_ref[...])

    return _call(
        body, name=name, grid=(S // tm,),
        in_specs=[pl.BlockSpec((tm, K), lambda i: (i, 0)), _resident(w.shape), pl.BlockSpec((tm, N), lambda i: (i, 0))],
        out_specs=pl.BlockSpec((tm, N), lambda i: (i, 0)),
        out_shape=jax.ShapeDtypeStruct((S, N), F32),
        sem=("parallel",), args=(a, w, res), comm=comm)


def _matmul_nt(dh, w, name, comm=None, tm=1024):
    S, N = dh.shape
    K = w.shape[0]
    tm = min(tm, S)

    def body(d_ref, w_ref, o_ref):
        o_ref[...] = _dot_nt(d_ref[...].astype(BF16), w_ref[...]).astype(BF16)

    return _call(
        body, name=name, grid=(S // tm,),
        in_specs=[pl.BlockSpec((tm, N), lambda i: (i, 0)), _resident(w.shape)],
        out_specs=pl.BlockSpec((tm, K), lambda i: (i, 0)),
        out_shape=jax.ShapeDtypeStruct((S, K), BF16),
        sem=("parallel",), args=(dh, w), comm=comm)


def _matmul_tn(a, b, name, *, ka, nb, out_dtype, comm=None):
    S, KA = a.shape
    NB = b.shape[1]
    fixed = ka * nb * (4 + 2 * jnp.dtype(out_dtype).itemsize)
    per_row = 2 * (ka * a.dtype.itemsize + nb * b.dtype.itemsize)
    ts = S
    while ts > 512 and fixed + ts * per_row > VMEM_LIMIT_V7X * 3 // 4:
        ts //= 2
    J = max(KA // ka, NB // nb)
    a_map = (lambda j, s: (s, j)) if KA // ka > 1 else (lambda j, s: (s, 0))
    b_map = (lambda j, s: (s, j)) if NB // nb > 1 else (lambda j, s: (s, 0))
    last = S // ts - 1

    def body(a_ref, b_ref, o_ref, acc):
        s = pl.program_id(1)

        @pl.when(s == 0)
        def _():
            acc[...] = jnp.zeros_like(acc)

        acc[...] += _dot_tn(a_ref[...].astype(BF16), b_ref[...].astype(BF16))

        @pl.when(s == last)
        def _():
            o_ref[0] = acc[...].astype(out_dtype)

    return _call(
        body, name=name, grid=(J, S // ts),
        in_specs=[pl.BlockSpec((ts, ka), a_map), pl.BlockSpec((ts, nb), b_map)],
        out_specs=pl.BlockSpec((1, ka, nb), lambda j, s: (j, 0, 0)),
        out_shape=jax.ShapeDtypeStruct((J, ka, nb), out_dtype),
        scratch_shapes=[pltpu.VMEM((ka, nb), F32)],
        sem=("parallel", "arbitrary"), args=(a, b), comm=comm)


def _sgu_prep(w_s):
    G = w_s.shape[0]

    def body(w_ref, t_ref, tt_ref):
        tri = lax.broadcasted_iota(jnp.int32, (CHUNK, CHUNK), 0) >= lax.broadcasted_iota(jnp.int32, (CHUNK, CHUNK), 1)
        for g in range(G):
            t = jnp.where(tri, w_ref[g], 0.0)
            t_ref[g] = t.astype(BF16)
            tt_ref[g] = t.T.astype(BF16)

    return pl.pallas_call(
        body, name="sgu_prep",
        out_shape=[jax.ShapeDtypeStruct(w_s.shape, BF16), jax.ShapeDtypeStruct(w_s.shape, BF16)],
        compiler_params=_cp(None),
    )(w_s)


def _sgu_fwd(z, cdf, x, vg, wtril, bT, wout, comm=None, tm=512):
    S = z.shape[0]
    W = z.shape[1] // 2
    D = x.shape[1]
    tm = min(tm, S)

    def body(z_ref, cdf_ref, x_ref, vg_ref, wt_ref, bT_ref, wo_ref, yp_ref, h_ref):
        def chunk(c, carry):
            r0 = pl.multiple_of(c * CHUNK, CHUNK)
            zc = z_ref[pl.ds(r0, CHUNK), :].astype(F32)
            cdf = cdf_ref[pl.ds(r0, CHUNK), :].astype(F32)
            u = zc[:, :W] * cdf[:, :W]
            v = zc[:, W:] * cdf[:, W:]
            rv = lax.rsqrt(jnp.mean(v * v, axis=-1, keepdims=True) + EPS)
            vn = (v * rv * vg_ref[...]).astype(BF16)
            for g in range(SGU_GROUPS):
                sl = slice(g * LANES, (g + 1) * LANES)
                s = _dot(wt_ref[g], vn[:, sl]) + bT_ref[:, g:g + 1]
                yp_ref[pl.ds(r0, CHUNK), sl] = (u[:, sl] * s).astype(BF16)
            return carry

        lax.fori_loop(0, tm // CHUNK, chunk, 0)
        h_ref[...] = x_ref[...] + _dot(yp_ref[...], wo_ref[...])

    return _call(
        body, name="sgu_fwd", grid=(S // tm,),
        in_specs=[pl.BlockSpec((tm, 2 * W), lambda i: (i, 0)), pl.BlockSpec((tm, 2 * W), lambda i: (i, 0)),
                  pl.BlockSpec((tm, D), lambda i: (i, 0)),
                  _resident((1, W)), _resident(wtril.shape), _resident(bT.shape), _resident(wout.shape)],
        out_specs=[pl.BlockSpec((tm, W), lambda i: (i, 0)), pl.BlockSpec((tm, D), lambda i: (i, 0))],
        out_shape=[jax.ShapeDtypeStruct((S, W), BF16), jax.ShapeDtypeStruct((S, D), F32)],
        sem=("parallel",), args=(z, cdf, x, vg, wtril, bT, wout), comm=comm)


def _sgu_bwd(dh, z, cdf, vg, wtril, wtrilT, bT, wout, comm=None, tm=512):
    S = z.shape[0]
    W = z.shape[1] // 2
    D = dh.shape[1]
    G = SGU_GROUPS
    tm = min(tm, S)
    last = S // tm - 1

    def body(dh_ref, z_ref, cdf_ref, vg_ref, wt_ref, wtT_ref, bT_ref, wo_ref,
             dz_ref, dws_ref, dbT_ref, dvg_ref, dyp_s, du_s, dvn_s, dsacc):
        i = pl.program_id(0)

        @pl.when(i == 0)
        def _():
            dws_ref[...] = jnp.zeros_like(dws_ref)
            dvg_ref[...] = jnp.zeros_like(dvg_ref)
            dsacc[...] = jnp.zeros_like(dsacc)

        dyp_s[...] = _dot_nt(dh_ref[...].astype(BF16), wo_ref[...])
        tri = lax.broadcasted_iota(jnp.int32, (CHUNK, CHUNK), 0) >= lax.broadcasted_iota(jnp.int32, (CHUNK, CHUNK), 1)

        def chunk(c, carry):
            r0 = pl.multiple_of(c * CHUNK, CHUNK)
            zc = z_ref[pl.ds(r0, CHUNK), :].astype(F32)
            cdf = cdf_ref[pl.ds(r0, CHUNK), :].astype(F32)
            u, gu = _gelu_and_grad(zc[:, :W], cdf[:, :W])
            v, gv = _gelu_and_grad(zc[:, W:], cdf[:, W:])
            rv = lax.rsqrt(jnp.mean(v * v, axis=-1, keepdims=True) + EPS)
            vhat = v * rv
            vgain = vg_ref[...]
            vn = (vhat * vgain).astype(BF16)
            dyp = dyp_s[pl.ds(r0, CHUNK), :]
            for g in range(G):
                sl = slice(g * LANES, (g + 1) * LANES)
                vng = vn[:, sl]
                s = _dot(wt_ref[g], vng) + bT_ref[:, g:g + 1]
                ds = dyp[:, sl] * u[:, sl]
                du_s[:, sl] = dyp[:, sl] * s
                dsb = ds.astype(BF16)
                dvn_s[:, sl] = _dot(wtT_ref[g], dsb)
                dws_ref[g] += jnp.where(tri, _dot_nt(dsb, vng), 0.0)
                dsacc[g] += ds
            dvn = dvn_s[...]
            dvg_ref[...] += jnp.sum(dvn * vhat, axis=0, keepdims=True)
            gdy = dvn * vgain
            dv = rv * (gdy - vhat * jnp.mean(gdy * vhat, axis=-1, keepdims=True))
            dz_ref[pl.ds(r0, CHUNK), :W] = (du_s[...] * gu).astype(BF16)
            dz_ref[pl.ds(r0, CHUNK), W:] = (dv * gv).astype(BF16)
            return carry

        lax.fori_loop(0, tm // CHUNK, chunk, 0)

        @pl.when(i == last)
        def _():
            for g in range(G):
                dbT_ref[:, g:g + 1] = jnp.sum(dsacc[g], axis=1, keepdims=True)

    return _call(
        body, name="sgu_bwd", grid=(S // tm,),
        in_specs=[pl.BlockSpec((tm, D), lambda i: (i, 0)), pl.BlockSpec((tm, 2 * W), lambda i: (i, 0)),
                  pl.BlockSpec((tm, 2 * W), lambda i: (i, 0)),
                  _resident((1, W)), _resident(wtril.shape), _resident(wtrilT.shape), _resident(bT.shape),
                  _resident(wout.shape)],
        out_specs=[pl.BlockSpec((tm, 2 * W), lambda i: (i, 0)),
                   pl.BlockSpec((G, CHUNK, CHUNK), lambda i: (0, 0, 0)),
                   pl.BlockSpec((CHUNK, G), lambda i: (0, 0)),
                   pl.BlockSpec((1, W), lambda i: (0, 0))],
        out_shape=[jax.ShapeDtypeStruct((S, 2 * W), BF16), jax.ShapeDtypeStruct((G, CHUNK, CHUNK), F32),
                   jax.ShapeDtypeStruct((CHUNK, G), F32), jax.ShapeDtypeStruct((1, W), F32)],
        scratch_shapes=[pltpu.VMEM((tm, W), F32), pltpu.VMEM((CHUNK, W), F32), pltpu.VMEM((CHUNK, W), F32),
                        pltpu.VMEM((G, CHUNK, CHUNK), F32)],
        sem=("arbitrary",), args=(dh, z, cdf, vg, wtril, wtrilT, bT, wout), comm=comm)


def _ffn_fwd(cp, h_in, wdown, name, comm=None, target=None, tm=512, R=128):
    S, C = cp.shape
    F = C // 2
    D = h_in.shape[1]
    tm = min(tm, S)
    head = target is not None

    def body(cp_ref, h_ref, wd_ref, *rest):
        if head:
            t_ref, f_ref, dy_ref, l_ref = rest
        else:
            f_ref, ho_ref = rest
        for r0 in range(0, tm, R):
            g = cp_ref[r0:r0 + R, :F].astype(F32)
            f_ref[r0:r0 + R, :] = (g * _sigmoid(g) * cp_ref[r0:r0 + R, F:].astype(F32)).astype(BF16)
        h_out = h_ref[...] + _dot(f_ref[...], wd_ref[...])
        if not head:
            ho_ref[...] = h_out
            return

        @pl.when(pl.program_id(0) == 0)
        def _():
            l_ref[...] = jnp.zeros_like(l_ref)

        e = h_out - t_ref[...]
        dy_ref[...] = e * (1.0 / D)
        rows = jnp.sum(e * e, axis=-1, keepdims=True) * (1.0 / D)
        l_ref[...] += 0.5 * jnp.sum(rows, axis=0, keepdims=True)

    row = pl.BlockSpec((tm, D), lambda i: (i, 0))
    in_specs = [pl.BlockSpec((tm, C), lambda i: (i, 0)), row, _resident(wdown.shape)]
    out_specs = [pl.BlockSpec((tm, F), lambda i: (i, 0)), row]
    out_shape = [jax.ShapeDtypeStruct((S, F), BF16), jax.ShapeDtypeStruct((S, D), F32)]
    args = (cp, h_in, wdown)
    if head:
        in_specs.append(row)
        out_specs.append(pl.BlockSpec((1, 1), lambda i: (0, 0)))
        out_shape.append(jax.ShapeDtypeStruct((1, 1), F32))
        args += (target,)
    return _call(
        body, name=name, grid=(S // tm,), in_specs=in_specs, out_specs=out_specs, out_shape=out_shape,
        sem=("arbitrary",) if head else ("parallel",), args=args, comm=comm)


def _rows8(v):
    return functools.reduce(jnp.add, [v[8 * k:8 * k + 8] for k in range(v.shape[0] // 8)])


def _ffn_bwd_dc(dh, cp, wdown, name, comm=None, tm=512, R=64):
    S, C = cp.shape
    F = C // 2
    D = dh.shape[1]
    tm = min(tm, S)

    def body(dh_ref, cp_ref, wd_ref, dc_ref, dcb_ref, df_s, acc):
        i = pl.program_id(0)

        @pl.when(i == 0)
        def _():
            acc[...] = jnp.zeros_like(acc)

        df_s[...] = _dot_nt(dh_ref[...].astype(BF16), wd_ref[...])
        for r0 in range(0, tm, R):
            cpre = cp_ref[r0:r0 + R, :].astype(F32)
            g = cpre[:, :F]
            val = cpre[:, F:]
            sg = _sigmoid(g)
            df = df_s[r0:r0 + R, :]
            dg = df * val * (sg * (1.0 + g * (1.0 - sg)))
            dval = df * (g * sg)
            dc = jnp.concatenate([dg, dval], axis=1)
            dc_ref[r0:r0 + R, :] = dc.astype(BF16)
            acc[...] += _rows8(dc)

        @pl.when(i == S // tm - 1)
        def _():
            dcb_ref[...] = jnp.sum(acc[...], axis=0, keepdims=True)

    return _call(
        body, name=name, grid=(S // tm,),
        in_specs=[pl.BlockSpec((tm, D), lambda i: (i, 0)), pl.BlockSpec((tm, C), lambda i: (i, 0)),
                  _resident(wdown.shape)],
        out_specs=[pl.BlockSpec((tm, C), lambda i: (i, 0)), pl.BlockSpec((1, C), lambda i: (0, 0))],
        out_shape=[jax.ShapeDtypeStruct((S, C), BF16), jax.ShapeDtypeStruct((1, C), F32)],
        scratch_shapes=[pltpu.VMEM((tm, F), F32), pltpu.VMEM((8, C), F32)],
        sem=("arbitrary",), args=(dh, cp, wdown), comm=comm)


def _bwd_norm(dA, w4, h_in, gain, dh_out, name, conv_w=None, conv_in=None, comm=None, R=64):
    S, N = dA.shape
    nsh, D, ns = w4.shape
    conv = conv_w is not None
    tm = min(256 if conv else 512, S)
    nt = S // tm

    def finish(src_ref, w_ref, h_ref, g_ref, dho_ref, dhi_ref, dg_ref):
        dhn = _dot_nt(src_ref[:, 0:ns], w_ref[0])
        for j in range(1, nsh):
            dhn += _dot_nt(src_ref[:, j * ns:(j + 1) * ns], w_ref[j])
        dx, dgain = _rms_bwd(dhn, h_ref[...], g_ref[...])
        dg_ref[...] += dgain
        dhi_ref[...] = dho_ref[...] + dx

    def body_plain(dA_ref, w_ref, h_ref, g_ref, dho_ref, dhi_ref, dg_ref):
        @pl.when(pl.program_id(0) == 0)
        def _():
            dg_ref[...] = jnp.zeros_like(dg_ref)

        finish(dA_ref, w_ref, h_ref, g_ref, dho_ref, dhi_ref, dg_ref)

    def body_conv(dc_ref, halo_ref, cw_ref, a_ref, w_ref, h_ref, g_ref, dho_ref, dhi_ref, dg_ref, da_ref, dcw_ref,
                  dc32, acc):
        i = pl.program_id(0)

        @pl.when(i == 0)
        def _():
            dg_ref[...] = jnp.zeros_like(dg_ref)
            acc[...] = jnp.zeros_like(acc)

        dc32[0:tm, :] = dc_ref[...].astype(F32)
        dc32[tm:, :] = jnp.where(i < nt - 1, halo_ref[...].astype(F32), 0.0)

        for r0 in range(0, tm, R):
            X = dc32[r0:r0 + R + 8, :]
            d0 = X[:R]
            d1 = pltpu.roll(X, R + 7, 0)[:R]
            d2 = pltpu.roll(X, R + 6, 0)[:R]
            da = cw_ref[2:3, :] * d0 + cw_ref[1:2, :] * d1 + cw_ref[0:1, :] * d2
            da_ref[r0:r0 + R, :] = da.astype(BF16)
            a0 = a_ref[r0:r0 + R, :].astype(F32)
            acc[0] += _rows8(d2 * a0)
            acc[1] += _rows8(d1 * a0)
            acc[2] += _rows8(d0 * a0)
        finish(da_ref, w_ref, h_ref, g_ref, dho_ref, dhi_ref, dg_ref)

        @pl.when(i == nt - 1)
        def _():
            for k in range(3):
                dcw_ref[k:k + 1, :] = jnp.sum(acc[k], axis=0, keepdims=True)

    row = lambda width: pl.BlockSpec((tm, width), lambda i: (i, 0))
    common_in = [_resident(w4.shape), row(D), _resident((1, D)), row(D)]
    common_out = [row(D), pl.BlockSpec((1, D), lambda i: (0, 0))]
    common_shape = [jax.ShapeDtypeStruct((S, D), F32), jax.ShapeDtypeStruct((1, D), F32)]
    if not conv:
        return _call(
            body_plain, name=name, grid=(nt,),
            in_specs=[row(N)] + common_in, out_specs=common_out, out_shape=common_shape,
            sem=("arbitrary",), args=(dA, w4, h_in, gain, dh_out), comm=comm)
    hb = tm // HALO
    nhb = S // HALO
    return _call(
        body_conv, name=name, grid=(nt,),
        in_specs=[row(N), pl.BlockSpec((HALO, N), lambda i: (jnp.minimum((i + 1) * hb, nhb - 1), 0)),
                  _resident((3, N)), row(N)] + common_in,
        out_specs=common_out + [row(N), pl.BlockSpec((3, N), lambda i: (0, 0))],
        out_shape=common_shape + [jax.ShapeDtypeStruct((S, N), BF16), jax.ShapeDtypeStruct((3, N), F32)],
        scratch_shapes=[pltpu.VMEM((tm + HALO, N), F32), pltpu.VMEM((3, 8, N), F32)],
        sem=("arbitrary",), args=(dA, dA, conv_w, conv_in, w4, h_in, gain, dh_out), comm=comm)


def _rel_buckets_flat():
    q = np.arange(CHUNK)[:, None] + CHUNK
    k = np.arange(2 * CHUNK)[None, :]
    n = np.maximum(q - k, 0)
    max_exact = REL_BUCKETS // 2
    large = max_exact + (np.log(np.maximum(n, 1).astype(np.float32) / max_exact)
                         / math.log(REL_MAX_DIST / max_exact) * (REL_BUCKETS - max_exact)).astype(np.int32)
    large = np.minimum(large, REL_BUCKETS - 1)
    return np.where(n < max_exact, n, large).astype(np.int32).reshape(1, CHUNK * 2 * CHUNK)


def _split_bf16(x):
    hi = x.astype(BF16)
    return hi, (x - hi.astype(F32)).astype(BF16)


def _rel_bias_expand(rel_bias, bucket):
    B, H = rel_bias.shape
    n = bucket.shape[1]

    def body(rb_ref, bk_ref, o_ref):
        oh = (bk_ref[...] == lax.broadcasted_iota(jnp.int32, (B, n), 0)).astype(BF16)
        hi, lo = _split_bf16(rb_ref[...])
        o_ref[...] = _dot_tn(hi, oh) + _dot_tn(lo, oh)

    return pl.pallas_call(body, name="rel_bias_expand", out_shape=jax.ShapeDtypeStruct((H, n), F32),
                          compiler_params=_cp(None))(rel_bias, bucket)


def _rel_bias_reduce(dbias, bucket):
    H, n = dbias.shape
    B = REL_BUCKETS

    def body(db_ref, bk_ref, o_ref):
        oh = (bk_ref[...] == lax.broadcasted_iota(jnp.int32, (B, n), 0)).astype(BF16)
        hi, lo = _split_bf16(db_ref[...])
        o_ref[...] = _dot_nt(oh, hi) + _dot_nt(oh, lo)

    return pl.pallas_call(body, name="rel_bias_reduce", out_shape=jax.ShapeDtypeStruct((B, H), F32),
                          compiler_params=_cp(None))(dbias, bucket)


def _lo_mask(rows):
    return lax.broadcasted_iota(jnp.int32, (rows, LANES), 1) < HEAD_DIM


def _half_sums(y, lo, mxu=False):
    if not mxu:
        s_lo = jnp.sum(jnp.where(lo, y, 0.0), axis=-1, keepdims=True)
        s_hi = jnp.sum(jnp.where(lo, 0.0, y), axis=-1, keepdims=True)
        return jnp.where(lo, s_lo, s_hi)
    i = lax.broadcasted_iota(jnp.int32, (LANES, LANES), 0) < HEAD_DIM
    j = lax.broadcasted_iota(jnp.int32, (LANES, LANES), 1) < HEAD_DIM
    same = (i == j).astype(BF16)
    hi, lo_part = _split_bf16(y)
    return _dot(hi, same) + _dot(lo_part, same)


def _half_rms(x, gain, lo, mxu=False):
    r = lax.rsqrt(_half_sums(x * x, lo, mxu) * (1.0 / HEAD_DIM) + EPS)
    xhat = x * r
    return xhat * gain, xhat, r


def _half_rms_bwd(dy, xhat, r, gain, lo, mxu=False):
    gdy = dy * gain
    dx = r * (gdy - xhat * (_half_sums(gdy * xhat, lo, mxu) * (1.0 / HEAD_DIM)))
    return dx, jnp.sum(dy * xhat, axis=0, keepdims=True)


def _dup_half(pair, e, lo):
    sw = pltpu.roll(pair, HEAD_DIM, 1)
    return jnp.where(lo, pair, sw) if e == 0 else jnp.where(lo, sw, pair)


def _band_valid(n):
    qi = lax.broadcasted_iota(jnp.int32, (KV_GROUP * CHUNK, 2 * CHUNK), 0) & (CHUNK - 1)
    ki = lax.broadcasted_iota(jnp.int32, (KV_GROUP * CHUNK, 2 * CHUNK), 1)
    dist = qi + CHUNK - ki
    return (dist >= 0) & (dist < CHUNK) & ((n > 0) | (ki >= CHUNK))


def _stack_heads(a, b, lo):
    return jnp.concatenate([jnp.where(lo, a, 0.0), jnp.where(lo, 0.0, a), jnp.where(lo, b, 0.0), jnp.where(lo, 0.0, b)],
                           axis=0)


def _unstack_heads(x4, lo):
    return (jnp.where(lo, x4[0:CHUNK], x4[CHUNK:2 * CHUNK]),
            jnp.where(lo, x4[2 * CHUNK:3 * CHUNK], x4[3 * CHUNK:]))


def _sink_col(sink_ref, hk):
    row = lax.broadcasted_iota(jnp.int32, (KV_GROUP * CHUNK, 1), 0)
    col = jnp.full((KV_GROUP * CHUNK, 1), sink_ref[KV_GROUP * hk + KV_GROUP - 1], F32)
    for r in range(KV_GROUP - 2, -1, -1):
        col = jnp.where(row < (r + 1) * CHUNK, sink_ref[KV_GROUP * hk + r], col)
    return col


def _softmax_sink(s, valid, sink):
    s = jnp.where(valid, s, -jnp.inf)
    m = jnp.maximum(jnp.max(s, axis=-1, keepdims=True), sink)
    p = jnp.exp(s - m)
    esink = jnp.exp(sink - m)
    inv = 1.0 / (jnp.sum(p, axis=-1, keepdims=True) + esink)
    return p * inv, esink * inv


QW = N_HEADS * HEAD_DIM
KVW = N_KV_HEADS * HEAD_DIM


def _attn_fwd(qkv, qg2, kg2, sinks, bias, comm=None):
    S = qkv.shape[0]
    nb = S // CHUNK

    def body(cur_ref, prev_ref, qg_ref, kg_ref, sink_ref, bias_ref, o_ref, p_ref, ps_ref):
        n = pl.program_id(0)
        lo = _lo_mask(CHUNK)
        lo2 = _lo_mask(2 * CHUNK)
        valid = _band_valid(n)
        for j in range(N_KV_HEADS // 2):
            kc = slice(QW + j * LANES, QW + (j + 1) * LANES)
            vc = slice(QW + KVW + j * LANES, QW + KVW + (j + 1) * LANES)
            kpair = jnp.concatenate([prev_ref[:, j * LANES:(j + 1) * LANES], cur_ref[:, kc]], axis=0).astype(F32)
            vpair = jnp.concatenate([prev_ref[:, KVW + j * LANES:KVW + (j + 1) * LANES], cur_ref[:, vc]], axis=0).astype(F32)
            knpair, _, _ = _half_rms(kpair, kg_ref[...], lo2)
            for e in range(2):
                hk = 2 * j + e
                kdup = _dup_half(knpair, e, lo2).astype(BF16)
                vdup = _dup_half(vpair, e, lo2).astype(BF16)
                ca = slice(2 * hk * LANES, (2 * hk + 1) * LANES)
                cb = slice((2 * hk + 1) * LANES, (2 * hk + 2) * LANES)
                qna, _, _ = _half_rms(cur_ref[:, ca].astype(F32), qg_ref[...], lo)
                qnb, _, _ = _half_rms(cur_ref[:, cb].astype(F32), qg_ref[...], lo)
                qm4 = _stack_heads(qna, qnb, lo).astype(BF16)
                s = _dot_nt(qm4, kdup) * (HEAD_DIM ** -0.5) + bias_ref[hk]
                p, psink = _softmax_sink(s, valid, _sink_col(sink_ref, hk))
                pb = p.astype(BF16)
                p_ref[0, hk] = pb
                for r in range(KV_GROUP):
                    h = KV_GROUP * hk + r
                    ps_ref[:, h:h + 1] = psink[r * CHUNK:(r + 1) * CHUNK]
                oa, ob = _unstack_heads(_dot(pb, vdup), lo)
                o_ref[:, ca] = oa.astype(BF16)
                o_ref[:, cb] = ob.astype(BF16)

    pshape = (nb, N_KV_HEADS, KV_GROUP * CHUNK, 2 * CHUNK)
    return _call(
        body, name="attn_fwd", grid=(nb,),
        in_specs=[pl.BlockSpec((CHUNK, QW + 2 * KVW), lambda n: (n, 0)),
                  pl.BlockSpec((CHUNK, 2 * KVW), lambda n: (jnp.maximum(n - 1, 0), QW // (2 * KVW))),
                  _resident((1, LANES)), _resident((1, LANES)),
                  pl.BlockSpec(memory_space=pltpu.SMEM),
                  _resident(bias.shape)],
        out_specs=[pl.BlockSpec((CHUNK, QW), lambda n: (n, 0)),
                   pl.BlockSpec((1,) + pshape[1:], lambda n: (n, 0, 0, 0)),
                   pl.BlockSpec((CHUNK, LANES), lambda n: (n, 0))],
        out_shape=[jax.ShapeDtypeStruct((S, QW), BF16), jax.ShapeDtypeStruct(pshape, BF16),
                   jax.ShapeDtypeStruct((S, LANES), F32)],
        sem=("parallel",), args=(qkv, qkv, qg2, kg2, sinks, bias), comm=comm)


def _attn_bwd(qkv, do, probs, psinks, qg2, kg2, comm=None):
    S = qkv.shape[0]
    nb = S // CHUNK

    def body(cur_ref, prev_ref, do_ref, p_ref, ps_ref, qg_ref, kg_ref,
             dqkv_ref, dbias_ref, dqg_ref, dkg_ref, dsink_ref, carry, band, dsacc, gacc):
        i = pl.program_id(0)
        lo = _lo_mask(CHUNK)
        lo2 = _lo_mask(2 * CHUNK)
        lane = lax.broadcasted_iota(jnp.int32, (KV_GROUP * CHUNK, LANES), 1)

        @pl.when(i == 0)
        def _():
            dbias_ref[...] = jnp.zeros_like(dbias_ref)
            carry[...] = jnp.zeros_like(carry)
            dsacc[...] = jnp.zeros_like(dsacc)
            gacc[...] = jnp.zeros_like(gacc)

        qgain = qg_ref[...]
        kgain = kg_ref[...]
        for j in range(N_KV_HEADS // 2):
            kc = slice(QW + j * LANES, QW + (j + 1) * LANES)
            vc = slice(QW + KVW + j * LANES, QW + KVW + (j + 1) * LANES)
            kpair = jnp.concatenate([prev_ref[:, j * LANES:(j + 1) * LANES], cur_ref[:, kc]], axis=0).astype(F32)
            vpair = jnp.concatenate([prev_ref[:, KVW + j * LANES:KVW + (j + 1) * LANES], cur_ref[:, vc]], axis=0).astype(F32)
            knpair, khat, kr = _half_rms(kpair, kgain, lo2, mxu=True)
            dk_folds = []
            dv_folds = []
            for e in range(2):
                hk = 2 * j + e
                kdup = _dup_half(knpair, e, lo2).astype(BF16)
                vdup = _dup_half(vpair, e, lo2).astype(BF16)
                ca = slice(2 * hk * LANES, (2 * hk + 1) * LANES)
                cb = slice((2 * hk + 1) * LANES, (2 * hk + 2) * LANES)
                qna, qhata, qra = _half_rms(cur_ref[:, ca].astype(F32), qgain, lo, mxu=True)
                qnb, qhatb, qrb = _half_rms(cur_ref[:, cb].astype(F32), qgain, lo, mxu=True)
                qm4 = _stack_heads(qna, qnb, lo).astype(BF16)
                dom4 = _stack_heads(do_ref[:, ca].astype(F32), do_ref[:, cb].astype(F32), lo).astype(BF16)
                pb = p_ref[0, hk]
                p = pb.astype(F32)
                psink = jnp.concatenate([ps_ref[:, KV_GROUP * hk + r:KV_GROUP * hk + r + 1] for r in range(KV_GROUP)],
                                        axis=0)
                dp = _dot_nt(dom4, vdup)
                delta = jnp.sum(p * dp, axis=-1, keepdims=True)
                ds = p * (dp - delta)
                dbias_ref[hk] += ds
                dsacc[...] += jnp.where(lane == hk, -(psink * delta), 0.0)
                dsr = (ds * (HEAD_DIM ** -0.5)).astype(BF16)
                dqna, dqnb = _unstack_heads(_dot(dsr, kdup), lo)
                dkd = _dot_tn(dsr, qm4)
                dvd = _dot_tn(pb, dom4)
                dqa, dqga = _half_rms_bwd(dqna, qhata, qra, qgain, lo, mxu=True)
                dqb, dqgb = _half_rms_bwd(dqnb, qhatb, qrb, qgain, lo, mxu=True)
                gacc[0:1, :] += dqga + dqgb
                dqkv_ref[:, ca] = dqa.astype(BF16)
                dqkv_ref[:, cb] = dqb.astype(BF16)
                dk_folds.append(dkd + pltpu.roll(dkd, HEAD_DIM, 1))
                dv_folds.append(dvd + pltpu.roll(dvd, HEAD_DIM, 1))
            dkn = jnp.where(lo2, dk_folds[0], dk_folds[1])
            dk, dkg = _half_rms_bwd(dkn, khat, kr, kgain, lo2, mxu=True)
            gacc[1:2, :] += dkg
            band[:, j * LANES:(j + 1) * LANES] = dk
            band[:, KVW + j * LANES:KVW + (j + 1) * LANES] = jnp.where(lo2, dv_folds[0], dv_folds[1])
        dqkv_ref[:, QW:] = (band[CHUNK:, :] + carry[...]).astype(BF16)
        carry[...] = band[0:CHUNK, :]

        @pl.when(i == nb - 1)
        def _():
            g = gacc[...]
            g = g + pltpu.roll(g, HEAD_DIM, 1)
            dqg_ref[...] = g[0:1, :]
            dkg_ref[...] = g[1:2, :]
            for r in range(KV_GROUP):
                dsink_ref[r:r + 1, :] = jnp.sum(dsacc[r * CHUNK:(r + 1) * CHUNK, :], axis=0, keepdims=True)

    vec = pl.BlockSpec((1, LANES), lambda i: (0, 0))
    bshape = probs.shape[1:]
    return _call(
        body, name="attn_bwd", grid=(nb,),
        in_specs=[pl.BlockSpec((CHUNK, QW + 2 * KVW), lambda i: (nb - 1 - i, 0)),
                  pl.BlockSpec((CHUNK, 2 * KVW), lambda i: (jnp.maximum(nb - 2 - i, 0), QW // (2 * KVW))),
                  pl.BlockSpec((CHUNK, QW), lambda i: (nb - 1 - i, 0)),
                  pl.BlockSpec((1,) + bshape, lambda i: (nb - 1 - i, 0, 0, 0)),
                  pl.BlockSpec((CHUNK, LANES), lambda i: (nb - 1 - i, 0)),
                  _resident((1, LANES)), _resident((1, LANES))],
        out_specs=[pl.BlockSpec((CHUNK, QW + 2 * KVW), lambda i: (nb - 1 - i, 0)),
                   pl.BlockSpec(bshape, lambda i: (0, 0, 0)), vec, vec,
                   pl.BlockSpec((KV_GROUP, LANES), lambda i: (0, 0))],
        out_shape=[jax.ShapeDtypeStruct((S, QW + 2 * KVW), BF16),
                   jax.ShapeDtypeStruct(bshape, F32),
                   jax.ShapeDtypeStruct((1, LANES), F32), jax.ShapeDtypeStruct((1, LANES), F32),
                   jax.ShapeDtypeStruct((KV_GROUP, LANES), F32)],
        scratch_shapes=[pltpu.VMEM((CHUNK, 2 * KVW), F32), pltpu.VMEM((2 * CHUNK, 2 * KVW), F32),
                        pltpu.VMEM((KV_GROUP * CHUNK, LANES), F32), pltpu.VMEM((8, LANES), F32)],
        sem=("arbitrary",), args=(qkv, qkv, do, probs, psinks, qg2, kg2), comm=comm)


def _row_tile(rows, cols, n_arrays):
    budget = VMEM_LIMIT_V7X // 4 // (2 * n_arrays * 4 * cols)
    best = 8
    for n in range(1, rows // 8 + 1):
        if rows % n == 0 and (rows // n) % 8 == 0 and rows // n <= budget:
            best = rows // n
            break
    return best


def _adamw(g, w, m, v, name, comm=None):
    R, C = g.shape
    tr = _row_tile(R, C, 8)

    def body(g_ref, w_ref, m_ref, v_ref, d_ref, mo_ref, vo_ref, go_ref):
        gg = g_ref[...]
        go_ref[...] = gg
        mn = ADAM_B1 * m_ref[...] + (1.0 - ADAM_B1) * gg
        vn = ADAM_B2 * v_ref[...] + (1.0 - ADAM_B2) * jnp.square(gg)
        m_hat = mn / (1.0 - ADAM_B1 ** ADAM_STEP)
        v_hat = vn / (1.0 - ADAM_B2 ** ADAM_STEP)
        d_ref[...] = -ADAM_LR * (m_hat / (jnp.sqrt(v_hat) + ADAM_EPS) + ADAM_WD * w_ref[...])
        mo_ref[...] = mn
        vo_ref[...] = vn

    spec = pl.BlockSpec((tr, C), lambda i: (i, 0))
    return _call(
        body, name=name, grid=(R // tr,), in_specs=[spec] * 4, out_specs=[spec] * 4,
        out_shape=[jax.ShapeDtypeStruct((R, C), F32)] * 4, sem=("parallel",), args=(g, w, m, v), comm=comm)


def _place_shard(shards, layer, where, dtype, name):
    _, R, C = shards.shape
    tr = _row_tile(R, C, 2) if R % 8 == 0 else R

    def body(s_ref, x_ref, o_ref):
        o_ref[...] = x_ref[...].astype(dtype)

    return pl.pallas_call(
        body, name=name,
        grid_spec=pltpu.PrefetchScalarGridSpec(
            num_scalar_prefetch=1, grid=(R // tr,),
            in_specs=[pl.BlockSpec((1, tr, C), lambda i, s_ref: (layer, i, 0))],
            out_specs=pl.BlockSpec((1, tr, C), lambda i, s_ref: (s_ref[1], i, 0))),
        out_shape=jax.ShapeDtypeStruct((4, R, C), dtype),
        compiler_params=_cp(("parallel",)),
    )(where, shards)


def _pair_add(g4, rsib, where, name):
    J, R, C = g4.shape
    Rh = R // 2
    tr = _row_tile(Rh, C, 4)
    g5 = g4.reshape(J, 2, Rh, C)

    def body(s_ref, g_ref, r_ref, p_ref, q_ref):
        val = (g_ref[...].astype(F32)[0] + r_ref[...].astype(F32)).astype(BF16)
        p_ref[...] = val

        @pl.when(pl.program_id(1) == s_ref[1])
        def _():
            q_ref[...] = val

    return pl.pallas_call(
        body, name=name,
        grid_spec=pltpu.PrefetchScalarGridSpec(
            num_scalar_prefetch=1, grid=(Rh // tr, J),
            in_specs=[pl.BlockSpec((1, 1, tr, C), lambda i, j, s_ref: (j, s_ref[0], i, 0)),
                      pl.BlockSpec((1, tr, C), lambda i, j, s_ref: (j, i, 0))],
            out_specs=[pl.BlockSpec((1, tr, C), lambda i, j, s_ref: (j, i, 0)),
                       pl.BlockSpec((1, tr, C), lambda i, j, s_ref: (s_ref[1], i, 0))]),
        out_shape=[jax.ShapeDtypeStruct((J, Rh, C), BF16)] * 2,
        compiler_params=_cp(("parallel", "arbitrary")),
    )(where, g5, rsib)


def _sum_chips(q, where, dest, layer, out_shape, name):
    J, Rh, C = q.shape
    tr = _row_tile(Rh, C, 3)
    nb = Rh // tr

    def body(s_ref, q_ref, *rest):
        qq = q_ref[...].astype(F32)
        rest[-1][0] = ((qq[0] + qq[1]) + qq[2]) + qq[3]

    have = dest is not None
    return pl.pallas_call(
        body, name=name,
        grid_spec=pltpu.PrefetchScalarGridSpec(
            num_scalar_prefetch=1, grid=(nb,),
            in_specs=[pl.BlockSpec((J, tr, C), lambda i, s_ref: (0, i, 0))] + ([ANY] if have else []),
            out_specs=pl.BlockSpec((1, tr, C), lambda i, s_ref: (layer, s_ref[0] * nb + i, 0))),
        out_shape=jax.ShapeDtypeStruct(out_shape, F32),
        input_output_aliases={2: 0} if have else {},
        compiler_params=_cp(("parallel",)),
    )(*((where, q, dest) if have else (where, q)))


MESH = pl.DeviceIdType.MESH
ANY = pl.BlockSpec(memory_space=pl.ANY)


def _place():
    x, y, c = lax.axis_index("x"), lax.axis_index("y"), lax.axis_index("c")
    others = [(1 - x, y), (x, 1 - y), (1 - x, 1 - y)]
    return x, y, c, 2 * x + y, others, [2 * ox + oy for ox, oy in others]


def _gather_comm(items):
    n = len(items)
    placed = [it[0] for it in items]
    split = [it[3] for it in items]

    def rows(t, ref, half):
        _, lo, hi, _ = items[t]
        if not split[t]:
            return ref if (lo, hi) == (0, placed[t].shape[1]) else ref.at[pl.ds(lo, hi - lo), :]
        rh = (hi - lo) // 2
        return ref.at[pl.ds(lo + half * rh, rh), :]

    def sends(outs, sems):
        send, recv = sems[0], sems[1]
        x, y, c, me, others, okey = _place()
        cps = []
        for t in range(n):
            mine = rows(t, outs[t].at[me], c)
            for j, (ox, oy) in enumerate(others):
                cps.append(pltpu.make_async_remote_copy(
                    src_ref=mine, dst_ref=mine,
                    send_sem=send.at[t, j], recv_sem=recv.at[t, j], device_id=(ox, oy, c), device_id_type=MESH))
        return cps

    def start(srcs, outs, news, sems):
        for cp in sends(outs, sems):
            cp.start()

    def forwards(outs, sems):
        fsend, frecv = sems[2], sems[3]
        x, y, c, me, others, okey = _place()
        cps = []
        for t in range(n):
            if split[t]:
                for j in range(3):
                    landed = rows(t, outs[t].at[okey[j]], c)
                    cps.append(pltpu.make_async_remote_copy(
                        src_ref=landed, dst_ref=landed, send_sem=fsend.at[t, j], recv_sem=frecv.at[t, j],
                        device_id=(x, y, 1 - c), device_id_type=MESH))
        return cps

    def middle(srcs, outs, news, sems):
        send, recv = sems[0], sems[1]
        x, y, c, me, others, okey = _place()
        for t in range(n):
            for j in range(3):
                landed = rows(t, outs[t].at[okey[j]], c)
                pltpu.make_async_remote_copy(
                    src_ref=landed, dst_ref=landed, send_sem=send.at[t, j], recv_sem=recv.at[t, j],
                    device_id=(x, y, 1 - c), device_id_type=MESH).wait_recv()
        for cp in forwards(outs, sems):
            cp.start()

    def finish(srcs, outs, news, sems):
        fsend, frecv = sems[2], sems[3]
        x, y, c, me, others, okey = _place()
        for t in range(n):
            if split[t]:
                for j in range(3):
                    theirs = rows(t, outs[t].at[okey[j]], 1 - c)
                    pltpu.make_async_remote_copy(
                        src_ref=theirs, dst_ref=theirs, send_sem=fsend.at[t, j], recv_sem=frecv.at[t, j],
                        device_id=(x, y, 1 - c), device_id_type=MESH).wait_recv()
        for cp in sends(outs, sems) + forwards(outs, sems):
            cp.wait_send()

    return _Comm([], placed, [], [pltpu.SemaphoreType.DMA((n, 3))] * 4, start, finish, middle)


def _pair_exchange_comm(gs):
    n = len(gs)

    def copies(ins, outs, sems):
        send, recv = sems
        x, y, c, _, _, _ = _place()
        cps = []
        for t in range(n):
            rh = gs[t].shape[1] // 2
            cps.append(pltpu.make_async_remote_copy(
                src_ref=ins[t].at[:, pl.ds((1 - c) * rh, rh), :], dst_ref=outs[t],
                send_sem=send.at[t], recv_sem=recv.at[t], device_id=(x, y, 1 - c), device_id_type=MESH))
        return cps

    def start(ins, zones, outs, sems):
        for cp in copies(ins, outs, sems):
            cp.start()

    def finish(ins, zones, outs, sems):
        for cp in copies(ins, outs, sems):
            cp.wait()

    news = [jax.ShapeDtypeStruct((4, g.shape[1] // 2, g.shape[2]), g.dtype) for g in gs]
    return _Comm(gs, [], news, [pltpu.SemaphoreType.DMA((n,))] * 2, start, finish)


def _chip_scatter_comm(ps, qs):
    n = len(ps)

    def sends(ins, outs, sems):
        send, recv = sems
        x, y, c, me, others, okey = _place()
        return [pltpu.make_async_remote_copy(
            src_ref=ins[t].at[okey[j]], dst_ref=outs[t].at[me],
            send_sem=send.at[t, j], recv_sem=recv.at[t, j], device_id=(ox, oy, c), device_id_type=MESH)
            for t in range(n) for j, (ox, oy) in enumerate(others)]

    def start(ins, outs, news, sems):
        for cp in sends(ins, outs, sems):
            cp.start()

    def finish(ins, outs, news, sems):
        send, recv = sems
        x, y, c, me, others, okey = _place()
        for t in range(n):
            for j in range(3):
                slot = outs[t].at[okey[j]]
                pltpu.make_async_remote_copy(
                    src_ref=slot, dst_ref=slot, send_sem=send.at[t, j], recv_sem=recv.at[t, j],
                    device_id=(x, y, c), device_id_type=MESH).wait_recv()
        for cp in sends(ins, outs, sems):
            cp.wait_send()

    return _Comm(ps, qs, [], [pltpu.SemaphoreType.DMA((n, 3))] * 2, start, finish)


def _half_exchange_comm(arrs, layers=None):
    n = len(arrs)
    items = [(t, layer) for t in range(n) for layer in (range(arrs[t].shape[0]) if layers is None else layers[t])]

    def sends(outs, sems):
        send, recv = sems
        x, y, c, _, _, _ = _place()
        cps = []
        for k, (t, layer) in enumerate(items):
            rh = arrs[t].shape[1] // 2
            mine = outs[t].at[layer, pl.ds(c * rh, rh), :]
            cps.append(pltpu.make_async_remote_copy(
                src_ref=mine, dst_ref=mine, send_sem=send.at[k], recv_sem=recv.at[k],
                device_id=(x, y, 1 - c), device_id_type=MESH))
        return cps

    def start(srcs, outs, news, sems):
        for cp in sends(outs, sems):
            cp.start()

    def finish(srcs, outs, news, sems):
        send, recv = sems
        x, y, c, _, _, _ = _place()
        for k, (t, layer) in enumerate(items):
            rh = arrs[t].shape[1] // 2
            theirs = outs[t].at[layer, pl.ds((1 - c) * rh, rh), :]
            pltpu.make_async_remote_copy(
                src_ref=theirs, dst_ref=theirs, send_sem=send.at[k], recv_sem=recv.at[k],
                device_id=(x, y, 1 - c), device_id_type=MESH).wait_recv()
        for cp in sends(outs, sems):
            cp.wait_send()

    return _Comm([], arrs, [], [pltpu.SemaphoreType.DMA((len(items),))] * 2, start, finish)


SMALL_COLS = 1024
SMALL_PIECE_ROWS = 48


def _allreduce_small(buf, comm=None):
    pr = SMALL_PIECE_ROWS
    flips = [(d >> 2 & 1, d >> 1 & 1, d & 1) for d in range(1, 8)]

    def body(x_ref, o_ref, rbuf, send1, recv1, send2, recv2):
        x, y, c = lax.axis_index("x"), lax.axis_index("y"), lax.axis_index("c")
        me = 4 * x + 2 * y + c
        peers = [(x ^ fx, y ^ fy, c ^ fc) for fx, fy, fc in flips]
        pid = [4 * px + 2 * py + pc for px, py, pc in peers]

        def piece(ref, p):
            return ref.at[pl.ds(pl.multiple_of(p * pr, 8), pr), :]

        cps = []
        for d in range(7):
            cp = pltpu.make_async_remote_copy(
                src_ref=piece(x_ref, pid[d]), dst_ref=rbuf.at[d + 1],
                send_sem=send1.at[d], recv_sem=recv1.at[d], device_id=peers[d], device_id_type=MESH)
            cp.start()
            cps.append(cp)
        acc = piece(x_ref, me)[...]
        for d in range(7):
            cps[d].wait_recv()
            acc = acc + rbuf[d + 1]
        piece(o_ref, me)[...] = acc
        out = []
        for d in range(7):
            cp = pltpu.make_async_remote_copy(
                src_ref=piece(o_ref, me), dst_ref=piece(o_ref, me),
                send_sem=send2.at[d], recv_sem=recv2.at[d], device_id=peers[d], device_id_type=MESH)
            cp.start()
            out.append(cp)
        for d in range(7):
            pltpu.make_async_remote_copy(
                src_ref=piece(o_ref, pid[d]), dst_ref=piece(o_ref, pid[d]),
                send_sem=send2.at[d], recv_sem=recv2.at[d], device_id=peers[d], device_id_type=MESH).wait_recv()
        for cp in cps + out:
            cp.wait_send()

    vm = pl.BlockSpec(memory_space=pltpu.VMEM)
    return _call(
        body, name="small_allreduce", grid=(), in_specs=[vm], out_specs=vm,
        out_shape=jax.ShapeDtypeStruct(buf.shape, F32),
        scratch_shapes=[pltpu.VMEM((8, pr, SMALL_COLS), F32)] + [pltpu.SemaphoreType.DMA((7,))] * 4,
        args=(buf,), comm=comm)


def _rows_of(shape):
    return -(-math.prod(shape) // (8 * SMALL_COLS)) * 8


def _pack(arrays, rows):
    parts = []
    for a in arrays:
        r = _rows_of(a.shape)
        parts.append(jnp.pad(a.reshape(-1), (0, r * SMALL_COLS - a.size)).reshape(r, SMALL_COLS))
    used = sum(p.shape[0] for p in parts)
    if rows > used:
        parts.append(jnp.zeros((rows - used, SMALL_COLS), F32))
    return jnp.concatenate(parts, axis=0)


def _unpack(buf, shapes):
    out, off = [], 0
    for s in shapes:
        r = _rows_of(s)
        out.append(buf[off:off + r].reshape(-1)[:math.prod(s)].reshape(s))
        off += r
    return out


BIG = ["sgu_w_in", "sgu_w_out", "attn_w_qkv", "attn_w_o", "ffn_w_up", "ffn_w_down"]
SMALL = ["mix_norm", "ffn_norm", "sgu_v_gain", "sgu_w_s", "sgu_b_s", "attn_q_gain", "attn_k_gain", "attn_sinks",
         "rel_bias", "ffn_conv_b"]
ORDER = ["mix_norm", "ffn_norm", "sgu_w_in", "sgu_v_gain", "sgu_w_s", "sgu_b_s", "sgu_w_out", "attn_w_qkv",
         "attn_q_gain", "attn_k_gain", "attn_sinks", "attn_w_o", "rel_bias", "ffn_w_up", "ffn_conv_w", "ffn_conv_b",
         "ffn_w_down"]


def kernel(x, mix_norm, ffn_norm, sgu_w_in, sgu_v_gain, sgu_w_s, sgu_b_s, sgu_w_out, attn_w_qkv, attn_q_gain, attn_k_gain, attn_sinks, attn_w_o, rel_bias, ffn_w_up, ffn_conv_w, ffn_conv_b, ffn_w_down, loss_target, m_mix_norm, m_ffn_norm, m_sgu_w_in, m_sgu_v_gain, m_sgu_w_s, m_sgu_b_s, m_sgu_w_out, m_attn_w_qkv, m_attn_q_gain, m_attn_k_gain, m_attn_sinks, m_attn_w_o, m_rel_bias, m_ffn_w_up, m_ffn_conv_w, m_ffn_conv_b, m_ffn_w_down, v_mix_norm, v_ffn_norm, v_sgu_w_in, v_sgu_v_gain, v_sgu_w_s, v_sgu_b_s, v_sgu_w_out, v_attn_w_qkv, v_attn_q_gain, v_attn_k_gain, v_attn_sinks, v_attn_w_o, v_rel_bias, v_ffn_w_up, v_ffn_conv_w, v_ffn_conv_b, v_ffn_w_down):
    w = dict(mix_norm=mix_norm, ffn_norm=ffn_norm, sgu_w_in=sgu_w_in, sgu_v_gain=sgu_v_gain, sgu_w_s=sgu_w_s,
             sgu_b_s=sgu_b_s, sgu_w_out=sgu_w_out, attn_w_qkv=attn_w_qkv, attn_q_gain=attn_q_gain,
             attn_k_gain=attn_k_gain, attn_sinks=attn_sinks, attn_w_o=attn_w_o, rel_bias=rel_bias, ffn_w_up=ffn_w_up,
             ffn_conv_w=ffn_conv_w, ffn_conv_b=ffn_conv_b, ffn_w_down=ffn_w_down)
    mom = dict(mix_norm=m_mix_norm, ffn_norm=m_ffn_norm, sgu_w_in=m_sgu_w_in, sgu_v_gain=m_sgu_v_gain,
               sgu_w_s=m_sgu_w_s, sgu_b_s=m_sgu_b_s, sgu_w_out=m_sgu_w_out, attn_w_qkv=m_attn_w_qkv,
               attn_q_gain=m_attn_q_gain, attn_k_gain=m_attn_k_gain, attn_sinks=m_attn_sinks, attn_w_o=m_attn_w_o,
               rel_bias=m_rel_bias, ffn_w_up=m_ffn_w_up, ffn_conv_w=m_ffn_conv_w, ffn_conv_b=m_ffn_conv_b,
               ffn_w_down=m_ffn_w_down)
    var = dict(mix_norm=v_mix_norm, ffn_norm=v_ffn_norm, sgu_w_in=v_sgu_w_in, sgu_v_gain=v_sgu_v_gain,
               sgu_w_s=v_sgu_w_s, sgu_b_s=v_sgu_b_s, sgu_w_out=v_sgu_w_out, attn_w_qkv=v_attn_w_qkv,
               attn_q_gain=v_attn_q_gain, attn_k_gain=v_attn_k_gain, attn_sinks=v_attn_sinks, attn_w_o=v_attn_w_o,
               rel_bias=v_rel_bias, ffn_w_up=v_ffn_w_up, ffn_conv_w=v_ffn_conv_w, ffn_conv_b=v_ffn_conv_b,
               ffn_w_down=v_ffn_w_down)
    chip = 2 * lax.axis_index("x") + lax.axis_index("y")
    core = lax.axis_index("c")

    where = jnp.stack([core, chip]).astype(jnp.int32)
    names = ["sgu_w_in", "sgu_w_out", "attn_w_qkv", "attn_w_o", "ffn_w_up0", "ffn_w_up1", "ffn_w_down0", "ffn_w_down1"]
    shards = [(sgu_w_in, 0), (sgu_w_out, 0), (attn_w_qkv, 0), (attn_w_o, 0), (ffn_w_up, 0), (ffn_w_up, 1),
              (ffn_w_down, 0), (ffn_w_down, 1)]
    T = {nm: _place_shard(s, l, where, BF16, "place_" + nm) for (s, l), nm in zip(shards, names)}
    for l in range(2):
        T["conv_w%d" % l] = _place_shard(ffn_conv_w, l, where, F32, "place_conv_w%d" % l)

    def gather(keys):
        items = []
        for k in keys:
            k, lo, hi = (k, 0, None) if isinstance(k, str) else k
            items.append((T[k], lo, T[k].shape[1] if hi is None else hi, not k.startswith("conv")))
        return _gather_comm(items)

    def gathered(keys, res):
        for k, a in zip(keys, res[0]):
            T[k if isinstance(k, str) else k[0]] = a

    D = x.shape[2]
    first = ["sgu_w_in", "conv_w0", "conv_w1"]
    gathered(first, _run_comm(gather(first), "gather_first"))
    unshard_cols = lambda a: jnp.transpose(a, (1, 0, 2)).reshape(a.shape[1], -1)
    cw = [unshard_cols(T["conv_w0"]), unshard_cols(T["conv_w1"])]
    cb = ffn_conv_b
    flat = lambda k: T[k].reshape(-1, D)
    x2, target = x[0], loss_target[0]
    bucket = jnp.asarray(_rel_buckets_flat())
    wtril, wtrilT = _sgu_prep(sgu_w_s[0])
    bT = sgu_b_s[0].T
    bias = _rel_bias_expand(rel_bias, bucket).reshape(N_KV_HEADS, KV_GROUP * CHUNK, 2 * CHUNK)
    qg2 = jnp.tile(attn_q_gain, (1, 2))
    kg2 = jnp.tile(attn_k_gain, (1, 2))
    sinks = attn_sinks.reshape(N_HEADS)
    mix0, mix1 = mix_norm[0:1], mix_norm[1:2]
    fn0, fn1 = ffn_norm[0:1], ffn_norm[1:2]

    ks = ["sgu_w_out", ("ffn_w_up0", 0, 384)]
    (hn0, z, cdf), r = _norm_matmul(x2, mix0, T["sgu_w_in"], "sgu_in", comm=gather(ks), with_cdf=True)
    gathered(ks, r)
    ks = [("ffn_w_up0", 384, None)]
    (yp, h1), r = _sgu_fwd(z, cdf, x2, sgu_v_gain, wtril, bT, flat("sgu_w_out"), comm=gather(ks))
    gathered(ks, r)
    ks = ["ffn_w_down0", "attn_w_qkv", "attn_w_o"]
    (hf0, a0, cp0), r = _norm_matmul(h1, fn0, T["ffn_w_up0"], "ffn0_up", comm=gather(ks), conv=(cw[0], cb[0:1]))
    gathered(ks, r)
    ks = [("ffn_w_up1", 0, 384)]
    (f0, h2), r = _ffn_fwd(cp0, h1, flat("ffn_w_down0"), "ffn0_fwd", comm=gather(ks))
    gathered(ks, r)
    (hn1, qkv), _ = _norm_matmul(h2, mix1, T["attn_w_qkv"], "attn_qkv")
    ks = [("ffn_w_up1", 384, None)]
    (o, probs, psinks), r = _attn_fwd(qkv, qg2, kg2, sinks, bias, comm=gather(ks))
    gathered(ks, r)
    h3, _ = _matmul_res(o, flat("attn_w_o"), h2, "attn_out")
    ks = ["ffn_w_down1"]
    (hf1, a1, cp1), r = _norm_matmul(h3, fn1, T["ffn_w_up1"], "ffn1_up", comm=gather(ks), conv=(cw[1], cb[1:2]))
    gathered(ks, r)
    (f1, dh4, loss_local), _ = _ffn_fwd(cp1, h3, flat("ffn_w_down1"), "ffn1_fwd", target=target)

    G, RS, PQ, QD, halves, grads = {}, {}, {}, {}, {}, {}
    dest_of = {"sgu_w_in": ("sgu_w_in", 0), "sgu_w_out": ("sgu_w_out", 0), "attn_w_qkv": ("attn_w_qkv", 0),
               "attn_w_o": ("attn_w_o", 0), "ffn_w_up0": ("ffn_w_up", 0), "ffn_w_up1": ("ffn_w_up", 1),
               "ffn_w_down0": ("ffn_w_down", 0), "ffn_w_down1": ("ffn_w_down", 1)}

    def px(keys):
        return _pair_exchange_comm([G[k] for k in keys])

    def px_done(keys, res):
        for k, a in zip(keys, res[1]):
            RS[k] = a
            PQ[k] = _pair_add(G[k], a, where, "pair_add_" + k)

    def sc(keys):
        return _chip_scatter_comm([PQ[k][0] for k in keys], [PQ[k][1] for k in keys])

    def sc_done(keys, res):
        for k, a in zip(keys, res[0]):
            wk, layer = dest_of[k]
            halves[wk] = _sum_chips(a, where, halves.get(wk), layer, w[wk].shape, "sum_chips_" + k)

    nup = ffn_w_up.shape[2]
    ndown = ffn_w_down.shape[1]
    rows4 = lambda a: a.reshape(4, ndown, D)
    (dc1, dcb1), _ = _ffn_bwd_dc(dh4, cp1, flat("ffn_w_down1"), "ffn1_bwd_dc")
    gw, _ = _matmul_tn(f1, dh4, "ffn1_dw_down", ka=2 * ndown, nb=D, out_dtype=BF16)
    G["ffn_w_down1"] = rows4(gw)
    (dh3, dfn1, da1, dcw1), _ = _bwd_norm(dc1, T["ffn_w_up1"], h3, fn1, dh4, "ffn1_bwd_in", conv_w=cw[1], conv_in=a1)
    G["ffn_w_up1"], _ = _matmul_tn(hf1, da1, "ffn1_dw_up", ka=D, nb=nup, out_dtype=BF16)
    ks = ["ffn_w_down1", "ffn_w_up1"]
    G["attn_w_o"], r = _matmul_tn(o, dh3, "attn_dw_o", ka=QW // 4, nb=D, out_dtype=BF16, comm=px(ks))
    px_done(ks, r)
    do, _ = _matmul_nt(dh3, flat("attn_w_o"), "attn_bwd_out")
    (dqkv, dbias, dqg, dkg, dsink), r = _attn_bwd(qkv, do, probs, psinks, qg2, kg2, comm=sc(ks))
    sc_done(ks, r)
    G["attn_w_qkv"], _ = _matmul_tn(hn1, dqkv, "attn_dw_qkv", ka=D, nb=dqkv.shape[1] // 4, out_dtype=BF16)
    ks = ["attn_w_o", "attn_w_qkv"]
    (dh2, dmix1), r = _bwd_norm(dqkv, T["attn_w_qkv"], h2, mix1, dh3, "attn_bwd_in", comm=px(ks))
    px_done(ks, r)
    (dc0, dcb0), r = _ffn_bwd_dc(dh2, cp0, flat("ffn_w_down0"), "ffn0_bwd_dc", comm=sc(ks))
    sc_done(ks, r)
    gw, _ = _matmul_tn(f0, dh2, "ffn0_dw_down", ka=2 * ndown, nb=D, out_dtype=BF16)
    G["ffn_w_down0"] = rows4(gw)
    (dh1, dfn0, da0, dcw0), r = _bwd_norm(dc0, T["ffn_w_up0"], h1, fn0, dh2, "ffn0_bwd_in", conv_w=cw[0], conv_in=a0,
                                          comm=px(["ffn_w_down0"]))
    px_done(["ffn_w_down0"], r)
    G["ffn_w_up0"], r = _matmul_tn(hf0, da0, "ffn0_dw_up", ka=D, nb=nup, out_dtype=BF16, comm=sc(["ffn_w_down0"]))
    sc_done(["ffn_w_down0"], r)
    ks = ["ffn_w_up0"]
    G["sgu_w_out"], r = _matmul_tn(yp, dh1, "sgu_dw_out", ka=yp.shape[1] // 4, nb=D, out_dtype=BF16, comm=px(ks))
    px_done(ks, r)
    both = [sc(ks), px(["sgu_w_out"])]
    (dz, dws, dbT, dvg), r = _sgu_bwd(dh1, z, cdf, sgu_v_gain, wtril, wtrilT, bT, flat("sgu_w_out"),
                                      comm=_join(both))
    r = _split(both, r)
    sc_done(ks, r[0])
    px_done(["sgu_w_out"], r[1])
    done = ["attn_w_qkv", "attn_w_o", "ffn_w_up", "ffn_w_down"]
    both = [sc(["sgu_w_out"]), _half_exchange_comm([halves[k] for k in done])]
    G["sgu_w_in"], r = _matmul_tn(hn0, dz, "sgu_dw_in", ka=D, nb=dz.shape[1] // 4, out_dtype=BF16, comm=_join(both))
    r = _split(both, r)
    sc_done(["sgu_w_out"], r[0])
    for k, a in zip(done, r[1][0]):
        grads[k] = a
    px_done(["sgu_w_in"], _run_comm(px(["sgu_w_in"]), "grad_last_pair_exchange"))
    (grad_x, dmix0), r = _bwd_norm(dz, T["sgu_w_in"], x2, mix0, dh1, "sgu_bwd_in", comm=sc(["sgu_w_in"]))
    sc_done(["sgu_w_in"], r)

    g = dict(mix_norm=jnp.concatenate([dmix0, dmix1], axis=0), ffn_norm=jnp.concatenate([dfn0, dfn1], axis=0),
             sgu_v_gain=dvg, sgu_w_s=dws, sgu_b_s=dbT.T, attn_q_gain=dqg[:, :HEAD_DIM], attn_k_gain=dkg[:, :HEAD_DIM],
             attn_sinks=dsink[:, :N_KV_HEADS].T.reshape(1, N_HEADS),
             rel_bias=_rel_bias_reduce(dbias.reshape(N_HEADS, CHUNK * 2 * CHUNK), bucket),
             ffn_conv_b=jnp.concatenate([dcb0, dcb1], axis=0))
    small_list = [g[k].reshape(w[k].shape) for k in SMALL] + [jnp.stack([dcw0, dcw1]), loss_local]
    small_shapes = [a.shape for a in small_list]
    done = ["sgu_w_in", "sgu_w_out"]
    red, r = _allreduce_small(_pack(small_list, 8 * SMALL_PIECE_ROWS),
                              comm=_half_exchange_comm([halves[k] for k in done]))
    for k, a in zip(done, r[0]):
        grads[k] = a
    red = _unpack(red, small_shapes)
    for k, a in zip(SMALL, red):
        grads[k] = a
    grads["ffn_conv_w"] = lax.dynamic_slice_in_dim(red[-2], chip * ffn_conv_w.shape[2], ffn_conv_w.shape[2], axis=2)
    loss = red[-1][0, 0]

    delta, new_m, new_v = {}, {}, {}
    two = lambda a: a.reshape(-1, a.shape[-1])
    for k in BIG:
        (d2, m2, v2, g2), _ = _adamw(two(grads[k]), two(w[k]), two(mom[k]), two(var[k]), "adamw_" + k)
        delta[k], new_m[k], new_v[k], grads[k] = (a.reshape(w[k].shape) for a in (d2, m2, v2, g2))
    sm = SMALL + ["ffn_conv_w"]
    sm_shapes = [w[k].shape for k in sm]
    rows = sum(_rows_of(s) for s in sm_shapes)
    (d2, m2, v2, _), _ = _adamw(_pack([grads[k] for k in sm], rows), _pack([w[k] for k in sm], rows),
                                _pack([mom[k] for k in sm], rows), _pack([var[k] for k in sm], rows), "adamw_small")
    for dst, buf in ((delta, d2), (new_m, m2), (new_v, v2)):
        for k, a in zip(sm, _unpack(buf, sm_shapes)):
            dst[k] = a

    return (loss, grad_x[None], *[grads[k] for k in ORDER], *[delta[k] for k in ORDER],
            *[new_m[k] for k in ORDER], *[new_v[k] for k in ORDER])
```
